```python
import jax, jax.numpy as jnp
from jax import lax
import numpy as np

D_MODEL = 1024
BATCH = 8
SEQ = 8192
DEPTH = 1

SSD_EXPAND = 2
SSD_D_INNER = SSD_EXPAND * D_MODEL
SSD_HEAD_DIM = 64
SSD_N_HEADS = SSD_D_INNER // SSD_HEAD_DIM
SSD_N_GROUPS = 4
SSD_HEADS_PER_GROUP = SSD_N_HEADS // SSD_N_GROUPS
SSD_D_STATE = 128
SSD_CONV_WIDTH = 4
SSD_CHUNK = 128
SSD_CONV_DIM = SSD_D_INNER + 2 * SSD_N_GROUPS * SSD_D_STATE

ATTN_HEAD_DIM = 64
ATTN_N_HEADS = D_MODEL // ATTN_HEAD_DIM
ATTN_N_KV_HEADS = 4
ATTN_REP = ATTN_N_HEADS // ATTN_N_KV_HEADS
ATTN_WINDOW = 128
ATTN_WIDTH = ATTN_N_HEADS * ATTN_HEAD_DIM
KV_WIDTH = ATTN_N_KV_HEADS * ATTN_HEAD_DIM
ROPE_THETA = 10000.0

FFN_D_FF = 2816
FFN_CONV_WIDTH = 3

NORM_EPS = 1e-6

IN_SIZES = (SSD_D_INNER, SSD_CONV_DIM, SSD_N_HEADS, ATTN_WIDTH, KV_WIDTH, KV_WIDTH, D_MODEL, D_MODEL)
IN_PROJ_DIM = sum(IN_SIZES)

kernel_name = "hybrid_ssd_swa_sink_gated_convffn"


def _split(t, sizes):
    idx = np.cumsum(np.array(sizes))[:-1].tolist()
    return jnp.split(t, idx, axis=-1)


def rms_norm(x, w):
    xf = x.astype(jnp.float32)
    y = xf * lax.rsqrt(jnp.mean(xf * xf, axis=-1, keepdims=True) + NORM_EPS)
    return (y * w.astype(jnp.float32)).astype(x.dtype)


def gated_group_rms_norm(y, z, w):
    b, s, d = y.shape
    g = (y.astype(jnp.float32) * jax.nn.silu(z.astype(jnp.float32))).reshape(b, s, SSD_N_GROUPS, d // SSD_N_GROUPS)
    g = g * lax.rsqrt(jnp.mean(g * g, axis=-1, keepdims=True) + NORM_EPS)
    return (g.reshape(b, s, d) * w.astype(jnp.float32)).astype(y.dtype)


def causal_dwconv(x, w, bias):
    k = w.shape[0]
    y = lax.conv_general_dilated(x, w[:, None, :].astype(x.dtype), window_strides=(1,), padding=((k - 1, 0),),
                                 dimension_numbers=('NWC', 'WIO', 'NWC'), feature_group_count=x.shape[-1])
    return y + bias.astype(x.dtype)


def rope_tables(positions):
    half = ATTN_HEAD_DIM // 2
    inv_freq = ROPE_THETA ** (-jnp.arange(half, dtype=jnp.float32) * 2.0 / ATTN_HEAD_DIM)
    ang = positions.astype(jnp.float32)[..., None] * inv_freq
    return jnp.cos(ang), jnp.sin(ang)


def apply_rope(t, cos, sin):
    b, s = t.shape[:2]
    half = t.shape[-1] // 2
    shp = (b, s) + (1,) * (t.ndim - 3) + (half,)
    c, sn = cos.reshape(shp), sin.reshape(shp)
    tf = t.astype(jnp.float32)
    t1, t2 = tf[..., :half], tf[..., half:]
    return jnp.concatenate([t1 * c - t2 * sn, t2 * c + t1 * sn], axis=-1).astype(t.dtype)


def ssd_chunked(xh, dt, a, bm, cm):
    b, s, g, j, p = xh.shape
    n = bm.shape[-1]
    c = s // SSD_CHUNK
    xf = xh.astype(jnp.float32)
    X = (xf * dt[..., None]).reshape(b, c, SSD_CHUNK, g, j, p)
    adt = (dt * a).reshape(b, c, SSD_CHUNK, g, j).transpose(0, 3, 4, 1, 2)
    a_cs = jnp.cumsum(adt, axis=-1)
    Bc = bm.astype(jnp.float32).reshape(b, c, SSD_CHUNK, g, n)
    Cc = cm.astype(jnp.float32).reshape(b, c, SSD_CHUNK, g, n)
    causal = jnp.tril(jnp.ones((SSD_CHUNK, SSD_CHUNK), dtype=bool))
    seg = a_cs[..., :, None] - a_cs[..., None, :]
    lmat = jnp.exp(jnp.where(causal, seg, -jnp.inf))
    cb = jnp.einsum('bclgn,bcsgn->bcgls', Cc, Bc)
    y_diag = jnp.einsum('bcgls,bgjcls,bcsgjp->bclgjp', cb, lmat, X)
    decay_states = jnp.exp(a_cs[..., -1:] - a_cs)
    states = jnp.einsum('bclgn,bgjcl,bclgjp->bcgjpn', Bc, decay_states, X)
    chunk_decay = jnp.exp(a_cs[..., -1])

    def step(h, inp):
        st, dc = inp
        return h * dc[..., None, None] + st, h

    h0 = jnp.zeros((b, g, j, p, n), jnp.float32)
    _, prev = lax.scan(step, h0, (jnp.moveaxis(states, 1, 0), jnp.moveaxis(chunk_decay, -1, 0)))
    prev = jnp.moveaxis(prev, 0, 1)
    y_off = jnp.einsum('bclgn,bcgjpn,bgjcl->bclgjp', Cc, prev, jnp.exp(a_cs))
    return (y_diag + y_off).reshape(b, s, g, j, p)


def sliding_window_sink_attention(q, k, v, sinks):
    b, s, g, r, d = q.shape
    w = ATTN_WINDOW
    nb = s // w
    qb = q.reshape(b, nb, w, g, r, d) * (d ** -0.5)
    kb = k.reshape(b, nb, w, g, d)
    vb = v.reshape(b, nb, w, g, d)
    kk = jnp.concatenate([jnp.concatenate([jnp.zeros_like(kb[:, :1]), kb[:, :-1]], axis=1), kb], axis=2)
    vv = jnp.concatenate([jnp.concatenate([jnp.zeros_like(vb[:, :1]), vb[:, :-1]], axis=1), vb], axis=2)
    scores = jnp.einsum('bnqgrd,bnkgd->bgrnqk', qb, kk, preferred_element_type=jnp.float32)
    qpos = jnp.arange(w)[:, None] + w
    kpos = jnp.arange(2 * w)[None, :]
    diff = qpos - kpos
    band = (diff >= 0) & (diff < w)
    valid = band[None] & ((jnp.arange(nb)[:, None, None] > 0) | (kpos[None] >= w))
    scores = jnp.where(valid, scores, -jnp.inf)
    sink = sinks.astype(jnp.float32).reshape(g, r)[None, :, :, None, None, None]
    m = jnp.maximum(jnp.max(scores, axis=-1, keepdims=True), sink)
    pexp = jnp.exp(scores - m)
    probs = pexp / (jnp.sum(pexp, axis=-1, keepdims=True) + jnp.exp(sink - m))
    out = jnp.einsum('bgrnqk,bnkgd->bnqgrd', probs.astype(v.dtype), vv)
    return out.reshape(b, s, g * r * d)


def hybrid_layer(x, cos, sin, norm_mix_pre_w, w_in, ssd_conv_w, ssd_conv_b, ssd_dt_bias, ssd_a_log, ssd_d,
                 ssd_norm_w, ssd_w_out, attn_sinks, attn_w_out, w_mix_out, norm_mix_post_w,
                 norm_ffn_pre_w, ffn_w_up, ffn_conv_w, ffn_conv_b, ffn_w_down, norm_ffn_post_w):
    b, s, _ = x.shape
    u = rms_norm(x, norm_mix_pre_w)
    proj = u @ w_in
    z, xbc, dt_raw, q, k, v, gate_ssd, gate_attn = _split(proj, IN_SIZES)

    xbc = jax.nn.silu(causal_dwconv(xbc, ssd_conv_w, ssd_conv_b))
    xs, bm, cm = _split(xbc, (SSD_D_INNER, SSD_N_GROUPS * SSD_D_STATE, SSD_N_GROUPS * SSD_D_STATE))
    xh = xs.reshape(b, s, SSD_N_GROUPS, SSD_HEADS_PER_GROUP, SSD_HEAD_DIM)
    bm = bm.reshape(b, s, SSD_N_GROUPS, SSD_D_STATE)
    cm = cm.reshape(b, s, SSD_N_GROUPS, SSD_D_STATE)
    dt = jax.nn.softplus(dt_raw.astype(jnp.float32) + ssd_dt_bias.astype(jnp.float32)).reshape(b, s, SSD_N_GROUPS, SSD_HEADS_PER_GROUP)
    a = -jnp.exp(ssd_a_log.astype(jnp.float32)).reshape(SSD_N_GROUPS, SSD_HEADS_PER_GROUP)
    y = ssd_chunked(xh, dt, a, bm, cm)
    y = y + ssd_d.astype(jnp.float32).reshape(SSD_N_GROUPS, SSD_HEADS_PER_GROUP)[..., None] * xh.astype(jnp.float32)
    y = gated_group_rms_norm(y.reshape(b, s, SSD_D_INNER).astype(x.dtype), z, ssd_norm_w)
    y_ssd = y @ ssd_w_out

    q = apply_rope(q.reshape(b, s, ATTN_N_KV_HEADS, ATTN_REP, ATTN_HEAD_DIM), cos, sin)
    k = apply_rope(k.reshape(b, s, ATTN_N_KV_HEADS, ATTN_HEAD_DIM), cos, sin)
    v = v.reshape(b, s, ATTN_N_KV_HEADS, ATTN_HEAD_DIM)
    y_attn = sliding_window_sink_attention(q, k, v, attn_sinks) @ attn_w_out

    merged = jax.nn.sigmoid(gate_ssd) * y_ssd + jax.nn.sigmoid(gate_attn) * y_attn
    x = x + rms_norm(merged @ w_mix_out, norm_mix_post_w)

    h = rms_norm(x, norm_ffn_pre_w)
    up = causal_dwconv(h @ ffn_w_up, ffn_conv_w, ffn_conv_b)
    gate, val = _split(up, (FFN_D_FF, FFN_D_FF))
    ff = (jax.nn.gelu(gate, approximate=True) * val) @ ffn_w_down
    return x + rms_norm(ff, norm_ffn_post_w)


def _fwd_setup_inputs(seed: int = 0) -> dict:
    key = jax.random.key(seed)
    ks = jax.random.split(key, 24)
    L = DEPTH
    f32 = jnp.float32

    def nrm(k, shape, scale):
        return jax.random.normal(k, shape, f32) * scale

    def gain(k, d):
        return 1.0 + 0.02 * jax.random.normal(k, (L, d), f32)

    x = jax.random.normal(ks[0], (BATCH, SEQ, D_MODEL), f32)
    start = jax.random.randint(ks[1], (BATCH,), 0, 4096, dtype=jnp.int32)
    positions = (start[:, None] + jnp.arange(SEQ, dtype=jnp.int32)[None, :]).astype(jnp.int32)
    u = jax.random.uniform(ks[2], (L, SSD_N_HEADS), f32)
    dt0 = jnp.exp(u * (np.log(0.1) - np.log(0.001)) + np.log(0.001)).astype(f32)
    ssd_dt_bias = dt0 + jnp.log(-jnp.expm1(-dt0))
    ssd_a_log = jnp.log(jax.random.uniform(ks[3], (L, SSD_N_HEADS), f32, minval=1.0, maxval=16.0))
    return {
        "x": x,
        "positions": positions,
        "norm_mix_pre_w": gain(ks[4], D_MODEL),
        "w_in": nrm(ks[5], (L, D_MODEL, IN_PROJ_DIM), D_MODEL ** -0.5),
        "ssd_conv_w": nrm(ks[6], (L, SSD_CONV_WIDTH, SSD_CONV_DIM), SSD_CONV_WIDTH ** -0.5),
        "ssd_conv_b": nrm(ks[7], (L, SSD_CONV_DIM), 0.02),
        "ssd_dt_bias": ssd_dt_bias,
        "ssd_a_log": ssd_a_log,
        "ssd_d": 1.0 + 0.1 * jax.random.normal(ks[8], (L, SSD_N_HEADS), f32),
        "ssd_norm_w": gain(ks[9], SSD_D_INNER),
        "ssd_w_out": nrm(ks[10], (L, SSD_D_INNER, D_MODEL), SSD_D_INNER ** -0.5),
        "attn_sinks": nrm(ks[11], (L, ATTN_N_HEADS), 1.0),
        "attn_w_out": nrm(ks[12], (L, ATTN_WIDTH, D_MODEL), ATTN_WIDTH ** -0.5),
        "w_mix_out": nrm(ks[13], (L, D_MODEL, D_MODEL), D_MODEL ** -0.5),
        "norm_mix_post_w": gain(ks[14], D_MODEL),
        "norm_ffn_pre_w": gain(ks[15], D_MODEL),
        "ffn_w_up": nrm(ks[16], (L, D_MODEL, 2 * FFN_D_FF), D_MODEL ** -0.5),
        "ffn_conv_w": nrm(ks[17], (L, FFN_CONV_WIDTH, 2 * FFN_D_FF), FFN_CONV_WIDTH ** -0.5),
        "ffn_conv_b": nrm(ks[18], (L, 2 * FFN_D_FF), 0.02),
        "ffn_w_down": nrm(ks[19], (L, FFN_D_FF, D_MODEL), FFN_D_FF ** -0.5),
        "norm_ffn_post_w": gain(ks[20], D_MODEL),
    }


def _fwd_reference(x, positions, norm_mix_pre_w, w_in, ssd_conv_w, ssd_conv_b, ssd_dt_bias, ssd_a_log, ssd_d,
              ssd_norm_w, ssd_w_out, attn_sinks, attn_w_out, w_mix_out, norm_mix_post_w,
              norm_ffn_pre_w, ffn_w_up, ffn_conv_w, ffn_conv_b, ffn_w_down, norm_ffn_post_w):
    cos, sin = rope_tables(positions)
    for i in range(DEPTH):
        x = hybrid_layer(x, cos, sin, norm_mix_pre_w[i], w_in[i], ssd_conv_w[i], ssd_conv_b[i], ssd_dt_bias[i],
                         ssd_a_log[i], ssd_d[i], ssd_norm_w[i], ssd_w_out[i], attn_sinks[i], attn_w_out[i],
                         w_mix_out[i], norm_mix_post_w[i], norm_ffn_pre_w[i], ffn_w_up[i], ffn_conv_w[i],
                         ffn_conv_b[i], ffn_w_down[i], norm_ffn_post_w[i])
    return x


import jax as _jax
import jax.numpy as _jnp

TWIN_FORMAT = 'train_step'
FWD_PARAMS = ['x', 'positions', 'norm_mix_pre_w', 'w_in', 'ssd_conv_w', 'ssd_conv_b', 'ssd_dt_bias', 'ssd_a_log', 'ssd_d', 'ssd_norm_w', 'ssd_w_out', 'attn_sinks', 'attn_w_out', 'w_mix_out', 'norm_mix_post_w', 'norm_ffn_pre_w', 'ffn_w_up', 'ffn_conv_w', 'ffn_conv_b', 'ffn_w_down', 'norm_ffn_post_w']
TWIN_WEIGHTS = ['norm_mix_pre_w', 'w_in', 'ssd_conv_w', 'ssd_conv_b', 'ssd_dt_bias', 'ssd_a_log', 'ssd_d', 'ssd_norm_w', 'ssd_w_out', 'attn_sinks', 'attn_w_out', 'w_mix_out', 'norm_mix_post_w', 'norm_ffn_pre_w', 'ffn_w_up', 'ffn_conv_w', 'ffn_conv_b', 'ffn_w_down', 'norm_ffn_post_w']
TWIN_DIFF_INPUT = 'x'
TWIN_INPUTS = ['x', 'positions', 'norm_mix_pre_w', 'w_in', 'ssd_conv_w', 'ssd_conv_b', 'ssd_dt_bias', 'ssd_a_log', 'ssd_d', 'ssd_norm_w', 'ssd_w_out', 'attn_sinks', 'attn_w_out', 'w_mix_out', 'norm_mix_post_w', 'norm_ffn_pre_w', 'ffn_w_up', 'ffn_conv_w', 'ffn_conv_b', 'ffn_w_down', 'norm_ffn_post_w', 'loss_target', 'm_norm_mix_pre_w', 'm_w_in', 'm_ssd_conv_w', 'm_ssd_conv_b', 'm_ssd_dt_bias', 'm_ssd_a_log', 'm_ssd_d', 'm_ssd_norm_w', 'm_ssd_w_out', 'm_attn_sinks', 'm_attn_w_out', 'm_w_mix_out', 'm_norm_mix_post_w', 'm_norm_ffn_pre_w', 'm_ffn_w_up', 'm_ffn_conv_w', 'm_ffn_conv_b', 'm_ffn_w_down', 'm_norm_ffn_post_w', 'v_norm_mix_pre_w', 'v_w_in', 'v_ssd_conv_w', 'v_ssd_conv_b', 'v_ssd_dt_bias', 'v_ssd_a_log', 'v_ssd_d', 'v_ssd_norm_w', 'v_ssd_w_out', 'v_attn_sinks', 'v_attn_w_out', 'v_w_mix_out', 'v_norm_mix_post_w', 'v_norm_ffn_pre_w', 'v_ffn_w_up', 'v_ffn_conv_w', 'v_ffn_conv_b', 'v_ffn_w_down', 'v_norm_ffn_post_w']
TWIN_OUTPUTS = ['loss', 'grad_x', 'grad_norm_mix_pre_w', 'grad_w_in', 'grad_ssd_conv_w', 'grad_ssd_conv_b', 'grad_ssd_dt_bias', 'grad_ssd_a_log', 'grad_ssd_d', 'grad_ssd_norm_w', 'grad_ssd_w_out', 'grad_attn_sinks', 'grad_attn_w_out', 'grad_w_mix_out', 'grad_norm_mix_post_w', 'grad_norm_ffn_pre_w', 'grad_ffn_w_up', 'grad_ffn_conv_w', 'grad_ffn_conv_b', 'grad_ffn_w_down', 'grad_norm_ffn_post_w', 'delta_norm_mix_pre_w', 'delta_w_in', 'delta_ssd_conv_w', 'delta_ssd_conv_b', 'delta_ssd_dt_bias', 'delta_ssd_a_log', 'delta_ssd_d', 'delta_ssd_norm_w', 'delta_ssd_w_out', 'delta_attn_sinks', 'delta_attn_w_out', 'delta_w_mix_out', 'delta_norm_mix_post_w', 'delta_norm_ffn_pre_w', 'delta_ffn_w_up', 'delta_ffn_conv_w', 'delta_ffn_conv_b', 'delta_ffn_w_down', 'delta_norm_ffn_post_w', 'new_m_norm_mix_pre_w', 'new_m_w_in', 'new_m_ssd_conv_w', 'new_m_ssd_conv_b', 'new_m_ssd_dt_bias', 'new_m_ssd_a_log', 'new_m_ssd_d', 'new_m_ssd_norm_w', 'new_m_ssd_w_out', 'new_m_attn_sinks', 'new_m_attn_w_out', 'new_m_w_mix_out', 'new_m_norm_mix_post_w', 'new_m_norm_ffn_pre_w', 'new_m_ffn_w_up', 'new_m_ffn_conv_w', 'new_m_ffn_conv_b', 'new_m_ffn_w_down', 'new_m_norm_ffn_post_w', 'new_v_norm_mix_pre_w', 'new_v_w_in', 'new_v_ssd_conv_w', 'new_v_ssd_conv_b', 'new_v_ssd_dt_bias', 'new_v_ssd_a_log', 'new_v_ssd_d', 'new_v_ssd_norm_w', 'new_v_ssd_w_out', 'new_v_attn_sinks', 'new_v_attn_w_out', 'new_v_w_mix_out', 'new_v_norm_mix_post_w', 'new_v_norm_ffn_pre_w', 'new_v_ffn_w_up', 'new_v_ffn_conv_w', 'new_v_ffn_conv_b', 'new_v_ffn_w_down', 'new_v_norm_ffn_post_w']
TWIN_LEAF_KINDS = {'loss': 'loss', 'grad_x': 'grad_x', 'grad_norm_mix_pre_w': 'grad_w', 'grad_w_in': 'grad_w', 'grad_ssd_conv_w': 'grad_w', 'grad_ssd_conv_b': 'grad_w', 'grad_ssd_dt_bias': 'grad_w', 'grad_ssd_a_log': 'grad_w', 'grad_ssd_d': 'grad_w', 'grad_ssd_norm_w': 'grad_w', 'grad_ssd_w_out': 'grad_w', 'grad_attn_sinks': 'grad_w', 'grad_attn_w_out': 'grad_w', 'grad_w_mix_out': 'grad_w', 'grad_norm_mix_post_w': 'grad_w', 'grad_norm_ffn_pre_w': 'grad_w', 'grad_ffn_w_up': 'grad_w', 'grad_ffn_conv_w': 'grad_w', 'grad_ffn_conv_b': 'grad_w', 'grad_ffn_w_down': 'grad_w', 'grad_norm_ffn_post_w': 'grad_w', 'delta_norm_mix_pre_w': 'delta_w', 'delta_w_in': 'delta_w', 'delta_ssd_conv_w': 'delta_w', 'delta_ssd_conv_b': 'delta_w', 'delta_ssd_dt_bias': 'delta_w', 'delta_ssd_a_log': 'delta_w', 'delta_ssd_d': 'delta_w', 'delta_ssd_norm_w': 'delta_w', 'delta_ssd_w_out': 'delta_w', 'delta_attn_sinks': 'delta_w', 'delta_attn_w_out': 'delta_w', 'delta_w_mix_out': 'delta_w', 'delta_norm_mix_post_w': 'delta_w', 'delta_norm_ffn_pre_w': 'delta_w', 'delta_ffn_w_up': 'delta_w', 'delta_ffn_conv_w': 'delta_w', 'delta_ffn_conv_b': 'delta_w', 'delta_ffn_w_down': 'delta_w', 'delta_norm_ffn_post_w': 'delta_w', 'new_m_norm_mix_pre_w': 'new_m', 'new_m_w_in': 'new_m', 'new_m_ssd_conv_w': 'new_m', 'new_m_ssd_conv_b': 'new_m', 'new_m_ssd_dt_bias': 'new_m', 'new_m_ssd_a_log': 'new_m', 'new_m_ssd_d': 'new_m', 'new_m_ssd_norm_w': 'new_m', 'new_m_ssd_w_out': 'new_m', 'new_m_attn_sinks': 'new_m', 'new_m_attn_w_out': 'new_m', 'new_m_w_mix_out': 'new_m', 'new_m_norm_mix_post_w': 'new_m', 'new_m_norm_ffn_pre_w': 'new_m', 'new_m_ffn_w_up': 'new_m', 'new_m_ffn_conv_w': 'new_m', 'new_m_ffn_conv_b': 'new_m', 'new_m_ffn_w_down': 'new_m', 'new_m_norm_ffn_post_w': 'new_m', 'new_v_norm_mix_pre_w': 'new_v', 'new_v_w_in': 'new_v', 'new_v_ssd_conv_w': 'new_v', 'new_v_ssd_conv_b': 'new_v', 'new_v_ssd_dt_bias': 'new_v', 'new_v_ssd_a_log': 'new_v', 'new_v_ssd_d': 'new_v', 'new_v_ssd_norm_w': 'new_v', 'new_v_ssd_w_out': 'new_v', 'new_v_attn_sinks': 'new_v', 'new_v_attn_w_out': 'new_v', 'new_v_w_mix_out': 'new_v', 'new_v_norm_mix_post_w': 'new_v', 'new_v_norm_ffn_pre_w': 'new_v', 'new_v_ffn_w_up': 'new_v', 'new_v_ffn_conv_w': 'new_v', 'new_v_ffn_conv_b': 'new_v', 'new_v_ffn_w_down': 'new_v', 'new_v_norm_ffn_post_w': 'new_v'}


def _forward(args):
    return _fwd_reference(*[args[k] for k in FWD_PARAMS])


def _output_shape():
    def fwd():
        inp = _fwd_setup_inputs(0)
        return _fwd_reference(*[inp[k] for k in FWD_PARAMS])
    out = _jax.eval_shape(fwd)
    return out.shape, out.dtype

N_MICROBATCH = 1
ADAM_LR = 0.001
ADAM_B1 = 0.9
ADAM_B2 = 0.999
ADAM_EPS = 1e-08
ADAM_WD = 0.01
ADAM_STEP = 10
PER_EXAMPLE_BATCH_AXIS = {'x': 0, 'positions': 0, 'loss_target': 0}
SHARED_INPUTS = []
_WEIGHT_DTYPES = {'norm_mix_pre_w': _jnp.float32, 'w_in': _jnp.float32, 'ssd_conv_w': _jnp.float32, 'ssd_conv_b': _jnp.float32, 'ssd_dt_bias': _jnp.float32, 'ssd_a_log': _jnp.float32, 'ssd_d': _jnp.float32, 'ssd_norm_w': _jnp.float32, 'ssd_w_out': _jnp.float32, 'attn_sinks': _jnp.float32, 'attn_w_out': _jnp.float32, 'w_mix_out': _jnp.float32, 'norm_mix_post_w': _jnp.float32, 'norm_ffn_pre_w': _jnp.float32, 'ffn_w_up': _jnp.float32, 'ffn_conv_w': _jnp.float32, 'ffn_conv_b': _jnp.float32, 'ffn_w_down': _jnp.float32, 'norm_ffn_post_w': _jnp.float32}
MOMENT_SCALE = {'norm_mix_pre_w': 1.006238e+00, 'w_in': 3.661485e-01, 'ssd_conv_w': 5.196729e-01, 'ssd_conv_b': 1.479051e+00, 'ssd_dt_bias': 1.350074e+00, 'ssd_a_log': 2.447746e+00, 'ssd_d': 3.409080e+00, 'ssd_norm_w': 9.101636e-01, 'ssd_w_out': 1.163144e+00, 'attn_sinks': 9.735184e-02, 'attn_w_out': 1.266927e-01, 'w_mix_out': 1.198830e+00, 'norm_mix_post_w': 6.383274e+01, 'norm_ffn_pre_w': 1.001236e+00, 'ffn_w_up': 4.157937e-01, 'ffn_conv_w': 4.833992e-01, 'ffn_conv_b': 2.451683e+00, 'ffn_w_down': 9.005533e-01, 'norm_ffn_post_w': 6.387665e+01}


def _to_microbatches(a, axis):
    t = _jnp.moveaxis(a, axis, 0)
    t = t.reshape((N_MICROBATCH, t.shape[0] // N_MICROBATCH) + t.shape[1:])
    return _jnp.moveaxis(t, 1, axis + 1)


def setup_inputs(seed: int = 0) -> dict:
    inp = _fwd_setup_inputs(seed)
    key = _jax.random.fold_in(_jax.random.key(seed), 7919)
    shape, _ = _output_shape()
    out = dict(inp)
    out["loss_target"] = _jax.random.normal(_jax.random.fold_in(key, 0), shape, _jnp.float32)
    for i, name in enumerate(TWIN_WEIGHTS):
        w = inp[name].astype(_jnp.float32)
        if MOMENT_SCALE is None:
            s = _jnp.sqrt(_jnp.mean(_jnp.square(w)) + 1e-30)
        else:
            s = MOMENT_SCALE[name]
        km, kv = _jax.random.split(_jax.random.fold_in(key, i + 1))
        out[name] = w
        out["m_" + name] = s * _jax.random.normal(km, w.shape, _jnp.float32)
        out["v_" + name] = (s * s) * _jax.random.uniform(kv, w.shape, _jnp.float32, 0.5, 1.5)
    if N_MICROBATCH > 1:
        for name, axis in PER_EXAMPLE_BATCH_AXIS.items():
            out[name] = _to_microbatches(out[name], axis)
    return {'x': out['x'], 'positions': out['positions'], 'norm_mix_pre_w': out['norm_mix_pre_w'], 'w_in': out['w_in'], 'ssd_conv_w': out['ssd_conv_w'], 'ssd_conv_b': out['ssd_conv_b'], 'ssd_dt_bias': out['ssd_dt_bias'], 'ssd_a_log': out['ssd_a_log'], 'ssd_d': out['ssd_d'], 'ssd_norm_w': out['ssd_norm_w'], 'ssd_w_out': out['ssd_w_out'], 'attn_sinks': out['attn_sinks'], 'attn_w_out': out['attn_w_out'], 'w_mix_out': out['w_mix_out'], 'norm_mix_post_w': out['norm_mix_post_w'], 'norm_ffn_pre_w': out['norm_ffn_pre_w'], 'ffn_w_up': out['ffn_w_up'], 'ffn_conv_w': out['ffn_conv_w'], 'ffn_conv_b': out['ffn_conv_b'], 'ffn_w_down': out['ffn_w_down'], 'norm_ffn_post_w': out['norm_ffn_post_w'], 'loss_target': out['loss_target'], 'm_norm_mix_pre_w': out['m_norm_mix_pre_w'], 'm_w_in': out['m_w_in'], 'm_ssd_conv_w': out['m_ssd_conv_w'], 'm_ssd_conv_b': out['m_ssd_conv_b'], 'm_ssd_dt_bias': out['m_ssd_dt_bias'], 'm_ssd_a_log': out['m_ssd_a_log'], 'm_ssd_d': out['m_ssd_d'], 'm_ssd_norm_w': out['m_ssd_norm_w'], 'm_ssd_w_out': out['m_ssd_w_out'], 'm_attn_sinks': out['m_attn_sinks'], 'm_attn_w_out': out['m_attn_w_out'], 'm_w_mix_out': out['m_w_mix_out'], 'm_norm_mix_post_w': out['m_norm_mix_post_w'], 'm_norm_ffn_pre_w': out['m_norm_ffn_pre_w'], 'm_ffn_w_up': out['m_ffn_w_up'], 'm_ffn_conv_w': out['m_ffn_conv_w'], 'm_ffn_conv_b': out['m_ffn_conv_b'], 'm_ffn_w_down': out['m_ffn_w_down'], 'm_norm_ffn_post_w': out['m_norm_ffn_post_w'], 'v_norm_mix_pre_w': out['v_norm_mix_pre_w'], 'v_w_in': out['v_w_in'], 'v_ssd_conv_w': out['v_ssd_conv_w'], 'v_ssd_conv_b': out['v_ssd_conv_b'], 'v_ssd_dt_bias': out['v_ssd_dt_bias'], 'v_ssd_a_log': out['v_ssd_a_log'], 'v_ssd_d': out['v_ssd_d'], 'v_ssd_norm_w': out['v_ssd_norm_w'], 'v_ssd_w_out': out['v_ssd_w_out'], 'v_attn_sinks': out['v_attn_sinks'], 'v_attn_w_out': out['v_attn_w_out'], 'v_w_mix_out': out['v_w_mix_out'], 'v_norm_mix_post_w': out['v_norm_mix_post_w'], 'v_norm_ffn_pre_w': out['v_norm_ffn_pre_w'], 'v_ffn_w_up': out['v_ffn_w_up'], 'v_ffn_conv_w': out['v_ffn_conv_w'], 'v_ffn_conv_b': out['v_ffn_conv_b'], 'v_ffn_w_down': out['v_ffn_w_down'], 'v_norm_ffn_post_w': out['v_norm_ffn_post_w']}


def _loss(weights, diff, rest, loss_target):
    with _jax.named_scope("forward"):
        args = {**rest, TWIN_DIFF_INPUT: diff, **{k: w.astype(_WEIGHT_DTYPES[k]) for k, w in weights.items()}}
        y = _forward(args)
    with _jax.named_scope("loss_head"):
        err = _jnp.square(y.astype(_jnp.float32) - loss_target)
        return 0.5 * _jnp.sum(_jnp.mean(err, axis=-1)) if err.ndim else 0.5 * err


def _adamw(w, g, m, v):
    m = ADAM_B1 * m + (1.0 - ADAM_B1) * g
    v = ADAM_B2 * v + (1.0 - ADAM_B2) * _jnp.square(g)
    m_hat = m / (1.0 - ADAM_B1 ** ADAM_STEP)
    v_hat = v / (1.0 - ADAM_B2 ** ADAM_STEP)
    delta = -ADAM_LR * (m_hat / (_jnp.sqrt(v_hat) + ADAM_EPS) + ADAM_WD * w)
    return delta, m, v


def reference(x, positions, norm_mix_pre_w, w_in, ssd_conv_w, ssd_conv_b, ssd_dt_bias, ssd_a_log, ssd_d, ssd_norm_w, ssd_w_out, attn_sinks, attn_w_out, w_mix_out, norm_mix_post_w, norm_ffn_pre_w, ffn_w_up, ffn_conv_w, ffn_conv_b, ffn_w_down, norm_ffn_post_w, loss_target, m_norm_mix_pre_w, m_w_in, m_ssd_conv_w, m_ssd_conv_b, m_ssd_dt_bias, m_ssd_a_log, m_ssd_d, m_ssd_norm_w, m_ssd_w_out, m_attn_sinks, m_attn_w_out, m_w_mix_out, m_norm_mix_post_w, m_norm_ffn_pre_w, m_ffn_w_up, m_ffn_conv_w, m_ffn_conv_b, m_ffn_w_down, m_norm_ffn_post_w, v_norm_mix_pre_w, v_w_in, v_ssd_conv_w, v_ssd_conv_b, v_ssd_dt_bias, v_ssd_a_log, v_ssd_d, v_ssd_norm_w, v_ssd_w_out, v_attn_sinks, v_attn_w_out, v_w_mix_out, v_norm_mix_post_w, v_norm_ffn_pre_w, v_ffn_w_up, v_ffn_conv_w, v_ffn_conv_b, v_ffn_w_down, v_norm_ffn_post_w):
    given = dict(x=x, positions=positions, norm_mix_pre_w=norm_mix_pre_w, w_in=w_in, ssd_conv_w=ssd_conv_w, ssd_conv_b=ssd_conv_b, ssd_dt_bias=ssd_dt_bias, ssd_a_log=ssd_a_log, ssd_d=ssd_d, ssd_norm_w=ssd_norm_w, ssd_w_out=ssd_w_out, attn_sinks=attn_sinks, attn_w_out=attn_w_out, w_mix_out=w_mix_out, norm_mix_post_w=norm_mix_post_w, norm_ffn_pre_w=norm_ffn_pre_w, ffn_w_up=ffn_w_up, ffn_conv_w=ffn_conv_w, ffn_conv_b=ffn_conv_b, ffn_w_down=ffn_w_down, norm_ffn_post_w=norm_ffn_post_w, loss_target=loss_target, m_norm_mix_pre_w=m_norm_mix_pre_w, m_w_in=m_w_in, m_ssd_conv_w=m_ssd_conv_w, m_ssd_conv_b=m_ssd_conv_b, m_ssd_dt_bias=m_ssd_dt_bias, m_ssd_a_log=m_ssd_a_log, m_ssd_d=m_ssd_d, m_ssd_norm_w=m_ssd_norm_w, m_ssd_w_out=m_ssd_w_out, m_attn_sinks=m_attn_sinks, m_attn_w_out=m_attn_w_out, m_w_mix_out=m_w_mix_out, m_norm_mix_post_w=m_norm_mix_post_w, m_norm_ffn_pre_w=m_norm_ffn_pre_w, m_ffn_w_up=m_ffn_w_up, m_ffn_conv_w=m_ffn_conv_w, m_ffn_conv_b=m_ffn_conv_b, m_ffn_w_down=m_ffn_w_down, m_norm_ffn_post_w=m_norm_ffn_post_w, v_norm_mix_pre_w=v_norm_mix_pre_w, v_w_in=v_w_in, v_ssd_conv_w=v_ssd_conv_w, v_ssd_conv_b=v_ssd_conv_b, v_ssd_dt_bias=v_ssd_dt_bias, v_ssd_a_log=v_ssd_a_log, v_ssd_d=v_ssd_d, v_ssd_norm_w=v_ssd_norm_w, v_ssd_w_out=v_ssd_w_out, v_attn_sinks=v_attn_sinks, v_attn_w_out=v_attn_w_out, v_w_mix_out=v_w_mix_out, v_norm_mix_post_w=v_norm_mix_post_w, v_norm_ffn_pre_w=v_norm_ffn_pre_w, v_ffn_w_up=v_ffn_w_up, v_ffn_conv_w=v_ffn_conv_w, v_ffn_conv_b=v_ffn_conv_b, v_ffn_w_down=v_ffn_w_down, v_norm_ffn_post_w=v_norm_ffn_post_w)
    weights = {n: given[n] for n in TWIN_WEIGHTS}
    shared = {n: given[n] for n in SHARED_INPUTS}
    per_example = {n: given[n] for n in ['x', 'positions']}
    grad_fn = _jax.value_and_grad(_loss, argnums=(0, 1))

    def one_microbatch(ex, loss_target):
        ex = dict(ex)
        diff = ex.pop(TWIN_DIFF_INPUT)
        return grad_fn(weights, diff, {**shared, **ex}, loss_target)

    if N_MICROBATCH == 1:
        loss, (grad_w, grad_x) = one_microbatch(per_example, given["loss_target"])
    else:
        def body(carry, xs):
            loss_sum, grad_sum = carry
            l_k, (gw_k, gx_k) = one_microbatch(xs[0], xs[1])
            with _jax.named_scope("update"):
                return (loss_sum + l_k, _jax.tree.map(_jnp.add, grad_sum, gw_k)), gx_k

        init = (_jnp.zeros((), _jnp.float32), _jax.tree.map(_jnp.zeros_like, weights))
        (loss, grad_w), grad_x = _jax.lax.scan(body, init, (per_example, given["loss_target"]))
    with _jax.named_scope("update"):
        delta_w, new_m, new_v = {}, {}, {}
        for n in TWIN_WEIGHTS:
            delta_w[n], new_m[n], new_v[n] = _adamw(weights[n], grad_w[n], given["m_" + n], given["v_" + n])
    return (loss, grad_x, *[grad_w[n] for n in TWIN_WEIGHTS], *[delta_w[n] for n in TWIN_WEIGHTS],
            *[new_m[n] for n in TWIN_WEIGHTS], *[new_v[n] for n in TWIN_WEIGHTS])
```

```python
import functools

import jax
import jax.numpy as jnp
import numpy as np
from jax import lax
from jax.experimental import pallas as pl
from jax.experimental.pallas import tpu as pltpu

f32 = jnp.float32
bf16 = jnp.bfloat16

N_DEV = 8
D = 1024
DI = 2048
NH = 32
HD = 64
NG = 4
GW = DI // NG
NS = 128
CH = 128
CONVD = DI + 2 * NG * NS
SSD_K = 4
AH = 16
AD = 64
KVH = 4
REP = AH // KVH
KVW = KVH * AD
WIN = 128
FF = 2816
FFN_K = 3
EPS = 1e-6
ROPE_THETA = 10000.0
LANES = 128

C_Z, C_XBC, C_Q, C_GS, C_GA, C_K, C_V, PW = 0, 2048, 5120, 6144, 7168, 8192, 8448, 8704
IN_SIZES = (DI, CONVD, NH, D, KVW, KVW, D, D)
IN_OFF = tuple(int(v) for v in np.cumsum((0,) + IN_SIZES))
IN_DIM = IN_OFF[-1]

ADAM_LR, ADAM_B1, ADAM_B2, ADAM_EPS, ADAM_WD, ADAM_STEP = 0.001, 0.9, 0.999, 1e-08, 0.01, 10

VMEM_LIMIT = 56 * 1024 * 1024


def _cp(*sem):
    return pltpu.CompilerParams(dimension_semantics=sem, vmem_limit_bytes=VMEM_LIMIT)


def _pick(n, cands):
    for c in cands:
        if n % c == 0:
            return c
    raise ValueError(f"no tile for {n}")


def _dot(a, b, mode="nn"):
    dims = {"nn": (((1,), (0,)), ((), ())), "nt": (((1,), (1,)), ((), ())), "tn": (((0,), (0,)), ((), ()))}[mode]
    return lax.dot_general(a, b, dims, preferred_element_type=f32)


def _split3(v):
    hi = v.astype(bf16)
    r = v - hi.astype(f32)
    mid = r.astype(bf16)
    lo = (r - mid.astype(f32)).astype(bf16)
    return hi, mid, lo


def _dot3_left(m01, v):
    hi, mid, lo = _split3(v)
    return _dot(m01, hi) + _dot(m01, mid) + _dot(m01, lo)


def _dot3_right(v, m01):
    hi, mid, lo = _split3(v)
    return _dot(hi, m01) + _dot(mid, m01) + _dot(lo, m01)


def _dot2_right(v, m01):
    hi = v.astype(bf16)
    lo = (v - hi.astype(f32)).astype(bf16)
    return _dot(hi, m01) + _dot(lo, m01)


def _sigmoid(x):
    return 1.0 / (1.0 + jnp.exp(-x))


def mm(a, b, mode, out_dtype, name):
    if mode == "nn":
        (M, K), (_, N) = a.shape, b.shape
    elif mode == "nt":
        (M, K), (N, _) = a.shape, b.shape
    else:
        (K, M), (_, N) = a.shape, b.shape
    bm = _pick(M, (1024, 512, 256, 128))
    bn = _pick(N, (512, 256, 128))
    bk = _pick(K, (1024, 512, 256, 128))
    nk = K // bk

    def body(a_ref, b_ref, o_ref, acc_ref):
        k = pl.program_id(2)
        p = _dot(a_ref[...], b_ref[...], mode)

        @pl.when(k == 0)
        def _():
            acc_ref[...] = p

        @pl.when(k > 0)
        def _():
            acc_ref[...] += p

        @pl.when(k == nk - 1)
        def _():
            o_ref[...] = acc_ref[...].astype(o_ref.dtype)

    if mode == "nn":
        a_spec = pl.BlockSpec((bm, bk), lambda i, j, k: (i, k))
        b_spec = pl.BlockSpec((bk, bn), lambda i, j, k: (k, j))
    elif mode == "nt":
        a_spec = pl.BlockSpec((bm, bk), lambda i, j, k: (i, k))
        b_spec = pl.BlockSpec((bn, bk), lambda i, j, k: (j, k))
    else:
        a_spec = pl.BlockSpec((bk, bm), lambda i, j, k: (k, i))
        b_spec = pl.BlockSpec((bk, bn), lambda i, j, k: (k, j))
    return pl.pallas_call(
        body, name=name, grid=(M // bm, N // bn, nk),
        in_specs=[a_spec, b_spec],
        out_specs=pl.BlockSpec((bm, bn), lambda i, j, k: (i, j)),
        out_shape=jax.ShapeDtypeStruct((M, N), out_dtype),
        scratch_shapes=[pltpu.VMEM((bm, bn), f32)],
        compiler_params=_cp("parallel", "parallel", "arbitrary"),
    )(a, b)


def _rms(x):
    return lax.rsqrt(jnp.mean(x * x, axis=-1, keepdims=True) + EPS)


def _rms_bwd(x, r, dn):
    n = x * r
    return r * (dn - n * jnp.mean(dn * n, axis=-1, keepdims=True))


def _acc_row(ref, row, val, first):
    s = jnp.sum(val, axis=0, keepdims=True)

    @pl.when(first)
    def _():
        ref[row:row + 1, :] = s

    @pl.when(jnp.logical_not(first))
    def _():
        ref[row:row + 1, :] += s


def prenorm_fwd(x, w, ts):
    S = x.shape[0]

    def body(x_ref, w_ref, u_ref):
        xv = x_ref[...]
        u_ref[...] = (xv * _rms(xv) * w_ref[...]).astype(bf16)

    return pl.pallas_call(
        body, name="prenorm_fwd", grid=(S // ts,),
        in_specs=[pl.BlockSpec((ts, D), lambda i: (i, 0)), pl.BlockSpec((1, D), lambda i: (0, 0))],
        out_specs=pl.BlockSpec((ts, D), lambda i: (i, 0)),
        out_shape=jax.ShapeDtypeStruct((S, D), bf16),
        compiler_params=_cp("parallel"),
    )(x, w)


def prenorm_bwd(x, w, du_a, du_b, dx1, ts):
    S = x.shape[0]

    def body(x_ref, w_ref, da_ref, db_ref, dx1_ref, gx_ref, gw_ref):
        i = pl.program_id(0)
        xv = x_ref[...]
        r = _rms(xv)
        du = da_ref[...].astype(f32) + db_ref[...].astype(f32)
        gx_ref[...] = dx1_ref[...] + _rms_bwd(xv, r, du * w_ref[...])

        @pl.when(i == 0)
        def _():
            gw_ref[...] = jnp.zeros_like(gw_ref)

        gw_ref[0:1, :] += jnp.sum(du * xv * r, axis=0, keepdims=True)

    row = pl.BlockSpec((ts, D), lambda i: (i, 0))
    return pl.pallas_call(
        body, name="prenorm_bwd", grid=(S // ts,),
        in_specs=[row, pl.BlockSpec((1, D), lambda i: (0, 0)), row, row, row],
        out_specs=[row, pl.BlockSpec((8, D), lambda i: (0, 0))],
        out_shape=[jax.ShapeDtypeStruct((S, D), f32), jax.ShapeDtypeStruct((8, D), f32)],
        compiler_params=_cp("arbitrary"),
    )(x, w, du_a, du_b, dx1)


def post_fwd(x, mo, w_post, w_pre2, ts):
    S = x.shape[0]

    def body(x_ref, mo_ref, wp_ref, w2_ref, x1_ref, h_ref):
        mv = mo_ref[...]
        x1 = x_ref[...] + mv * _rms(mv) * wp_ref[...]
        x1_ref[...] = x1
        h_ref[...] = (x1 * _rms(x1) * w2_ref[...]).astype(bf16)

    row = pl.BlockSpec((ts, D), lambda i: (i, 0))
    par = pl.BlockSpec((1, D), lambda i: (0, 0))
    return pl.pallas_call(
        body, name="post_fwd", grid=(S // ts,),
        in_specs=[row, row, par, par], out_specs=[row, row],
        out_shape=[jax.ShapeDtypeStruct((S, D), f32), jax.ShapeDtypeStruct((S, D), bf16)],
        compiler_params=_cp("parallel"),
    )(x, mo, w_post, w_pre2)


def post_bwd(dout, dh, x1, mo, w_post, w_pre2, ts):
    S = x1.shape[0]

    def body(dout_ref, dh_ref, x1_ref, mo_ref, wp_ref, w2_ref, dx1_ref, dmo_ref, gw_ref):
        i = pl.program_id(0)
        x1 = x1_ref[...]
        r1 = _rms(x1)
        dh = dh_ref[...].astype(f32)
        dx1 = dout_ref[...] + _rms_bwd(x1, r1, dh * w2_ref[...])
        dx1_ref[...] = dx1
        mv = mo_ref[...]
        rm = _rms(mv)
        dmo_ref[...] = _rms_bwd(mv, rm, dx1 * wp_ref[...]).astype(bf16)

        @pl.when(i == 0)
        def _():
            gw_ref[...] = jnp.zeros_like(gw_ref)

        gw_ref[0:1, :] += jnp.sum(dh * x1 * r1, axis=0, keepdims=True)
        gw_ref[1:2, :] += jnp.sum(dx1 * mv * rm, axis=0, keepdims=True)

    row = pl.BlockSpec((ts, D), lambda i: (i, 0))
    par = pl.BlockSpec((1, D), lambda i: (0, 0))
    return pl.pallas_call(
        body, name="post_bwd", grid=(S // ts,),
        in_specs=[row, row, row, row, par, par],
        out_specs=[row, row, pl.BlockSpec((8, D), lambda i: (0, 0))],
        out_shape=[jax.ShapeDtypeStruct((S, D), f32), jax.ShapeDtypeStruct((S, D), bf16),
                   jax.ShapeDtypeStruct((8, D), f32)],
        compiler_params=_cp("arbitrary"),
    )(dout, dh, x1, mo, w_post, w_pre2)


def loss_head(x1, ff, target, w, ts):
    S = x1.shape[0]

    def body(x1_ref, ff_ref, t_ref, w_ref, loss_ref, dout_ref, dff_ref, gw_ref):
        i = pl.program_id(0)
        fv = ff_ref[...]
        r = _rms(fv)
        n = fv * r
        e = x1_ref[...] + n * w_ref[...] - t_ref[...]
        part = jnp.sum(jnp.sum(e * e, axis=1, keepdims=True), axis=0, keepdims=True) * (0.5 / D)
        dout = e * (1.0 / D)
        dout_ref[...] = dout
        dff_ref[...] = _rms_bwd(fv, r, dout * w_ref[...]).astype(bf16)

        @pl.when(i == 0)
        def _():
            gw_ref[...] = jnp.zeros_like(gw_ref)
            loss_ref[...] = jnp.zeros_like(loss_ref)

        gw_ref[0:1, :] += jnp.sum(dout * n, axis=0, keepdims=True)
        loss_ref[...] += jnp.broadcast_to(part, loss_ref.shape)

    row = pl.BlockSpec((ts, D), lambda i: (i, 0))
    return pl.pallas_call(
        body, name="loss_head", grid=(S // ts,),
        in_specs=[row, row, row, pl.BlockSpec((1, D), lambda i: (0, 0))],
        out_specs=[pl.BlockSpec((8, LANES), lambda i: (0, 0)), row, row, pl.BlockSpec((8, D), lambda i: (0, 0))],
        out_shape=[jax.ShapeDtypeStruct((8, LANES), f32), jax.ShapeDtypeStruct((S, D), f32),
                   jax.ShapeDtypeStruct((S, D), bf16), jax.ShapeDtypeStruct((8, D), f32)],
        compiler_params=_cp("arbitrary"),
    )(x1, ff, target, w)


def _conv_tile(scr_ref, cur, prev8, first, w_ref, b_ref, K, ts):
    scr_ref[0:8, :] = jnp.where(first, 0.0, prev8.astype(f32))
    scr_ref[8:8 + ts, :] = cur.astype(f32)
    acc = jnp.broadcast_to(b_ref[...], (ts, cur.shape[1]))
    for k in range(K):
        acc = acc + w_ref[k:k + 1, :] * scr_ref[pl.ds(8 - (K - 1) + k, ts), :]
    return acc


def _prev8_map(ts, cb):
    return lambda i, j: (jnp.maximum(i * (ts // 8) - 1, 0), cb + j)


def ssdconv_fwd(proj, w8, b, ts):
    S = proj.shape[0]
    bw = 1024
    cb = C_XBC // bw

    def body(cur_ref, prev_ref, w_ref, b_ref, o_ref, scr_ref):
        c = _conv_tile(scr_ref, cur_ref[...], prev_ref[...], pl.program_id(0) == 0, w_ref, b_ref, SSD_K, ts)
        o_ref[...] = (c * _sigmoid(c)).astype(bf16)

    return pl.pallas_call(
        body, name="ssdconv_fwd", grid=(S // ts, CONVD // bw),
        in_specs=[pl.BlockSpec((ts, bw), lambda i, j: (i, cb + j)),
                  pl.BlockSpec((8, bw), _prev8_map(ts, cb)),
                  pl.BlockSpec((8, bw), lambda i, j: (0, j)),
                  pl.BlockSpec((1, bw), lambda i, j: (0, j))],
        out_specs=pl.BlockSpec((ts, bw), lambda i, j: (i, j)),
        out_shape=jax.ShapeDtypeStruct((S, CONVD), bf16),
        scratch_shapes=[pltpu.VMEM((ts + 8, bw), f32)],
        compiler_params=_cp("parallel", "parallel"),
    )(proj, proj, w8, b)


def ssdconv_bwd_act(dxbc, proj, w8, b, ts):
    S = proj.shape[0]
    bw = 1024
    cb = C_XBC // bw

    def body(d_ref, cur_ref, prev_ref, w_ref, b_ref, o_ref, scr_ref):
        c = _conv_tile(scr_ref, cur_ref[...], prev_ref[...], pl.program_id(0) == 0, w_ref, b_ref, SSD_K, ts)
        s = _sigmoid(c)
        o_ref[...] = (d_ref[...].astype(f32) * s * (1.0 + c * (1.0 - s))).astype(bf16)

    return pl.pallas_call(
        body, name="ssdconv_bwd_act", grid=(S // ts, CONVD // bw),
        in_specs=[pl.BlockSpec((ts, bw), lambda i, j: (i, j)),
                  pl.BlockSpec((ts, bw), lambda i, j: (i, cb + j)),
                  pl.BlockSpec((8, bw), _prev8_map(ts, cb)),
                  pl.BlockSpec((8, bw), lambda i, j: (0, j)),
                  pl.BlockSpec((1, bw), lambda i, j: (0, j))],
        out_specs=pl.BlockSpec((ts, bw), lambda i, j: (i, j)),
        out_shape=jax.ShapeDtypeStruct((S, CONVD), bf16),
        scratch_shapes=[pltpu.VMEM((ts + 8, bw), f32)],
        compiler_params=_cp("parallel", "parallel"),
    )(dxbc, proj, proj, w8, b)


def _gelu_tanh(x):
    c = 0.7978845608028654
    t = jnp.tanh(c * (x + 0.044715 * x * x * x))
    return 0.5 * x * (1.0 + t), t


def ffnact_fwd(up, w8, b, ts):
    S = up.shape[0]
    bw = FF // 2
    nb = FF // bw

    def body(g_ref, gp_ref, v_ref, vp_ref, wg_ref, wv_ref, bg_ref, bv_ref, o_ref, scr_ref):
        first = pl.program_id(0) == 0
        g = _conv_tile(scr_ref, g_ref[...], gp_ref[...], first, wg_ref, bg_ref, FFN_K, ts)
        v = _conv_tile(scr_ref, v_ref[...], vp_ref[...], first, wv_ref, bv_ref, FFN_K, ts)
        o_ref[...] = (_gelu_tanh(g)[0] * v).astype(bf16)

    return pl.pallas_call(
        body, name="ffnact_fwd", grid=(S // ts, nb),
        in_specs=[pl.BlockSpec((ts, bw), lambda i, j: (i, j)), pl.BlockSpec((8, bw), _prev8_map(ts, 0)),
                  pl.BlockSpec((ts, bw), lambda i, j: (i, nb + j)), pl.BlockSpec((8, bw), _prev8_map(ts, nb)),
                  pl.BlockSpec((8, bw), lambda i, j: (0, j)), pl.BlockSpec((8, bw), lambda i, j: (0, nb + j)),
                  pl.BlockSpec((1, bw), lambda i, j: (0, j)), pl.BlockSpec((1, bw), lambda i, j: (0, nb + j))],
        out_specs=pl.BlockSpec((ts, bw), lambda i, j: (i, j)),
        out_shape=jax.ShapeDtypeStruct((S, FF), bf16),
        scratch_shapes=[pltpu.VMEM((ts + 8, bw), f32)],
        compiler_params=_cp("parallel", "parallel"),
    )(up, up, up, up, w8, w8, b, b)


def ffnact_bwd(dact, up, w8, b, ts):
    S = up.shape[0]
    bw = FF // 2
    nb = FF // bw

    def body(d_ref, g_ref, gp_ref, v_ref, vp_ref, wg_ref, wv_ref, bg_ref, bv_ref, dg_ref, dv_ref, scr_ref):
        first = pl.program_id(0) == 0
        g = _conv_tile(scr_ref, g_ref[...], gp_ref[...], first, wg_ref, bg_ref, FFN_K, ts)
        v = _conv_tile(scr_ref, v_ref[...], vp_ref[...], first, wv_ref, bv_ref, FFN_K, ts)
        d = d_ref[...].astype(f32)
        ge, t = _gelu_tanh(g)
        c = 0.7978845608028654
        dgelu = 0.5 * (1.0 + t) + 0.5 * g * (1.0 - t * t) * c * (1.0 + 3.0 * 0.044715 * g * g)
        dg_ref[...] = (d * v * dgelu).astype(bf16)
        dv_ref[...] = (d * ge).astype(bf16)

    blk = pl.BlockSpec((ts, bw), lambda i, j: (i, j))
    return pl.pallas_call(
        body, name="ffnact_bwd", grid=(S // ts, nb),
        in_specs=[blk, blk, pl.BlockSpec((8, bw), _prev8_map(ts, 0)),
                  pl.BlockSpec((ts, bw), lambda i, j: (i, nb + j)), pl.BlockSpec((8, bw), _prev8_map(ts, nb)),
                  pl.BlockSpec((8, bw), lambda i, j: (0, j)), pl.BlockSpec((8, bw), lambda i, j: (0, nb + j)),
                  pl.BlockSpec((1, bw), lambda i, j: (0, j)), pl.BlockSpec((1, bw), lambda i, j: (0, nb + j))],
        out_specs=[blk, blk],
        out_shape=[jax.ShapeDtypeStruct((S, FF), bf16), jax.ShapeDtypeStruct((S, FF), bf16)],
        scratch_shapes=[pltpu.VMEM((ts + 8, bw), f32)],
        compiler_params=_cp("parallel", "parallel"),
    )(dact, up, up, up, up, w8, w8, b, b)


def dwconv_bwd(dy, x, xcb, w8, K, bw, ts, name):
    S, C = dy.shape
    nr = S // ts

    def body(dy_ref, dyn_ref, x_ref, xp_ref, w_ref, dx_ref, dw_ref, sx_ref, sd_ref):
        i = pl.program_id(1)
        first = i == 0
        dyv = dy_ref[...].astype(f32)
        sd_ref[0:ts, :] = dyv
        sd_ref[ts:ts + 8, :] = jnp.where(i == nr - 1, 0.0, dyn_ref[...].astype(f32))
        sx_ref[0:8, :] = jnp.where(first, 0.0, xp_ref[...].astype(f32))
        sx_ref[8:8 + ts, :] = x_ref[...].astype(f32)
        acc = jnp.zeros((ts, bw), f32)
        for k in range(K):
            acc = acc + w_ref[k:k + 1, :] * sd_ref[pl.ds(K - 1 - k, ts), :]
        dx_ref[...] = acc.astype(bf16)

        @pl.when(first)
        def _():
            dw_ref[...] = jnp.zeros_like(dw_ref)

        for k in range(K):
            dw_ref[k:k + 1, :] += jnp.sum(dyv * sx_ref[pl.ds(8 - (K - 1) + k, ts), :], axis=0, keepdims=True)
        dw_ref[7:8, :] += jnp.sum(dyv, axis=0, keepdims=True)

    return pl.pallas_call(
        body, name=name, grid=(C // bw, nr),
        in_specs=[pl.BlockSpec((ts, bw), lambda j, i: (i, j)),
                  pl.BlockSpec((8, bw), lambda j, i: (jnp.minimum((i + 1) * (ts // 8), S // 8 - 1), j)),
                  pl.BlockSpec((ts, bw), lambda j, i: (i, xcb + j)),
                  pl.BlockSpec((8, bw), lambda j, i: (jnp.maximum(i * (ts // 8) - 1, 0), xcb + j)),
                  pl.BlockSpec((8, bw), lambda j, i: (0, j))],
        out_specs=[pl.BlockSpec((ts, bw), lambda j, i: (i, j)), pl.BlockSpec((8, bw), lambda j, i: (0, j))],
        out_shape=[jax.ShapeDtypeStruct((S, C), bf16), jax.ShapeDtypeStruct((8, C), f32)],
        scratch_shapes=[pltpu.VMEM((ts + 8, bw), f32), pltpu.VMEM((ts + 8, bw), f32)],
        compiler_params=_cp("parallel", "arbitrary"),
    )(dy, dy, x, x, w8)


def gnorm_fwd(y, proj, w, ts):
    S = y.shape[0]

    def body(y_ref, z_ref, w_ref, o_ref):
        z = z_ref[...].astype(f32)
        g = y_ref[...].astype(f32) * z * _sigmoid(z)
        for k in range(NG):
            gk = g[:, k * GW:(k + 1) * GW]
            o_ref[:, k * GW:(k + 1) * GW] = (gk * _rms(gk) * w_ref[:, k * GW:(k + 1) * GW]).astype(bf16)

    row = pl.BlockSpec((ts, DI), lambda i: (i, 0))
    return pl.pallas_call(
        body, name="gnorm_fwd", grid=(S // ts,),
        in_specs=[row, row, pl.BlockSpec((1, DI), lambda i: (0, 0))],
        out_specs=row, out_shape=jax.ShapeDtypeStruct((S, DI), bf16),
        compiler_params=_cp("parallel"),
    )(y, proj, w)


def gnorm_bwd(dyn, y, proj, w, ts):
    S = y.shape[0]

    def body(d_ref, y_ref, z_ref, w_ref, dy_ref, dz_ref, gw_ref):
        i = pl.program_id(0)
        z = z_ref[...].astype(f32)
        yv = y_ref[...].astype(f32)
        s = _sigmoid(z)
        sz = z * s
        g = yv * sz
        d = d_ref[...].astype(f32)

        @pl.when(i == 0)
        def _():
            gw_ref[...] = jnp.zeros_like(gw_ref)

        for k in range(NG):
            sl = slice(k * GW, (k + 1) * GW)
            gk = g[:, sl]
            r = _rms(gk)
            dk = d[:, sl]
            gw_ref[0:1, sl] += jnp.sum(dk * gk * r, axis=0, keepdims=True)
            dg = _rms_bwd(gk, r, dk * w_ref[:, sl])
            dy_ref[:, sl] = (dg * sz[:, sl]).astype(bf16)
            dz_ref[:, sl] = (dg * yv[:, sl] * s[:, sl] * (1.0 + z[:, sl] * (1.0 - s[:, sl]))).astype(bf16)

    row = pl.BlockSpec((ts, DI), lambda i: (i, 0))
    return pl.pallas_call(
        body, name="gnorm_bwd", grid=(S // ts,),
        in_specs=[row, row, row, pl.BlockSpec((1, DI), lambda i: (0, 0))],
        out_specs=[row, row, pl.BlockSpec((8, DI), lambda i: (0, 0))],
        out_shape=[jax.ShapeDtypeStruct((S, DI), bf16), jax.ShapeDtypeStruct((S, DI), bf16),
                   jax.ShapeDtypeStruct((8, DI), f32)],
        compiler_params=_cp("arbitrary"),
    )(dyn, y, proj, w)


def merge_fwd(proj, ys, ya, ts):
    S = ys.shape[0]

    def body(gs_ref, ga_ref, ys_ref, ya_ref, o_ref):
        o_ref[...] = (_sigmoid(gs_ref[...].astype(f32)) * ys_ref[...].astype(f32)
                      + _sigmoid(ga_ref[...].astype(f32)) * ya_ref[...].astype(f32)).astype(bf16)

    row = pl.BlockSpec((ts, D), lambda i: (i, 0))
    return pl.pallas_call(
        body, name="merge_fwd", grid=(S // ts,),
        in_specs=[pl.BlockSpec((ts, D), lambda i: (i, C_GS // D)), pl.BlockSpec((ts, D), lambda i: (i, C_GA // D)), row, row],
        out_specs=row, out_shape=jax.ShapeDtypeStruct((S, D), bf16),
        compiler_params=_cp("parallel"),
    )(proj, proj, ys, ya)


def merge_bwd(dm, proj, ys, ya, ts):
    S = ys.shape[0]

    def body(d_ref, gs_ref, ga_ref, ys_ref, ya_ref, dys_ref, dya_ref, dgs_ref, dga_ref):
        d = d_ref[...].astype(f32)
        ss = _sigmoid(gs_ref[...].astype(f32))
        sa = _sigmoid(ga_ref[...].astype(f32))
        dys_ref[...] = (d * ss).astype(bf16)
        dya_ref[...] = (d * sa).astype(bf16)
        dgs_ref[...] = (d * ys_ref[...].astype(f32) * ss * (1.0 - ss)).astype(bf16)
        dga_ref[...] = (d * ya_ref[...].astype(f32) * sa * (1.0 - sa)).astype(bf16)

    row = pl.BlockSpec((ts, D), lambda i: (i, 0))
    o = jax.ShapeDtypeStruct((S, D), bf16)
    return pl.pallas_call(
        body, name="merge_bwd", grid=(S // ts,),
        in_specs=[row, pl.BlockSpec((ts, D), lambda i: (i, C_GS // D)), pl.BlockSpec((ts, D), lambda i: (i, C_GA // D)), row, row],
        out_specs=[row, row, row, row], out_shape=[o, o, o, o],
        compiler_params=_cp("parallel"),
    )(dm, proj, proj, ys, ya)


def _ssd_consts():
    h = lax.broadcasted_iota(jnp.int32, (LANES, DI), 0)
    c = lax.broadcasted_iota(jnp.int32, (LANES, DI), 1)
    expand = (c // HD == h).astype(bf16)
    r = lax.broadcasted_iota(jnp.int32, (CH, CH), 0)
    cc = lax.broadcasted_iota(jnp.int32, (CH, CH), 1)
    tril = (cc <= r).astype(bf16)
    triu = (cc >= r).astype(bf16)
    return expand, expand.T, tril, triu


def _ssd_common(xbc_ref, dtr_ref, bias_ref, alog_ref, expand_ref, tril_ref):
    dtr = dtr_ref[...] + bias_ref[...]
    dt = jnp.maximum(dtr, 0.0) + jnp.log1p(jnp.exp(-jnp.abs(dtr)))
    a = -jnp.exp(alog_ref[...])
    acs = _dot3_left(tril_ref[...], dt * a)
    acsx = _dot3_right(acs, expand_ref[...])
    dtx = _dot3_right(dt, expand_ref[...])
    x = xbc_ref[:, 0:DI].astype(f32)
    xdt = x * dtx
    e = jnp.exp(acsx)
    dsx = jnp.exp(acsx[CH - 1:CH, :] - acsx)
    return dtr, dt, a, acs, dtx, x, xdt, e, dsx


def _ssd_lmat(acs, acs_t, hh, causal):
    seg = acs[:, hh:hh + 1] - acs_t[hh:hh + 1, :]
    return jnp.where(causal, jnp.exp(jnp.minimum(seg, 0.0)), 0.0)


def ssd_fwd(xbc, dtr, bias, alog, dx_row):
    S = xbc.shape[0]
    nc = S // CH
    expand, _, tril, _ = _ssd_consts()

    def body(xbc_ref, dtr_ref, bias_ref, alog_ref, dxr_ref, expand_ref, tril_ref, y_ref, hp_ref, h_ref, yd_ref):
        c = pl.program_id(0)

        @pl.when(c == 0)
        def _():
            h_ref[...] = jnp.zeros_like(h_ref)

        _, _, _, acs, _, x, xdt, e, dsx = _ssd_common(xbc_ref, dtr_ref, bias_ref, alog_ref, expand_ref, tril_ref)
        acs_t = acs.T
        xb = xdt.astype(bf16)
        xd = (xdt * dsx).astype(bf16)
        causal = tril_ref[...] > 0
        for g in range(NG):
            gs = slice(g * GW, (g + 1) * GW)
            bg = xbc_ref[:, DI + g * NS:DI + (g + 1) * NS]
            cg = xbc_ref[:, DI + NG * NS + g * NS:DI + NG * NS + (g + 1) * NS]
            cb = _dot(cg, bg, "nt")
            hp = h_ref[g]
            hpb = hp.astype(bf16)
            hp_ref[0, g] = hpb
            yd_ref[:, gs] = _dot(cg, hpb) * e[:, gs]
            h_ref[g] = hp * e[CH - 1:CH, gs] + _dot(bg, xd[:, gs], "tn")
            for j in range(NH // NG):
                hh = g * (NH // NG) + j
                hs = slice(hh * HD, (hh + 1) * HD)
                m = (cb * _ssd_lmat(acs, acs_t, hh, causal)).astype(bf16)
                yd_ref[:, hs] += _dot(m, xb[:, hs])
        y_ref[...] = (yd_ref[...] + dxr_ref[...] * x).astype(bf16)

    par = lambda shape: pl.BlockSpec(shape, lambda c: (0,) * len(shape))
    return pl.pallas_call(
        body, name="ssd_fwd", grid=(nc,),
        in_specs=[pl.BlockSpec((CH, CONVD), lambda c: (c, 0)), pl.BlockSpec((CH, LANES), lambda c: (c, 0)),
                  par((1, LANES)), par((1, LANES)), par((1, DI)), par((LANES, DI)), par((CH, CH))],
        out_specs=[pl.BlockSpec((CH, DI), lambda c: (c, 0)), pl.BlockSpec((1, NG, NS, GW), lambda c: (c, 0, 0, 0))],
        out_shape=[jax.ShapeDtypeStruct((S, DI), bf16), jax.ShapeDtypeStruct((nc, NG, NS, GW), bf16)],
        scratch_shapes=[pltpu.VMEM((NG, NS, GW), f32), pltpu.VMEM((CH, DI), f32)],
        compiler_params=_cp("arbitrary"),
    )(xbc, dtr, bias, alog, dx_row, expand, tril)


def ssd_bwd(xbc, dtr, dy, hprev, bias, alog, dx_row):
    S = xbc.shape[0]
    nc = S // CH
    expand, expand_t, tril, triu = _ssd_consts()

    def body(xbc_ref, dtr_ref, dy_ref, hp_ref, bias_ref, alog_ref, dxr_ref, expand_ref, expt_ref, tril_ref, triu_ref,
             dxbc_ref, ddtr_ref, acc_ref, dh_ref, dxs_ref, t_ref, accb_ref, acca_ref, accd_ref):
        c = pl.program_id(0)

        @pl.when(c == 0)
        def _():
            dh_ref[...] = jnp.zeros_like(dh_ref)
            accb_ref[...] = jnp.zeros_like(accb_ref)
            acca_ref[...] = jnp.zeros_like(acca_ref)
            accd_ref[...] = jnp.zeros_like(accd_ref)

        dtr, dt, a, acs, dtx, x, xdt, e, dsx = _ssd_common(xbc_ref, dtr_ref, bias_ref, alog_ref, expand_ref, tril_ref)
        acs_t = acs.T
        xb = xdt.astype(bf16)
        xdf = xdt * dsx
        xd = xdf.astype(bf16)
        dyv = dy_ref[...].astype(f32)
        dyb = dy_ref[...]
        dye = (dyv * e).astype(bf16)
        causal = tril_ref[...] > 0
        lane = lax.broadcasted_iota(jnp.int32, (CH, LANES), 1)
        subl = lax.broadcasted_iota(jnp.int32, (LANES, CH), 0)
        ccol = jnp.zeros((CH, LANES), f32)
        rrow = jnp.zeros((LANES, CH), f32)
        last_row = lax.broadcasted_iota(jnp.int32, (CH, 1), 0) == CH - 1
        for g in range(NG):
            gs = slice(g * GW, (g + 1) * GW)
            bsl = slice(DI + g * NS, DI + (g + 1) * NS)
            csl = slice(DI + NG * NS + g * NS, DI + NG * NS + (g + 1) * NS)
            bg = xbc_ref[:, bsl]
            cg = xbc_ref[:, csl]
            cb = _dot(cg, bg, "nt")
            hpb = hp_ref[0, g]
            dhn = dh_ref[g]
            dhnb = dhn.astype(bf16)
            yoff = _dot(cg, hpb) * e[:, gs]
            dxd = _dot(bg, dhnb)
            t2 = dxd * xdf[:, gs]
            t3 = jnp.sum(dhn * hpb.astype(f32), axis=0, keepdims=True) * e[CH - 1:CH, gs]
            t_ref[:, gs] = dyv[:, gs] * yoff - t2 + jnp.where(last_row, jnp.sum(t2, axis=0, keepdims=True) + t3, 0.0)
            dxs_ref[:, gs] = dxd * dsx[:, gs]
            dcg = _dot(dye[:, gs], hpb, "nt")
            dbg = _dot(xd[:, gs], dhnb, "nt")
            dh_ref[g] = dhn * e[CH - 1:CH, gs] + _dot(cg, dye[:, gs], "tn")
            dcb = jnp.zeros((CH, CH), f32)
            for j in range(NH // NG):
                hh = g * (NH // NG) + j
                hs = slice(hh * HD, (hh + 1) * HD)
                lm = _ssd_lmat(acs, acs_t, hh, causal)
                m = cb * lm
                dm = _dot(dyb[:, hs], xb[:, hs], "nt")
                gm = dm * m
                ccol = ccol + jnp.sum(gm, axis=1, keepdims=True) * (lane == hh).astype(f32)
                rrow = rrow + jnp.sum(gm, axis=0, keepdims=True) * (subl == hh).astype(f32)
                dcb = dcb + dm * lm
                dxs_ref[:, hs] += _dot(m.astype(bf16), dyb[:, hs], "tn")
            dcbb = dcb.astype(bf16)
            dxbc_ref[:, csl] = (dcg + _dot(dcbb, bg)).astype(bf16)
            dxbc_ref[:, bsl] = (dbg + _dot(dcbb, cg, "tn")).astype(bf16)
        dxf = dxs_ref[...]
        dxbc_ref[:, 0:DI] = (dxf * dtx + dxr_ref[...] * dyv).astype(bf16)
        expt = expt_ref[...]
        dacs = ccol - rrow.T + _dot2_right(t_ref[...], expt)
        dadt = _dot3_left(triu_ref[...], dacs)
        ddt = _dot2_right(dxf * x, expt) + dadt * a
        ddtr = ddt * _sigmoid(dtr)
        ddtr_ref[...] = ddtr
        accb_ref[...] += ddtr
        acca_ref[...] += dadt * dt
        accd_ref[...] += _dot2_right(dyv * x, expt)

        @pl.when(c == nc - 1)
        def _():
            acc_ref[...] = jnp.zeros_like(acc_ref)
            acc_ref[0:1, :] = jnp.sum(accb_ref[...], axis=0, keepdims=True)
            acc_ref[1:2, :] = jnp.sum(acca_ref[...], axis=0, keepdims=True) * a
            acc_ref[2:3, :] = jnp.sum(accd_ref[...], axis=0, keepdims=True)

    par = lambda shape: pl.BlockSpec(shape, lambda c: (0,) * len(shape))
    rev = lambda c: (nc - 1 - c, 0)
    return pl.pallas_call(
        body, name="ssd_bwd", grid=(nc,),
        in_specs=[pl.BlockSpec((CH, CONVD), rev), pl.BlockSpec((CH, LANES), rev), pl.BlockSpec((CH, DI), rev),
                  pl.BlockSpec((1, NG, NS, GW), lambda c: (nc - 1 - c, 0, 0, 0)),
                  par((1, LANES)), par((1, LANES)), par((1, DI)), par((LANES, DI)), par((DI, LANES)),
                  par((CH, CH)), par((CH, CH))],
        out_specs=[pl.BlockSpec((CH, CONVD), rev), pl.BlockSpec((CH, LANES), rev), par((8, LANES))],
        out_shape=[jax.ShapeDtypeStruct((S, CONVD), bf16), jax.ShapeDtypeStruct((S, LANES), f32),
                   jax.ShapeDtypeStruct((8, LANES), f32)],
        scratch_shapes=[pltpu.VMEM((NG, NS, GW), f32), pltpu.VMEM((CH, DI), f32), pltpu.VMEM((CH, DI), f32),
                        pltpu.VMEM((CH, LANES), f32), pltpu.VMEM((CH, LANES), f32), pltpu.VMEM((CH, LANES), f32)],
        compiler_params=_cp("arbitrary"),
    )(xbc, dtr, dy, hprev, bias, alog, dx_row, expand, expand_t, tril, triu)


def rope_tables(pos_col, ts):
    S = pos_col.shape[0]
    half = AD // 2
    inv = ROPE_THETA ** (-jnp.arange(half, dtype=f32) * 2.0 / AD)
    inv_row = jnp.tile(inv, LANES // half)[None, :]

    def body(p_ref, inv_ref, cos_ref, sin_ref):
        ang = p_ref[...].astype(f32) * inv_ref[...]
        lane = lax.broadcasted_iota(jnp.int32, ang.shape, 1)
        cos_ref[...] = jnp.cos(ang)
        sin_ref[...] = jnp.where(lane % AD < half, -1.0, 1.0) * jnp.sin(ang)

    o = jax.ShapeDtypeStruct((S, LANES), f32)
    return pl.pallas_call(
        body, name="rope_tables", grid=(S // ts,),
        in_specs=[pl.BlockSpec((ts, 1), lambda i: (i, 0)), pl.BlockSpec((1, LANES), lambda i: (0, 0))],
        out_specs=[pl.BlockSpec((ts, LANES), lambda i: (i, 0))] * 2, out_shape=[o, o],
        compiler_params=_cp("parallel"),
    )(pos_col, inv_row)


def _partner(t):
    w = t.shape[1]
    lane = lax.broadcasted_iota(jnp.int32, t.shape, 1)
    return jnp.where(lane % AD < AD // 2, pltpu.roll(t, w - AD // 2, 1), pltpu.roll(t, AD // 2, 1))


def _rope(t, cos, sin):
    reps = t.shape[1] // LANES
    return t * jnp.tile(cos, (1, reps)) + _partner(t) * jnp.tile(sin, (1, reps))


def _rope_t(d, cos, sin):
    reps = d.shape[1] // LANES
    return d * jnp.tile(cos, (1, reps)) - _partner(d) * jnp.tile(sin, (1, reps))


def _attn_probs(qh, kp, kc, sink, not_first):
    r = lax.broadcasted_iota(jnp.int32, (WIN, WIN), 0)
    c = lax.broadcasted_iota(jnp.int32, (WIN, WIN), 1)
    neg = -1e30
    sp = jnp.where(jnp.logical_and(c > r, not_first), _dot(qh, kp, "nt"), neg)
    sc = jnp.where(c <= r, _dot(qh, kc, "nt"), neg)
    m = jnp.maximum(jnp.maximum(jnp.max(sp, axis=1, keepdims=True), jnp.max(sc, axis=1, keepdims=True)), sink)
    pp = jnp.exp(sp - m)
    pc = jnp.exp(sc - m)
    ps = jnp.exp(sink - m)
    inv = 1.0 / (jnp.sum(pp, axis=1, keepdims=True) + jnp.sum(pc, axis=1, keepdims=True) + ps)
    return pp * inv, pc * inv, ps * inv


def attn_fwd(proj, cos, sin, sinks):
    S = proj.shape[0]
    nb = S // WIN
    prev = lambda cb: (lambda i: (jnp.maximum(i - 1, 0), cb))

    def body(q_ref, k_ref, kp_ref, v_ref, vp_ref, cos_ref, sin_ref, cosp_ref, sinp_ref, sink_ref, o_ref):
        i = pl.program_id(0)
        q = (_rope(q_ref[...].astype(f32), cos_ref[...], sin_ref[...]) * (AD ** -0.5)).astype(bf16)
        kc = _rope(k_ref[...].astype(f32), cos_ref[...], sin_ref[...]).astype(bf16)
        kp = _rope(kp_ref[...].astype(f32), cosp_ref[...], sinp_ref[...]).astype(bf16)
        vc = v_ref[...]
        vp = vp_ref[...]
        for h in range(AH):
            g = h // REP
            ks = slice(g * AD, (g + 1) * AD)
            pp, pc, _ = _attn_probs(q[:, h * AD:(h + 1) * AD], kp[:, ks], kc[:, ks], sink_ref[0:1, h:h + 1], i > 0)
            o_ref[:, h * AD:(h + 1) * AD] = (_dot(pp.astype(bf16), vp[:, ks]) + _dot(pc.astype(bf16), vc[:, ks])).astype(bf16)

    tab = pl.BlockSpec((WIN, LANES), lambda i: (i, 0))
    tabp = pl.BlockSpec((WIN, LANES), prev(0))
    return pl.pallas_call(
        body, name="attn_fwd", grid=(nb,),
        in_specs=[pl.BlockSpec((WIN, D), lambda i: (i, C_Q // D)),
                  pl.BlockSpec((WIN, KVW), lambda i: (i, C_K // KVW)), pl.BlockSpec((WIN, KVW), prev(C_K // KVW)),
                  pl.BlockSpec((WIN, KVW), lambda i: (i, C_V // KVW)), pl.BlockSpec((WIN, KVW), prev(C_V // KVW)),
                  tab, tab, tabp, tabp, pl.BlockSpec((1, LANES), lambda i: (0, 0))],
        out_specs=pl.BlockSpec((WIN, D), lambda i: (i, 0)),
        out_shape=jax.ShapeDtypeStruct((S, D), bf16),
        compiler_params=_cp("parallel"),
    )(proj, proj, proj, proj, proj, cos, sin, cos, sin, sinks)


def attn_bwd(proj, cos, sin, sinks, dao):
    S = proj.shape[0]
    nb = S // WIN
    cur = lambda cb: (lambda i: (jnp.minimum(i, nb - 1), cb))
    prev = lambda cb: (lambda i: (jnp.maximum(i - 1, 0), cb))

    def body(q_ref, k_ref, kp_ref, v_ref, vp_ref, cos_ref, sin_ref, cosp_ref, sinp_ref, sink_ref, do_ref,
             dq_ref, dk_ref, dv_ref, ds_ref, ck_ref, cv_ref, dqs_ref, dkp_ref, dvp_ref, dkc_ref, dvc_ref, accs_ref):
        i = pl.program_id(0)

        @pl.when(i == 0)
        def _():
            ck_ref[...] = jnp.zeros_like(ck_ref)
            cv_ref[...] = jnp.zeros_like(cv_ref)
            accs_ref[...] = jnp.zeros_like(accs_ref)

        @pl.when(i == nb)
        def _():
            dkp_ref[...] = jnp.zeros_like(dkp_ref)
            dvp_ref[...] = jnp.zeros_like(dvp_ref)

        @pl.when(i < nb)
        def _():
            q = (_rope(q_ref[...].astype(f32), cos_ref[...], sin_ref[...]) * (AD ** -0.5)).astype(bf16)
            kc = _rope(k_ref[...].astype(f32), cos_ref[...], sin_ref[...]).astype(bf16)
            kp = _rope(kp_ref[...].astype(f32), cosp_ref[...], sinp_ref[...]).astype(bf16)
            vc = v_ref[...]
            vp = vp_ref[...]
            do = do_ref[...]
            lane = lax.broadcasted_iota(jnp.int32, (WIN, LANES), 1)
            accs = accs_ref[...]
            for g in range(KVH):
                ks = slice(g * AD, (g + 1) * AD)
                dkp = jnp.zeros((WIN, AD), f32)
                dkc = jnp.zeros((WIN, AD), f32)
                dvp = jnp.zeros((WIN, AD), f32)
                dvc = jnp.zeros((WIN, AD), f32)
                for r in range(REP):
                    h = g * REP + r
                    hs = slice(h * AD, (h + 1) * AD)
                    qh = q[:, hs]
                    doh = do[:, hs]
                    pp, pc, ps = _attn_probs(qh, kp[:, ks], kc[:, ks], sink_ref[0:1, h:h + 1], i > 0)
                    dpp = _dot(doh, vp[:, ks], "nt")
                    dpc = _dot(doh, vc[:, ks], "nt")
                    delta = jnp.sum(pp * dpp + pc * dpc, axis=1, keepdims=True)
                    dsp = (pp * (dpp - delta)).astype(bf16)
                    dsc = (pc * (dpc - delta)).astype(bf16)
                    accs = accs - (ps * delta) * (lane == h).astype(f32)
                    dqs_ref[:, hs] = (_dot(dsp, kp[:, ks]) + _dot(dsc, kc[:, ks])) * (AD ** -0.5)
                    dkp = dkp + _dot(dsp, qh, "tn")
                    dkc = dkc + _dot(dsc, qh, "tn")
                    dvp = dvp + _dot(pp.astype(bf16), doh, "tn")
                    dvc = dvc + _dot(pc.astype(bf16), doh, "tn")
                dkp_ref[:, ks] = dkp
                dkc_ref[:, ks] = dkc
                dvp_ref[:, ks] = dvp
                dvc_ref[:, ks] = dvc
            accs_ref[...] = accs
            dq_ref[...] = _rope_t(dqs_ref[...], cos_ref[...], sin_ref[...]).astype(bf16)

        dk_ref[...] = _rope_t(ck_ref[...] + dkp_ref[...], cosp_ref[...], sinp_ref[...]).astype(bf16)
        dv_ref[...] = (cv_ref[...] + dvp_ref[...]).astype(bf16)

        @pl.when(i < nb)
        def _():
            ck_ref[...] = dkc_ref[...]
            cv_ref[...] = dvc_ref[...]

        @pl.when(i == nb)
        def _():
            ds_ref[...] = jnp.zeros_like(ds_ref)
            ds_ref[0:1, :] = jnp.sum(accs_ref[...], axis=0, keepdims=True)

    tab = pl.BlockSpec((WIN, LANES), cur(0))
    tabp = pl.BlockSpec((WIN, LANES), prev(0))
    kv = lambda: pltpu.VMEM((WIN, KVW), f32)
    return pl.pallas_call(
        body, name="attn_bwd", grid=(nb + 1,),
        in_specs=[pl.BlockSpec((WIN, D), cur(C_Q // D)),
                  pl.BlockSpec((WIN, KVW), cur(C_K // KVW)), pl.BlockSpec((WIN, KVW), prev(C_K // KVW)),
                  pl.BlockSpec((WIN, KVW), cur(C_V // KVW)), pl.BlockSpec((WIN, KVW), prev(C_V // KVW)),
                  tab, tab, tabp, tabp, pl.BlockSpec((1, LANES), lambda i: (0, 0)),
                  pl.BlockSpec((WIN, D), cur(0))],
        out_specs=[pl.BlockSpec((WIN, D), cur(0)), pl.BlockSpec((WIN, KVW), prev(0)), pl.BlockSpec((WIN, KVW), prev(0)),
                   pl.BlockSpec((8, LANES), lambda i: (0, 0))],
        out_shape=[jax.ShapeDtypeStruct((S, D), bf16), jax.ShapeDtypeStruct((S, KVW), bf16),
                   jax.ShapeDtypeStruct((S, KVW), bf16), jax.ShapeDtypeStruct((8, LANES), f32)],
        scratch_shapes=[kv(), kv(), pltpu.VMEM((WIN, D), f32), kv(), kv(), kv(), kv(), pltpu.VMEM((WIN, LANES), f32)],
        compiler_params=_cp("arbitrary"),
    )(proj, proj, proj, proj, proj, cos, sin, cos, sin, sinks, dao)


def adamw(parts, w, m, v, tr, name):
    n, R, C = parts.shape
    c1 = 1.0 / (1.0 - ADAM_B1 ** ADAM_STEP)
    c2 = 1.0 / (1.0 - ADAM_B2 ** ADAM_STEP)

    def body(p_ref, w_ref, m_ref, v_ref, g_ref, d_ref, nm_ref, nv_ref):
        g = p_ref[0].astype(f32)
        for k in range(1, n):
            g = g + p_ref[k].astype(f32)
        nm = ADAM_B1 * m_ref[...] + (1.0 - ADAM_B1) * g
        nv = ADAM_B2 * v_ref[...] + (1.0 - ADAM_B2) * (g * g)
        g_ref[...] = g
        nm_ref[...] = nm
        nv_ref[...] = nv
        d_ref[...] = -ADAM_LR * ((nm * c1) / (jnp.sqrt(nv * c2) + ADAM_EPS) + ADAM_WD * w_ref[...])

    row = pl.BlockSpec((tr, C), lambda i: (i, 0))
    o = jax.ShapeDtypeStruct((R, C), f32)
    return pl.pallas_call(
        body, name=name, grid=(R // tr,),
        in_specs=[pl.BlockSpec((n, tr, C), lambda i: (0, i, 0)), row, row, row],
        out_specs=[row, row, row, row], out_shape=[o, o, o, o],
        compiler_params=_cp("parallel"),
    )(parts, w, m, v)


def exchange(buf, personalised, name):
    shape = buf.shape[1:] if personalised else buf.shape

    def body(buf_ref, out_ref, send_sems, recv_sems, local_sem):
        x, y, c = lax.axis_index("x"), lax.axis_index("y"), lax.axis_index("c")
        me = 4 * x + 2 * y + c

        def peer(k):
            px = (1 - x) if k & 4 else x
            py = (1 - y) if k & 2 else y
            pc = (1 - c) if k & 1 else c
            return px, py, pc

        def copy(k):
            px, py, pc = peer(k)
            src = buf_ref.at[4 * px + 2 * py + pc] if personalised else buf_ref
            return pltpu.make_async_remote_copy(
                src_ref=src, dst_ref=out_ref.at[me], send_sem=send_sems.at[k - 1], recv_sem=recv_sems.at[k - 1],
                device_id=(px, py, pc), device_id_type=pl.DeviceIdType.MESH)

        mine = pltpu.make_async_copy(buf_ref.at[me] if personalised else buf_ref, out_ref.at[me], local_sem)
        mine.start()
        copies = [copy(k) for k in range(1, N_DEV)]
        for cp in copies:
            cp.start()
        for cp in copies:
            cp.wait_recv()
        for cp in copies:
            cp.wait_send()
        mine.wait()

    return pl.pallas_call(
        body, name=name,
        in_specs=[pl.BlockSpec(memory_space=pl.ANY)],
        out_specs=pl.BlockSpec(memory_space=pl.ANY),
        out_shape=jax.ShapeDtypeStruct((N_DEV,) + tuple(shape), buf.dtype),
        scratch_shapes=[pltpu.SemaphoreType.DMA((N_DEV - 1,)), pltpu.SemaphoreType.DMA((N_DEV - 1,)),
                        pltpu.SemaphoreType.DMA],
        compiler_params=pltpu.CompilerParams(has_side_effects=True),
    )(buf)


BIG = (("w_in", 1092, 1104), ("ffn_w_up", 704, 704), ("ssd_w_out", 256, 256), ("attn_w_out", 128, 128),
       ("w_mix_out", 128, 128), ("ffn_w_down", 352, 352))
BIG_ROWS = 2688
BIG_TILE = 128


def _big_offsets():
    off, o = {}, 0
    for name, rows, pad in BIG:
        off[name] = (o, rows)
        o += pad
    return off


def pack_big(parts, dtype):
    pieces, o = [], 0
    for name, rows, pad in BIG:
        pieces.append(jnp.pad(parts[name].astype(dtype), ((0, pad - rows), (0, 0))))
        o += pad
    pieces.append(jnp.zeros((BIG_ROWS - o, D), dtype))
    return jnp.concatenate(pieces, axis=0)


def unpack_big(a):
    return {name: a[..., o:o + rows, :] for name, (o, rows) in _big_offsets().items()}


SMALL = (("norm_mix_pre_w", D), ("norm_mix_post_w", D), ("norm_ffn_pre_w", D), ("norm_ffn_post_w", D),
         ("ssd_norm_w", DI), ("ssd_conv_b", CONVD), ("ffn_conv_b", 2 * FF), ("ssd_dt_bias", NH), ("ssd_a_log", NH),
         ("ssd_d", NH), ("attn_sinks", AH), ("loss", 1), ("ssd_conv_w", SSD_K * CONVD), ("ffn_conv_w", FFN_K * 2 * FF))


def _rows_of(n):
    return -(-n // (8 * LANES)) * 8


def pack_small(vals, entries, total_rows):
    pieces, o = [], 0
    for name, n in entries:
        r = _rows_of(n)
        v = vals[name].reshape(-1).astype(f32)
        pieces.append(jnp.pad(v, (0, r * LANES - n)).reshape(r, LANES))
        o += r
    if total_rows > o:
        pieces.append(jnp.zeros((total_rows - o, LANES), f32))
    return jnp.concatenate(pieces, axis=0)


def unpack_small(a, entries):
    out, o = {}, 0
    for name, n in entries:
        r = _rows_of(n)
        out[name] = a[o:o + r].reshape(-1)[:n]
        o += r
    return out


def _round_up(n, k):
    return -(-n // k) * k


SMALL_ROWS = _round_up(sum(_rows_of(n) for _, n in SMALL), 8)


def _pad_rows8(w):
    return jnp.pad(w, ((0, 8 - w.shape[0]), (0, 0)))


def _pad_lanes(v):
    return jnp.pad(v.reshape(1, -1), ((0, 0), (0, LANES - v.size)))


def local_step(x, positions, target, W, ts=256):
    S = x.shape[0]
    f = lambda a: a.reshape(1, -1).astype(f32)
    wf = W["w_in"]
    w_in_p = jnp.concatenate([wf[:, IN_OFF[0]:IN_OFF[2]], wf[:, IN_OFF[3]:IN_OFF[4]], wf[:, IN_OFF[6]:IN_OFF[8]],
                              wf[:, IN_OFF[4]:IN_OFF[6]]], axis=1)
    w_dt = jnp.pad(wf[:, IN_OFF[2]:IN_OFF[3]], ((0, 0), (0, LANES - NH)))
    conv_w8 = _pad_rows8(W["ssd_conv_w"])
    conv_b = f(W["ssd_conv_b"])
    fconv_w8 = _pad_rows8(W["ffn_conv_w"])
    fconv_b = f(W["ffn_conv_b"])
    bias = _pad_lanes(W["ssd_dt_bias"])
    alog = _pad_lanes(W["ssd_a_log"])
    dx_row = jnp.repeat(W["ssd_d"].reshape(-1), HD).reshape(1, DI)
    sinks = _pad_lanes(W["attn_sinks"])

    u = prenorm_fwd(x, f(W["norm_mix_pre_w"]), ts)
    proj = mm(u, w_in_p, "nn", bf16, "mm_proj")
    dtr = mm(u, w_dt, "nn", f32, "mm_dt")
    xbc = ssdconv_fwd(proj, conv_w8, conv_b, ts)
    y, hprev = ssd_fwd(xbc, dtr, bias, alog, dx_row)
    yn = gnorm_fwd(y, proj, f(W["ssd_norm_w"]), ts)
    ys = mm(yn, W["ssd_w_out"], "nn", bf16, "mm_ssd_out")
    cos, sin = rope_tables(positions.reshape(S, 1), ts)
    ao = attn_fwd(proj, cos, sin, sinks)
    ya = mm(ao, W["attn_w_out"], "nn", bf16, "mm_attn_out")
    merged = merge_fwd(proj, ys, ya, ts)
    mo = mm(merged, W["w_mix_out"], "nn", f32, "mm_mix")
    x1, h = post_fwd(x, mo, f(W["norm_mix_post_w"]), f(W["norm_ffn_pre_w"]), ts)
    up = mm(h, W["ffn_w_up"], "nn", bf16, "mm_up")
    act = ffnact_fwd(up, fconv_w8, fconv_b, ts)
    ff = mm(act, W["ffn_w_down"], "nn", f32, "mm_down")
    loss, dout, dff, g_post2 = loss_head(x1, ff, target, f(W["norm_ffn_post_w"]), ts)

    G = {}
    dact = mm(dff, W["ffn_w_down"], "nt", bf16, "mm_dact")
    G["ffn_w_down"] = mm(act, dff, "tn", bf16, "mm_g_down")
    dgate, dval = ffnact_bwd(dact, up, fconv_w8, fconv_b, ts)
    dup = jnp.concatenate([dgate, dval], axis=1)
    dup_pre, g_fconv = dwconv_bwd(dup, up, 0, fconv_w8, FFN_K, FF // 2, ts, "ffnconv_bwd")
    dh = mm(dup_pre, W["ffn_w_up"], "nt", bf16, "mm_dh")
    G["ffn_w_up"] = mm(h, dup_pre, "tn", bf16, "mm_g_up")
    dx1, dmo, g_norms = post_bwd(dout, dh, x1, mo, f(W["norm_mix_post_w"]), f(W["norm_ffn_pre_w"]), ts)
    dmerged = mm(dmo, W["w_mix_out"], "nt", bf16, "mm_dmerged")
    G["w_mix_out"] = mm(merged, dmo, "tn", bf16, "mm_g_mix")
    dys, dya, dgs, dga = merge_bwd(dmerged, proj, ys, ya, ts)
    dao = mm(dya, W["attn_w_out"], "nt", bf16, "mm_dao")
    G["attn_w_out"] = mm(ao, dya, "tn", bf16, "mm_g_attn_out")
    dq, dk, dv, g_sinks = attn_bwd(proj, cos, sin, sinks, dao)
    dyn = mm(dys, W["ssd_w_out"], "nt", bf16, "mm_dyn")
    G["ssd_w_out"] = mm(yn, dys, "tn", bf16, "mm_g_ssd_out")
    dy, dz, g_gnorm = gnorm_bwd(dyn, y, proj, f(W["ssd_norm_w"]), ts)
    dxbc, ddtr, g_ssd = ssd_bwd(xbc, dtr, dy, hprev, bias, alog, dx_row)
    dconv = ssdconv_bwd_act(dxbc, proj, conv_w8, conv_b, ts)
    dxbc_pre, g_conv = dwconv_bwd(dconv, proj, C_XBC // 1024, conv_w8, SSD_K, 1024, ts, "ssdconv_bwd")
    dproj = jnp.concatenate([dz, dxbc_pre, dq, dgs, dga, dk, dv], axis=1)
    ddtr_b = ddtr.astype(bf16)
    du_a = mm(dproj, w_in_p, "nt", bf16, "mm_du")
    du_b = mm(ddtr_b, w_dt, "nt", bf16, "mm_du_dt")
    g_in_p = mm(u, dproj, "tn", bf16, "mm_g_in")
    g_dt = mm(u, ddtr_b, "tn", bf16, "mm_g_dt")
    grad_x, g_pre = prenorm_bwd(x, f(W["norm_mix_pre_w"]), du_a, du_b, dx1, ts)

    G["w_in"] = jnp.concatenate([g_in_p[:, C_Z:C_Q], g_dt[:, :NH], g_in_p[:, C_Q:C_GS], g_in_p[:, C_K:PW],
                                 g_in_p[:, C_GS:C_K]], axis=1)
    G["norm_mix_pre_w"] = g_pre[0]
    G["norm_mix_post_w"] = g_norms[1]
    G["norm_ffn_pre_w"] = g_norms[0]
    G["norm_ffn_post_w"] = g_post2[0]
    G["ssd_norm_w"] = g_gnorm[0]
    G["ssd_conv_b"] = g_conv[7]
    G["ssd_conv_w"] = g_conv[:SSD_K]
    G["ffn_conv_b"] = g_fconv[7]
    G["ffn_conv_w"] = g_fconv[:FFN_K]
    G["ssd_dt_bias"] = g_ssd[0, :NH]
    G["ssd_a_log"] = g_ssd[1, :NH]
    G["ssd_d"] = g_ssd[2, :NH]
    G["attn_sinks"] = g_sinks[0, :AH]
    return loss, grad_x, G


WEIGHTS = ('norm_mix_pre_w', 'w_in', 'ssd_conv_w', 'ssd_conv_b', 'ssd_dt_bias', 'ssd_a_log', 'ssd_d', 'ssd_norm_w',
           'ssd_w_out', 'attn_sinks', 'attn_w_out', 'w_mix_out', 'norm_mix_post_w', 'norm_ffn_pre_w', 'ffn_w_up',
           'ffn_conv_w', 'ffn_conv_b', 'ffn_w_down', 'norm_ffn_post_w')
COL_SHARDED = ("w_in", "ffn_w_up")
CONV_SHARDED = ("ssd_conv_w", "ffn_conv_w")
LOCAL_SMALL = tuple((n, s) for n, s in SMALL if n not in ("loss",) + CONV_SHARDED) + (
    ("ssd_conv_w", SSD_K * CONVD // N_DEV), ("ffn_conv_w", FFN_K * 2 * FF // N_DEV))
LOCAL_SMALL_ROWS = _round_up(sum(_rows_of(n) for _, n in LOCAL_SMALL), 8)


def kernel(x, positions, norm_mix_pre_w, w_in, ssd_conv_w, ssd_conv_b, ssd_dt_bias, ssd_a_log, ssd_d, ssd_norm_w, ssd_w_out, attn_sinks, attn_w_out, w_mix_out, norm_mix_post_w, norm_ffn_pre_w, ffn_w_up, ffn_conv_w, ffn_conv_b, ffn_w_down, norm_ffn_post_w, loss_target, m_norm_mix_pre_w, m_w_in, m_ssd_conv_w, m_ssd_conv_b, m_ssd_dt_bias, m_ssd_a_log, m_ssd_d, m_ssd_norm_w, m_ssd_w_out, m_attn_sinks, m_attn_w_out, m_w_mix_out, m_norm_mix_post_w, m_norm_ffn_pre_w, m_ffn_w_up, m_ffn_conv_w, m_ffn_conv_b, m_ffn_w_down, m_norm_ffn_post_w, v_norm_mix_pre_w, v_w_in, v_ssd_conv_w, v_ssd_conv_b, v_ssd_dt_bias, v_ssd_a_log, v_ssd_d, v_ssd_norm_w, v_ssd_w_out, v_attn_sinks, v_attn_w_out, v_w_mix_out, v_norm_mix_post_w, v_norm_ffn_pre_w, v_ffn_w_up, v_ffn_conv_w, v_ffn_conv_b, v_ffn_w_down, v_norm_ffn_post_w):
    a = locals()
    w = {n: a[n][0] for n in WEIGHTS}
    m = {n: a["m_" + n][0] for n in WEIGHTS}
    v = {n: a["v_" + n][0] for n in WEIGHTS}
    me = 4 * lax.axis_index("x") + 2 * lax.axis_index("y") + lax.axis_index("c")
    big_names = tuple(n for n, _, _ in BIG)

    def to_rows(name, t):
        return t.reshape(-1, D)

    def from_rows(name, t):
        return t.reshape(w[name].shape)

    gathered = exchange(pack_big({n: to_rows(n, w[n]) for n in big_names}, bf16), False, "gather_weights")
    gb = unpack_big(gathered)
    W = {n: w[n] for n in WEIGHTS if n not in big_names + CONV_SHARDED}
    for n in big_names:
        if n in COL_SHARDED:
            cols = w[n].shape[1]
            W[n] = gb[n].reshape(N_DEV, D, cols).transpose(1, 0, 2).reshape(D, N_DEV * cols)
        else:
            W[n] = gb[n].reshape(-1, D)
    conv_rows = _round_up(_rows_of(SSD_K * CONVD // N_DEV) + _rows_of(FFN_K * 2 * FF // N_DEV), 8)
    conv_entries = (("ssd_conv_w", SSD_K * CONVD // N_DEV), ("ffn_conv_w", FFN_K * 2 * FF // N_DEV))
    gconv = exchange(pack_small({n: w[n] for n in CONV_SHARDED}, conv_entries, conv_rows), False, "gather_conv_weights")
    for n, k in (("ssd_conv_w", SSD_K), ("ffn_conv_w", FFN_K)):
        per_dev = [unpack_small(gconv[d], conv_entries)[n].reshape(k, -1) for d in range(N_DEV)]
        W[n] = jnp.concatenate(per_dev, axis=1)

    loss_blk, grad_x, G = local_step(x[0], positions[0], loss_target[0], W)

    send = []
    for d in range(N_DEV):
        parts = {}
        for n in big_names:
            if n in COL_SHARDED:
                cols = w[n].shape[1]
                parts[n] = G[n][:, d * cols:(d + 1) * cols].reshape(-1, D)
            else:
                rows = w[n].shape[0]
                parts[n] = G[n][d * rows:(d + 1) * rows]
        send.append(pack_big(parts, bf16))
    recv = exchange(jnp.stack(send), True, "scatter_grads")

    small_vals = {n: G[n] for n, _ in SMALL if n != "loss"}
    small_vals["loss"] = loss_blk[0, 0:1]
    recv_small = exchange(pack_small(small_vals, SMALL, SMALL_ROWS), False, "gather_small_grads")

    pw = pack_big({n: to_rows(n, w[n]) for n in big_names}, f32)
    pm = pack_big({n: to_rows(n, m[n]) for n in big_names}, f32)
    pv = pack_big({n: to_rows(n, v[n]) for n in big_names}, f32)
    big_out = [unpack_big(t) for t in adamw(recv, pw, pm, pv, BIG_TILE, "adamw_big")]

    zeros_small = jnp.zeros((SMALL_ROWS, LANES), f32)
    gsum = adamw(recv_small, zeros_small, zeros_small, zeros_small, SMALL_ROWS, "sum_small")[0]
    gs = unpack_small(gsum, SMALL)
    loss = gs["loss"].reshape(())
    for n, k in (("ssd_conv_w", SSD_K), ("ffn_conv_w", FFN_K)):
        cols = w[n].shape[1]
        gs[n] = lax.dynamic_slice_in_dim(gs[n].reshape(k, -1), me * cols, cols, axis=1)
    pk = lambda d: pack_small({n: d[n] for n, _ in LOCAL_SMALL}, LOCAL_SMALL, LOCAL_SMALL_ROWS)
    small_out = [unpack_small(t, LOCAL_SMALL)
                 for t in adamw(pk(gs)[None], pk(w), pk(m), pk(v), LOCAL_SMALL_ROWS, "adamw_small")]

    outs = [loss, grad_x[None]]
    for k in range(4):
        for n in WEIGHTS:
            t = from_rows(n, big_out[k][n]) if n in big_names else small_out[k][n].reshape(w[n].shape)
            outs.append(t[None])
    return tuple(outs)
```

```python
import functools

import jax
import jax.numpy as jnp
import numpy as np
from jax import lax
from jax.experimental import pallas as pl
from jax.experimental.pallas import tpu as pltpu

f32 = jnp.float32
bf16 = jnp.bfloat16

N_DEV = 8
D = 1024
DI = 2048
NH = 32
HD = 64
NG = 4
GW = DI // NG
NS = 128
CH = 128
CONVD = DI + 2 * NG * NS
SSD_K = 4
AH = 16
AD = 64
KVH = 4
REP = AH // KVH
KVW = KVH * AD
WIN = 128
FF = 2816
FFN_K = 3
EPS = 1e-6
ROPE_THETA = 10000.0
LANES = 128

C_Z, C_XBC, C_Q, C_GS, C_GA, PM = 0, 2048, 5120, 6144, 7168, 8192
IN_SIZES = (DI, CONVD, NH, D, KVW, KVW, D, D)
IN_OFF = tuple(int(v) for v in np.cumsum((0,) + IN_SIZES))
IN_DIM = IN_OFF[-1]

ADAM_LR, ADAM_B1, ADAM_B2, ADAM_EPS, ADAM_WD, ADAM_STEP = 0.001, 0.9, 0.999, 1e-08, 0.01, 10

VMEM_LIMIT = 56 * 1024 * 1024


def _cp(*sem):
    return pltpu.CompilerParams(dimension_semantics=sem, vmem_limit_bytes=VMEM_LIMIT)


def _pick(n, cands):
    for c in cands:
        if n % c == 0:
            return c
    raise ValueError(f"no tile for {n}")


def _dot(a, b, mode="nn"):
    dims = {"nn": (((1,), (0,)), ((), ())), "nt": (((1,), (1,)), ((), ())), "tn": (((0,), (0,)), ((), ()))}[mode]
    return lax.dot_general(a, b, dims, preferred_element_type=f32)


def _split3(v):
    hi = v.astype(bf16)
    r = v - hi.astype(f32)
    mid = r.astype(bf16)
    lo = (r - mid.astype(f32)).astype(bf16)
    return hi, mid, lo


def _dot3_left(m01, v):
    hi, mid, lo = _split3(v)
    return _dot(m01, hi) + _dot(m01, mid) + _dot(m01, lo)


def _dot3_right(v, m01):
    hi, mid, lo = _split3(v)
    return _dot(hi, m01) + _dot(mid, m01) + _dot(lo, m01)


def _dot2_right(v, m01):
    hi = v.astype(bf16)
    lo = (v - hi.astype(f32)).astype(bf16)
    return _dot(hi, m01) + _dot(lo, m01)


def _sigmoid(x):
    return 1.0 / (1.0 + jnp.exp(-x))


MM_TILES = (2176, 2048, 1408, 1024, 512, 256, 128)
MM_VMEM_BUDGET = 40 * 1024 * 1024


def _mm_tiles(M, N, K, out_bytes):
    cm = [t for t in MM_TILES if M % t == 0 and t <= 2176]
    cn = [t for t in MM_TILES if N % t == 0]
    ck = [t for t in MM_TILES if K % t == 0]
    best = None
    for bm in cm[:2]:
        for bn in cn:
            for bk in ck:
                need = 4 * (bm * bk + bk * bn) + bm * bn * (4 + 2 * out_bytes)
                if need <= MM_VMEM_BUDGET:
                    score = (bm * bn * bk, bk)
                    if best is None or score > best[0]:
                        best = (score, (bm, bn, bk))
    return best[1]


def _peer(k, x, y, c):
    return ((1 - x) if k & 4 else x, (1 - y) if k & 2 else y, (1 - c) if k & 1 else c)


def _xchg_out_shapes(bufs, personalised):
    return [jax.ShapeDtypeStruct((N_DEV,) + tuple(b.shape[1:] if p else b.shape), b.dtype)
            for b, p in zip(bufs, personalised)]


def _xchg_scratch(n):
    return [pltpu.SemaphoreType.DMA((n * (N_DEV - 1),)), pltpu.SemaphoreType.DMA((n * (N_DEV - 1),)),
            pltpu.SemaphoreType.DMA((n,))]


def _xchg_copies(buf_refs, out_refs, send_sems, recv_sems, local_sems, personalised):
    x, y, c = lax.axis_index("x"), lax.axis_index("y"), lax.axis_index("c")
    me = 4 * x + 2 * y + c
    local, remote = [], []
    for b, (buf, out, pers) in enumerate(zip(buf_refs, out_refs, personalised)):
        local.append(pltpu.make_async_copy(buf.at[me] if pers else buf, out.at[me], local_sems.at[b]))
        for k in range(1, N_DEV):
            px, py, pc = _peer(k, x, y, c)
            s = b * (N_DEV - 1) + k - 1
            remote.append(pltpu.make_async_remote_copy(
                src_ref=buf.at[4 * px + 2 * py + pc] if pers else buf, dst_ref=out.at[me],
                send_sem=send_sems.at[s], recv_sem=recv_sems.at[s],
                device_id=(px, py, pc), device_id_type=pl.DeviceIdType.MESH))
    return local, remote


def _xchg_start(copies):
    local, remote = copies
    for cp in local + remote:
        cp.start()


def _xchg_wait(copies):
    local, remote = copies
    for cp in remote:
        cp.wait_recv()
    for cp in remote:
        cp.wait_send()
    for cp in local:
        cp.wait()


def exchange(bufs, personalised, name):
    n = len(bufs)

    def body(*refs):
        copies = _xchg_copies(refs[:n], refs[n:2 * n], *refs[2 * n:], personalised)
        _xchg_start(copies)
        _xchg_wait(copies)

    return pl.pallas_call(
        body, name=name,
        in_specs=[pl.BlockSpec(memory_space=pl.ANY)] * n,
        out_specs=[pl.BlockSpec(memory_space=pl.ANY)] * n,
        out_shape=_xchg_out_shapes(bufs, personalised),
        scratch_shapes=_xchg_scratch(n),
        compiler_params=pltpu.CompilerParams(has_side_effects=True),
    )(*bufs)


def mm(a, b, mode, out_dtype, name, comm=None):
    if mode == "nn":
        (M, K), (_, N) = a.shape, b.shape
    elif mode == "nt":
        (M, K), (N, _) = a.shape, b.shape
    else:
        (K, M), (_, N) = a.shape, b.shape
    bm, bn, bk = _mm_tiles(M, N, K, jnp.dtype(out_dtype).itemsize)
    gm, gn, nk = M // bm, N // bn, K // bk
    bufs, personalised = comm if comm else ((), ())
    n = len(bufs)

    def body(a_ref, b_ref, *rest):
        o_ref = rest[n]
        i, j, k = pl.program_id(0), pl.program_id(1), pl.program_id(2)
        if n:
            copies = _xchg_copies(rest[:n], rest[n + 1:2 * n + 1], *rest[2 * n + 1:2 * n + 4], personalised)

            @pl.when(jnp.logical_and(jnp.logical_and(i == 0, j == 0), k == 0))
            def _():
                _xchg_start(copies)

        p = _dot(a_ref[...], b_ref[...], mode)
        if nk == 1:
            o_ref[...] = p.astype(o_ref.dtype)
        else:
            acc_ref = rest[-1]

            @pl.when(k == 0)
            def _():
                acc_ref[...] = p

            @pl.when(k > 0)
            def _():
                acc_ref[...] += p

            @pl.when(k == nk - 1)
            def _():
                o_ref[...] = acc_ref[...].astype(o_ref.dtype)

        if n:
            @pl.when(jnp.logical_and(jnp.logical_and(i == gm - 1, j == gn - 1), k == nk - 1))
            def _():
                _xchg_wait(copies)

    if mode == "nn":
        a_spec = pl.BlockSpec((bm, bk), lambda i, j, k: (i, k))
        b_spec = pl.BlockSpec((bk, bn), lambda i, j, k: (k, j))
    elif mode == "nt":
        a_spec = pl.BlockSpec((bm, bk), lambda i, j, k: (i, k))
        b_spec = pl.BlockSpec((bn, bk), lambda i, j, k: (j, k))
    else:
        a_spec = pl.BlockSpec((bk, bm), lambda i, j, k: (k, i))
        b_spec = pl.BlockSpec((bk, bn), lambda i, j, k: (k, j))
    hbm = pl.BlockSpec(memory_space=pl.ANY)
    sem = ("arbitrary",) * 3 if n else ("parallel", "parallel", "arbitrary")
    res = pl.pallas_call(
        body, name=name, grid=(gm, gn, nk),
        in_specs=[a_spec, b_spec] + [hbm] * n,
        out_specs=[pl.BlockSpec((bm, bn), lambda i, j, k: (i, j))] + [hbm] * n,
        out_shape=[jax.ShapeDtypeStruct((M, N), out_dtype)] + _xchg_out_shapes(bufs, personalised),
        scratch_shapes=(_xchg_scratch(n) if n else []) + ([pltpu.VMEM((bm, bn), f32)] if nk > 1 else []),
        compiler_params=pltpu.CompilerParams(dimension_semantics=sem, vmem_limit_bytes=VMEM_LIMIT,
                                             has_side_effects=bool(n)),
    )(a, b, *bufs)
    return (res[0], res[1:]) if n else res[0]


def _rms(x):
    return lax.rsqrt(jnp.mean(x * x, axis=-1, keepdims=True) + EPS)


def _rms_bwd(x, r, dn):
    n = x * r
    return r * (dn - n * jnp.mean(dn * n, axis=-1, keepdims=True))


def _acc_row(ref, row, val, first):
    s = jnp.sum(val, axis=0, keepdims=True)

    @pl.when(first)
    def _():
        ref[row:row + 1, :] = s

    @pl.when(jnp.logical_not(first))
    def _():
        ref[row:row + 1, :] += s


def prenorm_fwd(x, w, ts):
    S = x.shape[0]

    def body(x_ref, w_ref, u_ref):
        xv = x_ref[...]
        u_ref[...] = (xv * _rms(xv) * w_ref[...]).astype(bf16)

    return pl.pallas_call(
        body, name="prenorm_fwd", grid=(S // ts,),
        in_specs=[pl.BlockSpec((ts, D), lambda i: (i, 0)), pl.BlockSpec((1, D), lambda i: (0, 0))],
        out_specs=pl.BlockSpec((ts, D), lambda i: (i, 0)),
        out_shape=jax.ShapeDtypeStruct((S, D), bf16),
        compiler_params=_cp("parallel"),
    )(x, w)


def prenorm_bwd(x, w, du_a, du_b, du_c, dx1, ts):
    S = x.shape[0]

    def body(x_ref, w_ref, da_ref, db_ref, dc_ref, dx1_ref, gx_ref, gw_ref):
        i = pl.program_id(0)
        xv = x_ref[...]
        r = _rms(xv)
        du = da_ref[...].astype(f32) + db_ref[...].astype(f32) + dc_ref[...].astype(f32)
        gx_ref[...] = dx1_ref[...] + _rms_bwd(xv, r, du * w_ref[...])

        @pl.when(i == 0)
        def _():
            gw_ref[...] = jnp.zeros_like(gw_ref)

        gw_ref[0:1, :] += jnp.sum(du * xv * r, axis=0, keepdims=True)

    row = pl.BlockSpec((ts, D), lambda i: (i, 0))
    return pl.pallas_call(
        body, name="prenorm_bwd", grid=(S // ts,),
        in_specs=[row, pl.BlockSpec((1, D), lambda i: (0, 0)), row, row, row, row],
        out_specs=[row, pl.BlockSpec((8, D), lambda i: (0, 0))],
        out_shape=[jax.ShapeDtypeStruct((S, D), f32), jax.ShapeDtypeStruct((8, D), f32)],
        compiler_params=_cp("arbitrary"),
    )(x, w, du_a, du_b, du_c, dx1)


def post_fwd(x, mo, w_post, w_pre2, ts):
    S = x.shape[0]

    def body(x_ref, mo_ref, wp_ref, w2_ref, x1_ref, h_ref):
        mv = mo_ref[...]
        x1 = x_ref[...] + mv * _rms(mv) * wp_ref[...]
        x1_ref[...] = x1
        h_ref[...] = (x1 * _rms(x1) * w2_ref[...]).astype(bf16)

    row = pl.BlockSpec((ts, D), lambda i: (i, 0))
    par = pl.BlockSpec((1, D), lambda i: (0, 0))
    return pl.pallas_call(
        body, name="post_fwd", grid=(S // ts,),
        in_specs=[row, row, par, par], out_specs=[row, row],
        out_shape=[jax.ShapeDtypeStruct((S, D), f32), jax.ShapeDtypeStruct((S, D), bf16)],
        compiler_params=_cp("parallel"),
    )(x, mo, w_post, w_pre2)


def post_bwd(dout, dh, x1, mo, w_post, w_pre2, ts):
    S = x1.shape[0]

    def body(dout_ref, dh_ref, x1_ref, mo_ref, wp_ref, w2_ref, dx1_ref, dmo_ref, gw_ref):
        i = pl.program_id(0)
        x1 = x1_ref[...]
        r1 = _rms(x1)
        dh = dh_ref[...].astype(f32)
        dx1 = dout_ref[...] + _rms_bwd(x1, r1, dh * w2_ref[...])
        dx1_ref[...] = dx1
        mv = mo_ref[...]
        rm = _rms(mv)
        dmo_ref[...] = _rms_bwd(mv, rm, dx1 * wp_ref[...]).astype(bf16)

        @pl.when(i == 0)
        def _():
            gw_ref[...] = jnp.zeros_like(gw_ref)

        gw_ref[0:1, :] += jnp.sum(dh * x1 * r1, axis=0, keepdims=True)
        gw_ref[1:2, :] += jnp.sum(dx1 * mv * rm, axis=0, keepdims=True)

    row = pl.BlockSpec((ts, D), lambda i: (i, 0))
    par = pl.BlockSpec((1, D), lambda i: (0, 0))
    return pl.pallas_call(
        body, name="post_bwd", grid=(S // ts,),
        in_specs=[row, row, row, row, par, par],
        out_specs=[row, row, pl.BlockSpec((8, D), lambda i: (0, 0))],
        out_shape=[jax.ShapeDtypeStruct((S, D), f32), jax.ShapeDtypeStruct((S, D), bf16),
                   jax.ShapeDtypeStruct((8, D), f32)],
        compiler_params=_cp("arbitrary"),
    )(dout, dh, x1, mo, w_post, w_pre2)


def loss_head(x1, ff, target, w, ts):
    S = x1.shape[0]

    def body(x1_ref, ff_ref, t_ref, w_ref, loss_ref, dout_ref, dff_ref, gw_ref):
        i = pl.program_id(0)
        fv = ff_ref[...]
        r = _rms(fv)
        n = fv * r
        e = x1_ref[...] + n * w_ref[...] - t_ref[...]
        part = jnp.sum(jnp.sum(e * e, axis=1, keepdims=True), axis=0, keepdims=True) * (0.5 / D)
        dout = e * (1.0 / D)
        dout_ref[...] = dout
        dff_ref[...] = _rms_bwd(fv, r, dout * w_ref[...]).astype(bf16)

        @pl.when(i == 0)
        def _():
            gw_ref[...] = jnp.zeros_like(gw_ref)
            loss_ref[...] = jnp.zeros_like(loss_ref)

        gw_ref[0:1, :] += jnp.sum(dout * n, axis=0, keepdims=True)
        loss_ref[...] += jnp.broadcast_to(part, loss_ref.shape)

    row = pl.BlockSpec((ts, D), lambda i: (i, 0))
    return pl.pallas_call(
        body, name="loss_head", grid=(S // ts,),
        in_specs=[row, row, row, pl.BlockSpec((1, D), lambda i: (0, 0))],
        out_specs=[pl.BlockSpec((8, LANES), lambda i: (0, 0)), row, row, pl.BlockSpec((8, D), lambda i: (0, 0))],
        out_shape=[jax.ShapeDtypeStruct((8, LANES), f32), jax.ShapeDtypeStruct((S, D), f32),
                   jax.ShapeDtypeStruct((S, D), bf16), jax.ShapeDtypeStruct((8, D), f32)],
        compiler_params=_cp("arbitrary"),
    )(x1, ff, target, w)


def _conv_tile(scr_ref, cur, prev8, first, w_ref, b_ref, K, ts):
    scr_ref[0:8, :] = jnp.where(first, 0.0, prev8.astype(f32))
    scr_ref[8:8 + ts, :] = cur.astype(f32)
    acc = jnp.broadcast_to(b_ref[...], (ts, cur.shape[1]))
    for k in range(K):
        acc = acc + w_ref[k:k + 1, :] * scr_ref[pl.ds(8 - (K - 1) + k, ts), :]
    return acc


def _prev8_map(ts, cb):
    return lambda i, j: (jnp.maximum(i * (ts // 8) - 1, 0), cb + j)


def ssdconv_fwd(proj, w8, b, ts):
    S = proj.shape[0]
    bw = 1024
    cb = C_XBC // bw

    def body(cur_ref, prev_ref, w_ref, b_ref, o_ref, scr_ref):
        c = _conv_tile(scr_ref, cur_ref[...], prev_ref[...], pl.program_id(0) == 0, w_ref, b_ref, SSD_K, ts)
        o_ref[...] = (c * _sigmoid(c)).astype(bf16)

    return pl.pallas_call(
        body, name="ssdconv_fwd", grid=(S // ts, CONVD // bw),
        in_specs=[pl.BlockSpec((ts, bw), lambda i, j: (i, cb + j)),
                  pl.BlockSpec((8, bw), _prev8_map(ts, cb)),
                  pl.BlockSpec((8, bw), lambda i, j: (0, j)),
                  pl.BlockSpec((1, bw), lambda i, j: (0, j))],
        out_specs=pl.BlockSpec((ts, bw), lambda i, j: (i, j)),
        out_shape=jax.ShapeDtypeStruct((S, CONVD), bf16),
        scratch_shapes=[pltpu.VMEM((ts + 8, bw), f32)],
        compiler_params=_cp("parallel", "parallel"),
    )(proj, proj, w8, b)


def ssdconv_bwd_act(dxbc, proj, w8, b, ts):
    S = proj.shape[0]
    bw = 1024
    cb = C_XBC // bw

    def body(d_ref, cur_ref, prev_ref, w_ref, b_ref, o_ref, scr_ref):
        c = _conv_tile(scr_ref, cur_ref[...], prev_ref[...], pl.program_id(0) == 0, w_ref, b_ref, SSD_K, ts)
        s = _sigmoid(c)
        o_ref[...] = (d_ref[...].astype(f32) * s * (1.0 + c * (1.0 - s))).astype(bf16)

    return pl.pallas_call(
        body, name="ssdconv_bwd_act", grid=(S // ts, CONVD // bw),
        in_specs=[pl.BlockSpec((ts, bw), lambda i, j: (i, j)),
                  pl.BlockSpec((ts, bw), lambda i, j: (i, cb + j)),
                  pl.BlockSpec((8, bw), _prev8_map(ts, cb)),
                  pl.BlockSpec((8, bw), lambda i, j: (0, j)),
                  pl.BlockSpec((1, bw), lambda i, j: (0, j))],
        out_specs=pl.BlockSpec((ts, bw), lambda i, j: (i, j)),
        out_shape=jax.ShapeDtypeStruct((S, CONVD), bf16),
        scratch_shapes=[pltpu.VMEM((ts + 8, bw), f32)],
        compiler_params=_cp("parallel", "parallel"),
    )(dxbc, proj, proj, w8, b)


def _gelu_tanh(x):
    c = 0.7978845608028654
    t = jnp.tanh(c * (x + 0.044715 * x * x * x))
    return 0.5 * x * (1.0 + t), t


def ffnact_fwd(up, w8, b, ts):
    S = up.shape[0]
    bw = FF // 2
    nb = FF // bw

    def body(g_ref, gp_ref, v_ref, vp_ref, wg_ref, wv_ref, bg_ref, bv_ref, o_ref, scr_ref):
        first = pl.program_id(0) == 0
        g = _conv_tile(scr_ref, g_ref[...], gp_ref[...], first, wg_ref, bg_ref, FFN_K, ts)
        v = _conv_tile(scr_ref, v_ref[...], vp_ref[...], first, wv_ref, bv_ref, FFN_K, ts)
        o_ref[...] = (_gelu_tanh(g)[0] * v).astype(bf16)

    return pl.pallas_call(
        body, name="ffnact_fwd", grid=(S // ts, nb),
        in_specs=[pl.BlockSpec((ts, bw), lambda i, j: (i, j)), pl.BlockSpec((8, bw), _prev8_map(ts, 0)),
                  pl.BlockSpec((ts, bw), lambda i, j: (i, nb + j)), pl.BlockSpec((8, bw), _prev8_map(ts, nb)),
                  pl.BlockSpec((8, bw), lambda i, j: (0, j)), pl.BlockSpec((8, bw), lambda i, j: (0, nb + j)),
                  pl.BlockSpec((1, bw), lambda i, j: (0, j)), pl.BlockSpec((1, bw), lambda i, j: (0, nb + j))],
        out_specs=pl.BlockSpec((ts, bw), lambda i, j: (i, j)),
        out_shape=jax.ShapeDtypeStruct((S, FF), bf16),
        scratch_shapes=[pltpu.VMEM((ts + 8, bw), f32)],
        compiler_params=_cp("parallel", "parallel"),
    )(up, up, up, up, w8, w8, b, b)


def ffnact_bwd(dact, up, w8, b, ts):
    S = up.shape[0]
    bw = FF // 2
    nb = FF // bw

    def body(d_ref, g_ref, gp_ref, v_ref, vp_ref, wg_ref, wv_ref, bg_ref, bv_ref, dg_ref, dv_ref, scr_ref):
        first = pl.program_id(0) == 0
        g = _conv_tile(scr_ref, g_ref[...], gp_ref[...], first, wg_ref, bg_ref, FFN_K, ts)
        v = _conv_tile(scr_ref, v_ref[...], vp_ref[...], first, wv_ref, bv_ref, FFN_K, ts)
        d = d_ref[...].astype(f32)
        ge, t = _gelu_tanh(g)
        c = 0.7978845608028654
        dgelu = 0.5 * (1.0 + t) + 0.5 * g * (1.0 - t * t) * c * (1.0 + 3.0 * 0.044715 * g * g)
        dg_ref[...] = (d * v * dgelu).astype(bf16)
        dv_ref[...] = (d * ge).astype(bf16)

    blk = pl.BlockSpec((ts, bw), lambda i, j: (i, j))
    return pl.pallas_call(
        body, name="ffnact_bwd", grid=(S // ts, nb),
        in_specs=[blk, blk, pl.BlockSpec((8, bw), _prev8_map(ts, 0)),
                  pl.BlockSpec((ts, bw), lambda i, j: (i, nb + j)), pl.BlockSpec((8, bw), _prev8_map(ts, nb)),
                  pl.BlockSpec((8, bw), lambda i, j: (0, j)), pl.BlockSpec((8, bw), lambda i, j: (0, nb + j)),
                  pl.BlockSpec((1, bw), lambda i, j: (0, j)), pl.BlockSpec((1, bw), lambda i, j: (0, nb + j))],
        out_specs=[blk, blk],
        out_shape=[jax.ShapeDtypeStruct((S, FF), bf16), jax.ShapeDtypeStruct((S, FF), bf16)],
        scratch_shapes=[pltpu.VMEM((ts + 8, bw), f32)],
        compiler_params=_cp("parallel", "parallel"),
    )(dact, up, up, up, up, w8, w8, b, b)


def dwconv_bwd(dy, x, xcb, w8, wcb, K, bw, ts, name, into=None, ocb=0, out_cols=None):
    S, C = dy.shape
    nr = S // ts
    out_cols = out_cols or C

    def body(dy_ref, dyn_ref, x_ref, xp_ref, w_ref, *rest):
        dx_ref, dw_ref, sx_ref, sd_ref = rest[-4:]
        i = pl.program_id(1)
        first = i == 0
        dyv = dy_ref[...].astype(f32)
        sd_ref[0:ts, :] = dyv
        sd_ref[ts:ts + 8, :] = jnp.where(i == nr - 1, 0.0, dyn_ref[...].astype(f32))
        sx_ref[0:8, :] = jnp.where(first, 0.0, xp_ref[...].astype(f32))
        sx_ref[8:8 + ts, :] = x_ref[...].astype(f32)
        acc = jnp.zeros((ts, bw), f32)
        for k in range(K):
            acc = acc + w_ref[k:k + 1, :] * sd_ref[pl.ds(K - 1 - k, ts), :]
        dx_ref[...] = acc.astype(bf16)

        @pl.when(first)
        def _():
            dw_ref[...] = jnp.zeros_like(dw_ref)

        for k in range(K):
            dw_ref[k:k + 1, :] += jnp.sum(dyv * sx_ref[pl.ds(8 - (K - 1) + k, ts), :], axis=0, keepdims=True)
        dw_ref[7:8, :] += jnp.sum(dyv, axis=0, keepdims=True)

    extra = [] if into is None else [into]
    return pl.pallas_call(
        body, name=name, grid=(C // bw, nr),
        in_specs=[pl.BlockSpec((ts, bw), lambda j, i: (i, j)),
                  pl.BlockSpec((8, bw), lambda j, i: (jnp.minimum((i + 1) * (ts // 8), S // 8 - 1), j)),
                  pl.BlockSpec((ts, bw), lambda j, i: (i, xcb + j)),
                  pl.BlockSpec((8, bw), lambda j, i: (jnp.maximum(i * (ts // 8) - 1, 0), xcb + j)),
                  pl.BlockSpec((8, bw), lambda j, i: (0, wcb + j))] + [pl.BlockSpec(memory_space=pl.ANY)] * len(extra),
        out_specs=[pl.BlockSpec((ts, bw), lambda j, i: (i, ocb + j)), pl.BlockSpec((8, bw), lambda j, i: (0, j))],
        out_shape=[jax.ShapeDtypeStruct((S, out_cols), bf16), jax.ShapeDtypeStruct((8, C), f32)],
        scratch_shapes=[pltpu.VMEM((ts + 8, bw), f32), pltpu.VMEM((ts + 8, bw), f32)],
        input_output_aliases={5: 0} if extra else {},
        compiler_params=_cp("parallel", "arbitrary"),
    )(dy, dy, x, x, w8, *extra)


def gnorm_fwd(y, proj, w, ts):
    S = y.shape[0]

    def body(y_ref, z_ref, w_ref, o_ref):
        z = z_ref[...].astype(f32)
        g = y_ref[...].astype(f32) * z * _sigmoid(z)
        for k in range(NG):
            gk = g[:, k * GW:(k + 1) * GW]
            o_ref[:, k * GW:(k + 1) * GW] = (gk * _rms(gk) * w_ref[:, k * GW:(k + 1) * GW]).astype(bf16)

    row = pl.BlockSpec((ts, DI), lambda i: (i, 0))
    return pl.pallas_call(
        body, name="gnorm_fwd", grid=(S // ts,),
        in_specs=[row, row, pl.BlockSpec((1, DI), lambda i: (0, 0))],
        out_specs=row, out_shape=jax.ShapeDtypeStruct((S, DI), bf16),
        compiler_params=_cp("parallel"),
    )(y, proj, w)


def gnorm_bwd(dyn, y, proj, w, dproj, ts):
    S = y.shape[0]

    def body(d_ref, y_ref, z_ref, w_ref, _, dy_ref, dz_ref, gw_ref):
        i = pl.program_id(0)
        z = z_ref[...].astype(f32)
        yv = y_ref[...].astype(f32)
        s = _sigmoid(z)
        sz = z * s
        g = yv * sz
        d = d_ref[...].astype(f32)

        @pl.when(i == 0)
        def _():
            gw_ref[...] = jnp.zeros_like(gw_ref)

        for k in range(NG):
            sl = slice(k * GW, (k + 1) * GW)
            gk = g[:, sl]
            r = _rms(gk)
            dk = d[:, sl]
            gw_ref[0:1, sl] += jnp.sum(dk * gk * r, axis=0, keepdims=True)
            dg = _rms_bwd(gk, r, dk * w_ref[:, sl])
            dy_ref[:, sl] = (dg * sz[:, sl]).astype(bf16)
            dz_ref[:, sl] = (dg * yv[:, sl] * s[:, sl] * (1.0 + z[:, sl] * (1.0 - s[:, sl]))).astype(bf16)

    row = pl.BlockSpec((ts, DI), lambda i: (i, 0))
    return pl.pallas_call(
        body, name="gnorm_bwd", grid=(S // ts,),
        in_specs=[row, row, row, pl.BlockSpec((1, DI), lambda i: (0, 0)), pl.BlockSpec(memory_space=pl.ANY)],
        out_specs=[row, row, pl.BlockSpec((8, DI), lambda i: (0, 0))],
        out_shape=[jax.ShapeDtypeStruct((S, DI), bf16), jax.ShapeDtypeStruct(dproj.shape, bf16),
                   jax.ShapeDtypeStruct((8, DI), f32)],
        input_output_aliases={4: 1},
        compiler_params=_cp("arbitrary"),
    )(dyn, y, proj, w, dproj)


def merge_fwd(proj, ys, ya, ts):
    S = ys.shape[0]

    def body(gs_ref, ga_ref, ys_ref, ya_ref, o_ref):
        o_ref[...] = (_sigmoid(gs_ref[...].astype(f32)) * ys_ref[...].astype(f32)
                      + _sigmoid(ga_ref[...].astype(f32)) * ya_ref[...].astype(f32)).astype(bf16)

    row = pl.BlockSpec((ts, D), lambda i: (i, 0))
    return pl.pallas_call(
        body, name="merge_fwd", grid=(S // ts,),
        in_specs=[pl.BlockSpec((ts, D), lambda i: (i, C_GS // D)), pl.BlockSpec((ts, D), lambda i: (i, C_GA // D)), row, row],
        out_specs=row, out_shape=jax.ShapeDtypeStruct((S, D), bf16),
        compiler_params=_cp("parallel"),
    )(proj, proj, ys, ya)


def merge_bwd(dm, proj, ys, ya, ts):
    S = ys.shape[0]

    def body(d_ref, gs_ref, ga_ref, ys_ref, ya_ref, dys_ref, dya_ref, dg_ref):
        d = d_ref[...].astype(f32)
        ss = _sigmoid(gs_ref[...].astype(f32))
        sa = _sigmoid(ga_ref[...].astype(f32))
        dys_ref[...] = (d * ss).astype(bf16)
        dya_ref[...] = (d * sa).astype(bf16)
        dg_ref[:, 0:D] = (d * ys_ref[...].astype(f32) * ss * (1.0 - ss)).astype(bf16)
        dg_ref[:, D:2 * D] = (d * ya_ref[...].astype(f32) * sa * (1.0 - sa)).astype(bf16)

    row = pl.BlockSpec((ts, D), lambda i: (i, 0))
    o = jax.ShapeDtypeStruct((S, D), bf16)
    return pl.pallas_call(
        body, name="merge_bwd", grid=(S // ts,),
        in_specs=[row, pl.BlockSpec((ts, D), lambda i: (i, C_GS // D)), pl.BlockSpec((ts, D), lambda i: (i, C_GA // D)), row, row],
        out_specs=[row, row, pl.BlockSpec((ts, 2 * D), lambda i: (i, C_GS // (2 * D)))],
        out_shape=[o, o, jax.ShapeDtypeStruct((S, PM), bf16)],
        compiler_params=_cp("parallel"),
    )(dm, proj, proj, ys, ya)


def _ssd_consts():
    h = lax.broadcasted_iota(jnp.int32, (LANES, DI), 0)
    c = lax.broadcasted_iota(jnp.int32, (LANES, DI), 1)
    expand = (c // HD == h).astype(bf16)
    r = lax.broadcasted_iota(jnp.int32, (CH, CH), 0)
    cc = lax.broadcasted_iota(jnp.int32, (CH, CH), 1)
    tril = (cc <= r).astype(bf16)
    triu = (cc >= r).astype(bf16)
    return expand, expand.T, tril, triu


def _ssd_common(xbc_ref, dtr_ref, bias_ref, alog_ref, expand_ref, tril_ref):
    dtr = dtr_ref[...] + bias_ref[...]
    dt = jnp.maximum(dtr, 0.0) + jnp.log1p(jnp.exp(-jnp.abs(dtr)))
    a = -jnp.exp(alog_ref[...])
    acs = _dot3_left(tril_ref[...], dt * a)
    acsx = _dot3_right(acs, expand_ref[...])
    dtx = _dot3_right(dt, expand_ref[...])
    x = xbc_ref[:, 0:DI].astype(f32)
    xdt = x * dtx
    e = jnp.exp(acsx)
    dsx = jnp.exp(acsx[CH - 1:CH, :] - acsx)
    return dtr, dt, a, acs, dtx, x, xdt, e, dsx


def _ssd_lmat(acs, acs_t, hh, causal):
    seg = acs[:, hh:hh + 1] - acs_t[hh:hh + 1, :]
    return jnp.where(causal, jnp.exp(jnp.minimum(seg, 0.0)), 0.0)


def ssd_fwd(xbc, dtr, bias, alog, dx_row):
    S = xbc.shape[0]
    nc = S // CH
    expand, _, tril, _ = _ssd_consts()

    def body(xbc_ref, dtr_ref, bias_ref, alog_ref, dxr_ref, expand_ref, tril_ref, y_ref, hp_ref, h_ref, yd_ref):
        c = pl.program_id(0)

        @pl.when(c == 0)
        def _():
            h_ref[...] = jnp.zeros_like(h_ref)

        _, _, _, acs, _, x, xdt, e, dsx = _ssd_common(xbc_ref, dtr_ref, bias_ref, alog_ref, expand_ref, tril_ref)
        acs_t = acs.T
        xb = xdt.astype(bf16)
        xd = (xdt * dsx).astype(bf16)
        causal = tril_ref[...] > 0
        for g in range(NG):
            gs = slice(g * GW, (g + 1) * GW)
            bg = xbc_ref[:, DI + g * NS:DI + (g + 1) * NS]
            cg = xbc_ref[:, DI + NG * NS + g * NS:DI + NG * NS + (g + 1) * NS]
            cb = _dot(cg, bg, "nt")
            hp = h_ref[g]
            hpb = hp.astype(bf16)
            hp_ref[0, g] = hpb
            yd_ref[:, gs] = _dot(cg, hpb) * e[:, gs]
            h_ref[g] = hp * e[CH - 1:CH, gs] + _dot(bg, xd[:, gs], "tn")
            for j in range(NH // NG):
                hh = g * (NH // NG) + j
                hs = slice(hh * HD, (hh + 1) * HD)
                m = (cb * _ssd_lmat(acs, acs_t, hh, causal)).astype(bf16)
                yd_ref[:, hs] += _dot(m, xb[:, hs])
        y_ref[...] = (yd_ref[...] + dxr_ref[...] * x).astype(bf16)

    par = lambda shape: pl.BlockSpec(shape, lambda c: (0,) * len(shape))
    return pl.pallas_call(
        body, name="ssd_fwd", grid=(nc,),
        in_specs=[pl.BlockSpec((CH, CONVD), lambda c: (c, 0)), pl.BlockSpec((CH, LANES), lambda c: (c, 0)),
                  par((1, LANES)), par((1, LANES)), par((1, DI)), par((LANES, DI)), par((CH, CH))],
        out_specs=[pl.BlockSpec((CH, DI), lambda c: (c, 0)), pl.BlockSpec((1, NG, NS, GW), lambda c: (c, 0, 0, 0))],
        out_shape=[jax.ShapeDtypeStruct((S, DI), bf16), jax.ShapeDtypeStruct((nc, NG, NS, GW), bf16)],
        scratch_shapes=[pltpu.VMEM((NG, NS, GW), f32), pltpu.VMEM((CH, DI), f32)],
        compiler_params=_cp("arbitrary"),
    )(xbc, dtr, bias, alog, dx_row, expand, tril)


def ssd_bwd(xbc, dtr, dy, hprev, bias, alog, dx_row):
    S = xbc.shape[0]
    nc = S // CH
    expand, expand_t, tril, triu = _ssd_consts()

    def body(xbc_ref, dtr_ref, dy_ref, hp_ref, bias_ref, alog_ref, dxr_ref, expand_ref, expt_ref, tril_ref, triu_ref,
             dxbc_ref, ddtr_ref, acc_ref, dh_ref, dxs_ref, t_ref, accb_ref, acca_ref, accd_ref):
        c = pl.program_id(0)

        @pl.when(c == 0)
        def _():
            dh_ref[...] = jnp.zeros_like(dh_ref)
            accb_ref[...] = jnp.zeros_like(accb_ref)
            acca_ref[...] = jnp.zeros_like(acca_ref)
            accd_ref[...] = jnp.zeros_like(accd_ref)

        dtr, dt, a, acs, dtx, x, xdt, e, dsx = _ssd_common(xbc_ref, dtr_ref, bias_ref, alog_ref, expand_ref, tril_ref)
        acs_t = acs.T
        xb = xdt.astype(bf16)
        xdf = xdt * dsx
        xd = xdf.astype(bf16)
        dyv = dy_ref[...].astype(f32)
        dyb = dy_ref[...]
        dye = (dyv * e).astype(bf16)
        causal = tril_ref[...] > 0
        lane = lax.broadcasted_iota(jnp.int32, (CH, LANES), 1)
        subl = lax.broadcasted_iota(jnp.int32, (LANES, CH), 0)
        ccol = jnp.zeros((CH, LANES), f32)
        rrow = jnp.zeros((LANES, CH), f32)
        last_row = lax.broadcasted_iota(jnp.int32, (CH, 1), 0) == CH - 1
        for g in range(NG):
            gs = slice(g * GW, (g + 1) * GW)
            bsl = slice(DI + g * NS, DI + (g + 1) * NS)
            csl = slice(DI + NG * NS + g * NS, DI + NG * NS + (g + 1) * NS)
            bg = xbc_ref[:, bsl]
            cg = xbc_ref[:, csl]
            cb = _dot(cg, bg, "nt")
            hpb = hp_ref[0, g]
            dhn = dh_ref[g]
            dhnb = dhn.astype(bf16)
            yoff = _dot(cg, hpb) * e[:, gs]
            dxd = _dot(bg, dhnb)
            t2 = dxd * xdf[:, gs]
            t3 = jnp.sum(dhn * hpb.astype(f32), axis=0, keepdims=True) * e[CH - 1:CH, gs]
            t_ref[:, gs] = dyv[:, gs] * yoff - t2 + jnp.where(last_row, jnp.sum(t2, axis=0, keepdims=True) + t3, 0.0)
            dxs_ref[:, gs] = dxd * dsx[:, gs]
            dcg = _dot(dye[:, gs], hpb, "nt")
            dbg = _dot(xd[:, gs], dhnb, "nt")
            dh_ref[g] = dhn * e[CH - 1:CH, gs] + _dot(cg, dye[:, gs], "tn")
            dcb = jnp.zeros((CH, CH), f32)
            for j in range(NH // NG):
                hh = g * (NH // NG) + j
                hs = slice(hh * HD, (hh + 1) * HD)
                lm = _ssd_lmat(acs, acs_t, hh, causal)
                m = cb * lm
                dm = _dot(dyb[:, hs], xb[:, hs], "nt")
                gm = dm * m
                ccol = ccol + jnp.sum(gm, axis=1, keepdims=True) * (lane == hh).astype(f32)
                rrow = rrow + jnp.sum(gm, axis=0, keepdims=True) * (subl == hh).astype(f32)
                dcb = dcb + dm * lm
                dxs_ref[:, hs] += _dot(m.astype(bf16), dyb[:, hs], "tn")
            dcbb = dcb.astype(bf16)
            dxbc_ref[:, csl] = (dcg + _dot(dcbb, bg)).astype(bf16)
            dxbc_ref[:, bsl] = (dbg + _dot(dcbb, cg, "tn")).astype(bf16)
        dxf = dxs_ref[...]
        dxbc_ref[:, 0:DI] = (dxf * dtx + dxr_ref[...] * dyv).astype(bf16)
        expt = expt_ref[...]
        dacs = ccol - rrow.T + _dot2_right(t_ref[...], expt)
        dadt = _dot3_left(triu_ref[...], dacs)
        ddt = _dot2_right(dxf * x, expt) + dadt * a
        ddtr = ddt * _sigmoid(dtr)
        ddtr_ref[...] = ddtr
        accb_ref[...] += ddtr
        acca_ref[...] += dadt * dt
        accd_ref[...] += _dot2_right(dyv * x, expt)

        @pl.when(c == nc - 1)
        def _():
            acc_ref[...] = jnp.zeros_like(acc_ref)
            acc_ref[0:1, :] = jnp.sum(accb_ref[...], axis=0, keepdims=True)
            acc_ref[1:2, :] = jnp.sum(acca_ref[...], axis=0, keepdims=True) * a
            acc_ref[2:3, :] = jnp.sum(accd_ref[...], axis=0, keepdims=True)

    par = lambda shape: pl.BlockSpec(shape, lambda c: (0,) * len(shape))
    rev = lambda c: (nc - 1 - c, 0)
    return pl.pallas_call(
        body, name="ssd_bwd", grid=(nc,),
        in_specs=[pl.BlockSpec((CH, CONVD), rev), pl.BlockSpec((CH, LANES), rev), pl.BlockSpec((CH, DI), rev),
                  pl.BlockSpec((1, NG, NS, GW), lambda c: (nc - 1 - c, 0, 0, 0)),
                  par((1, LANES)), par((1, LANES)), par((1, DI)), par((LANES, DI)), par((DI, LANES)),
                  par((CH, CH)), par((CH, CH))],
        out_specs=[pl.BlockSpec((CH, CONVD), rev), pl.BlockSpec((CH, LANES), rev), par((8, LANES))],
        out_shape=[jax.ShapeDtypeStruct((S, CONVD), bf16), jax.ShapeDtypeStruct((S, LANES), f32),
                   jax.ShapeDtypeStruct((8, LANES), f32)],
        scratch_shapes=[pltpu.VMEM((NG, NS, GW), f32), pltpu.VMEM((CH, DI), f32), pltpu.VMEM((CH, DI), f32),
                        pltpu.VMEM((CH, LANES), f32), pltpu.VMEM((CH, LANES), f32), pltpu.VMEM((CH, LANES), f32)],
        compiler_params=_cp("arbitrary"),
    )(xbc, dtr, dy, hprev, bias, alog, dx_row, expand, expand_t, tril, triu)


def rope_tables(pos_col, ts):
    S = pos_col.shape[0]
    half = AD // 2
    inv = ROPE_THETA ** (-jnp.arange(half, dtype=f32) * 2.0 / AD)
    inv_row = jnp.tile(inv, LANES // half)[None, :]

    def body(p_ref, inv_ref, cos_ref, sin_ref):
        ang = p_ref[...].astype(f32) * inv_ref[...]
        lane = lax.broadcasted_iota(jnp.int32, ang.shape, 1)
        cos_ref[...] = jnp.cos(ang)
        sin_ref[...] = jnp.where(lane % AD < half, -1.0, 1.0) * jnp.sin(ang)

    o = jax.ShapeDtypeStruct((S, LANES), f32)
    return pl.pallas_call(
        body, name="rope_tables", grid=(S // ts,),
        in_specs=[pl.BlockSpec((ts, 1), lambda i: (i, 0)), pl.BlockSpec((1, LANES), lambda i: (0, 0))],
        out_specs=[pl.BlockSpec((ts, LANES), lambda i: (i, 0))] * 2, out_shape=[o, o],
        compiler_params=_cp("parallel"),
    )(pos_col, inv_row)


def _partner(t):
    w = t.shape[1]
    lane = lax.broadcasted_iota(jnp.int32, t.shape, 1)
    return jnp.where(lane % AD < AD // 2, pltpu.roll(t, w - AD // 2, 1), pltpu.roll(t, AD // 2, 1))


def _rope(t, cos, sin):
    reps = t.shape[1] // LANES
    return t * jnp.tile(cos, (1, reps)) + _partner(t) * jnp.tile(sin, (1, reps))


def _rope_t(d, cos, sin):
    reps = d.shape[1] // LANES
    return d * jnp.tile(cos, (1, reps)) - _partner(d) * jnp.tile(sin, (1, reps))


def _stack(t, g):
    return jnp.concatenate([t[:, (g * REP + r) * AD:(g * REP + r + 1) * AD] for r in range(REP)], axis=0)


def _sink_col(sink_ref, g):
    return jnp.concatenate([jnp.broadcast_to(sink_ref[0:1, g * REP + r:g * REP + r + 1], (WIN, 1)) for r in range(REP)],
                           axis=0)


def _attn_probs(qh, kp, kc, sink, not_first):
    n = qh.shape[0]
    r = lax.broadcasted_iota(jnp.int32, (n, WIN), 0) % WIN
    c = lax.broadcasted_iota(jnp.int32, (n, WIN), 1)
    neg = -1e30
    sp = jnp.where(jnp.logical_and(c > r, not_first), _dot(qh, kp, "nt"), neg)
    sc = jnp.where(c <= r, _dot(qh, kc, "nt"), neg)
    m = jnp.maximum(jnp.maximum(jnp.max(sp, axis=1, keepdims=True), jnp.max(sc, axis=1, keepdims=True)), sink)
    pp = jnp.exp(sp - m)
    pc = jnp.exp(sc - m)
    ps = jnp.exp(sink - m)
    inv = 1.0 / (jnp.sum(pp, axis=1, keepdims=True) + jnp.sum(pc, axis=1, keepdims=True) + ps)
    return pp * inv, pc * inv, ps * inv


def attn_fwd(proj, kv, cos, sin, sinks):
    S = proj.shape[0]
    nb = S // WIN
    prev = lambda cb: (lambda i: (jnp.maximum(i - 1, 0), cb))

    def body(q_ref, k_ref, kp_ref, v_ref, vp_ref, cos_ref, sin_ref, cosp_ref, sinp_ref, sink_ref, o_ref):
        i = pl.program_id(0)
        q = (_rope(q_ref[...].astype(f32), cos_ref[...], sin_ref[...]) * (AD ** -0.5)).astype(bf16)
        kc = _rope(k_ref[...].astype(f32), cos_ref[...], sin_ref[...]).astype(bf16)
        kp = _rope(kp_ref[...].astype(f32), cosp_ref[...], sinp_ref[...]).astype(bf16)
        vc = v_ref[...]
        vp = vp_ref[...]
        for g in range(KVH):
            ks = slice(g * AD, (g + 1) * AD)
            pp, pc, _ = _attn_probs(_stack(q, g), kp[:, ks], kc[:, ks], _sink_col(sink_ref, g), i > 0)
            o = _dot(pp.astype(bf16), vp[:, ks]) + _dot(pc.astype(bf16), vc[:, ks])
            for r in range(REP):
                h = g * REP + r
                o_ref[:, h * AD:(h + 1) * AD] = o[r * WIN:(r + 1) * WIN].astype(bf16)

    tab = pl.BlockSpec((WIN, LANES), lambda i: (i, 0))
    tabp = pl.BlockSpec((WIN, LANES), prev(0))
    return pl.pallas_call(
        body, name="attn_fwd", grid=(nb,),
        in_specs=[pl.BlockSpec((WIN, D), lambda i: (i, C_Q // D)),
                  pl.BlockSpec((WIN, KVW), lambda i: (i, 0)), pl.BlockSpec((WIN, KVW), prev(0)),
                  pl.BlockSpec((WIN, KVW), lambda i: (i, 1)), pl.BlockSpec((WIN, KVW), prev(1)),
                  tab, tab, tabp, tabp, pl.BlockSpec((1, LANES), lambda i: (0, 0))],
        out_specs=pl.BlockSpec((WIN, D), lambda i: (i, 0)),
        out_shape=jax.ShapeDtypeStruct((S, D), bf16),
        compiler_params=_cp("parallel"),
    )(proj, kv, kv, kv, kv, cos, sin, cos, sin, sinks)


def attn_bwd(proj, kv, cos, sin, sinks, dao, dproj):
    S = proj.shape[0]
    nb = S // WIN
    cur = lambda cb: (lambda i: (jnp.minimum(i, nb - 1), cb))
    prev = lambda cb: (lambda i: (jnp.maximum(i - 1, 0), cb))

    def body(q_ref, k_ref, kp_ref, v_ref, vp_ref, cos_ref, sin_ref, cosp_ref, sinp_ref, sink_ref, do_ref, _,
             dq_ref, dkv_ref, ds_ref, ck_ref, cv_ref, dqs_ref, dkp_ref, dvp_ref, dkc_ref, dvc_ref, accs_ref):
        i = pl.program_id(0)

        @pl.when(i == 0)
        def _():
            ck_ref[...] = jnp.zeros_like(ck_ref)
            cv_ref[...] = jnp.zeros_like(cv_ref)
            accs_ref[...] = jnp.zeros_like(accs_ref)

        @pl.when(i == nb)
        def _():
            dkp_ref[...] = jnp.zeros_like(dkp_ref)
            dvp_ref[...] = jnp.zeros_like(dvp_ref)

        @pl.when(i < nb)
        def _():
            q = (_rope(q_ref[...].astype(f32), cos_ref[...], sin_ref[...]) * (AD ** -0.5)).astype(bf16)
            kc = _rope(k_ref[...].astype(f32), cos_ref[...], sin_ref[...]).astype(bf16)
            kp = _rope(kp_ref[...].astype(f32), cosp_ref[...], sinp_ref[...]).astype(bf16)
            vc = v_ref[...]
            vp = vp_ref[...]
            do = do_ref[...]
            lane = lax.broadcasted_iota(jnp.int32, (WIN, LANES), 1)
            accs = accs_ref[...]
            for g in range(KVH):
                ks = slice(g * AD, (g + 1) * AD)
                qs = _stack(q, g)
                dos = _stack(do, g)
                pp, pc, ps = _attn_probs(qs, kp[:, ks], kc[:, ks], _sink_col(sink_ref, g), i > 0)
                dpp = _dot(dos, vp[:, ks], "nt")
                dpc = _dot(dos, vc[:, ks], "nt")
                delta = jnp.sum(pp * dpp + pc * dpc, axis=1, keepdims=True)
                dsp = (pp * (dpp - delta)).astype(bf16)
                dsc = (pc * (dpc - delta)).astype(bf16)
                sd = ps * delta
                dqs = (_dot(dsp, kp[:, ks]) + _dot(dsc, kc[:, ks])) * (AD ** -0.5)
                for r in range(REP):
                    h = g * REP + r
                    accs = accs - sd[r * WIN:(r + 1) * WIN] * (lane == h).astype(f32)
                    dqs_ref[:, h * AD:(h + 1) * AD] = dqs[r * WIN:(r + 1) * WIN]
                dkp_ref[:, ks] = _dot(dsp, qs, "tn")
                dkc_ref[:, ks] = _dot(dsc, qs, "tn")
                dvp_ref[:, ks] = _dot(pp.astype(bf16), dos, "tn")
                dvc_ref[:, ks] = _dot(pc.astype(bf16), dos, "tn")
            accs_ref[...] = accs
            dq_ref[...] = _rope_t(dqs_ref[...], cos_ref[...], sin_ref[...]).astype(bf16)

        dkv_ref[:, 0:KVW] = _rope_t(ck_ref[...] + dkp_ref[...], cosp_ref[...], sinp_ref[...]).astype(bf16)
        dkv_ref[:, KVW:2 * KVW] = (cv_ref[...] + dvp_ref[...]).astype(bf16)

        @pl.when(i < nb)
        def _():
            ck_ref[...] = dkc_ref[...]
            cv_ref[...] = dvc_ref[...]

        @pl.when(i == nb)
        def _():
            ds_ref[...] = jnp.zeros_like(ds_ref)
            ds_ref[0:1, :] = jnp.sum(accs_ref[...], axis=0, keepdims=True)

    tab = pl.BlockSpec((WIN, LANES), cur(0))
    tabp = pl.BlockSpec((WIN, LANES), prev(0))
    kvs = lambda: pltpu.VMEM((WIN, KVW), f32)
    return pl.pallas_call(
        body, name="attn_bwd", grid=(nb + 1,),
        in_specs=[pl.BlockSpec((WIN, D), cur(C_Q // D)),
                  pl.BlockSpec((WIN, KVW), cur(0)), pl.BlockSpec((WIN, KVW), prev(0)),
                  pl.BlockSpec((WIN, KVW), cur(1)), pl.BlockSpec((WIN, KVW), prev(1)),
                  tab, tab, tabp, tabp, pl.BlockSpec((1, LANES), lambda i: (0, 0)),
                  pl.BlockSpec((WIN, D), cur(0)), pl.BlockSpec(memory_space=pl.ANY)],
        out_specs=[pl.BlockSpec((WIN, D), cur(C_Q // D)), pl.BlockSpec((WIN, 2 * KVW), prev(0)),
                   pl.BlockSpec((8, LANES), lambda i: (0, 0))],
        out_shape=[jax.ShapeDtypeStruct(dproj.shape, bf16), jax.ShapeDtypeStruct((S, 2 * KVW), bf16),
                   jax.ShapeDtypeStruct((8, LANES), f32)],
        scratch_shapes=[kvs(), kvs(), pltpu.VMEM((WIN, D), f32), kvs(), kvs(), kvs(), kvs(), pltpu.VMEM((WIN, LANES), f32)],
        input_output_aliases={11: 0},
        compiler_params=_cp("arbitrary"),
    )(proj, kv, kv, kv, kv, cos, sin, cos, sin, sinks, dao, dproj)


def adamw(parts, w, m, v, tr, name):
    n, R, C = parts.shape
    c1 = 1.0 / (1.0 - ADAM_B1 ** ADAM_STEP)
    c2 = 1.0 / (1.0 - ADAM_B2 ** ADAM_STEP)

    def body(p_ref, w_ref, m_ref, v_ref, g_ref, d_ref, nm_ref, nv_ref):
        g = p_ref[0].astype(f32)
        for k in range(1, n):
            g = g + p_ref[k].astype(f32)
        nm = ADAM_B1 * m_ref[...] + (1.0 - ADAM_B1) * g
        nv = ADAM_B2 * v_ref[...] + (1.0 - ADAM_B2) * (g * g)
        g_ref[...] = g
        nm_ref[...] = nm
        nv_ref[...] = nv
        d_ref[...] = -ADAM_LR * ((nm * c1) / (jnp.sqrt(nv * c2) + ADAM_EPS) + ADAM_WD * w_ref[...])

    row = pl.BlockSpec((tr, C), lambda i: (i, 0))
    o = jax.ShapeDtypeStruct((R, C), f32)
    return pl.pallas_call(
        body, name=name, grid=(R // tr,),
        in_specs=[pl.BlockSpec((n, tr, C), lambda i: (0, i, 0)), row, row, row],
        out_specs=[row, row, row, row], out_shape=[o, o, o, o],
        compiler_params=_cp("parallel"),
    )(parts, w, m, v)


SMALL_ROW = (("norm_mix_post_w", D), ("norm_ffn_pre_w", D), ("norm_ffn_post_w", D), ("ssd_norm_w", DI),
             ("ssd_conv_b", CONVD), ("ffn_conv_b", 2 * FF), ("ssd_dt_bias", NH), ("ssd_a_log", NH), ("ssd_d", NH),
             ("attn_sinks", AH), ("loss", 1))
CONV_BLOCK = 1152
SSD_CONV_COLS = CONVD // N_DEV
FFN_CONV_COLS = 2 * FF // N_DEV


def _row_offsets():
    off, o = {}, 0
    for name, n in SMALL_ROW:
        off[name] = (o, n)
        o += -(-n // LANES) * LANES
    return off, o


def adamw_small(recv_row, recv_pre, recv_conv, params):
    off, _ = _row_offsets()
    names = list(params)
    c1 = 1.0 / (1.0 - ADAM_B1 ** ADAM_STEP)
    c2 = 1.0 / (1.0 - ADAM_B2 ** ADAM_STEP)
    n = len(names)

    def grad_of(name, row_ref, pre_ref, conv_ref):
        def total(ref, rows, lo, width):
            g = ref[0, rows, lo:lo + width]
            for d in range(1, N_DEV):
                g = g + ref[d, rows, lo:lo + width]
            return g
        if name == "norm_mix_pre_w":
            return total(pre_ref, slice(0, 1), 0, D)
        if name == "ssd_conv_w":
            return total(conv_ref, slice(0, SSD_K), 0, SSD_CONV_COLS)
        if name == "ffn_conv_w":
            return total(conv_ref, slice(0, FFN_K), 3 * LANES, FFN_CONV_COLS)
        o, width = off[name]
        return total(row_ref, slice(0, 1), o, width)

    def body(row_ref, pre_ref, conv_ref, *refs):
        ins, outs = refs[:3 * n], refs[3 * n:]
        for k, name in enumerate(names):
            w_ref, m_ref, v_ref = ins[3 * k:3 * k + 3]
            g_ref, d_ref, nm_ref, nv_ref = outs[4 * k:4 * k + 4]
            g = grad_of(name, row_ref, pre_ref, conv_ref)
            nm = ADAM_B1 * m_ref[...] + (1.0 - ADAM_B1) * g
            nv = ADAM_B2 * v_ref[...] + (1.0 - ADAM_B2) * (g * g)
            g_ref[...] = g
            nm_ref[...] = nm
            nv_ref[...] = nv
            d_ref[...] = -ADAM_LR * ((nm * c1) / (jnp.sqrt(nv * c2) + ADAM_EPS) + ADAM_WD * w_ref[...])
        o, _ = off["loss"]
        loss = row_ref[0, 0:1, o:o + LANES]
        for d in range(1, N_DEV):
            loss = loss + row_ref[d, 0:1, o:o + LANES]
        outs[4 * n][...] = loss

    flat = [t for name in names for t in params[name]]
    out_shape = [jax.ShapeDtypeStruct(params[name][0].shape, f32) for name in names for _ in range(4)]
    res = pl.pallas_call(
        body, name="adamw_small",
        out_shape=out_shape + [jax.ShapeDtypeStruct((1, LANES), f32)],
        compiler_params=pltpu.CompilerParams(vmem_limit_bytes=VMEM_LIMIT),
    )(recv_row, recv_pre, recv_conv, *flat)
    return {name: res[4 * k:4 * k + 4] for k, name in enumerate(names)}, res[4 * n]


def _pad_rows8(w):
    return jnp.pad(w, ((0, 8 - w.shape[0]), (0, 0)))


def _pad_lanes(v):
    return jnp.pad(v.reshape(1, -1), ((0, 0), (0, LANES - v.size)))


WEIGHTS = ('norm_mix_pre_w', 'w_in', 'ssd_conv_w', 'ssd_conv_b', 'ssd_dt_bias', 'ssd_a_log', 'ssd_d', 'ssd_norm_w',
           'ssd_w_out', 'attn_sinks', 'attn_w_out', 'w_mix_out', 'norm_mix_post_w', 'norm_ffn_pre_w', 'ffn_w_up',
           'ffn_conv_w', 'ffn_conv_b', 'ffn_w_down', 'norm_ffn_post_w')
W_IN_ROWS = IN_DIM // N_DEV
W_IN_PAD = 1104
TS = 256


def kernel(x, positions, norm_mix_pre_w, w_in, ssd_conv_w, ssd_conv_b, ssd_dt_bias, ssd_a_log, ssd_d, ssd_norm_w, ssd_w_out, attn_sinks, attn_w_out, w_mix_out, norm_mix_post_w, norm_ffn_pre_w, ffn_w_up, ffn_conv_w, ffn_conv_b, ffn_w_down, norm_ffn_post_w, loss_target, m_norm_mix_pre_w, m_w_in, m_ssd_conv_w, m_ssd_conv_b, m_ssd_dt_bias, m_ssd_a_log, m_ssd_d, m_ssd_norm_w, m_ssd_w_out, m_attn_sinks, m_attn_w_out, m_w_mix_out, m_norm_mix_post_w, m_norm_ffn_pre_w, m_ffn_w_up, m_ffn_conv_w, m_ffn_conv_b, m_ffn_w_down, m_norm_ffn_post_w, v_norm_mix_pre_w, v_w_in, v_ssd_conv_w, v_ssd_conv_b, v_ssd_dt_bias, v_ssd_a_log, v_ssd_d, v_ssd_norm_w, v_ssd_w_out, v_attn_sinks, v_attn_w_out, v_w_mix_out, v_norm_mix_post_w, v_norm_ffn_pre_w, v_ffn_w_up, v_ffn_conv_w, v_ffn_conv_b, v_ffn_w_down, v_norm_ffn_post_w):
    a = locals()
    r2 = lambda t: t.reshape(t.shape[-2], t.shape[-1])
    w = {n: r2(a[n]) for n in WEIGHTS}
    m = {n: r2(a["m_" + n]) for n in WEIGHTS}
    v = {n: r2(a["v_" + n]) for n in WEIGHTS}
    xs, target = x[0], loss_target[0]
    S = xs.shape[0]
    ts = TS

    w_in_blk = jnp.pad(w["w_in"].T.astype(bf16), ((0, W_IN_PAD - W_IN_ROWS), (0, 0)))
    conv_blk = jnp.concatenate([_pad_rows8(w["ssd_conv_w"]), _pad_rows8(w["ffn_conv_w"]),
                                jnp.zeros((8, CONV_BLOCK - SSD_CONV_COLS - FFN_CONV_COLS), f32)], axis=1)
    g_in, g_conv = exchange([w_in_blk, conv_blk], (False, False), "gather_first")
    wt = g_in[:, :W_IN_ROWS].reshape(IN_DIM, D)
    w_main_t = jnp.concatenate([wt[IN_OFF[0]:IN_OFF[2]], wt[IN_OFF[3]:IN_OFF[4]], wt[IN_OFF[6]:IN_OFF[8]]], axis=0)
    w_kv_t = wt[IN_OFF[4]:IN_OFF[6]]
    w_dt_t = jnp.pad(wt[IN_OFF[2]:IN_OFF[3]], ((0, LANES - NH), (0, 0)))
    conv_w8 = g_conv[:, :, 0:SSD_CONV_COLS].transpose(1, 0, 2).reshape(8, CONVD)
    fconv_w8 = g_conv[:, :, SSD_CONV_COLS:SSD_CONV_COLS + FFN_CONV_COLS].transpose(1, 0, 2).reshape(8, 2 * FF)
    bias = _pad_lanes(w["ssd_dt_bias"])
    alog = _pad_lanes(w["ssd_a_log"])
    dx_row = jnp.repeat(w["ssd_d"].reshape(-1), HD).reshape(1, DI)
    sinks = _pad_lanes(w["attn_sinks"])

    u = prenorm_fwd(xs, w["norm_mix_pre_w"], ts)
    later = [w["ssd_w_out"].astype(bf16), w["attn_w_out"].astype(bf16), w["w_mix_out"].astype(bf16),
             w["ffn_w_up"].T.astype(bf16)]
    proj, (g_so, g_ao, g_mix, g_up) = mm(u, w_main_t, "nt", bf16, "mm_proj", comm=(later, (False,) * 4))
    w_ssd_out, w_attn_out, w_mix = g_so.reshape(DI, D), g_ao.reshape(D, D), g_mix.reshape(D, D)
    w_up_t = g_up.reshape(2 * FF, D)
    kv = mm(u, w_kv_t, "nt", bf16, "mm_kv")
    dtr = mm(u, w_dt_t, "nt", f32, "mm_dt")
    xbc = ssdconv_fwd(proj, conv_w8, w["ssd_conv_b"], ts)
    y, hprev = ssd_fwd(xbc, dtr, bias, alog, dx_row)
    yn = gnorm_fwd(y, proj, w["ssd_norm_w"], ts)
    ys, (g_down,) = mm(yn, w_ssd_out, "nn", bf16, "mm_ssd_out", comm=([w["ffn_w_down"].astype(bf16)], (False,)))
    w_down = g_down.reshape(FF, D)
    cos, sin = rope_tables(positions.reshape(S, 1), ts)
    ao = attn_fwd(proj, kv, cos, sin, sinks)
    ya = mm(ao, w_attn_out, "nn", bf16, "mm_attn_out")
    merged = merge_fwd(proj, ys, ya, ts)
    mo = mm(merged, w_mix, "nn", f32, "mm_mix")
    x1, h = post_fwd(xs, mo, w["norm_mix_post_w"], w["norm_ffn_pre_w"], ts)
    up = mm(h, w_up_t, "nt", bf16, "mm_up")
    act = ffnact_fwd(up, fconv_w8, w["ffn_conv_b"], ts)
    ff = mm(act, w_down, "nn", f32, "mm_down")
    loss_blk, dout, dff, g_post2 = loss_head(x1, ff, target, w["norm_ffn_post_w"], ts)

    dact = mm(dff, w_down, "nt", bf16, "mm_dact")
    gw_down = mm(act, dff, "tn", bf16, "mm_g_down")
    dgate, dval = ffnact_bwd(dact, up, fconv_w8, w["ffn_conv_b"], ts)
    hw = FF // 2
    dup_pre, g_fconv_a = dwconv_bwd(dgate, up, 0, fconv_w8, 0, FFN_K, hw, ts, "ffnconv_bwd_gate", out_cols=2 * FF)
    dup_pre, g_fconv_b = dwconv_bwd(dval, up, 2, fconv_w8, 2, FFN_K, hw, ts, "ffnconv_bwd_val", into=dup_pre, ocb=2,
                                    out_cols=2 * FF)
    g_fconv = jnp.concatenate([g_fconv_a, g_fconv_b], axis=1)
    dh, (r_down,) = mm(dup_pre, w_up_t, "nn", bf16, "mm_dh", comm=([gw_down.reshape(N_DEV, FF // N_DEV, D)], (True,)))
    gw_up_t = mm(dup_pre, h, "tn", bf16, "mm_g_up")
    dx1, dmo, g_norms = post_bwd(dout, dh, x1, mo, w["norm_mix_post_w"], w["norm_ffn_pre_w"], ts)
    dmerged = mm(dmo, w_mix, "nt", bf16, "mm_dmerged")
    gw_mix = mm(merged, dmo, "tn", bf16, "mm_g_mix")
    dys, dya, dproj = merge_bwd(dmerged, proj, ys, ya, ts)
    dao = mm(dya, w_attn_out, "nt", bf16, "mm_dao")
    gw_attn_out = mm(ao, dya, "tn", bf16, "mm_g_attn_out")
    dproj, dkv, g_sinks = attn_bwd(proj, kv, cos, sin, sinks, dao, dproj)
    dyn = mm(dys, w_ssd_out, "nt", bf16, "mm_dyn")
    gw_ssd_out = mm(yn, dys, "tn", bf16, "mm_g_ssd_out")
    dy, dproj, g_gnorm = gnorm_bwd(dyn, y, proj, w["ssd_norm_w"], dproj, ts)
    dxbc, ddtr, g_ssd = ssd_bwd(xbc, dtr, dy, hprev, bias, alog, dx_row)
    dconv = ssdconv_bwd_act(dxbc, proj, conv_w8, w["ssd_conv_b"], ts)
    dproj, g_conv_w = dwconv_bwd(dconv, proj, C_XBC // 1024, conv_w8, 0, SSD_K, 1024, ts, "ssdconv_bwd", into=dproj,
                                 ocb=C_XBC // 1024, out_cols=PM)
    ddtr_b = ddtr.astype(bf16)
    g_main_t = mm(dproj, u, "tn", bf16, "mm_g_in")
    g_kv_t = mm(dkv, u, "tn", bf16, "mm_g_kv")
    g_dt_t = mm(ddtr_b, u, "tn", bf16, "mm_g_dt")
    g_wt = jnp.concatenate([g_main_t[C_Z:C_Q], g_dt_t[:NH], g_main_t[C_Q:C_GS], g_kv_t, g_main_t[C_GS:PM]], axis=0)
    send_in = jnp.pad(g_wt.reshape(N_DEV, W_IN_ROWS, D), ((0, 0), (0, W_IN_PAD - W_IN_ROWS), (0, 0)))
    off, row_len = _row_offsets()
    pieces = {"norm_mix_post_w": g_norms[1:2], "norm_ffn_pre_w": g_norms[0:1], "norm_ffn_post_w": g_post2[0:1],
              "ssd_norm_w": g_gnorm[0:1], "ssd_conv_b": g_conv_w[7:8], "ffn_conv_b": g_fconv[7:8],
              "ssd_dt_bias": g_ssd[0:1], "ssd_a_log": g_ssd[1:2], "ssd_d": g_ssd[2:3], "attn_sinks": g_sinks[0:1],
              "loss": loss_blk[0:1]}
    row = jnp.concatenate([jnp.pad(pieces[n][:, :min(k, pieces[n].shape[1])],
                                   ((0, 0), (0, -(-k // LANES) * LANES - min(k, pieces[n].shape[1]))))
                           for n, k in SMALL_ROW], axis=1)
    send_row = jnp.pad(row, ((0, 7), (0, 0)))
    send_conv = jnp.concatenate(
        [g_conv_w.reshape(8, N_DEV, SSD_CONV_COLS).transpose(1, 0, 2),
         g_fconv.reshape(8, N_DEV, FFN_CONV_COLS).transpose(1, 0, 2),
         jnp.zeros((N_DEV, 8, CONV_BLOCK - SSD_CONV_COLS - FFN_CONV_COLS), f32)], axis=2)
    sends = [send_in, gw_up_t.reshape(N_DEV, 2 * FF // N_DEV, D), gw_ssd_out.reshape(N_DEV, DI // N_DEV, D),
             gw_attn_out.reshape(N_DEV, D // N_DEV, D), gw_mix.reshape(N_DEV, D // N_DEV, D), send_row, send_conv]
    du_a, (r_in, r_up, r_so, r_ao, r_mix, recv_row, recv_conv) = mm(
        dproj, w_main_t, "nn", bf16, "mm_du", comm=(sends, (True, True, True, True, True, False, True)))
    du_b = mm(dkv, w_kv_t, "nn", bf16, "mm_du_kv")
    du_c = mm(ddtr_b, w_dt_t, "nn", bf16, "mm_du_dt")
    grad_x, g_pre = prenorm_bwd(xs, w["norm_mix_pre_w"], du_a, du_b, du_c, dx1, ts)
    (recv_pre,) = exchange([g_pre], (False,), "gather_last")

    tpad = lambda t: jnp.pad(t.T, ((0, W_IN_PAD - W_IN_ROWS), (0, 0)))
    o_in = [t[:W_IN_ROWS].T for t in adamw(r_in, tpad(w["w_in"]), tpad(m["w_in"]), tpad(v["w_in"]), 368, "adamw_w_in")]
    o_up = [t.T for t in adamw(r_up, w["ffn_w_up"].T, m["ffn_w_up"].T, v["ffn_w_up"].T, 352, "adamw_w_up")]
    big = {"w_in": o_in, "ffn_w_up": o_up,
           "ssd_w_out": adamw(r_so, w["ssd_w_out"], m["ssd_w_out"], v["ssd_w_out"], 256, "adamw_ssd_out"),
           "attn_w_out": adamw(r_ao, w["attn_w_out"], m["attn_w_out"], v["attn_w_out"], 128, "adamw_attn_out"),
           "w_mix_out": adamw(r_mix, w["w_mix_out"], m["w_mix_out"], v["w_mix_out"], 128, "adamw_mix"),
           "ffn_w_down": adamw(r_down, w["ffn_w_down"], m["ffn_w_down"], v["ffn_w_down"], 352, "adamw_down")}
    small_names = [n for n in WEIGHTS if n not in big]
    small, loss_row = adamw_small(recv_row, recv_pre, recv_conv, {n: (w[n], m[n], v[n]) for n in small_names})

    outs = [loss_row[0, 0], grad_x[None]]
    for k in range(4):
        for n in WEIGHTS:
            outs.append((big[n][k] if n in big else small[n][k]).reshape(a[n].shape))
    return tuple(outs)
```

```python
import jax
import jax.numpy as jnp
import numpy as np
from jax import lax
from jax.experimental import pallas as pl
from jax.experimental.pallas import tpu as pltpu

f32 = jnp.float32
bf16 = jnp.bfloat16

N_DEV = 8
D = 1024
DI = 2048
NH = 32
HD = 64
NG = 4
GW = DI // NG
NS = 128
CH = 128
CONVD = DI + 2 * NG * NS
SSD_K = 4
AH = 16
AD = 64
KVH = 4
REP = AH // KVH
KVW = KVH * AD
WIN = 128
FF = 2816
FFN_K = 3
EPS = 1e-6
ROPE_THETA = 10000.0
LANES = 128
RG = 16
CW = 256

C_Z, C_XBC, C_Q, C_GS, C_GA, PM = 0, 2048, 5120, 6144, 7168, 8192
IN_SIZES = (DI, CONVD, NH, D, KVW, KVW, D, D)
IN_OFF = tuple(int(v) for v in np.cumsum((0,) + IN_SIZES))
IN_DIM = IN_OFF[-1]

ADAM_LR, ADAM_B1, ADAM_B2, ADAM_EPS, ADAM_WD, ADAM_STEP = 0.001, 0.9, 0.999, 1e-08, 0.01, 10

VMEM_LIMIT = 56 * 1024 * 1024


def _cp(*sem, side_effects=False):
    return pltpu.CompilerParams(dimension_semantics=sem, vmem_limit_bytes=VMEM_LIMIT, has_side_effects=side_effects)


def _dot(a, b, mode="nn"):
    dims = {"nn": (((1,), (0,)), ((), ())), "nt": (((1,), (1,)), ((), ())), "tn": (((0,), (0,)), ((), ()))}[mode]
    return lax.dot_general(a, b, dims, preferred_element_type=f32)


def _split3(v):
    hi = v.astype(bf16)
    r = v - hi.astype(f32)
    mid = r.astype(bf16)
    lo = (r - mid.astype(f32)).astype(bf16)
    return hi, mid, lo


def _dot3_left(m01, v):
    hi, mid, lo = _split3(v)
    return _dot(m01, hi) + _dot(m01, mid) + _dot(m01, lo)


def _dot3_right(v, m01):
    hi, mid, lo = _split3(v)
    return _dot(hi, m01) + _dot(mid, m01) + _dot(lo, m01)


def _dot2_right(v, m01):
    hi = v.astype(bf16)
    lo = (v - hi.astype(f32)).astype(bf16)
    return _dot(hi, m01) + _dot(lo, m01)


def _sigmoid(x):
    return 1.0 / (1.0 + jnp.exp(-x))


def _sigmoid_fast(x):
    return pl.reciprocal(1.0 + jnp.exp(-x), approx=True)


def _peer(k, x, y, c):
    return ((1 - x) if k & 4 else x, (1 - y) if k & 2 else y, (1 - c) if k & 1 else c)


def _xchg_copies(buf_refs, out_refs, send_sems, recv_sems, local_sems, personalised):
    x, y, c = lax.axis_index("x"), lax.axis_index("y"), lax.axis_index("c")
    me = 4 * x + 2 * y + c
    local, remote = [], []
    for b, (buf, out, pers) in enumerate(zip(buf_refs, out_refs, personalised)):
        local.append(pltpu.make_async_copy(buf.at[me] if pers else buf, out.at[me], local_sems.at[b]))
        for k in range(1, N_DEV):
            px, py, pc = _peer(k, x, y, c)
            s = b * (N_DEV - 1) + k - 1
            remote.append(pltpu.make_async_remote_copy(
                src_ref=buf.at[4 * px + 2 * py + pc] if pers else buf, dst_ref=out.at[me],
                send_sem=send_sems.at[s], recv_sem=recv_sems.at[s],
                device_id=(px, py, pc), device_id_type=pl.DeviceIdType.MESH))
    return local, remote


class _Comm:
    def __init__(self, comm):
        self.bufs, self.pers = comm if comm else ((), ())
        self.n = len(self.bufs)

    def in_specs(self):
        return [pl.BlockSpec(memory_space=pl.ANY)] * self.n

    out_specs = in_specs

    def out_shape(self):
        return [jax.ShapeDtypeStruct((N_DEV,) + tuple(b.shape[1:] if p else b.shape), b.dtype)
                for b, p in zip(self.bufs, self.pers)]

    def scratch(self):
        n = self.n
        return [pltpu.SemaphoreType.DMA((n * (N_DEV - 1),)), pltpu.SemaphoreType.DMA((n * (N_DEV - 1),)),
                pltpu.SemaphoreType.DMA((n,))] if n else []

    def split(self, refs, n_in, n_out):
        n = self.n
        ins, outs = refs[:n_in], refs[n_in + n:n_in + n + n_out]
        rest = refs[n_in + n + n_out + n:]
        if not n:
            return ins, outs, rest, None
        copies = _xchg_copies(refs[n_in:n_in + n], refs[n_in + n + n_out:n_in + n + n_out + n], *rest[-3:], self.pers)
        return ins, outs, rest[:-3], copies

    def start(self, copies, first):
        if copies:
            @pl.when(first)
            def _():
                for cp in copies[0] + copies[1]:
                    cp.start()

    def wait(self, copies, last):
        if copies:
            @pl.when(last)
            def _():
                for cp in copies[1]:
                    cp.wait_recv()
                for cp in copies[1]:
                    cp.wait_send()
                for cp in copies[0]:
                    cp.wait()


def exchange(bufs, personalised, name):
    cm = _Comm((bufs, personalised))

    def body(*refs):
        _, _, _, copies = cm.split(refs, 0, 0)
        cm.start(copies, True)
        cm.wait(copies, True)

    return pl.pallas_call(
        body, name=name, in_specs=cm.in_specs(), out_specs=cm.out_specs(), out_shape=cm.out_shape(),
        scratch_shapes=cm.scratch(), compiler_params=pltpu.CompilerParams(has_side_effects=True),
    )(*bufs)


MM_TILES = (2176, 2048, 1408, 1024, 512, 256, 128)
MM_VMEM_BUDGET = 40 * 1024 * 1024


def _mm_tiles(M, N, K, out_bytes):
    cm = [t for t in MM_TILES if M % t == 0]
    cn = [t for t in MM_TILES if N % t == 0]
    ck = [t for t in MM_TILES if K % t == 0]
    best = None
    for bm in cm[:2]:
        for bn in cn:
            for bk in ck:
                need = 4 * (bm * bk + bk * bn) + bm * bn * (4 + 2 * out_bytes)
                if need <= MM_VMEM_BUDGET:
                    score = (bm * bn * bk, bk)
                    if best is None or score > best[0]:
                        best = (score, (bm, bn, bk))
    return best[1]


def mm(a, b, mode, out_dtype, name, comm=None):
    if mode == "nn":
        (M, K), (_, N) = a.shape, b.shape
    elif mode == "nt":
        (M, K), (N, _) = a.shape, b.shape
    else:
        (K, M), (_, N) = a.shape, b.shape
    bm, bn, bk = _mm_tiles(M, N, K, jnp.dtype(out_dtype).itemsize)
    gm, gn, nk = M // bm, N // bn, K // bk
    cm = _Comm(comm)

    def body(*refs):
        (a_ref, b_ref), (o_ref,), scr, copies = cm.split(refs, 2, 1)
        i, j, k = pl.program_id(0), pl.program_id(1), pl.program_id(2)
        cm.start(copies, jnp.logical_and(jnp.logical_and(i == 0, j == 0), k == 0))
        p = _dot(a_ref[...], b_ref[...], mode)
        if nk == 1:
            o_ref[...] = p.astype(o_ref.dtype)
        else:
            acc_ref = scr[0]

            @pl.when(k == 0)
            def _():
                acc_ref[...] = p

            @pl.when(k > 0)
            def _():
                acc_ref[...] += p

            @pl.when(k == nk - 1)
            def _():
                o_ref[...] = acc_ref[...].astype(o_ref.dtype)

        cm.wait(copies, jnp.logical_and(jnp.logical_and(i == gm - 1, j == gn - 1), k == nk - 1))

    if mode == "nn":
        a_spec = pl.BlockSpec((bm, bk), lambda i, j, k: (i, k))
        b_spec = pl.BlockSpec((bk, bn), lambda i, j, k: (k, j))
    elif mode == "nt":
        a_spec = pl.BlockSpec((bm, bk), lambda i, j, k: (i, k))
        b_spec = pl.BlockSpec((bn, bk), lambda i, j, k: (j, k))
    else:
        a_spec = pl.BlockSpec((bk, bm), lambda i, j, k: (k, i))
        b_spec = pl.BlockSpec((bk, bn), lambda i, j, k: (k, j))
    sem = ("arbitrary",) * 3 if cm.n else ("parallel", "parallel", "arbitrary")
    res = pl.pallas_call(
        body, name=name, grid=(gm, gn, nk),
        in_specs=[a_spec, b_spec] + cm.in_specs(),
        out_specs=[pl.BlockSpec((bm, bn), lambda i, j, k: (i, j))] + cm.out_specs(),
        out_shape=[jax.ShapeDtypeStruct((M, N), out_dtype)] + cm.out_shape(),
        scratch_shapes=([pltpu.VMEM((bm, bn), f32)] if nk > 1 else []) + cm.scratch(),
        compiler_params=_cp(*sem, side_effects=bool(cm.n)),
    )(a, b, *cm.bufs)
    return (res[0], res[1:]) if cm.n else res[0]


def _groups(ts, fn, carry=None, reverse=False, unroll=2):
    n = ts // RG

    def body(g, c):
        return fn(pl.multiple_of((n - 1 - g if reverse else g) * RG, RG), c)

    return lax.fori_loop(0, n, body, carry, unroll=unroll)


def _rms(x):
    return lax.rsqrt(jnp.mean(x * x, axis=-1, keepdims=True) + EPS)


def _rms_bwd(x, r, dn):
    n = x * r
    return r * (dn - n * jnp.mean(dn * n, axis=-1, keepdims=True))


def _flush(acc_ref, out_ref, row):
    out_ref[row:row + 1, :] = jnp.sum(acc_ref[...], axis=0, keepdims=True)


def prenorm_fwd(x, w, ts):
    S = x.shape[0]

    def body(x_ref, w_ref, u_ref):
        wv = w_ref[...]

        def grp(r0, _):
            xv = x_ref[pl.ds(r0, RG), :]
            u_ref[pl.ds(r0, RG), :] = (xv * _rms(xv) * wv).astype(bf16)

        _groups(ts, grp)

    return pl.pallas_call(
        body, name="prenorm_fwd", grid=(S // ts,),
        in_specs=[pl.BlockSpec((ts, D), lambda i: (i, 0)), pl.BlockSpec((1, D), lambda i: (0, 0))],
        out_specs=pl.BlockSpec((ts, D), lambda i: (i, 0)),
        out_shape=jax.ShapeDtypeStruct((S, D), bf16),
        compiler_params=_cp("parallel"),
    )(x, w)


def prenorm_bwd(x, w, du_a, du_b, du_c, dx1, ts):
    S = x.shape[0]
    nt = S // ts

    def body(x_ref, w_ref, da_ref, db_ref, dc_ref, dx1_ref, gx_ref, gw_ref, acc_ref):
        i = pl.program_id(0)
        wv = w_ref[...]

        @pl.when(i == 0)
        def _():
            acc_ref[...] = jnp.zeros_like(acc_ref)
            gw_ref[...] = jnp.zeros_like(gw_ref)

        def grp(r0, _):
            rows = pl.ds(r0, RG)
            xv = x_ref[rows, :]
            r = _rms(xv)
            du = da_ref[rows, :].astype(f32) + db_ref[rows, :].astype(f32) + dc_ref[rows, :].astype(f32)
            gx_ref[rows, :] = dx1_ref[rows, :] + _rms_bwd(xv, r, du * wv)
            acc_ref[...] += du * xv * r

        _groups(ts, grp)

        @pl.when(i == nt - 1)
        def _():
            _flush(acc_ref, gw_ref, 0)

    row = pl.BlockSpec((ts, D), lambda i: (i, 0))
    return pl.pallas_call(
        body, name="prenorm_bwd", grid=(nt,),
        in_specs=[row, pl.BlockSpec((1, D), lambda i: (0, 0)), row, row, row, row],
        out_specs=[row, pl.BlockSpec((8, D), lambda i: (0, 0))],
        out_shape=[jax.ShapeDtypeStruct((S, D), f32), jax.ShapeDtypeStruct((8, D), f32)],
        scratch_shapes=[pltpu.VMEM((RG, D), f32)],
        compiler_params=_cp("arbitrary"),
    )(x, w, du_a, du_b, du_c, dx1)


def post_fwd(x, mo, w_post, w_pre2, ts):
    S = x.shape[0]

    def body(x_ref, mo_ref, wp_ref, w2_ref, x1_ref, h_ref):
        wp, w2 = wp_ref[...], w2_ref[...]

        def grp(r0, _):
            rows = pl.ds(r0, RG)
            mv = mo_ref[rows, :]
            x1 = x_ref[rows, :] + mv * _rms(mv) * wp
            x1_ref[rows, :] = x1
            h_ref[rows, :] = (x1 * _rms(x1) * w2).astype(bf16)

        _groups(ts, grp)

    row = pl.BlockSpec((ts, D), lambda i: (i, 0))
    par = pl.BlockSpec((1, D), lambda i: (0, 0))
    return pl.pallas_call(
        body, name="post_fwd", grid=(S // ts,),
        in_specs=[row, row, par, par], out_specs=[row, row],
        out_shape=[jax.ShapeDtypeStruct((S, D), f32), jax.ShapeDtypeStruct((S, D), bf16)],
        compiler_params=_cp("parallel"),
    )(x, mo, w_post, w_pre2)


def post_bwd(dout, dh, x1, mo, w_post, w_pre2, ts):
    S = x1.shape[0]
    nt = S // ts

    def body(dout_ref, dh_ref, x1_ref, mo_ref, wp_ref, w2_ref, dx1_ref, dmo_ref, gw_ref, acc2_ref, accp_ref):
        i = pl.program_id(0)
        wp, w2 = wp_ref[...], w2_ref[...]

        @pl.when(i == 0)
        def _():
            acc2_ref[...] = jnp.zeros_like(acc2_ref)
            accp_ref[...] = jnp.zeros_like(accp_ref)
            gw_ref[...] = jnp.zeros_like(gw_ref)

        def grp(r0, _):
            rows = pl.ds(r0, RG)
            x1 = x1_ref[rows, :]
            r1 = _rms(x1)
            dh = dh_ref[rows, :].astype(f32)
            dx1 = dout_ref[rows, :] + _rms_bwd(x1, r1, dh * w2)
            dx1_ref[rows, :] = dx1
            acc2_ref[...] += dh * x1 * r1
            mv = mo_ref[rows, :]
            rm = _rms(mv)
            dmo_ref[rows, :] = _rms_bwd(mv, rm, dx1 * wp).astype(bf16)
            accp_ref[...] += dx1 * mv * rm

        _groups(ts, grp)

        @pl.when(i == nt - 1)
        def _():
            _flush(acc2_ref, gw_ref, 0)
            _flush(accp_ref, gw_ref, 1)

    row = pl.BlockSpec((ts, D), lambda i: (i, 0))
    par = pl.BlockSpec((1, D), lambda i: (0, 0))
    return pl.pallas_call(
        body, name="post_bwd", grid=(nt,),
        in_specs=[row, row, row, row, par, par],
        out_specs=[row, row, pl.BlockSpec((8, D), lambda i: (0, 0))],
        out_shape=[jax.ShapeDtypeStruct((S, D), f32), jax.ShapeDtypeStruct((S, D), bf16),
                   jax.ShapeDtypeStruct((8, D), f32)],
        scratch_shapes=[pltpu.VMEM((RG, D), f32), pltpu.VMEM((RG, D), f32)],
        compiler_params=_cp("arbitrary"),
    )(dout, dh, x1, mo, w_post, w_pre2)


def loss_head(x1, ff, target, w, ts):
    S = x1.shape[0]
    nt = S // ts

    def body(x1_ref, ff_ref, t_ref, w_ref, loss_ref, dout_ref, dff_ref, gw_ref, accw_ref, accl_ref):
        i = pl.program_id(0)
        wv = w_ref[...]

        @pl.when(i == 0)
        def _():
            accw_ref[...] = jnp.zeros_like(accw_ref)
            accl_ref[...] = jnp.zeros_like(accl_ref)
            gw_ref[...] = jnp.zeros_like(gw_ref)

        def grp(r0, _):
            rows = pl.ds(r0, RG)
            fv = ff_ref[rows, :]
            r = _rms(fv)
            n = fv * r
            e = x1_ref[rows, :] + n * wv - t_ref[rows, :]
            dout = e * (1.0 / D)
            dout_ref[rows, :] = dout
            dff_ref[rows, :] = _rms_bwd(fv, r, dout * wv).astype(bf16)
            accw_ref[...] += dout * n
            accl_ref[...] += e * e

        _groups(ts, grp)

        @pl.when(i == nt - 1)
        def _():
            _flush(accw_ref, gw_ref, 0)
            tot = jnp.sum(jnp.sum(accl_ref[...], axis=1, keepdims=True), axis=0, keepdims=True) * (0.5 / D)
            loss_ref[...] = jnp.broadcast_to(tot, loss_ref.shape)

    row = pl.BlockSpec((ts, D), lambda i: (i, 0))
    return pl.pallas_call(
        body, name="loss_head", grid=(nt,),
        in_specs=[row, row, row, pl.BlockSpec((1, D), lambda i: (0, 0))],
        out_specs=[pl.BlockSpec((8, LANES), lambda i: (0, 0)), row, row, pl.BlockSpec((8, D), lambda i: (0, 0))],
        out_shape=[jax.ShapeDtypeStruct((8, LANES), f32), jax.ShapeDtypeStruct((S, D), f32),
                   jax.ShapeDtypeStruct((S, D), bf16), jax.ShapeDtypeStruct((8, D), f32)],
        scratch_shapes=[pltpu.VMEM((RG, D), f32), pltpu.VMEM((RG, D), f32)],
        compiler_params=_cp("arbitrary"),
    )(x1, ff, target, w)


def _taps(w_ref, cs, K):
    return [jnp.broadcast_to(w_ref[k:k + 1, cs], (8, CW)) for k in range(K)]


def _down(before, cur, s, sub):
    return jnp.where(sub < s, pltpu.roll(before, s, 0), pltpu.roll(cur, s, 0))


def _up(cur, after, s, sub):
    return jnp.where(sub < 8 - s, pltpu.roll(cur, 8 - s, 0), pltpu.roll(after, 8 - s, 0))


def _conv_group(p, a, b, taps, bias, K, sub):
    ya, yb = bias, bias
    for k in range(K):
        s = K - 1 - k
        xa, xb = (a, b) if s == 0 else (_down(p, a, s, sub), _down(a, b, s, sub))
        ya = ya + taps[k] * xa
        yb = yb + taps[k] * xb
    return ya, yb


def _prev8_map(ts, cb):
    return lambda i, j: (jnp.maximum(i * (ts // 8) - 1, 0), cb + j)


def ssdconv_fwd(proj, w8, b, ts):
    S = proj.shape[0]
    bw = 1024
    cb = C_XBC // bw

    def body(cur_ref, prev_ref, w_ref, b_ref, o_ref, c_ref):
        first = pl.program_id(0) == 0
        sub = lax.broadcasted_iota(jnp.int32, (8, CW), 0)
        for c0 in range(0, bw, CW):
            cs = slice(c0, c0 + CW)
            taps = _taps(w_ref, cs, SSD_K)
            bias = jnp.broadcast_to(b_ref[:, cs], (8, CW))

            def grp(r0, p, cs=cs, taps=taps, bias=bias):
                rows = pl.ds(r0, RG)
                xv = cur_ref[rows, cs].astype(f32)
                ya, yb = _conv_group(p, xv[0:8], xv[8:16], taps, bias, SSD_K, sub)
                y = jnp.concatenate([ya, yb], axis=0)
                c_ref[rows, cs] = y.astype(bf16)
                o_ref[rows, cs] = (y * _sigmoid_fast(y)).astype(bf16)
                return xv[8:16]

            _groups(ts, grp, jnp.where(first, 0.0, prev_ref[:, cs].astype(f32)))

    o = jax.ShapeDtypeStruct((S, CONVD), bf16)
    blk = pl.BlockSpec((ts, bw), lambda i, j: (i, j))
    return pl.pallas_call(
        body, name="ssdconv_fwd", grid=(S // ts, CONVD // bw),
        in_specs=[pl.BlockSpec((ts, bw), lambda i, j: (i, cb + j)),
                  pl.BlockSpec((8, bw), _prev8_map(ts, cb)),
                  pl.BlockSpec((8, bw), lambda i, j: (0, j)),
                  pl.BlockSpec((1, bw), lambda i, j: (0, j))],
        out_specs=[blk, blk], out_shape=[o, o],
        compiler_params=_cp("parallel", "parallel"),
    )(proj, proj, w8, b)


def _gelu_tanh(x):
    c = 0.7978845608028654
    t = jnp.tanh(c * (x + 0.044715 * x * x * x))
    return 0.5 * x * (1.0 + t), t


def ffnact_fwd(up, w8, b, ts):
    S = up.shape[0]

    def body(g_ref, gp_ref, v_ref, vp_ref, wg_ref, wv_ref, bg_ref, bv_ref, o_ref, gc_ref, vc_ref):
        first = pl.program_id(0) == 0
        sub = lax.broadcasted_iota(jnp.int32, (8, CW), 0)
        for c0 in range(0, FF, CW):
            cs = slice(c0, c0 + CW)
            tg, tv = _taps(wg_ref, cs, FFN_K), _taps(wv_ref, cs, FFN_K)
            bg = jnp.broadcast_to(bg_ref[:, cs], (8, CW))
            bv = jnp.broadcast_to(bv_ref[:, cs], (8, CW))

            def grp(r0, carry, cs=cs, tg=tg, tv=tv, bg=bg, bv=bv):
                pg, pv = carry
                rows = pl.ds(r0, RG)
                gx = g_ref[rows, cs].astype(f32)
                vx = v_ref[rows, cs].astype(f32)
                g = jnp.concatenate(_conv_group(pg, gx[0:8], gx[8:16], tg, bg, FFN_K, sub), axis=0)
                v = jnp.concatenate(_conv_group(pv, vx[0:8], vx[8:16], tv, bv, FFN_K, sub), axis=0)
                gc_ref[rows, cs] = g.astype(bf16)
                vc_ref[rows, cs] = v.astype(bf16)
                o_ref[rows, cs] = (_gelu_tanh(g)[0] * v).astype(bf16)
                return gx[8:16], vx[8:16]

            _groups(ts, grp, (jnp.where(first, 0.0, gp_ref[:, cs].astype(f32)),
                              jnp.where(first, 0.0, vp_ref[:, cs].astype(f32))))

    o = jax.ShapeDtypeStruct((S, FF), bf16)
    blk = pl.BlockSpec((ts, FF), lambda i: (i, 0))
    prev = lambda cb: pl.BlockSpec((8, FF), lambda i: (jnp.maximum(i * (ts // 8) - 1, 0), cb))
    return pl.pallas_call(
        body, name="ffnact_fwd", grid=(S // ts,),
        in_specs=[blk, prev(0), pl.BlockSpec((ts, FF), lambda i: (i, 1)), prev(1),
                  pl.BlockSpec((8, FF), lambda i: (0, 0)), pl.BlockSpec((8, FF), lambda i: (0, 1)),
                  pl.BlockSpec((1, FF), lambda i: (0, 0)), pl.BlockSpec((1, FF), lambda i: (0, 1))],
        out_specs=[blk, blk, blk], out_shape=[o, o, o],
        compiler_params=_cp("parallel"),
    )(up, up, up, up, w8, w8, b, b)


def ffnact_bwd(dact, gc, vc, ts):
    S = dact.shape[0]

    def body(d_ref, g_ref, v_ref, dg_ref, dv_ref):
        c = 0.7978845608028654
        for c0 in range(0, FF, CW):
            cs = slice(c0, c0 + CW)

            def grp(r0, _, cs=cs):
                rows = pl.ds(r0, RG)
                d = d_ref[rows, cs].astype(f32)
                g = g_ref[rows, cs].astype(f32)
                ge, t = _gelu_tanh(g)
                dgelu = 0.5 * (1.0 + t) + 0.5 * g * (1.0 - t * t) * c * (1.0 + 3.0 * 0.044715 * g * g)
                dg_ref[rows, cs] = (d * v_ref[rows, cs].astype(f32) * dgelu).astype(bf16)
                dv_ref[rows, cs] = (d * ge).astype(bf16)

            _groups(ts, grp)

    o = jax.ShapeDtypeStruct((S, FF), bf16)
    blk = pl.BlockSpec((ts, FF), lambda i: (i, 0))
    return pl.pallas_call(
        body, name="ffnact_bwd", grid=(S // ts,),
        in_specs=[blk, blk, blk], out_specs=[blk, blk], out_shape=[o, o],
        compiler_params=_cp("parallel"),
    )(dact, gc, vc)


def dwconv_bwd(dy, x, xcb, w8, wcb, K, bw, ts, name, act_c=None, into=None, ocb=0, out_cols=None):
    S, C = dy.shape
    nr = S // ts
    out_cols = out_cols or C
    n_act = 0 if act_c is None else 2

    def body(*refs):
        dy_ref, dyn_ref = refs[0:2]
        c_ref, cn_ref = (refs[2:4] if n_act else (None, None))
        x_ref, xp_ref, w_ref = refs[2 + n_act:5 + n_act]
        dx_ref, dw_ref, sd_ref = refs[-3:]
        i = pl.program_id(1)
        first, last = i == 0, i == nr - 1
        sub = lax.broadcasted_iota(jnp.int32, (8, CW), 0)

        def grad_y(d, c):
            if c is None:
                return d.astype(f32)
            cv = c.astype(f32)
            s = _sigmoid_fast(cv)
            return d.astype(f32) * s * (1.0 + cv * (1.0 - s))

        @pl.when(first)
        def _():
            dw_ref[...] = jnp.zeros_like(dw_ref)

        for c0 in range(0, bw, CW):
            cs = slice(c0, c0 + CW)
            taps = _taps(w_ref, cs, K)
            zero = jnp.zeros((8, CW), f32)

            def fwd(r0, carry, cs=cs):
                p, accs, accb = carry
                rows = pl.ds(r0, RG)
                g = grad_y(dy_ref[rows, cs], c_ref[rows, cs] if n_act else None)
                sd_ref[rows, cs] = g
                xv = x_ref[rows, cs].astype(f32)
                a, b = xv[0:8], xv[8:16]
                ga, gb = g[0:8], g[8:16]
                new = []
                for k in range(K):
                    s = K - 1 - k
                    xa, xb = (a, b) if s == 0 else (_down(p, a, s, sub), _down(a, b, s, sub))
                    new.append(accs[k] + ga * xa + gb * xb)
                return b, tuple(new), accb + ga + gb

            _, accs, accb = _groups(ts, fwd, (jnp.where(first, 0.0, xp_ref[:, cs].astype(f32)), (zero,) * K, zero))
            for k in range(K):
                dw_ref[k:k + 1, cs] += jnp.sum(accs[k], axis=0, keepdims=True)
            dw_ref[7:8, cs] += jnp.sum(accb, axis=0, keepdims=True)

            def bwd(r0, after, cs=cs, taps=taps):
                rows = pl.ds(r0, RG)
                g = sd_ref[rows, cs]
                a, b = g[0:8], g[8:16]
                da, db = zero, zero
                for k in range(K):
                    s = K - 1 - k
                    ua, ub = (a, b) if s == 0 else (_up(a, b, s, sub), _up(b, after, s, sub))
                    da = da + taps[k] * ua
                    db = db + taps[k] * ub
                dx_ref[rows, cs] = jnp.concatenate([da, db], axis=0).astype(bf16)
                return a

            halo = grad_y(dyn_ref[:, cs], cn_ref[:, cs] if n_act else None)
            _groups(ts, bwd, jnp.where(last, 0.0, halo), reverse=True)

    nxt = lambda j, i: (jnp.minimum((i + 1) * (ts // 8), S // 8 - 1), j)
    tile = pl.BlockSpec((ts, bw), lambda j, i: (i, j))
    acts = [] if act_c is None else [act_c, act_c]
    extra = [] if into is None else [into]
    n_in = 5 + n_act
    return pl.pallas_call(
        body, name=name, grid=(C // bw, nr),
        in_specs=[tile, pl.BlockSpec((8, bw), nxt)] + ([tile, pl.BlockSpec((8, bw), nxt)] if n_act else []) + [
            pl.BlockSpec((ts, bw), lambda j, i: (i, xcb + j)),
            pl.BlockSpec((8, bw), lambda j, i: (jnp.maximum(i * (ts // 8) - 1, 0), xcb + j)),
            pl.BlockSpec((8, bw), lambda j, i: (0, wcb + j))] + [pl.BlockSpec(memory_space=pl.ANY)] * len(extra),
        out_specs=[pl.BlockSpec((ts, bw), lambda j, i: (i, ocb + j)), pl.BlockSpec((8, bw), lambda j, i: (0, j))],
        out_shape=[jax.ShapeDtypeStruct((S, out_cols), bf16), jax.ShapeDtypeStruct((8, C), f32)],
        scratch_shapes=[pltpu.VMEM((ts, bw), f32)],
        input_output_aliases={n_in: 0} if extra else {},
        compiler_params=_cp("parallel", "arbitrary"),
    )(dy, dy, *acts, x, x, w8, *extra)


def gnorm_fwd(y, proj, w, ts):
    S = y.shape[0]

    def body(y_ref, z_ref, w_ref, o_ref):
        for k in range(NG):
            sl = slice(k * GW, (k + 1) * GW)
            wv = w_ref[:, sl]

            def grp(r0, _, sl=sl, wv=wv):
                rows = pl.ds(r0, RG)
                z = z_ref[rows, sl].astype(f32)
                g = y_ref[rows, sl].astype(f32) * z * _sigmoid_fast(z)
                o_ref[rows, sl] = (g * _rms(g) * wv).astype(bf16)

            _groups(ts, grp)

    row = pl.BlockSpec((ts, DI), lambda i: (i, 0))
    return pl.pallas_call(
        body, name="gnorm_fwd", grid=(S // ts,),
        in_specs=[row, row, pl.BlockSpec((1, DI), lambda i: (0, 0))],
        out_specs=row, out_shape=jax.ShapeDtypeStruct((S, DI), bf16),
        compiler_params=_cp("parallel"),
    )(y, proj, w)


def gnorm_bwd(dyn, y, proj, w, dproj, ts):
    S = y.shape[0]
    nt = S // ts

    def body(d_ref, y_ref, z_ref, w_ref, _, dy_ref, dz_ref, gw_ref, acc_ref):
        i = pl.program_id(0)

        @pl.when(i == 0)
        def _():
            acc_ref[...] = jnp.zeros_like(acc_ref)
            gw_ref[...] = jnp.zeros_like(gw_ref)

        for k in range(NG):
            sl = slice(k * GW, (k + 1) * GW)
            wv = w_ref[:, sl]

            def grp(r0, _, sl=sl, wv=wv):
                rows = pl.ds(r0, RG)
                z = z_ref[rows, sl].astype(f32)
                yv = y_ref[rows, sl].astype(f32)
                s = _sigmoid_fast(z)
                sz = z * s
                g = yv * sz
                r = _rms(g)
                d = d_ref[rows, sl].astype(f32)
                acc_ref[:, sl] += d * g * r
                dg = _rms_bwd(g, r, d * wv)
                dy_ref[rows, sl] = (dg * sz).astype(bf16)
                dz_ref[rows, sl] = (dg * yv * s * (1.0 + z * (1.0 - s))).astype(bf16)

            _groups(ts, grp)

        @pl.when(i == nt - 1)
        def _():
            _flush(acc_ref, gw_ref, 0)

    row = pl.BlockSpec((ts, DI), lambda i: (i, 0))
    return pl.pallas_call(
        body, name="gnorm_bwd", grid=(nt,),
        in_specs=[row, row, row, pl.BlockSpec((1, DI), lambda i: (0, 0)), pl.BlockSpec(memory_space=pl.ANY)],
        out_specs=[row, row, pl.BlockSpec((8, DI), lambda i: (0, 0))],
        out_shape=[jax.ShapeDtypeStruct((S, DI), bf16), jax.ShapeDtypeStruct(dproj.shape, bf16),
                   jax.ShapeDtypeStruct((8, DI), f32)],
        scratch_shapes=[pltpu.VMEM((RG, DI), f32)],
        input_output_aliases={4: 1},
        compiler_params=_cp("arbitrary"),
    )(dyn, y, proj, w, dproj)


def merge_fwd(proj, ys, ya, ts):
    S = ys.shape[0]

    def body(gs_ref, ga_ref, ys_ref, ya_ref, o_ref):
        for c0 in range(0, D, CW):
            cs = slice(c0, c0 + CW)

            def grp(r0, _, cs=cs):
                rows = pl.ds(r0, RG)
                o_ref[rows, cs] = (_sigmoid_fast(gs_ref[rows, cs].astype(f32)) * ys_ref[rows, cs].astype(f32)
                                   + _sigmoid_fast(ga_ref[rows, cs].astype(f32)) * ya_ref[rows, cs].astype(f32)
                                   ).astype(bf16)

            _groups(ts, grp)

    row = pl.BlockSpec((ts, D), lambda i: (i, 0))
    return pl.pallas_call(
        body, name="merge_fwd", grid=(S // ts,),
        in_specs=[pl.BlockSpec((ts, D), lambda i: (i, C_GS // D)), pl.BlockSpec((ts, D), lambda i: (i, C_GA // D)), row, row],
        out_specs=row, out_shape=jax.ShapeDtypeStruct((S, D), bf16),
        compiler_params=_cp("parallel"),
    )(proj, proj, ys, ya)


def merge_bwd(dm, proj, ys, ya, ts):
    S = ys.shape[0]

    def body(d_ref, gs_ref, ga_ref, ys_ref, ya_ref, dys_ref, dya_ref, dg_ref):
        for c0 in range(0, D, CW):
            cs = slice(c0, c0 + CW)

            def grp(r0, _, c0=c0, cs=cs):
                rows = pl.ds(r0, RG)
                d = d_ref[rows, cs].astype(f32)
                ss = _sigmoid_fast(gs_ref[rows, cs].astype(f32))
                sa = _sigmoid_fast(ga_ref[rows, cs].astype(f32))
                dys_ref[rows, cs] = (d * ss).astype(bf16)
                dya_ref[rows, cs] = (d * sa).astype(bf16)
                dg_ref[rows, cs] = (d * ys_ref[rows, cs].astype(f32) * ss * (1.0 - ss)).astype(bf16)
                dg_ref[rows, D + c0:D + c0 + CW] = (d * ya_ref[rows, cs].astype(f32) * sa * (1.0 - sa)).astype(bf16)

            _groups(ts, grp)

    row = pl.BlockSpec((ts, D), lambda i: (i, 0))
    o = jax.ShapeDtypeStruct((S, D), bf16)
    return pl.pallas_call(
        body, name="merge_bwd", grid=(S // ts,),
        in_specs=[row, pl.BlockSpec((ts, D), lambda i: (i, C_GS // D)), pl.BlockSpec((ts, D), lambda i: (i, C_GA // D)), row, row],
        out_specs=[row, row, pl.BlockSpec((ts, 2 * D), lambda i: (i, C_GS // (2 * D)))],
        out_shape=[o, o, jax.ShapeDtypeStruct((S, PM), bf16)],
        compiler_params=_cp("parallel"),
    )(dm, proj, proj, ys, ya)


def _ssd_consts():
    h = lax.broadcasted_iota(jnp.int32, (LANES, DI), 0)
    c = lax.broadcasted_iota(jnp.int32, (LANES, DI), 1)
    expand = (c // HD == h).astype(bf16)
    r = lax.broadcasted_iota(jnp.int32, (CH, CH), 0)
    cc = lax.broadcasted_iota(jnp.int32, (CH, CH), 1)
    tril = (cc <= r).astype(bf16)
    triu = (cc >= r).astype(bf16)
    return expand, expand.T, tril, triu


def _ssd_common(xbc_ref, dtr_ref, bias_ref, alog_ref, expand_ref, tril_ref):
    dtr = dtr_ref[...] + bias_ref[...]
    dt = jnp.maximum(dtr, 0.0) + jnp.log1p(jnp.exp(-jnp.abs(dtr)))
    a = -jnp.exp(alog_ref[...])
    acs = _dot3_left(tril_ref[...], dt * a)
    acsx = _dot3_right(acs, expand_ref[...])
    dtx = _dot3_right(dt, expand_ref[...])
    x = xbc_ref[:, 0:DI].astype(f32)
    xdt = x * dtx
    e = jnp.exp(acsx)
    dsx = jnp.exp(acsx[CH - 1:CH, :] - acsx)
    return dtr, dt, a, acs, dtx, x, xdt, e, dsx


def _ssd_lmat(acs, acs_t, hh, causal):
    seg = acs[:, hh:hh + 1] - acs_t[hh:hh + 1, :]
    return jnp.where(causal, jnp.exp(jnp.minimum(seg, 0.0)), 0.0)


def ssd_fwd(xbc, dtr, bias, alog, dx_row, comm=None):
    S = xbc.shape[0]
    nc = S // CH
    expand, _, tril, _ = _ssd_consts()
    cm = _Comm(comm)

    def body(*refs):
        ins, (y_ref, hp_ref), (h_ref, yd_ref), copies = cm.split(refs, 7, 2)
        xbc_ref, dtr_ref, bias_ref, alog_ref, dxr_ref, expand_ref, tril_ref = ins
        c = pl.program_id(0)
        cm.start(copies, c == 0)

        @pl.when(c == 0)
        def _():
            h_ref[...] = jnp.zeros_like(h_ref)

        _, _, _, acs, _, x, xdt, e, dsx = _ssd_common(xbc_ref, dtr_ref, bias_ref, alog_ref, expand_ref, tril_ref)
        acs_t = acs.T
        xb = xdt.astype(bf16)
        xd = (xdt * dsx).astype(bf16)
        causal = tril_ref[...] > 0
        for g in range(NG):
            gs = slice(g * GW, (g + 1) * GW)
            bg = xbc_ref[:, DI + g * NS:DI + (g + 1) * NS]
            cg = xbc_ref[:, DI + NG * NS + g * NS:DI + NG * NS + (g + 1) * NS]
            cb = _dot(cg, bg, "nt")
            hp = h_ref[g]
            hpb = hp.astype(bf16)
            hp_ref[0, g] = hpb
            yd_ref[:, gs] = _dot(cg, hpb) * e[:, gs]
            h_ref[g] = hp * e[CH - 1:CH, gs] + _dot(bg, xd[:, gs], "tn")
            for j in range(NH // NG):
                hh = g * (NH // NG) + j
                hs = slice(hh * HD, (hh + 1) * HD)
                m = (cb * _ssd_lmat(acs, acs_t, hh, causal)).astype(bf16)
                yd_ref[:, hs] += _dot(m, xb[:, hs])
        y_ref[...] = (yd_ref[...] + dxr_ref[...] * x).astype(bf16)
        cm.wait(copies, c == nc - 1)

    par = lambda shape: pl.BlockSpec(shape, lambda c: (0,) * len(shape))
    res = pl.pallas_call(
        body, name="ssd_fwd", grid=(nc,),
        in_specs=[pl.BlockSpec((CH, CONVD), lambda c: (c, 0)), pl.BlockSpec((CH, LANES), lambda c: (c, 0)),
                  par((1, LANES)), par((1, LANES)), par((1, DI)), par((LANES, DI)), par((CH, CH))] + cm.in_specs(),
        out_specs=[pl.BlockSpec((CH, DI), lambda c: (c, 0)),
                   pl.BlockSpec((1, NG, NS, GW), lambda c: (c, 0, 0, 0))] + cm.out_specs(),
        out_shape=[jax.ShapeDtypeStruct((S, DI), bf16), jax.ShapeDtypeStruct((nc, NG, NS, GW), bf16)] + cm.out_shape(),
        scratch_shapes=[pltpu.VMEM((NG, NS, GW), f32), pltpu.VMEM((CH, DI), f32)] + cm.scratch(),
        compiler_params=_cp("arbitrary", side_effects=bool(cm.n)),
    )(xbc, dtr, bias, alog, dx_row, expand, tril, *cm.bufs)
    return res[0], res[1], res[2:]


def ssd_bwd(xbc, dtr, dy, hprev, bias, alog, dx_row, comm=None):
    S = xbc.shape[0]
    nc = S // CH
    expand, expand_t, tril, triu = _ssd_consts()
    cm = _Comm(comm)

    def body(*refs):
        ins, outs, scr, copies = cm.split(refs, 11, 3)
        xbc_ref, dtr_ref, dy_ref, hp_ref, bias_ref, alog_ref, dxr_ref, expand_ref, expt_ref, tril_ref, triu_ref = ins
        dxbc_ref, ddtr_ref, acc_ref = outs
        dh_ref, dxs_ref, t_ref, accb_ref, acca_ref, accd_ref = scr
        c = pl.program_id(0)
        cm.start(copies, c == 0)

        @pl.when(c == 0)
        def _():
            dh_ref[...] = jnp.zeros_like(dh_ref)
            accb_ref[...] = jnp.zeros_like(accb_ref)
            acca_ref[...] = jnp.zeros_like(acca_ref)
            accd_ref[...] = jnp.zeros_like(accd_ref)

        dtr, dt, a, acs, dtx, x, xdt, e, dsx = _ssd_common(xbc_ref, dtr_ref, bias_ref, alog_ref, expand_ref, tril_ref)
        acs_t = acs.T
        xb = xdt.astype(bf16)
        xdf = xdt * dsx
        xd = xdf.astype(bf16)
        dyv = dy_ref[...].astype(f32)
        dyb = dy_ref[...]
        dye = (dyv * e).astype(bf16)
        causal = tril_ref[...] > 0
        lane = lax.broadcasted_iota(jnp.int32, (CH, LANES), 1)
        subl = lax.broadcasted_iota(jnp.int32, (LANES, CH), 0)
        ccol = jnp.zeros((CH, LANES), f32)
        rrow = jnp.zeros((LANES, CH), f32)
        last_row = lax.broadcasted_iota(jnp.int32, (CH, 1), 0) == CH - 1
        for g in range(NG):
            gs = slice(g * GW, (g + 1) * GW)
            bsl = slice(DI + g * NS, DI + (g + 1) * NS)
            csl = slice(DI + NG * NS + g * NS, DI + NG * NS + (g + 1) * NS)
            bg = xbc_ref[:, bsl]
            cg = xbc_ref[:, csl]
            cb = _dot(cg, bg, "nt")
            hpb = hp_ref[0, g]
            dhn = dh_ref[g]
            dhnb = dhn.astype(bf16)
            yoff = _dot(cg, hpb) * e[:, gs]
            dxd = _dot(bg, dhnb)
            t2 = dxd * xdf[:, gs]
            t3 = jnp.sum(dhn * hpb.astype(f32), axis=0, keepdims=True) * e[CH - 1:CH, gs]
            t_ref[:, gs] = dyv[:, gs] * yoff - t2 + jnp.where(last_row, jnp.sum(t2, axis=0, keepdims=True) + t3, 0.0)
            dxs_ref[:, gs] = dxd * dsx[:, gs]
            dcg = _dot(dye[:, gs], hpb, "nt")
            dbg = _dot(xd[:, gs], dhnb, "nt")
            dh_ref[g] = dhn * e[CH - 1:CH, gs] + _dot(cg, dye[:, gs], "tn")
            dcb = jnp.zeros((CH, CH), f32)
            for j in range(NH // NG):
                hh = g * (NH // NG) + j
                hs = slice(hh * HD, (hh + 1) * HD)
                lm = _ssd_lmat(acs, acs_t, hh, causal)
                m = cb * lm
                dm = _dot(dyb[:, hs], xb[:, hs], "nt")
                gm = dm * m
                ccol = ccol + jnp.sum(gm, axis=1, keepdims=True) * (lane == hh).astype(f32)
                rrow = rrow + jnp.sum(gm, axis=0, keepdims=True) * (subl == hh).astype(f32)
                dcb = dcb + dm * lm
                dxs_ref[:, hs] += _dot(m.astype(bf16), dyb[:, hs], "tn")
            dcbb = dcb.astype(bf16)
            dxbc_ref[:, csl] = (dcg + _dot(dcbb, bg)).astype(bf16)
            dxbc_ref[:, bsl] = (dbg + _dot(dcbb, cg, "tn")).astype(bf16)
        dxf = dxs_ref[...]
        dxbc_ref[:, 0:DI] = (dxf * dtx + dxr_ref[...] * dyv).astype(bf16)
        expt = expt_ref[...]
        dacs = ccol - rrow.T + _dot2_right(t_ref[...], expt)
        dadt = _dot3_left(triu_ref[...], dacs)
        ddt = _dot2_right(dxf * x, expt) + dadt * a
        ddtr = ddt * _sigmoid(dtr)
        ddtr_ref[...] = ddtr
        accb_ref[...] += ddtr
        acca_ref[...] += dadt * dt
        accd_ref[...] += _dot2_right(dyv * x, expt)

        @pl.when(c == nc - 1)
        def _():
            acc_ref[...] = jnp.zeros_like(acc_ref)
            acc_ref[0:1, :] = jnp.sum(accb_ref[...], axis=0, keepdims=True)
            acc_ref[1:2, :] = jnp.sum(acca_ref[...], axis=0, keepdims=True) * a
            acc_ref[2:3, :] = jnp.sum(accd_ref[...], axis=0, keepdims=True)

        cm.wait(copies, c == nc - 1)

    par = lambda shape: pl.BlockSpec(shape, lambda c: (0,) * len(shape))
    rev = lambda c: (nc - 1 - c, 0)
    res = pl.pallas_call(
        body, name="ssd_bwd", grid=(nc,),
        in_specs=[pl.BlockSpec((CH, CONVD), rev), pl.BlockSpec((CH, LANES), rev), pl.BlockSpec((CH, DI), rev),
                  pl.BlockSpec((1, NG, NS, GW), lambda c: (nc - 1 - c, 0, 0, 0)),
                  par((1, LANES)), par((1, LANES)), par((1, DI)), par((LANES, DI)), par((DI, LANES)),
                  par((CH, CH)), par((CH, CH))] + cm.in_specs(),
        out_specs=[pl.BlockSpec((CH, CONVD), rev), pl.BlockSpec((CH, LANES), rev), par((8, LANES))] + cm.out_specs(),
        out_shape=[jax.ShapeDtypeStruct((S, CONVD), bf16), jax.ShapeDtypeStruct((S, LANES), f32),
                   jax.ShapeDtypeStruct((8, LANES), f32)] + cm.out_shape(),
        scratch_shapes=[pltpu.VMEM((NG, NS, GW), f32), pltpu.VMEM((CH, DI), f32), pltpu.VMEM((CH, DI), f32),
                        pltpu.VMEM((CH, LANES), f32), pltpu.VMEM((CH, LANES), f32),
                        pltpu.VMEM((CH, LANES), f32)] + cm.scratch(),
        compiler_params=_cp("arbitrary", side_effects=bool(cm.n)),
    )(xbc, dtr, dy, hprev, bias, alog, dx_row, expand, expand_t, tril, triu, *cm.bufs)
    return res[0], res[1], res[2], res[3:]


def rope_tables(pos_col, ts):
    S = pos_col.shape[0]
    half = AD // 2
    inv = ROPE_THETA ** (-jnp.arange(half, dtype=f32) * 2.0 / AD)
    inv_row = jnp.tile(inv, LANES // half)[None, :]

    def body(p_ref, inv_ref, cos_ref, sin_ref):
        ang = p_ref[...].astype(f32) * inv_ref[...]
        lane = lax.broadcasted_iota(jnp.int32, ang.shape, 1)
        cos_ref[...] = jnp.cos(ang)
        sin_ref[...] = jnp.where(lane % AD < half, -1.0, 1.0) * jnp.sin(ang)

    o = jax.ShapeDtypeStruct((S, LANES), f32)
    return pl.pallas_call(
        body, name="rope_tables", grid=(S // ts,),
        in_specs=[pl.BlockSpec((ts, 1), lambda i: (i, 0)), pl.BlockSpec((1, LANES), lambda i: (0, 0))],
        out_specs=[pl.BlockSpec((ts, LANES), lambda i: (i, 0))] * 2, out_shape=[o, o],
        compiler_params=_cp("parallel"),
    )(pos_col, inv_row)


def _partner(t):
    w = t.shape[1]
    lane = lax.broadcasted_iota(jnp.int32, t.shape, 1)
    return jnp.where(lane % AD < AD // 2, pltpu.roll(t, w - AD // 2, 1), pltpu.roll(t, AD // 2, 1))


def _rope(t, cos, sin):
    reps = t.shape[1] // LANES
    return t * jnp.tile(cos, (1, reps)) + _partner(t) * jnp.tile(sin, (1, reps))


def _rope_t(d, cos, sin):
    reps = d.shape[1] // LANES
    return d * jnp.tile(cos, (1, reps)) - _partner(d) * jnp.tile(sin, (1, reps))


def _stack(t, g):
    return jnp.concatenate([t[:, (g * REP + r) * AD:(g * REP + r + 1) * AD] for r in range(REP)], axis=0)


def _sink_col(sink_ref, g):
    return jnp.concatenate([jnp.broadcast_to(sink_ref[0:1, g * REP + r:g * REP + r + 1], (WIN, 1)) for r in range(REP)],
                           axis=0)


def _attn_probs(qh, kp, kc, sink, not_first):
    n = qh.shape[0]
    r = lax.broadcasted_iota(jnp.int32, (n, WIN), 0) % WIN
    c = lax.broadcasted_iota(jnp.int32, (n, WIN), 1)
    neg = -1e30
    sp = jnp.where(jnp.logical_and(c > r, not_first), _dot(qh, kp, "nt"), neg)
    sc = jnp.where(c <= r, _dot(qh, kc, "nt"), neg)
    m = jnp.maximum(jnp.maximum(jnp.max(sp, axis=1, keepdims=True), jnp.max(sc, axis=1, keepdims=True)), sink)
    pp = jnp.exp(sp - m)
    pc = jnp.exp(sc - m)
    ps = jnp.exp(sink - m)
    inv = 1.0 / (jnp.sum(pp, axis=1, keepdims=True) + jnp.sum(pc, axis=1, keepdims=True) + ps)
    return pp * inv, pc * inv, ps * inv


def attn_fwd(proj, kv, cos, sin, sinks):
    S = proj.shape[0]
    nb = S // WIN
    prev = lambda cb: (lambda i: (jnp.maximum(i - 1, 0), cb))

    def body(q_ref, k_ref, kp_ref, v_ref, vp_ref, cos_ref, sin_ref, cosp_ref, sinp_ref, sink_ref, o_ref):
        i = pl.program_id(0)
        q = (_rope(q_ref[...].astype(f32), cos_ref[...], sin_ref[...]) * (AD ** -0.5)).astype(bf16)
        kc = _rope(k_ref[...].astype(f32), cos_ref[...], sin_ref[...]).astype(bf16)
        kp = _rope(kp_ref[...].astype(f32), cosp_ref[...], sinp_ref[...]).astype(bf16)
        vc = v_ref[...]
        vp = vp_ref[...]
        for g in range(KVH):
            ks = slice(g * AD, (g + 1) * AD)
            pp, pc, _ = _attn_probs(_stack(q, g), kp[:, ks], kc[:, ks], _sink_col(sink_ref, g), i > 0)
            o = _dot(pp.astype(bf16), vp[:, ks]) + _dot(pc.astype(bf16), vc[:, ks])
            for r in range(REP):
                h = g * REP + r
                o_ref[:, h * AD:(h + 1) * AD] = o[r * WIN:(r + 1) * WIN].astype(bf16)

    tab = pl.BlockSpec((WIN, LANES), lambda i: (i, 0))
    tabp = pl.BlockSpec((WIN, LANES), prev(0))
    return pl.pallas_call(
        body, name="attn_fwd", grid=(nb,),
        in_specs=[pl.BlockSpec((WIN, D), lambda i: (i, C_Q // D)),
                  pl.BlockSpec((WIN, KVW), lambda i: (i, 0)), pl.BlockSpec((WIN, KVW), prev(0)),
                  pl.BlockSpec((WIN, KVW), lambda i: (i, 1)), pl.BlockSpec((WIN, KVW), prev(1)),
                  tab, tab, tabp, tabp, pl.BlockSpec((1, LANES), lambda i: (0, 0))],
        out_specs=pl.BlockSpec((WIN, D), lambda i: (i, 0)),
        out_shape=jax.ShapeDtypeStruct((S, D), bf16),
        compiler_params=_cp("parallel"),
    )(proj, kv, kv, kv, kv, cos, sin, cos, sin, sinks)


def attn_bwd(proj, kv, cos, sin, sinks, dao, dproj, comm=None):
    S = proj.shape[0]
    nb = S // WIN
    cur = lambda cb: (lambda i: (jnp.minimum(i, nb - 1), cb))
    prev = lambda cb: (lambda i: (jnp.maximum(i - 1, 0), cb))
    cm = _Comm(comm)

    def body(*refs):
        ins, (dq_ref, dkv_ref, ds_ref), scr, copies = cm.split(refs, 12, 3)
        q_ref, k_ref, kp_ref, v_ref, vp_ref, cos_ref, sin_ref, cosp_ref, sinp_ref, sink_ref, do_ref, _ = ins
        ck_ref, cv_ref, dqs_ref, dkp_ref, dvp_ref, dkc_ref, dvc_ref, accs_ref = scr
        i = pl.program_id(0)
        cm.start(copies, i == 0)

        @pl.when(i == 0)
        def _():
            ck_ref[...] = jnp.zeros_like(ck_ref)
            cv_ref[...] = jnp.zeros_like(cv_ref)
            accs_ref[...] = jnp.zeros_like(accs_ref)

        @pl.when(i == nb)
        def _():
            dkp_ref[...] = jnp.zeros_like(dkp_ref)
            dvp_ref[...] = jnp.zeros_like(dvp_ref)

        @pl.when(i < nb)
        def _():
            q = (_rope(q_ref[...].astype(f32), cos_ref[...], sin_ref[...]) * (AD ** -0.5)).astype(bf16)
            kc = _rope(k_ref[...].astype(f32), cos_ref[...], sin_ref[...]).astype(bf16)
            kp = _rope(kp_ref[...].astype(f32), cosp_ref[...], sinp_ref[...]).astype(bf16)
            vc = v_ref[...]
            vp = vp_ref[...]
            do = do_ref[...]
            lane = lax.broadcasted_iota(jnp.int32, (WIN, LANES), 1)
            accs = accs_ref[...]
            for g in range(KVH):
                ks = slice(g * AD, (g + 1) * AD)
                qs = _stack(q, g)
                dos = _stack(do, g)
                pp, pc, ps = _attn_probs(qs, kp[:, ks], kc[:, ks], _sink_col(sink_ref, g), i > 0)
                dpp = _dot(dos, vp[:, ks], "nt")
                dpc = _dot(dos, vc[:, ks], "nt")
                delta = jnp.sum(pp * dpp + pc * dpc, axis=1, keepdims=True)
                dsp = (pp * (dpp - delta)).astype(bf16)
                dsc = (pc * (dpc - delta)).astype(bf16)
                sd = ps * delta
                dqs = (_dot(dsp, kp[:, ks]) + _dot(dsc, kc[:, ks])) * (AD ** -0.5)
                for r in range(REP):
                    h = g * REP + r
                    accs = accs - sd[r * WIN:(r + 1) * WIN] * (lane == h).astype(f32)
                    dqs_ref[:, h * AD:(h + 1) * AD] = dqs[r * WIN:(r + 1) * WIN]
                dkp_ref[:, ks] = _dot(dsp, qs, "tn")
                dkc_ref[:, ks] = _dot(dsc, qs, "tn")
                dvp_ref[:, ks] = _dot(pp.astype(bf16), dos, "tn")
                dvc_ref[:, ks] = _dot(pc.astype(bf16), dos, "tn")
            accs_ref[...] = accs
            dq_ref[...] = _rope_t(dqs_ref[...], cos_ref[...], sin_ref[...]).astype(bf16)

        dkv_ref[:, 0:KVW] = _rope_t(ck_ref[...] + dkp_ref[...], cosp_ref[...], sinp_ref[...]).astype(bf16)
        dkv_ref[:, KVW:2 * KVW] = (cv_ref[...] + dvp_ref[...]).astype(bf16)

        @pl.when(i < nb)
        def _():
            ck_ref[...] = dkc_ref[...]
            cv_ref[...] = dvc_ref[...]

        @pl.when(i == nb)
        def _():
            ds_ref[...] = jnp.zeros_like(ds_ref)
            ds_ref[0:1, :] = jnp.sum(accs_ref[...], axis=0, keepdims=True)

        cm.wait(copies, i == nb)

    tab = pl.BlockSpec((WIN, LANES), cur(0))
    tabp = pl.BlockSpec((WIN, LANES), prev(0))
    kvs = lambda: pltpu.VMEM((WIN, KVW), f32)
    res = pl.pallas_call(
        body, name="attn_bwd", grid=(nb + 1,),
        in_specs=[pl.BlockSpec((WIN, D), cur(C_Q // D)),
                  pl.BlockSpec((WIN, KVW), cur(0)), pl.BlockSpec((WIN, KVW), prev(0)),
                  pl.BlockSpec((WIN, KVW), cur(1)), pl.BlockSpec((WIN, KVW), prev(1)),
                  tab, tab, tabp, tabp, pl.BlockSpec((1, LANES), lambda i: (0, 0)),
                  pl.BlockSpec((WIN, D), cur(0)), pl.BlockSpec(memory_space=pl.ANY)] + cm.in_specs(),
        out_specs=[pl.BlockSpec((WIN, D), cur(C_Q // D)), pl.BlockSpec((WIN, 2 * KVW), prev(0)),
                   pl.BlockSpec((8, LANES), lambda i: (0, 0))] + cm.out_specs(),
        out_shape=[jax.ShapeDtypeStruct(dproj.shape, bf16), jax.ShapeDtypeStruct((S, 2 * KVW), bf16),
                   jax.ShapeDtypeStruct((8, LANES), f32)] + cm.out_shape(),
        scratch_shapes=[kvs(), kvs(), pltpu.VMEM((WIN, D), f32), kvs(), kvs(), kvs(), kvs(),
                        pltpu.VMEM((WIN, LANES), f32)] + cm.scratch(),
        input_output_aliases={11: 0},
        compiler_params=_cp("arbitrary", side_effects=bool(cm.n)),
    )(proj, kv, kv, kv, kv, cos, sin, cos, sin, sinks, dao, dproj, *cm.bufs)
    return res[0], res[1], res[2], res[3:]


ADAM_C1 = 1.0 / (1.0 - ADAM_B1 ** ADAM_STEP)
ADAM_C2 = 1.0 / (1.0 - ADAM_B2 ** ADAM_STEP)


def _adam_update(g, w, m, v):
    nm = ADAM_B1 * m + (1.0 - ADAM_B1) * g
    nv = ADAM_B2 * v + (1.0 - ADAM_B2) * (g * g)
    return -ADAM_LR * ((nm * ADAM_C1) / (jnp.sqrt(nv * ADAM_C2) + ADAM_EPS) + ADAM_WD * w), nm, nv


def adamw(parts, w, m, v, tr, name):
    n, R, C = parts.shape

    def body(p_ref, w_ref, m_ref, v_ref, g_ref, d_ref, nm_ref, nv_ref):
        def grp(g0, _):
            r0 = pl.multiple_of(g0 * RG, RG)
            rows = pl.ds(r0, RG)
            g = p_ref[0, rows, :].astype(f32)
            for k in range(1, n):
                g = g + p_ref[k, rows, :].astype(f32)
            d, nm, nv = _adam_update(g, w_ref[rows, :], m_ref[rows, :], v_ref[rows, :])
            g_ref[rows, :] = g
            d_ref[rows, :] = d
            nm_ref[rows, :] = nm
            nv_ref[rows, :] = nv
            return 0

        lax.fori_loop(0, tr // RG, grp, 0)

    row = pl.BlockSpec((tr, C), lambda i: (i, 0))
    o = jax.ShapeDtypeStruct((R, C), f32)
    return pl.pallas_call(
        body, name=name, grid=(R // tr,),
        in_specs=[pl.BlockSpec((n, tr, C), lambda i: (0, i, 0)), row, row, row],
        out_specs=[row, row, row, row], out_shape=[o, o, o, o],
        compiler_params=_cp("parallel"),
    )(parts, w, m, v)


SMALL_ROW = (("norm_mix_post_w", D), ("norm_ffn_pre_w", D), ("norm_ffn_post_w", D), ("ssd_norm_w", DI),
             ("ssd_conv_b", CONVD), ("ffn_conv_b", 2 * FF), ("ssd_dt_bias", NH), ("ssd_a_log", NH), ("ssd_d", NH),
             ("attn_sinks", AH), ("loss", 1))
CONV_BLOCK = 1152
SSD_CONV_COLS = CONVD // N_DEV
FFN_CONV_COLS = 2 * FF // N_DEV


def _row_offsets():
    off, o = {}, 0
    for name, n in SMALL_ROW:
        off[name] = (o, n)
        o += -(-n // LANES) * LANES
    return off, o


def adamw_small(recv_row, recv_pre, recv_conv, params):
    off, _ = _row_offsets()
    names = list(params)
    n = len(names)

    def total(ref, rows, lo, width):
        g = ref[0, rows, lo:lo + width]
        for d in range(1, N_DEV):
            g = g + ref[d, rows, lo:lo + width]
        return g

    def grad_of(name, row_ref, pre_ref, conv_ref):
        if name == "norm_mix_pre_w":
            return total(pre_ref, slice(0, 1), 0, D)
        if name == "ssd_conv_w":
            return total(conv_ref, slice(0, SSD_K), 0, SSD_CONV_COLS)
        if name == "ffn_conv_w":
            return total(conv_ref, slice(0, FFN_K), 3 * LANES, FFN_CONV_COLS)
        o, width = off[name]
        return total(row_ref, slice(0, 1), o, width)

    def body(row_ref, pre_ref, conv_ref, *refs):
        ins, outs = refs[:3 * n], refs[3 * n:]
        for k, name in enumerate(names):
            w_ref, m_ref, v_ref = ins[3 * k:3 * k + 3]
            g_ref, d_ref, nm_ref, nv_ref = outs[4 * k:4 * k + 4]
            g = grad_of(name, row_ref, pre_ref, conv_ref)
            d, nm, nv = _adam_update(g, w_ref[...], m_ref[...], v_ref[...])
            g_ref[...] = g
            d_ref[...] = d
            nm_ref[...] = nm
            nv_ref[...] = nv
        outs[4 * n][...] = total(row_ref, slice(0, 1), off["loss"][0], LANES)

    flat = [t for name in names for t in params[name]]
    out_shape = [jax.ShapeDtypeStruct(params[name][0].shape, f32) for name in names for _ in range(4)]
    res = pl.pallas_call(
        body, name="adamw_small",
        out_shape=out_shape + [jax.ShapeDtypeStruct((1, LANES), f32)],
        compiler_params=pltpu.CompilerParams(vmem_limit_bytes=VMEM_LIMIT),
    )(recv_row, recv_pre, recv_conv, *flat)
    return {name: res[4 * k:4 * k + 4] for k, name in enumerate(names)}, res[4 * n]


def _pad_rows8(w):
    return jnp.pad(w, ((0, 8 - w.shape[0]), (0, 0)))


def _pad_lanes(v):
    return jnp.pad(v.reshape(1, -1), ((0, 0), (0, LANES - v.size)))


WEIGHTS = ('norm_mix_pre_w', 'w_in', 'ssd_conv_w', 'ssd_conv_b', 'ssd_dt_bias', 'ssd_a_log', 'ssd_d', 'ssd_norm_w',
           'ssd_w_out', 'attn_sinks', 'attn_w_out', 'w_mix_out', 'norm_mix_post_w', 'norm_ffn_pre_w', 'ffn_w_up',
           'ffn_conv_w', 'ffn_conv_b', 'ffn_w_down', 'norm_ffn_post_w')
W_IN_ROWS = IN_DIM // N_DEV
W_IN_PAD = 1104
TS = 256


def kernel(x, positions, norm_mix_pre_w, w_in, ssd_conv_w, ssd_conv_b, ssd_dt_bias, ssd_a_log, ssd_d, ssd_norm_w, ssd_w_out, attn_sinks, attn_w_out, w_mix_out, norm_mix_post_w, norm_ffn_pre_w, ffn_w_up, ffn_conv_w, ffn_conv_b, ffn_w_down, norm_ffn_post_w, loss_target, m_norm_mix_pre_w, m_w_in, m_ssd_conv_w, m_ssd_conv_b, m_ssd_dt_bias, m_ssd_a_log, m_ssd_d, m_ssd_norm_w, m_ssd_w_out, m_attn_sinks, m_attn_w_out, m_w_mix_out, m_norm_mix_post_w, m_norm_ffn_pre_w, m_ffn_w_up, m_ffn_conv_w, m_ffn_conv_b, m_ffn_w_down, m_norm_ffn_post_w, v_norm_mix_pre_w, v_w_in, v_ssd_conv_w, v_ssd_conv_b, v_ssd_dt_bias, v_ssd_a_log, v_ssd_d, v_ssd_norm_w, v_ssd_w_out, v_attn_sinks, v_attn_w_out, v_w_mix_out, v_norm_mix_post_w, v_norm_ffn_pre_w, v_ffn_w_up, v_ffn_conv_w, v_ffn_conv_b, v_ffn_w_down, v_norm_ffn_post_w):
    a = locals()
    r2 = lambda t: t.reshape(t.shape[-2], t.shape[-1])
    w = {n: r2(a[n]) for n in WEIGHTS}
    m = {n: r2(a["m_" + n]) for n in WEIGHTS}
    v = {n: r2(a["v_" + n]) for n in WEIGHTS}
    xs, target = x[0], loss_target[0]
    S = xs.shape[0]
    ts = TS

    w_in_blk = jnp.pad(w["w_in"].T.astype(bf16), ((0, W_IN_PAD - W_IN_ROWS), (0, 0)))
    conv_blk = jnp.concatenate([_pad_rows8(w["ssd_conv_w"]), _pad_rows8(w["ffn_conv_w"]),
                                jnp.zeros((8, CONV_BLOCK - SSD_CONV_COLS - FFN_CONV_COLS), f32)], axis=1)
    g_in, g_conv = exchange([w_in_blk, conv_blk], (False, False), "gather_first")
    wt = g_in[:, :W_IN_ROWS].reshape(IN_DIM, D)
    w_main_t = jnp.concatenate([wt[IN_OFF[0]:IN_OFF[2]], wt[IN_OFF[3]:IN_OFF[4]], wt[IN_OFF[6]:IN_OFF[8]]], axis=0)
    w_kv_t = wt[IN_OFF[4]:IN_OFF[6]]
    w_dt_t = jnp.pad(wt[IN_OFF[2]:IN_OFF[3]], ((0, LANES - NH), (0, 0)))
    conv_w8 = g_conv[:, :, 0:SSD_CONV_COLS].transpose(1, 0, 2).reshape(8, CONVD)
    fconv_w8 = g_conv[:, :, SSD_CONV_COLS:SSD_CONV_COLS + FFN_CONV_COLS].transpose(1, 0, 2).reshape(8, 2 * FF)
    bias = _pad_lanes(w["ssd_dt_bias"])
    alog = _pad_lanes(w["ssd_a_log"])
    dx_row = jnp.repeat(w["ssd_d"].reshape(-1), HD).reshape(1, DI)
    sinks = _pad_lanes(w["attn_sinks"])

    u = prenorm_fwd(xs, w["norm_mix_pre_w"], ts)
    later = [w["ssd_w_out"].astype(bf16), w["attn_w_out"].astype(bf16), w["w_mix_out"].astype(bf16)]
    proj, (g_so, g_ao, g_mix) = mm(u, w_main_t, "nt", bf16, "mm_proj", comm=(later, (False,) * 3))
    w_ssd_out, w_attn_out, w_mix = g_so.reshape(DI, D), g_ao.reshape(D, D), g_mix.reshape(D, D)
    kv = mm(u, w_kv_t, "nt", bf16, "mm_kv")
    dtr = mm(u, w_dt_t, "nt", f32, "mm_dt")
    xbc, conv_c = ssdconv_fwd(proj, conv_w8, w["ssd_conv_b"], ts)
    y, hprev, (g_up,) = ssd_fwd(xbc, dtr, bias, alog, dx_row, comm=([w["ffn_w_up"].T.astype(bf16)], (False,)))
    w_up_t = g_up.reshape(2 * FF, D)
    yn = gnorm_fwd(y, proj, w["ssd_norm_w"], ts)
    ys, (g_down,) = mm(yn, w_ssd_out, "nn", bf16, "mm_ssd_out", comm=([w["ffn_w_down"].astype(bf16)], (False,)))
    w_down = g_down.reshape(FF, D)
    cos, sin = rope_tables(positions.reshape(S, 1), ts)
    ao = attn_fwd(proj, kv, cos, sin, sinks)
    ya = mm(ao, w_attn_out, "nn", bf16, "mm_attn_out")
    merged = merge_fwd(proj, ys, ya, ts)
    mo = mm(merged, w_mix, "nn", f32, "mm_mix")
    x1, h = post_fwd(xs, mo, w["norm_mix_post_w"], w["norm_ffn_pre_w"], ts)
    up = mm(h, w_up_t, "nt", bf16, "mm_up")
    act, gate_c, val_c = ffnact_fwd(up, fconv_w8, w["ffn_conv_b"], ts)
    ff = mm(act, w_down, "nn", f32, "mm_down")
    loss_blk, dout, dff, g_post2 = loss_head(x1, ff, target, w["norm_ffn_post_w"], ts)

    dact = mm(dff, w_down, "nt", bf16, "mm_dact")
    gw_down = mm(act, dff, "tn", bf16, "mm_g_down")
    dgate, dval = ffnact_bwd(dact, gate_c, val_c, ts)
    dup_pre, g_fconv_a = dwconv_bwd(dgate, up, 0, fconv_w8, 0, FFN_K, FF, ts, "ffnconv_bwd_gate", out_cols=2 * FF)
    dup_pre, g_fconv_b = dwconv_bwd(dval, up, 1, fconv_w8, 1, FFN_K, FF, ts, "ffnconv_bwd_val", into=dup_pre, ocb=1,
                                    out_cols=2 * FF)
    g_fconv = jnp.concatenate([g_fconv_a, g_fconv_b], axis=1)
    dh, (r_down,) = mm(dup_pre, w_up_t, "nn", bf16, "mm_dh", comm=([gw_down.reshape(N_DEV, FF // N_DEV, D)], (True,)))
    gw_up_t = mm(dup_pre, h, "tn", bf16, "mm_g_up")
    dx1, dmo, g_norms = post_bwd(dout, dh, x1, mo, w["norm_mix_post_w"], w["norm_ffn_pre_w"], ts)
    dmerged = mm(dmo, w_mix, "nt", bf16, "mm_dmerged")
    gw_mix = mm(merged, dmo, "tn", bf16, "mm_g_mix")
    dys, dya, dproj = merge_bwd(dmerged, proj, ys, ya, ts)
    dao = mm(dya, w_attn_out, "nt", bf16, "mm_dao")
    gw_attn_out = mm(ao, dya, "tn", bf16, "mm_g_attn_out")
    dproj, dkv, g_sinks, (r_up,) = attn_bwd(proj, kv, cos, sin, sinks, dao, dproj,
                                            comm=([gw_up_t.reshape(N_DEV, 2 * FF // N_DEV, D)], (True,)))
    dyn = mm(dys, w_ssd_out, "nt", bf16, "mm_dyn")
    gw_ssd_out = mm(yn, dys, "tn", bf16, "mm_g_ssd_out")
    dy, dproj, g_gnorm = gnorm_bwd(dyn, y, proj, w["ssd_norm_w"], dproj, ts)
    sends = [gw_ssd_out.reshape(N_DEV, DI // N_DEV, D), gw_attn_out.reshape(N_DEV, D // N_DEV, D),
             gw_mix.reshape(N_DEV, D // N_DEV, D)]
    dxbc, ddtr, g_ssd, (r_so, r_ao, r_mix) = ssd_bwd(xbc, dtr, dy, hprev, bias, alog, dx_row, comm=(sends, (True,) * 3))
    dproj, g_conv_w = dwconv_bwd(dxbc, proj, C_XBC // 1024, conv_w8, 0, SSD_K, 1024, ts, "ssdconv_bwd", act_c=conv_c,
                                 into=dproj, ocb=C_XBC // 1024, out_cols=PM)
    ddtr_b = ddtr.astype(bf16)
    g_main_t = mm(dproj, u, "tn", bf16, "mm_g_in")
    g_kv_t = mm(dkv, u, "tn", bf16, "mm_g_kv")
    g_dt_t = mm(ddtr_b, u, "tn", bf16, "mm_g_dt")
    g_wt = jnp.concatenate([g_main_t[C_Z:C_Q], g_dt_t[:NH], g_main_t[C_Q:C_GS], g_kv_t, g_main_t[C_GS:PM]], axis=0)
    send_in = jnp.pad(g_wt.reshape(N_DEV, W_IN_ROWS, D), ((0, 0), (0, W_IN_PAD - W_IN_ROWS), (0, 0)))
    pieces = {"norm_mix_post_w": g_norms[1:2], "norm_ffn_pre_w": g_norms[0:1], "norm_ffn_post_w": g_post2[0:1],
              "ssd_norm_w": g_gnorm[0:1], "ssd_conv_b": g_conv_w[7:8], "ffn_conv_b": g_fconv[7:8],
              "ssd_dt_bias": g_ssd[0:1], "ssd_a_log": g_ssd[1:2], "ssd_d": g_ssd[2:3], "attn_sinks": g_sinks[0:1],
              "loss": loss_blk[0:1]}
    row = jnp.concatenate([jnp.pad(pieces[n][:, :min(k, pieces[n].shape[1])],
                                   ((0, 0), (0, -(-k // LANES) * LANES - min(k, pieces[n].shape[1]))))
                           for n, k in SMALL_ROW], axis=1)
    send_row = jnp.pad(row, ((0, 7), (0, 0)))
    send_conv = jnp.concatenate(
        [g_conv_w.reshape(8, N_DEV, SSD_CONV_COLS).transpose(1, 0, 2),
         g_fconv.reshape(8, N_DEV, FFN_CONV_COLS).transpose(1, 0, 2),
         jnp.zeros((N_DEV, 8, CONV_BLOCK - SSD_CONV_COLS - FFN_CONV_COLS), f32)], axis=2)
    du_a, (r_in, recv_row, recv_conv) = mm(dproj, w_main_t, "nn", bf16, "mm_du",
                                           comm=([send_in, send_row, send_conv], (True, False, True)))
    du_b = mm(dkv, w_kv_t, "nn", bf16, "mm_du_kv")
    du_c = mm(ddtr_b, w_dt_t, "nn", bf16, "mm_du_dt")
    grad_x, g_pre = prenorm_bwd(xs, w["norm_mix_pre_w"], du_a, du_b, du_c, dx1, ts)
    (recv_pre,) = exchange([g_pre], (False,), "gather_last")

    tpad = lambda t: jnp.pad(t.T, ((0, W_IN_PAD - W_IN_ROWS), (0, 0)))
    o_in = [t[:W_IN_ROWS].T for t in adamw(r_in, tpad(w["w_in"]), tpad(m["w_in"]), tpad(v["w_in"]), 368, "adamw_w_in")]
    o_up = [t.T for t in adamw(r_up, w["ffn_w_up"].T, m["ffn_w_up"].T, v["ffn_w_up"].T, 352, "adamw_w_up")]
    big = {"w_in": o_in, "ffn_w_up": o_up,
           "ssd_w_out": adamw(r_so, w["ssd_w_out"], m["ssd_w_out"], v["ssd_w_out"], 256, "adamw_ssd_out"),
           "attn_w_out": adamw(r_ao, w["attn_w_out"], m["attn_w_out"], v["attn_w_out"], 128, "adamw_attn_out"),
           "w_mix_out": adamw(r_mix, w["w_mix_out"], m["w_mix_out"], v["w_mix_out"], 128, "adamw_mix"),
           "ffn_w_down": adamw(r_down, w["ffn_w_down"], m["ffn_w_down"], v["ffn_w_down"], 352, "adamw_down")}
    small_names = [n for n in WEIGHTS if n not in big]
    small, loss_row = adamw_small(recv_row, recv_pre, recv_conv, {n: (w[n], m[n], v[n]) for n in small_names})

    outs = [loss_row[0, 0], grad_x[None]]
    for k in range(4):
        for n in WEIGHTS:
            outs.append((big[n][k] if n in big else small[n][k]).reshape(a[n].shape))
    return tuple(outs)
```

```python
import jax
import jax.numpy as jnp
import numpy as np
from jax import lax
from jax.experimental import pallas as pl
from jax.experimental.pallas import tpu as pltpu

f32 = jnp.float32
bf16 = jnp.bfloat16

N_DEV = 8
D = 1024
DI = 2048
NH = 32
HD = 64
NG = 4
GW = DI // NG
NS = 128
CH = 128
CONVD = DI + 2 * NG * NS
SSD_K = 4
AH = 16
AD = 64
KVH = 4
REP = AH // KVH
KVW = KVH * AD
WIN = 128
FF = 2816
FFN_K = 3
EPS = 1e-6
ROPE_THETA = 10000.0
LANES = 128
RG = 16
CW = 256

C_Z, C_GS, C_GA, C_XBC, PM = 0, 2048, 3072, 4096, 7168
IN_SIZES = (DI, CONVD, NH, D, KVW, KVW, D, D)
IN_OFF = tuple(int(v) for v in np.cumsum((0,) + IN_SIZES))
IN_DIM = IN_OFF[-1]

ADAM_LR, ADAM_B1, ADAM_B2, ADAM_EPS, ADAM_WD, ADAM_STEP = 0.001, 0.9, 0.999, 1e-08, 0.01, 10

VMEM_LIMIT = 56 * 1024 * 1024


def _cp(*sem, side_effects=False):
    return pltpu.CompilerParams(dimension_semantics=sem, vmem_limit_bytes=VMEM_LIMIT, has_side_effects=side_effects)


def _dot(a, b, mode="nn"):
    dims = {"nn": (((1,), (0,)), ((), ())), "nt": (((1,), (1,)), ((), ())), "tn": (((0,), (0,)), ((), ()))}[mode]
    return lax.dot_general(a, b, dims, preferred_element_type=f32)


def _split3(v):
    hi = v.astype(bf16)
    r = v - hi.astype(f32)
    mid = r.astype(bf16)
    lo = (r - mid.astype(f32)).astype(bf16)
    return hi, mid, lo


def _dot3_left(m01, v):
    hi, mid, lo = _split3(v)
    return _dot(m01, hi) + _dot(m01, mid) + _dot(m01, lo)


def _dot3_right(v, m01):
    hi, mid, lo = _split3(v)
    return _dot(hi, m01) + _dot(mid, m01) + _dot(lo, m01)


def _dot2_right(v, m01):
    hi = v.astype(bf16)
    lo = (v - hi.astype(f32)).astype(bf16)
    return _dot(hi, m01) + _dot(lo, m01)


def _sigmoid(x):
    return 1.0 / (1.0 + jnp.exp(-x))


def _sigmoid_fast(x):
    return pl.reciprocal(1.0 + jnp.exp(-x), approx=True)


def _peer(k, x, y, c):
    return ((1 - x) if k & 4 else x, (1 - y) if k & 2 else y, (1 - c) if k & 1 else c)


def _xchg_copies(buf_refs, out_refs, send_sems, recv_sems, local_sems, personalised):
    x, y, c = lax.axis_index("x"), lax.axis_index("y"), lax.axis_index("c")
    me = 4 * x + 2 * y + c
    local, remote = [], []
    for b, (buf, out, pers) in enumerate(zip(buf_refs, out_refs, personalised)):
        local.append(pltpu.make_async_copy(buf.at[me] if pers else buf, out.at[me], local_sems.at[b]))
        for k in range(1, N_DEV):
            px, py, pc = _peer(k, x, y, c)
            s = b * (N_DEV - 1) + k - 1
            remote.append(pltpu.make_async_remote_copy(
                src_ref=buf.at[4 * px + 2 * py + pc] if pers else buf, dst_ref=out.at[me],
                send_sem=send_sems.at[s], recv_sem=recv_sems.at[s],
                device_id=(px, py, pc), device_id_type=pl.DeviceIdType.MESH))
    return local, remote


class _Comm:
    def __init__(self, comm):
        self.bufs, self.pers = comm if comm else ((), ())
        self.n = len(self.bufs)

    def in_specs(self):
        return [pl.BlockSpec(memory_space=pl.ANY)] * self.n

    out_specs = in_specs

    def out_shape(self):
        return [jax.ShapeDtypeStruct((N_DEV,) + tuple(b.shape[1:] if p else b.shape), b.dtype)
                for b, p in zip(self.bufs, self.pers)]

    def scratch(self):
        n = self.n
        return [pltpu.SemaphoreType.DMA((n * (N_DEV - 1),)), pltpu.SemaphoreType.DMA((n * (N_DEV - 1),)),
                pltpu.SemaphoreType.DMA((n,))] if n else []

    def split(self, refs, n_in, n_out):
        n = self.n
        ins, outs = refs[:n_in], refs[n_in + n:n_in + n + n_out]
        rest = refs[n_in + n + n_out + n:]
        if not n:
            return ins, outs, rest, None
        copies = _xchg_copies(refs[n_in:n_in + n], refs[n_in + n + n_out:n_in + n + n_out + n], *rest[-3:], self.pers)
        return ins, outs, rest[:-3], copies

    def start(self, copies, first):
        if copies:
            @pl.when(first)
            def _():
                for cp in copies[0] + copies[1]:
                    cp.start()

    def wait(self, copies, last):
        if copies:
            @pl.when(last)
            def _():
                for cp in copies[1]:
                    cp.wait_recv()
                for cp in copies[1]:
                    cp.wait_send()
                for cp in copies[0]:
                    cp.wait()


def exchange(bufs, personalised, name):
    cm = _Comm((bufs, personalised))

    def body(*refs):
        _, _, _, copies = cm.split(refs, 0, 0)
        cm.start(copies, True)
        cm.wait(copies, True)

    return pl.pallas_call(
        body, name=name, in_specs=cm.in_specs(), out_specs=cm.out_specs(), out_shape=cm.out_shape(),
        scratch_shapes=cm.scratch(), compiler_params=pltpu.CompilerParams(has_side_effects=True),
    )(*bufs)


MM_TILES = (2176, 2048, 1408, 1024, 512, 256, 128)
MM_VMEM_BUDGET = 40 * 1024 * 1024


def _mm_tiles(M, N, K, out_bytes):
    cm = [t for t in MM_TILES if M % t == 0]
    cn = [t for t in MM_TILES if N % t == 0]
    ck = [t for t in MM_TILES if K % t == 0]
    best = None
    for bm in cm[:2]:
        for bn in cn:
            for bk in ck:
                need = 4 * (bm * bk + bk * bn) + bm * bn * (4 + 2 * out_bytes)
                if need <= MM_VMEM_BUDGET:
                    score = (bm * bn * bk, bk)
                    if best is None or score > best[0]:
                        best = (score, (bm, bn, bk))
    return best[1]


def mm(a, b, mode, out_dtype, name, comm=None):
    if mode == "nn":
        (M, K), (_, N) = a.shape, b.shape
    elif mode == "nt":
        (M, K), (N, _) = a.shape, b.shape
    else:
        (K, M), (_, N) = a.shape, b.shape
    bm, bn, bk = _mm_tiles(M, N, K, jnp.dtype(out_dtype).itemsize)
    gm, gn, nk = M // bm, N // bn, K // bk
    cm = _Comm(comm)

    def body(*refs):
        (a_ref, b_ref), (o_ref,), scr, copies = cm.split(refs, 2, 1)
        i, j, k = pl.program_id(0), pl.program_id(1), pl.program_id(2)
        cm.start(copies, jnp.logical_and(jnp.logical_and(i == 0, j == 0), k == 0))
        p = _dot(a_ref[...], b_ref[...], mode)
        if nk == 1:
            o_ref[...] = p.astype(o_ref.dtype)
        else:
            acc_ref = scr[0]

            @pl.when(k == 0)
            def _():
                acc_ref[...] = p

            @pl.when(k > 0)
            def _():
                acc_ref[...] += p

            @pl.when(k == nk - 1)
            def _():
                o_ref[...] = acc_ref[...].astype(o_ref.dtype)

        cm.wait(copies, jnp.logical_and(jnp.logical_and(i == gm - 1, j == gn - 1), k == nk - 1))

    if mode == "nn":
        a_spec = pl.BlockSpec((bm, bk), lambda i, j, k: (i, k))
        b_spec = pl.BlockSpec((bk, bn), lambda i, j, k: (k, j))
    elif mode == "nt":
        a_spec = pl.BlockSpec((bm, bk), lambda i, j, k: (i, k))
        b_spec = pl.BlockSpec((bn, bk), lambda i, j, k: (j, k))
    else:
        a_spec = pl.BlockSpec((bk, bm), lambda i, j, k: (k, i))
        b_spec = pl.BlockSpec((bk, bn), lambda i, j, k: (k, j))
    sem = ("arbitrary",) * 3 if cm.n else ("parallel", "parallel", "arbitrary")
    res = pl.pallas_call(
        body, name=name, grid=(gm, gn, nk),
        in_specs=[a_spec, b_spec] + cm.in_specs(),
        out_specs=[pl.BlockSpec((bm, bn), lambda i, j, k: (i, j))] + cm.out_specs(),
        out_shape=[jax.ShapeDtypeStruct((M, N), out_dtype)] + cm.out_shape(),
        scratch_shapes=([pltpu.VMEM((bm, bn), f32)] if nk > 1 else []) + cm.scratch(),
        compiler_params=_cp(*sem, side_effects=bool(cm.n)),
    )(a, b, *cm.bufs)
    return (res[0], res[1:]) if cm.n else res[0]


def _groups(ts, fn, carry=None, reverse=False, unroll=4, rg=RG):
    n = ts // rg
    if n == 1:
        return fn(0, carry)

    def body(g, c):
        return fn(pl.multiple_of((n - 1 - g if reverse else g) * rg, rg), c)

    return lax.fori_loop(0, n, body, carry, unroll=unroll)


def _rms(x):
    return lax.rsqrt(jnp.mean(x * x, axis=-1, keepdims=True) + EPS)


def _rms_bwd(x, r, dn):
    n = x * r
    return r * (dn - n * jnp.mean(dn * n, axis=-1, keepdims=True))


NRG = 256


def _fold(x):
    return jnp.sum(x.reshape(x.shape[0] // 8, 8, x.shape[1]), axis=0)


def _flush(acc_ref, out_ref, row):
    out_ref[row:row + 1, :] = jnp.sum(acc_ref[...], axis=0, keepdims=True)


def prenorm_fwd(x, w, ts):
    S = x.shape[0]

    def body(x_ref, w_ref, u_ref):
        wv = w_ref[...]

        def grp(r0, _):
            xv = x_ref[pl.ds(r0, NRG), :]
            u_ref[pl.ds(r0, NRG), :] = (xv * _rms(xv) * wv).astype(bf16)

        _groups(ts, grp, rg=NRG)

    return pl.pallas_call(
        body, name="prenorm_fwd", grid=(S // ts,),
        in_specs=[pl.BlockSpec((ts, D), lambda i: (i, 0)), pl.BlockSpec((1, D), lambda i: (0, 0))],
        out_specs=pl.BlockSpec((ts, D), lambda i: (i, 0)),
        out_shape=jax.ShapeDtypeStruct((S, D), bf16),
        compiler_params=_cp("parallel"),
    )(x, w)


def prenorm_bwd(x, w, dus, dx1, ts):
    S = x.shape[0]
    nt = S // ts
    nd = len(dus)

    def body(x_ref, w_ref, *refs):
        du_refs = refs[:nd]
        dx1_ref, gx_ref, gw_ref, acc_ref = refs[nd:]
        i = pl.program_id(0)
        wv = w_ref[...]

        @pl.when(i == 0)
        def _():
            acc_ref[...] = jnp.zeros_like(acc_ref)
            gw_ref[...] = jnp.zeros_like(gw_ref)

        def grp(r0, _):
            rows = pl.ds(r0, NRG)
            xv = x_ref[rows, :]
            r = _rms(xv)
            du = du_refs[0][rows, :].astype(f32)
            for d_ref in du_refs[1:]:
                du = du + d_ref[rows, :].astype(f32)
            gx_ref[rows, :] = dx1_ref[rows, :] + _rms_bwd(xv, r, du * wv)
            acc_ref[...] += _fold(du * xv * r)

        _groups(ts, grp, rg=NRG)

        @pl.when(i == nt - 1)
        def _():
            _flush(acc_ref, gw_ref, 0)

    row = pl.BlockSpec((ts, D), lambda i: (i, 0))
    return pl.pallas_call(
        body, name="prenorm_bwd", grid=(nt,),
        in_specs=[row, pl.BlockSpec((1, D), lambda i: (0, 0))] + [row] * (nd + 1),
        out_specs=[row, pl.BlockSpec((8, D), lambda i: (0, 0))],
        out_shape=[jax.ShapeDtypeStruct((S, D), f32), jax.ShapeDtypeStruct((8, D), f32)],
        scratch_shapes=[pltpu.VMEM((8, D), f32)],
        compiler_params=_cp("arbitrary"),
    )(x, w, *dus, dx1)


def post_fwd(x, mo, w_post, w_pre2, ts):
    S = x.shape[0]

    def body(x_ref, mo_ref, wp_ref, w2_ref, x1_ref, h_ref):
        wp, w2 = wp_ref[...], w2_ref[...]

        def grp(r0, _):
            rows = pl.ds(r0, NRG)
            mv = mo_ref[rows, :]
            x1 = x_ref[rows, :] + mv * _rms(mv) * wp
            x1_ref[rows, :] = x1
            h_ref[rows, :] = (x1 * _rms(x1) * w2).astype(bf16)

        _groups(ts, grp, rg=NRG)

    row = pl.BlockSpec((ts, D), lambda i: (i, 0))
    par = pl.BlockSpec((1, D), lambda i: (0, 0))
    return pl.pallas_call(
        body, name="post_fwd", grid=(S // ts,),
        in_specs=[row, row, par, par], out_specs=[row, row],
        out_shape=[jax.ShapeDtypeStruct((S, D), f32), jax.ShapeDtypeStruct((S, D), bf16)],
        compiler_params=_cp("parallel"),
    )(x, mo, w_post, w_pre2)


def post_bwd(dout, dh, x1, mo, w_post, w_pre2, ts):
    S = x1.shape[0]
    nt = S // ts

    def body(dout_ref, dh_ref, x1_ref, mo_ref, wp_ref, w2_ref, dx1_ref, dmo_ref, gw_ref, acc2_ref, accp_ref):
        i = pl.program_id(0)
        wp, w2 = wp_ref[...], w2_ref[...]

        @pl.when(i == 0)
        def _():
            acc2_ref[...] = jnp.zeros_like(acc2_ref)
            accp_ref[...] = jnp.zeros_like(accp_ref)
            gw_ref[...] = jnp.zeros_like(gw_ref)

        def grp(r0, _):
            rows = pl.ds(r0, NRG)
            x1 = x1_ref[rows, :]
            r1 = _rms(x1)
            dh = dh_ref[rows, :].astype(f32)
            dx1 = dout_ref[rows, :] + _rms_bwd(x1, r1, dh * w2)
            dx1_ref[rows, :] = dx1
            acc2_ref[...] += _fold(dh * x1 * r1)
            mv = mo_ref[rows, :]
            rm = _rms(mv)
            dmo_ref[rows, :] = _rms_bwd(mv, rm, dx1 * wp).astype(bf16)
            accp_ref[...] += _fold(dx1 * mv * rm)

        _groups(ts, grp, rg=NRG)

        @pl.when(i == nt - 1)
        def _():
            _flush(acc2_ref, gw_ref, 0)
            _flush(accp_ref, gw_ref, 1)

    row = pl.BlockSpec((ts, D), lambda i: (i, 0))
    par = pl.BlockSpec((1, D), lambda i: (0, 0))
    return pl.pallas_call(
        body, name="post_bwd", grid=(nt,),
        in_specs=[row, row, row, row, par, par],
        out_specs=[row, row, pl.BlockSpec((8, D), lambda i: (0, 0))],
        out_shape=[jax.ShapeDtypeStruct((S, D), f32), jax.ShapeDtypeStruct((S, D), bf16),
                   jax.ShapeDtypeStruct((8, D), f32)],
        scratch_shapes=[pltpu.VMEM((8, D), f32), pltpu.VMEM((8, D), f32)],
        compiler_params=_cp("arbitrary"),
    )(dout, dh, x1, mo, w_post, w_pre2)


def loss_head(x1, ff, target, w, ts):
    S = x1.shape[0]
    nt = S // ts

    def body(x1_ref, ff_ref, t_ref, w_ref, loss_ref, dout_ref, dff_ref, gw_ref, accw_ref, accl_ref):
        i = pl.program_id(0)
        wv = w_ref[...]

        @pl.when(i == 0)
        def _():
            accw_ref[...] = jnp.zeros_like(accw_ref)
            accl_ref[...] = jnp.zeros_like(accl_ref)
            gw_ref[...] = jnp.zeros_like(gw_ref)

        def grp(r0, _):
            rows = pl.ds(r0, NRG)
            fv = ff_ref[rows, :]
            r = _rms(fv)
            n = fv * r
            e = x1_ref[rows, :] + n * wv - t_ref[rows, :]
            dout = e * (1.0 / D)
            dout_ref[rows, :] = dout
            dff_ref[rows, :] = _rms_bwd(fv, r, dout * wv).astype(bf16)
            accw_ref[...] += _fold(dout * n)
            accl_ref[...] += _fold(e * e)

        _groups(ts, grp, rg=NRG)

        @pl.when(i == nt - 1)
        def _():
            _flush(accw_ref, gw_ref, 0)
            tot = jnp.sum(jnp.sum(accl_ref[...], axis=1, keepdims=True), axis=0, keepdims=True) * (0.5 / D)
            loss_ref[...] = jnp.broadcast_to(tot, loss_ref.shape)

    row = pl.BlockSpec((ts, D), lambda i: (i, 0))
    return pl.pallas_call(
        body, name="loss_head", grid=(nt,),
        in_specs=[row, row, row, pl.BlockSpec((1, D), lambda i: (0, 0))],
        out_specs=[pl.BlockSpec((8, LANES), lambda i: (0, 0)), row, row, pl.BlockSpec((8, D), lambda i: (0, 0))],
        out_shape=[jax.ShapeDtypeStruct((8, LANES), f32), jax.ShapeDtypeStruct((S, D), f32),
                   jax.ShapeDtypeStruct((S, D), bf16), jax.ShapeDtypeStruct((8, D), f32)],
        scratch_shapes=[pltpu.VMEM((8, D), f32), pltpu.VMEM((8, D), f32)],
        compiler_params=_cp("arbitrary"),
    )(x1, ff, target, w)


def _taps(w_ref, cs, K):
    return [jnp.broadcast_to(w_ref[k:k + 1, cs], (8, CW)) for k in range(K)]


def _down(before, cur, s, sub):
    return jnp.where(sub < s, pltpu.roll(before, s, 0), pltpu.roll(cur, s, 0))


def _up(cur, after, s, sub):
    return jnp.where(sub < 8 - s, pltpu.roll(cur, 8 - s, 0), pltpu.roll(after, 8 - s, 0))


def _conv_group(p, a, b, taps, bias, K, sub):
    ya, yb = bias, bias
    for k in range(K):
        s = K - 1 - k
        xa, xb = (a, b) if s == 0 else (_down(p, a, s, sub), _down(a, b, s, sub))
        ya = ya + taps[k] * xa
        yb = yb + taps[k] * xb
    return ya, yb


def _prev8_map(ts, cb):
    return lambda i, j: (jnp.maximum(i * (ts // 8) - 1, 0), cb + j)


def ssdconv_fwd(proj, w8, b, ts):
    S = proj.shape[0]
    bw = 1024
    cb = C_XBC // bw

    def body(cur_ref, prev_ref, w_ref, b_ref, o_ref, c_ref):
        first = pl.program_id(0) == 0
        sub = lax.broadcasted_iota(jnp.int32, (8, CW), 0)
        for c0 in range(0, bw, CW):
            cs = slice(c0, c0 + CW)
            taps = _taps(w_ref, cs, SSD_K)
            bias = jnp.broadcast_to(b_ref[:, cs], (8, CW))

            def grp(r0, p, cs=cs, taps=taps, bias=bias):
                rows = pl.ds(r0, RG)
                xv = cur_ref[rows, cs].astype(f32)
                ya, yb = _conv_group(p, xv[0:8], xv[8:16], taps, bias, SSD_K, sub)
                y = jnp.concatenate([ya, yb], axis=0)
                c_ref[rows, cs] = y.astype(bf16)
                o_ref[rows, cs] = (y * _sigmoid_fast(y)).astype(bf16)
                return xv[8:16]

            _groups(ts, grp, jnp.where(first, 0.0, prev_ref[:, cs].astype(f32)))

    o = jax.ShapeDtypeStruct((S, CONVD), bf16)
    blk = pl.BlockSpec((ts, bw), lambda i, j: (i, j))
    return pl.pallas_call(
        body, name="ssdconv_fwd", grid=(S // ts, CONVD // bw),
        in_specs=[pl.BlockSpec((ts, bw), lambda i, j: (i, cb + j)),
                  pl.BlockSpec((8, bw), _prev8_map(ts, cb)),
                  pl.BlockSpec((8, bw), lambda i, j: (0, j)),
                  pl.BlockSpec((1, bw), lambda i, j: (0, j))],
        out_specs=[blk, blk], out_shape=[o, o],
        compiler_params=_cp("parallel", "parallel"),
    )(proj, proj, w8, b)


def _gelu_tanh(x):
    c = 0.7978845608028654
    t = jnp.tanh(c * (x + 0.044715 * x * x * x))
    return 0.5 * x * (1.0 + t), t


def ffnact_fwd(up, w8, b, ts):
    S = up.shape[0]

    def body(g_ref, gp_ref, v_ref, vp_ref, wg_ref, wv_ref, bg_ref, bv_ref, o_ref, gc_ref, vc_ref):
        first = pl.program_id(0) == 0
        sub = lax.broadcasted_iota(jnp.int32, (8, CW), 0)
        for c0 in range(0, FF, CW):
            cs = slice(c0, c0 + CW)
            tg, tv = _taps(wg_ref, cs, FFN_K), _taps(wv_ref, cs, FFN_K)
            bg = jnp.broadcast_to(bg_ref[:, cs], (8, CW))
            bv = jnp.broadcast_to(bv_ref[:, cs], (8, CW))

            def grp(r0, carry, cs=cs, tg=tg, tv=tv, bg=bg, bv=bv):
                pg, pv = carry
                rows = pl.ds(r0, RG)
                gx = g_ref[rows, cs].astype(f32)
                vx = v_ref[rows, cs].astype(f32)
                g = jnp.concatenate(_conv_group(pg, gx[0:8], gx[8:16], tg, bg, FFN_K, sub), axis=0)
                v = jnp.concatenate(_conv_group(pv, vx[0:8], vx[8:16], tv, bv, FFN_K, sub), axis=0)
                gc_ref[rows, cs] = g.astype(bf16)
                vc_ref[rows, cs] = v.astype(bf16)
                o_ref[rows, cs] = (_gelu_tanh(g)[0] * v).astype(bf16)
                return gx[8:16], vx[8:16]

            _groups(ts, grp, (jnp.where(first, 0.0, gp_ref[:, cs].astype(f32)),
                              jnp.where(first, 0.0, vp_ref[:, cs].astype(f32))))

    o = jax.ShapeDtypeStruct((S, FF), bf16)
    blk = pl.BlockSpec((ts, FF), lambda i: (i, 0))
    prev = lambda cb: pl.BlockSpec((8, FF), lambda i: (jnp.maximum(i * (ts // 8) - 1, 0), cb))
    return pl.pallas_call(
        body, name="ffnact_fwd", grid=(S // ts,),
        in_specs=[blk, prev(0), pl.BlockSpec((ts, FF), lambda i: (i, 1)), prev(1),
                  pl.BlockSpec((8, FF), lambda i: (0, 0)), pl.BlockSpec((8, FF), lambda i: (0, 1)),
                  pl.BlockSpec((1, FF), lambda i: (0, 0)), pl.BlockSpec((1, FF), lambda i: (0, 1))],
        out_specs=[blk, blk, blk], out_shape=[o, o, o],
        compiler_params=_cp("parallel"),
    )(up, up, up, up, w8, w8, b, b)


def ffnact_bwd(dact, gc, vc, ts):
    S = dact.shape[0]

    def body(d_ref, g_ref, v_ref, dg_ref, dv_ref):
        c = 0.7978845608028654
        for c0 in range(0, FF, CW):
            cs = slice(c0, c0 + CW)

            def grp(r0, _, cs=cs):
                rows = pl.ds(r0, RG)
                d = d_ref[rows, cs].astype(f32)
                g = g_ref[rows, cs].astype(f32)
                ge, t = _gelu_tanh(g)
                dgelu = 0.5 * (1.0 + t) + 0.5 * g * (1.0 - t * t) * c * (1.0 + 3.0 * 0.044715 * g * g)
                dg_ref[rows, cs] = (d * v_ref[rows, cs].astype(f32) * dgelu).astype(bf16)
                dv_ref[rows, cs] = (d * ge).astype(bf16)

            _groups(ts, grp)

    o = jax.ShapeDtypeStruct((S, FF), bf16)
    blk = pl.BlockSpec((ts, FF), lambda i: (i, 0))
    return pl.pallas_call(
        body, name="ffnact_bwd", grid=(S // ts,),
        in_specs=[blk, blk, blk], out_specs=[blk, blk], out_shape=[o, o],
        compiler_params=_cp("parallel"),
    )(dact, gc, vc)


def dwconv_bwd(dy, x, xcb, w8, wcb, K, bw, ts, name, act_c=None, into=None, ocb=0, out_cols=None):
    S, C = dy.shape
    nr = S // ts
    out_cols = out_cols or C
    n_act = 0 if act_c is None else 2

    def body(*refs):
        dy_ref, dyn_ref = refs[0:2]
        c_ref, cn_ref = (refs[2:4] if n_act else (None, None))
        x_ref, xp_ref, w_ref = refs[2 + n_act:5 + n_act]
        dx_ref, dw_ref, sd_ref = refs[-3:]
        i = pl.program_id(1)
        first, last = i == 0, i == nr - 1
        sub = lax.broadcasted_iota(jnp.int32, (8, CW), 0)

        def grad_y(d, c):
            if c is None:
                return d.astype(f32)
            cv = c.astype(f32)
            s = _sigmoid_fast(cv)
            return d.astype(f32) * s * (1.0 + cv * (1.0 - s))

        @pl.when(first)
        def _():
            dw_ref[...] = jnp.zeros_like(dw_ref)

        for c0 in range(0, bw, CW):
            cs = slice(c0, c0 + CW)
            taps = _taps(w_ref, cs, K)
            zero = jnp.zeros((8, CW), f32)

            def fwd(r0, carry, cs=cs):
                p, accs, accb = carry
                rows = pl.ds(r0, RG)
                g = grad_y(dy_ref[rows, cs], c_ref[rows, cs] if n_act else None)
                sd_ref[rows, cs] = g
                xv = x_ref[rows, cs].astype(f32)
                a, b = xv[0:8], xv[8:16]
                ga, gb = g[0:8], g[8:16]
                new = []
                for k in range(K):
                    s = K - 1 - k
                    xa, xb = (a, b) if s == 0 else (_down(p, a, s, sub), _down(a, b, s, sub))
                    new.append(accs[k] + ga * xa + gb * xb)
                return b, tuple(new), accb + ga + gb

            _, accs, accb = _groups(ts, fwd, (jnp.where(first, 0.0, xp_ref[:, cs].astype(f32)), (zero,) * K, zero))
            for k in range(K):
                dw_ref[k:k + 1, cs] += jnp.sum(accs[k], axis=0, keepdims=True)
            dw_ref[7:8, cs] += jnp.sum(accb, axis=0, keepdims=True)

            def bwd(r0, after, cs=cs, taps=taps):
                rows = pl.ds(r0, RG)
                g = sd_ref[rows, cs]
                a, b = g[0:8], g[8:16]
                da, db = zero, zero
                for k in range(K):
                    s = K - 1 - k
                    ua, ub = (a, b) if s == 0 else (_up(a, b, s, sub), _up(b, after, s, sub))
                    da = da + taps[k] * ua
                    db = db + taps[k] * ub
                dx_ref[rows, cs] = jnp.concatenate([da, db], axis=0).astype(bf16)
                return a

            halo = grad_y(dyn_ref[:, cs], cn_ref[:, cs] if n_act else None)
            _groups(ts, bwd, jnp.where(last, 0.0, halo), reverse=True)

    nxt = lambda j, i: (jnp.minimum((i + 1) * (ts // 8), S // 8 - 1), j)
    tile = pl.BlockSpec((ts, bw), lambda j, i: (i, j))
    acts = [] if act_c is None else [act_c, act_c]
    extra = [] if into is None else [into]
    n_in = 5 + n_act
    return pl.pallas_call(
        body, name=name, grid=(C // bw, nr),
        in_specs=[tile, pl.BlockSpec((8, bw), nxt)] + ([tile, pl.BlockSpec((8, bw), nxt)] if n_act else []) + [
            pl.BlockSpec((ts, bw), lambda j, i: (i, xcb + j)),
            pl.BlockSpec((8, bw), lambda j, i: (jnp.maximum(i * (ts // 8) - 1, 0), xcb + j)),
            pl.BlockSpec((8, bw), lambda j, i: (0, wcb + j))] + [pl.BlockSpec(memory_space=pl.ANY)] * len(extra),
        out_specs=[pl.BlockSpec((ts, bw), lambda j, i: (i, ocb + j)), pl.BlockSpec((8, bw), lambda j, i: (0, j))],
        out_shape=[jax.ShapeDtypeStruct((S, out_cols), bf16), jax.ShapeDtypeStruct((8, C), f32)],
        scratch_shapes=[pltpu.VMEM((ts, bw), f32)],
        input_output_aliases={n_in: 0} if extra else {},
        compiler_params=_cp("parallel", "arbitrary"),
    )(dy, dy, *acts, x, x, w8, *extra)


def gnorm_fwd(y, proj, w, ts):
    S = y.shape[0]

    def body(y_ref, z_ref, w_ref, o_ref):
        for k in range(NG):
            sl = slice(k * GW, (k + 1) * GW)
            wv = w_ref[:, sl]

            def grp(r0, _, sl=sl, wv=wv):
                rows = pl.ds(r0, NRG)
                z = z_ref[rows, sl].astype(f32)
                g = y_ref[rows, sl].astype(f32) * z * _sigmoid_fast(z)
                o_ref[rows, sl] = (g * _rms(g) * wv).astype(bf16)

            _groups(ts, grp, rg=NRG)

    row = pl.BlockSpec((ts, DI), lambda i: (i, 0))
    return pl.pallas_call(
        body, name="gnorm_fwd", grid=(S // ts,),
        in_specs=[row, row, pl.BlockSpec((1, DI), lambda i: (0, 0))],
        out_specs=row, out_shape=jax.ShapeDtypeStruct((S, DI), bf16),
        compiler_params=_cp("parallel"),
    )(y, proj, w)


def gnorm_bwd(dyn, y, proj, w, dproj, ts):
    S = y.shape[0]
    nt = S // ts

    def body(d_ref, y_ref, z_ref, w_ref, _, dy_ref, dz_ref, gw_ref, acc_ref):
        i = pl.program_id(0)

        @pl.when(i == 0)
        def _():
            acc_ref[...] = jnp.zeros_like(acc_ref)
            gw_ref[...] = jnp.zeros_like(gw_ref)

        for k in range(NG):
            sl = slice(k * GW, (k + 1) * GW)
            wv = w_ref[:, sl]

            def grp(r0, _, sl=sl, wv=wv):
                rows = pl.ds(r0, NRG)
                z = z_ref[rows, sl].astype(f32)
                yv = y_ref[rows, sl].astype(f32)
                s = _sigmoid_fast(z)
                sz = z * s
                g = yv * sz
                r = _rms(g)
                d = d_ref[rows, sl].astype(f32)
                acc_ref[:, sl] += _fold(d * g * r)
                dg = _rms_bwd(g, r, d * wv)
                dy_ref[rows, sl] = (dg * sz).astype(bf16)
                dz_ref[rows, sl] = (dg * yv * s * (1.0 + z * (1.0 - s))).astype(bf16)

            _groups(ts, grp, rg=NRG)

        @pl.when(i == nt - 1)
        def _():
            _flush(acc_ref, gw_ref, 0)

    row = pl.BlockSpec((ts, DI), lambda i: (i, 0))
    return pl.pallas_call(
        body, name="gnorm_bwd", grid=(nt,),
        in_specs=[row, row, row, pl.BlockSpec((1, DI), lambda i: (0, 0)), pl.BlockSpec(memory_space=pl.ANY)],
        out_specs=[row, row, pl.BlockSpec((8, DI), lambda i: (0, 0))],
        out_shape=[jax.ShapeDtypeStruct((S, DI), bf16), jax.ShapeDtypeStruct(dproj.shape, bf16),
                   jax.ShapeDtypeStruct((8, DI), f32)],
        scratch_shapes=[pltpu.VMEM((8, DI), f32)],
        input_output_aliases={4: 1},
        compiler_params=_cp("arbitrary"),
    )(dyn, y, proj, w, dproj)


def merge_fwd(proj, ys, ya, ts):
    S = ys.shape[0]

    def body(gs_ref, ga_ref, ys_ref, ya_ref, o_ref):
        for c0 in range(0, D, CW):
            cs = slice(c0, c0 + CW)

            def grp(r0, _, cs=cs):
                rows = pl.ds(r0, NRG)
                o_ref[rows, cs] = (_sigmoid_fast(gs_ref[rows, cs].astype(f32)) * ys_ref[rows, cs].astype(f32)
                                   + _sigmoid_fast(ga_ref[rows, cs].astype(f32)) * ya_ref[rows, cs].astype(f32)
                                   ).astype(bf16)

            _groups(ts, grp, rg=NRG)

    row = pl.BlockSpec((ts, D), lambda i: (i, 0))
    return pl.pallas_call(
        body, name="merge_fwd", grid=(S // ts,),
        in_specs=[pl.BlockSpec((ts, D), lambda i: (i, C_GS // D)), pl.BlockSpec((ts, D), lambda i: (i, C_GA // D)), row, row],
        out_specs=row, out_shape=jax.ShapeDtypeStruct((S, D), bf16),
        compiler_params=_cp("parallel"),
    )(proj, proj, ys, ya)


def merge_bwd(dm, proj, ys, ya, ts):
    S = ys.shape[0]

    def body(d_ref, gs_ref, ga_ref, ys_ref, ya_ref, dys_ref, dya_ref, dg_ref):
        for c0 in range(0, D, CW):
            cs = slice(c0, c0 + CW)

            def grp(r0, _, c0=c0, cs=cs):
                rows = pl.ds(r0, NRG)
                d = d_ref[rows, cs].astype(f32)
                ss = _sigmoid_fast(gs_ref[rows, cs].astype(f32))
                sa = _sigmoid_fast(ga_ref[rows, cs].astype(f32))
                dys_ref[rows, cs] = (d * ss).astype(bf16)
                dya_ref[rows, cs] = (d * sa).astype(bf16)
                dg_ref[rows, cs] = (d * ys_ref[rows, cs].astype(f32) * ss * (1.0 - ss)).astype(bf16)
                dg_ref[rows, D + c0:D + c0 + CW] = (d * ya_ref[rows, cs].astype(f32) * sa * (1.0 - sa)).astype(bf16)

            _groups(ts, grp, rg=NRG)

    row = pl.BlockSpec((ts, D), lambda i: (i, 0))
    o = jax.ShapeDtypeStruct((S, D), bf16)
    return pl.pallas_call(
        body, name="merge_bwd", grid=(S // ts,),
        in_specs=[row, pl.BlockSpec((ts, D), lambda i: (i, C_GS // D)), pl.BlockSpec((ts, D), lambda i: (i, C_GA // D)), row, row],
        out_specs=[row, row, pl.BlockSpec((ts, 2 * D), lambda i: (i, C_GS // (2 * D)))],
        out_shape=[o, o, jax.ShapeDtypeStruct((S, PM), bf16)],
        compiler_params=_cp("parallel"),
    )(dm, proj, proj, ys, ya)


def _ssd_consts():
    h = lax.broadcasted_iota(jnp.int32, (LANES, DI), 0)
    c = lax.broadcasted_iota(jnp.int32, (LANES, DI), 1)
    expand = (c // HD == h).astype(bf16)
    r = lax.broadcasted_iota(jnp.int32, (CH, CH), 0)
    cc = lax.broadcasted_iota(jnp.int32, (CH, CH), 1)
    tril = (cc <= r).astype(bf16)
    triu = (cc >= r).astype(bf16)
    return expand, expand.T, tril, triu


def _ssd_common(xbc_ref, dtr_ref, bias_ref, alog_ref, expand_ref, tril_ref):
    dtr = dtr_ref[...] + bias_ref[...]
    dt = jnp.maximum(dtr, 0.0) + jnp.log1p(jnp.exp(-jnp.abs(dtr)))
    a = -jnp.exp(alog_ref[...])
    acs = _dot3_left(tril_ref[...], dt * a)
    acsx = _dot3_right(acs, expand_ref[...])
    dtx = _dot3_right(dt, expand_ref[...])
    x = xbc_ref[:, 0:DI].astype(f32)
    xdt = x * dtx
    e = jnp.exp(acsx)
    dsx = jnp.exp(acsx[CH - 1:CH, :] - acsx)
    return dtr, dt, a, acs, dtx, x, xdt, e, dsx


def _ssd_lmat(acs, acs_t, hh, causal):
    seg = acs[:, hh:hh + 1] - acs_t[hh:hh + 1, :]
    return jnp.where(causal, jnp.exp(jnp.minimum(seg, 0.0)), 0.0)


def ssd_fwd(xbc, dtr, bias, alog, dx_row, comm=None):
    S = xbc.shape[0]
    nc = S // CH
    expand, _, tril, _ = _ssd_consts()
    cm = _Comm(comm)

    def body(*refs):
        ins, (y_ref, hp_ref), (h_ref, yd_ref), copies = cm.split(refs, 7, 2)
        xbc_ref, dtr_ref, bias_ref, alog_ref, dxr_ref, expand_ref, tril_ref = ins
        c = pl.program_id(0)
        cm.start(copies, c == 0)

        @pl.when(c == 0)
        def _():
            h_ref[...] = jnp.zeros_like(h_ref)

        _, _, _, acs, _, x, xdt, e, dsx = _ssd_common(xbc_ref, dtr_ref, bias_ref, alog_ref, expand_ref, tril_ref)
        acs_t = acs.T
        xb = xdt.astype(bf16)
        xd = (xdt * dsx).astype(bf16)
        causal = tril_ref[...] > 0
        for g in range(NG):
            gs = slice(g * GW, (g + 1) * GW)
            bg = xbc_ref[:, DI + g * NS:DI + (g + 1) * NS]
            cg = xbc_ref[:, DI + NG * NS + g * NS:DI + NG * NS + (g + 1) * NS]
            cb = _dot(cg, bg, "nt")
            hp = h_ref[g]
            hpb = hp.astype(bf16)
            hp_ref[0, g] = hpb
            yd_ref[:, gs] = _dot(cg, hpb) * e[:, gs]
            h_ref[g] = hp * e[CH - 1:CH, gs] + _dot(bg, xd[:, gs], "tn")
            for j in range(NH // NG):
                hh = g * (NH // NG) + j
                hs = slice(hh * HD, (hh + 1) * HD)
                m = (cb * _ssd_lmat(acs, acs_t, hh, causal)).astype(bf16)
                yd_ref[:, hs] += _dot(m, xb[:, hs])
        y_ref[...] = (yd_ref[...] + dxr_ref[...] * x).astype(bf16)
        cm.wait(copies, c == nc - 1)

    par = lambda shape: pl.BlockSpec(shape, lambda c: (0,) * len(shape))
    res = pl.pallas_call(
        body, name="ssd_fwd", grid=(nc,),
        in_specs=[pl.BlockSpec((CH, CONVD), lambda c: (c, 0)), pl.BlockSpec((CH, LANES), lambda c: (c, 0)),
                  par((1, LANES)), par((1, LANES)), par((1, DI)), par((LANES, DI)), par((CH, CH))] + cm.in_specs(),
        out_specs=[pl.BlockSpec((CH, DI), lambda c: (c, 0)),
                   pl.BlockSpec((1, NG, NS, GW), lambda c: (c, 0, 0, 0))] + cm.out_specs(),
        out_shape=[jax.ShapeDtypeStruct((S, DI), bf16), jax.ShapeDtypeStruct((nc, NG, NS, GW), bf16)] + cm.out_shape(),
        scratch_shapes=[pltpu.VMEM((NG, NS, GW), f32), pltpu.VMEM((CH, DI), f32)] + cm.scratch(),
        compiler_params=_cp("arbitrary", side_effects=bool(cm.n)),
    )(xbc, dtr, bias, alog, dx_row, expand, tril, *cm.bufs)
    return res[0], res[1], res[2:]


def ssd_bwd(xbc, dtr, dy, hprev, bias, alog, dx_row, comm=None):
    S = xbc.shape[0]
    nc = S // CH
    expand, expand_t, tril, triu = _ssd_consts()
    cm = _Comm(comm)

    def body(*refs):
        ins, outs, scr, copies = cm.split(refs, 11, 3)
        xbc_ref, dtr_ref, dy_ref, hp_ref, bias_ref, alog_ref, dxr_ref, expand_ref, expt_ref, tril_ref, triu_ref = ins
        dxbc_ref, ddtr_ref, acc_ref = outs
        dh_ref, dxs_ref, t_ref, accb_ref, acca_ref, accd_ref = scr
        c = pl.program_id(0)
        cm.start(copies, c == 0)

        @pl.when(c == 0)
        def _():
            dh_ref[...] = jnp.zeros_like(dh_ref)
            accb_ref[...] = jnp.zeros_like(accb_ref)
            acca_ref[...] = jnp.zeros_like(acca_ref)
            accd_ref[...] = jnp.zeros_like(accd_ref)

        dtr, dt, a, acs, dtx, x, xdt, e, dsx = _ssd_common(xbc_ref, dtr_ref, bias_ref, alog_ref, expand_ref, tril_ref)
        acs_t = acs.T
        xb = xdt.astype(bf16)
        xdf = xdt * dsx
        xd = xdf.astype(bf16)
        dyv = dy_ref[...].astype(f32)
        dyb = dy_ref[...]
        dye = (dyv * e).astype(bf16)
        causal = tril_ref[...] > 0
        lane = lax.broadcasted_iota(jnp.int32, (CH, LANES), 1)
        subl = lax.broadcasted_iota(jnp.int32, (LANES, CH), 0)
        ccol = jnp.zeros((CH, LANES), f32)
        rrow = jnp.zeros((LANES, CH), f32)
        last_row = lax.broadcasted_iota(jnp.int32, (CH, 1), 0) == CH - 1
        for g in range(NG):
            gs = slice(g * GW, (g + 1) * GW)
            bsl = slice(DI + g * NS, DI + (g + 1) * NS)
            csl = slice(DI + NG * NS + g * NS, DI + NG * NS + (g + 1) * NS)
            bg = xbc_ref[:, bsl]
            cg = xbc_ref[:, csl]
            cb = _dot(cg, bg, "nt")
            hpb = hp_ref[0, g]
            dhn = dh_ref[g]
            dhnb = dhn.astype(bf16)
            yoff = _dot(cg, hpb) * e[:, gs]
            dxd = _dot(bg, dhnb)
            t2 = dxd * xdf[:, gs]
            t3 = jnp.sum(dhn * hpb.astype(f32), axis=0, keepdims=True) * e[CH - 1:CH, gs]
            t_ref[:, gs] = dyv[:, gs] * yoff - t2 + jnp.where(last_row, jnp.sum(t2, axis=0, keepdims=True) + t3, 0.0)
            dxs_ref[:, gs] = dxd * dsx[:, gs]
            dcg = _dot(dye[:, gs], hpb, "nt")
            dbg = _dot(xd[:, gs], dhnb, "nt")
            dh_ref[g] = dhn * e[CH - 1:CH, gs] + _dot(cg, dye[:, gs], "tn")
            dcb = jnp.zeros((CH, CH), f32)
            for j in range(NH // NG):
                hh = g * (NH // NG) + j
                hs = slice(hh * HD, (hh + 1) * HD)
                lm = _ssd_lmat(acs, acs_t, hh, causal)
                m = cb * lm
                dm = _dot(dyb[:, hs], xb[:, hs], "nt")
                gm = dm * m
                ccol = ccol + jnp.sum(gm, axis=1, keepdims=True) * (lane == hh).astype(f32)
                rrow = rrow + jnp.sum(gm, axis=0, keepdims=True) * (subl == hh).astype(f32)
                dcb = dcb + dm * lm
                dxs_ref[:, hs] += _dot(m.astype(bf16), dyb[:, hs], "tn")
            dcbb = dcb.astype(bf16)
            dxbc_ref[:, csl] = (dcg + _dot(dcbb, bg)).astype(bf16)
            dxbc_ref[:, bsl] = (dbg + _dot(dcbb, cg, "tn")).astype(bf16)
        dxf = dxs_ref[...]
        dxbc_ref[:, 0:DI] = (dxf * dtx + dxr_ref[...] * dyv).astype(bf16)
        expt = expt_ref[...]
        dacs = ccol - rrow.T + _dot2_right(t_ref[...], expt)
        dadt = _dot3_left(triu_ref[...], dacs)
        ddt = _dot2_right(dxf * x, expt) + dadt * a
        ddtr = ddt * _sigmoid(dtr)
        ddtr_ref[...] = ddtr
        accb_ref[...] += ddtr
        acca_ref[...] += dadt * dt
        accd_ref[...] += _dot2_right(dyv * x, expt)

        @pl.when(c == nc - 1)
        def _():
            acc_ref[...] = jnp.zeros_like(acc_ref)
            acc_ref[0:1, :] = jnp.sum(accb_ref[...], axis=0, keepdims=True)
            acc_ref[1:2, :] = jnp.sum(acca_ref[...], axis=0, keepdims=True) * a
            acc_ref[2:3, :] = jnp.sum(accd_ref[...], axis=0, keepdims=True)

        cm.wait(copies, c == nc - 1)

    par = lambda shape: pl.BlockSpec(shape, lambda c: (0,) * len(shape))
    rev = lambda c: (nc - 1 - c, 0)
    res = pl.pallas_call(
        body, name="ssd_bwd", grid=(nc,),
        in_specs=[pl.BlockSpec((CH, CONVD), rev), pl.BlockSpec((CH, LANES), rev), pl.BlockSpec((CH, DI), rev),
                  pl.BlockSpec((1, NG, NS, GW), lambda c: (nc - 1 - c, 0, 0, 0)),
                  par((1, LANES)), par((1, LANES)), par((1, DI)), par((LANES, DI)), par((DI, LANES)),
                  par((CH, CH)), par((CH, CH))] + cm.in_specs(),
        out_specs=[pl.BlockSpec((CH, CONVD), rev), pl.BlockSpec((CH, LANES), rev), par((8, LANES))] + cm.out_specs(),
        out_shape=[jax.ShapeDtypeStruct((S, CONVD), bf16), jax.ShapeDtypeStruct((S, LANES), f32),
                   jax.ShapeDtypeStruct((8, LANES), f32)] + cm.out_shape(),
        scratch_shapes=[pltpu.VMEM((NG, NS, GW), f32), pltpu.VMEM((CH, DI), f32), pltpu.VMEM((CH, DI), f32),
                        pltpu.VMEM((CH, LANES), f32), pltpu.VMEM((CH, LANES), f32),
                        pltpu.VMEM((CH, LANES), f32)] + cm.scratch(),
        compiler_params=_cp("arbitrary", side_effects=bool(cm.n)),
    )(xbc, dtr, dy, hprev, bias, alog, dx_row, expand, expand_t, tril, triu, *cm.bufs)
    return res[0], res[1], res[2], res[3:]


def rope_tables(pos_row, ts):
    S = pos_row.shape[1]
    half = AD // 2
    inv = ROPE_THETA ** (-jnp.arange(half, dtype=f32) * 2.0 / AD)
    inv_col = jnp.tile(inv, 2)[:, None]

    def body(p_ref, inv_ref, cos_ref, sin_ref):
        ang = inv_ref[...] * p_ref[...].astype(f32)
        row = lax.broadcasted_iota(jnp.int32, ang.shape, 0)
        cos_ref[...] = jnp.cos(ang)
        sin_ref[...] = jnp.where(row < half, -1.0, 1.0) * jnp.sin(ang)

    o = jax.ShapeDtypeStruct((AD, S), f32)
    return pl.pallas_call(
        body, name="rope_tables", grid=(S // ts,),
        in_specs=[pl.BlockSpec((1, ts), lambda i: (0, i)), pl.BlockSpec((AD, 1), lambda i: (0, 0))],
        out_specs=[pl.BlockSpec((AD, ts), lambda i: (0, i))] * 2, out_shape=[o, o],
        compiler_params=_cp("parallel"),
    )(pos_row, inv_col)


def _partner(t):
    half = AD // 2
    return jnp.concatenate([t[h * AD + o:h * AD + o + half] for h in range(t.shape[0] // AD) for o in (half, 0)], axis=0)


def _rope(t, cos, sin):
    reps = t.shape[0] // AD
    return t * jnp.tile(cos, (reps, 1)) + _partner(t) * jnp.tile(sin, (reps, 1))


def _rope_t(d, cos, sin):
    reps = d.shape[0] // AD
    return d * jnp.tile(cos, (reps, 1)) - _partner(d) * jnp.tile(sin, (reps, 1))


def _lanes_of_group(t, g):
    return jnp.concatenate([t[(g * REP + r) * AD:(g * REP + r + 1) * AD] for r in range(REP)], axis=1)


def _attn_probs(qg, kp, kc, sink_ref, g, not_first):
    n = qg.shape[1]
    s = lax.broadcasted_iota(jnp.int32, (WIN, n), 0)
    t = lax.broadcasted_iota(jnp.int32, (WIN, n), 1) % WIN
    neg = -1e30
    sink = jnp.concatenate([jnp.broadcast_to(sink_ref[0:1, g * REP + r:g * REP + r + 1], (1, WIN)) for r in range(REP)],
                           axis=1)
    sp = jnp.where(jnp.logical_and(s > t, not_first), _dot(kp, qg, "tn"), neg)
    sc = jnp.where(s <= t, _dot(kc, qg, "tn"), neg)
    m = jnp.maximum(jnp.maximum(jnp.max(sp, axis=0, keepdims=True), jnp.max(sc, axis=0, keepdims=True)), sink)
    pp = jnp.exp(sp - m)
    pc = jnp.exp(sc - m)
    ps = jnp.exp(sink - m)
    inv = 1.0 / (jnp.sum(pp, axis=0, keepdims=True) + jnp.sum(pc, axis=0, keepdims=True) + ps)
    return pp * inv, pc * inv, ps * inv


def attn_fwd(qt, kvt, cos, sin, sinks):
    S = qt.shape[1]
    nb = S // WIN
    cur = lambda i: (0, i)
    prev = lambda i: (0, jnp.maximum(i - 1, 0))

    def body(q_ref, kv_ref, kvp_ref, cos_ref, sin_ref, cosp_ref, sinp_ref, sink_ref, o_ref):
        i = pl.program_id(0)
        q = (_rope(q_ref[...].astype(f32), cos_ref[...], sin_ref[...]) * (AD ** -0.5)).astype(bf16)
        kc = _rope(kv_ref[0:KVW, :].astype(f32), cos_ref[...], sin_ref[...]).astype(bf16)
        kp = _rope(kvp_ref[0:KVW, :].astype(f32), cosp_ref[...], sinp_ref[...]).astype(bf16)
        for g in range(KVH):
            ks = slice(g * AD, (g + 1) * AD)
            vs = slice(KVW + g * AD, KVW + (g + 1) * AD)
            pp, pc, _ = _attn_probs(_lanes_of_group(q, g), kp[ks], kc[ks], sink_ref, g, i > 0)
            o = _dot(kvp_ref[vs, :], pp.astype(bf16)) + _dot(kv_ref[vs, :], pc.astype(bf16))
            for r in range(REP):
                h = g * REP + r
                o_ref[h * AD:(h + 1) * AD, :] = o[:, r * WIN:(r + 1) * WIN].astype(bf16)

    tab = pl.BlockSpec((AD, WIN), cur)
    tabp = pl.BlockSpec((AD, WIN), prev)
    return pl.pallas_call(
        body, name="attn_fwd", grid=(nb,),
        in_specs=[pl.BlockSpec((D, WIN), cur), pl.BlockSpec((2 * KVW, WIN), cur), pl.BlockSpec((2 * KVW, WIN), prev),
                  tab, tab, tabp, tabp, pl.BlockSpec((1, LANES), lambda i: (0, 0))],
        out_specs=pl.BlockSpec((D, WIN), cur),
        out_shape=jax.ShapeDtypeStruct((D, S), bf16),
        compiler_params=_cp("parallel"),
    )(qt, kvt, kvt, cos, sin, cos, sin, sinks)


def attn_bwd(qt, kvt, cos, sin, sinks, daot, comm=None):
    S = qt.shape[1]
    nb = S // WIN
    cur = lambda i: (0, jnp.minimum(i, nb - 1))
    prev = lambda i: (0, jnp.maximum(i - 1, 0))
    cm = _Comm(comm)

    def body(*refs):
        ins, (dq_ref, dkv_ref, ds_ref), scr, copies = cm.split(refs, 9, 3)
        q_ref, kv_ref, kvp_ref, cos_ref, sin_ref, cosp_ref, sinp_ref, sink_ref, do_ref = ins
        ck_ref, cv_ref, dqs_ref, dkp_ref, dvp_ref, dkc_ref, dvc_ref, accs_ref = scr
        i = pl.program_id(0)
        cm.start(copies, i == 0)

        @pl.when(i == 0)
        def _():
            ck_ref[...] = jnp.zeros_like(ck_ref)
            cv_ref[...] = jnp.zeros_like(cv_ref)
            accs_ref[...] = jnp.zeros_like(accs_ref)

        @pl.when(i == nb)
        def _():
            dkp_ref[...] = jnp.zeros_like(dkp_ref)
            dvp_ref[...] = jnp.zeros_like(dvp_ref)

        @pl.when(i < nb)
        def _():
            q = (_rope(q_ref[...].astype(f32), cos_ref[...], sin_ref[...]) * (AD ** -0.5)).astype(bf16)
            kc = _rope(kv_ref[0:KVW, :].astype(f32), cos_ref[...], sin_ref[...]).astype(bf16)
            kp = _rope(kvp_ref[0:KVW, :].astype(f32), cosp_ref[...], sinp_ref[...]).astype(bf16)
            do = do_ref[...]
            for g in range(KVH):
                ks = slice(g * AD, (g + 1) * AD)
                vs = slice(KVW + g * AD, KVW + (g + 1) * AD)
                qg = _lanes_of_group(q, g)
                dog = _lanes_of_group(do, g)
                pp, pc, ps = _attn_probs(qg, kp[ks], kc[ks], sink_ref, g, i > 0)
                dpp = _dot(kvp_ref[vs, :], dog, "tn")
                dpc = _dot(kv_ref[vs, :], dog, "tn")
                delta = jnp.sum(pp * dpp + pc * dpc, axis=0, keepdims=True)
                dsp = (pp * (dpp - delta)).astype(bf16)
                dsc = (pc * (dpc - delta)).astype(bf16)
                accs_ref[g:g + 1, :] -= ps * delta
                dqg = (_dot(kp[ks], dsp) + _dot(kc[ks], dsc)) * (AD ** -0.5)
                for r in range(REP):
                    h = g * REP + r
                    dqs_ref[h * AD:(h + 1) * AD, :] = dqg[:, r * WIN:(r + 1) * WIN]
                dkp_ref[ks, :] = _dot(qg, dsp, "nt")
                dkc_ref[ks, :] = _dot(qg, dsc, "nt")
                dvp_ref[ks, :] = _dot(dog, pp.astype(bf16), "nt")
                dvc_ref[ks, :] = _dot(dog, pc.astype(bf16), "nt")
            dq_ref[...] = _rope_t(dqs_ref[...], cos_ref[...], sin_ref[...]).astype(bf16)

        dkv_ref[0:KVW, :] = _rope_t(ck_ref[...] + dkp_ref[...], cosp_ref[...], sinp_ref[...]).astype(bf16)
        dkv_ref[KVW:2 * KVW, :] = (cv_ref[...] + dvp_ref[...]).astype(bf16)

        @pl.when(i < nb)
        def _():
            ck_ref[...] = dkc_ref[...]
            cv_ref[...] = dvc_ref[...]

        @pl.when(i == nb)
        def _():
            lane = lax.broadcasted_iota(jnp.int32, (1, LANES), 1)
            row = jnp.zeros((1, LANES), f32)
            for h in range(AH):
                part = accs_ref[h // REP:h // REP + 1, (h % REP) * WIN:(h % REP + 1) * WIN]
                row = row + jnp.where(lane == h, jnp.sum(part, axis=1, keepdims=True), 0.0)
            ds_ref[...] = jnp.zeros_like(ds_ref)
            ds_ref[0:1, :] = row

        cm.wait(copies, i == nb)

    tab = pl.BlockSpec((AD, WIN), cur)
    tabp = pl.BlockSpec((AD, WIN), prev)
    kvs = lambda: pltpu.VMEM((KVW, WIN), f32)
    res = pl.pallas_call(
        body, name="attn_bwd", grid=(nb + 1,),
        in_specs=[pl.BlockSpec((D, WIN), cur), pl.BlockSpec((2 * KVW, WIN), cur), pl.BlockSpec((2 * KVW, WIN), prev),
                  tab, tab, tabp, tabp, pl.BlockSpec((1, LANES), lambda i: (0, 0)),
                  pl.BlockSpec((D, WIN), cur)] + cm.in_specs(),
        out_specs=[pl.BlockSpec((D, WIN), cur), pl.BlockSpec((2 * KVW, WIN), prev),
                   pl.BlockSpec((8, LANES), lambda i: (0, 0))] + cm.out_specs(),
        out_shape=[jax.ShapeDtypeStruct((D, S), bf16), jax.ShapeDtypeStruct((2 * KVW, S), bf16),
                   jax.ShapeDtypeStruct((8, LANES), f32)] + cm.out_shape(),
        scratch_shapes=[kvs(), kvs(), pltpu.VMEM((D, WIN), f32), kvs(), kvs(), kvs(), kvs(),
                        pltpu.VMEM((8, REP * WIN), f32)] + cm.scratch(),
        compiler_params=_cp("arbitrary", side_effects=bool(cm.n)),
    )(qt, kvt, kvt, cos, sin, cos, sin, sinks, daot, *cm.bufs)
    return res[0], res[1], res[2], res[3:]


ADAM_C1 = 1.0 / (1.0 - ADAM_B1 ** ADAM_STEP)
ADAM_C2 = 1.0 / (1.0 - ADAM_B2 ** ADAM_STEP)


def _adam_update(g, w, m, v):
    nm = ADAM_B1 * m + (1.0 - ADAM_B1) * g
    nv = ADAM_B2 * v + (1.0 - ADAM_B2) * (g * g)
    return -ADAM_LR * ((nm * ADAM_C1) / (jnp.sqrt(nv * ADAM_C2) + ADAM_EPS) + ADAM_WD * w), nm, nv


def adamw(parts, w, m, v, tr, name):
    n, R, C = parts.shape

    def body(p_ref, w_ref, m_ref, v_ref, g_ref, d_ref, nm_ref, nv_ref):
        def grp(g0, _):
            r0 = pl.multiple_of(g0 * RG, RG)
            rows = pl.ds(r0, RG)
            g = p_ref[0, rows, :].astype(f32)
            for k in range(1, n):
                g = g + p_ref[k, rows, :].astype(f32)
            d, nm, nv = _adam_update(g, w_ref[rows, :], m_ref[rows, :], v_ref[rows, :])
            g_ref[rows, :] = g
            d_ref[rows, :] = d
            nm_ref[rows, :] = nm
            nv_ref[rows, :] = nv
            return 0

        lax.fori_loop(0, tr // RG, grp, 0)

    row = pl.BlockSpec((tr, C), lambda i: (i, 0))
    o = jax.ShapeDtypeStruct((R, C), f32)
    return pl.pallas_call(
        body, name=name, grid=(R // tr,),
        in_specs=[pl.BlockSpec((n, tr, C), lambda i: (0, i, 0)), row, row, row],
        out_specs=[row, row, row, row], out_shape=[o, o, o, o],
        compiler_params=_cp("parallel"),
    )(parts, w, m, v)


SMALL_ROW = (("norm_mix_post_w", D), ("norm_ffn_pre_w", D), ("norm_ffn_post_w", D), ("ssd_norm_w", DI),
             ("ssd_conv_b", CONVD), ("ffn_conv_b", 2 * FF), ("ssd_dt_bias", NH), ("ssd_a_log", NH), ("ssd_d", NH),
             ("attn_sinks", AH), ("loss", 1))
CONV_BLOCK = 1152
SSD_CONV_COLS = CONVD // N_DEV
FFN_CONV_COLS = 2 * FF // N_DEV


def _row_offsets():
    off, o = {}, 0
    for name, n in SMALL_ROW:
        off[name] = (o, n)
        o += -(-n // LANES) * LANES
    return off, o


def adamw_small(recv_row, recv_pre, recv_conv, params):
    off, _ = _row_offsets()
    names = list(params)
    n = len(names)

    def total(ref, rows, lo, width):
        g = ref[0, rows, lo:lo + width]
        for d in range(1, N_DEV):
            g = g + ref[d, rows, lo:lo + width]
        return g

    def grad_of(name, row_ref, pre_ref, conv_ref):
        if name == "norm_mix_pre_w":
            return total(pre_ref, slice(0, 1), 0, D)
        if name == "ssd_conv_w":
            return total(conv_ref, slice(0, SSD_K), 0, SSD_CONV_COLS)
        if name == "ffn_conv_w":
            return total(conv_ref, slice(0, FFN_K), 3 * LANES, FFN_CONV_COLS)
        o, width = off[name]
        return total(row_ref, slice(0, 1), o, width)

    def body(row_ref, pre_ref, conv_ref, *refs):
        ins, outs = refs[:3 * n], refs[3 * n:]
        for k, name in enumerate(names):
            w_ref, m_ref, v_ref = ins[3 * k:3 * k + 3]
            g_ref, d_ref, nm_ref, nv_ref = outs[4 * k:4 * k + 4]
            g = grad_of(name, row_ref, pre_ref, conv_ref)
            d, nm, nv = _adam_update(g, w_ref[...], m_ref[...], v_ref[...])
            g_ref[...] = g
            d_ref[...] = d
            nm_ref[...] = nm
            nv_ref[...] = nv
        outs[4 * n][...] = total(row_ref, slice(0, 1), off["loss"][0], LANES)

    flat = [t for name in names for t in params[name]]
    out_shape = [jax.ShapeDtypeStruct(params[name][0].shape, f32) for name in names for _ in range(4)]
    res = pl.pallas_call(
        body, name="adamw_small",
        out_shape=out_shape + [jax.ShapeDtypeStruct((1, LANES), f32)],
        compiler_params=pltpu.CompilerParams(vmem_limit_bytes=VMEM_LIMIT),
    )(recv_row, recv_pre, recv_conv, *flat)
    return {name: res[4 * k:4 * k + 4] for k, name in enumerate(names)}, res[4 * n]


def _pad_rows8(w):
    return jnp.pad(w, ((0, 8 - w.shape[0]), (0, 0)))


def _pad_lanes(v):
    return jnp.pad(v.reshape(1, -1), ((0, 0), (0, LANES - v.size)))


WEIGHTS = ('norm_mix_pre_w', 'w_in', 'ssd_conv_w', 'ssd_conv_b', 'ssd_dt_bias', 'ssd_a_log', 'ssd_d', 'ssd_norm_w',
           'ssd_w_out', 'attn_sinks', 'attn_w_out', 'w_mix_out', 'norm_mix_post_w', 'norm_ffn_pre_w', 'ffn_w_up',
           'ffn_conv_w', 'ffn_conv_b', 'ffn_w_down', 'norm_ffn_post_w')
W_IN_ROWS = IN_DIM // N_DEV
W_IN_PAD = 1104
TS = 256


def kernel(x, positions, norm_mix_pre_w, w_in, ssd_conv_w, ssd_conv_b, ssd_dt_bias, ssd_a_log, ssd_d, ssd_norm_w, ssd_w_out, attn_sinks, attn_w_out, w_mix_out, norm_mix_post_w, norm_ffn_pre_w, ffn_w_up, ffn_conv_w, ffn_conv_b, ffn_w_down, norm_ffn_post_w, loss_target, m_norm_mix_pre_w, m_w_in, m_ssd_conv_w, m_ssd_conv_b, m_ssd_dt_bias, m_ssd_a_log, m_ssd_d, m_ssd_norm_w, m_ssd_w_out, m_attn_sinks, m_attn_w_out, m_w_mix_out, m_norm_mix_post_w, m_norm_ffn_pre_w, m_ffn_w_up, m_ffn_conv_w, m_ffn_conv_b, m_ffn_w_down, m_norm_ffn_post_w, v_norm_mix_pre_w, v_w_in, v_ssd_conv_w, v_ssd_conv_b, v_ssd_dt_bias, v_ssd_a_log, v_ssd_d, v_ssd_norm_w, v_ssd_w_out, v_attn_sinks, v_attn_w_out, v_w_mix_out, v_norm_mix_post_w, v_norm_ffn_pre_w, v_ffn_w_up, v_ffn_conv_w, v_ffn_conv_b, v_ffn_w_down, v_norm_ffn_post_w):
    a = locals()
    r2 = lambda t: t.reshape(t.shape[-2], t.shape[-1])
    w = {n: r2(a[n]) for n in WEIGHTS}
    m = {n: r2(a["m_" + n]) for n in WEIGHTS}
    v = {n: r2(a["v_" + n]) for n in WEIGHTS}
    xs, target = x[0], loss_target[0]
    S = xs.shape[0]
    ts = TS

    w_in_blk = jnp.pad(w["w_in"].T.astype(bf16), ((0, W_IN_PAD - W_IN_ROWS), (0, 0)))
    conv_blk = jnp.concatenate([_pad_rows8(w["ssd_conv_w"]), _pad_rows8(w["ffn_conv_w"]),
                                jnp.zeros((8, CONV_BLOCK - SSD_CONV_COLS - FFN_CONV_COLS), f32)], axis=1)
    g_in, g_conv = exchange([w_in_blk, conv_blk], (False, False), "gather_first")
    wt = g_in[:, :W_IN_ROWS].reshape(IN_DIM, D)
    w_main_t = jnp.concatenate([wt[IN_OFF[0]:IN_OFF[1]], wt[IN_OFF[6]:IN_OFF[8]], wt[IN_OFF[1]:IN_OFF[2]]], axis=0)
    w_q_t = wt[IN_OFF[3]:IN_OFF[4]]
    w_kv_t = wt[IN_OFF[4]:IN_OFF[6]]
    w_dt_t = jnp.pad(wt[IN_OFF[2]:IN_OFF[3]], ((0, LANES - NH), (0, 0)))
    conv_w8 = g_conv[:, :, 0:SSD_CONV_COLS].transpose(1, 0, 2).reshape(8, CONVD)
    fconv_w8 = g_conv[:, :, SSD_CONV_COLS:SSD_CONV_COLS + FFN_CONV_COLS].transpose(1, 0, 2).reshape(8, 2 * FF)
    bias = _pad_lanes(w["ssd_dt_bias"])
    alog = _pad_lanes(w["ssd_a_log"])
    dx_row = jnp.repeat(w["ssd_d"].reshape(-1), HD).reshape(1, DI)
    sinks = _pad_lanes(w["attn_sinks"])

    u = prenorm_fwd(xs, w["norm_mix_pre_w"], ts)
    later = [w["ssd_w_out"].astype(bf16), w["attn_w_out"].astype(bf16), w["w_mix_out"].astype(bf16)]
    proj, (g_so, g_ao, g_mix) = mm(u, w_main_t, "nt", bf16, "mm_proj", comm=(later, (False,) * 3))
    w_ssd_out, w_attn_out, w_mix = g_so.reshape(DI, D), g_ao.reshape(D, D), g_mix.reshape(D, D)
    qt = mm(w_q_t, u, "nt", bf16, "mm_q")
    kvt = mm(w_kv_t, u, "nt", bf16, "mm_kv")
    dtr = mm(u, w_dt_t, "nt", f32, "mm_dt")
    xbc, conv_c = ssdconv_fwd(proj, conv_w8, w["ssd_conv_b"], ts)
    y, hprev, (g_up,) = ssd_fwd(xbc, dtr, bias, alog, dx_row, comm=([w["ffn_w_up"].T.astype(bf16)], (False,)))
    w_up_t = g_up.reshape(2 * FF, D)
    yn = gnorm_fwd(y, proj, w["ssd_norm_w"], ts)
    ys, (g_down,) = mm(yn, w_ssd_out, "nn", bf16, "mm_ssd_out", comm=([w["ffn_w_down"].astype(bf16)], (False,)))
    w_down = g_down.reshape(FF, D)
    cos, sin = rope_tables(positions, ts)
    aot = attn_fwd(qt, kvt, cos, sin, sinks)
    ya = mm(aot, w_attn_out, "tn", bf16, "mm_attn_out")
    merged = merge_fwd(proj, ys, ya, ts)
    mo = mm(merged, w_mix, "nn", f32, "mm_mix")
    x1, h = post_fwd(xs, mo, w["norm_mix_post_w"], w["norm_ffn_pre_w"], ts)
    up = mm(h, w_up_t, "nt", bf16, "mm_up")
    act, gate_c, val_c = ffnact_fwd(up, fconv_w8, w["ffn_conv_b"], ts)
    ff = mm(act, w_down, "nn", f32, "mm_down")
    loss_blk, dout, dff, g_post2 = loss_head(x1, ff, target, w["norm_ffn_post_w"], ts)

    dact = mm(dff, w_down, "nt", bf16, "mm_dact")
    gw_down = mm(act, dff, "tn", bf16, "mm_g_down")
    dgate, dval = ffnact_bwd(dact, gate_c, val_c, ts)
    dup_pre, g_fconv_a = dwconv_bwd(dgate, up, 0, fconv_w8, 0, FFN_K, FF, ts, "ffnconv_bwd_gate", out_cols=2 * FF)
    dup_pre, g_fconv_b = dwconv_bwd(dval, up, 1, fconv_w8, 1, FFN_K, FF, ts, "ffnconv_bwd_val", into=dup_pre, ocb=1,
                                    out_cols=2 * FF)
    g_fconv = jnp.concatenate([g_fconv_a, g_fconv_b], axis=1)
    dh, (r_down,) = mm(dup_pre, w_up_t, "nn", bf16, "mm_dh", comm=([gw_down.reshape(N_DEV, FF // N_DEV, D)], (True,)))
    gw_up_t = mm(dup_pre, h, "tn", bf16, "mm_g_up")
    dx1, dmo, g_norms = post_bwd(dout, dh, x1, mo, w["norm_mix_post_w"], w["norm_ffn_pre_w"], ts)
    dmerged = mm(dmo, w_mix, "nt", bf16, "mm_dmerged")
    gw_mix = mm(merged, dmo, "tn", bf16, "mm_g_mix")
    dys, dya, dproj = merge_bwd(dmerged, proj, ys, ya, ts)
    daot = mm(w_attn_out, dya, "nt", bf16, "mm_dao")
    gw_attn_out = mm(aot, dya, "nn", bf16, "mm_g_attn_out")
    dqt, dkvt, g_sinks, (r_up,) = attn_bwd(qt, kvt, cos, sin, sinks, daot,
                                           comm=([gw_up_t.reshape(N_DEV, 2 * FF // N_DEV, D)], (True,)))
    dyn = mm(dys, w_ssd_out, "nt", bf16, "mm_dyn")
    gw_ssd_out = mm(yn, dys, "tn", bf16, "mm_g_ssd_out")
    dy, dproj, g_gnorm = gnorm_bwd(dyn, y, proj, w["ssd_norm_w"], dproj, ts)
    sends = [gw_ssd_out.reshape(N_DEV, DI // N_DEV, D), gw_attn_out.reshape(N_DEV, D // N_DEV, D),
             gw_mix.reshape(N_DEV, D // N_DEV, D)]
    dxbc, ddtr, g_ssd, (r_so, r_ao, r_mix) = ssd_bwd(xbc, dtr, dy, hprev, bias, alog, dx_row, comm=(sends, (True,) * 3))
    dproj, g_conv_w = dwconv_bwd(dxbc, proj, C_XBC // 1024, conv_w8, 0, SSD_K, 1024, ts, "ssdconv_bwd", act_c=conv_c,
                                 into=dproj, ocb=C_XBC // 1024, out_cols=PM)
    ddtr_b = ddtr.astype(bf16)
    g_main_t = mm(dproj, u, "tn", bf16, "mm_g_in")
    g_q_t = mm(dqt, u, "nn", bf16, "mm_g_q")
    g_kv_t = mm(dkvt, u, "nn", bf16, "mm_g_kv")
    g_dt_t = mm(ddtr_b, u, "tn", bf16, "mm_g_dt")
    g_wt = jnp.concatenate([g_main_t[C_Z:C_GS], g_main_t[C_XBC:PM], g_dt_t[:NH], g_q_t, g_kv_t, g_main_t[C_GS:C_XBC]],
                           axis=0)
    send_in = jnp.pad(g_wt.reshape(N_DEV, W_IN_ROWS, D), ((0, 0), (0, W_IN_PAD - W_IN_ROWS), (0, 0)))
    pieces = {"norm_mix_post_w": g_norms[1:2], "norm_ffn_pre_w": g_norms[0:1], "norm_ffn_post_w": g_post2[0:1],
              "ssd_norm_w": g_gnorm[0:1], "ssd_conv_b": g_conv_w[7:8], "ffn_conv_b": g_fconv[7:8],
              "ssd_dt_bias": g_ssd[0:1], "ssd_a_log": g_ssd[1:2], "ssd_d": g_ssd[2:3], "attn_sinks": g_sinks[0:1],
              "loss": loss_blk[0:1]}
    row = jnp.concatenate([jnp.pad(pieces[n][:, :min(k, pieces[n].shape[1])],
                                   ((0, 0), (0, -(-k // LANES) * LANES - min(k, pieces[n].shape[1]))))
                           for n, k in SMALL_ROW], axis=1)
    send_row = jnp.pad(row, ((0, 7), (0, 0)))
    send_conv = jnp.concatenate(
        [g_conv_w.reshape(8, N_DEV, SSD_CONV_COLS).transpose(1, 0, 2),
         g_fconv.reshape(8, N_DEV, FFN_CONV_COLS).transpose(1, 0, 2),
         jnp.zeros((N_DEV, 8, CONV_BLOCK - SSD_CONV_COLS - FFN_CONV_COLS), f32)], axis=2)
    du_a, (r_in, recv_row, recv_conv) = mm(dproj, w_main_t, "nn", bf16, "mm_du",
                                           comm=([send_in, send_row, send_conv], (True, False, True)))
    du_b = mm(dkvt, w_kv_t, "tn", bf16, "mm_du_kv")
    du_c = mm(ddtr_b, w_dt_t, "nn", bf16, "mm_du_dt")
    du_d = mm(dqt, w_q_t, "tn", bf16, "mm_du_q")
    grad_x, g_pre = prenorm_bwd(xs, w["norm_mix_pre_w"], (du_a, du_b, du_c, du_d), dx1, ts)
    (recv_pre,) = exchange([g_pre], (False,), "gather_last")

    tpad = lambda t: jnp.pad(t.T, ((0, W_IN_PAD - W_IN_ROWS), (0, 0)))
    o_in = [t[:W_IN_ROWS].T for t in adamw(r_in, tpad(w["w_in"]), tpad(m["w_in"]), tpad(v["w_in"]), 368, "adamw_w_in")]
    o_up = [t.T for t in adamw(r_up, w["ffn_w_up"].T, m["ffn_w_up"].T, v["ffn_w_up"].T, 352, "adamw_w_up")]
    big = {"w_in": o_in, "ffn_w_up": o_up,
           "ssd_w_out": adamw(r_so, w["ssd_w_out"], m["ssd_w_out"], v["ssd_w_out"], 256, "adamw_ssd_out"),
           "attn_w_out": adamw(r_ao, w["attn_w_out"], m["attn_w_out"], v["attn_w_out"], 128, "adamw_attn_out"),
           "w_mix_out": adamw(r_mix, w["w_mix_out"], m["w_mix_out"], v["w_mix_out"], 128, "adamw_mix"),
           "ffn_w_down": adamw(r_down, w["ffn_w_down"], m["ffn_w_down"], v["ffn_w_down"], 352, "adamw_down")}
    small_names = [n for n in WEIGHTS if n not in big]
    small, loss_row = adamw_small(recv_row, recv_pre, recv_conv, {n: (w[n], m[n], v[n]) for n in small_names})

    outs = [loss_row[0, 0], grad_x[None]]
    for k in range(4):
        for n in WEIGHTS:
            outs.append((big[n][k] if n in big else small[n][k]).reshape(a[n].shape))
    return tuple(outs)
```

```python
import jax
import jax.numpy as jnp
import numpy as np
from jax import lax
from jax.experimental import pallas as pl
from jax.experimental.pallas import tpu as pltpu

f32 = jnp.float32
bf16 = jnp.bfloat16

N_DEV = 8
D = 1024
DI = 2048
NH = 32
HD = 64
NG = 4
GW = DI // NG
NS = 128
CH = 128
CONVD = DI + 2 * NG * NS
SSD_K = 4
AH = 16
AD = 64
KVH = 4
REP = AH // KVH
KVW = KVH * AD
WIN = 128
FF = 2816
FFN_K = 3
EPS = 1e-6
ROPE_THETA = 10000.0
LANES = 128
RG = 16
CW = 256

C_Z, C_GS, C_GA, C_XBC, PM = 0, 2048, 3072, 4096, 7168
IN_SIZES = (DI, CONVD, NH, D, KVW, KVW, D, D)
IN_OFF = tuple(int(v) for v in np.cumsum((0,) + IN_SIZES))
IN_DIM = IN_OFF[-1]

ADAM_LR, ADAM_B1, ADAM_B2, ADAM_EPS, ADAM_WD, ADAM_STEP = 0.001, 0.9, 0.999, 1e-08, 0.01, 10

VMEM_LIMIT = 56 * 1024 * 1024


def _cp(*sem, side_effects=False):
    return pltpu.CompilerParams(dimension_semantics=sem, vmem_limit_bytes=VMEM_LIMIT, has_side_effects=side_effects)


def _dot(a, b, mode="nn"):
    dims = {"nn": (((1,), (0,)), ((), ())), "nt": (((1,), (1,)), ((), ())), "tn": (((0,), (0,)), ((), ()))}[mode]
    return lax.dot_general(a, b, dims, preferred_element_type=f32)


def _split3(v):
    hi = v.astype(bf16)
    r = v - hi.astype(f32)
    mid = r.astype(bf16)
    lo = (r - mid.astype(f32)).astype(bf16)
    return hi, mid, lo


def _dot3_left(m01, v):
    hi, mid, lo = _split3(v)
    return _dot(m01, hi) + _dot(m01, mid) + _dot(m01, lo)


def _dot3_right(v, m01):
    hi, mid, lo = _split3(v)
    return _dot(hi, m01) + _dot(mid, m01) + _dot(lo, m01)


def _dot2_right(v, m01):
    hi = v.astype(bf16)
    lo = (v - hi.astype(f32)).astype(bf16)
    return _dot(hi, m01) + _dot(lo, m01)


def _sigmoid(x):
    return 1.0 / (1.0 + jnp.exp(-x))


def _sigmoid_fast(x):
    return pl.reciprocal(1.0 + jnp.exp(-x), approx=True)


def _peer(k, x, y, c):
    return ((1 - x) if k & 4 else x, (1 - y) if k & 2 else y, (1 - c) if k & 1 else c)


def _xchg_copies(buf_refs, out_refs, send_sems, recv_sems, local_sems, personalised):
    x, y, c = lax.axis_index("x"), lax.axis_index("y"), lax.axis_index("c")
    me = 4 * x + 2 * y + c
    local, remote = [], []
    for b, (buf, out, pers) in enumerate(zip(buf_refs, out_refs, personalised)):
        local.append(pltpu.make_async_copy(buf.at[me] if pers else buf, out.at[me], local_sems.at[b]))
        for k in range(1, N_DEV):
            px, py, pc = _peer(k, x, y, c)
            s = b * (N_DEV - 1) + k - 1
            remote.append(pltpu.make_async_remote_copy(
                src_ref=buf.at[4 * px + 2 * py + pc] if pers else buf, dst_ref=out.at[me],
                send_sem=send_sems.at[s], recv_sem=recv_sems.at[s],
                device_id=(px, py, pc), device_id_type=pl.DeviceIdType.MESH))
    return local, remote


class _Comm:
    def __init__(self, comm):
        self.bufs, self.pers = comm if comm else ((), ())
        self.n = len(self.bufs)

    def in_specs(self):
        return [pl.BlockSpec(memory_space=pl.ANY)] * self.n

    out_specs = in_specs

    def out_shape(self):
        return [jax.ShapeDtypeStruct((N_DEV,) + tuple(b.shape[1:] if p else b.shape), b.dtype)
                for b, p in zip(self.bufs, self.pers)]

    def scratch(self):
        n = self.n
        return [pltpu.SemaphoreType.DMA((n * (N_DEV - 1),)), pltpu.SemaphoreType.DMA((n * (N_DEV - 1),)),
                pltpu.SemaphoreType.DMA((n,))] if n else []

    def split(self, refs, n_in, n_out):
        n = self.n
        ins, outs = refs[:n_in], refs[n_in + n:n_in + n + n_out]
        rest = refs[n_in + n + n_out + n:]
        if not n:
            return ins, outs, rest, None
        copies = _xchg_copies(refs[n_in:n_in + n], refs[n_in + n + n_out:n_in + n + n_out + n], *rest[-3:], self.pers)
        return ins, outs, rest[:-3], copies

    def start(self, copies, first):
        if copies:
            @pl.when(first)
            def _():
                for cp in copies[0] + copies[1]:
                    cp.start()

    def wait(self, copies, last):
        if copies:
            @pl.when(last)
            def _():
                for cp in copies[1]:
                    cp.wait_recv()
                for cp in copies[1]:
                    cp.wait_send()
                for cp in copies[0]:
                    cp.wait()


def exchange(bufs, personalised, name):
    cm = _Comm((bufs, personalised))

    def body(*refs):
        _, _, _, copies = cm.split(refs, 0, 0)
        cm.start(copies, True)
        cm.wait(copies, True)

    return pl.pallas_call(
        body, name=name, in_specs=cm.in_specs(), out_specs=cm.out_specs(), out_shape=cm.out_shape(),
        scratch_shapes=cm.scratch(), compiler_params=pltpu.CompilerParams(has_side_effects=True),
    )(*bufs)


def gather_two_level(bufs, name):
    n = len(bufs)
    per = N_DEV - 1

    def body(*refs):
        ins, outs = refs[:n], refs[n:2 * n]
        send_sems, recv_sems, local_sems = refs[2 * n:]
        x, y, c = lax.axis_index("x"), lax.axis_index("y"), lax.axis_index("c")
        me, sibling = (x, y, c), (x, y, 1 - c)
        chips = [(1 - x, y), (x, 1 - y), (1 - x, 1 - y)]

        def copy(b, k, block, to, src=None):
            dst = outs[b].at[4 * block[0] + 2 * block[1] + block[2]]
            return pltpu.make_async_remote_copy(
                src_ref=dst if src is None else src, dst_ref=dst,
                send_sem=send_sems.at[b * per + k], recv_sem=recv_sems.at[b * per + k],
                device_id=to, device_id_type=pl.DeviceIdType.MESH)

        mine = [pltpu.make_async_copy(ins[b], outs[b].at[4 * x + 2 * y + c], local_sems.at[b]) for b in range(n)]
        first = []
        for b in range(n):
            first.append(copy(b, 0, me, sibling, src=ins[b]))
            first += [copy(b, 1 + j, me, (*chip, c), src=ins[b]) for j, chip in enumerate(chips)]
        for cp in mine + first:
            cp.start()
        passed = []
        for j, chip in enumerate(chips):
            for b in range(n):
                copy(b, 1 + j, (*chip, c), me).wait_recv()
                passed.append(copy(b, 4 + j, (*chip, c), sibling))
                passed[-1].start()
        for b in range(n):
            copy(b, 0, sibling, me).wait_recv()
            for j, chip in enumerate(chips):
                copy(b, 4 + j, (*chip, 1 - c), me).wait_recv()
        for cp in first + passed:
            cp.wait_send()
        for cp in mine:
            cp.wait()

    hbm = pl.BlockSpec(memory_space=pl.ANY)
    return pl.pallas_call(
        body, name=name, in_specs=[hbm] * n, out_specs=[hbm] * n,
        out_shape=[jax.ShapeDtypeStruct((N_DEV,) + tuple(b.shape), b.dtype) for b in bufs],
        scratch_shapes=[pltpu.SemaphoreType.DMA((n * per,)), pltpu.SemaphoreType.DMA((n * per,)),
                        pltpu.SemaphoreType.DMA((n,))],
        compiler_params=pltpu.CompilerParams(has_side_effects=True),
    )(*bufs)


MM_TILES = (2176, 2048, 1408, 1024, 512, 256, 128)
MM_VMEM_BUDGET = 40 * 1024 * 1024


def _mm_tiles(M, N, K, out_bytes):
    cm = [t for t in MM_TILES if M % t == 0]
    cn = [t for t in MM_TILES if N % t == 0]
    ck = [t for t in MM_TILES if K % t == 0]
    best = None
    for bm in cm[:2]:
        for bn in cn:
            for bk in ck:
                need = 4 * (bm * bk + bk * bn) + bm * bn * (4 + 2 * out_bytes)
                if need <= MM_VMEM_BUDGET:
                    score = (bm * bn * bk, bk)
                    if best is None or score > best[0]:
                        best = (score, (bm, bn, bk))
    return best[1]


def mm(a, b, mode, out_dtype, name, comm=None):
    if mode == "nn":
        (M, K), (_, N) = a.shape, b.shape
    elif mode == "nt":
        (M, K), (N, _) = a.shape, b.shape
    else:
        (K, M), (_, N) = a.shape, b.shape
    bm, bn, bk = _mm_tiles(M, N, K, jnp.dtype(out_dtype).itemsize)
    gm, gn, nk = M // bm, N // bn, K // bk
    cm = _Comm(comm)

    def body(*refs):
        (a_ref, b_ref), (o_ref,), scr, copies = cm.split(refs, 2, 1)
        i, j, k = pl.program_id(0), pl.program_id(1), pl.program_id(2)
        cm.start(copies, jnp.logical_and(jnp.logical_and(i == 0, j == 0), k == 0))
        p = _dot(a_ref[...], b_ref[...], mode)
        if nk == 1:
            o_ref[...] = p.astype(o_ref.dtype)
        else:
            acc_ref = scr[0]

            @pl.when(k == 0)
            def _():
                acc_ref[...] = p

            @pl.when(k > 0)
            def _():
                acc_ref[...] += p

            @pl.when(k == nk - 1)
            def _():
                o_ref[...] = acc_ref[...].astype(o_ref.dtype)

        cm.wait(copies, jnp.logical_and(jnp.logical_and(i == gm - 1, j == gn - 1), k == nk - 1))

    if mode == "nn":
        a_spec = pl.BlockSpec((bm, bk), lambda i, j, k: (i, k))
        b_spec = pl.BlockSpec((bk, bn), lambda i, j, k: (k, j))
    elif mode == "nt":
        a_spec = pl.BlockSpec((bm, bk), lambda i, j, k: (i, k))
        b_spec = pl.BlockSpec((bn, bk), lambda i, j, k: (j, k))
    else:
        a_spec = pl.BlockSpec((bk, bm), lambda i, j, k: (k, i))
        b_spec = pl.BlockSpec((bk, bn), lambda i, j, k: (k, j))
    sem = ("arbitrary",) * 3 if cm.n else ("parallel", "parallel", "arbitrary")
    res = pl.pallas_call(
        body, name=name, grid=(gm, gn, nk),
        in_specs=[a_spec, b_spec] + cm.in_specs(),
        out_specs=[pl.BlockSpec((bm, bn), lambda i, j, k: (i, j))] + cm.out_specs(),
        out_shape=[jax.ShapeDtypeStruct((M, N), out_dtype)] + cm.out_shape(),
        scratch_shapes=([pltpu.VMEM((bm, bn), f32)] if nk > 1 else []) + cm.scratch(),
        compiler_params=_cp(*sem, side_effects=bool(cm.n)),
    )(a, b, *cm.bufs)
    return (res[0], res[1:]) if cm.n else res[0]


def _groups(ts, fn, carry=None, reverse=False, unroll=4, rg=RG):
    n = ts // rg
    if n == 1:
        return fn(0, carry)
    unroll = min(unroll, n)
    span = rg * unroll

    def body(g, c):
        r0 = pl.multiple_of((n // unroll - 1 - g if reverse else g) * span, span)
        for u in (range(unroll - 1, -1, -1) if reverse else range(unroll)):
            c = fn(pl.multiple_of(r0 + u * rg, rg), c)
        return c

    return lax.fori_loop(0, n // unroll, body, carry)


def _rms(x):
    return lax.rsqrt(jnp.mean(x * x, axis=-1, keepdims=True) + EPS)


def _rms_bwd(x, r, dn):
    n = x * r
    return r * (dn - n * jnp.mean(dn * n, axis=-1, keepdims=True))


NRG = 256


def _fold(x):
    return jnp.sum(x.reshape(x.shape[0] // 8, 8, x.shape[1]), axis=0)


def _flush(acc_ref, out_ref, row):
    out_ref[row:row + 1, :] = jnp.sum(acc_ref[...], axis=0, keepdims=True)


def prenorm_fwd(x, w, ts):
    S = x.shape[0]

    def body(x_ref, w_ref, u_ref):
        wv = w_ref[...]

        def grp(r0, _):
            xv = x_ref[pl.ds(r0, NRG), :]
            u_ref[pl.ds(r0, NRG), :] = (xv * _rms(xv) * wv).astype(bf16)

        _groups(ts, grp, rg=NRG)

    return pl.pallas_call(
        body, name="prenorm_fwd", grid=(S // ts,),
        in_specs=[pl.BlockSpec((ts, D), lambda i: (i, 0)), pl.BlockSpec((1, D), lambda i: (0, 0))],
        out_specs=pl.BlockSpec((ts, D), lambda i: (i, 0)),
        out_shape=jax.ShapeDtypeStruct((S, D), bf16),
        compiler_params=_cp("parallel"),
    )(x, w)


def prenorm_bwd(x, w, dus, dx1, ts):
    S = x.shape[0]
    nt = S // ts
    nd = len(dus)

    def body(x_ref, w_ref, *refs):
        du_refs = refs[:nd]
        dx1_ref, gx_ref, gw_ref, acc_ref = refs[nd:]
        i = pl.program_id(0)
        wv = w_ref[...]

        @pl.when(i == 0)
        def _():
            acc_ref[...] = jnp.zeros_like(acc_ref)
            gw_ref[...] = jnp.zeros_like(gw_ref)

        def grp(r0, _):
            rows = pl.ds(r0, NRG)
            xv = x_ref[rows, :]
            r = _rms(xv)
            du = du_refs[0][rows, :].astype(f32)
            for d_ref in du_refs[1:]:
                du = du + d_ref[rows, :].astype(f32)
            gx_ref[rows, :] = dx1_ref[rows, :] + _rms_bwd(xv, r, du * wv)
            acc_ref[...] += _fold(du * xv * r)

        _groups(ts, grp, rg=NRG)

        @pl.when(i == nt - 1)
        def _():
            _flush(acc_ref, gw_ref, 0)

    row = pl.BlockSpec((ts, D), lambda i: (i, 0))
    return pl.pallas_call(
        body, name="prenorm_bwd", grid=(nt,),
        in_specs=[row, pl.BlockSpec((1, D), lambda i: (0, 0))] + [row] * (nd + 1),
        out_specs=[row, pl.BlockSpec((8, D), lambda i: (0, 0))],
        out_shape=[jax.ShapeDtypeStruct((S, D), f32), jax.ShapeDtypeStruct((8, D), f32)],
        scratch_shapes=[pltpu.VMEM((8, D), f32)],
        compiler_params=_cp("arbitrary"),
    )(x, w, *dus, dx1)


def post_fwd(x, mo, w_post, w_pre2, ts):
    S = x.shape[0]

    def body(x_ref, mo_ref, wp_ref, w2_ref, x1_ref, h_ref):
        wp, w2 = wp_ref[...], w2_ref[...]

        def grp(r0, _):
            rows = pl.ds(r0, NRG)
            mv = mo_ref[rows, :]
            x1 = x_ref[rows, :] + mv * _rms(mv) * wp
            x1_ref[rows, :] = x1
            h_ref[rows, :] = (x1 * _rms(x1) * w2).astype(bf16)

        _groups(ts, grp, rg=NRG)

    row = pl.BlockSpec((ts, D), lambda i: (i, 0))
    par = pl.BlockSpec((1, D), lambda i: (0, 0))
    return pl.pallas_call(
        body, name="post_fwd", grid=(S // ts,),
        in_specs=[row, row, par, par], out_specs=[row, row],
        out_shape=[jax.ShapeDtypeStruct((S, D), f32), jax.ShapeDtypeStruct((S, D), bf16)],
        compiler_params=_cp("parallel"),
    )(x, mo, w_post, w_pre2)


def post_bwd(dout, dh, x1, mo, w_post, w_pre2, ts):
    S = x1.shape[0]
    nt = S // ts

    def body(dout_ref, dh_ref, x1_ref, mo_ref, wp_ref, w2_ref, dx1_ref, dmo_ref, gw_ref, acc2_ref, accp_ref):
        i = pl.program_id(0)
        wp, w2 = wp_ref[...], w2_ref[...]

        @pl.when(i == 0)
        def _():
            acc2_ref[...] = jnp.zeros_like(acc2_ref)
            accp_ref[...] = jnp.zeros_like(accp_ref)
            gw_ref[...] = jnp.zeros_like(gw_ref)

        def grp(r0, _):
            rows = pl.ds(r0, NRG)
            x1 = x1_ref[rows, :]
            r1 = _rms(x1)
            dh = dh_ref[rows, :].astype(f32)
            dx1 = dout_ref[rows, :] + _rms_bwd(x1, r1, dh * w2)
            dx1_ref[rows, :] = dx1
            acc2_ref[...] += _fold(dh * x1 * r1)
            mv = mo_ref[rows, :]
            rm = _rms(mv)
            dmo_ref[rows, :] = _rms_bwd(mv, rm, dx1 * wp).astype(bf16)
            accp_ref[...] += _fold(dx1 * mv * rm)

        _groups(ts, grp, rg=NRG)

        @pl.when(i == nt - 1)
        def _():
            _flush(acc2_ref, gw_ref, 0)
            _flush(accp_ref, gw_ref, 1)

    row = pl.BlockSpec((ts, D), lambda i: (i, 0))
    par = pl.BlockSpec((1, D), lambda i: (0, 0))
    return pl.pallas_call(
        body, name="post_bwd", grid=(nt,),
        in_specs=[row, row, row, row, par, par],
        out_specs=[row, row, pl.BlockSpec((8, D), lambda i: (0, 0))],
        out_shape=[jax.ShapeDtypeStruct((S, D), f32), jax.ShapeDtypeStruct((S, D), bf16),
                   jax.ShapeDtypeStruct((8, D), f32)],
        scratch_shapes=[pltpu.VMEM((8, D), f32), pltpu.VMEM((8, D), f32)],
        compiler_params=_cp("arbitrary"),
    )(dout, dh, x1, mo, w_post, w_pre2)


def loss_head(x1, ff, target, w, ts):
    S = x1.shape[0]
    nt = S // ts

    def body(x1_ref, ff_ref, t_ref, w_ref, loss_ref, dout_ref, dff_ref, gw_ref, accw_ref, accl_ref):
        i = pl.program_id(0)
        wv = w_ref[...]

        @pl.when(i == 0)
        def _():
            accw_ref[...] = jnp.zeros_like(accw_ref)
            accl_ref[...] = jnp.zeros_like(accl_ref)
            gw_ref[...] = jnp.zeros_like(gw_ref)

        def grp(r0, _):
            rows = pl.ds(r0, NRG)
            fv = ff_ref[rows, :]
            r = _rms(fv)
            n = fv * r
            e = x1_ref[rows, :] + n * wv - t_ref[rows, :]
            dout = e * (1.0 / D)
            dout_ref[rows, :] = dout
            dff_ref[rows, :] = _rms_bwd(fv, r, dout * wv).astype(bf16)
            accw_ref[...] += _fold(dout * n)
            accl_ref[...] += _fold(e * e)

        _groups(ts, grp, rg=NRG)

        @pl.when(i == nt - 1)
        def _():
            _flush(accw_ref, gw_ref, 0)
            tot = jnp.sum(jnp.sum(accl_ref[...], axis=1, keepdims=True), axis=0, keepdims=True) * (0.5 / D)
            loss_ref[...] = jnp.broadcast_to(tot, loss_ref.shape)

    row = pl.BlockSpec((ts, D), lambda i: (i, 0))
    return pl.pallas_call(
        body, name="loss_head", grid=(nt,),
        in_specs=[row, row, row, pl.BlockSpec((1, D), lambda i: (0, 0))],
        out_specs=[pl.BlockSpec((8, LANES), lambda i: (0, 0)), row, row, pl.BlockSpec((8, D), lambda i: (0, 0))],
        out_shape=[jax.ShapeDtypeStruct((8, LANES), f32), jax.ShapeDtypeStruct((S, D), f32),
                   jax.ShapeDtypeStruct((S, D), bf16), jax.ShapeDtypeStruct((8, D), f32)],
        scratch_shapes=[pltpu.VMEM((8, D), f32), pltpu.VMEM((8, D), f32)],
        compiler_params=_cp("arbitrary"),
    )(x1, ff, target, w)


def _taps(w_ref, cs, K):
    return [jnp.broadcast_to(w_ref[k:k + 1, cs], (8, CW)) for k in range(K)]


def _down(before, cur, s, sub):
    return jnp.where(sub < s, pltpu.roll(before, s, 0), pltpu.roll(cur, s, 0))


def _up(cur, after, s, sub):
    return jnp.where(sub < 8 - s, pltpu.roll(cur, 8 - s, 0), pltpu.roll(after, 8 - s, 0))


def _conv_group(p, a, b, taps, bias, K, sub):
    ya, yb = bias, bias
    for k in range(K):
        s = K - 1 - k
        xa, xb = (a, b) if s == 0 else (_down(p, a, s, sub), _down(a, b, s, sub))
        ya = ya + taps[k] * xa
        yb = yb + taps[k] * xb
    return ya, yb


def _prev8_map(ts, cb):
    return lambda i, j: (jnp.maximum(i * (ts // 8) - 1, 0), cb + j)


def ssdconv_fwd(proj, w8, b, ts):
    S = proj.shape[0]
    bw = 1024
    cb = C_XBC // bw

    def body(cur_ref, prev_ref, w_ref, b_ref, o_ref, c_ref):
        first = pl.program_id(0) == 0
        sub = lax.broadcasted_iota(jnp.int32, (8, CW), 0)
        for c0 in range(0, bw, CW):
            cs = slice(c0, c0 + CW)
            taps = _taps(w_ref, cs, SSD_K)
            bias = jnp.broadcast_to(b_ref[:, cs], (8, CW))

            def grp(r0, p, cs=cs, taps=taps, bias=bias):
                rows = pl.ds(r0, RG)
                xv = cur_ref[rows, cs].astype(f32)
                ya, yb = _conv_group(p, xv[0:8], xv[8:16], taps, bias, SSD_K, sub)
                y = jnp.concatenate([ya, yb], axis=0)
                c_ref[rows, cs] = y.astype(bf16)
                o_ref[rows, cs] = (y * _sigmoid_fast(y)).astype(bf16)
                return xv[8:16]

            _groups(ts, grp, jnp.where(first, 0.0, prev_ref[:, cs].astype(f32)))

    o = jax.ShapeDtypeStruct((S, CONVD), bf16)
    blk = pl.BlockSpec((ts, bw), lambda i, j: (i, j))
    return pl.pallas_call(
        body, name="ssdconv_fwd", grid=(S // ts, CONVD // bw),
        in_specs=[pl.BlockSpec((ts, bw), lambda i, j: (i, cb + j)),
                  pl.BlockSpec((8, bw), _prev8_map(ts, cb)),
                  pl.BlockSpec((8, bw), lambda i, j: (0, j)),
                  pl.BlockSpec((1, bw), lambda i, j: (0, j))],
        out_specs=[blk, blk], out_shape=[o, o],
        compiler_params=_cp("parallel", "parallel"),
    )(proj, proj, w8, b)


def _gelu_tanh(x):
    c = 0.7978845608028654
    t = jnp.tanh(c * (x + 0.044715 * x * x * x))
    return 0.5 * x * (1.0 + t), t


def ffnact_fwd(up, w8, b, ts):
    S = up.shape[0]

    def body(g_ref, gp_ref, v_ref, vp_ref, wg_ref, wv_ref, bg_ref, bv_ref, o_ref, gc_ref, vc_ref):
        first = pl.program_id(0) == 0
        sub = lax.broadcasted_iota(jnp.int32, (8, CW), 0)
        for c0 in range(0, FF, CW):
            cs = slice(c0, c0 + CW)
            tg, tv = _taps(wg_ref, cs, FFN_K), _taps(wv_ref, cs, FFN_K)
            bg = jnp.broadcast_to(bg_ref[:, cs], (8, CW))
            bv = jnp.broadcast_to(bv_ref[:, cs], (8, CW))

            def grp(r0, carry, cs=cs, tg=tg, tv=tv, bg=bg, bv=bv):
                pg, pv = carry
                rows = pl.ds(r0, RG)
                gx = g_ref[rows, cs].astype(f32)
                vx = v_ref[rows, cs].astype(f32)
                g = jnp.concatenate(_conv_group(pg, gx[0:8], gx[8:16], tg, bg, FFN_K, sub), axis=0)
                v = jnp.concatenate(_conv_group(pv, vx[0:8], vx[8:16], tv, bv, FFN_K, sub), axis=0)
                gc_ref[rows, cs] = g.astype(bf16)
                vc_ref[rows, cs] = v.astype(bf16)
                o_ref[rows, cs] = (_gelu_tanh(g)[0] * v).astype(bf16)
                return gx[8:16], vx[8:16]

            _groups(ts, grp, (jnp.where(first, 0.0, gp_ref[:, cs].astype(f32)),
                              jnp.where(first, 0.0, vp_ref[:, cs].astype(f32))))

    o = jax.ShapeDtypeStruct((S, FF), bf16)
    blk = pl.BlockSpec((ts, FF), lambda i: (i, 0))
    prev = lambda cb: pl.BlockSpec((8, FF), lambda i: (jnp.maximum(i * (ts // 8) - 1, 0), cb))
    return pl.pallas_call(
        body, name="ffnact_fwd", grid=(S // ts,),
        in_specs=[blk, prev(0), pl.BlockSpec((ts, FF), lambda i: (i, 1)), prev(1),
                  pl.BlockSpec((8, FF), lambda i: (0, 0)), pl.BlockSpec((8, FF), lambda i: (0, 1)),
                  pl.BlockSpec((1, FF), lambda i: (0, 0)), pl.BlockSpec((1, FF), lambda i: (0, 1))],
        out_specs=[blk, blk, blk], out_shape=[o, o, o],
        compiler_params=_cp("parallel"),
    )(up, up, up, up, w8, w8, b, b)


def ffnact_bwd(dact, gc, vc, ts):
    S = dact.shape[0]

    def body(d_ref, g_ref, v_ref, dg_ref, dv_ref):
        c = 0.7978845608028654
        for c0 in range(0, FF, CW):
            cs = slice(c0, c0 + CW)

            def grp(r0, _, cs=cs):
                rows = pl.ds(r0, RG)
                d = d_ref[rows, cs].astype(f32)
                g = g_ref[rows, cs].astype(f32)
                ge, t = _gelu_tanh(g)
                dgelu = 0.5 * (1.0 + t) + 0.5 * g * (1.0 - t * t) * c * (1.0 + 3.0 * 0.044715 * g * g)
                dg_ref[rows, cs] = (d * v_ref[rows, cs].astype(f32) * dgelu).astype(bf16)
                dv_ref[rows, cs] = (d * ge).astype(bf16)

            _groups(ts, grp)

    o = jax.ShapeDtypeStruct((S, FF), bf16)
    blk = pl.BlockSpec((ts, FF), lambda i: (i, 0))
    return pl.pallas_call(
        body, name="ffnact_bwd", grid=(S // ts,),
        in_specs=[blk, blk, blk], out_specs=[blk, blk], out_shape=[o, o],
        compiler_params=_cp("parallel"),
    )(dact, gc, vc)


def dwconv_bwd(dy, x, xcb, w8, wcb, K, bw, ts, name, act_c=None, into=None, ocb=0, out_cols=None):
    S, C = dy.shape
    nr = S // ts
    out_cols = out_cols or C
    n_act = 0 if act_c is None else 2

    def body(*refs):
        dy_ref, dyn_ref = refs[0:2]
        c_ref, cn_ref = (refs[2:4] if n_act else (None, None))
        x_ref, xp_ref, w_ref = refs[2 + n_act:5 + n_act]
        dx_ref, dw_ref, sd_ref = refs[-3:]
        i = pl.program_id(1)
        first, last = i == 0, i == nr - 1
        sub = lax.broadcasted_iota(jnp.int32, (8, CW), 0)

        def grad_y(d, c):
            if c is None:
                return d.astype(f32)
            cv = c.astype(f32)
            s = _sigmoid_fast(cv)
            return d.astype(f32) * s * (1.0 + cv * (1.0 - s))

        @pl.when(first)
        def _():
            dw_ref[...] = jnp.zeros_like(dw_ref)

        for c0 in range(0, bw, CW):
            cs = slice(c0, c0 + CW)
            taps = _taps(w_ref, cs, K)
            zero = jnp.zeros((8, CW), f32)

            def fwd(r0, carry, cs=cs):
                p, accs, accb = carry
                rows = pl.ds(r0, RG)
                g = grad_y(dy_ref[rows, cs], c_ref[rows, cs] if n_act else None)
                sd_ref[rows, cs] = g
                xv = x_ref[rows, cs].astype(f32)
                a, b = xv[0:8], xv[8:16]
                ga, gb = g[0:8], g[8:16]
                new = []
                for k in range(K):
                    s = K - 1 - k
                    xa, xb = (a, b) if s == 0 else (_down(p, a, s, sub), _down(a, b, s, sub))
                    new.append(accs[k] + ga * xa + gb * xb)
                return b, tuple(new), accb + ga + gb

            _, accs, accb = _groups(ts, fwd, (jnp.where(first, 0.0, xp_ref[:, cs].astype(f32)), (zero,) * K, zero))
            for k in range(K):
                dw_ref[k:k + 1, cs] += jnp.sum(accs[k], axis=0, keepdims=True)
            dw_ref[7:8, cs] += jnp.sum(accb, axis=0, keepdims=True)

            def bwd(r0, after, cs=cs, taps=taps):
                rows = pl.ds(r0, RG)
                g = sd_ref[rows, cs]
                a, b = g[0:8], g[8:16]
                da, db = zero, zero
                for k in range(K):
                    s = K - 1 - k
                    ua, ub = (a, b) if s == 0 else (_up(a, b, s, sub), _up(b, after, s, sub))
                    da = da + taps[k] * ua
                    db = db + taps[k] * ub
                dx_ref[rows, cs] = jnp.concatenate([da, db], axis=0).astype(bf16)
                return a

            halo = grad_y(dyn_ref[:, cs], cn_ref[:, cs] if n_act else None)
            _groups(ts, bwd, jnp.where(last, 0.0, halo), reverse=True)

    nxt = lambda j, i: (jnp.minimum((i + 1) * (ts // 8), S // 8 - 1), j)
    tile = pl.BlockSpec((ts, bw), lambda j, i: (i, j))
    acts = [] if act_c is None else [act_c, act_c]
    extra = [] if into is None else [into]
    n_in = 5 + n_act
    return pl.pallas_call(
        body, name=name, grid=(C // bw, nr),
        in_specs=[tile, pl.BlockSpec((8, bw), nxt)] + ([tile, pl.BlockSpec((8, bw), nxt)] if n_act else []) + [
            pl.BlockSpec((ts, bw), lambda j, i: (i, xcb + j)),
            pl.BlockSpec((8, bw), lambda j, i: (jnp.maximum(i * (ts // 8) - 1, 0), xcb + j)),
            pl.BlockSpec((8, bw), lambda j, i: (0, wcb + j))] + [pl.BlockSpec(memory_space=pl.ANY)] * len(extra),
        out_specs=[pl.BlockSpec((ts, bw), lambda j, i: (i, ocb + j)), pl.BlockSpec((8, bw), lambda j, i: (0, j))],
        out_shape=[jax.ShapeDtypeStruct((S, out_cols), bf16), jax.ShapeDtypeStruct((8, C), f32)],
        scratch_shapes=[pltpu.VMEM((ts, bw), f32)],
        input_output_aliases={n_in: 0} if extra else {},
        compiler_params=_cp("parallel", "arbitrary"),
    )(dy, dy, *acts, x, x, w8, *extra)


def gnorm_fwd(y, proj, w, ts):
    S = y.shape[0]

    def body(y_ref, z_ref, w_ref, o_ref):
        for k in range(NG):
            sl = slice(k * GW, (k + 1) * GW)
            wv = w_ref[:, sl]

            def grp(r0, _, sl=sl, wv=wv):
                rows = pl.ds(r0, NRG)
                z = z_ref[rows, sl].astype(f32)
                g = y_ref[rows, sl].astype(f32) * z * _sigmoid_fast(z)
                o_ref[rows, sl] = (g * _rms(g) * wv).astype(bf16)

            _groups(ts, grp, rg=NRG)

    row = pl.BlockSpec((ts, DI), lambda i: (i, 0))
    return pl.pallas_call(
        body, name="gnorm_fwd", grid=(S // ts,),
        in_specs=[row, row, pl.BlockSpec((1, DI), lambda i: (0, 0))],
        out_specs=row, out_shape=jax.ShapeDtypeStruct((S, DI), bf16),
        compiler_params=_cp("parallel"),
    )(y, proj, w)


def gnorm_bwd(dyn, y, proj, w, dproj, ts):
    S = y.shape[0]
    nt = S // ts

    def body(d_ref, y_ref, z_ref, w_ref, _, dy_ref, dz_ref, gw_ref, acc_ref):
        i = pl.program_id(0)

        @pl.when(i == 0)
        def _():
            acc_ref[...] = jnp.zeros_like(acc_ref)
            gw_ref[...] = jnp.zeros_like(gw_ref)

        for k in range(NG):
            sl = slice(k * GW, (k + 1) * GW)
            wv = w_ref[:, sl]

            def grp(r0, _, sl=sl, wv=wv):
                rows = pl.ds(r0, NRG)
                z = z_ref[rows, sl].astype(f32)
                yv = y_ref[rows, sl].astype(f32)
                s = _sigmoid_fast(z)
                sz = z * s
                g = yv * sz
                r = _rms(g)
                d = d_ref[rows, sl].astype(f32)
                acc_ref[:, sl] += _fold(d * g * r)
                dg = _rms_bwd(g, r, d * wv)
                dy_ref[rows, sl] = (dg * sz).astype(bf16)
                dz_ref[rows, sl] = (dg * yv * s * (1.0 + z * (1.0 - s))).astype(bf16)

            _groups(ts, grp, rg=NRG)

        @pl.when(i == nt - 1)
        def _():
            _flush(acc_ref, gw_ref, 0)

    row = pl.BlockSpec((ts, DI), lambda i: (i, 0))
    return pl.pallas_call(
        body, name="gnorm_bwd", grid=(nt,),
        in_specs=[row, row, row, pl.BlockSpec((1, DI), lambda i: (0, 0)), pl.BlockSpec(memory_space=pl.ANY)],
        out_specs=[row, row, pl.BlockSpec((8, DI), lambda i: (0, 0))],
        out_shape=[jax.ShapeDtypeStruct((S, DI), bf16), jax.ShapeDtypeStruct(dproj.shape, bf16),
                   jax.ShapeDtypeStruct((8, DI), f32)],
        scratch_shapes=[pltpu.VMEM((8, DI), f32)],
        input_output_aliases={4: 1},
        compiler_params=_cp("arbitrary"),
    )(dyn, y, proj, w, dproj)


def merge_fwd(proj, ys, ya, ts):
    S = ys.shape[0]

    def body(gs_ref, ga_ref, ys_ref, ya_ref, o_ref):
        for c0 in range(0, D, CW):
            cs = slice(c0, c0 + CW)

            def grp(r0, _, cs=cs):
                rows = pl.ds(r0, NRG)
                o_ref[rows, cs] = (_sigmoid_fast(gs_ref[rows, cs].astype(f32)) * ys_ref[rows, cs].astype(f32)
                                   + _sigmoid_fast(ga_ref[rows, cs].astype(f32)) * ya_ref[rows, cs].astype(f32)
                                   ).astype(bf16)

            _groups(ts, grp, rg=NRG)

    row = pl.BlockSpec((ts, D), lambda i: (i, 0))
    return pl.pallas_call(
        body, name="merge_fwd", grid=(S // ts,),
        in_specs=[pl.BlockSpec((ts, D), lambda i: (i, C_GS // D)), pl.BlockSpec((ts, D), lambda i: (i, C_GA // D)), row, row],
        out_specs=row, out_shape=jax.ShapeDtypeStruct((S, D), bf16),
        compiler_params=_cp("parallel"),
    )(proj, proj, ys, ya)


def merge_bwd(dm, proj, ys, ya, ts):
    S = ys.shape[0]

    def body(d_ref, gs_ref, ga_ref, ys_ref, ya_ref, dys_ref, dya_ref, dg_ref):
        for c0 in range(0, D, CW):
            cs = slice(c0, c0 + CW)

            def grp(r0, _, c0=c0, cs=cs):
                rows = pl.ds(r0, NRG)
                d = d_ref[rows, cs].astype(f32)
                ss = _sigmoid_fast(gs_ref[rows, cs].astype(f32))
                sa = _sigmoid_fast(ga_ref[rows, cs].astype(f32))
                dys_ref[rows, cs] = (d * ss).astype(bf16)
                dya_ref[rows, cs] = (d * sa).astype(bf16)
                dg_ref[rows, cs] = (d * ys_ref[rows, cs].astype(f32) * ss * (1.0 - ss)).astype(bf16)
                dg_ref[rows, D + c0:D + c0 + CW] = (d * ya_ref[rows, cs].astype(f32) * sa * (1.0 - sa)).astype(bf16)

            _groups(ts, grp, rg=NRG)

    row = pl.BlockSpec((ts, D), lambda i: (i, 0))
    o = jax.ShapeDtypeStruct((S, D), bf16)
    return pl.pallas_call(
        body, name="merge_bwd", grid=(S // ts,),
        in_specs=[row, pl.BlockSpec((ts, D), lambda i: (i, C_GS // D)), pl.BlockSpec((ts, D), lambda i: (i, C_GA // D)), row, row],
        out_specs=[row, row, pl.BlockSpec((ts, 2 * D), lambda i: (i, C_GS // (2 * D)))],
        out_shape=[o, o, jax.ShapeDtypeStruct((S, PM), bf16)],
        compiler_params=_cp("parallel"),
    )(dm, proj, proj, ys, ya)


def _ssd_consts():
    h = lax.broadcasted_iota(jnp.int32, (LANES, DI), 0)
    c = lax.broadcasted_iota(jnp.int32, (LANES, DI), 1)
    expand = (c // HD == h).astype(bf16)
    r = lax.broadcasted_iota(jnp.int32, (CH, CH), 0)
    cc = lax.broadcasted_iota(jnp.int32, (CH, CH), 1)
    tril = (cc <= r).astype(bf16)
    triu = (cc >= r).astype(bf16)
    return expand, expand.T, tril, triu


def _ssd_common(xbc_ref, dtr_ref, bias_ref, alog_ref, expand_ref, tril_ref):
    dtr = dtr_ref[...] + bias_ref[...]
    dt = jnp.maximum(dtr, 0.0) + jnp.log1p(jnp.exp(-jnp.abs(dtr)))
    a = -jnp.exp(alog_ref[...])
    acs = _dot3_left(tril_ref[...], dt * a)
    acsx = _dot3_right(acs, expand_ref[...])
    dtx = _dot3_right(dt, expand_ref[...])
    x = xbc_ref[:, 0:DI].astype(f32)
    xdt = x * dtx
    e = jnp.exp(acsx)
    dsx = jnp.exp(acsx[CH - 1:CH, :] - acsx)
    return dtr, dt, a, acs, dtx, x, xdt, e, dsx


def _ssd_lmat(acs, acs_t, hh, causal):
    seg = acs[:, hh:hh + 1] - acs_t[hh:hh + 1, :]
    return jnp.where(causal, jnp.exp(jnp.minimum(seg, 0.0)), 0.0)


def ssd_fwd(xbc, dtr, bias, alog, dx_row, comm=None):
    S = xbc.shape[0]
    nc = S // CH
    expand, _, tril, _ = _ssd_consts()
    cm = _Comm(comm)

    def body(*refs):
        ins, (y_ref, hp_ref), (h_ref, yd_ref), copies = cm.split(refs, 7, 2)
        xbc_ref, dtr_ref, bias_ref, alog_ref, dxr_ref, expand_ref, tril_ref = ins
        c = pl.program_id(0)
        cm.start(copies, c == 0)

        @pl.when(c == 0)
        def _():
            h_ref[...] = jnp.zeros_like(h_ref)

        _, _, _, acs, _, x, xdt, e, dsx = _ssd_common(xbc_ref, dtr_ref, bias_ref, alog_ref, expand_ref, tril_ref)
        acs_t = acs.T
        xb = xdt.astype(bf16)
        xd = (xdt * dsx).astype(bf16)
        causal = tril_ref[...] > 0
        for g in range(NG):
            gs = slice(g * GW, (g + 1) * GW)
            bg = xbc_ref[:, DI + g * NS:DI + (g + 1) * NS]
            cg = xbc_ref[:, DI + NG * NS + g * NS:DI + NG * NS + (g + 1) * NS]
            cb = _dot(cg, bg, "nt")
            hp = h_ref[g]
            hpb = hp.astype(bf16)
            hp_ref[0, g] = hpb
            yd_ref[:, gs] = _dot(cg, hpb) * e[:, gs]
            h_ref[g] = hp * e[CH - 1:CH, gs] + _dot(bg, xd[:, gs], "tn")
            for j in range(NH // NG):
                hh = g * (NH // NG) + j
                hs = slice(hh * HD, (hh + 1) * HD)
                m = (cb * _ssd_lmat(acs, acs_t, hh, causal)).astype(bf16)
                yd_ref[:, hs] += _dot(m, xb[:, hs])
        y_ref[...] = (yd_ref[...] + dxr_ref[...] * x).astype(bf16)
        cm.wait(copies, c == nc - 1)

    par = lambda shape: pl.BlockSpec(shape, lambda c: (0,) * len(shape))
    res = pl.pallas_call(
        body, name="ssd_fwd", grid=(nc,),
        in_specs=[pl.BlockSpec((CH, CONVD), lambda c: (c, 0)), pl.BlockSpec((CH, LANES), lambda c: (c, 0)),
                  par((1, LANES)), par((1, LANES)), par((1, DI)), par((LANES, DI)), par((CH, CH))] + cm.in_specs(),
        out_specs=[pl.BlockSpec((CH, DI), lambda c: (c, 0)),
                   pl.BlockSpec((1, NG, NS, GW), lambda c: (c, 0, 0, 0))] + cm.out_specs(),
        out_shape=[jax.ShapeDtypeStruct((S, DI), bf16), jax.ShapeDtypeStruct((nc, NG, NS, GW), bf16)] + cm.out_shape(),
        scratch_shapes=[pltpu.VMEM((NG, NS, GW), f32), pltpu.VMEM((CH, DI), f32)] + cm.scratch(),
        compiler_params=_cp("arbitrary", side_effects=bool(cm.n)),
    )(xbc, dtr, bias, alog, dx_row, expand, tril, *cm.bufs)
    return res[0], res[1], res[2:]


def ssd_bwd(xbc, dtr, dy, hprev, bias, alog, dx_row, comm=None):
    S = xbc.shape[0]
    nc = S // CH
    expand, expand_t, tril, triu = _ssd_consts()
    cm = _Comm(comm)

    def body(*refs):
        ins, outs, scr, copies = cm.split(refs, 11, 3)
        xbc_ref, dtr_ref, dy_ref, hp_ref, bias_ref, alog_ref, dxr_ref, expand_ref, expt_ref, tril_ref, triu_ref = ins
        dxbc_ref, ddtr_ref, acc_ref = outs
        dh_ref, dxs_ref, t_ref, accb_ref, acca_ref, accd_ref = scr
        c = pl.program_id(0)
        cm.start(copies, c == 0)

        @pl.when(c == 0)
        def _():
            dh_ref[...] = jnp.zeros_like(dh_ref)
            accb_ref[...] = jnp.zeros_like(accb_ref)
            acca_ref[...] = jnp.zeros_like(acca_ref)
            accd_ref[...] = jnp.zeros_like(accd_ref)

        dtr, dt, a, acs, dtx, x, xdt, e, dsx = _ssd_common(xbc_ref, dtr_ref, bias_ref, alog_ref, expand_ref, tril_ref)
        acs_t = acs.T
        xb = xdt.astype(bf16)
        xdf = xdt * dsx
        xd = xdf.astype(bf16)
        dyv = dy_ref[...].astype(f32)
        dyb = dy_ref[...]
        dye = (dyv * e).astype(bf16)
        causal = tril_ref[...] > 0
        lane = lax.broadcasted_iota(jnp.int32, (CH, LANES), 1)
        subl = lax.broadcasted_iota(jnp.int32, (LANES, CH), 0)
        ccol = jnp.zeros((CH, LANES), f32)
        rrow = jnp.zeros((LANES, CH), f32)
        last_row = lax.broadcasted_iota(jnp.int32, (CH, 1), 0) == CH - 1
        for g in range(NG):
            gs = slice(g * GW, (g + 1) * GW)
            bsl = slice(DI + g * NS, DI + (g + 1) * NS)
            csl = slice(DI + NG * NS + g * NS, DI + NG * NS + (g + 1) * NS)
            bg = xbc_ref[:, bsl]
            cg = xbc_ref[:, csl]
            cb = _dot(cg, bg, "nt")
            hpb = hp_ref[0, g]
            dhn = dh_ref[g]
            dhnb = dhn.astype(bf16)
            yoff = _dot(cg, hpb) * e[:, gs]
            dxd = _dot(bg, dhnb)
            t2 = dxd * xdf[:, gs]
            t3 = jnp.sum(dhn * hpb.astype(f32), axis=0, keepdims=True) * e[CH - 1:CH, gs]
            t_ref[:, gs] = dyv[:, gs] * yoff - t2 + jnp.where(last_row, jnp.sum(t2, axis=0, keepdims=True) + t3, 0.0)
            dxs_ref[:, gs] = dxd * dsx[:, gs]
            dcg = _dot(dye[:, gs], hpb, "nt")
            dbg = _dot(xd[:, gs], dhnb, "nt")
            dh_ref[g] = dhn * e[CH - 1:CH, gs] + _dot(cg, dye[:, gs], "tn")
            dcb = jnp.zeros((CH, CH), f32)
            for j in range(NH // NG):
                hh = g * (NH // NG) + j
                hs = slice(hh * HD, (hh + 1) * HD)
                lm = _ssd_lmat(acs, acs_t, hh, causal)
                m = cb * lm
                dm = _dot(dyb[:, hs], xb[:, hs], "nt")
                gm = dm * m
                ccol = ccol + jnp.sum(gm, axis=1, keepdims=True) * (lane == hh).astype(f32)
                rrow = rrow + jnp.sum(gm, axis=0, keepdims=True) * (subl == hh).astype(f32)
                dcb = dcb + dm * lm
                dxs_ref[:, hs] += _dot(m.astype(bf16), dyb[:, hs], "tn")
            dcbb = dcb.astype(bf16)
            dxbc_ref[:, csl] = (dcg + _dot(dcbb, bg)).astype(bf16)
            dxbc_ref[:, bsl] = (dbg + _dot(dcbb, cg, "tn")).astype(bf16)
        dxf = dxs_ref[...]
        dxbc_ref[:, 0:DI] = (dxf * dtx + dxr_ref[...] * dyv).astype(bf16)
        expt = expt_ref[...]
        dacs = ccol - rrow.T + _dot2_right(t_ref[...], expt)
        dadt = _dot3_left(triu_ref[...], dacs)
        ddt = _dot2_right(dxf * x, expt) + dadt * a
        ddtr = ddt * _sigmoid(dtr)
        ddtr_ref[...] = ddtr
        accb_ref[...] += ddtr
        acca_ref[...] += dadt * dt
        accd_ref[...] += _dot2_right(dyv * x, expt)

        @pl.when(c == nc - 1)
        def _():
            acc_ref[...] = jnp.zeros_like(acc_ref)
            acc_ref[0:1, :] = jnp.sum(accb_ref[...], axis=0, keepdims=True)
            acc_ref[1:2, :] = jnp.sum(acca_ref[...], axis=0, keepdims=True) * a
            acc_ref[2:3, :] = jnp.sum(accd_ref[...], axis=0, keepdims=True)

        cm.wait(copies, c == nc - 1)

    par = lambda shape: pl.BlockSpec(shape, lambda c: (0,) * len(shape))
    rev = lambda c: (nc - 1 - c, 0)
    res = pl.pallas_call(
        body, name="ssd_bwd", grid=(nc,),
        in_specs=[pl.BlockSpec((CH, CONVD), rev), pl.BlockSpec((CH, LANES), rev), pl.BlockSpec((CH, DI), rev),
                  pl.BlockSpec((1, NG, NS, GW), lambda c: (nc - 1 - c, 0, 0, 0)),
                  par((1, LANES)), par((1, LANES)), par((1, DI)), par((LANES, DI)), par((DI, LANES)),
                  par((CH, CH)), par((CH, CH))] + cm.in_specs(),
        out_specs=[pl.BlockSpec((CH, CONVD), rev), pl.BlockSpec((CH, LANES), rev), par((8, LANES))] + cm.out_specs(),
        out_shape=[jax.ShapeDtypeStruct((S, CONVD), bf16), jax.ShapeDtypeStruct((S, LANES), f32),
                   jax.ShapeDtypeStruct((8, LANES), f32)] + cm.out_shape(),
        scratch_shapes=[pltpu.VMEM((NG, NS, GW), f32), pltpu.VMEM((CH, DI), f32), pltpu.VMEM((CH, DI), f32),
                        pltpu.VMEM((CH, LANES), f32), pltpu.VMEM((CH, LANES), f32),
                        pltpu.VMEM((CH, LANES), f32)] + cm.scratch(),
        compiler_params=_cp("arbitrary", side_effects=bool(cm.n)),
    )(xbc, dtr, dy, hprev, bias, alog, dx_row, expand, expand_t, tril, triu, *cm.bufs)
    return res[0], res[1], res[2], res[3:]


def rope_tables(pos_row, ts):
    S = pos_row.shape[1]
    half = AD // 2
    inv = ROPE_THETA ** (-jnp.arange(half, dtype=f32) * 2.0 / AD)
    inv_col = jnp.tile(inv, 2)[:, None]

    def body(p_ref, inv_ref, cos_ref, sin_ref):
        ang = inv_ref[...] * p_ref[...].astype(f32)
        row = lax.broadcasted_iota(jnp.int32, ang.shape, 0)
        cos_ref[...] = jnp.cos(ang)
        sin_ref[...] = jnp.where(row < half, -1.0, 1.0) * jnp.sin(ang)

    o = jax.ShapeDtypeStruct((AD, S), f32)
    return pl.pallas_call(
        body, name="rope_tables", grid=(S // ts,),
        in_specs=[pl.BlockSpec((1, ts), lambda i: (0, i)), pl.BlockSpec((AD, 1), lambda i: (0, 0))],
        out_specs=[pl.BlockSpec((AD, ts), lambda i: (0, i))] * 2, out_shape=[o, o],
        compiler_params=_cp("parallel"),
    )(pos_row, inv_col)


def _partner(t):
    half = AD // 2
    return jnp.concatenate([t[h * AD + o:h * AD + o + half] for h in range(t.shape[0] // AD) for o in (half, 0)], axis=0)


def _rope(t, cos, sin):
    reps = t.shape[0] // AD
    return t * jnp.tile(cos, (reps, 1)) + _partner(t) * jnp.tile(sin, (reps, 1))


def _rope_t(d, cos, sin):
    reps = d.shape[0] // AD
    return d * jnp.tile(cos, (reps, 1)) - _partner(d) * jnp.tile(sin, (reps, 1))


def _lanes_of_group(t, g):
    return jnp.concatenate([t[(g * REP + r) * AD:(g * REP + r + 1) * AD] for r in range(REP)], axis=1)


def _attn_probs(qg, kp, kc, sink_ref, g, not_first):
    n = qg.shape[1]
    s = lax.broadcasted_iota(jnp.int32, (WIN, n), 0)
    t = lax.broadcasted_iota(jnp.int32, (WIN, n), 1) % WIN
    neg = -1e30
    sink = jnp.concatenate([jnp.broadcast_to(sink_ref[0:1, g * REP + r:g * REP + r + 1], (1, WIN)) for r in range(REP)],
                           axis=1)
    sp = jnp.where(jnp.logical_and(s > t, not_first), _dot(kp, qg, "tn"), neg)
    sc = jnp.where(s <= t, _dot(kc, qg, "tn"), neg)
    m = jnp.maximum(jnp.maximum(jnp.max(sp, axis=0, keepdims=True), jnp.max(sc, axis=0, keepdims=True)), sink)
    pp = jnp.exp(sp - m)
    pc = jnp.exp(sc - m)
    ps = jnp.exp(sink - m)
    inv = 1.0 / (jnp.sum(pp, axis=0, keepdims=True) + jnp.sum(pc, axis=0, keepdims=True) + ps)
    return pp * inv, pc * inv, ps * inv


def attn_fwd(qt, kvt, cos, sin, sinks):
    S = qt.shape[1]
    nb = S // WIN
    cur = lambda i: (0, i)
    prev = lambda i: (0, jnp.maximum(i - 1, 0))

    def body(q_ref, kv_ref, kvp_ref, cos_ref, sin_ref, cosp_ref, sinp_ref, sink_ref, o_ref):
        i = pl.program_id(0)
        q = (_rope(q_ref[...].astype(f32), cos_ref[...], sin_ref[...]) * (AD ** -0.5)).astype(bf16)
        kc = _rope(kv_ref[0:KVW, :].astype(f32), cos_ref[...], sin_ref[...]).astype(bf16)
        kp = _rope(kvp_ref[0:KVW, :].astype(f32), cosp_ref[...], sinp_ref[...]).astype(bf16)
        for g in range(KVH):
            ks = slice(g * AD, (g + 1) * AD)
            vs = slice(KVW + g * AD, KVW + (g + 1) * AD)
            pp, pc, _ = _attn_probs(_lanes_of_group(q, g), kp[ks], kc[ks], sink_ref, g, i > 0)
            o = _dot(kvp_ref[vs, :], pp.astype(bf16)) + _dot(kv_ref[vs, :], pc.astype(bf16))
            for r in range(REP):
                h = g * REP + r
                o_ref[h * AD:(h + 1) * AD, :] = o[:, r * WIN:(r + 1) * WIN].astype(bf16)

    tab = pl.BlockSpec((AD, WIN), cur)
    tabp = pl.BlockSpec((AD, WIN), prev)
    return pl.pallas_call(
        body, name="attn_fwd", grid=(nb,),
        in_specs=[pl.BlockSpec((D, WIN), cur), pl.BlockSpec((2 * KVW, WIN), cur), pl.BlockSpec((2 * KVW, WIN), prev),
                  tab, tab, tabp, tabp, pl.BlockSpec((1, LANES), lambda i: (0, 0))],
        out_specs=pl.BlockSpec((D, WIN), cur),
        out_shape=jax.ShapeDtypeStruct((D, S), bf16),
        compiler_params=_cp("parallel"),
    )(qt, kvt, kvt, cos, sin, cos, sin, sinks)


def attn_bwd(qt, kvt, cos, sin, sinks, daot, comm=None):
    S = qt.shape[1]
    nb = S // WIN
    cur = lambda i: (0, jnp.minimum(i, nb - 1))
    prev = lambda i: (0, jnp.maximum(i - 1, 0))
    cm = _Comm(comm)

    def body(*refs):
        ins, (dq_ref, dkv_ref, ds_ref), scr, copies = cm.split(refs, 9, 3)
        q_ref, kv_ref, kvp_ref, cos_ref, sin_ref, cosp_ref, sinp_ref, sink_ref, do_ref = ins
        ck_ref, cv_ref, dqs_ref, dkp_ref, dvp_ref, dkc_ref, dvc_ref, accs_ref = scr
        i = pl.program_id(0)
        cm.start(copies, i == 0)

        @pl.when(i == 0)
        def _():
            ck_ref[...] = jnp.zeros_like(ck_ref)
            cv_ref[...] = jnp.zeros_like(cv_ref)
            accs_ref[...] = jnp.zeros_like(accs_ref)

        @pl.when(i == nb)
        def _():
            dkp_ref[...] = jnp.zeros_like(dkp_ref)
            dvp_ref[...] = jnp.zeros_like(dvp_ref)

        @pl.when(i < nb)
        def _():
            q = (_rope(q_ref[...].astype(f32), cos_ref[...], sin_ref[...]) * (AD ** -0.5)).astype(bf16)
            kc = _rope(kv_ref[0:KVW, :].astype(f32), cos_ref[...], sin_ref[...]).astype(bf16)
            kp = _rope(kvp_ref[0:KVW, :].astype(f32), cosp_ref[...], sinp_ref[...]).astype(bf16)
            do = do_ref[...]
            for g in range(KVH):
                ks = slice(g * AD, (g + 1) * AD)
                vs = slice(KVW + g * AD, KVW + (g + 1) * AD)
                qg = _lanes_of_group(q, g)
                dog = _lanes_of_group(do, g)
                pp, pc, ps = _attn_probs(qg, kp[ks], kc[ks], sink_ref, g, i > 0)
                dpp = _dot(kvp_ref[vs, :], dog, "tn")
                dpc = _dot(kv_ref[vs, :], dog, "tn")
                delta = jnp.sum(pp * dpp + pc * dpc, axis=0, keepdims=True)
                dsp = (pp * (dpp - delta)).astype(bf16)
                dsc = (pc * (dpc - delta)).astype(bf16)
                accs_ref[g:g + 1, :] -= ps * delta
                dqg = (_dot(kp[ks], dsp) + _dot(kc[ks], dsc)) * (AD ** -0.5)
                for r in range(REP):
                    h = g * REP + r
                    dqs_ref[h * AD:(h + 1) * AD, :] = dqg[:, r * WIN:(r + 1) * WIN]
                dkp_ref[ks, :] = _dot(qg, dsp, "nt")
                dkc_ref[ks, :] = _dot(qg, dsc, "nt")
                dvp_ref[ks, :] = _dot(dog, pp.astype(bf16), "nt")
                dvc_ref[ks, :] = _dot(dog, pc.astype(bf16), "nt")
            dq_ref[...] = _rope_t(dqs_ref[...], cos_ref[...], sin_ref[...]).astype(bf16)

        dkv_ref[0:KVW, :] = _rope_t(ck_ref[...] + dkp_ref[...], cosp_ref[...], sinp_ref[...]).astype(bf16)
        dkv_ref[KVW:2 * KVW, :] = (cv_ref[...] + dvp_ref[...]).astype(bf16)

        @pl.when(i < nb)
        def _():
            ck_ref[...] = dkc_ref[...]
            cv_ref[...] = dvc_ref[...]

        @pl.when(i == nb)
        def _():
            lane = lax.broadcasted_iota(jnp.int32, (1, LANES), 1)
            row = jnp.zeros((1, LANES), f32)
            for h in range(AH):
                part = accs_ref[h // REP:h // REP + 1, (h % REP) * WIN:(h % REP + 1) * WIN]
                row = row + jnp.where(lane == h, jnp.sum(part, axis=1, keepdims=True), 0.0)
            ds_ref[...] = jnp.zeros_like(ds_ref)
            ds_ref[0:1, :] = row

        cm.wait(copies, i == nb)

    tab = pl.BlockSpec((AD, WIN), cur)
    tabp = pl.BlockSpec((AD, WIN), prev)
    kvs = lambda: pltpu.VMEM((KVW, WIN), f32)
    res = pl.pallas_call(
        body, name="attn_bwd", grid=(nb + 1,),
        in_specs=[pl.BlockSpec((D, WIN), cur), pl.BlockSpec((2 * KVW, WIN), cur), pl.BlockSpec((2 * KVW, WIN), prev),
                  tab, tab, tabp, tabp, pl.BlockSpec((1, LANES), lambda i: (0, 0)),
                  pl.BlockSpec((D, WIN), cur)] + cm.in_specs(),
        out_specs=[pl.BlockSpec((D, WIN), cur), pl.BlockSpec((2 * KVW, WIN), prev),
                   pl.BlockSpec((8, LANES), lambda i: (0, 0))] + cm.out_specs(),
        out_shape=[jax.ShapeDtypeStruct((D, S), bf16), jax.ShapeDtypeStruct((2 * KVW, S), bf16),
                   jax.ShapeDtypeStruct((8, LANES), f32)] + cm.out_shape(),
        scratch_shapes=[kvs(), kvs(), pltpu.VMEM((D, WIN), f32), kvs(), kvs(), kvs(), kvs(),
                        pltpu.VMEM((8, REP * WIN), f32)] + cm.scratch(),
        compiler_params=_cp("arbitrary", side_effects=bool(cm.n)),
    )(qt, kvt, kvt, cos, sin, cos, sin, sinks, daot, *cm.bufs)
    return res[0], res[1], res[2], res[3:]


ADAM_C1 = 1.0 / (1.0 - ADAM_B1 ** ADAM_STEP)
ADAM_C2 = 1.0 / (1.0 - ADAM_B2 ** ADAM_STEP)


def _adam_update(g, w, m, v):
    nm = ADAM_B1 * m + (1.0 - ADAM_B1) * g
    nv = ADAM_B2 * v + (1.0 - ADAM_B2) * (g * g)
    return -ADAM_LR * ((nm * ADAM_C1) / (jnp.sqrt(nv * ADAM_C2) + ADAM_EPS) + ADAM_WD * w), nm, nv


def adamw(parts, w, m, v, tr, name):
    n, R, C = parts.shape

    def body(p_ref, w_ref, m_ref, v_ref, g_ref, d_ref, nm_ref, nv_ref):
        def grp(g0, _):
            r0 = pl.multiple_of(g0 * RG, RG)
            rows = pl.ds(r0, RG)
            g = p_ref[0, rows, :].astype(f32)
            for k in range(1, n):
                g = g + p_ref[k, rows, :].astype(f32)
            d, nm, nv = _adam_update(g, w_ref[rows, :], m_ref[rows, :], v_ref[rows, :])
            g_ref[rows, :] = g
            d_ref[rows, :] = d
            nm_ref[rows, :] = nm
            nv_ref[rows, :] = nv
            return 0

        lax.fori_loop(0, tr // RG, grp, 0)

    row = pl.BlockSpec((tr, C), lambda i: (i, 0))
    o = jax.ShapeDtypeStruct((R, C), f32)
    return pl.pallas_call(
        body, name=name, grid=(R // tr,),
        in_specs=[pl.BlockSpec((n, tr, C), lambda i: (0, i, 0)), row, row, row],
        out_specs=[row, row, row, row], out_shape=[o, o, o, o],
        compiler_params=_cp("parallel"),
    )(parts, w, m, v)


SMALL_ROW = (("norm_mix_post_w", D), ("norm_ffn_pre_w", D), ("norm_ffn_post_w", D), ("ssd_norm_w", DI),
             ("ssd_conv_b", CONVD), ("ffn_conv_b", 2 * FF), ("ssd_dt_bias", NH), ("ssd_a_log", NH), ("ssd_d", NH),
             ("attn_sinks", AH), ("loss", 1))
CONV_BLOCK = 1152
SSD_CONV_COLS = CONVD // N_DEV
FFN_CONV_COLS = 2 * FF // N_DEV


def _row_offsets():
    off, o = {}, 0
    for name, n in SMALL_ROW:
        off[name] = (o, n)
        o += -(-n // LANES) * LANES
    return off, o


def adamw_small(recv_row, recv_pre, recv_conv, params):
    off, _ = _row_offsets()
    names = list(params)
    n = len(names)

    def total(ref, rows, lo, width):
        g = ref[0, rows, lo:lo + width]
        for d in range(1, N_DEV):
            g = g + ref[d, rows, lo:lo + width]
        return g

    def grad_of(name, row_ref, pre_ref, conv_ref):
        if name == "norm_mix_pre_w":
            return total(pre_ref, slice(0, 1), 0, D)
        if name == "ssd_conv_w":
            return total(conv_ref, slice(0, SSD_K), 0, SSD_CONV_COLS)
        if name == "ffn_conv_w":
            return total(conv_ref, slice(0, FFN_K), 3 * LANES, FFN_CONV_COLS)
        o, width = off[name]
        return total(row_ref, slice(0, 1), o, width)

    def body(row_ref, pre_ref, conv_ref, *refs):
        ins, outs = refs[:3 * n], refs[3 * n:]
        for k, name in enumerate(names):
            w_ref, m_ref, v_ref = ins[3 * k:3 * k + 3]
            g_ref, d_ref, nm_ref, nv_ref = outs[4 * k:4 * k + 4]
            g = grad_of(name, row_ref, pre_ref, conv_ref)
            d, nm, nv = _adam_update(g, w_ref[...], m_ref[...], v_ref[...])
            g_ref[...] = g
            d_ref[...] = d
            nm_ref[...] = nm
            nv_ref[...] = nv
        outs[4 * n][...] = total(row_ref, slice(0, 1), off["loss"][0], LANES)

    flat = [t for name in names for t in params[name]]
    out_shape = [jax.ShapeDtypeStruct(params[name][0].shape, f32) for name in names for _ in range(4)]
    res = pl.pallas_call(
        body, name="adamw_small",
        out_shape=out_shape + [jax.ShapeDtypeStruct((1, LANES), f32)],
        compiler_params=pltpu.CompilerParams(vmem_limit_bytes=VMEM_LIMIT),
    )(recv_row, recv_pre, recv_conv, *flat)
    return {name: res[4 * k:4 * k + 4] for k, name in enumerate(names)}, res[4 * n]


def _pad_rows8(w):
    return jnp.pad(w, ((0, 8 - w.shape[0]), (0, 0)))


def _pad_lanes(v):
    return jnp.pad(v.reshape(1, -1), ((0, 0), (0, LANES - v.size)))


WEIGHTS = ('norm_mix_pre_w', 'w_in', 'ssd_conv_w', 'ssd_conv_b', 'ssd_dt_bias', 'ssd_a_log', 'ssd_d', 'ssd_norm_w',
           'ssd_w_out', 'attn_sinks', 'attn_w_out', 'w_mix_out', 'norm_mix_post_w', 'norm_ffn_pre_w', 'ffn_w_up',
           'ffn_conv_w', 'ffn_conv_b', 'ffn_w_down', 'norm_ffn_post_w')
W_IN_ROWS = IN_DIM // N_DEV
W_IN_PAD = 1104
TS = 256


def kernel(x, positions, norm_mix_pre_w, w_in, ssd_conv_w, ssd_conv_b, ssd_dt_bias, ssd_a_log, ssd_d, ssd_norm_w, ssd_w_out, attn_sinks, attn_w_out, w_mix_out, norm_mix_post_w, norm_ffn_pre_w, ffn_w_up, ffn_conv_w, ffn_conv_b, ffn_w_down, norm_ffn_post_w, loss_target, m_norm_mix_pre_w, m_w_in, m_ssd_conv_w, m_ssd_conv_b, m_ssd_dt_bias, m_ssd_a_log, m_ssd_d, m_ssd_norm_w, m_ssd_w_out, m_attn_sinks, m_attn_w_out, m_w_mix_out, m_norm_mix_post_w, m_norm_ffn_pre_w, m_ffn_w_up, m_ffn_conv_w, m_ffn_conv_b, m_ffn_w_down, m_norm_ffn_post_w, v_norm_mix_pre_w, v_w_in, v_ssd_conv_w, v_ssd_conv_b, v_ssd_dt_bias, v_ssd_a_log, v_ssd_d, v_ssd_norm_w, v_ssd_w_out, v_attn_sinks, v_attn_w_out, v_w_mix_out, v_norm_mix_post_w, v_norm_ffn_pre_w, v_ffn_w_up, v_ffn_conv_w, v_ffn_conv_b, v_ffn_w_down, v_norm_ffn_post_w):
    a = locals()
    r2 = lambda t: t.reshape(t.shape[-2], t.shape[-1])
    w = {n: r2(a[n]) for n in WEIGHTS}
    m = {n: r2(a["m_" + n]) for n in WEIGHTS}
    v = {n: r2(a["v_" + n]) for n in WEIGHTS}
    xs, target = x[0], loss_target[0]
    S = xs.shape[0]
    ts = TS

    w_in_blk = jnp.pad(w["w_in"].T.astype(bf16), ((0, W_IN_PAD - W_IN_ROWS), (0, 0)))
    conv_blk = jnp.concatenate([_pad_rows8(w["ssd_conv_w"]), _pad_rows8(w["ffn_conv_w"]),
                                jnp.zeros((8, CONV_BLOCK - SSD_CONV_COLS - FFN_CONV_COLS), f32)], axis=1)
    g_in, g_conv = gather_two_level([w_in_blk, conv_blk], "gather_first")
    wt = g_in[:, :W_IN_ROWS].reshape(IN_DIM, D)
    w_main_t = jnp.concatenate([wt[IN_OFF[0]:IN_OFF[1]], wt[IN_OFF[6]:IN_OFF[8]], wt[IN_OFF[1]:IN_OFF[2]]], axis=0)
    w_q_t = wt[IN_OFF[3]:IN_OFF[4]]
    w_kv_t = wt[IN_OFF[4]:IN_OFF[6]]
    w_dt_t = jnp.pad(wt[IN_OFF[2]:IN_OFF[3]], ((0, LANES - NH), (0, 0)))
    conv_w8 = g_conv[:, :, 0:SSD_CONV_COLS].transpose(1, 0, 2).reshape(8, CONVD)
    fconv_w8 = g_conv[:, :, SSD_CONV_COLS:SSD_CONV_COLS + FFN_CONV_COLS].transpose(1, 0, 2).reshape(8, 2 * FF)
    bias = _pad_lanes(w["ssd_dt_bias"])
    alog = _pad_lanes(w["ssd_a_log"])
    dx_row = jnp.repeat(w["ssd_d"].reshape(-1), HD).reshape(1, DI)
    sinks = _pad_lanes(w["attn_sinks"])

    u = prenorm_fwd(xs, w["norm_mix_pre_w"], ts)
    later = [w["ssd_w_out"].astype(bf16), w["attn_w_out"].astype(bf16), w["w_mix_out"].astype(bf16)]
    proj, (g_so, g_ao, g_mix) = mm(u, w_main_t, "nt", bf16, "mm_proj", comm=(later, (False,) * 3))
    w_ssd_out, w_attn_out, w_mix = g_so.reshape(DI, D), g_ao.reshape(D, D), g_mix.reshape(D, D)
    qt = mm(w_q_t, u, "nt", bf16, "mm_q")
    kvt = mm(w_kv_t, u, "nt", bf16, "mm_kv")
    dtr = mm(u, w_dt_t, "nt", f32, "mm_dt")
    xbc, conv_c = ssdconv_fwd(proj, conv_w8, w["ssd_conv_b"], ts)
    y, hprev, (g_up,) = ssd_fwd(xbc, dtr, bias, alog, dx_row, comm=([w["ffn_w_up"].T.astype(bf16)], (False,)))
    w_up_t = g_up.reshape(2 * FF, D)
    yn = gnorm_fwd(y, proj, w["ssd_norm_w"], ts)
    ys, (g_down,) = mm(yn, w_ssd_out, "nn", bf16, "mm_ssd_out", comm=([w["ffn_w_down"].astype(bf16)], (False,)))
    w_down = g_down.reshape(FF, D)
    cos, sin = rope_tables(positions, ts)
    aot = attn_fwd(qt, kvt, cos, sin, sinks)
    ya = mm(aot, w_attn_out, "tn", bf16, "mm_attn_out")
    merged = merge_fwd(proj, ys, ya, ts)
    mo = mm(merged, w_mix, "nn", f32, "mm_mix")
    x1, h = post_fwd(xs, mo, w["norm_mix_post_w"], w["norm_ffn_pre_w"], ts)
    up = mm(h, w_up_t, "nt", bf16, "mm_up")
    act, gate_c, val_c = ffnact_fwd(up, fconv_w8, w["ffn_conv_b"], ts)
    ff = mm(act, w_down, "nn", f32, "mm_down")
    loss_blk, dout, dff, g_post2 = loss_head(x1, ff, target, w["norm_ffn_post_w"], ts)

    dact = mm(dff, w_down, "nt", bf16, "mm_dact")
    gw_down = mm(act, dff, "tn", bf16, "mm_g_down")
    dgate, dval = ffnact_bwd(dact, gate_c, val_c, ts)
    dup_pre, g_fconv_a = dwconv_bwd(dgate, up, 0, fconv_w8, 0, FFN_K, FF, ts, "ffnconv_bwd_gate", out_cols=2 * FF)
    dup_pre, g_fconv_b = dwconv_bwd(dval, up, 1, fconv_w8, 1, FFN_K, FF, ts, "ffnconv_bwd_val", into=dup_pre, ocb=1,
                                    out_cols=2 * FF)
    g_fconv = jnp.concatenate([g_fconv_a, g_fconv_b], axis=1)
    dh, (r_down,) = mm(dup_pre, w_up_t, "nn", bf16, "mm_dh", comm=([gw_down.reshape(N_DEV, FF // N_DEV, D)], (True,)))
    gw_up_t = mm(dup_pre, h, "tn", bf16, "mm_g_up")
    dx1, dmo, g_norms = post_bwd(dout, dh, x1, mo, w["norm_mix_post_w"], w["norm_ffn_pre_w"], ts)
    dmerged = mm(dmo, w_mix, "nt", bf16, "mm_dmerged")
    gw_mix = mm(merged, dmo, "tn", bf16, "mm_g_mix")
    dys, dya, dproj = merge_bwd(dmerged, proj, ys, ya, ts)
    daot = mm(w_attn_out, dya, "nt", bf16, "mm_dao")
    gw_attn_out = mm(aot, dya, "nn", bf16, "mm_g_attn_out")
    dqt, dkvt, g_sinks, (r_up,) = attn_bwd(qt, kvt, cos, sin, sinks, daot,
                                           comm=([gw_up_t.reshape(N_DEV, 2 * FF // N_DEV, D)], (True,)))
    dyn = mm(dys, w_ssd_out, "nt", bf16, "mm_dyn")
    gw_ssd_out = mm(yn, dys, "tn", bf16, "mm_g_ssd_out")
    dy, dproj, g_gnorm = gnorm_bwd(dyn, y, proj, w["ssd_norm_w"], dproj, ts)
    sends = [gw_ssd_out.reshape(N_DEV, DI // N_DEV, D), gw_attn_out.reshape(N_DEV, D // N_DEV, D),
             gw_mix.reshape(N_DEV, D // N_DEV, D)]
    dxbc, ddtr, g_ssd, (r_so, r_ao, r_mix) = ssd_bwd(xbc, dtr, dy, hprev, bias, alog, dx_row, comm=(sends, (True,) * 3))
    dproj, g_conv_w = dwconv_bwd(dxbc, proj, C_XBC // 1024, conv_w8, 0, SSD_K, 1024, ts, "ssdconv_bwd", act_c=conv_c,
                                 into=dproj, ocb=C_XBC // 1024, out_cols=PM)
    ddtr_b = ddtr.astype(bf16)
    g_main_t = mm(dproj, u, "tn", bf16, "mm_g_in")
    g_q_t = mm(dqt, u, "nn", bf16, "mm_g_q")
    g_kv_t = mm(dkvt, u, "nn", bf16, "mm_g_kv")
    g_dt_t = mm(ddtr_b, u, "tn", bf16, "mm_g_dt")
    g_wt = jnp.concatenate([g_main_t[C_Z:C_GS], g_main_t[C_XBC:PM], g_dt_t[:NH], g_q_t, g_kv_t, g_main_t[C_GS:C_XBC]],
                           axis=0)
    send_in = jnp.pad(g_wt.reshape(N_DEV, W_IN_ROWS, D), ((0, 0), (0, W_IN_PAD - W_IN_ROWS), (0, 0)))
    pieces = {"norm_mix_post_w": g_norms[1:2], "norm_ffn_pre_w": g_norms[0:1], "norm_ffn_post_w": g_post2[0:1],
              "ssd_norm_w": g_gnorm[0:1], "ssd_conv_b": g_conv_w[7:8], "ffn_conv_b": g_fconv[7:8],
              "ssd_dt_bias": g_ssd[0:1], "ssd_a_log": g_ssd[1:2], "ssd_d": g_ssd[2:3], "attn_sinks": g_sinks[0:1],
              "loss": loss_blk[0:1]}
    row = jnp.concatenate([jnp.pad(pieces[n][:, :min(k, pieces[n].shape[1])],
                                   ((0, 0), (0, -(-k // LANES) * LANES - min(k, pieces[n].shape[1]))))
                           for n, k in SMALL_ROW], axis=1)
    send_row = jnp.pad(row, ((0, 7), (0, 0)))
    send_conv = jnp.concatenate(
        [g_conv_w.reshape(8, N_DEV, SSD_CONV_COLS).transpose(1, 0, 2),
         g_fconv.reshape(8, N_DEV, FFN_CONV_COLS).transpose(1, 0, 2),
         jnp.zeros((N_DEV, 8, CONV_BLOCK - SSD_CONV_COLS - FFN_CONV_COLS), f32)], axis=2)
    du_a, (r_in, recv_row, recv_conv) = mm(dproj, w_main_t, "nn", bf16, "mm_du",
                                           comm=([send_in, send_row, send_conv], (True, False, True)))
    du_b = mm(dkvt, w_kv_t, "tn", bf16, "mm_du_kv")
    du_c = mm(ddtr_b, w_dt_t, "nn", bf16, "mm_du_dt")
    du_d = mm(dqt, w_q_t, "tn", bf16, "mm_du_q")
    grad_x, g_pre = prenorm_bwd(xs, w["norm_mix_pre_w"], (du_a, du_b, du_c, du_d), dx1, ts)
    (recv_pre,) = exchange([g_pre], (False,), "gather_last")

    tpad = lambda t: jnp.pad(t.T, ((0, W_IN_PAD - W_IN_ROWS), (0, 0)))
    o_in = [t[:W_IN_ROWS].T for t in adamw(r_in, tpad(w["w_in"]), tpad(m["w_in"]), tpad(v["w_in"]), 368, "adamw_w_in")]
    o_up = [t.T for t in adamw(r_up, w["ffn_w_up"].T, m["ffn_w_up"].T, v["ffn_w_up"].T, 352, "adamw_w_up")]
    big = {"w_in": o_in, "ffn_w_up": o_up,
           "ssd_w_out": adamw(r_so, w["ssd_w_out"], m["ssd_w_out"], v["ssd_w_out"], 256, "adamw_ssd_out"),
           "attn_w_out": adamw(r_ao, w["attn_w_out"], m["attn_w_out"], v["attn_w_out"], 128, "adamw_attn_out"),
           "w_mix_out": adamw(r_mix, w["w_mix_out"], m["w_mix_out"], v["w_mix_out"], 128, "adamw_mix"),
           "ffn_w_down": adamw(r_down, w["ffn_w_down"], m["ffn_w_down"], v["ffn_w_down"], 352, "adamw_down")}
    small_names = [n for n in WEIGHTS if n not in big]
    small, loss_row = adamw_small(recv_row, recv_pre, recv_conv, {n: (w[n], m[n], v[n]) for n in small_names})

    outs = [loss_row[0, 0], grad_x[None]]
    for k in range(4):
        for n in WEIGHTS:
            outs.append((big[n][k] if n in big else small[n][k]).reshape(a[n].shape))
    return tuple(outs)
```

```python
import jax
import jax.numpy as jnp
import numpy as np
from jax import lax
from jax.experimental import pallas as pl
from jax.experimental.pallas import tpu as pltpu

f32 = jnp.float32
bf16 = jnp.bfloat16

N_DEV = 8
D = 1024
DI = 2048
NH = 32
HD = 64
NG = 4
GW = DI // NG
NS = 128
CH = 128
CONVD = DI + 2 * NG * NS
SSD_K = 4
AH = 16
AD = 64
KVH = 4
REP = AH // KVH
KVW = KVH * AD
WIN = 128
FF = 2816
FFN_K = 3
EPS = 1e-6
ROPE_THETA = 10000.0
LANES = 128
RG = 16
CW = 256

C_Z, C_GS, C_GA, C_XBC, PM = 0, 2048, 3072, 4096, 7168
IN_SIZES = (DI, CONVD, NH, D, KVW, KVW, D, D)
IN_OFF = tuple(int(v) for v in np.cumsum((0,) + IN_SIZES))
IN_DIM = IN_OFF[-1]

ADAM_LR, ADAM_B1, ADAM_B2, ADAM_EPS, ADAM_WD, ADAM_STEP = 0.001, 0.9, 0.999, 1e-08, 0.01, 10

VMEM_LIMIT = 56 * 1024 * 1024


def _cp(*sem, side_effects=False):
    return pltpu.CompilerParams(dimension_semantics=sem, vmem_limit_bytes=VMEM_LIMIT, has_side_effects=side_effects)


def _dot(a, b, mode="nn"):
    dims = {"nn": (((1,), (0,)), ((), ())), "nt": (((1,), (1,)), ((), ())), "tn": (((0,), (0,)), ((), ()))}[mode]
    return lax.dot_general(a, b, dims, preferred_element_type=f32)


def _split3(v):
    hi = v.astype(bf16)
    r = v - hi.astype(f32)
    mid = r.astype(bf16)
    lo = (r - mid.astype(f32)).astype(bf16)
    return hi, mid, lo


def _dot3_left(m01, v):
    hi, mid, lo = _split3(v)
    return _dot(m01, hi) + _dot(m01, mid) + _dot(m01, lo)


def _dot3_right(v, m01):
    hi, mid, lo = _split3(v)
    return _dot(hi, m01) + _dot(mid, m01) + _dot(lo, m01)


def _dot2_right(v, m01):
    hi = v.astype(bf16)
    lo = (v - hi.astype(f32)).astype(bf16)
    return _dot(hi, m01) + _dot(lo, m01)


def _sigmoid(x):
    return 1.0 / (1.0 + jnp.exp(-x))


def _sigmoid_fast(x):
    return pl.reciprocal(1.0 + jnp.exp(-x), approx=True)


def _peer(k, x, y, c):
    return ((1 - x) if k & 4 else x, (1 - y) if k & 2 else y, (1 - c) if k & 1 else c)


def _xchg_copies(buf_refs, out_refs, send_sems, recv_sems, local_sems, personalised):
    x, y, c = lax.axis_index("x"), lax.axis_index("y"), lax.axis_index("c")
    me = 4 * x + 2 * y + c
    local, remote = [], []
    for b, (buf, out, pers) in enumerate(zip(buf_refs, out_refs, personalised)):
        local.append(pltpu.make_async_copy(buf.at[me] if pers else buf, out.at[me], local_sems.at[b]))
        for k in range(1, N_DEV):
            px, py, pc = _peer(k, x, y, c)
            s = b * (N_DEV - 1) + k - 1
            remote.append(pltpu.make_async_remote_copy(
                src_ref=buf.at[4 * px + 2 * py + pc] if pers else buf, dst_ref=out.at[me],
                send_sem=send_sems.at[s], recv_sem=recv_sems.at[s],
                device_id=(px, py, pc), device_id_type=pl.DeviceIdType.MESH))
    return local, remote


class _Comm:
    def __init__(self, comm):
        self.bufs, self.pers = comm if comm else ((), ())
        self.n = len(self.bufs)

    def in_specs(self):
        return [pl.BlockSpec(memory_space=pl.ANY)] * self.n

    out_specs = in_specs

    def out_shape(self):
        return [jax.ShapeDtypeStruct((N_DEV,) + tuple(b.shape[1:] if p else b.shape), b.dtype)
                for b, p in zip(self.bufs, self.pers)]

    def scratch(self):
        n = self.n
        return [pltpu.SemaphoreType.DMA((n * (N_DEV - 1),)), pltpu.SemaphoreType.DMA((n * (N_DEV - 1),)),
                pltpu.SemaphoreType.DMA((n,))] if n else []

    def split(self, refs, n_in, n_out):
        n = self.n
        ins, outs = refs[:n_in], refs[n_in + n:n_in + n + n_out]
        rest = refs[n_in + n + n_out + n:]
        if not n:
            return ins, outs, rest, None
        copies = _xchg_copies(refs[n_in:n_in + n], refs[n_in + n + n_out:n_in + n + n_out + n], *rest[-3:], self.pers)
        return ins, outs, rest[:-3], copies

    def start(self, copies, first):
        if copies:
            @pl.when(first)
            def _():
                for cp in copies[0] + copies[1]:
                    cp.start()

    def wait(self, copies, last):
        if copies:
            @pl.when(last)
            def _():
                for cp in copies[1]:
                    cp.wait_recv()
                for cp in copies[1]:
                    cp.wait_send()
                for cp in copies[0]:
                    cp.wait()


def exchange(bufs, personalised, name):
    cm = _Comm((bufs, personalised))

    def body(*refs):
        _, _, _, copies = cm.split(refs, 0, 0)
        cm.start(copies, True)
        cm.wait(copies, True)

    return pl.pallas_call(
        body, name=name, in_specs=cm.in_specs(), out_specs=cm.out_specs(), out_shape=cm.out_shape(),
        scratch_shapes=cm.scratch(), compiler_params=pltpu.CompilerParams(has_side_effects=True),
    )(*bufs)


def gather_two_level(bufs, name):
    n = len(bufs)
    per = N_DEV - 1

    def body(*refs):
        ins, outs = refs[:n], refs[n:2 * n]
        send_sems, recv_sems, local_sems = refs[2 * n:]
        x, y, c = lax.axis_index("x"), lax.axis_index("y"), lax.axis_index("c")
        me, sibling = (x, y, c), (x, y, 1 - c)
        chips = [(1 - x, y), (x, 1 - y), (1 - x, 1 - y)]

        def copy(b, k, block, to, src=None):
            dst = outs[b].at[4 * block[0] + 2 * block[1] + block[2]]
            return pltpu.make_async_remote_copy(
                src_ref=dst if src is None else src, dst_ref=dst,
                send_sem=send_sems.at[b * per + k], recv_sem=recv_sems.at[b * per + k],
                device_id=to, device_id_type=pl.DeviceIdType.MESH)

        mine = [pltpu.make_async_copy(ins[b], outs[b].at[4 * x + 2 * y + c], local_sems.at[b]) for b in range(n)]
        first = []
        for b in range(n):
            first.append(copy(b, 0, me, sibling, src=ins[b]))
            first += [copy(b, 1 + j, me, (*chip, c), src=ins[b]) for j, chip in enumerate(chips)]
        for cp in mine + first:
            cp.start()
        passed = []
        for j, chip in enumerate(chips):
            for b in range(n):
                copy(b, 1 + j, (*chip, c), me).wait_recv()
                passed.append(copy(b, 4 + j, (*chip, c), sibling))
                passed[-1].start()
        for b in range(n):
            copy(b, 0, sibling, me).wait_recv()
            for j, chip in enumerate(chips):
                copy(b, 4 + j, (*chip, 1 - c), me).wait_recv()
        for cp in first + passed:
            cp.wait_send()
        for cp in mine:
            cp.wait()

    hbm = pl.BlockSpec(memory_space=pl.ANY)
    return pl.pallas_call(
        body, name=name, in_specs=[hbm] * n, out_specs=[hbm] * n,
        out_shape=[jax.ShapeDtypeStruct((N_DEV,) + tuple(b.shape), b.dtype) for b in bufs],
        scratch_shapes=[pltpu.SemaphoreType.DMA((n * per,)), pltpu.SemaphoreType.DMA((n * per,)),
                        pltpu.SemaphoreType.DMA((n,))],
        compiler_params=pltpu.CompilerParams(has_side_effects=True),
    )(*bufs)


MM_TILES = (2176, 2048, 1408, 1024, 512, 256, 128)
MM_VMEM_BUDGET = 40 * 1024 * 1024


def _mm_tiles(M, N, K, out_bytes):
    cm = [t for t in MM_TILES if M % t == 0]
    cn = [t for t in MM_TILES if N % t == 0]
    ck = [t for t in MM_TILES if K % t == 0]
    best = None
    for bm in cm[:2]:
        for bn in cn:
            for bk in ck:
                need = 4 * (bm * bk + bk * bn) + bm * bn * (4 + 2 * out_bytes)
                if need <= MM_VMEM_BUDGET:
                    score = (bm * bn * bk, bk)
                    if best is None or score > best[0]:
                        best = (score, (bm, bn, bk))
    return best[1]


def mm(a, b, mode, out_dtype, name, comm=None):
    if mode == "nn":
        (M, K), (_, N) = a.shape, b.shape
    elif mode == "nt":
        (M, K), (N, _) = a.shape, b.shape
    else:
        (K, M), (_, N) = a.shape, b.shape
    bm, bn, bk = _mm_tiles(M, N, K, jnp.dtype(out_dtype).itemsize)
    gm, gn, nk = M // bm, N // bn, K // bk
    cm = _Comm(comm)

    def body(*refs):
        (a_ref, b_ref), (o_ref,), scr, copies = cm.split(refs, 2, 1)
        i, j, k = pl.program_id(0), pl.program_id(1), pl.program_id(2)
        cm.start(copies, jnp.logical_and(jnp.logical_and(i == 0, j == 0), k == 0))
        p = _dot(a_ref[...], b_ref[...], mode)
        if nk == 1:
            o_ref[...] = p.astype(o_ref.dtype)
        else:
            acc_ref = scr[0]

            @pl.when(k == 0)
            def _():
                acc_ref[...] = p

            @pl.when(k > 0)
            def _():
                acc_ref[...] += p

            @pl.when(k == nk - 1)
            def _():
                o_ref[...] = acc_ref[...].astype(o_ref.dtype)

        cm.wait(copies, jnp.logical_and(jnp.logical_and(i == gm - 1, j == gn - 1), k == nk - 1))

    if mode == "nn":
        a_spec = pl.BlockSpec((bm, bk), lambda i, j, k: (i, k))
        b_spec = pl.BlockSpec((bk, bn), lambda i, j, k: (k, j))
    elif mode == "nt":
        a_spec = pl.BlockSpec((bm, bk), lambda i, j, k: (i, k))
        b_spec = pl.BlockSpec((bn, bk), lambda i, j, k: (j, k))
    else:
        a_spec = pl.BlockSpec((bk, bm), lambda i, j, k: (k, i))
        b_spec = pl.BlockSpec((bk, bn), lambda i, j, k: (k, j))
    sem = ("arbitrary",) * 3 if cm.n else ("parallel", "parallel", "arbitrary")
    res = pl.pallas_call(
        body, name=name, grid=(gm, gn, nk),
        in_specs=[a_spec, b_spec] + cm.in_specs(),
        out_specs=[pl.BlockSpec((bm, bn), lambda i, j, k: (i, j))] + cm.out_specs(),
        out_shape=[jax.ShapeDtypeStruct((M, N), out_dtype)] + cm.out_shape(),
        scratch_shapes=([pltpu.VMEM((bm, bn), f32)] if nk > 1 else []) + cm.scratch(),
        compiler_params=_cp(*sem, side_effects=bool(cm.n)),
    )(a, b, *cm.bufs)
    return (res[0], res[1:]) if cm.n else res[0]


def _groups(ts, fn, carry=None, reverse=False, unroll=4, rg=RG):
    n = ts // rg
    if n == 1:
        return fn(0, carry)
    unroll = min(unroll, n)
    span = rg * unroll

    def body(g, c):
        r0 = pl.multiple_of((n // unroll - 1 - g if reverse else g) * span, span)
        for u in (range(unroll - 1, -1, -1) if reverse else range(unroll)):
            c = fn(pl.multiple_of(r0 + u * rg, rg), c)
        return c

    return lax.fori_loop(0, n // unroll, body, carry)


def _rms(x):
    return lax.rsqrt(jnp.mean(x * x, axis=-1, keepdims=True) + EPS)


def _rms_bwd(x, r, dn):
    n = x * r
    return r * (dn - n * jnp.mean(dn * n, axis=-1, keepdims=True))


NRG = 256


def _fold(x):
    return jnp.sum(x.reshape(x.shape[0] // 8, 8, x.shape[1]), axis=0)


def _flush(acc_ref, out_ref, row):
    out_ref[row:row + 1, :] = jnp.sum(acc_ref[...], axis=0, keepdims=True)


def prenorm_fwd(x, w, ts):
    S = x.shape[0]

    def body(x_ref, w_ref, u_ref):
        wv = w_ref[...]

        def grp(r0, _):
            xv = x_ref[pl.ds(r0, NRG), :]
            u_ref[pl.ds(r0, NRG), :] = (xv * _rms(xv) * wv).astype(bf16)

        _groups(ts, grp, rg=NRG)

    return pl.pallas_call(
        body, name="prenorm_fwd", grid=(S // ts,),
        in_specs=[pl.BlockSpec((ts, D), lambda i: (i, 0)), pl.BlockSpec((1, D), lambda i: (0, 0))],
        out_specs=pl.BlockSpec((ts, D), lambda i: (i, 0)),
        out_shape=jax.ShapeDtypeStruct((S, D), bf16),
        compiler_params=_cp("parallel"),
    )(x, w)


def prenorm_bwd(x, w, dus, dx1, ts, comm=None):
    S = x.shape[0]
    nt = S // ts
    nd = len(dus)
    cm = _Comm(comm)

    def body(*refs):
        ins, (gx_ref, gw_ref), (acc_ref,), copies = cm.split(refs, nd + 3, 2)
        x_ref, w_ref = ins[:2]
        du_refs, dx1_ref = ins[2:2 + nd], ins[2 + nd]
        i = pl.program_id(0)
        cm.start(copies, i == 0)
        wv = w_ref[...]

        @pl.when(i == 0)
        def _():
            acc_ref[...] = jnp.zeros_like(acc_ref)
            gw_ref[...] = jnp.zeros_like(gw_ref)

        def grp(r0, _):
            rows = pl.ds(r0, NRG)
            xv = x_ref[rows, :]
            r = _rms(xv)
            du = du_refs[0][rows, :].astype(f32)
            for d_ref in du_refs[1:]:
                du = du + d_ref[rows, :].astype(f32)
            gx_ref[rows, :] = dx1_ref[rows, :] + _rms_bwd(xv, r, du * wv)
            acc_ref[...] += _fold(du * xv * r)

        _groups(ts, grp, rg=NRG)

        @pl.when(i == nt - 1)
        def _():
            _flush(acc_ref, gw_ref, 0)

        cm.wait(copies, i == nt - 1)

    row = pl.BlockSpec((ts, D), lambda i: (i, 0))
    res = pl.pallas_call(
        body, name="prenorm_bwd", grid=(nt,),
        in_specs=[row, pl.BlockSpec((1, D), lambda i: (0, 0))] + [row] * (nd + 1) + cm.in_specs(),
        out_specs=[row, pl.BlockSpec((8, D), lambda i: (0, 0))] + cm.out_specs(),
        out_shape=[jax.ShapeDtypeStruct((S, D), f32), jax.ShapeDtypeStruct((8, D), f32)] + cm.out_shape(),
        scratch_shapes=[pltpu.VMEM((8, D), f32)] + cm.scratch(),
        compiler_params=_cp("arbitrary", side_effects=bool(cm.n)),
    )(x, w, *dus, dx1, *cm.bufs)
    return res[0], res[1], res[2:]


def post_fwd(x, mo, w_post, w_pre2, ts):
    S = x.shape[0]

    def body(x_ref, mo_ref, wp_ref, w2_ref, x1_ref, h_ref):
        wp, w2 = wp_ref[...], w2_ref[...]

        def grp(r0, _):
            rows = pl.ds(r0, NRG)
            mv = mo_ref[rows, :]
            x1 = x_ref[rows, :] + mv * _rms(mv) * wp
            x1_ref[rows, :] = x1
            h_ref[rows, :] = (x1 * _rms(x1) * w2).astype(bf16)

        _groups(ts, grp, rg=NRG)

    row = pl.BlockSpec((ts, D), lambda i: (i, 0))
    par = pl.BlockSpec((1, D), lambda i: (0, 0))
    return pl.pallas_call(
        body, name="post_fwd", grid=(S // ts,),
        in_specs=[row, row, par, par], out_specs=[row, row],
        out_shape=[jax.ShapeDtypeStruct((S, D), f32), jax.ShapeDtypeStruct((S, D), bf16)],
        compiler_params=_cp("parallel"),
    )(x, mo, w_post, w_pre2)


def post_bwd(dout, dh, x1, mo, w_post, w_pre2, ts):
    S = x1.shape[0]
    nt = S // ts

    def body(dout_ref, dh_ref, x1_ref, mo_ref, wp_ref, w2_ref, dx1_ref, dmo_ref, gw_ref, acc2_ref, accp_ref):
        i = pl.program_id(0)
        wp, w2 = wp_ref[...], w2_ref[...]

        @pl.when(i == 0)
        def _():
            acc2_ref[...] = jnp.zeros_like(acc2_ref)
            accp_ref[...] = jnp.zeros_like(accp_ref)
            gw_ref[...] = jnp.zeros_like(gw_ref)

        def grp(r0, _):
            rows = pl.ds(r0, NRG)
            x1 = x1_ref[rows, :]
            r1 = _rms(x1)
            dh = dh_ref[rows, :].astype(f32)
            dx1 = dout_ref[rows, :] + _rms_bwd(x1, r1, dh * w2)
            dx1_ref[rows, :] = dx1
            acc2_ref[...] += _fold(dh * x1 * r1)
            mv = mo_ref[rows, :]
            rm = _rms(mv)
            dmo_ref[rows, :] = _rms_bwd(mv, rm, dx1 * wp).astype(bf16)
            accp_ref[...] += _fold(dx1 * mv * rm)

        _groups(ts, grp, rg=NRG)

        @pl.when(i == nt - 1)
        def _():
            _flush(acc2_ref, gw_ref, 0)
            _flush(accp_ref, gw_ref, 1)

    row = pl.BlockSpec((ts, D), lambda i: (i, 0))
    par = pl.BlockSpec((1, D), lambda i: (0, 0))
    return pl.pallas_call(
        body, name="post_bwd", grid=(nt,),
        in_specs=[row, row, row, row, par, par],
        out_specs=[row, row, pl.BlockSpec((8, D), lambda i: (0, 0))],
        out_shape=[jax.ShapeDtypeStruct((S, D), f32), jax.ShapeDtypeStruct((S, D), bf16),
                   jax.ShapeDtypeStruct((8, D), f32)],
        scratch_shapes=[pltpu.VMEM((8, D), f32), pltpu.VMEM((8, D), f32)],
        compiler_params=_cp("arbitrary"),
    )(dout, dh, x1, mo, w_post, w_pre2)


def loss_head(x1, ff, target, w, ts):
    S = x1.shape[0]
    nt = S // ts

    def body(x1_ref, ff_ref, t_ref, w_ref, loss_ref, dout_ref, dff_ref, gw_ref, accw_ref, accl_ref):
        i = pl.program_id(0)
        wv = w_ref[...]

        @pl.when(i == 0)
        def _():
            accw_ref[...] = jnp.zeros_like(accw_ref)
            accl_ref[...] = jnp.zeros_like(accl_ref)
            gw_ref[...] = jnp.zeros_like(gw_ref)

        def grp(r0, _):
            rows = pl.ds(r0, NRG)
            fv = ff_ref[rows, :]
            r = _rms(fv)
            n = fv * r
            e = x1_ref[rows, :] + n * wv - t_ref[rows, :]
            dout = e * (1.0 / D)
            dout_ref[rows, :] = dout
            dff_ref[rows, :] = _rms_bwd(fv, r, dout * wv).astype(bf16)
            accw_ref[...] += _fold(dout * n)
            accl_ref[...] += _fold(e * e)

        _groups(ts, grp, rg=NRG)

        @pl.when(i == nt - 1)
        def _():
            _flush(accw_ref, gw_ref, 0)
            tot = jnp.sum(jnp.sum(accl_ref[...], axis=1, keepdims=True), axis=0, keepdims=True) * (0.5 / D)
            loss_ref[...] = jnp.broadcast_to(tot, loss_ref.shape)

    row = pl.BlockSpec((ts, D), lambda i: (i, 0))
    return pl.pallas_call(
        body, name="loss_head", grid=(nt,),
        in_specs=[row, row, row, pl.BlockSpec((1, D), lambda i: (0, 0))],
        out_specs=[pl.BlockSpec((8, LANES), lambda i: (0, 0)), row, row, pl.BlockSpec((8, D), lambda i: (0, 0))],
        out_shape=[jax.ShapeDtypeStruct((8, LANES), f32), jax.ShapeDtypeStruct((S, D), f32),
                   jax.ShapeDtypeStruct((S, D), bf16), jax.ShapeDtypeStruct((8, D), f32)],
        scratch_shapes=[pltpu.VMEM((8, D), f32), pltpu.VMEM((8, D), f32)],
        compiler_params=_cp("arbitrary"),
    )(x1, ff, target, w)


def _taps(w_ref, cs, K):
    return [jnp.broadcast_to(w_ref[k:k + 1, cs], (8, CW)) for k in range(K)]


def _down(before, cur, s, sub):
    return jnp.where(sub < s, pltpu.roll(before, s, 0), pltpu.roll(cur, s, 0))


def _up(cur, after, s, sub):
    return jnp.where(sub < 8 - s, pltpu.roll(cur, 8 - s, 0), pltpu.roll(after, 8 - s, 0))


def _conv_group(p, a, b, taps, bias, K, sub):
    ya, yb = bias, bias
    for k in range(K):
        s = K - 1 - k
        xa, xb = (a, b) if s == 0 else (_down(p, a, s, sub), _down(a, b, s, sub))
        ya = ya + taps[k] * xa
        yb = yb + taps[k] * xb
    return ya, yb


def _prev8_map(ts, cb):
    return lambda i, j: (jnp.maximum(i * (ts // 8) - 1, 0), cb + j)


def ssdconv_fwd(proj, w8, b, ts):
    S = proj.shape[0]
    bw = 1024
    cb = C_XBC // bw

    def body(cur_ref, prev_ref, w_ref, b_ref, o_ref, c_ref):
        first = pl.program_id(0) == 0
        sub = lax.broadcasted_iota(jnp.int32, (8, CW), 0)
        for c0 in range(0, bw, CW):
            cs = slice(c0, c0 + CW)
            taps = _taps(w_ref, cs, SSD_K)
            bias = jnp.broadcast_to(b_ref[:, cs], (8, CW))

            def grp(r0, p, cs=cs, taps=taps, bias=bias):
                rows = pl.ds(r0, RG)
                xv = cur_ref[rows, cs].astype(f32)
                ya, yb = _conv_group(p, xv[0:8], xv[8:16], taps, bias, SSD_K, sub)
                y = jnp.concatenate([ya, yb], axis=0)
                c_ref[rows, cs] = y.astype(bf16)
                o_ref[rows, cs] = (y * _sigmoid_fast(y)).astype(bf16)
                return xv[8:16]

            _groups(ts, grp, jnp.where(first, 0.0, prev_ref[:, cs].astype(f32)))

    o = jax.ShapeDtypeStruct((S, CONVD), bf16)
    blk = pl.BlockSpec((ts, bw), lambda i, j: (i, j))
    return pl.pallas_call(
        body, name="ssdconv_fwd", grid=(S // ts, CONVD // bw),
        in_specs=[pl.BlockSpec((ts, bw), lambda i, j: (i, cb + j)),
                  pl.BlockSpec((8, bw), _prev8_map(ts, cb)),
                  pl.BlockSpec((8, bw), lambda i, j: (0, j)),
                  pl.BlockSpec((1, bw), lambda i, j: (0, j))],
        out_specs=[blk, blk], out_shape=[o, o],
        compiler_params=_cp("parallel", "parallel"),
    )(proj, proj, w8, b)


def _gelu_tanh(x):
    c = 0.7978845608028654
    t = jnp.tanh(c * (x + 0.044715 * x * x * x))
    return 0.5 * x * (1.0 + t), t


def ffnact_fwd(up, w8, b, ts):
    S = up.shape[0]

    def body(g_ref, gp_ref, v_ref, vp_ref, wg_ref, wv_ref, bg_ref, bv_ref, o_ref, gc_ref, vc_ref):
        first = pl.program_id(0) == 0
        sub = lax.broadcasted_iota(jnp.int32, (8, CW), 0)
        for c0 in range(0, FF, CW):
            cs = slice(c0, c0 + CW)
            tg, tv = _taps(wg_ref, cs, FFN_K), _taps(wv_ref, cs, FFN_K)
            bg = jnp.broadcast_to(bg_ref[:, cs], (8, CW))
            bv = jnp.broadcast_to(bv_ref[:, cs], (8, CW))

            def grp(r0, carry, cs=cs, tg=tg, tv=tv, bg=bg, bv=bv):
                pg, pv = carry
                rows = pl.ds(r0, RG)
                gx = g_ref[rows, cs].astype(f32)
                vx = v_ref[rows, cs].astype(f32)
                g = jnp.concatenate(_conv_group(pg, gx[0:8], gx[8:16], tg, bg, FFN_K, sub), axis=0)
                v = jnp.concatenate(_conv_group(pv, vx[0:8], vx[8:16], tv, bv, FFN_K, sub), axis=0)
                gc_ref[rows, cs] = g.astype(bf16)
                vc_ref[rows, cs] = v.astype(bf16)
                o_ref[rows, cs] = (_gelu_tanh(g)[0] * v).astype(bf16)
                return gx[8:16], vx[8:16]

            _groups(ts, grp, (jnp.where(first, 0.0, gp_ref[:, cs].astype(f32)),
                              jnp.where(first, 0.0, vp_ref[:, cs].astype(f32))))

    o = jax.ShapeDtypeStruct((S, FF), bf16)
    blk = pl.BlockSpec((ts, FF), lambda i: (i, 0))
    prev = lambda cb: pl.BlockSpec((8, FF), lambda i: (jnp.maximum(i * (ts // 8) - 1, 0), cb))
    return pl.pallas_call(
        body, name="ffnact_fwd", grid=(S // ts,),
        in_specs=[blk, prev(0), pl.BlockSpec((ts, FF), lambda i: (i, 1)), prev(1),
                  pl.BlockSpec((8, FF), lambda i: (0, 0)), pl.BlockSpec((8, FF), lambda i: (0, 1)),
                  pl.BlockSpec((1, FF), lambda i: (0, 0)), pl.BlockSpec((1, FF), lambda i: (0, 1))],
        out_specs=[blk, blk, blk], out_shape=[o, o, o],
        compiler_params=_cp("parallel"),
    )(up, up, up, up, w8, w8, b, b)


def ffnact_bwd(dact, gc, vc, ts):
    S = dact.shape[0]

    def body(d_ref, g_ref, v_ref, dg_ref, dv_ref):
        c = 0.7978845608028654
        for c0 in range(0, FF, CW):
            cs = slice(c0, c0 + CW)

            def grp(r0, _, cs=cs):
                rows = pl.ds(r0, RG)
                d = d_ref[rows, cs].astype(f32)
                g = g_ref[rows, cs].astype(f32)
                ge, t = _gelu_tanh(g)
                dgelu = 0.5 * (1.0 + t) + 0.5 * g * (1.0 - t * t) * c * (1.0 + 3.0 * 0.044715 * g * g)
                dg_ref[rows, cs] = (d * v_ref[rows, cs].astype(f32) * dgelu).astype(bf16)
                dv_ref[rows, cs] = (d * ge).astype(bf16)

            _groups(ts, grp)

    o = jax.ShapeDtypeStruct((S, FF), bf16)
    blk = pl.BlockSpec((ts, FF), lambda i: (i, 0))
    return pl.pallas_call(
        body, name="ffnact_bwd", grid=(S // ts,),
        in_specs=[blk, blk, blk], out_specs=[blk, blk], out_shape=[o, o],
        compiler_params=_cp("parallel"),
    )(dact, gc, vc)


def dwconv_bwd(dy, x, xcb, w8, wcb, K, bw, ts, name, act_c=None, into=None, ocb=0, out_cols=None):
    S, C = dy.shape
    nr = S // ts
    out_cols = out_cols or C
    n_act = 0 if act_c is None else 2

    def body(*refs):
        dy_ref, dyn_ref = refs[0:2]
        c_ref, cn_ref = (refs[2:4] if n_act else (None, None))
        x_ref, xp_ref, w_ref = refs[2 + n_act:5 + n_act]
        dx_ref, dw_ref, sd_ref = refs[-3:]
        i = pl.program_id(1)
        first, last = i == 0, i == nr - 1
        sub = lax.broadcasted_iota(jnp.int32, (8, CW), 0)

        def grad_y(d, c):
            if c is None:
                return d.astype(f32)
            cv = c.astype(f32)
            s = _sigmoid_fast(cv)
            return d.astype(f32) * s * (1.0 + cv * (1.0 - s))

        @pl.when(first)
        def _():
            dw_ref[...] = jnp.zeros_like(dw_ref)

        for c0 in range(0, bw, CW):
            cs = slice(c0, c0 + CW)
            taps = _taps(w_ref, cs, K)
            zero = jnp.zeros((8, CW), f32)

            def fwd(r0, carry, cs=cs):
                p, accs, accb = carry
                rows = pl.ds(r0, RG)
                g = grad_y(dy_ref[rows, cs], c_ref[rows, cs] if n_act else None)
                sd_ref[rows, cs] = g
                xv = x_ref[rows, cs].astype(f32)
                a, b = xv[0:8], xv[8:16]
                ga, gb = g[0:8], g[8:16]
                new = []
                for k in range(K):
                    s = K - 1 - k
                    xa, xb = (a, b) if s == 0 else (_down(p, a, s, sub), _down(a, b, s, sub))
                    new.append(accs[k] + ga * xa + gb * xb)
                return b, tuple(new), accb + ga + gb

            _, accs, accb = _groups(ts, fwd, (jnp.where(first, 0.0, xp_ref[:, cs].astype(f32)), (zero,) * K, zero))
            for k in range(K):
                dw_ref[k:k + 1, cs] += jnp.sum(accs[k], axis=0, keepdims=True)
            dw_ref[7:8, cs] += jnp.sum(accb, axis=0, keepdims=True)

            def bwd(r0, after, cs=cs, taps=taps):
                rows = pl.ds(r0, RG)
                g = sd_ref[rows, cs]
                a, b = g[0:8], g[8:16]
                da, db = zero, zero
                for k in range(K):
                    s = K - 1 - k
                    ua, ub = (a, b) if s == 0 else (_up(a, b, s, sub), _up(b, after, s, sub))
                    da = da + taps[k] * ua
                    db = db + taps[k] * ub
                dx_ref[rows, cs] = jnp.concatenate([da, db], axis=0).astype(bf16)
                return a

            halo = grad_y(dyn_ref[:, cs], cn_ref[:, cs] if n_act else None)
            _groups(ts, bwd, jnp.where(last, 0.0, halo), reverse=True)

    nxt = lambda j, i: (jnp.minimum((i + 1) * (ts // 8), S // 8 - 1), j)
    tile = pl.BlockSpec((ts, bw), lambda j, i: (i, j))
    acts = [] if act_c is None else [act_c, act_c]
    extra = [] if into is None else [into]
    n_in = 5 + n_act
    return pl.pallas_call(
        body, name=name, grid=(C // bw, nr),
        in_specs=[tile, pl.BlockSpec((8, bw), nxt)] + ([tile, pl.BlockSpec((8, bw), nxt)] if n_act else []) + [
            pl.BlockSpec((ts, bw), lambda j, i: (i, xcb + j)),
            pl.BlockSpec((8, bw), lambda j, i: (jnp.maximum(i * (ts // 8) - 1, 0), xcb + j)),
            pl.BlockSpec((8, bw), lambda j, i: (0, wcb + j))] + [pl.BlockSpec(memory_space=pl.ANY)] * len(extra),
        out_specs=[pl.BlockSpec((ts, bw), lambda j, i: (i, ocb + j)), pl.BlockSpec((8, bw), lambda j, i: (0, j))],
        out_shape=[jax.ShapeDtypeStruct((S, out_cols), bf16), jax.ShapeDtypeStruct((8, C), f32)],
        scratch_shapes=[pltpu.VMEM((ts, bw), f32)],
        input_output_aliases={n_in: 0} if extra else {},
        compiler_params=_cp("parallel", "arbitrary"),
    )(dy, dy, *acts, x, x, w8, *extra)


def gnorm_fwd(y, proj, w, ts):
    S = y.shape[0]

    def body(y_ref, z_ref, w_ref, o_ref):
        for k in range(NG):
            sl = slice(k * GW, (k + 1) * GW)
            wv = w_ref[:, sl]

            def grp(r0, _, sl=sl, wv=wv):
                rows = pl.ds(r0, NRG)
                z = z_ref[rows, sl].astype(f32)
                g = y_ref[rows, sl].astype(f32) * z * _sigmoid_fast(z)
                o_ref[rows, sl] = (g * _rms(g) * wv).astype(bf16)

            _groups(ts, grp, rg=NRG)

    row = pl.BlockSpec((ts, DI), lambda i: (i, 0))
    return pl.pallas_call(
        body, name="gnorm_fwd", grid=(S // ts,),
        in_specs=[row, row, pl.BlockSpec((1, DI), lambda i: (0, 0))],
        out_specs=row, out_shape=jax.ShapeDtypeStruct((S, DI), bf16),
        compiler_params=_cp("parallel"),
    )(y, proj, w)


def gnorm_bwd(dyn, y, proj, w, dproj, ts):
    S = y.shape[0]
    nt = S // ts

    def body(d_ref, y_ref, z_ref, w_ref, _, dy_ref, dz_ref, gw_ref, acc_ref):
        i = pl.program_id(0)

        @pl.when(i == 0)
        def _():
            acc_ref[...] = jnp.zeros_like(acc_ref)
            gw_ref[...] = jnp.zeros_like(gw_ref)

        for k in range(NG):
            sl = slice(k * GW, (k + 1) * GW)
            wv = w_ref[:, sl]

            def grp(r0, _, sl=sl, wv=wv):
                rows = pl.ds(r0, NRG)
                z = z_ref[rows, sl].astype(f32)
                yv = y_ref[rows, sl].astype(f32)
                s = _sigmoid_fast(z)
                sz = z * s
                g = yv * sz
                r = _rms(g)
                d = d_ref[rows, sl].astype(f32)
                acc_ref[:, sl] += _fold(d * g * r)
                dg = _rms_bwd(g, r, d * wv)
                dy_ref[rows, sl] = (dg * sz).astype(bf16)
                dz_ref[rows, sl] = (dg * yv * s * (1.0 + z * (1.0 - s))).astype(bf16)

            _groups(ts, grp, rg=NRG)

        @pl.when(i == nt - 1)
        def _():
            _flush(acc_ref, gw_ref, 0)

    row = pl.BlockSpec((ts, DI), lambda i: (i, 0))
    return pl.pallas_call(
        body, name="gnorm_bwd", grid=(nt,),
        in_specs=[row, row, row, pl.BlockSpec((1, DI), lambda i: (0, 0)), pl.BlockSpec(memory_space=pl.ANY)],
        out_specs=[row, row, pl.BlockSpec((8, DI), lambda i: (0, 0))],
        out_shape=[jax.ShapeDtypeStruct((S, DI), bf16), jax.ShapeDtypeStruct(dproj.shape, bf16),
                   jax.ShapeDtypeStruct((8, DI), f32)],
        scratch_shapes=[pltpu.VMEM((8, DI), f32)],
        input_output_aliases={4: 1},
        compiler_params=_cp("arbitrary"),
    )(dyn, y, proj, w, dproj)


def merge_fwd(proj, ys, ya, ts):
    S = ys.shape[0]

    def body(gs_ref, ga_ref, ys_ref, ya_ref, o_ref):
        for c0 in range(0, D, CW):
            cs = slice(c0, c0 + CW)

            def grp(r0, _, cs=cs):
                rows = pl.ds(r0, NRG)
                o_ref[rows, cs] = (_sigmoid_fast(gs_ref[rows, cs].astype(f32)) * ys_ref[rows, cs].astype(f32)
                                   + _sigmoid_fast(ga_ref[rows, cs].astype(f32)) * ya_ref[rows, cs].astype(f32)
                                   ).astype(bf16)

            _groups(ts, grp, rg=NRG)

    row = pl.BlockSpec((ts, D), lambda i: (i, 0))
    return pl.pallas_call(
        body, name="merge_fwd", grid=(S // ts,),
        in_specs=[pl.BlockSpec((ts, D), lambda i: (i, C_GS // D)), pl.BlockSpec((ts, D), lambda i: (i, C_GA // D)), row, row],
        out_specs=row, out_shape=jax.ShapeDtypeStruct((S, D), bf16),
        compiler_params=_cp("parallel"),
    )(proj, proj, ys, ya)


def merge_bwd(dm, proj, ys, ya, ts):
    S = ys.shape[0]

    def body(d_ref, gs_ref, ga_ref, ys_ref, ya_ref, dys_ref, dya_ref, dg_ref):
        for c0 in range(0, D, CW):
            cs = slice(c0, c0 + CW)

            def grp(r0, _, c0=c0, cs=cs):
                rows = pl.ds(r0, NRG)
                d = d_ref[rows, cs].astype(f32)
                ss = _sigmoid_fast(gs_ref[rows, cs].astype(f32))
                sa = _sigmoid_fast(ga_ref[rows, cs].astype(f32))
                dys_ref[rows, cs] = (d * ss).astype(bf16)
                dya_ref[rows, cs] = (d * sa).astype(bf16)
                dg_ref[rows, cs] = (d * ys_ref[rows, cs].astype(f32) * ss * (1.0 - ss)).astype(bf16)
                dg_ref[rows, D + c0:D + c0 + CW] = (d * ya_ref[rows, cs].astype(f32) * sa * (1.0 - sa)).astype(bf16)

            _groups(ts, grp, rg=NRG)

    row = pl.BlockSpec((ts, D), lambda i: (i, 0))
    o = jax.ShapeDtypeStruct((S, D), bf16)
    return pl.pallas_call(
        body, name="merge_bwd", grid=(S // ts,),
        in_specs=[row, pl.BlockSpec((ts, D), lambda i: (i, C_GS // D)), pl.BlockSpec((ts, D), lambda i: (i, C_GA // D)), row, row],
        out_specs=[row, row, pl.BlockSpec((ts, 2 * D), lambda i: (i, C_GS // (2 * D)))],
        out_shape=[o, o, jax.ShapeDtypeStruct((S, PM), bf16)],
        compiler_params=_cp("parallel"),
    )(dm, proj, proj, ys, ya)


def _ssd_consts():
    h = lax.broadcasted_iota(jnp.int32, (LANES, DI), 0)
    c = lax.broadcasted_iota(jnp.int32, (LANES, DI), 1)
    expand = (c // HD == h).astype(bf16)
    r = lax.broadcasted_iota(jnp.int32, (CH, CH), 0)
    cc = lax.broadcasted_iota(jnp.int32, (CH, CH), 1)
    tril = (cc <= r).astype(bf16)
    triu = (cc >= r).astype(bf16)
    return expand, expand.T, tril, triu


def _ssd_common(xbc_ref, dtr_ref, bias_ref, alog_ref, expand_ref, tril_ref):
    dtr = dtr_ref[...] + bias_ref[...]
    dt = jnp.maximum(dtr, 0.0) + jnp.log1p(jnp.exp(-jnp.abs(dtr)))
    a = -jnp.exp(alog_ref[...])
    acs = _dot3_left(tril_ref[...], dt * a)
    acsx = _dot3_right(acs, expand_ref[...])
    dtx = _dot3_right(dt, expand_ref[...])
    x = xbc_ref[:, 0:DI].astype(f32)
    xdt = x * dtx
    e = jnp.exp(acsx)
    dsx = jnp.exp(acsx[CH - 1:CH, :] - acsx)
    return dtr, dt, a, acs, dtx, x, xdt, e, dsx


def _ssd_lmat(acs, acs_t, hh, causal):
    seg = acs[:, hh:hh + 1] - acs_t[hh:hh + 1, :]
    return jnp.where(causal, jnp.exp(jnp.minimum(seg, 0.0)), 0.0)


def ssd_fwd(xbc, dtr, bias, alog, dx_row, comm=None):
    S = xbc.shape[0]
    nc = S // CH
    expand, _, tril, _ = _ssd_consts()
    cm = _Comm(comm)

    def body(*refs):
        ins, (y_ref, hp_ref), (h_ref, yd_ref), copies = cm.split(refs, 7, 2)
        xbc_ref, dtr_ref, bias_ref, alog_ref, dxr_ref, expand_ref, tril_ref = ins
        c = pl.program_id(0)
        cm.start(copies, c == 0)

        @pl.when(c == 0)
        def _():
            h_ref[...] = jnp.zeros_like(h_ref)

        _, _, _, acs, _, x, xdt, e, dsx = _ssd_common(xbc_ref, dtr_ref, bias_ref, alog_ref, expand_ref, tril_ref)
        acs_t = acs.T
        xb = xdt.astype(bf16)
        xd = (xdt * dsx).astype(bf16)
        causal = tril_ref[...] > 0
        for g in range(NG):
            gs = slice(g * GW, (g + 1) * GW)
            bg = xbc_ref[:, DI + g * NS:DI + (g + 1) * NS]
            cg = xbc_ref[:, DI + NG * NS + g * NS:DI + NG * NS + (g + 1) * NS]
            cb = _dot(cg, bg, "nt")
            hp = h_ref[g]
            hpb = hp.astype(bf16)
            hp_ref[0, g] = hpb
            yd_ref[:, gs] = _dot(cg, hpb) * e[:, gs]
            h_ref[g] = hp * e[CH - 1:CH, gs] + _dot(bg, xd[:, gs], "tn")
            for j in range(NH // NG):
                hh = g * (NH // NG) + j
                hs = slice(hh * HD, (hh + 1) * HD)
                m = (cb * _ssd_lmat(acs, acs_t, hh, causal)).astype(bf16)
                yd_ref[:, hs] += _dot(m, xb[:, hs])
        y_ref[...] = (yd_ref[...] + dxr_ref[...] * x).astype(bf16)
        cm.wait(copies, c == nc - 1)

    par = lambda shape: pl.BlockSpec(shape, lambda c: (0,) * len(shape))
    res = pl.pallas_call(
        body, name="ssd_fwd", grid=(nc,),
        in_specs=[pl.BlockSpec((CH, CONVD), lambda c: (c, 0)), pl.BlockSpec((CH, LANES), lambda c: (c, 0)),
                  par((1, LANES)), par((1, LANES)), par((1, DI)), par((LANES, DI)), par((CH, CH))] + cm.in_specs(),
        out_specs=[pl.BlockSpec((CH, DI), lambda c: (c, 0)),
                   pl.BlockSpec((1, NG, NS, GW), lambda c: (c, 0, 0, 0))] + cm.out_specs(),
        out_shape=[jax.ShapeDtypeStruct((S, DI), bf16), jax.ShapeDtypeStruct((nc, NG, NS, GW), bf16)] + cm.out_shape(),
        scratch_shapes=[pltpu.VMEM((NG, NS, GW), f32), pltpu.VMEM((CH, DI), f32)] + cm.scratch(),
        compiler_params=_cp("arbitrary", side_effects=bool(cm.n)),
    )(xbc, dtr, bias, alog, dx_row, expand, tril, *cm.bufs)
    return res[0], res[1], res[2:]


def ssd_bwd(xbc, dtr, dy, hprev, bias, alog, dx_row, comm=None):
    S = xbc.shape[0]
    nc = S // CH
    expand, expand_t, tril, triu = _ssd_consts()
    cm = _Comm(comm)

    def body(*refs):
        ins, outs, scr, copies = cm.split(refs, 11, 3)
        xbc_ref, dtr_ref, dy_ref, hp_ref, bias_ref, alog_ref, dxr_ref, expand_ref, expt_ref, tril_ref, triu_ref = ins
        dxbc_ref, ddtr_ref, acc_ref = outs
        dh_ref, dxs_ref, t_ref, accb_ref, acca_ref, accd_ref = scr
        c = pl.program_id(0)
        cm.start(copies, c == 0)

        @pl.when(c == 0)
        def _():
            dh_ref[...] = jnp.zeros_like(dh_ref)
            accb_ref[...] = jnp.zeros_like(accb_ref)
            acca_ref[...] = jnp.zeros_like(acca_ref)
            accd_ref[...] = jnp.zeros_like(accd_ref)

        dtr, dt, a, acs, dtx, x, xdt, e, dsx = _ssd_common(xbc_ref, dtr_ref, bias_ref, alog_ref, expand_ref, tril_ref)
        acs_t = acs.T
        xb = xdt.astype(bf16)
        xdf = xdt * dsx
        xd = xdf.astype(bf16)
        dyv = dy_ref[...].astype(f32)
        dyb = dy_ref[...]
        dye = (dyv * e).astype(bf16)
        causal = tril_ref[...] > 0
        lane = lax.broadcasted_iota(jnp.int32, (CH, LANES), 1)
        subl = lax.broadcasted_iota(jnp.int32, (LANES, CH), 0)
        ccol = jnp.zeros((CH, LANES), f32)
        rrow = jnp.zeros((LANES, CH), f32)
        last_row = lax.broadcasted_iota(jnp.int32, (CH, 1), 0) == CH - 1
        for g in range(NG):
            gs = slice(g * GW, (g + 1) * GW)
            bsl = slice(DI + g * NS, DI + (g + 1) * NS)
            csl = slice(DI + NG * NS + g * NS, DI + NG * NS + (g + 1) * NS)
            bg = xbc_ref[:, bsl]
            cg = xbc_ref[:, csl]
            cb = _dot(cg, bg, "nt")
            hpb = hp_ref[0, g]
            dhn = dh_ref[g]
            dhnb = dhn.astype(bf16)
            yoff = _dot(cg, hpb) * e[:, gs]
            dxd = _dot(bg, dhnb)
            t2 = dxd * xdf[:, gs]
            t3 = jnp.sum(dhn * hpb.astype(f32), axis=0, keepdims=True) * e[CH - 1:CH, gs]
            t_ref[:, gs] = dyv[:, gs] * yoff - t2 + jnp.where(last_row, jnp.sum(t2, axis=0, keepdims=True) + t3, 0.0)
            dxs_ref[:, gs] = dxd * dsx[:, gs]
            dcg = _dot(dye[:, gs], hpb, "nt")
            dbg = _dot(xd[:, gs], dhnb, "nt")
            dh_ref[g] = dhn * e[CH - 1:CH, gs] + _dot(cg, dye[:, gs], "tn")
            dcb = jnp.zeros((CH, CH), f32)
            for j in range(NH // NG):
                hh = g * (NH // NG) + j
                hs = slice(hh * HD, (hh + 1) * HD)
                lm = _ssd_lmat(acs, acs_t, hh, causal)
                m = cb * lm
                dm = _dot(dyb[:, hs], xb[:, hs], "nt")
                gm = dm * m
                ccol = ccol + jnp.sum(gm, axis=1, keepdims=True) * (lane == hh).astype(f32)
                rrow = rrow + jnp.sum(gm, axis=0, keepdims=True) * (subl == hh).astype(f32)
                dcb = dcb + dm * lm
                dxs_ref[:, hs] += _dot(m.astype(bf16), dyb[:, hs], "tn")
            dcbb = dcb.astype(bf16)
            dxbc_ref[:, csl] = (dcg + _dot(dcbb, bg)).astype(bf16)
            dxbc_ref[:, bsl] = (dbg + _dot(dcbb, cg, "tn")).astype(bf16)
        dxf = dxs_ref[...]
        dxbc_ref[:, 0:DI] = (dxf * dtx + dxr_ref[...] * dyv).astype(bf16)
        expt = expt_ref[...]
        dacs = ccol - rrow.T + _dot2_right(t_ref[...], expt)
        dadt = _dot3_left(triu_ref[...], dacs)
        ddt = _dot2_right(dxf * x, expt) + dadt * a
        ddtr = ddt * _sigmoid(dtr)
        ddtr_ref[...] = ddtr
        accb_ref[...] += ddtr
        acca_ref[...] += dadt * dt
        accd_ref[...] += _dot2_right(dyv * x, expt)

        @pl.when(c == nc - 1)
        def _():
            acc_ref[...] = jnp.zeros_like(acc_ref)
            acc_ref[0:1, :] = jnp.sum(accb_ref[...], axis=0, keepdims=True)
            acc_ref[1:2, :] = jnp.sum(acca_ref[...], axis=0, keepdims=True) * a
            acc_ref[2:3, :] = jnp.sum(accd_ref[...], axis=0, keepdims=True)

        cm.wait(copies, c == nc - 1)

    par = lambda shape: pl.BlockSpec(shape, lambda c: (0,) * len(shape))
    rev = lambda c: (nc - 1 - c, 0)
    res = pl.pallas_call(
        body, name="ssd_bwd", grid=(nc,),
        in_specs=[pl.BlockSpec((CH, CONVD), rev), pl.BlockSpec((CH, LANES), rev), pl.BlockSpec((CH, DI), rev),
                  pl.BlockSpec((1, NG, NS, GW), lambda c: (nc - 1 - c, 0, 0, 0)),
                  par((1, LANES)), par((1, LANES)), par((1, DI)), par((LANES, DI)), par((DI, LANES)),
                  par((CH, CH)), par((CH, CH))] + cm.in_specs(),
        out_specs=[pl.BlockSpec((CH, CONVD), rev), pl.BlockSpec((CH, LANES), rev), par((8, LANES))] + cm.out_specs(),
        out_shape=[jax.ShapeDtypeStruct((S, CONVD), bf16), jax.ShapeDtypeStruct((S, LANES), f32),
                   jax.ShapeDtypeStruct((8, LANES), f32)] + cm.out_shape(),
        scratch_shapes=[pltpu.VMEM((NG, NS, GW), f32), pltpu.VMEM((CH, DI), f32), pltpu.VMEM((CH, DI), f32),
                        pltpu.VMEM((CH, LANES), f32), pltpu.VMEM((CH, LANES), f32),
                        pltpu.VMEM((CH, LANES), f32)] + cm.scratch(),
        compiler_params=_cp("arbitrary", side_effects=bool(cm.n)),
    )(xbc, dtr, dy, hprev, bias, alog, dx_row, expand, expand_t, tril, triu, *cm.bufs)
    return res[0], res[1], res[2], res[3:]


def rope_tables(pos_row, ts):
    S = pos_row.shape[1]
    half = AD // 2
    inv = ROPE_THETA ** (-jnp.arange(half, dtype=f32) * 2.0 / AD)
    inv_col = jnp.tile(inv, 2)[:, None]

    def body(p_ref, inv_ref, cos_ref, sin_ref):
        ang = inv_ref[...] * p_ref[...].astype(f32)
        row = lax.broadcasted_iota(jnp.int32, ang.shape, 0)
        cos_ref[...] = jnp.cos(ang)
        sin_ref[...] = jnp.where(row < half, -1.0, 1.0) * jnp.sin(ang)

    o = jax.ShapeDtypeStruct((AD, S), f32)
    return pl.pallas_call(
        body, name="rope_tables", grid=(S // ts,),
        in_specs=[pl.BlockSpec((1, ts), lambda i: (0, i)), pl.BlockSpec((AD, 1), lambda i: (0, 0))],
        out_specs=[pl.BlockSpec((AD, ts), lambda i: (0, i))] * 2, out_shape=[o, o],
        compiler_params=_cp("parallel"),
    )(pos_row, inv_col)


def _partner(t):
    half = AD // 2
    return jnp.concatenate([t[h * AD + o:h * AD + o + half] for h in range(t.shape[0] // AD) for o in (half, 0)], axis=0)


def _rope(t, cos, sin):
    reps = t.shape[0] // AD
    return t * jnp.tile(cos, (reps, 1)) + _partner(t) * jnp.tile(sin, (reps, 1))


def _rope_t(d, cos, sin):
    reps = d.shape[0] // AD
    return d * jnp.tile(cos, (reps, 1)) - _partner(d) * jnp.tile(sin, (reps, 1))


def _lanes_of_group(t, g):
    return jnp.concatenate([t[(g * REP + r) * AD:(g * REP + r + 1) * AD] for r in range(REP)], axis=1)


def _attn_probs(qg, kp, kc, sink_ref, g, not_first):
    n = qg.shape[1]
    s = lax.broadcasted_iota(jnp.int32, (WIN, n), 0)
    t = lax.broadcasted_iota(jnp.int32, (WIN, n), 1) % WIN
    neg = -1e30
    sink = jnp.concatenate([jnp.broadcast_to(sink_ref[0:1, g * REP + r:g * REP + r + 1], (1, WIN)) for r in range(REP)],
                           axis=1)
    sp = jnp.where(jnp.logical_and(s > t, not_first), _dot(kp, qg, "tn"), neg)
    sc = jnp.where(s <= t, _dot(kc, qg, "tn"), neg)
    m = jnp.maximum(jnp.maximum(jnp.max(sp, axis=0, keepdims=True), jnp.max(sc, axis=0, keepdims=True)), sink)
    pp = jnp.exp(sp - m)
    pc = jnp.exp(sc - m)
    ps = jnp.exp(sink - m)
    inv = 1.0 / (jnp.sum(pp, axis=0, keepdims=True) + jnp.sum(pc, axis=0, keepdims=True) + ps)
    return pp * inv, pc * inv, ps * inv


def attn_fwd(qt, kvt, cos, sin, sinks):
    S = qt.shape[1]
    nb = S // WIN
    cur = lambda i: (0, i)
    prev = lambda i: (0, jnp.maximum(i - 1, 0))

    def body(q_ref, kv_ref, kvp_ref, cos_ref, sin_ref, cosp_ref, sinp_ref, sink_ref, o_ref):
        i = pl.program_id(0)
        q = (_rope(q_ref[...].astype(f32), cos_ref[...], sin_ref[...]) * (AD ** -0.5)).astype(bf16)
        kc = _rope(kv_ref[0:KVW, :].astype(f32), cos_ref[...], sin_ref[...]).astype(bf16)
        kp = _rope(kvp_ref[0:KVW, :].astype(f32), cosp_ref[...], sinp_ref[...]).astype(bf16)
        for g in range(KVH):
            ks = slice(g * AD, (g + 1) * AD)
            vs = slice(KVW + g * AD, KVW + (g + 1) * AD)
            pp, pc, _ = _attn_probs(_lanes_of_group(q, g), kp[ks], kc[ks], sink_ref, g, i > 0)
            o = _dot(kvp_ref[vs, :], pp.astype(bf16)) + _dot(kv_ref[vs, :], pc.astype(bf16))
            for r in range(REP):
                h = g * REP + r
                o_ref[h * AD:(h + 1) * AD, :] = o[:, r * WIN:(r + 1) * WIN].astype(bf16)

    tab = pl.BlockSpec((AD, WIN), cur)
    tabp = pl.BlockSpec((AD, WIN), prev)
    return pl.pallas_call(
        body, name="attn_fwd", grid=(nb,),
        in_specs=[pl.BlockSpec((D, WIN), cur), pl.BlockSpec((2 * KVW, WIN), cur), pl.BlockSpec((2 * KVW, WIN), prev),
                  tab, tab, tabp, tabp, pl.BlockSpec((1, LANES), lambda i: (0, 0))],
        out_specs=pl.BlockSpec((D, WIN), cur),
        out_shape=jax.ShapeDtypeStruct((D, S), bf16),
        compiler_params=_cp("parallel"),
    )(qt, kvt, kvt, cos, sin, cos, sin, sinks)


def attn_bwd(qt, kvt, cos, sin, sinks, daot, comm=None):
    S = qt.shape[1]
    nb = S // WIN
    cur = lambda i: (0, jnp.minimum(i, nb - 1))
    prev = lambda i: (0, jnp.maximum(i - 1, 0))
    cm = _Comm(comm)

    def body(*refs):
        ins, (dq_ref, dkv_ref, ds_ref), scr, copies = cm.split(refs, 9, 3)
        q_ref, kv_ref, kvp_ref, cos_ref, sin_ref, cosp_ref, sinp_ref, sink_ref, do_ref = ins
        ck_ref, cv_ref, dqs_ref, dkp_ref, dvp_ref, dkc_ref, dvc_ref, accs_ref = scr
        i = pl.program_id(0)
        cm.start(copies, i == 0)

        @pl.when(i == 0)
        def _():
            ck_ref[...] = jnp.zeros_like(ck_ref)
            cv_ref[...] = jnp.zeros_like(cv_ref)
            accs_ref[...] = jnp.zeros_like(accs_ref)

        @pl.when(i == nb)
        def _():
            dkp_ref[...] = jnp.zeros_like(dkp_ref)
            dvp_ref[...] = jnp.zeros_like(dvp_ref)

        @pl.when(i < nb)
        def _():
            q = (_rope(q_ref[...].astype(f32), cos_ref[...], sin_ref[...]) * (AD ** -0.5)).astype(bf16)
            kc = _rope(kv_ref[0:KVW, :].astype(f32), cos_ref[...], sin_ref[...]).astype(bf16)
            kp = _rope(kvp_ref[0:KVW, :].astype(f32), cosp_ref[...], sinp_ref[...]).astype(bf16)
            do = do_ref[...]
            for g in range(KVH):
                ks = slice(g * AD, (g + 1) * AD)
                vs = slice(KVW + g * AD, KVW + (g + 1) * AD)
                qg = _lanes_of_group(q, g)
                dog = _lanes_of_group(do, g)
                pp, pc, ps = _attn_probs(qg, kp[ks], kc[ks], sink_ref, g, i > 0)
                dpp = _dot(kvp_ref[vs, :], dog, "tn")
                dpc = _dot(kv_ref[vs, :], dog, "tn")
                delta = jnp.sum(pp * dpp + pc * dpc, axis=0, keepdims=True)
                dsp = (pp * (dpp - delta)).astype(bf16)
                dsc = (pc * (dpc - delta)).astype(bf16)
                accs_ref[g:g + 1, :] -= ps * delta
                dqg = (_dot(kp[ks], dsp) + _dot(kc[ks], dsc)) * (AD ** -0.5)
                for r in range(REP):
                    h = g * REP + r
                    dqs_ref[h * AD:(h + 1) * AD, :] = dqg[:, r * WIN:(r + 1) * WIN]
                dkp_ref[ks, :] = _dot(qg, dsp, "nt")
                dkc_ref[ks, :] = _dot(qg, dsc, "nt")
                dvp_ref[ks, :] = _dot(dog, pp.astype(bf16), "nt")
                dvc_ref[ks, :] = _dot(dog, pc.astype(bf16), "nt")
            dq_ref[...] = _rope_t(dqs_ref[...], cos_ref[...], sin_ref[...]).astype(bf16)

        dkv_ref[0:KVW, :] = _rope_t(ck_ref[...] + dkp_ref[...], cosp_ref[...], sinp_ref[...]).astype(bf16)
        dkv_ref[KVW:2 * KVW, :] = (cv_ref[...] + dvp_ref[...]).astype(bf16)

        @pl.when(i < nb)
        def _():
            ck_ref[...] = dkc_ref[...]
            cv_ref[...] = dvc_ref[...]

        @pl.when(i == nb)
        def _():
            lane = lax.broadcasted_iota(jnp.int32, (1, LANES), 1)
            row = jnp.zeros((1, LANES), f32)
            for h in range(AH):
                part = accs_ref[h // REP:h // REP + 1, (h % REP) * WIN:(h % REP + 1) * WIN]
                row = row + jnp.where(lane == h, jnp.sum(part, axis=1, keepdims=True), 0.0)
            ds_ref[...] = jnp.zeros_like(ds_ref)
            ds_ref[0:1, :] = row

        cm.wait(copies, i == nb)

    tab = pl.BlockSpec((AD, WIN), cur)
    tabp = pl.BlockSpec((AD, WIN), prev)
    kvs = lambda: pltpu.VMEM((KVW, WIN), f32)
    res = pl.pallas_call(
        body, name="attn_bwd", grid=(nb + 1,),
        in_specs=[pl.BlockSpec((D, WIN), cur), pl.BlockSpec((2 * KVW, WIN), cur), pl.BlockSpec((2 * KVW, WIN), prev),
                  tab, tab, tabp, tabp, pl.BlockSpec((1, LANES), lambda i: (0, 0)),
                  pl.BlockSpec((D, WIN), cur)] + cm.in_specs(),
        out_specs=[pl.BlockSpec((D, WIN), cur), pl.BlockSpec((2 * KVW, WIN), prev),
                   pl.BlockSpec((8, LANES), lambda i: (0, 0))] + cm.out_specs(),
        out_shape=[jax.ShapeDtypeStruct((D, S), bf16), jax.ShapeDtypeStruct((2 * KVW, S), bf16),
                   jax.ShapeDtypeStruct((8, LANES), f32)] + cm.out_shape(),
        scratch_shapes=[kvs(), kvs(), pltpu.VMEM((D, WIN), f32), kvs(), kvs(), kvs(), kvs(),
                        pltpu.VMEM((8, REP * WIN), f32)] + cm.scratch(),
        compiler_params=_cp("arbitrary", side_effects=bool(cm.n)),
    )(qt, kvt, kvt, cos, sin, cos, sin, sinks, daot, *cm.bufs)
    return res[0], res[1], res[2], res[3:]


ADAM_C1 = 1.0 / (1.0 - ADAM_B1 ** ADAM_STEP)
ADAM_C2 = 1.0 / (1.0 - ADAM_B2 ** ADAM_STEP)


def _adam_update(g, w, m, v):
    nm = ADAM_B1 * m + (1.0 - ADAM_B1) * g
    nv = ADAM_B2 * v + (1.0 - ADAM_B2) * (g * g)
    return -ADAM_LR * ((nm * ADAM_C1) / (jnp.sqrt(nv * ADAM_C2) + ADAM_EPS) + ADAM_WD * w), nm, nv


def adamw(parts, w, m, v, tr, name):
    n, R, C = parts.shape

    def body(p_ref, w_ref, m_ref, v_ref, g_ref, d_ref, nm_ref, nv_ref):
        def grp(g0, _):
            r0 = pl.multiple_of(g0 * RG, RG)
            rows = pl.ds(r0, RG)
            g = p_ref[0, rows, :].astype(f32)
            for k in range(1, n):
                g = g + p_ref[k, rows, :].astype(f32)
            d, nm, nv = _adam_update(g, w_ref[rows, :], m_ref[rows, :], v_ref[rows, :])
            g_ref[rows, :] = g
            d_ref[rows, :] = d
            nm_ref[rows, :] = nm
            nv_ref[rows, :] = nv
            return 0

        lax.fori_loop(0, tr // RG, grp, 0)

    row = pl.BlockSpec((tr, C), lambda i: (i, 0))
    o = jax.ShapeDtypeStruct((R, C), f32)
    return pl.pallas_call(
        body, name=name, grid=(R // tr,),
        in_specs=[pl.BlockSpec((n, tr, C), lambda i: (0, i, 0)), row, row, row],
        out_specs=[row, row, row, row], out_shape=[o, o, o, o],
        compiler_params=_cp("parallel"),
    )(parts, w, m, v)


def adamw_transposed(parts, w, m, v, name):
    n, R, C = parts.shape
    cols = w.shape[1]

    def body(p_ref, w_ref, m_ref, v_ref, g_ref, d_ref, nm_ref, nv_ref):
        g = p_ref[0].astype(f32)
        for k in range(1, n):
            g = g + p_ref[k].astype(f32)
        g = g.T
        d, nm, nv = _adam_update(g, w_ref[...], m_ref[...], v_ref[...])
        g_ref[...] = g
        d_ref[...] = d
        nm_ref[...] = nm
        nv_ref[...] = nv

    col = pl.BlockSpec((C, LANES), lambda i: (0, i))
    o = jax.ShapeDtypeStruct((C, cols), f32)
    return pl.pallas_call(
        body, name=name, grid=(R // LANES,),
        in_specs=[pl.BlockSpec((n, LANES, C), lambda i: (0, i, 0)), col, col, col],
        out_specs=[col, col, col, col], out_shape=[o, o, o, o],
        compiler_params=_cp("parallel"),
    )(parts, w, m, v)


SMALL_ROW = (("norm_mix_post_w", D), ("norm_ffn_pre_w", D), ("norm_ffn_post_w", D), ("ssd_norm_w", DI),
             ("ssd_conv_b", CONVD), ("ffn_conv_b", 2 * FF), ("ssd_dt_bias", NH), ("ssd_a_log", NH), ("ssd_d", NH),
             ("attn_sinks", AH), ("loss", 1))
CONV_BLOCK = 1152
SSD_CONV_COLS = CONVD // N_DEV
FFN_CONV_COLS = 2 * FF // N_DEV


def _row_offsets():
    off, o = {}, 0
    for name, n in SMALL_ROW:
        off[name] = (o, n)
        o += -(-n // LANES) * LANES
    return off, o


def adamw_small(recv_row, recv_pre, recv_conv, params):
    off, _ = _row_offsets()
    names = list(params)
    n = len(names)

    def total(ref, rows, lo, width):
        g = ref[0, rows, lo:lo + width]
        for d in range(1, N_DEV):
            g = g + ref[d, rows, lo:lo + width]
        return g

    def grad_of(name, row_ref, pre_ref, conv_ref):
        if name == "norm_mix_pre_w":
            return total(pre_ref, slice(0, 1), 0, D)
        if name == "ssd_conv_w":
            return total(conv_ref, slice(0, SSD_K), 0, SSD_CONV_COLS)
        if name == "ffn_conv_w":
            return total(conv_ref, slice(0, FFN_K), 3 * LANES, FFN_CONV_COLS)
        o, width = off[name]
        return total(row_ref, slice(0, 1), o, width)

    def body(row_ref, pre_ref, conv_ref, *refs):
        ins, outs = refs[:3 * n], refs[3 * n:]
        for k, name in enumerate(names):
            w_ref, m_ref, v_ref = ins[3 * k:3 * k + 3]
            g_ref, d_ref, nm_ref, nv_ref = outs[4 * k:4 * k + 4]
            g = grad_of(name, row_ref, pre_ref, conv_ref)
            d, nm, nv = _adam_update(g, w_ref[...], m_ref[...], v_ref[...])
            g_ref[...] = g
            d_ref[...] = d
            nm_ref[...] = nm
            nv_ref[...] = nv
        outs[4 * n][...] = total(row_ref, slice(0, 1), off["loss"][0], LANES)

    flat = [t for name in names for t in params[name]]
    out_shape = [jax.ShapeDtypeStruct(params[name][0].shape, f32) for name in names for _ in range(4)]
    res = pl.pallas_call(
        body, name="adamw_small",
        out_shape=out_shape + [jax.ShapeDtypeStruct((1, LANES), f32)],
        compiler_params=pltpu.CompilerParams(vmem_limit_bytes=VMEM_LIMIT),
    )(recv_row, recv_pre, recv_conv, *flat)
    return {name: res[4 * k:4 * k + 4] for k, name in enumerate(names)}, res[4 * n]


def _pad_rows8(w):
    return jnp.pad(w, ((0, 8 - w.shape[0]), (0, 0)))


def _pad_lanes(v):
    return jnp.pad(v.reshape(1, -1), ((0, 0), (0, LANES - v.size)))


WEIGHTS = ('norm_mix_pre_w', 'w_in', 'ssd_conv_w', 'ssd_conv_b', 'ssd_dt_bias', 'ssd_a_log', 'ssd_d', 'ssd_norm_w',
           'ssd_w_out', 'attn_sinks', 'attn_w_out', 'w_mix_out', 'norm_mix_post_w', 'norm_ffn_pre_w', 'ffn_w_up',
           'ffn_conv_w', 'ffn_conv_b', 'ffn_w_down', 'norm_ffn_post_w')
W_IN_ROWS = IN_DIM // N_DEV
W_IN_PAD = 1152
W_UP_PAD = 768
TS = 256


def kernel(x, positions, norm_mix_pre_w, w_in, ssd_conv_w, ssd_conv_b, ssd_dt_bias, ssd_a_log, ssd_d, ssd_norm_w, ssd_w_out, attn_sinks, attn_w_out, w_mix_out, norm_mix_post_w, norm_ffn_pre_w, ffn_w_up, ffn_conv_w, ffn_conv_b, ffn_w_down, norm_ffn_post_w, loss_target, m_norm_mix_pre_w, m_w_in, m_ssd_conv_w, m_ssd_conv_b, m_ssd_dt_bias, m_ssd_a_log, m_ssd_d, m_ssd_norm_w, m_ssd_w_out, m_attn_sinks, m_attn_w_out, m_w_mix_out, m_norm_mix_post_w, m_norm_ffn_pre_w, m_ffn_w_up, m_ffn_conv_w, m_ffn_conv_b, m_ffn_w_down, m_norm_ffn_post_w, v_norm_mix_pre_w, v_w_in, v_ssd_conv_w, v_ssd_conv_b, v_ssd_dt_bias, v_ssd_a_log, v_ssd_d, v_ssd_norm_w, v_ssd_w_out, v_attn_sinks, v_attn_w_out, v_w_mix_out, v_norm_mix_post_w, v_norm_ffn_pre_w, v_ffn_w_up, v_ffn_conv_w, v_ffn_conv_b, v_ffn_w_down, v_norm_ffn_post_w):
    a = locals()
    r2 = lambda t: t.reshape(t.shape[-2], t.shape[-1])
    w = {n: r2(a[n]) for n in WEIGHTS}
    m = {n: r2(a["m_" + n]) for n in WEIGHTS}
    v = {n: r2(a["v_" + n]) for n in WEIGHTS}
    xs, target = x[0], loss_target[0]
    S = xs.shape[0]
    ts = TS

    w_in_blk = jnp.pad(w["w_in"].T.astype(bf16), ((0, W_IN_PAD - W_IN_ROWS), (0, 0)))
    conv_blk = jnp.concatenate([_pad_rows8(w["ssd_conv_w"]), _pad_rows8(w["ffn_conv_w"]),
                                jnp.zeros((8, CONV_BLOCK - SSD_CONV_COLS - FFN_CONV_COLS), f32)], axis=1)
    g_in, g_conv = gather_two_level([w_in_blk, conv_blk], "gather_first")
    wt = g_in[:, :W_IN_ROWS].reshape(IN_DIM, D)
    w_main_t = jnp.concatenate([wt[IN_OFF[0]:IN_OFF[1]], wt[IN_OFF[6]:IN_OFF[8]], wt[IN_OFF[1]:IN_OFF[2]]], axis=0)
    w_q_t = wt[IN_OFF[3]:IN_OFF[4]]
    w_kv_t = wt[IN_OFF[4]:IN_OFF[6]]
    w_dt_t = jnp.pad(wt[IN_OFF[2]:IN_OFF[3]], ((0, LANES - NH), (0, 0)))
    conv_w8 = g_conv[:, :, 0:SSD_CONV_COLS].transpose(1, 0, 2).reshape(8, CONVD)
    fconv_w8 = g_conv[:, :, SSD_CONV_COLS:SSD_CONV_COLS + FFN_CONV_COLS].transpose(1, 0, 2).reshape(8, 2 * FF)
    bias = _pad_lanes(w["ssd_dt_bias"])
    alog = _pad_lanes(w["ssd_a_log"])
    dx_row = jnp.repeat(w["ssd_d"].reshape(-1), HD).reshape(1, DI)
    sinks = _pad_lanes(w["attn_sinks"])

    u = prenorm_fwd(xs, w["norm_mix_pre_w"], ts)
    later = [w["ssd_w_out"].astype(bf16), w["attn_w_out"].astype(bf16), w["w_mix_out"].astype(bf16)]
    proj, (g_so, g_ao, g_mix) = mm(u, w_main_t, "nt", bf16, "mm_proj", comm=(later, (False,) * 3))
    w_ssd_out, w_attn_out, w_mix = g_so.reshape(DI, D), g_ao.reshape(D, D), g_mix.reshape(D, D)
    qt = mm(w_q_t, u, "nt", bf16, "mm_q")
    kvt = mm(w_kv_t, u, "nt", bf16, "mm_kv")
    dtr = mm(u, w_dt_t, "nt", f32, "mm_dt")
    xbc, conv_c = ssdconv_fwd(proj, conv_w8, w["ssd_conv_b"], ts)
    y, hprev, (g_up, g_down) = ssd_fwd(xbc, dtr, bias, alog, dx_row, comm=(
        [w["ffn_w_up"].T.astype(bf16), w["ffn_w_down"].astype(bf16)], (False, False)))
    w_up_t = g_up.reshape(2 * FF, D)
    w_down = g_down.reshape(FF, D)
    yn = gnorm_fwd(y, proj, w["ssd_norm_w"], ts)
    ys = mm(yn, w_ssd_out, "nn", bf16, "mm_ssd_out")
    cos, sin = rope_tables(positions, ts)
    aot = attn_fwd(qt, kvt, cos, sin, sinks)
    ya = mm(aot, w_attn_out, "tn", bf16, "mm_attn_out")
    merged = merge_fwd(proj, ys, ya, ts)
    mo = mm(merged, w_mix, "nn", f32, "mm_mix")
    x1, h = post_fwd(xs, mo, w["norm_mix_post_w"], w["norm_ffn_pre_w"], ts)
    up = mm(h, w_up_t, "nt", bf16, "mm_up")
    act, gate_c, val_c = ffnact_fwd(up, fconv_w8, w["ffn_conv_b"], ts)
    ff = mm(act, w_down, "nn", f32, "mm_down")
    loss_blk, dout, dff, g_post2 = loss_head(x1, ff, target, w["norm_ffn_post_w"], ts)

    dact = mm(dff, w_down, "nt", bf16, "mm_dact")
    gw_down = mm(act, dff, "tn", bf16, "mm_g_down")
    dgate, dval = ffnact_bwd(dact, gate_c, val_c, ts)
    dup_pre, g_fconv_a = dwconv_bwd(dgate, up, 0, fconv_w8, 0, FFN_K, FF, ts, "ffnconv_bwd_gate", out_cols=2 * FF)
    dup_pre, g_fconv_b = dwconv_bwd(dval, up, 1, fconv_w8, 1, FFN_K, FF, ts, "ffnconv_bwd_val", into=dup_pre, ocb=1,
                                    out_cols=2 * FF)
    g_fconv = jnp.concatenate([g_fconv_a, g_fconv_b], axis=1)
    dh, (r_down,) = mm(dup_pre, w_up_t, "nn", bf16, "mm_dh", comm=([gw_down.reshape(N_DEV, FF // N_DEV, D)], (True,)))
    gw_up_t = mm(dup_pre, h, "tn", bf16, "mm_g_up")
    dx1, dmo, g_norms = post_bwd(dout, dh, x1, mo, w["norm_mix_post_w"], w["norm_ffn_pre_w"], ts)
    dmerged = mm(dmo, w_mix, "nt", bf16, "mm_dmerged")
    gw_mix = mm(merged, dmo, "tn", bf16, "mm_g_mix")
    dys, dya, dproj = merge_bwd(dmerged, proj, ys, ya, ts)
    daot = mm(w_attn_out, dya, "nt", bf16, "mm_dao")
    gw_attn_out = mm(aot, dya, "nn", bf16, "mm_g_attn_out")
    send_up = jnp.pad(gw_up_t.reshape(N_DEV, FFN_CONV_COLS, D), ((0, 0), (0, W_UP_PAD - FFN_CONV_COLS), (0, 0)))
    dqt, dkvt, g_sinks, (r_up,) = attn_bwd(qt, kvt, cos, sin, sinks, daot, comm=([send_up], (True,)))
    du_b = mm(dkvt, w_kv_t, "tn", bf16, "mm_du_kv")
    du_d = mm(dqt, w_q_t, "tn", bf16, "mm_du_q")
    dyn = mm(dys, w_ssd_out, "nt", bf16, "mm_dyn")
    gw_ssd_out = mm(yn, dys, "tn", bf16, "mm_g_ssd_out")
    dy, dproj, g_gnorm = gnorm_bwd(dyn, y, proj, w["ssd_norm_w"], dproj, ts)
    sends = [gw_ssd_out.reshape(N_DEV, DI // N_DEV, D), gw_attn_out.reshape(N_DEV, D // N_DEV, D),
             gw_mix.reshape(N_DEV, D // N_DEV, D)]
    dxbc, ddtr, g_ssd, (r_so, r_ao, r_mix) = ssd_bwd(xbc, dtr, dy, hprev, bias, alog, dx_row, comm=(sends, (True,) * 3))
    dproj, g_conv_w = dwconv_bwd(dxbc, proj, C_XBC // 1024, conv_w8, 0, SSD_K, 1024, ts, "ssdconv_bwd", act_c=conv_c,
                                 into=dproj, ocb=C_XBC // 1024, out_cols=PM)
    ddtr_b = ddtr.astype(bf16)
    du_c = mm(ddtr_b, w_dt_t, "nn", bf16, "mm_du_dt")
    g_main_t = mm(dproj, u, "tn", bf16, "mm_g_in")
    g_q_t = mm(dqt, u, "nn", bf16, "mm_g_q")
    g_kv_t = mm(dkvt, u, "nn", bf16, "mm_g_kv")
    g_dt_t = mm(ddtr_b, u, "tn", bf16, "mm_g_dt")
    g_wt = jnp.concatenate([g_main_t[C_Z:C_GS], g_main_t[C_XBC:PM], g_dt_t[:NH], g_q_t, g_kv_t, g_main_t[C_GS:C_XBC]],
                           axis=0)
    send_in = jnp.pad(g_wt.reshape(N_DEV, W_IN_ROWS, D), ((0, 0), (0, W_IN_PAD - W_IN_ROWS), (0, 0)))
    pieces = {"norm_mix_post_w": g_norms[1:2], "norm_ffn_pre_w": g_norms[0:1], "norm_ffn_post_w": g_post2[0:1],
              "ssd_norm_w": g_gnorm[0:1], "ssd_conv_b": g_conv_w[7:8], "ffn_conv_b": g_fconv[7:8],
              "ssd_dt_bias": g_ssd[0:1], "ssd_a_log": g_ssd[1:2], "ssd_d": g_ssd[2:3], "attn_sinks": g_sinks[0:1],
              "loss": loss_blk[0:1]}
    row = jnp.concatenate([jnp.pad(pieces[n][:, :min(k, pieces[n].shape[1])],
                                   ((0, 0), (0, -(-k // LANES) * LANES - min(k, pieces[n].shape[1]))))
                           for n, k in SMALL_ROW], axis=1)
    send_row = jnp.pad(row, ((0, 7), (0, 0)))
    send_conv = jnp.concatenate(
        [g_conv_w.reshape(8, N_DEV, SSD_CONV_COLS).transpose(1, 0, 2),
         g_fconv.reshape(8, N_DEV, FFN_CONV_COLS).transpose(1, 0, 2),
         jnp.zeros((N_DEV, 8, CONV_BLOCK - SSD_CONV_COLS - FFN_CONV_COLS), f32)], axis=2)
    half = W_IN_PAD // 2
    du_a, (r_in_a, recv_row, recv_conv) = mm(dproj, w_main_t, "nn", bf16, "mm_du",
                                             comm=([send_in[:, :half], send_row, send_conv], (True, False, True)))
    grad_x, g_pre, (r_in_b,) = prenorm_bwd(xs, w["norm_mix_pre_w"], (du_a, du_b, du_c, du_d), dx1, ts,
                                           comm=([send_in[:, half:]], (True,)))
    (recv_pre,) = exchange([g_pre], (False,), "gather_last")

    r_in = jnp.concatenate([r_in_a, r_in_b], axis=1)
    big = {"w_in": adamw_transposed(r_in, w["w_in"], m["w_in"], v["w_in"], "adamw_w_in"),
           "ffn_w_up": adamw_transposed(r_up, w["ffn_w_up"], m["ffn_w_up"], v["ffn_w_up"], "adamw_w_up"),
           "ssd_w_out": adamw(r_so, w["ssd_w_out"], m["ssd_w_out"], v["ssd_w_out"], 256, "adamw_ssd_out"),
           "attn_w_out": adamw(r_ao, w["attn_w_out"], m["attn_w_out"], v["attn_w_out"], 128, "adamw_attn_out"),
           "w_mix_out": adamw(r_mix, w["w_mix_out"], m["w_mix_out"], v["w_mix_out"], 128, "adamw_mix"),
           "ffn_w_down": adamw(r_down, w["ffn_w_down"], m["ffn_w_down"], v["ffn_w_down"], 352, "adamw_down")}
    small_names = [n for n in WEIGHTS if n not in big]
    small, loss_row = adamw_small(recv_row, recv_pre, recv_conv, {n: (w[n], m[n], v[n]) for n in small_names})

    outs = [loss_row[0, 0], grad_x[None]]
    for k in range(4):
        for n in WEIGHTS:
            outs.append((big[n][k] if n in big else small[n][k]).reshape(a[n].shape))
    return tuple(outs)
```

```python
import jax
import jax.numpy as jnp
import numpy as np
from jax import lax
from jax.experimental import pallas as pl
from jax.experimental.pallas import tpu as pltpu

f32 = jnp.float32
bf16 = jnp.bfloat16

N_DEV = 8
D = 1024
DI = 2048
NH = 32
HD = 64
NG = 4
GW = DI // NG
NS = 128
CH = 128
CONVD = DI + 2 * NG * NS
SSD_K = 4
AH = 16
AD = 64
KVH = 4
REP = AH // KVH
KVW = KVH * AD
WIN = 128
FF = 2816
FFN_K = 3
EPS = 1e-6
ROPE_THETA = 10000.0
LANES = 128
RG = 16
CW = 256

C_Z, C_GS, C_GA, C_XBC, PM = 0, 2048, 3072, 4096, 7168
IN_SIZES = (DI, CONVD, NH, D, KVW, KVW, D, D)
IN_OFF = tuple(int(v) for v in np.cumsum((0,) + IN_SIZES))
IN_DIM = IN_OFF[-1]

ADAM_LR, ADAM_B1, ADAM_B2, ADAM_EPS, ADAM_WD, ADAM_STEP = 0.001, 0.9, 0.999, 1e-08, 0.01, 10

VMEM_LIMIT = 56 * 1024 * 1024


def _cp(*sem, side_effects=False):
    return pltpu.CompilerParams(dimension_semantics=sem, vmem_limit_bytes=VMEM_LIMIT, has_side_effects=side_effects)


def _dot(a, b, mode="nn"):
    dims = {"nn": (((1,), (0,)), ((), ())), "nt": (((1,), (1,)), ((), ())), "tn": (((0,), (0,)), ((), ()))}[mode]
    return lax.dot_general(a, b, dims, preferred_element_type=f32)


def _split3(v):
    hi = v.astype(bf16)
    r = v - hi.astype(f32)
    mid = r.astype(bf16)
    lo = (r - mid.astype(f32)).astype(bf16)
    return hi, mid, lo


def _dot3_left(m01, v):
    hi, mid, lo = _split3(v)
    return _dot(m01, hi) + _dot(m01, mid) + _dot(m01, lo)


def _dot3_right(v, m01):
    hi, mid, lo = _split3(v)
    return _dot(hi, m01) + _dot(mid, m01) + _dot(lo, m01)


def _dot2_right(v, m01):
    hi = v.astype(bf16)
    lo = (v - hi.astype(f32)).astype(bf16)
    return _dot(hi, m01) + _dot(lo, m01)


def _sigmoid(x):
    return 1.0 / (1.0 + jnp.exp(-x))


def _sigmoid_fast(x):
    return pl.reciprocal(1.0 + jnp.exp(-x), approx=True)


def _peer(k, x, y, c):
    return ((1 - x) if k & 4 else x, (1 - y) if k & 2 else y, (1 - c) if k & 1 else c)


def _xchg_copies(buf_refs, out_refs, send_sems, recv_sems, local_sems, personalised):
    x, y, c = lax.axis_index("x"), lax.axis_index("y"), lax.axis_index("c")
    me = 4 * x + 2 * y + c
    local, remote = [], []
    for b, (buf, out, pers) in enumerate(zip(buf_refs, out_refs, personalised)):
        local.append(pltpu.make_async_copy(buf.at[me] if pers else buf, out.at[me], local_sems.at[b]))
        for k in range(1, N_DEV):
            px, py, pc = _peer(k, x, y, c)
            s = b * (N_DEV - 1) + k - 1
            remote.append(pltpu.make_async_remote_copy(
                src_ref=buf.at[4 * px + 2 * py + pc] if pers else buf, dst_ref=out.at[me],
                send_sem=send_sems.at[s], recv_sem=recv_sems.at[s],
                device_id=(px, py, pc), device_id_type=pl.DeviceIdType.MESH))
    return local, remote


class _Comm:
    def __init__(self, comm):
        self.bufs, self.pers = comm if comm else ((), ())
        self.n = len(self.bufs)

    def in_specs(self):
        return [pl.BlockSpec(memory_space=pl.ANY)] * self.n

    out_specs = in_specs

    def out_shape(self):
        return [jax.ShapeDtypeStruct((N_DEV,) + tuple(b.shape[1:] if p else b.shape), b.dtype)
                for b, p in zip(self.bufs, self.pers)]

    def scratch(self):
        n = self.n
        return [pltpu.SemaphoreType.DMA((n * (N_DEV - 1),)), pltpu.SemaphoreType.DMA((n * (N_DEV - 1),)),
                pltpu.SemaphoreType.DMA((n,))] if n else []

    def split(self, refs, n_in, n_out):
        n = self.n
        ins, outs = refs[:n_in], refs[n_in + n:n_in + n + n_out]
        rest = refs[n_in + n + n_out + n:]
        if not n:
            return ins, outs, rest, None
        copies = _xchg_copies(refs[n_in:n_in + n], refs[n_in + n + n_out:n_in + n + n_out + n], *rest[-3:], self.pers)
        return ins, outs, rest[:-3], copies

    def start(self, copies, first):
        if copies:
            @pl.when(first)
            def _():
                for cp in copies[0] + copies[1]:
                    cp.start()

    def wait(self, copies, last):
        if copies:
            @pl.when(last)
            def _():
                for cp in copies[1]:
                    cp.wait_recv()
                for cp in copies[1]:
                    cp.wait_send()
                for cp in copies[0]:
                    cp.wait()


def exchange(bufs, personalised, name):
    cm = _Comm((bufs, personalised))

    def body(*refs):
        _, _, _, copies = cm.split(refs, 0, 0)
        cm.start(copies, True)
        cm.wait(copies, True)

    return pl.pallas_call(
        body, name=name, in_specs=cm.in_specs(), out_specs=cm.out_specs(), out_shape=cm.out_shape(),
        scratch_shapes=cm.scratch(), compiler_params=pltpu.CompilerParams(has_side_effects=True),
    )(*bufs)


def gather_two_level(bufs, name):
    n = len(bufs)
    per = N_DEV - 1

    def body(*refs):
        ins, outs = refs[:n], refs[n:2 * n]
        send_sems, recv_sems, local_sems = refs[2 * n:]
        x, y, c = lax.axis_index("x"), lax.axis_index("y"), lax.axis_index("c")
        me, sibling = (x, y, c), (x, y, 1 - c)
        chips = [(1 - x, y), (x, 1 - y), (1 - x, 1 - y)]

        def copy(b, k, block, to, src=None):
            dst = outs[b].at[4 * block[0] + 2 * block[1] + block[2]]
            return pltpu.make_async_remote_copy(
                src_ref=dst if src is None else src, dst_ref=dst,
                send_sem=send_sems.at[b * per + k], recv_sem=recv_sems.at[b * per + k],
                device_id=to, device_id_type=pl.DeviceIdType.MESH)

        mine = [pltpu.make_async_copy(ins[b], outs[b].at[4 * x + 2 * y + c], local_sems.at[b]) for b in range(n)]
        first = []
        for b in range(n):
            first.append(copy(b, 0, me, sibling, src=ins[b]))
            first += [copy(b, 1 + j, me, (*chip, c), src=ins[b]) for j, chip in enumerate(chips)]
        for cp in mine + first:
            cp.start()
        passed = []
        for j, chip in enumerate(chips):
            for b in range(n):
                copy(b, 1 + j, (*chip, c), me).wait_recv()
                passed.append(copy(b, 4 + j, (*chip, c), sibling))
                passed[-1].start()
        for b in range(n):
            copy(b, 0, sibling, me).wait_recv()
            for j, chip in enumerate(chips):
                copy(b, 4 + j, (*chip, 1 - c), me).wait_recv()
        for cp in first + passed:
            cp.wait_send()
        for cp in mine:
            cp.wait()

    hbm = pl.BlockSpec(memory_space=pl.ANY)
    return pl.pallas_call(
        body, name=name, in_specs=[hbm] * n, out_specs=[hbm] * n,
        out_shape=[jax.ShapeDtypeStruct((N_DEV,) + tuple(b.shape), b.dtype) for b in bufs],
        scratch_shapes=[pltpu.SemaphoreType.DMA((n * per,)), pltpu.SemaphoreType.DMA((n * per,)),
                        pltpu.SemaphoreType.DMA((n,))],
        compiler_params=pltpu.CompilerParams(has_side_effects=True),
    )(*bufs)


MM_TILES = (3584, 2176, 2048, 1792, 1408, 1024, 512, 256, 128)
MM_VMEM_BUDGET = 40 * 1024 * 1024


def _mm_tiles(M, N, K, out_bytes):
    cm = [t for t in MM_TILES if M % t == 0]
    cn = [t for t in MM_TILES if N % t == 0]
    ck = [t for t in MM_TILES if K % t == 0]
    best = None
    for bm in cm[:2]:
        for bn in cn:
            for bk in ck:
                need = 4 * (bm * bk + bk * bn) + bm * bn * (4 + 2 * out_bytes)
                if need <= MM_VMEM_BUDGET:
                    score = (bm * bn * bk, bk)
                    if best is None or score > best[0]:
                        best = (score, (bm, bn, bk))
    return best[1]


def mm(a, b, mode, out_dtype, name, comm=None):
    if mode == "nn":
        (M, K), (_, N) = a.shape, b.shape
    elif mode == "nt":
        (M, K), (N, _) = a.shape, b.shape
    else:
        (K, M), (_, N) = a.shape, b.shape
    bm, bn, bk = _mm_tiles(M, N, K, jnp.dtype(out_dtype).itemsize)
    gm, gn, nk = M // bm, N // bn, K // bk
    cm = _Comm(comm)

    def body(*refs):
        (a_ref, b_ref), (o_ref,), scr, copies = cm.split(refs, 2, 1)
        i, j, k = pl.program_id(0), pl.program_id(1), pl.program_id(2)
        cm.start(copies, jnp.logical_and(jnp.logical_and(i == 0, j == 0), k == 0))
        p = _dot(a_ref[...], b_ref[...], mode)
        if nk == 1:
            o_ref[...] = p.astype(o_ref.dtype)
        else:
            acc_ref = scr[0]

            @pl.when(k == 0)
            def _():
                acc_ref[...] = p

            @pl.when(k > 0)
            def _():
                acc_ref[...] += p

            @pl.when(k == nk - 1)
            def _():
                o_ref[...] = acc_ref[...].astype(o_ref.dtype)

        cm.wait(copies, jnp.logical_and(jnp.logical_and(i == gm - 1, j == gn - 1), k == nk - 1))

    if mode == "nn":
        a_spec = pl.BlockSpec((bm, bk), lambda i, j, k: (i, k))
        b_spec = pl.BlockSpec((bk, bn), lambda i, j, k: (k, j))
    elif mode == "nt":
        a_spec = pl.BlockSpec((bm, bk), lambda i, j, k: (i, k))
        b_spec = pl.BlockSpec((bn, bk), lambda i, j, k: (j, k))
    else:
        a_spec = pl.BlockSpec((bk, bm), lambda i, j, k: (k, i))
        b_spec = pl.BlockSpec((bk, bn), lambda i, j, k: (k, j))
    sem = ("arbitrary",) * 3 if cm.n else ("parallel", "parallel", "arbitrary")
    res = pl.pallas_call(
        body, name=name, grid=(gm, gn, nk),
        in_specs=[a_spec, b_spec] + cm.in_specs(),
        out_specs=[pl.BlockSpec((bm, bn), lambda i, j, k: (i, j))] + cm.out_specs(),
        out_shape=[jax.ShapeDtypeStruct((M, N), out_dtype)] + cm.out_shape(),
        scratch_shapes=([pltpu.VMEM((bm, bn), f32)] if nk > 1 else []) + cm.scratch(),
        compiler_params=_cp(*sem, side_effects=bool(cm.n)),
    )(a, b, *cm.bufs)
    return (res[0], res[1:]) if cm.n else res[0]


def _groups(ts, fn, carry=None, reverse=False, unroll=4, rg=RG):
    n = ts // rg
    if n == 1:
        return fn(0, carry)
    unroll = min(unroll, n)
    span = rg * unroll

    def body(g, c):
        r0 = pl.multiple_of((n // unroll - 1 - g if reverse else g) * span, span)
        for u in (range(unroll - 1, -1, -1) if reverse else range(unroll)):
            c = fn(pl.multiple_of(r0 + u * rg, rg), c)
        return c

    return lax.fori_loop(0, n // unroll, body, carry)


def _rms(x):
    return lax.rsqrt(jnp.mean(x * x, axis=-1, keepdims=True) + EPS)


def _rms_bwd(x, r, dn):
    n = x * r
    return r * (dn - n * jnp.mean(dn * n, axis=-1, keepdims=True))


NRG = 256


def _fold(x):
    return jnp.sum(x.reshape(x.shape[0] // 8, 8, x.shape[1]), axis=0)


def _flush(acc_ref, out_ref, row):
    out_ref[row:row + 1, :] = jnp.sum(acc_ref[...], axis=0, keepdims=True)


def prenorm_fwd(x, w, ts):
    S = x.shape[0]

    def body(x_ref, w_ref, u_ref):
        wv = w_ref[...]

        def grp(r0, _):
            xv = x_ref[pl.ds(r0, NRG), :]
            u_ref[pl.ds(r0, NRG), :] = (xv * _rms(xv) * wv).astype(bf16)

        _groups(ts, grp, rg=NRG)

    return pl.pallas_call(
        body, name="prenorm_fwd", grid=(S // ts,),
        in_specs=[pl.BlockSpec((ts, D), lambda i: (i, 0)), pl.BlockSpec((1, D), lambda i: (0, 0))],
        out_specs=pl.BlockSpec((ts, D), lambda i: (i, 0)),
        out_shape=jax.ShapeDtypeStruct((S, D), bf16),
        compiler_params=_cp("parallel"),
    )(x, w)


def prenorm_bwd(x, w, dus, dx1, ts, comm=None):
    S = x.shape[0]
    nt = S // ts
    nd = len(dus)
    cm = _Comm(comm)

    def body(*refs):
        ins, (gx_ref, gw_ref), (acc_ref,), copies = cm.split(refs, nd + 3, 2)
        x_ref, w_ref = ins[:2]
        du_refs, dx1_ref = ins[2:2 + nd], ins[2 + nd]
        i = pl.program_id(0)
        cm.start(copies, i == 0)
        wv = w_ref[...]

        @pl.when(i == 0)
        def _():
            acc_ref[...] = jnp.zeros_like(acc_ref)
            gw_ref[...] = jnp.zeros_like(gw_ref)

        def grp(r0, _):
            rows = pl.ds(r0, NRG)
            xv = x_ref[rows, :]
            r = _rms(xv)
            du = du_refs[0][rows, :].astype(f32)
            for d_ref in du_refs[1:]:
                du = du + d_ref[rows, :].astype(f32)
            gx_ref[rows, :] = dx1_ref[rows, :] + _rms_bwd(xv, r, du * wv)
            acc_ref[...] += _fold(du * xv * r)

        _groups(ts, grp, rg=NRG)

        @pl.when(i == nt - 1)
        def _():
            _flush(acc_ref, gw_ref, 0)

        cm.wait(copies, i == nt - 1)

    row = pl.BlockSpec((ts, D), lambda i: (i, 0))
    res = pl.pallas_call(
        body, name="prenorm_bwd", grid=(nt,),
        in_specs=[row, pl.BlockSpec((1, D), lambda i: (0, 0))] + [row] * (nd + 1) + cm.in_specs(),
        out_specs=[row, pl.BlockSpec((8, D), lambda i: (0, 0))] + cm.out_specs(),
        out_shape=[jax.ShapeDtypeStruct((S, D), f32), jax.ShapeDtypeStruct((8, D), f32)] + cm.out_shape(),
        scratch_shapes=[pltpu.VMEM((8, D), f32)] + cm.scratch(),
        compiler_params=_cp("arbitrary", side_effects=bool(cm.n)),
    )(x, w, *dus, dx1, *cm.bufs)
    return res[0], res[1], res[2:]


def post_fwd(x, mo, w_post, w_pre2, ts):
    S = x.shape[0]

    def body(x_ref, mo_ref, wp_ref, w2_ref, x1_ref, h_ref):
        wp, w2 = wp_ref[...], w2_ref[...]

        def grp(r0, _):
            rows = pl.ds(r0, NRG)
            mv = mo_ref[rows, :].astype(f32)
            x1 = x_ref[rows, :] + mv * _rms(mv) * wp
            x1_ref[rows, :] = x1
            h_ref[rows, :] = (x1 * _rms(x1) * w2).astype(bf16)

        _groups(ts, grp, rg=NRG)

    row = pl.BlockSpec((ts, D), lambda i: (i, 0))
    par = pl.BlockSpec((1, D), lambda i: (0, 0))
    return pl.pallas_call(
        body, name="post_fwd", grid=(S // ts,),
        in_specs=[row, row, par, par], out_specs=[row, row],
        out_shape=[jax.ShapeDtypeStruct((S, D), f32), jax.ShapeDtypeStruct((S, D), bf16)],
        compiler_params=_cp("parallel"),
    )(x, mo, w_post, w_pre2)


def post_bwd(dout, dh, x1, mo, w_post, w_pre2, ts):
    S = x1.shape[0]
    nt = S // ts

    def body(dout_ref, dh_ref, x1_ref, mo_ref, wp_ref, w2_ref, dx1_ref, dmo_ref, gw_ref, acc2_ref, accp_ref):
        i = pl.program_id(0)
        wp, w2 = wp_ref[...], w2_ref[...]

        @pl.when(i == 0)
        def _():
            acc2_ref[...] = jnp.zeros_like(acc2_ref)
            accp_ref[...] = jnp.zeros_like(accp_ref)
            gw_ref[...] = jnp.zeros_like(gw_ref)

        def grp(r0, _):
            rows = pl.ds(r0, NRG)
            x1 = x1_ref[rows, :]
            r1 = _rms(x1)
            dh = dh_ref[rows, :].astype(f32)
            dx1 = dout_ref[rows, :] + _rms_bwd(x1, r1, dh * w2)
            dx1_ref[rows, :] = dx1
            acc2_ref[...] += _fold(dh * x1 * r1)
            mv = mo_ref[rows, :].astype(f32)
            rm = _rms(mv)
            dmo_ref[rows, :] = _rms_bwd(mv, rm, dx1 * wp).astype(bf16)
            accp_ref[...] += _fold(dx1 * mv * rm)

        _groups(ts, grp, rg=NRG)

        @pl.when(i == nt - 1)
        def _():
            _flush(acc2_ref, gw_ref, 0)
            _flush(accp_ref, gw_ref, 1)

    row = pl.BlockSpec((ts, D), lambda i: (i, 0))
    par = pl.BlockSpec((1, D), lambda i: (0, 0))
    return pl.pallas_call(
        body, name="post_bwd", grid=(nt,),
        in_specs=[row, row, row, row, par, par],
        out_specs=[row, row, pl.BlockSpec((8, D), lambda i: (0, 0))],
        out_shape=[jax.ShapeDtypeStruct((S, D), f32), jax.ShapeDtypeStruct((S, D), bf16),
                   jax.ShapeDtypeStruct((8, D), f32)],
        scratch_shapes=[pltpu.VMEM((8, D), f32), pltpu.VMEM((8, D), f32)],
        compiler_params=_cp("arbitrary"),
    )(dout, dh, x1, mo, w_post, w_pre2)


def loss_head(x1, ff, target, w, ts):
    S = x1.shape[0]
    nt = S // ts

    def body(x1_ref, ff_ref, t_ref, w_ref, loss_ref, dout_ref, dff_ref, gw_ref, accw_ref, accl_ref):
        i = pl.program_id(0)
        wv = w_ref[...]

        @pl.when(i == 0)
        def _():
            accw_ref[...] = jnp.zeros_like(accw_ref)
            accl_ref[...] = jnp.zeros_like(accl_ref)
            gw_ref[...] = jnp.zeros_like(gw_ref)

        def grp(r0, _):
            rows = pl.ds(r0, NRG)
            fv = ff_ref[rows, :].astype(f32)
            r = _rms(fv)
            n = fv * r
            e = x1_ref[rows, :] + n * wv - t_ref[rows, :]
            dout = e * (1.0 / D)
            dout_ref[rows, :] = dout
            dff_ref[rows, :] = _rms_bwd(fv, r, dout * wv).astype(bf16)
            accw_ref[...] += _fold(dout * n)
            accl_ref[...] += _fold(e * e)

        _groups(ts, grp, rg=NRG)

        @pl.when(i == nt - 1)
        def _():
            _flush(accw_ref, gw_ref, 0)
            tot = jnp.sum(jnp.sum(accl_ref[...], axis=1, keepdims=True), axis=0, keepdims=True) * (0.5 / D)
            loss_ref[...] = jnp.broadcast_to(tot, loss_ref.shape)

    row = pl.BlockSpec((ts, D), lambda i: (i, 0))
    return pl.pallas_call(
        body, name="loss_head", grid=(nt,),
        in_specs=[row, row, row, pl.BlockSpec((1, D), lambda i: (0, 0))],
        out_specs=[pl.BlockSpec((8, LANES), lambda i: (0, 0)), row, row, pl.BlockSpec((8, D), lambda i: (0, 0))],
        out_shape=[jax.ShapeDtypeStruct((8, LANES), f32), jax.ShapeDtypeStruct((S, D), f32),
                   jax.ShapeDtypeStruct((S, D), bf16), jax.ShapeDtypeStruct((8, D), f32)],
        scratch_shapes=[pltpu.VMEM((8, D), f32), pltpu.VMEM((8, D), f32)],
        compiler_params=_cp("arbitrary"),
    )(x1, ff, target, w)


def _taps(w_ref, cs, K):
    return [jnp.broadcast_to(w_ref[k:k + 1, cs], (8, CW)) for k in range(K)]


def _down(before, cur, s, sub):
    return jnp.where(sub < s, pltpu.roll(before, s, 0), pltpu.roll(cur, s, 0))


def _up(cur, after, s, sub):
    return jnp.where(sub < 8 - s, pltpu.roll(cur, 8 - s, 0), pltpu.roll(after, 8 - s, 0))


def _conv_group(p, a, b, taps, bias, K, sub):
    ya, yb = bias, bias
    for k in range(K):
        s = K - 1 - k
        xa, xb = (a, b) if s == 0 else (_down(p, a, s, sub), _down(a, b, s, sub))
        ya = ya + taps[k] * xa
        yb = yb + taps[k] * xb
    return ya, yb


def _prev8_map(ts, cb):
    return lambda i, j: (jnp.maximum(i * (ts // 8) - 1, 0), cb + j)


def ssdconv_fwd(proj, w8, b, ts):
    S = proj.shape[0]
    bw = 1024
    cb = C_XBC // bw

    def body(cur_ref, prev_ref, w_ref, b_ref, o_ref, c_ref):
        first = pl.program_id(0) == 0
        sub = lax.broadcasted_iota(jnp.int32, (8, CW), 0)
        for c0 in range(0, bw, CW):
            cs = slice(c0, c0 + CW)
            taps = _taps(w_ref, cs, SSD_K)
            bias = jnp.broadcast_to(b_ref[:, cs], (8, CW))

            def grp(r0, p, cs=cs, taps=taps, bias=bias):
                rows = pl.ds(r0, RG)
                xv = cur_ref[rows, cs].astype(f32)
                ya, yb = _conv_group(p, xv[0:8], xv[8:16], taps, bias, SSD_K, sub)
                y = jnp.concatenate([ya, yb], axis=0)
                c_ref[rows, cs] = y.astype(bf16)
                o_ref[rows, cs] = (y * _sigmoid_fast(y)).astype(bf16)
                return xv[8:16]

            _groups(ts, grp, jnp.where(first, 0.0, prev_ref[:, cs].astype(f32)))

    o = jax.ShapeDtypeStruct((S, CONVD), bf16)
    blk = pl.BlockSpec((ts, bw), lambda i, j: (i, j))
    return pl.pallas_call(
        body, name="ssdconv_fwd", grid=(S // ts, CONVD // bw),
        in_specs=[pl.BlockSpec((ts, bw), lambda i, j: (i, cb + j)),
                  pl.BlockSpec((8, bw), _prev8_map(ts, cb)),
                  pl.BlockSpec((8, bw), lambda i, j: (0, j)),
                  pl.BlockSpec((1, bw), lambda i, j: (0, j))],
        out_specs=[blk, blk], out_shape=[o, o],
        compiler_params=_cp("parallel", "parallel"),
    )(proj, proj, w8, b)


def _gelu_tanh(x):
    c = 0.7978845608028654
    t = jnp.tanh(c * (x + 0.044715 * x * x * x))
    return 0.5 * x * (1.0 + t), t


def ffnact_fwd(up, w8, b, ts):
    S = up.shape[0]

    def body(g_ref, gp_ref, v_ref, vp_ref, wg_ref, wv_ref, bg_ref, bv_ref, o_ref, gc_ref, vc_ref):
        first = pl.program_id(0) == 0
        sub = lax.broadcasted_iota(jnp.int32, (8, CW), 0)
        for c0 in range(0, FF, CW):
            cs = slice(c0, c0 + CW)
            tg, tv = _taps(wg_ref, cs, FFN_K), _taps(wv_ref, cs, FFN_K)
            bg = jnp.broadcast_to(bg_ref[:, cs], (8, CW))
            bv = jnp.broadcast_to(bv_ref[:, cs], (8, CW))

            def grp(r0, carry, cs=cs, tg=tg, tv=tv, bg=bg, bv=bv):
                pg, pv = carry
                rows = pl.ds(r0, RG)
                gx = g_ref[rows, cs].astype(f32)
                vx = v_ref[rows, cs].astype(f32)
                g = jnp.concatenate(_conv_group(pg, gx[0:8], gx[8:16], tg, bg, FFN_K, sub), axis=0)
                v = jnp.concatenate(_conv_group(pv, vx[0:8], vx[8:16], tv, bv, FFN_K, sub), axis=0)
                gc_ref[rows, cs] = g.astype(bf16)
                vc_ref[rows, cs] = v.astype(bf16)
                o_ref[rows, cs] = (_gelu_tanh(g)[0] * v).astype(bf16)
                return gx[8:16], vx[8:16]

            _groups(ts, grp, (jnp.where(first, 0.0, gp_ref[:, cs].astype(f32)),
                              jnp.where(first, 0.0, vp_ref[:, cs].astype(f32))))

    o = jax.ShapeDtypeStruct((S, FF), bf16)
    blk = pl.BlockSpec((ts, FF), lambda i: (i, 0))
    prev = lambda cb: pl.BlockSpec((8, FF), lambda i: (jnp.maximum(i * (ts // 8) - 1, 0), cb))
    return pl.pallas_call(
        body, name="ffnact_fwd", grid=(S // ts,),
        in_specs=[blk, prev(0), pl.BlockSpec((ts, FF), lambda i: (i, 1)), prev(1),
                  pl.BlockSpec((8, FF), lambda i: (0, 0)), pl.BlockSpec((8, FF), lambda i: (0, 1)),
                  pl.BlockSpec((1, FF), lambda i: (0, 0)), pl.BlockSpec((1, FF), lambda i: (0, 1))],
        out_specs=[blk, blk, blk], out_shape=[o, o, o],
        compiler_params=_cp("parallel"),
    )(up, up, up, up, w8, w8, b, b)


def ffnact_bwd(dact, gc, vc, ts):
    S = dact.shape[0]

    def body(d_ref, g_ref, v_ref, dg_ref, dv_ref):
        c = 0.7978845608028654
        for c0 in range(0, FF, CW):
            cs = slice(c0, c0 + CW)

            def grp(r0, _, cs=cs):
                rows = pl.ds(r0, RG)
                d = d_ref[rows, cs].astype(f32)
                g = g_ref[rows, cs].astype(f32)
                ge, t = _gelu_tanh(g)
                dgelu = 0.5 * (1.0 + t) + 0.5 * g * (1.0 - t * t) * c * (1.0 + 3.0 * 0.044715 * g * g)
                dg_ref[rows, cs] = (d * v_ref[rows, cs].astype(f32) * dgelu).astype(bf16)
                dv_ref[rows, cs] = (d * ge).astype(bf16)

            _groups(ts, grp)

    o = jax.ShapeDtypeStruct((S, FF), bf16)
    blk = pl.BlockSpec((ts, FF), lambda i: (i, 0))
    return pl.pallas_call(
        body, name="ffnact_bwd", grid=(S // ts,),
        in_specs=[blk, blk, blk], out_specs=[blk, blk], out_shape=[o, o],
        compiler_params=_cp("parallel"),
    )(dact, gc, vc)


def dwconv_bwd(dy, x, xcb, w8, wcb, K, bw, ts, name, act_c=None, into=None, ocb=0, out_cols=None):
    S, C = dy.shape
    nr = S // ts
    out_cols = out_cols or C
    n_act = 0 if act_c is None else 2

    def body(*refs):
        dy_ref, dyn_ref = refs[0:2]
        c_ref, cn_ref = (refs[2:4] if n_act else (None, None))
        x_ref, xp_ref, w_ref = refs[2 + n_act:5 + n_act]
        dx_ref, dw_ref, sd_ref = refs[-3:]
        i = pl.program_id(1)
        first, last = i == 0, i == nr - 1
        sub = lax.broadcasted_iota(jnp.int32, (8, CW), 0)

        def grad_y(d, c):
            if c is None:
                return d.astype(f32)
            cv = c.astype(f32)
            s = _sigmoid_fast(cv)
            return d.astype(f32) * s * (1.0 + cv * (1.0 - s))

        @pl.when(first)
        def _():
            dw_ref[...] = jnp.zeros_like(dw_ref)

        for c0 in range(0, bw, CW):
            cs = slice(c0, c0 + CW)
            taps = _taps(w_ref, cs, K)
            zero = jnp.zeros((8, CW), f32)

            def fwd(r0, carry, cs=cs):
                p, accs, accb = carry
                rows = pl.ds(r0, RG)
                g = grad_y(dy_ref[rows, cs], c_ref[rows, cs] if n_act else None)
                sd_ref[rows, cs] = g
                xv = x_ref[rows, cs].astype(f32)
                a, b = xv[0:8], xv[8:16]
                ga, gb = g[0:8], g[8:16]
                new = []
                for k in range(K):
                    s = K - 1 - k
                    xa, xb = (a, b) if s == 0 else (_down(p, a, s, sub), _down(a, b, s, sub))
                    new.append(accs[k] + ga * xa + gb * xb)
                return b, tuple(new), accb + ga + gb

            _, accs, accb = _groups(ts, fwd, (jnp.where(first, 0.0, xp_ref[:, cs].astype(f32)), (zero,) * K, zero))
            for k in range(K):
                dw_ref[k:k + 1, cs] += jnp.sum(accs[k], axis=0, keepdims=True)
            dw_ref[7:8, cs] += jnp.sum(accb, axis=0, keepdims=True)

            def bwd(r0, after, cs=cs, taps=taps):
                rows = pl.ds(r0, RG)
                g = sd_ref[rows, cs]
                a, b = g[0:8], g[8:16]
                da, db = zero, zero
                for k in range(K):
                    s = K - 1 - k
                    ua, ub = (a, b) if s == 0 else (_up(a, b, s, sub), _up(b, after, s, sub))
                    da = da + taps[k] * ua
                    db = db + taps[k] * ub
                dx_ref[rows, cs] = jnp.concatenate([da, db], axis=0).astype(bf16)
                return a

            halo = grad_y(dyn_ref[:, cs], cn_ref[:, cs] if n_act else None)
            _groups(ts, bwd, jnp.where(last, 0.0, halo), reverse=True)

    nxt = lambda j, i: (jnp.minimum((i + 1) * (ts // 8), S // 8 - 1), j)
    tile = pl.BlockSpec((ts, bw), lambda j, i: (i, j))
    acts = [] if act_c is None else [act_c, act_c]
    extra = [] if into is None else [into]
    n_in = 5 + n_act
    return pl.pallas_call(
        body, name=name, grid=(C // bw, nr),
        in_specs=[tile, pl.BlockSpec((8, bw), nxt)] + ([tile, pl.BlockSpec((8, bw), nxt)] if n_act else []) + [
            pl.BlockSpec((ts, bw), lambda j, i: (i, xcb + j)),
            pl.BlockSpec((8, bw), lambda j, i: (jnp.maximum(i * (ts // 8) - 1, 0), xcb + j)),
            pl.BlockSpec((8, bw), lambda j, i: (0, wcb + j))] + [pl.BlockSpec(memory_space=pl.ANY)] * len(extra),
        out_specs=[pl.BlockSpec((ts, bw), lambda j, i: (i, ocb + j)), pl.BlockSpec((8, bw), lambda j, i: (0, j))],
        out_shape=[jax.ShapeDtypeStruct((S, out_cols), bf16), jax.ShapeDtypeStruct((8, C), f32)],
        scratch_shapes=[pltpu.VMEM((ts, bw), f32)],
        input_output_aliases={n_in: 0} if extra else {},
        compiler_params=_cp("parallel", "arbitrary"),
    )(dy, dy, *acts, x, x, w8, *extra)


def gnorm_fwd(y, proj, w, ts):
    S = y.shape[0]

    def body(y_ref, z_ref, w_ref, o_ref):
        for k in range(NG):
            sl = slice(k * GW, (k + 1) * GW)
            wv = w_ref[:, sl]

            def grp(r0, _, sl=sl, wv=wv):
                rows = pl.ds(r0, NRG)
                z = z_ref[rows, sl].astype(f32)
                g = y_ref[rows, sl].astype(f32) * z * _sigmoid_fast(z)
                o_ref[rows, sl] = (g * _rms(g) * wv).astype(bf16)

            _groups(ts, grp, rg=NRG)

    row = pl.BlockSpec((ts, DI), lambda i: (i, 0))
    return pl.pallas_call(
        body, name="gnorm_fwd", grid=(S // ts,),
        in_specs=[row, row, pl.BlockSpec((1, DI), lambda i: (0, 0))],
        out_specs=row, out_shape=jax.ShapeDtypeStruct((S, DI), bf16),
        compiler_params=_cp("parallel"),
    )(y, proj, w)


def gnorm_bwd(dyn, y, proj, w, dproj, ts):
    S = y.shape[0]
    nt = S // ts

    def body(d_ref, y_ref, z_ref, w_ref, _, dy_ref, dz_ref, gw_ref, acc_ref):
        i = pl.program_id(0)

        @pl.when(i == 0)
        def _():
            acc_ref[...] = jnp.zeros_like(acc_ref)
            gw_ref[...] = jnp.zeros_like(gw_ref)

        for k in range(NG):
            sl = slice(k * GW, (k + 1) * GW)
            wv = w_ref[:, sl]

            def grp(r0, _, sl=sl, wv=wv):
                rows = pl.ds(r0, NRG)
                z = z_ref[rows, sl].astype(f32)
                yv = y_ref[rows, sl].astype(f32)
                s = _sigmoid_fast(z)
                sz = z * s
                g = yv * sz
                r = _rms(g)
                d = d_ref[rows, sl].astype(f32)
                acc_ref[:, sl] += _fold(d * g * r)
                dg = _rms_bwd(g, r, d * wv)
                dy_ref[rows, sl] = (dg * sz).astype(bf16)
                dz_ref[rows, sl] = (dg * yv * s * (1.0 + z * (1.0 - s))).astype(bf16)

            _groups(ts, grp, rg=NRG)

        @pl.when(i == nt - 1)
        def _():
            _flush(acc_ref, gw_ref, 0)

    row = pl.BlockSpec((ts, DI), lambda i: (i, 0))
    return pl.pallas_call(
        body, name="gnorm_bwd", grid=(nt,),
        in_specs=[row, row, row, pl.BlockSpec((1, DI), lambda i: (0, 0)), pl.BlockSpec(memory_space=pl.ANY)],
        out_specs=[row, row, pl.BlockSpec((8, DI), lambda i: (0, 0))],
        out_shape=[jax.ShapeDtypeStruct((S, DI), bf16), jax.ShapeDtypeStruct(dproj.shape, bf16),
                   jax.ShapeDtypeStruct((8, DI), f32)],
        scratch_shapes=[pltpu.VMEM((8, DI), f32)],
        input_output_aliases={4: 1},
        compiler_params=_cp("arbitrary"),
    )(dyn, y, proj, w, dproj)


def merge_fwd(proj, ys, ya, ts):
    S = ys.shape[0]

    def body(gs_ref, ga_ref, ys_ref, ya_ref, o_ref):
        for c0 in range(0, D, CW):
            cs = slice(c0, c0 + CW)

            def grp(r0, _, cs=cs):
                rows = pl.ds(r0, NRG)
                o_ref[rows, cs] = (_sigmoid_fast(gs_ref[rows, cs].astype(f32)) * ys_ref[rows, cs].astype(f32)
                                   + _sigmoid_fast(ga_ref[rows, cs].astype(f32)) * ya_ref[rows, cs].astype(f32)
                                   ).astype(bf16)

            _groups(ts, grp, rg=NRG)

    row = pl.BlockSpec((ts, D), lambda i: (i, 0))
    return pl.pallas_call(
        body, name="merge_fwd", grid=(S // ts,),
        in_specs=[pl.BlockSpec((ts, D), lambda i: (i, C_GS // D)), pl.BlockSpec((ts, D), lambda i: (i, C_GA // D)), row, row],
        out_specs=row, out_shape=jax.ShapeDtypeStruct((S, D), bf16),
        compiler_params=_cp("parallel"),
    )(proj, proj, ys, ya)


def merge_bwd(dm, proj, ys, ya, ts):
    S = ys.shape[0]

    def body(d_ref, gs_ref, ga_ref, ys_ref, ya_ref, dys_ref, dya_ref, dg_ref):
        for c0 in range(0, D, CW):
            cs = slice(c0, c0 + CW)

            def grp(r0, _, c0=c0, cs=cs):
                rows = pl.ds(r0, NRG)
                d = d_ref[rows, cs].astype(f32)
                ss = _sigmoid_fast(gs_ref[rows, cs].astype(f32))
                sa = _sigmoid_fast(ga_ref[rows, cs].astype(f32))
                dys_ref[rows, cs] = (d * ss).astype(bf16)
                dya_ref[rows, cs] = (d * sa).astype(bf16)
                dg_ref[rows, cs] = (d * ys_ref[rows, cs].astype(f32) * ss * (1.0 - ss)).astype(bf16)
                dg_ref[rows, D + c0:D + c0 + CW] = (d * ya_ref[rows, cs].astype(f32) * sa * (1.0 - sa)).astype(bf16)

            _groups(ts, grp, rg=NRG)

    row = pl.BlockSpec((ts, D), lambda i: (i, 0))
    o = jax.ShapeDtypeStruct((S, D), bf16)
    return pl.pallas_call(
        body, name="merge_bwd", grid=(S // ts,),
        in_specs=[row, pl.BlockSpec((ts, D), lambda i: (i, C_GS // D)), pl.BlockSpec((ts, D), lambda i: (i, C_GA // D)), row, row],
        out_specs=[row, row, pl.BlockSpec((ts, 2 * D), lambda i: (i, C_GS // (2 * D)))],
        out_shape=[o, o, jax.ShapeDtypeStruct((S, PM), bf16)],
        compiler_params=_cp("parallel"),
    )(dm, proj, proj, ys, ya)


def _ssd_consts():
    h = lax.broadcasted_iota(jnp.int32, (LANES, DI), 0)
    c = lax.broadcasted_iota(jnp.int32, (LANES, DI), 1)
    expand = (c // HD == h).astype(bf16)
    r = lax.broadcasted_iota(jnp.int32, (CH, CH), 0)
    cc = lax.broadcasted_iota(jnp.int32, (CH, CH), 1)
    tril = (cc <= r).astype(bf16)
    triu = (cc >= r).astype(bf16)
    return expand, expand.T, tril, triu


def _ssd_common(xbc_ref, dtr_ref, bias_ref, alog_ref, expand_ref, tril_ref):
    dtr = dtr_ref[...] + bias_ref[...]
    dt = jnp.maximum(dtr, 0.0) + jnp.log1p(jnp.exp(-jnp.abs(dtr)))
    a = -jnp.exp(alog_ref[...])
    acs = _dot3_left(tril_ref[...], dt * a)
    acsx = _dot3_right(acs, expand_ref[...])
    dtx = _dot3_right(dt, expand_ref[...])
    x = xbc_ref[:, 0:DI].astype(f32)
    xdt = x * dtx
    e = jnp.exp(acsx)
    dsx = jnp.exp(acsx[CH - 1:CH, :] - acsx)
    return dtr, dt, a, acs, dtx, x, xdt, e, dsx


def _ssd_lmat(acs, acs_t, hh, causal):
    seg = acs[:, hh:hh + 1] - acs_t[hh:hh + 1, :]
    return jnp.where(causal, jnp.exp(jnp.minimum(seg, 0.0)), 0.0)


def ssd_fwd(xbc, dtr, bias, alog, dx_row, comm=None):
    S = xbc.shape[0]
    nc = S // CH
    expand, _, tril, _ = _ssd_consts()
    cm = _Comm(comm)

    def body(*refs):
        ins, (y_ref, hp_ref), (h_ref, yd_ref), copies = cm.split(refs, 7, 2)
        xbc_ref, dtr_ref, bias_ref, alog_ref, dxr_ref, expand_ref, tril_ref = ins
        c = pl.program_id(0)
        cm.start(copies, c == 0)

        @pl.when(c == 0)
        def _():
            h_ref[...] = jnp.zeros_like(h_ref)

        _, _, _, acs, _, x, xdt, e, dsx = _ssd_common(xbc_ref, dtr_ref, bias_ref, alog_ref, expand_ref, tril_ref)
        acs_t = acs.T
        xb = xdt.astype(bf16)
        xd = (xdt * dsx).astype(bf16)
        causal = tril_ref[...] > 0
        for g in range(NG):
            gs = slice(g * GW, (g + 1) * GW)
            bg = xbc_ref[:, DI + g * NS:DI + (g + 1) * NS]
            cg = xbc_ref[:, DI + NG * NS + g * NS:DI + NG * NS + (g + 1) * NS]
            cb = _dot(cg, bg, "nt")
            hp = h_ref[g]
            hpb = hp.astype(bf16)
            hp_ref[0, g] = hpb
            yd_ref[:, gs] = _dot(cg, hpb) * e[:, gs]
            h_ref[g] = hp * e[CH - 1:CH, gs] + _dot(bg, xd[:, gs], "tn")
            for j in range(NH // NG):
                hh = g * (NH // NG) + j
                hs = slice(hh * HD, (hh + 1) * HD)
                m = (cb * _ssd_lmat(acs, acs_t, hh, causal)).astype(bf16)
                yd_ref[:, hs] += _dot(m, xb[:, hs])
        y_ref[...] = (yd_ref[...] + dxr_ref[...] * x).astype(bf16)
        cm.wait(copies, c == nc - 1)

    par = lambda shape: pl.BlockSpec(shape, lambda c: (0,) * len(shape))
    res = pl.pallas_call(
        body, name="ssd_fwd", grid=(nc,),
        in_specs=[pl.BlockSpec((CH, CONVD), lambda c: (c, 0)), pl.BlockSpec((CH, LANES), lambda c: (c, 0)),
                  par((1, LANES)), par((1, LANES)), par((1, DI)), par((LANES, DI)), par((CH, CH))] + cm.in_specs(),
        out_specs=[pl.BlockSpec((CH, DI), lambda c: (c, 0)),
                   pl.BlockSpec((1, NG, NS, GW), lambda c: (c, 0, 0, 0))] + cm.out_specs(),
        out_shape=[jax.ShapeDtypeStruct((S, DI), bf16), jax.ShapeDtypeStruct((nc, NG, NS, GW), bf16)] + cm.out_shape(),
        scratch_shapes=[pltpu.VMEM((NG, NS, GW), f32), pltpu.VMEM((CH, DI), f32)] + cm.scratch(),
        compiler_params=_cp("arbitrary", side_effects=bool(cm.n)),
    )(xbc, dtr, bias, alog, dx_row, expand, tril, *cm.bufs)
    return res[0], res[1], res[2:]


def ssd_bwd(xbc, dtr, dy, hprev, bias, alog, dx_row, comm=None):
    S = xbc.shape[0]
    nc = S // CH
    expand, expand_t, tril, triu = _ssd_consts()
    cm = _Comm(comm)

    def body(*refs):
        ins, outs, scr, copies = cm.split(refs, 11, 3)
        xbc_ref, dtr_ref, dy_ref, hp_ref, bias_ref, alog_ref, dxr_ref, expand_ref, expt_ref, tril_ref, triu_ref = ins
        dxbc_ref, ddtr_ref, acc_ref = outs
        dh_ref, dxs_ref, t_ref, accb_ref, acca_ref, accd_ref = scr
        c = pl.program_id(0)
        cm.start(copies, c == 0)

        @pl.when(c == 0)
        def _():
            dh_ref[...] = jnp.zeros_like(dh_ref)
            accb_ref[...] = jnp.zeros_like(accb_ref)
            acca_ref[...] = jnp.zeros_like(acca_ref)
            accd_ref[...] = jnp.zeros_like(accd_ref)

        dtr, dt, a, acs, dtx, x, xdt, e, dsx = _ssd_common(xbc_ref, dtr_ref, bias_ref, alog_ref, expand_ref, tril_ref)
        acs_t = acs.T
        xb = xdt.astype(bf16)
        xdf = xdt * dsx
        xd = xdf.astype(bf16)
        dyv = dy_ref[...].astype(f32)
        dyb = dy_ref[...]
        dye = (dyv * e).astype(bf16)
        causal = tril_ref[...] > 0
        lane = lax.broadcasted_iota(jnp.int32, (CH, LANES), 1)
        subl = lax.broadcasted_iota(jnp.int32, (LANES, CH), 0)
        ccol = jnp.zeros((CH, LANES), f32)
        rrow = jnp.zeros((LANES, CH), f32)
        last_row = lax.broadcasted_iota(jnp.int32, (CH, 1), 0) == CH - 1
        for g in range(NG):
            gs = slice(g * GW, (g + 1) * GW)
            bsl = slice(DI + g * NS, DI + (g + 1) * NS)
            csl = slice(DI + NG * NS + g * NS, DI + NG * NS + (g + 1) * NS)
            bg = xbc_ref[:, bsl]
            cg = xbc_ref[:, csl]
            cb = _dot(cg, bg, "nt")
            hpb = hp_ref[0, g]
            dhn = dh_ref[g]
            dhnb = dhn.astype(bf16)
            yoff = _dot(cg, hpb) * e[:, gs]
            dxd = _dot(bg, dhnb)
            t2 = dxd * xdf[:, gs]
            t3 = jnp.sum(dhn * hpb.astype(f32), axis=0, keepdims=True) * e[CH - 1:CH, gs]
            t_ref[:, gs] = dyv[:, gs] * yoff - t2 + jnp.where(last_row, jnp.sum(t2, axis=0, keepdims=True) + t3, 0.0)
            dxs_ref[:, gs] = dxd * dsx[:, gs]
            dcg = _dot(dye[:, gs], hpb, "nt")
            dbg = _dot(xd[:, gs], dhnb, "nt")
            dh_ref[g] = dhn * e[CH - 1:CH, gs] + _dot(cg, dye[:, gs], "tn")
            dcb = jnp.zeros((CH, CH), f32)
            for j in range(NH // NG):
                hh = g * (NH // NG) + j
                hs = slice(hh * HD, (hh + 1) * HD)
                lm = _ssd_lmat(acs, acs_t, hh, causal)
                m = cb * lm
                dm = _dot(dyb[:, hs], xb[:, hs], "nt")
                gm = dm * m
                ccol = ccol + jnp.sum(gm, axis=1, keepdims=True) * (lane == hh).astype(f32)
                rrow = rrow + jnp.sum(gm, axis=0, keepdims=True) * (subl == hh).astype(f32)
                dcb = dcb + dm * lm
                dxs_ref[:, hs] += _dot(m.astype(bf16), dyb[:, hs], "tn")
            dcbb = dcb.astype(bf16)
            dxbc_ref[:, csl] = (dcg + _dot(dcbb, bg)).astype(bf16)
            dxbc_ref[:, bsl] = (dbg + _dot(dcbb, cg, "tn")).astype(bf16)
        dxf = dxs_ref[...]
        dxbc_ref[:, 0:DI] = (dxf * dtx + dxr_ref[...] * dyv).astype(bf16)
        expt = expt_ref[...]
        dacs = ccol - rrow.T + _dot2_right(t_ref[...], expt)
        dadt = _dot3_left(triu_ref[...], dacs)
        ddt = _dot2_right(dxf * x, expt) + dadt * a
        ddtr = ddt * _sigmoid(dtr)
        ddtr_ref[...] = ddtr
        accb_ref[...] += ddtr
        acca_ref[...] += dadt * dt
        accd_ref[...] += _dot2_right(dyv * x, expt)

        @pl.when(c == nc - 1)
        def _():
            acc_ref[...] = jnp.zeros_like(acc_ref)
            acc_ref[0:1, :] = jnp.sum(accb_ref[...], axis=0, keepdims=True)
            acc_ref[1:2, :] = jnp.sum(acca_ref[...], axis=0, keepdims=True) * a
            acc_ref[2:3, :] = jnp.sum(accd_ref[...], axis=0, keepdims=True)

        cm.wait(copies, c == nc - 1)

    par = lambda shape: pl.BlockSpec(shape, lambda c: (0,) * len(shape))
    rev = lambda c: (nc - 1 - c, 0)
    res = pl.pallas_call(
        body, name="ssd_bwd", grid=(nc,),
        in_specs=[pl.BlockSpec((CH, CONVD), rev), pl.BlockSpec((CH, LANES), rev), pl.BlockSpec((CH, DI), rev),
                  pl.BlockSpec((1, NG, NS, GW), lambda c: (nc - 1 - c, 0, 0, 0)),
                  par((1, LANES)), par((1, LANES)), par((1, DI)), par((LANES, DI)), par((DI, LANES)),
                  par((CH, CH)), par((CH, CH))] + cm.in_specs(),
        out_specs=[pl.BlockSpec((CH, CONVD), rev), pl.BlockSpec((CH, LANES), rev), par((8, LANES))] + cm.out_specs(),
        out_shape=[jax.ShapeDtypeStruct((S, CONVD), bf16), jax.ShapeDtypeStruct((S, LANES), f32),
                   jax.ShapeDtypeStruct((8, LANES), f32)] + cm.out_shape(),
        scratch_shapes=[pltpu.VMEM((NG, NS, GW), f32), pltpu.VMEM((CH, DI), f32), pltpu.VMEM((CH, DI), f32),
                        pltpu.VMEM((CH, LANES), f32), pltpu.VMEM((CH, LANES), f32),
                        pltpu.VMEM((CH, LANES), f32)] + cm.scratch(),
        compiler_params=_cp("arbitrary", side_effects=bool(cm.n)),
    )(xbc, dtr, dy, hprev, bias, alog, dx_row, expand, expand_t, tril, triu, *cm.bufs)
    return res[0], res[1], res[2], res[3:]


def rope_tables(pos_row, ts):
    S = pos_row.shape[1]
    half = AD // 2
    inv = ROPE_THETA ** (-jnp.arange(half, dtype=f32) * 2.0 / AD)
    inv_col = jnp.tile(inv, 2)[:, None]

    def body(p_ref, inv_ref, cos_ref, sin_ref):
        ang = inv_ref[...] * p_ref[...].astype(f32)
        row = lax.broadcasted_iota(jnp.int32, ang.shape, 0)
        cos_ref[...] = jnp.cos(ang)
        sin_ref[...] = jnp.where(row < half, -1.0, 1.0) * jnp.sin(ang)

    o = jax.ShapeDtypeStruct((AD, S), f32)
    return pl.pallas_call(
        body, name="rope_tables", grid=(S // ts,),
        in_specs=[pl.BlockSpec((1, ts), lambda i: (0, i)), pl.BlockSpec((AD, 1), lambda i: (0, 0))],
        out_specs=[pl.BlockSpec((AD, ts), lambda i: (0, i))] * 2, out_shape=[o, o],
        compiler_params=_cp("parallel"),
    )(pos_row, inv_col)


def _partner(t):
    half = AD // 2
    return jnp.concatenate([t[h * AD + o:h * AD + o + half] for h in range(t.shape[0] // AD) for o in (half, 0)], axis=0)


def _rope(t, cos, sin):
    reps = t.shape[0] // AD
    return t * jnp.tile(cos, (reps, 1)) + _partner(t) * jnp.tile(sin, (reps, 1))


def _rope_t(d, cos, sin):
    reps = d.shape[0] // AD
    return d * jnp.tile(cos, (reps, 1)) - _partner(d) * jnp.tile(sin, (reps, 1))


def _lanes_of_group(t, g):
    return jnp.concatenate([t[(g * REP + r) * AD:(g * REP + r + 1) * AD] for r in range(REP)], axis=1)


def _attn_probs(qg, kp, kc, sink_ref, g, not_first):
    n = qg.shape[1]
    s = lax.broadcasted_iota(jnp.int32, (WIN, n), 0)
    t = lax.broadcasted_iota(jnp.int32, (WIN, n), 1) % WIN
    neg = -1e30
    sink = jnp.concatenate([jnp.broadcast_to(sink_ref[0:1, g * REP + r:g * REP + r + 1], (1, WIN)) for r in range(REP)],
                           axis=1)
    sp = jnp.where(jnp.logical_and(s > t, not_first), _dot(kp, qg, "tn"), neg)
    sc = jnp.where(s <= t, _dot(kc, qg, "tn"), neg)
    m = jnp.maximum(jnp.maximum(jnp.max(sp, axis=0, keepdims=True), jnp.max(sc, axis=0, keepdims=True)), sink)
    pp = jnp.exp(sp - m)
    pc = jnp.exp(sc - m)
    ps = jnp.exp(sink - m)
    inv = 1.0 / (jnp.sum(pp, axis=0, keepdims=True) + jnp.sum(pc, axis=0, keepdims=True) + ps)
    return pp * inv, pc * inv, ps * inv


def attn_fwd(qt, kvt, cos, sin, sinks):
    S = qt.shape[1]
    nb = S // WIN
    cur = lambda i: (0, i)
    prev = lambda i: (0, jnp.maximum(i - 1, 0))

    def body(q_ref, kv_ref, kvp_ref, cos_ref, sin_ref, cosp_ref, sinp_ref, sink_ref, o_ref):
        i = pl.program_id(0)
        q = (_rope(q_ref[...].astype(f32), cos_ref[...], sin_ref[...]) * (AD ** -0.5)).astype(bf16)
        kc = _rope(kv_ref[0:KVW, :].astype(f32), cos_ref[...], sin_ref[...]).astype(bf16)
        kp = _rope(kvp_ref[0:KVW, :].astype(f32), cosp_ref[...], sinp_ref[...]).astype(bf16)
        for g in range(KVH):
            ks = slice(g * AD, (g + 1) * AD)
            vs = slice(KVW + g * AD, KVW + (g + 1) * AD)
            pp, pc, _ = _attn_probs(_lanes_of_group(q, g), kp[ks], kc[ks], sink_ref, g, i > 0)
            o = _dot(kvp_ref[vs, :], pp.astype(bf16)) + _dot(kv_ref[vs, :], pc.astype(bf16))
            for r in range(REP):
                h = g * REP + r
                o_ref[h * AD:(h + 1) * AD, :] = o[:, r * WIN:(r + 1) * WIN].astype(bf16)

    tab = pl.BlockSpec((AD, WIN), cur)
    tabp = pl.BlockSpec((AD, WIN), prev)
    return pl.pallas_call(
        body, name="attn_fwd", grid=(nb,),
        in_specs=[pl.BlockSpec((D, WIN), cur), pl.BlockSpec((2 * KVW, WIN), cur), pl.BlockSpec((2 * KVW, WIN), prev),
                  tab, tab, tabp, tabp, pl.BlockSpec((1, LANES), lambda i: (0, 0))],
        out_specs=pl.BlockSpec((D, WIN), cur),
        out_shape=jax.ShapeDtypeStruct((D, S), bf16),
        compiler_params=_cp("parallel"),
    )(qt, kvt, kvt, cos, sin, cos, sin, sinks)


def attn_bwd(qt, kvt, cos, sin, sinks, daot, comm=None):
    S = qt.shape[1]
    nb = S // WIN
    cur = lambda i: (0, jnp.minimum(i, nb - 1))
    prev = lambda i: (0, jnp.maximum(i - 1, 0))
    cm = _Comm(comm)

    def body(*refs):
        ins, (dq_ref, dkv_ref, ds_ref), scr, copies = cm.split(refs, 9, 3)
        q_ref, kv_ref, kvp_ref, cos_ref, sin_ref, cosp_ref, sinp_ref, sink_ref, do_ref = ins
        ck_ref, cv_ref, dqs_ref, dkp_ref, dvp_ref, dkc_ref, dvc_ref, accs_ref = scr
        i = pl.program_id(0)
        cm.start(copies, i == 0)

        @pl.when(i == 0)
        def _():
            ck_ref[...] = jnp.zeros_like(ck_ref)
            cv_ref[...] = jnp.zeros_like(cv_ref)
            accs_ref[...] = jnp.zeros_like(accs_ref)

        @pl.when(i == nb)
        def _():
            dkp_ref[...] = jnp.zeros_like(dkp_ref)
            dvp_ref[...] = jnp.zeros_like(dvp_ref)

        @pl.when(i < nb)
        def _():
            q = (_rope(q_ref[...].astype(f32), cos_ref[...], sin_ref[...]) * (AD ** -0.5)).astype(bf16)
            kc = _rope(kv_ref[0:KVW, :].astype(f32), cos_ref[...], sin_ref[...]).astype(bf16)
            kp = _rope(kvp_ref[0:KVW, :].astype(f32), cosp_ref[...], sinp_ref[...]).astype(bf16)
            do = do_ref[...]
            for g in range(KVH):
                ks = slice(g * AD, (g + 1) * AD)
                vs = slice(KVW + g * AD, KVW + (g + 1) * AD)
                qg = _lanes_of_group(q, g)
                dog = _lanes_of_group(do, g)
                pp, pc, ps = _attn_probs(qg, kp[ks], kc[ks], sink_ref, g, i > 0)
                dpp = _dot(kvp_ref[vs, :], dog, "tn")
                dpc = _dot(kv_ref[vs, :], dog, "tn")
                delta = jnp.sum(pp * dpp + pc * dpc, axis=0, keepdims=True)
                dsp = (pp * (dpp - delta)).astype(bf16)
                dsc = (pc * (dpc - delta)).astype(bf16)
                accs_ref[g:g + 1, :] -= ps * delta
                dqg = (_dot(kp[ks], dsp) + _dot(kc[ks], dsc)) * (AD ** -0.5)
                for r in range(REP):
                    h = g * REP + r
                    dqs_ref[h * AD:(h + 1) * AD, :] = dqg[:, r * WIN:(r + 1) * WIN]
                dkp_ref[ks, :] = _dot(qg, dsp, "nt")
                dkc_ref[ks, :] = _dot(qg, dsc, "nt")
                dvp_ref[ks, :] = _dot(dog, pp.astype(bf16), "nt")
                dvc_ref[ks, :] = _dot(dog, pc.astype(bf16), "nt")
            dq_ref[...] = _rope_t(dqs_ref[...], cos_ref[...], sin_ref[...]).astype(bf16)

        dkv_ref[0:KVW, :] = _rope_t(ck_ref[...] + dkp_ref[...], cosp_ref[...], sinp_ref[...]).astype(bf16)
        dkv_ref[KVW:2 * KVW, :] = (cv_ref[...] + dvp_ref[...]).astype(bf16)

        @pl.when(i < nb)
        def _():
            ck_ref[...] = dkc_ref[...]
            cv_ref[...] = dvc_ref[...]

        @pl.when(i == nb)
        def _():
            lane = lax.broadcasted_iota(jnp.int32, (1, LANES), 1)
            row = jnp.zeros((1, LANES), f32)
            for h in range(AH):
                part = accs_ref[h // REP:h // REP + 1, (h % REP) * WIN:(h % REP + 1) * WIN]
                row = row + jnp.where(lane == h, jnp.sum(part, axis=1, keepdims=True), 0.0)
            ds_ref[...] = jnp.zeros_like(ds_ref)
            ds_ref[0:1, :] = row

        cm.wait(copies, i == nb)

    tab = pl.BlockSpec((AD, WIN), cur)
    tabp = pl.BlockSpec((AD, WIN), prev)
    kvs = lambda: pltpu.VMEM((KVW, WIN), f32)
    res = pl.pallas_call(
        body, name="attn_bwd", grid=(nb + 1,),
        in_specs=[pl.BlockSpec((D, WIN), cur), pl.BlockSpec((2 * KVW, WIN), cur), pl.BlockSpec((2 * KVW, WIN), prev),
                  tab, tab, tabp, tabp, pl.BlockSpec((1, LANES), lambda i: (0, 0)),
                  pl.BlockSpec((D, WIN), cur)] + cm.in_specs(),
        out_specs=[pl.BlockSpec((D, WIN), cur), pl.BlockSpec((2 * KVW, WIN), prev),
                   pl.BlockSpec((8, LANES), lambda i: (0, 0))] + cm.out_specs(),
        out_shape=[jax.ShapeDtypeStruct((D, S), bf16), jax.ShapeDtypeStruct((2 * KVW, S), bf16),
                   jax.ShapeDtypeStruct((8, LANES), f32)] + cm.out_shape(),
        scratch_shapes=[kvs(), kvs(), pltpu.VMEM((D, WIN), f32), kvs(), kvs(), kvs(), kvs(),
                        pltpu.VMEM((8, REP * WIN), f32)] + cm.scratch(),
        compiler_params=_cp("arbitrary", side_effects=bool(cm.n)),
    )(qt, kvt, kvt, cos, sin, cos, sin, sinks, daot, *cm.bufs)
    return res[0], res[1], res[2], res[3:]


ADAM_C1 = 1.0 / (1.0 - ADAM_B1 ** ADAM_STEP)
ADAM_C2 = 1.0 / (1.0 - ADAM_B2 ** ADAM_STEP)


def _adam_update(g, w, m, v):
    nm = ADAM_B1 * m + (1.0 - ADAM_B1) * g
    nv = ADAM_B2 * v + (1.0 - ADAM_B2) * (g * g)
    return -ADAM_LR * ((nm * ADAM_C1) / (jnp.sqrt(nv * ADAM_C2) + ADAM_EPS) + ADAM_WD * w), nm, nv


def adamw(parts, w, m, v, tr, name):
    n, R, C = parts.shape

    def body(p_ref, w_ref, m_ref, v_ref, g_ref, d_ref, nm_ref, nv_ref):
        def grp(g0, _):
            r0 = pl.multiple_of(g0 * RG, RG)
            rows = pl.ds(r0, RG)
            g = p_ref[0, rows, :].astype(f32)
            for k in range(1, n):
                g = g + p_ref[k, rows, :].astype(f32)
            d, nm, nv = _adam_update(g, w_ref[rows, :], m_ref[rows, :], v_ref[rows, :])
            g_ref[rows, :] = g
            d_ref[rows, :] = d
            nm_ref[rows, :] = nm
            nv_ref[rows, :] = nv
            return 0

        lax.fori_loop(0, tr // RG, grp, 0)

    row = pl.BlockSpec((tr, C), lambda i: (i, 0))
    o = jax.ShapeDtypeStruct((R, C), f32)
    return pl.pallas_call(
        body, name=name, grid=(R // tr,),
        in_specs=[pl.BlockSpec((n, tr, C), lambda i: (0, i, 0)), row, row, row],
        out_specs=[row, row, row, row], out_shape=[o, o, o, o],
        compiler_params=_cp("parallel"),
    )(parts, w, m, v)


SMALL_ROW = (("norm_mix_post_w", D), ("norm_ffn_pre_w", D), ("norm_ffn_post_w", D), ("ssd_norm_w", DI),
             ("ssd_conv_b", CONVD), ("ffn_conv_b", 2 * FF), ("ssd_dt_bias", NH), ("ssd_a_log", NH), ("ssd_d", NH),
             ("attn_sinks", AH), ("loss", 1))
CONV_BLOCK = 1152
SSD_CONV_COLS = CONVD // N_DEV
FFN_CONV_COLS = 2 * FF // N_DEV


def _row_offsets():
    off, o = {}, 0
    for name, n in SMALL_ROW:
        off[name] = (o, n)
        o += -(-n // LANES) * LANES
    return off, o


def adamw_small(recv_row, recv_pre, recv_conv, params):
    off, _ = _row_offsets()
    names = list(params)
    n = len(names)

    def total(ref, rows, lo, width):
        g = ref[0, rows, lo:lo + width]
        for d in range(1, N_DEV):
            g = g + ref[d, rows, lo:lo + width]
        return g

    def grad_of(name, row_ref, pre_ref, conv_ref):
        if name == "norm_mix_pre_w":
            return total(pre_ref, slice(0, 1), 0, D)
        if name == "ssd_conv_w":
            return total(conv_ref, slice(0, SSD_K), 0, SSD_CONV_COLS)
        if name == "ffn_conv_w":
            return total(conv_ref, slice(0, FFN_K), 3 * LANES, FFN_CONV_COLS)
        o, width = off[name]
        return total(row_ref, slice(0, 1), o, width)

    def body(row_ref, pre_ref, conv_ref, *refs):
        ins, outs = refs[:3 * n], refs[3 * n:]
        for k, name in enumerate(names):
            w_ref, m_ref, v_ref = ins[3 * k:3 * k + 3]
            g_ref, d_ref, nm_ref, nv_ref = outs[4 * k:4 * k + 4]
            g = grad_of(name, row_ref, pre_ref, conv_ref)
            d, nm, nv = _adam_update(g, w_ref[...], m_ref[...], v_ref[...])
            g_ref[...] = g
            d_ref[...] = d
            nm_ref[...] = nm
            nv_ref[...] = nv
        outs[4 * n][...] = total(row_ref, slice(0, 1), off["loss"][0], LANES)

    flat = [t for name in names for t in params[name]]
    out_shape = [jax.ShapeDtypeStruct(params[name][0].shape, f32) for name in names for _ in range(4)]
    res = pl.pallas_call(
        body, name="adamw_small",
        out_shape=out_shape + [jax.ShapeDtypeStruct((1, LANES), f32)],
        compiler_params=pltpu.CompilerParams(vmem_limit_bytes=VMEM_LIMIT),
    )(recv_row, recv_pre, recv_conv, *flat)
    return {name: res[4 * k:4 * k + 4] for k, name in enumerate(names)}, res[4 * n]


def _pad_rows8(w):
    return jnp.pad(w, ((0, 8 - w.shape[0]), (0, 0)))


def _pad_lanes(v):
    return jnp.pad(v.reshape(1, -1), ((0, 0), (0, LANES - v.size)))


WEIGHTS = ('norm_mix_pre_w', 'w_in', 'ssd_conv_w', 'ssd_conv_b', 'ssd_dt_bias', 'ssd_a_log', 'ssd_d', 'ssd_norm_w',
           'ssd_w_out', 'attn_sinks', 'attn_w_out', 'w_mix_out', 'norm_mix_post_w', 'norm_ffn_pre_w', 'ffn_w_up',
           'ffn_conv_w', 'ffn_conv_b', 'ffn_w_down', 'norm_ffn_post_w')
W_IN_ROWS = IN_DIM // N_DEV
W_IN_PAD = 1104
TS = 512


def kernel(x, positions, norm_mix_pre_w, w_in, ssd_conv_w, ssd_conv_b, ssd_dt_bias, ssd_a_log, ssd_d, ssd_norm_w, ssd_w_out, attn_sinks, attn_w_out, w_mix_out, norm_mix_post_w, norm_ffn_pre_w, ffn_w_up, ffn_conv_w, ffn_conv_b, ffn_w_down, norm_ffn_post_w, loss_target, m_norm_mix_pre_w, m_w_in, m_ssd_conv_w, m_ssd_conv_b, m_ssd_dt_bias, m_ssd_a_log, m_ssd_d, m_ssd_norm_w, m_ssd_w_out, m_attn_sinks, m_attn_w_out, m_w_mix_out, m_norm_mix_post_w, m_norm_ffn_pre_w, m_ffn_w_up, m_ffn_conv_w, m_ffn_conv_b, m_ffn_w_down, m_norm_ffn_post_w, v_norm_mix_pre_w, v_w_in, v_ssd_conv_w, v_ssd_conv_b, v_ssd_dt_bias, v_ssd_a_log, v_ssd_d, v_ssd_norm_w, v_ssd_w_out, v_attn_sinks, v_attn_w_out, v_w_mix_out, v_norm_mix_post_w, v_norm_ffn_pre_w, v_ffn_w_up, v_ffn_conv_w, v_ffn_conv_b, v_ffn_w_down, v_norm_ffn_post_w):
    a = locals()
    r2 = lambda t: t.reshape(t.shape[-2], t.shape[-1])
    w = {n: r2(a[n]) for n in WEIGHTS}
    m = {n: r2(a["m_" + n]) for n in WEIGHTS}
    v = {n: r2(a["v_" + n]) for n in WEIGHTS}
    xs, target = x[0], loss_target[0]
    S = xs.shape[0]
    ts = TS

    w_in_blk = jnp.pad(w["w_in"].T.astype(bf16), ((0, W_IN_PAD - W_IN_ROWS), (0, 0)))
    conv_blk = jnp.concatenate([_pad_rows8(w["ssd_conv_w"]), _pad_rows8(w["ffn_conv_w"]),
                                jnp.zeros((8, CONV_BLOCK - SSD_CONV_COLS - FFN_CONV_COLS), f32)], axis=1)
    g_in, g_conv = gather_two_level([w_in_blk, conv_blk], "gather_first")
    wt = g_in[:, :W_IN_ROWS].reshape(IN_DIM, D)
    w_main_t = jnp.concatenate([wt[IN_OFF[0]:IN_OFF[1]], wt[IN_OFF[6]:IN_OFF[8]], wt[IN_OFF[1]:IN_OFF[2]]], axis=0)
    w_q_t = wt[IN_OFF[3]:IN_OFF[4]]
    w_kv_t = wt[IN_OFF[4]:IN_OFF[6]]
    w_dt_t = jnp.pad(wt[IN_OFF[2]:IN_OFF[3]], ((0, LANES - NH), (0, 0)))
    conv_w8 = g_conv[:, :, 0:SSD_CONV_COLS].transpose(1, 0, 2).reshape(8, CONVD)
    fconv_w8 = g_conv[:, :, SSD_CONV_COLS:SSD_CONV_COLS + FFN_CONV_COLS].transpose(1, 0, 2).reshape(8, 2 * FF)
    bias = _pad_lanes(w["ssd_dt_bias"])
    alog = _pad_lanes(w["ssd_a_log"])
    dx_row = jnp.repeat(w["ssd_d"].reshape(-1), HD).reshape(1, DI)
    sinks = _pad_lanes(w["attn_sinks"])

    u = prenorm_fwd(xs, w["norm_mix_pre_w"], ts)
    later = [w["ssd_w_out"].astype(bf16), w["attn_w_out"].astype(bf16), w["w_mix_out"].astype(bf16)]
    proj, (g_so, g_ao, g_mix) = mm(u, w_main_t, "nt", bf16, "mm_proj", comm=(later, (False,) * 3))
    w_ssd_out, w_attn_out, w_mix = g_so.reshape(DI, D), g_ao.reshape(D, D), g_mix.reshape(D, D)
    qt = mm(w_q_t, u, "nt", bf16, "mm_q")
    kvt = mm(w_kv_t, u, "nt", bf16, "mm_kv")
    dtr = mm(u, w_dt_t, "nt", f32, "mm_dt")
    xbc, conv_c = ssdconv_fwd(proj, conv_w8, w["ssd_conv_b"], ts)
    y, hprev, (g_up, g_down) = ssd_fwd(xbc, dtr, bias, alog, dx_row, comm=(
        [w["ffn_w_up"].T.astype(bf16), w["ffn_w_down"].astype(bf16)], (False, False)))
    w_up_t = g_up.reshape(2 * FF, D)
    w_down = g_down.reshape(FF, D)
    yn = gnorm_fwd(y, proj, w["ssd_norm_w"], ts)
    ys = mm(yn, w_ssd_out, "nn", bf16, "mm_ssd_out")
    cos, sin = rope_tables(positions, ts)
    aot = attn_fwd(qt, kvt, cos, sin, sinks)
    ya = mm(aot, w_attn_out, "tn", bf16, "mm_attn_out")
    merged = merge_fwd(proj, ys, ya, ts)
    mo = mm(merged, w_mix, "nn", bf16, "mm_mix")
    x1, h = post_fwd(xs, mo, w["norm_mix_post_w"], w["norm_ffn_pre_w"], ts)
    up = mm(h, w_up_t, "nt", bf16, "mm_up")
    act, gate_c, val_c = ffnact_fwd(up, fconv_w8, w["ffn_conv_b"], ts)
    ff = mm(act, w_down, "nn", bf16, "mm_down")
    loss_blk, dout, dff, g_post2 = loss_head(x1, ff, target, w["norm_ffn_post_w"], ts)

    dact = mm(dff, w_down, "nt", bf16, "mm_dact")
    gw_down = mm(act, dff, "tn", bf16, "mm_g_down")
    dgate, dval = ffnact_bwd(dact, gate_c, val_c, ts)
    dup_pre, g_fconv_a = dwconv_bwd(dgate, up, 0, fconv_w8, 0, FFN_K, FF, ts, "ffnconv_bwd_gate", out_cols=2 * FF)
    dup_pre, g_fconv_b = dwconv_bwd(dval, up, 1, fconv_w8, 1, FFN_K, FF, ts, "ffnconv_bwd_val", into=dup_pre, ocb=1,
                                    out_cols=2 * FF)
    g_fconv = jnp.concatenate([g_fconv_a, g_fconv_b], axis=1)
    dh, (r_down,) = mm(dup_pre, w_up_t, "nn", bf16, "mm_dh", comm=([gw_down.reshape(N_DEV, FF // N_DEV, D)], (True,)))
    gw_up_t = mm(dup_pre, h, "tn", bf16, "mm_g_up")
    dx1, dmo, g_norms = post_bwd(dout, dh, x1, mo, w["norm_mix_post_w"], w["norm_ffn_pre_w"], ts)
    dmerged = mm(dmo, w_mix, "nt", bf16, "mm_dmerged")
    gw_mix = mm(merged, dmo, "tn", bf16, "mm_g_mix")
    dys, dya, dproj = merge_bwd(dmerged, proj, ys, ya, ts)
    daot = mm(w_attn_out, dya, "nt", bf16, "mm_dao")
    gw_attn_out = mm(aot, dya, "nn", bf16, "mm_g_attn_out")
    dqt, dkvt, g_sinks, (r_up,) = attn_bwd(qt, kvt, cos, sin, sinks, daot,
                                           comm=([gw_up_t.reshape(N_DEV, 2 * FF // N_DEV, D)], (True,)))
    du_b = mm(dkvt, w_kv_t, "tn", bf16, "mm_du_kv")
    du_d = mm(dqt, w_q_t, "tn", bf16, "mm_du_q")
    dyn = mm(dys, w_ssd_out, "nt", bf16, "mm_dyn")
    gw_ssd_out = mm(yn, dys, "tn", bf16, "mm_g_ssd_out")
    dy, dproj, g_gnorm = gnorm_bwd(dyn, y, proj, w["ssd_norm_w"], dproj, ts)
    sends = [gw_ssd_out.reshape(N_DEV, DI // N_DEV, D), gw_attn_out.reshape(N_DEV, D // N_DEV, D),
             gw_mix.reshape(N_DEV, D // N_DEV, D)]
    dxbc, ddtr, g_ssd, (r_so, r_ao, r_mix) = ssd_bwd(xbc, dtr, dy, hprev, bias, alog, dx_row, comm=(sends, (True,) * 3))
    dproj, g_conv_w = dwconv_bwd(dxbc, proj, C_XBC // 1024, conv_w8, 0, SSD_K, 1024, ts, "ssdconv_bwd", act_c=conv_c,
                                 into=dproj, ocb=C_XBC // 1024, out_cols=PM)
    ddtr_b = ddtr.astype(bf16)
    du_c = mm(ddtr_b, w_dt_t, "nn", bf16, "mm_du_dt")
    g_main_t = mm(dproj, u, "tn", bf16, "mm_g_in")
    g_q_t = mm(dqt, u, "nn", bf16, "mm_g_q")
    g_kv_t = mm(dkvt, u, "nn", bf16, "mm_g_kv")
    g_dt_t = mm(ddtr_b, u, "tn", bf16, "mm_g_dt")
    g_wt = jnp.concatenate([g_main_t[C_Z:C_GS], g_main_t[C_XBC:PM], g_dt_t[:NH], g_q_t, g_kv_t, g_main_t[C_GS:C_XBC]],
                           axis=0)
    send_in = jnp.pad(g_wt.reshape(N_DEV, W_IN_ROWS, D), ((0, 0), (0, W_IN_PAD - W_IN_ROWS), (0, 0)))
    pieces = {"norm_mix_post_w": g_norms[1:2], "norm_ffn_pre_w": g_norms[0:1], "norm_ffn_post_w": g_post2[0:1],
              "ssd_norm_w": g_gnorm[0:1], "ssd_conv_b": g_conv_w[7:8], "ffn_conv_b": g_fconv[7:8],
              "ssd_dt_bias": g_ssd[0:1], "ssd_a_log": g_ssd[1:2], "ssd_d": g_ssd[2:3], "attn_sinks": g_sinks[0:1],
              "loss": loss_blk[0:1]}
    row = jnp.concatenate([jnp.pad(pieces[n][:, :min(k, pieces[n].shape[1])],
                                   ((0, 0), (0, -(-k // LANES) * LANES - min(k, pieces[n].shape[1]))))
                           for n, k in SMALL_ROW], axis=1)
    send_row = jnp.pad(row, ((0, 7), (0, 0)))
    send_conv = jnp.concatenate(
        [g_conv_w.reshape(8, N_DEV, SSD_CONV_COLS).transpose(1, 0, 2),
         g_fconv.reshape(8, N_DEV, FFN_CONV_COLS).transpose(1, 0, 2),
         jnp.zeros((N_DEV, 8, CONV_BLOCK - SSD_CONV_COLS - FFN_CONV_COLS), f32)], axis=2)
    du_a, (r_in, recv_row, recv_conv) = mm(dproj, w_main_t, "nn", bf16, "mm_du",
                                           comm=([send_in, send_row, send_conv], (True, False, True)))
    grad_x, g_pre, _ = prenorm_bwd(xs, w["norm_mix_pre_w"], (du_a, du_b, du_c, du_d), dx1, ts)
    (recv_pre,) = exchange([g_pre], (False,), "gather_last")

    tpad = lambda t: jnp.pad(t.T, ((0, W_IN_PAD - W_IN_ROWS), (0, 0)))
    o_in = [t[:W_IN_ROWS].T for t in adamw(r_in, tpad(w["w_in"]), tpad(m["w_in"]), tpad(v["w_in"]), 368, "adamw_w_in")]
    o_up = [t.T for t in adamw(r_up, w["ffn_w_up"].T, m["ffn_w_up"].T, v["ffn_w_up"].T, 352, "adamw_w_up")]
    big = {"w_in": o_in, "ffn_w_up": o_up,
           "ssd_w_out": adamw(r_so, w["ssd_w_out"], m["ssd_w_out"], v["ssd_w_out"], 256, "adamw_ssd_out"),
           "attn_w_out": adamw(r_ao, w["attn_w_out"], m["attn_w_out"], v["attn_w_out"], 128, "adamw_attn_out"),
           "w_mix_out": adamw(r_mix, w["w_mix_out"], m["w_mix_out"], v["w_mix_out"], 128, "adamw_mix"),
           "ffn_w_down": adamw(r_down, w["ffn_w_down"], m["ffn_w_down"], v["ffn_w_down"], 352, "adamw_down")}
    small_names = [n for n in WEIGHTS if n not in big]
    small, loss_row = adamw_small(recv_row, recv_pre, recv_conv, {n: (w[n], m[n], v[n]) for n in small_names})

    outs = [loss_row[0, 0], grad_x[None]]
    for k in range(4):
        for n in WEIGHTS:
            outs.append((big[n][k] if n in big else small[n][k]).reshape(a[n].shape))
    return tuple(outs)
```

```python
import jax
import jax.numpy as jnp
import numpy as np
from jax import lax
from jax.experimental import pallas as pl
from jax.experimental.pallas import tpu as pltpu

f32 = jnp.float32
bf16 = jnp.bfloat16

N_DEV = 8
D = 1024
DI = 2048
NH = 32
HD = 64
NG = 4
GW = DI // NG
NS = 128
CH = 128
CONVD = DI + 2 * NG * NS
SSD_K = 4
AH = 16
AD = 64
KVH = 4
REP = AH // KVH
KVW = KVH * AD
WIN = 128
FF = 2816
FFN_K = 3
EPS = 1e-6
ROPE_THETA = 10000.0
LANES = 128
RG = 16
CW = 256

C_Z, C_GS, C_GA, C_XBC, PM = 0, 2048, 3072, 4096, 7168
IN_SIZES = (DI, CONVD, NH, D, KVW, KVW, D, D)
IN_OFF = tuple(int(v) for v in np.cumsum((0,) + IN_SIZES))
IN_DIM = IN_OFF[-1]

ADAM_LR, ADAM_B1, ADAM_B2, ADAM_EPS, ADAM_WD, ADAM_STEP = 0.001, 0.9, 0.999, 1e-08, 0.01, 10

VMEM_LIMIT = 56 * 1024 * 1024


def _cp(*sem, side_effects=False):
    return pltpu.CompilerParams(dimension_semantics=sem, vmem_limit_bytes=VMEM_LIMIT, has_side_effects=side_effects)


def _dot(a, b, mode="nn"):
    dims = {"nn": (((1,), (0,)), ((), ())), "nt": (((1,), (1,)), ((), ())), "tn": (((0,), (0,)), ((), ()))}[mode]
    return lax.dot_general(a, b, dims, preferred_element_type=f32)


def _split3(v):
    hi = v.astype(bf16)
    r = v - hi.astype(f32)
    mid = r.astype(bf16)
    lo = (r - mid.astype(f32)).astype(bf16)
    return hi, mid, lo


def _dot3_left(m01, v):
    hi, mid, lo = _split3(v)
    return _dot(m01, hi) + _dot(m01, mid) + _dot(m01, lo)


def _dot3_right(v, m01):
    hi, mid, lo = _split3(v)
    return _dot(hi, m01) + _dot(mid, m01) + _dot(lo, m01)


def _dot2_right(v, m01):
    hi = v.astype(bf16)
    lo = (v - hi.astype(f32)).astype(bf16)
    return _dot(hi, m01) + _dot(lo, m01)


def _sigmoid(x):
    return 1.0 / (1.0 + jnp.exp(-x))


def _sigmoid_fast(x):
    return pl.reciprocal(1.0 + jnp.exp(-x), approx=True)


def _peer(k, x, y, c):
    return ((1 - x) if k & 4 else x, (1 - y) if k & 2 else y, (1 - c) if k & 1 else c)


def _xchg_copies(buf_refs, out_refs, send_sems, recv_sems, local_sems, personalised):
    x, y, c = lax.axis_index("x"), lax.axis_index("y"), lax.axis_index("c")
    me = 4 * x + 2 * y + c
    local, remote = [], []
    for b, (buf, out, pers) in enumerate(zip(buf_refs, out_refs, personalised)):
        local.append(pltpu.make_async_copy(buf.at[me] if pers else buf, out.at[me], local_sems.at[b]))
        for k in range(1, N_DEV):
            px, py, pc = _peer(k, x, y, c)
            s = b * (N_DEV - 1) + k - 1
            remote.append(pltpu.make_async_remote_copy(
                src_ref=buf.at[4 * px + 2 * py + pc] if pers else buf, dst_ref=out.at[me],
                send_sem=send_sems.at[s], recv_sem=recv_sems.at[s],
                device_id=(px, py, pc), device_id_type=pl.DeviceIdType.MESH))
    return local, remote


class _Comm:
    def __init__(self, comm):
        self.bufs, self.pers = comm if comm else ((), ())
        self.n = len(self.bufs)

    def in_specs(self):
        return [pl.BlockSpec(memory_space=pl.ANY)] * self.n

    out_specs = in_specs

    def out_shape(self):
        return [jax.ShapeDtypeStruct((N_DEV,) + tuple(b.shape[1:] if p else b.shape), b.dtype)
                for b, p in zip(self.bufs, self.pers)]

    def scratch(self):
        n = self.n
        return [pltpu.SemaphoreType.DMA((n * (N_DEV - 1),)), pltpu.SemaphoreType.DMA((n * (N_DEV - 1),)),
                pltpu.SemaphoreType.DMA((n,))] if n else []

    def split(self, refs, n_in, n_out):
        n = self.n
        ins, outs = refs[:n_in], refs[n_in + n:n_in + n + n_out]
        rest = refs[n_in + n + n_out + n:]
        if not n:
            return ins, outs, rest, None
        copies = _xchg_copies(refs[n_in:n_in + n], refs[n_in + n + n_out:n_in + n + n_out + n], *rest[-3:], self.pers)
        return ins, outs, rest[:-3], copies

    def start(self, copies, first):
        if copies:
            @pl.when(first)
            def _():
                for cp in copies[0] + copies[1]:
                    cp.start()

    def wait(self, copies, last):
        if copies:
            @pl.when(last)
            def _():
                for cp in copies[1]:
                    cp.wait_recv()
                for cp in copies[1]:
                    cp.wait_send()
                for cp in copies[0]:
                    cp.wait()


def exchange(bufs, personalised, name):
    cm = _Comm((bufs, personalised))

    def body(*refs):
        _, _, _, copies = cm.split(refs, 0, 0)
        cm.start(copies, True)
        cm.wait(copies, True)

    return pl.pallas_call(
        body, name=name, in_specs=cm.in_specs(), out_specs=cm.out_specs(), out_shape=cm.out_shape(),
        scratch_shapes=cm.scratch(), compiler_params=pltpu.CompilerParams(has_side_effects=True),
    )(*bufs)


class _TwoLevelGather:
    def __init__(self, bufs):
        self.bufs = list(bufs)
        self.n = len(self.bufs)

    def in_specs(self):
        return [pl.BlockSpec(memory_space=pl.ANY)] * self.n

    out_specs = in_specs

    def out_shape(self):
        return [jax.ShapeDtypeStruct((N_DEV,) + tuple(b.shape), b.dtype) for b in self.bufs]

    def scratch(self):
        per = N_DEV - 1
        return [pltpu.SemaphoreType.DMA((self.n * per,)), pltpu.SemaphoreType.DMA((self.n * per,)),
                pltpu.SemaphoreType.DMA((self.n,))]

    def bind(self, ins, outs, send_sems, recv_sems, local_sems):
        n, per = self.n, N_DEV - 1
        x, y, c = lax.axis_index("x"), lax.axis_index("y"), lax.axis_index("c")
        me, sibling = (x, y, c), (x, y, 1 - c)
        chips = [(1 - x, y), (x, 1 - y), (1 - x, 1 - y)]

        def copy(b, k, block, to, src=None):
            dst = outs[b].at[4 * block[0] + 2 * block[1] + block[2]]
            return pltpu.make_async_remote_copy(
                src_ref=dst if src is None else src, dst_ref=dst,
                send_sem=send_sems.at[b * per + k], recv_sem=recv_sems.at[b * per + k],
                device_id=to, device_id_type=pl.DeviceIdType.MESH)

        mine = [pltpu.make_async_copy(ins[b], outs[b].at[4 * x + 2 * y + c], local_sems.at[b]) for b in range(n)]
        first = []
        for b in range(n):
            first.append(copy(b, 0, me, sibling, src=ins[b]))
            first += [copy(b, 1 + j, me, (*chip, c), src=ins[b]) for j, chip in enumerate(chips)]

        def start():
            for cp in mine + first:
                cp.start()

        def finish():
            passed = []
            for j, chip in enumerate(chips):
                for b in range(n):
                    copy(b, 1 + j, (*chip, c), me).wait_recv()
                    passed.append(copy(b, 4 + j, (*chip, c), sibling))
                    passed[-1].start()
            for b in range(n):
                copy(b, 0, sibling, me).wait_recv()
                for j, chip in enumerate(chips):
                    copy(b, 4 + j, (*chip, 1 - c), me).wait_recv()
            for cp in first + passed:
                cp.wait_send()
            for cp in mine:
                cp.wait()

        return start, finish


MM_TILES = (3584, 2176, 2048, 1792, 1408, 1024, 512, 256, 128)
MM_VMEM_BUDGET = 40 * 1024 * 1024


def _mm_tiles(M, N, K, out_bytes):
    cm = [t for t in MM_TILES if M % t == 0]
    cn = [t for t in MM_TILES if N % t == 0]
    ck = [t for t in MM_TILES if K % t == 0]
    best = None
    for bm in cm[:2]:
        for bn in cn:
            for bk in ck:
                need = 4 * (bm * bk + bk * bn) + bm * bn * (4 + 2 * out_bytes)
                if need <= MM_VMEM_BUDGET:
                    score = (bm * bn * bk, bk)
                    if best is None or score > best[0]:
                        best = (score, (bm, bn, bk))
    return best[1]


def mm(a, b, mode, out_dtype, name, comm=None):
    if mode == "nn":
        (M, K), (_, N) = a.shape, b.shape
    elif mode == "nt":
        (M, K), (N, _) = a.shape, b.shape
    else:
        (K, M), (_, N) = a.shape, b.shape
    bm, bn, bk = _mm_tiles(M, N, K, jnp.dtype(out_dtype).itemsize)
    gm, gn, nk = M // bm, N // bn, K // bk
    cm = _Comm(comm)

    def body(*refs):
        (a_ref, b_ref), (o_ref,), scr, copies = cm.split(refs, 2, 1)
        i, j, k = pl.program_id(0), pl.program_id(1), pl.program_id(2)
        cm.start(copies, jnp.logical_and(jnp.logical_and(i == 0, j == 0), k == 0))
        p = _dot(a_ref[...], b_ref[...], mode)
        if nk == 1:
            o_ref[...] = p.astype(o_ref.dtype)
        else:
            acc_ref = scr[0]

            @pl.when(k == 0)
            def _():
                acc_ref[...] = p

            @pl.when(k > 0)
            def _():
                acc_ref[...] += p

            @pl.when(k == nk - 1)
            def _():
                o_ref[...] = acc_ref[...].astype(o_ref.dtype)

        cm.wait(copies, jnp.logical_and(jnp.logical_and(i == gm - 1, j == gn - 1), k == nk - 1))

    if mode == "nn":
        a_spec = pl.BlockSpec((bm, bk), lambda i, j, k: (i, k))
        b_spec = pl.BlockSpec((bk, bn), lambda i, j, k: (k, j))
    elif mode == "nt":
        a_spec = pl.BlockSpec((bm, bk), lambda i, j, k: (i, k))
        b_spec = pl.BlockSpec((bn, bk), lambda i, j, k: (j, k))
    else:
        a_spec = pl.BlockSpec((bk, bm), lambda i, j, k: (k, i))
        b_spec = pl.BlockSpec((bk, bn), lambda i, j, k: (k, j))
    sem = ("arbitrary",) * 3 if cm.n else ("parallel", "parallel", "arbitrary")
    res = pl.pallas_call(
        body, name=name, grid=(gm, gn, nk),
        in_specs=[a_spec, b_spec] + cm.in_specs(),
        out_specs=[pl.BlockSpec((bm, bn), lambda i, j, k: (i, j))] + cm.out_specs(),
        out_shape=[jax.ShapeDtypeStruct((M, N), out_dtype)] + cm.out_shape(),
        scratch_shapes=([pltpu.VMEM((bm, bn), f32)] if nk > 1 else []) + cm.scratch(),
        compiler_params=_cp(*sem, side_effects=bool(cm.n)),
    )(a, b, *cm.bufs)
    return (res[0], res[1:]) if cm.n else res[0]


def _groups(ts, fn, carry=None, reverse=False, unroll=4, rg=RG):
    n = ts // rg
    if n == 1:
        return fn(0, carry)
    unroll = min(unroll, n)
    span = rg * unroll

    def body(g, c):
        r0 = pl.multiple_of((n // unroll - 1 - g if reverse else g) * span, span)
        for u in (range(unroll - 1, -1, -1) if reverse else range(unroll)):
            c = fn(pl.multiple_of(r0 + u * rg, rg), c)
        return c

    return lax.fori_loop(0, n // unroll, body, carry)


def _rms(x):
    return lax.rsqrt(jnp.mean(x * x, axis=-1, keepdims=True) + EPS)


def _rms_bwd(x, r, dn):
    n = x * r
    return r * (dn - n * jnp.mean(dn * n, axis=-1, keepdims=True))


NRG = 256


def _fold(x):
    return jnp.sum(x.reshape(x.shape[0] // 8, 8, x.shape[1]), axis=0)


def _flush(acc_ref, out_ref, row):
    out_ref[row:row + 1, :] = jnp.sum(acc_ref[...], axis=0, keepdims=True)


def prenorm_fwd(x, w, ts, gather):
    S = x.shape[0]
    nt = S // ts
    tg = _TwoLevelGather(gather)
    n = tg.n

    def body(x_ref, w_ref, *refs):
        u_ref = refs[n]
        start, finish = tg.bind(refs[:n], refs[n + 1:2 * n + 1], *refs[2 * n + 1:])
        i = pl.program_id(0)
        pl.when(i == 0)(start)
        wv = w_ref[...]

        def grp(r0, _):
            xv = x_ref[pl.ds(r0, NRG), :]
            u_ref[pl.ds(r0, NRG), :] = (xv * _rms(xv) * wv).astype(bf16)

        _groups(ts, grp, rg=NRG)
        pl.when(i == nt - 1)(finish)

    res = pl.pallas_call(
        body, name="prenorm_fwd", grid=(nt,),
        in_specs=[pl.BlockSpec((ts, D), lambda i: (i, 0)), pl.BlockSpec((1, D), lambda i: (0, 0))] + tg.in_specs(),
        out_specs=[pl.BlockSpec((ts, D), lambda i: (i, 0))] + tg.out_specs(),
        out_shape=[jax.ShapeDtypeStruct((S, D), bf16)] + tg.out_shape(),
        scratch_shapes=tg.scratch(),
        compiler_params=_cp("arbitrary", side_effects=True),
    )(x, w, *tg.bufs)
    return res[0], res[1:]


def prenorm_bwd(x, w, dus, dx1, ts, comm=None):
    S = x.shape[0]
    nt = S // ts
    nd = len(dus)
    cm = _Comm(comm)

    def body(*refs):
        ins, (gx_ref, gw_ref), (acc_ref,), copies = cm.split(refs, nd + 3, 2)
        x_ref, w_ref = ins[:2]
        du_refs, dx1_ref = ins[2:2 + nd], ins[2 + nd]
        i = pl.program_id(0)
        cm.start(copies, i == 0)
        wv = w_ref[...]

        @pl.when(i == 0)
        def _():
            acc_ref[...] = jnp.zeros_like(acc_ref)
            gw_ref[...] = jnp.zeros_like(gw_ref)

        def grp(r0, _):
            rows = pl.ds(r0, NRG)
            xv = x_ref[rows, :]
            r = _rms(xv)
            du = du_refs[0][rows, :].astype(f32)
            for d_ref in du_refs[1:]:
                du = du + d_ref[rows, :].astype(f32)
            gx_ref[rows, :] = dx1_ref[rows, :] + _rms_bwd(xv, r, du * wv)
            acc_ref[...] += _fold(du * xv * r)

        _groups(ts, grp, rg=NRG)

        @pl.when(i == nt - 1)
        def _():
            _flush(acc_ref, gw_ref, 0)

        cm.wait(copies, i == nt - 1)

    row = pl.BlockSpec((ts, D), lambda i: (i, 0))
    res = pl.pallas_call(
        body, name="prenorm_bwd", grid=(nt,),
        in_specs=[row, pl.BlockSpec((1, D), lambda i: (0, 0))] + [row] * (nd + 1) + cm.in_specs(),
        out_specs=[row, pl.BlockSpec((8, D), lambda i: (0, 0))] + cm.out_specs(),
        out_shape=[jax.ShapeDtypeStruct((S, D), f32), jax.ShapeDtypeStruct((8, D), f32)] + cm.out_shape(),
        scratch_shapes=[pltpu.VMEM((8, D), f32)] + cm.scratch(),
        compiler_params=_cp("arbitrary", side_effects=bool(cm.n)),
    )(x, w, *dus, dx1, *cm.bufs)
    return res[0], res[1], res[2:]


def post_fwd(x, mo, w_post, w_pre2, ts):
    S = x.shape[0]

    def body(x_ref, mo_ref, wp_ref, w2_ref, x1_ref, h_ref):
        wp, w2 = wp_ref[...], w2_ref[...]

        def grp(r0, _):
            rows = pl.ds(r0, NRG)
            mv = mo_ref[rows, :].astype(f32)
            x1 = x_ref[rows, :] + mv * _rms(mv) * wp
            x1_ref[rows, :] = x1
            h_ref[rows, :] = (x1 * _rms(x1) * w2).astype(bf16)

        _groups(ts, grp, rg=NRG)

    row = pl.BlockSpec((ts, D), lambda i: (i, 0))
    par = pl.BlockSpec((1, D), lambda i: (0, 0))
    return pl.pallas_call(
        body, name="post_fwd", grid=(S // ts,),
        in_specs=[row, row, par, par], out_specs=[row, row],
        out_shape=[jax.ShapeDtypeStruct((S, D), f32), jax.ShapeDtypeStruct((S, D), bf16)],
        compiler_params=_cp("parallel"),
    )(x, mo, w_post, w_pre2)


def post_bwd(dout, dh, x1, mo, w_post, w_pre2, ts):
    S = x1.shape[0]
    nt = S // ts

    def body(dout_ref, dh_ref, x1_ref, mo_ref, wp_ref, w2_ref, dx1_ref, dmo_ref, gw_ref, acc2_ref, accp_ref):
        i = pl.program_id(0)
        wp, w2 = wp_ref[...], w2_ref[...]

        @pl.when(i == 0)
        def _():
            acc2_ref[...] = jnp.zeros_like(acc2_ref)
            accp_ref[...] = jnp.zeros_like(accp_ref)
            gw_ref[...] = jnp.zeros_like(gw_ref)

        def grp(r0, _):
            rows = pl.ds(r0, NRG)
            x1 = x1_ref[rows, :]
            r1 = _rms(x1)
            dh = dh_ref[rows, :].astype(f32)
            dx1 = dout_ref[rows, :] + _rms_bwd(x1, r1, dh * w2)
            dx1_ref[rows, :] = dx1
            acc2_ref[...] += _fold(dh * x1 * r1)
            mv = mo_ref[rows, :].astype(f32)
            rm = _rms(mv)
            dmo_ref[rows, :] = _rms_bwd(mv, rm, dx1 * wp).astype(bf16)
            accp_ref[...] += _fold(dx1 * mv * rm)

        _groups(ts, grp, rg=NRG)

        @pl.when(i == nt - 1)
        def _():
            _flush(acc2_ref, gw_ref, 0)
            _flush(accp_ref, gw_ref, 1)

    row = pl.BlockSpec((ts, D), lambda i: (i, 0))
    par = pl.BlockSpec((1, D), lambda i: (0, 0))
    return pl.pallas_call(
        body, name="post_bwd", grid=(nt,),
        in_specs=[row, row, row, row, par, par],
        out_specs=[row, row, pl.BlockSpec((8, D), lambda i: (0, 0))],
        out_shape=[jax.ShapeDtypeStruct((S, D), f32), jax.ShapeDtypeStruct((S, D), bf16),
                   jax.ShapeDtypeStruct((8, D), f32)],
        scratch_shapes=[pltpu.VMEM((8, D), f32), pltpu.VMEM((8, D), f32)],
        compiler_params=_cp("arbitrary"),
    )(dout, dh, x1, mo, w_post, w_pre2)


def loss_head(x1, ff, target, w, ts):
    S = x1.shape[0]
    nt = S // ts

    def body(x1_ref, ff_ref, t_ref, w_ref, loss_ref, dout_ref, dff_ref, gw_ref, accw_ref, accl_ref):
        i = pl.program_id(0)
        wv = w_ref[...]

        @pl.when(i == 0)
        def _():
            accw_ref[...] = jnp.zeros_like(accw_ref)
            accl_ref[...] = jnp.zeros_like(accl_ref)
            gw_ref[...] = jnp.zeros_like(gw_ref)

        def grp(r0, _):
            rows = pl.ds(r0, NRG)
            fv = ff_ref[rows, :].astype(f32)
            r = _rms(fv)
            n = fv * r
            e = x1_ref[rows, :] + n * wv - t_ref[rows, :]
            dout = e * (1.0 / D)
            dout_ref[rows, :] = dout
            dff_ref[rows, :] = _rms_bwd(fv, r, dout * wv).astype(bf16)
            accw_ref[...] += _fold(dout * n)
            accl_ref[...] += _fold(e * e)

        _groups(ts, grp, rg=NRG)

        @pl.when(i == nt - 1)
        def _():
            _flush(accw_ref, gw_ref, 0)
            tot = jnp.sum(jnp.sum(accl_ref[...], axis=1, keepdims=True), axis=0, keepdims=True) * (0.5 / D)
            loss_ref[...] = jnp.broadcast_to(tot, loss_ref.shape)

    row = pl.BlockSpec((ts, D), lambda i: (i, 0))
    return pl.pallas_call(
        body, name="loss_head", grid=(nt,),
        in_specs=[row, row, row, pl.BlockSpec((1, D), lambda i: (0, 0))],
        out_specs=[pl.BlockSpec((8, LANES), lambda i: (0, 0)), row, row, pl.BlockSpec((8, D), lambda i: (0, 0))],
        out_shape=[jax.ShapeDtypeStruct((8, LANES), f32), jax.ShapeDtypeStruct((S, D), f32),
                   jax.ShapeDtypeStruct((S, D), bf16), jax.ShapeDtypeStruct((8, D), f32)],
        scratch_shapes=[pltpu.VMEM((8, D), f32), pltpu.VMEM((8, D), f32)],
        compiler_params=_cp("arbitrary"),
    )(x1, ff, target, w)


def _taps(w_ref, cs, K):
    return [jnp.broadcast_to(w_ref[k:k + 1, cs], (8, CW)) for k in range(K)]


def _down(before, cur, s, sub):
    return jnp.where(sub < s, pltpu.roll(before, s, 0), pltpu.roll(cur, s, 0))


def _up(cur, after, s, sub):
    return jnp.where(sub < 8 - s, pltpu.roll(cur, 8 - s, 0), pltpu.roll(after, 8 - s, 0))


def _conv_group(p, a, b, taps, bias, K, sub):
    ya, yb = bias, bias
    for k in range(K):
        s = K - 1 - k
        xa, xb = (a, b) if s == 0 else (_down(p, a, s, sub), _down(a, b, s, sub))
        ya = ya + taps[k] * xa
        yb = yb + taps[k] * xb
    return ya, yb


def _prev8_map(ts, cb):
    return lambda i, j: (jnp.maximum(i * (ts // 8) - 1, 0), cb + j)


def ssdconv_fwd(proj, w8, b, ts):
    S = proj.shape[0]
    bw = 1024
    cb = C_XBC // bw

    def body(cur_ref, prev_ref, w_ref, b_ref, o_ref, c_ref):
        first = pl.program_id(0) == 0
        sub = lax.broadcasted_iota(jnp.int32, (8, CW), 0)
        for c0 in range(0, bw, CW):
            cs = slice(c0, c0 + CW)
            taps = _taps(w_ref, cs, SSD_K)
            bias = jnp.broadcast_to(b_ref[:, cs], (8, CW))

            def grp(r0, p, cs=cs, taps=taps, bias=bias):
                rows = pl.ds(r0, RG)
                xv = cur_ref[rows, cs].astype(f32)
                ya, yb = _conv_group(p, xv[0:8], xv[8:16], taps, bias, SSD_K, sub)
                y = jnp.concatenate([ya, yb], axis=0)
                c_ref[rows, cs] = y.astype(bf16)
                o_ref[rows, cs] = (y * _sigmoid_fast(y)).astype(bf16)
                return xv[8:16]

            _groups(ts, grp, jnp.where(first, 0.0, prev_ref[:, cs].astype(f32)))

    o = jax.ShapeDtypeStruct((S, CONVD), bf16)
    blk = pl.BlockSpec((ts, bw), lambda i, j: (i, j))
    return pl.pallas_call(
        body, name="ssdconv_fwd", grid=(S // ts, CONVD // bw),
        in_specs=[pl.BlockSpec((ts, bw), lambda i, j: (i, cb + j)),
                  pl.BlockSpec((8, bw), _prev8_map(ts, cb)),
                  pl.BlockSpec((8, bw), lambda i, j: (0, j)),
                  pl.BlockSpec((1, bw), lambda i, j: (0, j))],
        out_specs=[blk, blk], out_shape=[o, o],
        compiler_params=_cp("parallel", "parallel"),
    )(proj, proj, w8, b)


def _gelu_tanh(x):
    c = 0.7978845608028654
    t = jnp.tanh(c * (x + 0.044715 * x * x * x))
    return 0.5 * x * (1.0 + t), t


def ffnact_fwd(up, w8, b, ts):
    S = up.shape[0]

    def body(g_ref, gp_ref, v_ref, vp_ref, wg_ref, wv_ref, bg_ref, bv_ref, o_ref, gc_ref, vc_ref):
        first = pl.program_id(0) == 0
        sub = lax.broadcasted_iota(jnp.int32, (8, CW), 0)
        for c0 in range(0, FF, CW):
            cs = slice(c0, c0 + CW)
            tg, tv = _taps(wg_ref, cs, FFN_K), _taps(wv_ref, cs, FFN_K)
            bg = jnp.broadcast_to(bg_ref[:, cs], (8, CW))
            bv = jnp.broadcast_to(bv_ref[:, cs], (8, CW))

            def grp(r0, carry, cs=cs, tg=tg, tv=tv, bg=bg, bv=bv):
                pg, pv = carry
                rows = pl.ds(r0, RG)
                gx = g_ref[rows, cs].astype(f32)
                vx = v_ref[rows, cs].astype(f32)
                g = jnp.concatenate(_conv_group(pg, gx[0:8], gx[8:16], tg, bg, FFN_K, sub), axis=0)
                v = jnp.concatenate(_conv_group(pv, vx[0:8], vx[8:16], tv, bv, FFN_K, sub), axis=0)
                gc_ref[rows, cs] = g.astype(bf16)
                vc_ref[rows, cs] = v.astype(bf16)
                o_ref[rows, cs] = (_gelu_tanh(g)[0] * v).astype(bf16)
                return gx[8:16], vx[8:16]

            _groups(ts, grp, (jnp.where(first, 0.0, gp_ref[:, cs].astype(f32)),
                              jnp.where(first, 0.0, vp_ref[:, cs].astype(f32))))

    o = jax.ShapeDtypeStruct((S, FF), bf16)
    blk = pl.BlockSpec((ts, FF), lambda i: (i, 0))
    prev = lambda cb: pl.BlockSpec((8, FF), lambda i: (jnp.maximum(i * (ts // 8) - 1, 0), cb))
    return pl.pallas_call(
        body, name="ffnact_fwd", grid=(S // ts,),
        in_specs=[blk, prev(0), pl.BlockSpec((ts, FF), lambda i: (i, 1)), prev(1),
                  pl.BlockSpec((8, FF), lambda i: (0, 0)), pl.BlockSpec((8, FF), lambda i: (0, 1)),
                  pl.BlockSpec((1, FF), lambda i: (0, 0)), pl.BlockSpec((1, FF), lambda i: (0, 1))],
        out_specs=[blk, blk, blk], out_shape=[o, o, o],
        compiler_params=_cp("parallel"),
    )(up, up, up, up, w8, w8, b, b)


def ffnact_bwd(dact, gc, vc, ts):
    S = dact.shape[0]

    def body(d_ref, g_ref, v_ref, dg_ref, dv_ref):
        c = 0.7978845608028654
        for c0 in range(0, FF, CW):
            cs = slice(c0, c0 + CW)

            def grp(r0, _, cs=cs):
                rows = pl.ds(r0, RG)
                d = d_ref[rows, cs].astype(f32)
                g = g_ref[rows, cs].astype(f32)
                ge, t = _gelu_tanh(g)
                dgelu = 0.5 * (1.0 + t) + 0.5 * g * (1.0 - t * t) * c * (1.0 + 3.0 * 0.044715 * g * g)
                dg_ref[rows, cs] = (d * v_ref[rows, cs].astype(f32) * dgelu).astype(bf16)
                dv_ref[rows, cs] = (d * ge).astype(bf16)

            _groups(ts, grp)

    o = jax.ShapeDtypeStruct((S, FF), bf16)
    blk = pl.BlockSpec((ts, FF), lambda i: (i, 0))
    return pl.pallas_call(
        body, name="ffnact_bwd", grid=(S // ts,),
        in_specs=[blk, blk, blk], out_specs=[blk, blk], out_shape=[o, o],
        compiler_params=_cp("parallel"),
    )(dact, gc, vc)


def dwconv_bwd(dy, x, xcb, w8, wcb, K, bw, ts, name, act_c=None, into=None, ocb=0, out_cols=None):
    S, C = dy.shape
    nr = S // ts
    out_cols = out_cols or C
    n_act = 0 if act_c is None else 2

    def body(*refs):
        dy_ref, dyn_ref = refs[0:2]
        c_ref, cn_ref = (refs[2:4] if n_act else (None, None))
        x_ref, xp_ref, w_ref = refs[2 + n_act:5 + n_act]
        dx_ref, dw_ref, sd_ref = refs[-3:]
        i = pl.program_id(1)
        first, last = i == 0, i == nr - 1
        sub = lax.broadcasted_iota(jnp.int32, (8, CW), 0)

        def grad_y(d, c):
            if c is None:
                return d.astype(f32)
            cv = c.astype(f32)
            s = _sigmoid_fast(cv)
            return d.astype(f32) * s * (1.0 + cv * (1.0 - s))

        @pl.when(first)
        def _():
            dw_ref[...] = jnp.zeros_like(dw_ref)

        for c0 in range(0, bw, CW):
            cs = slice(c0, c0 + CW)
            taps = _taps(w_ref, cs, K)
            zero = jnp.zeros((8, CW), f32)

            def fwd(r0, carry, cs=cs):
                p, accs, accb = carry
                rows = pl.ds(r0, RG)
                g = grad_y(dy_ref[rows, cs], c_ref[rows, cs] if n_act else None)
                sd_ref[rows, cs] = g
                xv = x_ref[rows, cs].astype(f32)
                a, b = xv[0:8], xv[8:16]
                ga, gb = g[0:8], g[8:16]
                new = []
                for k in range(K):
                    s = K - 1 - k
                    xa, xb = (a, b) if s == 0 else (_down(p, a, s, sub), _down(a, b, s, sub))
                    new.append(accs[k] + ga * xa + gb * xb)
                return b, tuple(new), accb + ga + gb

            _, accs, accb = _groups(ts, fwd, (jnp.where(first, 0.0, xp_ref[:, cs].astype(f32)), (zero,) * K, zero))
            for k in range(K):
                dw_ref[k:k + 1, cs] += jnp.sum(accs[k], axis=0, keepdims=True)
            dw_ref[7:8, cs] += jnp.sum(accb, axis=0, keepdims=True)

            def bwd(r0, after, cs=cs, taps=taps):
                rows = pl.ds(r0, RG)
                g = sd_ref[rows, cs]
                a, b = g[0:8], g[8:16]
                da, db = zero, zero
                for k in range(K):
                    s = K - 1 - k
                    ua, ub = (a, b) if s == 0 else (_up(a, b, s, sub), _up(b, after, s, sub))
                    da = da + taps[k] * ua
                    db = db + taps[k] * ub
                dx_ref[rows, cs] = jnp.concatenate([da, db], axis=0).astype(bf16)
                return a

            halo = grad_y(dyn_ref[:, cs], cn_ref[:, cs] if n_act else None)
            _groups(ts, bwd, jnp.where(last, 0.0, halo), reverse=True)

    nxt = lambda j, i: (jnp.minimum((i + 1) * (ts // 8), S // 8 - 1), j)
    tile = pl.BlockSpec((ts, bw), lambda j, i: (i, j))
    acts = [] if act_c is None else [act_c, act_c]
    extra = [] if into is None else [into]
    n_in = 5 + n_act
    return pl.pallas_call(
        body, name=name, grid=(C // bw, nr),
        in_specs=[tile, pl.BlockSpec((8, bw), nxt)] + ([tile, pl.BlockSpec((8, bw), nxt)] if n_act else []) + [
            pl.BlockSpec((ts, bw), lambda j, i: (i, xcb + j)),
            pl.BlockSpec((8, bw), lambda j, i: (jnp.maximum(i * (ts // 8) - 1, 0), xcb + j)),
            pl.BlockSpec((8, bw), lambda j, i: (0, wcb + j))] + [pl.BlockSpec(memory_space=pl.ANY)] * len(extra),
        out_specs=[pl.BlockSpec((ts, bw), lambda j, i: (i, ocb + j)), pl.BlockSpec((8, bw), lambda j, i: (0, j))],
        out_shape=[jax.ShapeDtypeStruct((S, out_cols), bf16), jax.ShapeDtypeStruct((8, C), f32)],
        scratch_shapes=[pltpu.VMEM((ts, bw), f32)],
        input_output_aliases={n_in: 0} if extra else {},
        compiler_params=_cp("parallel", "arbitrary"),
    )(dy, dy, *acts, x, x, w8, *extra)


def gnorm_fwd(y, proj, w, ts):
    S = y.shape[0]

    def body(y_ref, z_ref, w_ref, o_ref):
        for k in range(NG):
            sl = slice(k * GW, (k + 1) * GW)
            wv = w_ref[:, sl]

            def grp(r0, _, sl=sl, wv=wv):
                rows = pl.ds(r0, NRG)
                z = z_ref[rows, sl].astype(f32)
                g = y_ref[rows, sl].astype(f32) * z * _sigmoid_fast(z)
                o_ref[rows, sl] = (g * _rms(g) * wv).astype(bf16)

            _groups(ts, grp, rg=NRG)

    row = pl.BlockSpec((ts, DI), lambda i: (i, 0))
    return pl.pallas_call(
        body, name="gnorm_fwd", grid=(S // ts,),
        in_specs=[row, row, pl.BlockSpec((1, DI), lambda i: (0, 0))],
        out_specs=row, out_shape=jax.ShapeDtypeStruct((S, DI), bf16),
        compiler_params=_cp("parallel"),
    )(y, proj, w)


def gnorm_bwd(dyn, y, proj, w, dproj, ts):
    S = y.shape[0]
    nt = S // ts

    def body(d_ref, y_ref, z_ref, w_ref, _, dy_ref, dz_ref, gw_ref, acc_ref):
        i = pl.program_id(0)

        @pl.when(i == 0)
        def _():
            acc_ref[...] = jnp.zeros_like(acc_ref)
            gw_ref[...] = jnp.zeros_like(gw_ref)

        for k in range(NG):
            sl = slice(k * GW, (k + 1) * GW)
            wv = w_ref[:, sl]

            def grp(r0, _, sl=sl, wv=wv):
                rows = pl.ds(r0, NRG)
                z = z_ref[rows, sl].astype(f32)
                yv = y_ref[rows, sl].astype(f32)
                s = _sigmoid_fast(z)
                sz = z * s
                g = yv * sz
                r = _rms(g)
                d = d_ref[rows, sl].astype(f32)
                acc_ref[:, sl] += _fold(d * g * r)
                dg = _rms_bwd(g, r, d * wv)
                dy_ref[rows, sl] = (dg * sz).astype(bf16)
                dz_ref[rows, sl] = (dg * yv * s * (1.0 + z * (1.0 - s))).astype(bf16)

            _groups(ts, grp, rg=NRG)

        @pl.when(i == nt - 1)
        def _():
            _flush(acc_ref, gw_ref, 0)

    row = pl.BlockSpec((ts, DI), lambda i: (i, 0))
    return pl.pallas_call(
        body, name="gnorm_bwd", grid=(nt,),
        in_specs=[row, row, row, pl.BlockSpec((1, DI), lambda i: (0, 0)), pl.BlockSpec(memory_space=pl.ANY)],
        out_specs=[row, row, pl.BlockSpec((8, DI), lambda i: (0, 0))],
        out_shape=[jax.ShapeDtypeStruct((S, DI), bf16), jax.ShapeDtypeStruct(dproj.shape, bf16),
                   jax.ShapeDtypeStruct((8, DI), f32)],
        scratch_shapes=[pltpu.VMEM((8, DI), f32)],
        input_output_aliases={4: 1},
        compiler_params=_cp("arbitrary"),
    )(dyn, y, proj, w, dproj)


def merge_fwd(proj, ys, ya, ts):
    S = ys.shape[0]

    def body(gs_ref, ga_ref, ys_ref, ya_ref, o_ref):
        for c0 in range(0, D, CW):
            cs = slice(c0, c0 + CW)

            def grp(r0, _, cs=cs):
                rows = pl.ds(r0, NRG)
                o_ref[rows, cs] = (_sigmoid_fast(gs_ref[rows, cs].astype(f32)) * ys_ref[rows, cs].astype(f32)
                                   + _sigmoid_fast(ga_ref[rows, cs].astype(f32)) * ya_ref[rows, cs].astype(f32)
                                   ).astype(bf16)

            _groups(ts, grp, rg=NRG)

    row = pl.BlockSpec((ts, D), lambda i: (i, 0))
    return pl.pallas_call(
        body, name="merge_fwd", grid=(S // ts,),
        in_specs=[pl.BlockSpec((ts, D), lambda i: (i, C_GS // D)), pl.BlockSpec((ts, D), lambda i: (i, C_GA // D)), row, row],
        out_specs=row, out_shape=jax.ShapeDtypeStruct((S, D), bf16),
        compiler_params=_cp("parallel"),
    )(proj, proj, ys, ya)


def merge_bwd(dm, proj, ys, ya, ts):
    S = ys.shape[0]

    def body(d_ref, gs_ref, ga_ref, ys_ref, ya_ref, dys_ref, dya_ref, dg_ref):
        for c0 in range(0, D, CW):
            cs = slice(c0, c0 + CW)

            def grp(r0, _, c0=c0, cs=cs):
                rows = pl.ds(r0, NRG)
                d = d_ref[rows, cs].astype(f32)
                ss = _sigmoid_fast(gs_ref[rows, cs].astype(f32))
                sa = _sigmoid_fast(ga_ref[rows, cs].astype(f32))
                dys_ref[rows, cs] = (d * ss).astype(bf16)
                dya_ref[rows, cs] = (d * sa).astype(bf16)
                dg_ref[rows, cs] = (d * ys_ref[rows, cs].astype(f32) * ss * (1.0 - ss)).astype(bf16)
                dg_ref[rows, D + c0:D + c0 + CW] = (d * ya_ref[rows, cs].astype(f32) * sa * (1.0 - sa)).astype(bf16)

            _groups(ts, grp, rg=NRG)

    row = pl.BlockSpec((ts, D), lambda i: (i, 0))
    o = jax.ShapeDtypeStruct((S, D), bf16)
    return pl.pallas_call(
        body, name="merge_bwd", grid=(S // ts,),
        in_specs=[row, pl.BlockSpec((ts, D), lambda i: (i, C_GS // D)), pl.BlockSpec((ts, D), lambda i: (i, C_GA // D)), row, row],
        out_specs=[row, row, pl.BlockSpec((ts, 2 * D), lambda i: (i, C_GS // (2 * D)))],
        out_shape=[o, o, jax.ShapeDtypeStruct((S, PM), bf16)],
        compiler_params=_cp("parallel"),
    )(dm, proj, proj, ys, ya)


def _ssd_consts():
    h = lax.broadcasted_iota(jnp.int32, (LANES, DI), 0)
    c = lax.broadcasted_iota(jnp.int32, (LANES, DI), 1)
    expand = (c // HD == h).astype(bf16)
    r = lax.broadcasted_iota(jnp.int32, (CH, CH), 0)
    cc = lax.broadcasted_iota(jnp.int32, (CH, CH), 1)
    tril = (cc <= r).astype(bf16)
    triu = (cc >= r).astype(bf16)
    return expand, expand.T, tril, triu


def _ssd_common(xbc_ref, dtr_ref, bias_ref, alog_ref, expand_ref, tril_ref):
    dtr = dtr_ref[...] + bias_ref[...]
    dt = jnp.maximum(dtr, 0.0) + jnp.log1p(jnp.exp(-jnp.abs(dtr)))
    a = -jnp.exp(alog_ref[...])
    acs = _dot3_left(tril_ref[...], dt * a)
    acsx = _dot3_right(acs, expand_ref[...])
    dtx = _dot3_right(dt, expand_ref[...])
    x = xbc_ref[:, 0:DI].astype(f32)
    xdt = x * dtx
    e = jnp.exp(acsx)
    dsx = jnp.exp(acsx[CH - 1:CH, :] - acsx)
    return dtr, dt, a, acs, dtx, x, xdt, e, dsx


def _ssd_lmat(acs, acs_t, hh, causal):
    seg = acs[:, hh:hh + 1] - acs_t[hh:hh + 1, :]
    return jnp.where(causal, jnp.exp(jnp.minimum(seg, 0.0)), 0.0)


def ssd_fwd(xbc, dtr, bias, alog, dx_row, comm=None):
    S = xbc.shape[0]
    nc = S // CH
    expand, _, tril, _ = _ssd_consts()
    cm = _Comm(comm)

    def body(*refs):
        ins, (y_ref, hp_ref), (h_ref, yd_ref), copies = cm.split(refs, 7, 2)
        xbc_ref, dtr_ref, bias_ref, alog_ref, dxr_ref, expand_ref, tril_ref = ins
        c = pl.program_id(0)
        cm.start(copies, c == 0)

        @pl.when(c == 0)
        def _():
            h_ref[...] = jnp.zeros_like(h_ref)

        _, _, _, acs, _, x, xdt, e, dsx = _ssd_common(xbc_ref, dtr_ref, bias_ref, alog_ref, expand_ref, tril_ref)
        acs_t = acs.T
        xb = xdt.astype(bf16)
        xd = (xdt * dsx).astype(bf16)
        causal = tril_ref[...] > 0
        for g in range(NG):
            gs = slice(g * GW, (g + 1) * GW)
            bg = xbc_ref[:, DI + g * NS:DI + (g + 1) * NS]
            cg = xbc_ref[:, DI + NG * NS + g * NS:DI + NG * NS + (g + 1) * NS]
            cb = _dot(cg, bg, "nt")
            hp = h_ref[g]
            hpb = hp.astype(bf16)
            hp_ref[0, g] = hpb
            yd_ref[:, gs] = _dot(cg, hpb) * e[:, gs]
            h_ref[g] = hp * e[CH - 1:CH, gs] + _dot(bg, xd[:, gs], "tn")
            for j in range(NH // NG):
                hh = g * (NH // NG) + j
                hs = slice(hh * HD, (hh + 1) * HD)
                m = (cb * _ssd_lmat(acs, acs_t, hh, causal)).astype(bf16)
                yd_ref[:, hs] += _dot(m, xb[:, hs])
        y_ref[...] = (yd_ref[...] + dxr_ref[...] * x).astype(bf16)
        cm.wait(copies, c == nc - 1)

    par = lambda shape: pl.BlockSpec(shape, lambda c: (0,) * len(shape))
    res = pl.pallas_call(
        body, name="ssd_fwd", grid=(nc,),
        in_specs=[pl.BlockSpec((CH, CONVD), lambda c: (c, 0)), pl.BlockSpec((CH, LANES), lambda c: (c, 0)),
                  par((1, LANES)), par((1, LANES)), par((1, DI)), par((LANES, DI)), par((CH, CH))] + cm.in_specs(),
        out_specs=[pl.BlockSpec((CH, DI), lambda c: (c, 0)),
                   pl.BlockSpec((1, NG, NS, GW), lambda c: (c, 0, 0, 0))] + cm.out_specs(),
        out_shape=[jax.ShapeDtypeStruct((S, DI), bf16), jax.ShapeDtypeStruct((nc, NG, NS, GW), bf16)] + cm.out_shape(),
        scratch_shapes=[pltpu.VMEM((NG, NS, GW), f32), pltpu.VMEM((CH, DI), f32)] + cm.scratch(),
        compiler_params=_cp("arbitrary", side_effects=bool(cm.n)),
    )(xbc, dtr, bias, alog, dx_row, expand, tril, *cm.bufs)
    return res[0], res[1], res[2:]


def ssd_bwd(xbc, dtr, dy, hprev, bias, alog, dx_row, comm=None):
    S = xbc.shape[0]
    nc = S // CH
    expand, expand_t, tril, triu = _ssd_consts()
    cm = _Comm(comm)

    def body(*refs):
        ins, outs, scr, copies = cm.split(refs, 11, 3)
        xbc_ref, dtr_ref, dy_ref, hp_ref, bias_ref, alog_ref, dxr_ref, expand_ref, expt_ref, tril_ref, triu_ref = ins
        dxbc_ref, ddtr_ref, acc_ref = outs
        dh_ref, dxs_ref, t_ref, accb_ref, acca_ref, accd_ref = scr
        c = pl.program_id(0)
        cm.start(copies, c == 0)

        @pl.when(c == 0)
        def _():
            dh_ref[...] = jnp.zeros_like(dh_ref)
            accb_ref[...] = jnp.zeros_like(accb_ref)
            acca_ref[...] = jnp.zeros_like(acca_ref)
            accd_ref[...] = jnp.zeros_like(accd_ref)

        dtr, dt, a, acs, dtx, x, xdt, e, dsx = _ssd_common(xbc_ref, dtr_ref, bias_ref, alog_ref, expand_ref, tril_ref)
        acs_t = acs.T
        xb = xdt.astype(bf16)
        xdf = xdt * dsx
        xd = xdf.astype(bf16)
        dyv = dy_ref[...].astype(f32)
        dyb = dy_ref[...]
        dye = (dyv * e).astype(bf16)
        causal = tril_ref[...] > 0
        lane = lax.broadcasted_iota(jnp.int32, (CH, LANES), 1)
        subl = lax.broadcasted_iota(jnp.int32, (LANES, CH), 0)
        ccol = jnp.zeros((CH, LANES), f32)
        rrow = jnp.zeros((LANES, CH), f32)
        last_row = lax.broadcasted_iota(jnp.int32, (CH, 1), 0) == CH - 1
        for g in range(NG):
            gs = slice(g * GW, (g + 1) * GW)
            bsl = slice(DI + g * NS, DI + (g + 1) * NS)
            csl = slice(DI + NG * NS + g * NS, DI + NG * NS + (g + 1) * NS)
            bg = xbc_ref[:, bsl]
            cg = xbc_ref[:, csl]
            cb = _dot(cg, bg, "nt")
            hpb = hp_ref[0, g]
            dhn = dh_ref[g]
            dhnb = dhn.astype(bf16)
            yoff = _dot(cg, hpb) * e[:, gs]
            dxd = _dot(bg, dhnb)
            t2 = dxd * xdf[:, gs]
            t3 = jnp.sum(dhn * hpb.astype(f32), axis=0, keepdims=True) * e[CH - 1:CH, gs]
            t_ref[:, gs] = dyv[:, gs] * yoff - t2 + jnp.where(last_row, jnp.sum(t2, axis=0, keepdims=True) + t3, 0.0)
            dxs_ref[:, gs] = dxd * dsx[:, gs]
            dcg = _dot(dye[:, gs], hpb, "nt")
            dbg = _dot(xd[:, gs], dhnb, "nt")
            dh_ref[g] = dhn * e[CH - 1:CH, gs] + _dot(cg, dye[:, gs], "tn")
            dcb = jnp.zeros((CH, CH), f32)
            for j in range(NH // NG):
                hh = g * (NH // NG) + j
                hs = slice(hh * HD, (hh + 1) * HD)
                lm = _ssd_lmat(acs, acs_t, hh, causal)
                m = cb * lm
                dm = _dot(dyb[:, hs], xb[:, hs], "nt")
                gm = dm * m
                ccol = ccol + jnp.sum(gm, axis=1, keepdims=True) * (lane == hh).astype(f32)
                rrow = rrow + jnp.sum(gm, axis=0, keepdims=True) * (subl == hh).astype(f32)
                dcb = dcb + dm * lm
                dxs_ref[:, hs] += _dot(m.astype(bf16), dyb[:, hs], "tn")
            dcbb = dcb.astype(bf16)
            dxbc_ref[:, csl] = (dcg + _dot(dcbb, bg)).astype(bf16)
            dxbc_ref[:, bsl] = (dbg + _dot(dcbb, cg, "tn")).astype(bf16)
        dxf = dxs_ref[...]
        dxbc_ref[:, 0:DI] = (dxf * dtx + dxr_ref[...] * dyv).astype(bf16)
        expt = expt_ref[...]
        dacs = ccol - rrow.T + _dot2_right(t_ref[...], expt)
        dadt = _dot3_left(triu_ref[...], dacs)
        ddt = _dot2_right(dxf * x, expt) + dadt * a
        ddtr = ddt * _sigmoid(dtr)
        ddtr_ref[...] = ddtr
        accb_ref[...] += ddtr
        acca_ref[...] += dadt * dt
        accd_ref[...] += _dot2_right(dyv * x, expt)

        @pl.when(c == nc - 1)
        def _():
            acc_ref[...] = jnp.zeros_like(acc_ref)
            acc_ref[0:1, :] = jnp.sum(accb_ref[...], axis=0, keepdims=True)
            acc_ref[1:2, :] = jnp.sum(acca_ref[...], axis=0, keepdims=True) * a
            acc_ref[2:3, :] = jnp.sum(accd_ref[...], axis=0, keepdims=True)

        cm.wait(copies, c == nc - 1)

    par = lambda shape: pl.BlockSpec(shape, lambda c: (0,) * len(shape))
    rev = lambda c: (nc - 1 - c, 0)
    res = pl.pallas_call(
        body, name="ssd_bwd", grid=(nc,),
        in_specs=[pl.BlockSpec((CH, CONVD), rev), pl.BlockSpec((CH, LANES), rev), pl.BlockSpec((CH, DI), rev),
                  pl.BlockSpec((1, NG, NS, GW), lambda c: (nc - 1 - c, 0, 0, 0)),
                  par((1, LANES)), par((1, LANES)), par((1, DI)), par((LANES, DI)), par((DI, LANES)),
                  par((CH, CH)), par((CH, CH))] + cm.in_specs(),
        out_specs=[pl.BlockSpec((CH, CONVD), rev), pl.BlockSpec((CH, LANES), rev), par((8, LANES))] + cm.out_specs(),
        out_shape=[jax.ShapeDtypeStruct((S, CONVD), bf16), jax.ShapeDtypeStruct((S, LANES), f32),
                   jax.ShapeDtypeStruct((8, LANES), f32)] + cm.out_shape(),
        scratch_shapes=[pltpu.VMEM((NG, NS, GW), f32), pltpu.VMEM((CH, DI), f32), pltpu.VMEM((CH, DI), f32),
                        pltpu.VMEM((CH, LANES), f32), pltpu.VMEM((CH, LANES), f32),
                        pltpu.VMEM((CH, LANES), f32)] + cm.scratch(),
        compiler_params=_cp("arbitrary", side_effects=bool(cm.n)),
    )(xbc, dtr, dy, hprev, bias, alog, dx_row, expand, expand_t, tril, triu, *cm.bufs)
    return res[0], res[1], res[2], res[3:]


def rope_tables(pos_row, ts):
    S = pos_row.shape[1]
    half = AD // 2
    inv = ROPE_THETA ** (-jnp.arange(half, dtype=f32) * 2.0 / AD)
    inv_col = jnp.tile(inv, 2)[:, None]

    def body(p_ref, inv_ref, cos_ref, sin_ref):
        ang = inv_ref[...] * p_ref[...].astype(f32)
        row = lax.broadcasted_iota(jnp.int32, ang.shape, 0)
        cos_ref[...] = jnp.cos(ang)
        sin_ref[...] = jnp.where(row < half, -1.0, 1.0) * jnp.sin(ang)

    o = jax.ShapeDtypeStruct((AD, S), f32)
    return pl.pallas_call(
        body, name="rope_tables", grid=(S // ts,),
        in_specs=[pl.BlockSpec((1, ts), lambda i: (0, i)), pl.BlockSpec((AD, 1), lambda i: (0, 0))],
        out_specs=[pl.BlockSpec((AD, ts), lambda i: (0, i))] * 2, out_shape=[o, o],
        compiler_params=_cp("parallel"),
    )(pos_row, inv_col)


def _partner(t):
    half = AD // 2
    return jnp.concatenate([t[h * AD + o:h * AD + o + half] for h in range(t.shape[0] // AD) for o in (half, 0)], axis=0)


def _rope(t, cos, sin):
    reps = t.shape[0] // AD
    return t * jnp.tile(cos, (reps, 1)) + _partner(t) * jnp.tile(sin, (reps, 1))


def _rope_t(d, cos, sin):
    reps = d.shape[0] // AD
    return d * jnp.tile(cos, (reps, 1)) - _partner(d) * jnp.tile(sin, (reps, 1))


def _lanes_of_group(t, g):
    return jnp.concatenate([t[(g * REP + r) * AD:(g * REP + r + 1) * AD] for r in range(REP)], axis=1)


def _attn_probs(qg, kp, kc, sink_ref, g, not_first):
    n = qg.shape[1]
    s = lax.broadcasted_iota(jnp.int32, (WIN, n), 0)
    t = lax.broadcasted_iota(jnp.int32, (WIN, n), 1) % WIN
    neg = -1e30
    sink = jnp.concatenate([jnp.broadcast_to(sink_ref[0:1, g * REP + r:g * REP + r + 1], (1, WIN)) for r in range(REP)],
                           axis=1)
    sp = jnp.where(jnp.logical_and(s > t, not_first), _dot(kp, qg, "tn"), neg)
    sc = jnp.where(s <= t, _dot(kc, qg, "tn"), neg)
    m = jnp.maximum(jnp.maximum(jnp.max(sp, axis=0, keepdims=True), jnp.max(sc, axis=0, keepdims=True)), sink)
    pp = jnp.exp(sp - m)
    pc = jnp.exp(sc - m)
    ps = jnp.exp(sink - m)
    inv = 1.0 / (jnp.sum(pp, axis=0, keepdims=True) + jnp.sum(pc, axis=0, keepdims=True) + ps)
    return pp * inv, pc * inv, ps * inv


def attn_fwd(qt, kvt, cos, sin, sinks):
    S = qt.shape[1]
    nb = S // WIN
    cur = lambda i: (0, i)
    prev = lambda i: (0, jnp.maximum(i - 1, 0))

    def body(q_ref, kv_ref, kvp_ref, cos_ref, sin_ref, cosp_ref, sinp_ref, sink_ref, o_ref):
        i = pl.program_id(0)
        q = (_rope(q_ref[...].astype(f32), cos_ref[...], sin_ref[...]) * (AD ** -0.5)).astype(bf16)
        kc = _rope(kv_ref[0:KVW, :].astype(f32), cos_ref[...], sin_ref[...]).astype(bf16)
        kp = _rope(kvp_ref[0:KVW, :].astype(f32), cosp_ref[...], sinp_ref[...]).astype(bf16)
        for g in range(KVH):
            ks = slice(g * AD, (g + 1) * AD)
            vs = slice(KVW + g * AD, KVW + (g + 1) * AD)
            pp, pc, _ = _attn_probs(_lanes_of_group(q, g), kp[ks], kc[ks], sink_ref, g, i > 0)
            o = _dot(kvp_ref[vs, :], pp.astype(bf16)) + _dot(kv_ref[vs, :], pc.astype(bf16))
            for r in range(REP):
                h = g * REP + r
                o_ref[h * AD:(h + 1) * AD, :] = o[:, r * WIN:(r + 1) * WIN].astype(bf16)

    tab = pl.BlockSpec((AD, WIN), cur)
    tabp = pl.BlockSpec((AD, WIN), prev)
    return pl.pallas_call(
        body, name="attn_fwd", grid=(nb,),
        in_specs=[pl.BlockSpec((D, WIN), cur), pl.BlockSpec((2 * KVW, WIN), cur), pl.BlockSpec((2 * KVW, WIN), prev),
                  tab, tab, tabp, tabp, pl.BlockSpec((1, LANES), lambda i: (0, 0))],
        out_specs=pl.BlockSpec((D, WIN), cur),
        out_shape=jax.ShapeDtypeStruct((D, S), bf16),
        compiler_params=_cp("parallel"),
    )(qt, kvt, kvt, cos, sin, cos, sin, sinks)


def attn_bwd(qt, kvt, cos, sin, sinks, daot, comm=None):
    S = qt.shape[1]
    nb = S // WIN
    cur = lambda i: (0, jnp.minimum(i, nb - 1))
    prev = lambda i: (0, jnp.maximum(i - 1, 0))
    cm = _Comm(comm)

    def body(*refs):
        ins, (dq_ref, dkv_ref, ds_ref), scr, copies = cm.split(refs, 9, 3)
        q_ref, kv_ref, kvp_ref, cos_ref, sin_ref, cosp_ref, sinp_ref, sink_ref, do_ref = ins
        ck_ref, cv_ref, dqs_ref, dkp_ref, dvp_ref, dkc_ref, dvc_ref, accs_ref = scr
        i = pl.program_id(0)
        cm.start(copies, i == 0)

        @pl.when(i == 0)
        def _():
            ck_ref[...] = jnp.zeros_like(ck_ref)
            cv_ref[...] = jnp.zeros_like(cv_ref)
            accs_ref[...] = jnp.zeros_like(accs_ref)

        @pl.when(i == nb)
        def _():
            dkp_ref[...] = jnp.zeros_like(dkp_ref)
            dvp_ref[...] = jnp.zeros_like(dvp_ref)

        @pl.when(i < nb)
        def _():
            q = (_rope(q_ref[...].astype(f32), cos_ref[...], sin_ref[...]) * (AD ** -0.5)).astype(bf16)
            kc = _rope(kv_ref[0:KVW, :].astype(f32), cos_ref[...], sin_ref[...]).astype(bf16)
            kp = _rope(kvp_ref[0:KVW, :].astype(f32), cosp_ref[...], sinp_ref[...]).astype(bf16)
            do = do_ref[...]
            for g in range(KVH):
                ks = slice(g * AD, (g + 1) * AD)
                vs = slice(KVW + g * AD, KVW + (g + 1) * AD)
                qg = _lanes_of_group(q, g)
                dog = _lanes_of_group(do, g)
                pp, pc, ps = _attn_probs(qg, kp[ks], kc[ks], sink_ref, g, i > 0)
                dpp = _dot(kvp_ref[vs, :], dog, "tn")
                dpc = _dot(kv_ref[vs, :], dog, "tn")
                delta = jnp.sum(pp * dpp + pc * dpc, axis=0, keepdims=True)
                dsp = (pp * (dpp - delta)).astype(bf16)
                dsc = (pc * (dpc - delta)).astype(bf16)
                accs_ref[g:g + 1, :] -= ps * delta
                dqg = (_dot(kp[ks], dsp) + _dot(kc[ks], dsc)) * (AD ** -0.5)
                for r in range(REP):
                    h = g * REP + r
                    dqs_ref[h * AD:(h + 1) * AD, :] = dqg[:, r * WIN:(r + 1) * WIN]
                dkp_ref[ks, :] = _dot(qg, dsp, "nt")
                dkc_ref[ks, :] = _dot(qg, dsc, "nt")
                dvp_ref[ks, :] = _dot(dog, pp.astype(bf16), "nt")
                dvc_ref[ks, :] = _dot(dog, pc.astype(bf16), "nt")
            dq_ref[...] = _rope_t(dqs_ref[...], cos_ref[...], sin_ref[...]).astype(bf16)

        dkv_ref[0:KVW, :] = _rope_t(ck_ref[...] + dkp_ref[...], cosp_ref[...], sinp_ref[...]).astype(bf16)
        dkv_ref[KVW:2 * KVW, :] = (cv_ref[...] + dvp_ref[...]).astype(bf16)

        @pl.when(i < nb)
        def _():
            ck_ref[...] = dkc_ref[...]
            cv_ref[...] = dvc_ref[...]

        @pl.when(i == nb)
        def _():
            lane = lax.broadcasted_iota(jnp.int32, (1, LANES), 1)
            row = jnp.zeros((1, LANES), f32)
            for h in range(AH):
                part = accs_ref[h // REP:h // REP + 1, (h % REP) * WIN:(h % REP + 1) * WIN]
                row = row + jnp.where(lane == h, jnp.sum(part, axis=1, keepdims=True), 0.0)
            ds_ref[...] = jnp.zeros_like(ds_ref)
            ds_ref[0:1, :] = row

        cm.wait(copies, i == nb)

    tab = pl.BlockSpec((AD, WIN), cur)
    tabp = pl.BlockSpec((AD, WIN), prev)
    kvs = lambda: pltpu.VMEM((KVW, WIN), f32)
    res = pl.pallas_call(
        body, name="attn_bwd", grid=(nb + 1,),
        in_specs=[pl.BlockSpec((D, WIN), cur), pl.BlockSpec((2 * KVW, WIN), cur), pl.BlockSpec((2 * KVW, WIN), prev),
                  tab, tab, tabp, tabp, pl.BlockSpec((1, LANES), lambda i: (0, 0)),
                  pl.BlockSpec((D, WIN), cur)] + cm.in_specs(),
        out_specs=[pl.BlockSpec((D, WIN), cur), pl.BlockSpec((2 * KVW, WIN), prev),
                   pl.BlockSpec((8, LANES), lambda i: (0, 0))] + cm.out_specs(),
        out_shape=[jax.ShapeDtypeStruct((D, S), bf16), jax.ShapeDtypeStruct((2 * KVW, S), bf16),
                   jax.ShapeDtypeStruct((8, LANES), f32)] + cm.out_shape(),
        scratch_shapes=[kvs(), kvs(), pltpu.VMEM((D, WIN), f32), kvs(), kvs(), kvs(), kvs(),
                        pltpu.VMEM((8, REP * WIN), f32)] + cm.scratch(),
        compiler_params=_cp("arbitrary", side_effects=bool(cm.n)),
    )(qt, kvt, kvt, cos, sin, cos, sin, sinks, daot, *cm.bufs)
    return res[0], res[1], res[2], res[3:]


ADAM_C1 = 1.0 / (1.0 - ADAM_B1 ** ADAM_STEP)
ADAM_C2 = 1.0 / (1.0 - ADAM_B2 ** ADAM_STEP)


def _adam_update(g, w, m, v):
    nm = ADAM_B1 * m + (1.0 - ADAM_B1) * g
    nv = ADAM_B2 * v + (1.0 - ADAM_B2) * (g * g)
    return -ADAM_LR * ((nm * ADAM_C1) / (jnp.sqrt(nv * ADAM_C2) + ADAM_EPS) + ADAM_WD * w), nm, nv


def adamw(parts, w, m, v, tr, name):
    n, R, C = parts.shape

    def body(p_ref, w_ref, m_ref, v_ref, g_ref, d_ref, nm_ref, nv_ref):
        def grp(g0, _):
            r0 = pl.multiple_of(g0 * RG, RG)
            rows = pl.ds(r0, RG)
            g = p_ref[0, rows, :].astype(f32)
            for k in range(1, n):
                g = g + p_ref[k, rows, :].astype(f32)
            d, nm, nv = _adam_update(g, w_ref[rows, :], m_ref[rows, :], v_ref[rows, :])
            g_ref[rows, :] = g
            d_ref[rows, :] = d
            nm_ref[rows, :] = nm
            nv_ref[rows, :] = nv
            return 0

        lax.fori_loop(0, tr // RG, grp, 0)

    row = pl.BlockSpec((tr, C), lambda i: (i, 0))
    o = jax.ShapeDtypeStruct((R, C), f32)
    return pl.pallas_call(
        body, name=name, grid=(R // tr,),
        in_specs=[pl.BlockSpec((n, tr, C), lambda i: (0, i, 0)), row, row, row],
        out_specs=[row, row, row, row], out_shape=[o, o, o, o],
        compiler_params=_cp("parallel"),
    )(parts, w, m, v)


SMALL_ROW = (("norm_mix_post_w", D), ("norm_ffn_pre_w", D), ("norm_ffn_post_w", D), ("ssd_norm_w", DI),
             ("ssd_conv_b", CONVD), ("ffn_conv_b", 2 * FF), ("ssd_dt_bias", NH), ("ssd_a_log", NH), ("ssd_d", NH),
             ("attn_sinks", AH), ("loss", 1))
CONV_BLOCK = 1152
SSD_CONV_COLS = CONVD // N_DEV
FFN_CONV_COLS = 2 * FF // N_DEV


def _row_offsets():
    off, o = {}, 0
    for name, n in SMALL_ROW:
        off[name] = (o, n)
        o += -(-n // LANES) * LANES
    return off, o


def adamw_small(recv_row, recv_pre, recv_conv, params):
    off, _ = _row_offsets()
    names = list(params)
    n = len(names)

    def total(ref, rows, lo, width):
        g = ref[0, rows, lo:lo + width]
        for d in range(1, N_DEV):
            g = g + ref[d, rows, lo:lo + width]
        return g

    def grad_of(name, row_ref, pre_ref, conv_ref):
        if name == "norm_mix_pre_w":
            return total(pre_ref, slice(0, 1), 0, D)
        if name == "ssd_conv_w":
            return total(conv_ref, slice(0, SSD_K), 0, SSD_CONV_COLS)
        if name == "ffn_conv_w":
            return total(conv_ref, slice(0, FFN_K), 3 * LANES, FFN_CONV_COLS)
        o, width = off[name]
        return total(row_ref, slice(0, 1), o, width)

    def body(row_ref, pre_ref, conv_ref, *refs):
        ins, outs = refs[:3 * n], refs[3 * n:]
        for k, name in enumerate(names):
            w_ref, m_ref, v_ref = ins[3 * k:3 * k + 3]
            g_ref, d_ref, nm_ref, nv_ref = outs[4 * k:4 * k + 4]
            g = grad_of(name, row_ref, pre_ref, conv_ref)
            d, nm, nv = _adam_update(g, w_ref[...], m_ref[...], v_ref[...])
            g_ref[...] = g
            d_ref[...] = d
            nm_ref[...] = nm
            nv_ref[...] = nv
        outs[4 * n][...] = total(row_ref, slice(0, 1), off["loss"][0], LANES)

    flat = [t for name in names for t in params[name]]
    out_shape = [jax.ShapeDtypeStruct(params[name][0].shape, f32) for name in names for _ in range(4)]
    res = pl.pallas_call(
        body, name="adamw_small",
        out_shape=out_shape + [jax.ShapeDtypeStruct((1, LANES), f32)],
        compiler_params=pltpu.CompilerParams(vmem_limit_bytes=VMEM_LIMIT),
    )(recv_row, recv_pre, recv_conv, *flat)
    return {name: res[4 * k:4 * k + 4] for k, name in enumerate(names)}, res[4 * n]


def _pad_rows8(w):
    return jnp.pad(w, ((0, 8 - w.shape[0]), (0, 0)))


def _pad_lanes(v):
    return jnp.pad(v.reshape(1, -1), ((0, 0), (0, LANES - v.size)))


WEIGHTS = ('norm_mix_pre_w', 'w_in', 'ssd_conv_w', 'ssd_conv_b', 'ssd_dt_bias', 'ssd_a_log', 'ssd_d', 'ssd_norm_w',
           'ssd_w_out', 'attn_sinks', 'attn_w_out', 'w_mix_out', 'norm_mix_post_w', 'norm_ffn_pre_w', 'ffn_w_up',
           'ffn_conv_w', 'ffn_conv_b', 'ffn_w_down', 'norm_ffn_post_w')
W_IN_ROWS = IN_DIM // N_DEV
W_IN_PAD = 1104
W_IN_SPLIT = 832
TS = 512


def kernel(x, positions, norm_mix_pre_w, w_in, ssd_conv_w, ssd_conv_b, ssd_dt_bias, ssd_a_log, ssd_d, ssd_norm_w, ssd_w_out, attn_sinks, attn_w_out, w_mix_out, norm_mix_post_w, norm_ffn_pre_w, ffn_w_up, ffn_conv_w, ffn_conv_b, ffn_w_down, norm_ffn_post_w, loss_target, m_norm_mix_pre_w, m_w_in, m_ssd_conv_w, m_ssd_conv_b, m_ssd_dt_bias, m_ssd_a_log, m_ssd_d, m_ssd_norm_w, m_ssd_w_out, m_attn_sinks, m_attn_w_out, m_w_mix_out, m_norm_mix_post_w, m_norm_ffn_pre_w, m_ffn_w_up, m_ffn_conv_w, m_ffn_conv_b, m_ffn_w_down, m_norm_ffn_post_w, v_norm_mix_pre_w, v_w_in, v_ssd_conv_w, v_ssd_conv_b, v_ssd_dt_bias, v_ssd_a_log, v_ssd_d, v_ssd_norm_w, v_ssd_w_out, v_attn_sinks, v_attn_w_out, v_w_mix_out, v_norm_mix_post_w, v_norm_ffn_pre_w, v_ffn_w_up, v_ffn_conv_w, v_ffn_conv_b, v_ffn_w_down, v_norm_ffn_post_w):
    a = locals()
    r2 = lambda t: t.reshape(t.shape[-2], t.shape[-1])
    w = {n: r2(a[n]) for n in WEIGHTS}
    m = {n: r2(a["m_" + n]) for n in WEIGHTS}
    v = {n: r2(a["v_" + n]) for n in WEIGHTS}
    xs, target = x[0], loss_target[0]
    S = xs.shape[0]
    ts = TS

    w_in_blk = jnp.pad(w["w_in"].T.astype(bf16), ((0, W_IN_PAD - W_IN_ROWS), (0, 0)))
    conv_blk = jnp.concatenate([_pad_rows8(w["ssd_conv_w"]), _pad_rows8(w["ffn_conv_w"]),
                                jnp.zeros((8, CONV_BLOCK - SSD_CONV_COLS - FFN_CONV_COLS), f32)], axis=1)
    u, (g_in, g_conv) = prenorm_fwd(xs, w["norm_mix_pre_w"], ts, [w_in_blk, conv_blk])
    wt = g_in[:, :W_IN_ROWS].reshape(IN_DIM, D)
    w_main_t = jnp.concatenate([wt[IN_OFF[0]:IN_OFF[1]], wt[IN_OFF[6]:IN_OFF[8]], wt[IN_OFF[1]:IN_OFF[2]]], axis=0)
    w_q_t = wt[IN_OFF[3]:IN_OFF[4]]
    w_kv_t = wt[IN_OFF[4]:IN_OFF[6]]
    w_dt_t = jnp.pad(wt[IN_OFF[2]:IN_OFF[3]], ((0, LANES - NH), (0, 0)))
    conv_w8 = g_conv[:, :, 0:SSD_CONV_COLS].transpose(1, 0, 2).reshape(8, CONVD)
    fconv_w8 = g_conv[:, :, SSD_CONV_COLS:SSD_CONV_COLS + FFN_CONV_COLS].transpose(1, 0, 2).reshape(8, 2 * FF)
    bias = _pad_lanes(w["ssd_dt_bias"])
    alog = _pad_lanes(w["ssd_a_log"])
    dx_row = jnp.repeat(w["ssd_d"].reshape(-1), HD).reshape(1, DI)
    sinks = _pad_lanes(w["attn_sinks"])

    later = [w["ssd_w_out"].astype(bf16), w["attn_w_out"].astype(bf16), w["w_mix_out"].astype(bf16)]
    proj, (g_so, g_ao, g_mix) = mm(u, w_main_t, "nt", bf16, "mm_proj", comm=(later, (False,) * 3))
    w_ssd_out, w_attn_out, w_mix = g_so.reshape(DI, D), g_ao.reshape(D, D), g_mix.reshape(D, D)
    qt = mm(w_q_t, u, "nt", bf16, "mm_q")
    kvt = mm(w_kv_t, u, "nt", bf16, "mm_kv")
    dtr = mm(u, w_dt_t, "nt", f32, "mm_dt")
    xbc, conv_c = ssdconv_fwd(proj, conv_w8, w["ssd_conv_b"], ts)
    y, hprev, (g_up, g_down) = ssd_fwd(xbc, dtr, bias, alog, dx_row, comm=(
        [w["ffn_w_up"].T.astype(bf16), w["ffn_w_down"].astype(bf16)], (False, False)))
    w_up_t = g_up.reshape(2 * FF, D)
    w_down = g_down.reshape(FF, D)
    yn = gnorm_fwd(y, proj, w["ssd_norm_w"], ts)
    ys = mm(yn, w_ssd_out, "nn", bf16, "mm_ssd_out")
    cos, sin = rope_tables(positions, ts)
    aot = attn_fwd(qt, kvt, cos, sin, sinks)
    ya = mm(aot, w_attn_out, "tn", bf16, "mm_attn_out")
    merged = merge_fwd(proj, ys, ya, ts)
    mo = mm(merged, w_mix, "nn", bf16, "mm_mix")
    x1, h = post_fwd(xs, mo, w["norm_mix_post_w"], w["norm_ffn_pre_w"], ts)
    up = mm(h, w_up_t, "nt", bf16, "mm_up")
    act, gate_c, val_c = ffnact_fwd(up, fconv_w8, w["ffn_conv_b"], ts)
    ff = mm(act, w_down, "nn", bf16, "mm_down")
    loss_blk, dout, dff, g_post2 = loss_head(x1, ff, target, w["norm_ffn_post_w"], ts)

    dact = mm(dff, w_down, "nt", bf16, "mm_dact")
    gw_down = mm(act, dff, "tn", bf16, "mm_g_down")
    dgate, dval = ffnact_bwd(dact, gate_c, val_c, ts)
    dup_pre, g_fconv_a = dwconv_bwd(dgate, up, 0, fconv_w8, 0, FFN_K, FF, ts, "ffnconv_bwd_gate", out_cols=2 * FF)
    dup_pre, g_fconv_b = dwconv_bwd(dval, up, 1, fconv_w8, 1, FFN_K, FF, ts, "ffnconv_bwd_val", into=dup_pre, ocb=1,
                                    out_cols=2 * FF)
    g_fconv = jnp.concatenate([g_fconv_a, g_fconv_b], axis=1)
    dh, (r_down,) = mm(dup_pre, w_up_t, "nn", bf16, "mm_dh", comm=([gw_down.reshape(N_DEV, FF // N_DEV, D)], (True,)))
    gw_up_t = mm(dup_pre, h, "tn", bf16, "mm_g_up")
    dx1, dmo, g_norms = post_bwd(dout, dh, x1, mo, w["norm_mix_post_w"], w["norm_ffn_pre_w"], ts)
    dmerged = mm(dmo, w_mix, "nt", bf16, "mm_dmerged")
    gw_mix = mm(merged, dmo, "tn", bf16, "mm_g_mix")
    dys, dya, dproj = merge_bwd(dmerged, proj, ys, ya, ts)
    daot = mm(w_attn_out, dya, "nt", bf16, "mm_dao")
    gw_attn_out = mm(aot, dya, "nn", bf16, "mm_g_attn_out")
    dqt, dkvt, g_sinks, (r_up,) = attn_bwd(qt, kvt, cos, sin, sinks, daot,
                                           comm=([gw_up_t.reshape(N_DEV, 2 * FF // N_DEV, D)], (True,)))
    du_b = mm(dkvt, w_kv_t, "tn", bf16, "mm_du_kv")
    du_d = mm(dqt, w_q_t, "tn", bf16, "mm_du_q")
    dyn = mm(dys, w_ssd_out, "nt", bf16, "mm_dyn")
    gw_ssd_out = mm(yn, dys, "tn", bf16, "mm_g_ssd_out")
    dy, dproj, g_gnorm = gnorm_bwd(dyn, y, proj, w["ssd_norm_w"], dproj, ts)
    sends = [gw_ssd_out.reshape(N_DEV, DI // N_DEV, D), gw_attn_out.reshape(N_DEV, D // N_DEV, D),
             gw_mix.reshape(N_DEV, D // N_DEV, D)]
    dxbc, ddtr, g_ssd, (r_so, r_ao, r_mix) = ssd_bwd(xbc, dtr, dy, hprev, bias, alog, dx_row, comm=(sends, (True,) * 3))
    dproj, g_conv_w = dwconv_bwd(dxbc, proj, C_XBC // 1024, conv_w8, 0, SSD_K, 1024, ts, "ssdconv_bwd", act_c=conv_c,
                                 into=dproj, ocb=C_XBC // 1024, out_cols=PM)
    ddtr_b = ddtr.astype(bf16)
    du_c = mm(ddtr_b, w_dt_t, "nn", bf16, "mm_du_dt")
    g_main_t = mm(dproj, u, "tn", bf16, "mm_g_in")
    g_q_t = mm(dqt, u, "nn", bf16, "mm_g_q")
    g_kv_t = mm(dkvt, u, "nn", bf16, "mm_g_kv")
    g_dt_t = mm(ddtr_b, u, "tn", bf16, "mm_g_dt")
    g_wt = jnp.concatenate([g_main_t[C_Z:C_GS], g_main_t[C_XBC:PM], g_dt_t[:NH], g_q_t, g_kv_t, g_main_t[C_GS:C_XBC]],
                           axis=0)
    send_in = jnp.pad(g_wt.reshape(N_DEV, W_IN_ROWS, D), ((0, 0), (0, W_IN_PAD - W_IN_ROWS), (0, 0)))
    pieces = {"norm_mix_post_w": g_norms[1:2], "norm_ffn_pre_w": g_norms[0:1], "norm_ffn_post_w": g_post2[0:1],
              "ssd_norm_w": g_gnorm[0:1], "ssd_conv_b": g_conv_w[7:8], "ffn_conv_b": g_fconv[7:8],
              "ssd_dt_bias": g_ssd[0:1], "ssd_a_log": g_ssd[1:2], "ssd_d": g_ssd[2:3], "attn_sinks": g_sinks[0:1],
              "loss": loss_blk[0:1]}
    row = jnp.concatenate([jnp.pad(pieces[n][:, :min(k, pieces[n].shape[1])],
                                   ((0, 0), (0, -(-k // LANES) * LANES - min(k, pieces[n].shape[1]))))
                           for n, k in SMALL_ROW], axis=1)
    send_row = jnp.pad(row, ((0, 7), (0, 0)))
    send_conv = jnp.concatenate(
        [g_conv_w.reshape(8, N_DEV, SSD_CONV_COLS).transpose(1, 0, 2),
         g_fconv.reshape(8, N_DEV, FFN_CONV_COLS).transpose(1, 0, 2),
         jnp.zeros((N_DEV, 8, CONV_BLOCK - SSD_CONV_COLS - FFN_CONV_COLS), f32)], axis=2)
    du_a, (r_in_a, recv_row, recv_conv) = mm(dproj, w_main_t, "nn", bf16, "mm_du", comm=(
        [send_in[:, :W_IN_SPLIT], send_row, send_conv], (True, False, True)))
    grad_x, g_pre, (r_in_b,) = prenorm_bwd(xs, w["norm_mix_pre_w"], (du_a, du_b, du_c, du_d), dx1, ts,
                                           comm=([send_in[:, W_IN_SPLIT:]], (True,)))
    (recv_pre,) = exchange([g_pre], (False,), "gather_last")

    r_in = jnp.concatenate([r_in_a, r_in_b], axis=1)
    tpad = lambda t: jnp.pad(t.T, ((0, W_IN_PAD - W_IN_ROWS), (0, 0)))
    o_in = [t[:W_IN_ROWS].T for t in adamw(r_in, tpad(w["w_in"]), tpad(m["w_in"]), tpad(v["w_in"]), 368, "adamw_w_in")]
    o_up = [t.T for t in adamw(r_up, w["ffn_w_up"].T, m["ffn_w_up"].T, v["ffn_w_up"].T, 352, "adamw_w_up")]
    big = {"w_in": o_in, "ffn_w_up": o_up,
           "ssd_w_out": adamw(r_so, w["ssd_w_out"], m["ssd_w_out"], v["ssd_w_out"], 256, "adamw_ssd_out"),
           "attn_w_out": adamw(r_ao, w["attn_w_out"], m["attn_w_out"], v["attn_w_out"], 128, "adamw_attn_out"),
           "w_mix_out": adamw(r_mix, w["w_mix_out"], m["w_mix_out"], v["w_mix_out"], 128, "adamw_mix"),
           "ffn_w_down": adamw(r_down, w["ffn_w_down"], m["ffn_w_down"], v["ffn_w_down"], 352, "adamw_down")}
    small_names = [n for n in WEIGHTS if n not in big]
    small, loss_row = adamw_small(recv_row, recv_pre, recv_conv, {n: (w[n], m[n], v[n]) for n in small_names})

    outs = [loss_row[0, 0], grad_x[None]]
    for k in range(4):
        for n in WEIGHTS:
            outs.append((big[n][k] if n in big else small[n][k]).reshape(a[n].shape))
    return tuple(outs)
```

```python
import jax
import jax.numpy as jnp
import numpy as np
from jax import lax
from jax.experimental import pallas as pl
from jax.experimental.pallas import tpu as pltpu

f32 = jnp.float32
bf16 = jnp.bfloat16

N_DEV = 8
D = 1024
DI = 2048
NH = 32
HD = 64
NG = 4
GW = DI // NG
NS = 128
CH = 128
CONVD = DI + 2 * NG * NS
SSD_K = 4
AH = 16
AD = 64
KVH = 4
REP = AH // KVH
KVW = KVH * AD
WIN = 128
FF = 2816
FFN_K = 3
EPS = 1e-6
ROPE_THETA = 10000.0
LANES = 128
RG = 16
CW = 256

C_Z, C_GS, C_GA, C_XBC, PM = 0, 2048, 3072, 4096, 7168
IN_SIZES = (DI, CONVD, NH, D, KVW, KVW, D, D)
IN_OFF = tuple(int(v) for v in np.cumsum((0,) + IN_SIZES))
IN_DIM = IN_OFF[-1]

ADAM_LR, ADAM_B1, ADAM_B2, ADAM_EPS, ADAM_WD, ADAM_STEP = 0.001, 0.9, 0.999, 1e-08, 0.01, 10

VMEM_LIMIT = 56 * 1024 * 1024


def _cp(*sem, side_effects=False):
    return pltpu.CompilerParams(dimension_semantics=sem, vmem_limit_bytes=VMEM_LIMIT, has_side_effects=side_effects)


def _dot(a, b, mode="nn"):
    dims = {"nn": (((1,), (0,)), ((), ())), "nt": (((1,), (1,)), ((), ())), "tn": (((0,), (0,)), ((), ()))}[mode]
    return lax.dot_general(a, b, dims, preferred_element_type=f32)


def _split3(v):
    hi = v.astype(bf16)
    r = v - hi.astype(f32)
    mid = r.astype(bf16)
    lo = (r - mid.astype(f32)).astype(bf16)
    return hi, mid, lo


def _dot3_left(m01, v):
    hi, mid, lo = _split3(v)
    return _dot(m01, hi) + _dot(m01, mid) + _dot(m01, lo)


def _dot3_right(v, m01):
    hi, mid, lo = _split3(v)
    return _dot(hi, m01) + _dot(mid, m01) + _dot(lo, m01)


def _dot2_right(v, m01):
    hi = v.astype(bf16)
    lo = (v - hi.astype(f32)).astype(bf16)
    return _dot(hi, m01) + _dot(lo, m01)


def _sigmoid(x):
    return 1.0 / (1.0 + jnp.exp(-x))


def _sigmoid_fast(x):
    return pl.reciprocal(1.0 + jnp.exp(-x), approx=True)


def _peer(k, x, y, c):
    return ((1 - x) if k & 4 else x, (1 - y) if k & 2 else y, (1 - c) if k & 1 else c)


def _xchg_copies(buf_refs, out_refs, send_sems, recv_sems, local_sems, personalised):
    x, y, c = lax.axis_index("x"), lax.axis_index("y"), lax.axis_index("c")
    me = 4 * x + 2 * y + c
    local, remote = [], []
    for b, (buf, out, pers) in enumerate(zip(buf_refs, out_refs, personalised)):
        local.append(pltpu.make_async_copy(buf.at[me] if pers else buf, out.at[me], local_sems.at[b]))
        for k in range(1, N_DEV):
            px, py, pc = _peer(k, x, y, c)
            s = b * (N_DEV - 1) + k - 1
            remote.append(pltpu.make_async_remote_copy(
                src_ref=buf.at[4 * px + 2 * py + pc] if pers else buf, dst_ref=out.at[me],
                send_sem=send_sems.at[s], recv_sem=recv_sems.at[s],
                device_id=(px, py, pc), device_id_type=pl.DeviceIdType.MESH))
    return local, remote


class _Comm:
    def __init__(self, comm):
        self.bufs, self.pers = comm if comm else ((), ())
        self.n = len(self.bufs)

    def in_specs(self):
        return [pl.BlockSpec(memory_space=pl.ANY)] * self.n

    out_specs = in_specs

    def out_shape(self):
        return [jax.ShapeDtypeStruct((N_DEV,) + tuple(b.shape[1:] if p else b.shape), b.dtype)
                for b, p in zip(self.bufs, self.pers)]

    def scratch(self):
        n = self.n
        return [pltpu.SemaphoreType.DMA((n * (N_DEV - 1),)), pltpu.SemaphoreType.DMA((n * (N_DEV - 1),)),
                pltpu.SemaphoreType.DMA((n,))] if n else []

    def split(self, refs, n_in, n_out):
        n = self.n
        ins, outs = refs[:n_in], refs[n_in + n:n_in + n + n_out]
        rest = refs[n_in + n + n_out + n:]
        if not n:
            return ins, outs, rest, None
        copies = _xchg_copies(refs[n_in:n_in + n], refs[n_in + n + n_out:n_in + n + n_out + n], *rest[-3:], self.pers)
        return ins, outs, rest[:-3], copies

    def start(self, copies, first):
        if copies:
            @pl.when(first)
            def _():
                for cp in copies[0] + copies[1]:
                    cp.start()

    def wait(self, copies, last):
        if copies:
            @pl.when(last)
            def _():
                for cp in copies[1]:
                    cp.wait_recv()
                for cp in copies[1]:
                    cp.wait_send()
                for cp in copies[0]:
                    cp.wait()


def exchange(bufs, personalised, name):
    cm = _Comm((bufs, personalised))

    def body(*refs):
        _, _, _, copies = cm.split(refs, 0, 0)
        cm.start(copies, True)
        cm.wait(copies, True)

    return pl.pallas_call(
        body, name=name, in_specs=cm.in_specs(), out_specs=cm.out_specs(), out_shape=cm.out_shape(),
        scratch_shapes=cm.scratch(), compiler_params=pltpu.CompilerParams(has_side_effects=True),
    )(*bufs)


class _TwoLevelGather:
    def __init__(self, bufs):
        self.bufs = list(bufs)
        self.n = len(self.bufs)

    def in_specs(self):
        return [pl.BlockSpec(memory_space=pl.ANY)] * self.n

    out_specs = in_specs

    def out_shape(self):
        return [jax.ShapeDtypeStruct((N_DEV,) + tuple(b.shape), b.dtype) for b in self.bufs]

    def scratch(self):
        per = N_DEV - 1
        return [pltpu.SemaphoreType.DMA((self.n * per,)), pltpu.SemaphoreType.DMA((self.n * per,)),
                pltpu.SemaphoreType.DMA((self.n,))]

    def bind(self, ins, outs, send_sems, recv_sems, local_sems):
        n, per = self.n, N_DEV - 1
        x, y, c = lax.axis_index("x"), lax.axis_index("y"), lax.axis_index("c")
        me, sibling = (x, y, c), (x, y, 1 - c)
        chips = [(1 - x, y), (x, 1 - y), (1 - x, 1 - y)]

        def copy(b, k, block, to, src=None):
            dst = outs[b].at[4 * block[0] + 2 * block[1] + block[2]]
            return pltpu.make_async_remote_copy(
                src_ref=dst if src is None else src, dst_ref=dst,
                send_sem=send_sems.at[b * per + k], recv_sem=recv_sems.at[b * per + k],
                device_id=to, device_id_type=pl.DeviceIdType.MESH)

        mine = [pltpu.make_async_copy(ins[b], outs[b].at[4 * x + 2 * y + c], local_sems.at[b]) for b in range(n)]
        first = []
        for b in range(n):
            first.append(copy(b, 0, me, sibling, src=ins[b]))
            first += [copy(b, 1 + j, me, (*chip, c), src=ins[b]) for j, chip in enumerate(chips)]

        def start():
            for cp in mine + first:
                cp.start()

        def finish():
            passed = []
            for j, chip in enumerate(chips):
                for b in range(n):
                    copy(b, 1 + j, (*chip, c), me).wait_recv()
                    passed.append(copy(b, 4 + j, (*chip, c), sibling))
                    passed[-1].start()
            for b in range(n):
                copy(b, 0, sibling, me).wait_recv()
                for j, chip in enumerate(chips):
                    copy(b, 4 + j, (*chip, 1 - c), me).wait_recv()
            for cp in first + passed:
                cp.wait_send()
            for cp in mine:
                cp.wait()

        return start, finish


MM_TILES = (3584, 2176, 2048, 1792, 1408, 1024, 512, 256, 128)
MM_VMEM_BUDGET = 40 * 1024 * 1024


def _mm_tiles(M, N, K, out_bytes):
    cm = [t for t in MM_TILES if M % t == 0]
    cn = [t for t in MM_TILES if N % t == 0]
    ck = [t for t in MM_TILES if K % t == 0]
    best = None
    for bm in cm[:2]:
        for bn in cn:
            for bk in ck:
                need = 4 * (bm * bk + bk * bn) + bm * bn * (4 + 2 * out_bytes)
                if need <= MM_VMEM_BUDGET:
                    score = (bm * bn * bk, bk)
                    if best is None or score > best[0]:
                        best = (score, (bm, bn, bk))
    return best[1]


def mm(a, b, mode, out_dtype, name, comm=None):
    if mode == "nn":
        (M, K), (_, N) = a.shape, b.shape
    elif mode == "nt":
        (M, K), (N, _) = a.shape, b.shape
    else:
        (K, M), (_, N) = a.shape, b.shape
    bm, bn, bk = _mm_tiles(M, N, K, jnp.dtype(out_dtype).itemsize)
    gm, gn, nk = M // bm, N // bn, K // bk
    cm = _Comm(comm)

    def body(*refs):
        (a_ref, b_ref), (o_ref,), scr, copies = cm.split(refs, 2, 1)
        i, j, k = pl.program_id(0), pl.program_id(1), pl.program_id(2)
        cm.start(copies, jnp.logical_and(jnp.logical_and(i == 0, j == 0), k == 0))
        p = _dot(a_ref[...], b_ref[...], mode)
        if nk == 1:
            o_ref[...] = p.astype(o_ref.dtype)
        else:
            acc_ref = scr[0]

            @pl.when(k == 0)
            def _():
                acc_ref[...] = p

            @pl.when(k > 0)
            def _():
                acc_ref[...] += p

            @pl.when(k == nk - 1)
            def _():
                o_ref[...] = acc_ref[...].astype(o_ref.dtype)

        cm.wait(copies, jnp.logical_and(jnp.logical_and(i == gm - 1, j == gn - 1), k == nk - 1))

    if mode == "nn":
        a_spec = pl.BlockSpec((bm, bk), lambda i, j, k: (i, k))
        b_spec = pl.BlockSpec((bk, bn), lambda i, j, k: (k, j))
    elif mode == "nt":
        a_spec = pl.BlockSpec((bm, bk), lambda i, j, k: (i, k))
        b_spec = pl.BlockSpec((bn, bk), lambda i, j, k: (j, k))
    else:
        a_spec = pl.BlockSpec((bk, bm), lambda i, j, k: (k, i))
        b_spec = pl.BlockSpec((bk, bn), lambda i, j, k: (k, j))
    sem = ("arbitrary",) * 3 if cm.n else ("parallel", "parallel", "arbitrary")
    res = pl.pallas_call(
        body, name=name, grid=(gm, gn, nk),
        in_specs=[a_spec, b_spec] + cm.in_specs(),
        out_specs=[pl.BlockSpec((bm, bn), lambda i, j, k: (i, j))] + cm.out_specs(),
        out_shape=[jax.ShapeDtypeStruct((M, N), out_dtype)] + cm.out_shape(),
        scratch_shapes=([pltpu.VMEM((bm, bn), f32)] if nk > 1 else []) + cm.scratch(),
        compiler_params=_cp(*sem, side_effects=bool(cm.n)),
    )(a, b, *cm.bufs)
    return (res[0], res[1:]) if cm.n else res[0]


def _groups(ts, fn, carry=None, reverse=False, unroll=8, rg=RG):
    n = ts // rg
    if n == 1:
        return fn(0, carry)
    unroll = min(unroll, n)
    span = rg * unroll

    def body(g, c):
        r0 = pl.multiple_of((n // unroll - 1 - g if reverse else g) * span, span)
        for u in (range(unroll - 1, -1, -1) if reverse else range(unroll)):
            c = fn(pl.multiple_of(r0 + u * rg, rg), c)
        return c

    return lax.fori_loop(0, n // unroll, body, carry)


def _rms(x):
    return lax.rsqrt(jnp.mean(x * x, axis=-1, keepdims=True) + EPS)


def _rms_bwd(x, r, dn):
    n = x * r
    return r * (dn - n * jnp.mean(dn * n, axis=-1, keepdims=True))


NRG = 256


def _fold(x):
    return jnp.sum(x.reshape(x.shape[0] // 8, 8, x.shape[1]), axis=0)


def _flush(acc_ref, out_ref, row):
    out_ref[row:row + 1, :] = jnp.sum(acc_ref[...], axis=0, keepdims=True)


def prenorm_fwd(x, w, ts, gather):
    S = x.shape[0]
    nt = S // ts
    tg = _TwoLevelGather(gather)
    n = tg.n

    def body(x_ref, w_ref, *refs):
        u_ref = refs[n]
        start, finish = tg.bind(refs[:n], refs[n + 1:2 * n + 1], *refs[2 * n + 1:])
        i = pl.program_id(0)
        pl.when(i == 0)(start)
        wv = w_ref[...]

        def grp(r0, _):
            xv = x_ref[pl.ds(r0, NRG), :]
            u_ref[pl.ds(r0, NRG), :] = (xv * _rms(xv) * wv).astype(bf16)

        _groups(ts, grp, rg=NRG)
        pl.when(i == nt - 1)(finish)

    res = pl.pallas_call(
        body, name="prenorm_fwd", grid=(nt,),
        in_specs=[pl.BlockSpec((ts, D), lambda i: (i, 0)), pl.BlockSpec((1, D), lambda i: (0, 0))] + tg.in_specs(),
        out_specs=[pl.BlockSpec((ts, D), lambda i: (i, 0))] + tg.out_specs(),
        out_shape=[jax.ShapeDtypeStruct((S, D), bf16)] + tg.out_shape(),
        scratch_shapes=tg.scratch(),
        compiler_params=_cp("arbitrary", side_effects=True),
    )(x, w, *tg.bufs)
    return res[0], res[1:]


def prenorm_bwd(x, w, dus, dx1, ts, comm=None):
    S = x.shape[0]
    nt = S // ts
    nd = len(dus)
    cm = _Comm(comm)

    def body(*refs):
        ins, (gx_ref, gw_ref), (acc_ref,), copies = cm.split(refs, nd + 3, 2)
        x_ref, w_ref = ins[:2]
        du_refs, dx1_ref = ins[2:2 + nd], ins[2 + nd]
        i = pl.program_id(0)
        cm.start(copies, i == 0)
        wv = w_ref[...]

        @pl.when(i == 0)
        def _():
            acc_ref[...] = jnp.zeros_like(acc_ref)
            gw_ref[...] = jnp.zeros_like(gw_ref)

        def grp(r0, _):
            rows = pl.ds(r0, NRG)
            xv = x_ref[rows, :]
            r = _rms(xv)
            du = du_refs[0][rows, :].astype(f32)
            for d_ref in du_refs[1:]:
                du = du + d_ref[rows, :].astype(f32)
            gx_ref[rows, :] = dx1_ref[rows, :] + _rms_bwd(xv, r, du * wv)
            acc_ref[...] += _fold(du * xv * r)

        _groups(ts, grp, rg=NRG)

        @pl.when(i == nt - 1)
        def _():
            _flush(acc_ref, gw_ref, 0)

        cm.wait(copies, i == nt - 1)

    row = pl.BlockSpec((ts, D), lambda i: (i, 0))
    res = pl.pallas_call(
        body, name="prenorm_bwd", grid=(nt,),
        in_specs=[row, pl.BlockSpec((1, D), lambda i: (0, 0))] + [row] * (nd + 1) + cm.in_specs(),
        out_specs=[row, pl.BlockSpec((8, D), lambda i: (0, 0))] + cm.out_specs(),
        out_shape=[jax.ShapeDtypeStruct((S, D), f32), jax.ShapeDtypeStruct((8, D), f32)] + cm.out_shape(),
        scratch_shapes=[pltpu.VMEM((8, D), f32)] + cm.scratch(),
        compiler_params=_cp("arbitrary", side_effects=bool(cm.n)),
    )(x, w, *dus, dx1, *cm.bufs)
    return res[0], res[1], res[2:]


def post_fwd(x, mo, w_post, w_pre2, ts):
    S = x.shape[0]

    def body(x_ref, mo_ref, wp_ref, w2_ref, x1_ref, h_ref):
        wp, w2 = wp_ref[...], w2_ref[...]

        def grp(r0, _):
            rows = pl.ds(r0, NRG)
            mv = mo_ref[rows, :].astype(f32)
            x1 = x_ref[rows, :] + mv * _rms(mv) * wp
            x1_ref[rows, :] = x1
            h_ref[rows, :] = (x1 * _rms(x1) * w2).astype(bf16)

        _groups(ts, grp, rg=NRG)

    row = pl.BlockSpec((ts, D), lambda i: (i, 0))
    par = pl.BlockSpec((1, D), lambda i: (0, 0))
    return pl.pallas_call(
        body, name="post_fwd", grid=(S // ts,),
        in_specs=[row, row, par, par], out_specs=[row, row],
        out_shape=[jax.ShapeDtypeStruct((S, D), f32), jax.ShapeDtypeStruct((S, D), bf16)],
        compiler_params=_cp("parallel"),
    )(x, mo, w_post, w_pre2)


def post_bwd(dout, dh, x1, mo, w_post, w_pre2, ts):
    S = x1.shape[0]
    nt = S // ts

    def body(dout_ref, dh_ref, x1_ref, mo_ref, wp_ref, w2_ref, dx1_ref, dmo_ref, gw_ref, acc2_ref, accp_ref):
        i = pl.program_id(0)
        wp, w2 = wp_ref[...], w2_ref[...]

        @pl.when(i == 0)
        def _():
            acc2_ref[...] = jnp.zeros_like(acc2_ref)
            accp_ref[...] = jnp.zeros_like(accp_ref)
            gw_ref[...] = jnp.zeros_like(gw_ref)

        def grp(r0, _):
            rows = pl.ds(r0, NRG)
            x1 = x1_ref[rows, :]
            r1 = _rms(x1)
            dh = dh_ref[rows, :].astype(f32)
            dx1 = dout_ref[rows, :] + _rms_bwd(x1, r1, dh * w2)
            dx1_ref[rows, :] = dx1
            acc2_ref[...] += _fold(dh * x1 * r1)
            mv = mo_ref[rows, :].astype(f32)
            rm = _rms(mv)
            dmo_ref[rows, :] = _rms_bwd(mv, rm, dx1 * wp).astype(bf16)
            accp_ref[...] += _fold(dx1 * mv * rm)

        _groups(ts, grp, rg=NRG)

        @pl.when(i == nt - 1)
        def _():
            _flush(acc2_ref, gw_ref, 0)
            _flush(accp_ref, gw_ref, 1)

    row = pl.BlockSpec((ts, D), lambda i: (i, 0))
    par = pl.BlockSpec((1, D), lambda i: (0, 0))
    return pl.pallas_call(
        body, name="post_bwd", grid=(nt,),
        in_specs=[row, row, row, row, par, par],
        out_specs=[row, row, pl.BlockSpec((8, D), lambda i: (0, 0))],
        out_shape=[jax.ShapeDtypeStruct((S, D), f32), jax.ShapeDtypeStruct((S, D), bf16),
                   jax.ShapeDtypeStruct((8, D), f32)],
        scratch_shapes=[pltpu.VMEM((8, D), f32), pltpu.VMEM((8, D), f32)],
        compiler_params=_cp("arbitrary"),
    )(dout, dh, x1, mo, w_post, w_pre2)


def loss_head(x1, ff, target, w, ts):
    S = x1.shape[0]
    nt = S // ts

    def body(x1_ref, ff_ref, t_ref, w_ref, loss_ref, dout_ref, dff_ref, gw_ref, accw_ref, accl_ref):
        i = pl.program_id(0)
        wv = w_ref[...]

        @pl.when(i == 0)
        def _():
            accw_ref[...] = jnp.zeros_like(accw_ref)
            accl_ref[...] = jnp.zeros_like(accl_ref)
            gw_ref[...] = jnp.zeros_like(gw_ref)

        def grp(r0, _):
            rows = pl.ds(r0, NRG)
            fv = ff_ref[rows, :].astype(f32)
            r = _rms(fv)
            n = fv * r
            e = x1_ref[rows, :] + n * wv - t_ref[rows, :]
            dout = e * (1.0 / D)
            dout_ref[rows, :] = dout
            dff_ref[rows, :] = _rms_bwd(fv, r, dout * wv).astype(bf16)
            accw_ref[...] += _fold(dout * n)
            accl_ref[...] += _fold(e * e)

        _groups(ts, grp, rg=NRG)

        @pl.when(i == nt - 1)
        def _():
            _flush(accw_ref, gw_ref, 0)
            tot = jnp.sum(jnp.sum(accl_ref[...], axis=1, keepdims=True), axis=0, keepdims=True) * (0.5 / D)
            loss_ref[...] = jnp.broadcast_to(tot, loss_ref.shape)

    row = pl.BlockSpec((ts, D), lambda i: (i, 0))
    return pl.pallas_call(
        body, name="loss_head", grid=(nt,),
        in_specs=[row, row, row, pl.BlockSpec((1, D), lambda i: (0, 0))],
        out_specs=[pl.BlockSpec((8, LANES), lambda i: (0, 0)), row, row, pl.BlockSpec((8, D), lambda i: (0, 0))],
        out_shape=[jax.ShapeDtypeStruct((8, LANES), f32), jax.ShapeDtypeStruct((S, D), f32),
                   jax.ShapeDtypeStruct((S, D), bf16), jax.ShapeDtypeStruct((8, D), f32)],
        scratch_shapes=[pltpu.VMEM((8, D), f32), pltpu.VMEM((8, D), f32)],
        compiler_params=_cp("arbitrary"),
    )(x1, ff, target, w)


def _taps(w_ref, cs, K):
    return [jnp.broadcast_to(w_ref[k:k + 1, cs], (8, CW)) for k in range(K)]


def _down(before, cur, s, sub):
    return jnp.where(sub < s, pltpu.roll(before, s, 0), pltpu.roll(cur, s, 0))


def _up(cur, after, s, sub):
    return jnp.where(sub < 8 - s, pltpu.roll(cur, 8 - s, 0), pltpu.roll(after, 8 - s, 0))


def _conv_group(p, a, b, taps, bias, K, sub):
    ya, yb = bias, bias
    for k in range(K):
        s = K - 1 - k
        xa, xb = (a, b) if s == 0 else (_down(p, a, s, sub), _down(a, b, s, sub))
        ya = ya + taps[k] * xa
        yb = yb + taps[k] * xb
    return ya, yb


def _prev8_map(ts, cb):
    return lambda i, j: (jnp.maximum(i * (ts // 8) - 1, 0), cb + j)


def ssdconv_fwd(proj, w8, b, ts):
    S = proj.shape[0]
    bw = 1024
    cb = C_XBC // bw

    def body(cur_ref, prev_ref, w_ref, b_ref, o_ref, c_ref):
        first = pl.program_id(0) == 0
        sub = lax.broadcasted_iota(jnp.int32, (8, CW), 0)
        for c0 in range(0, bw, CW):
            cs = slice(c0, c0 + CW)
            taps = _taps(w_ref, cs, SSD_K)
            bias = jnp.broadcast_to(b_ref[:, cs], (8, CW))

            def grp(r0, p, cs=cs, taps=taps, bias=bias):
                rows = pl.ds(r0, RG)
                xv = cur_ref[rows, cs].astype(f32)
                ya, yb = _conv_group(p, xv[0:8], xv[8:16], taps, bias, SSD_K, sub)
                y = jnp.concatenate([ya, yb], axis=0)
                c_ref[rows, cs] = y.astype(bf16)
                o_ref[rows, cs] = (y * _sigmoid_fast(y)).astype(bf16)
                return xv[8:16]

            _groups(ts, grp, jnp.where(first, 0.0, prev_ref[:, cs].astype(f32)))

    o = jax.ShapeDtypeStruct((S, CONVD), bf16)
    blk = pl.BlockSpec((ts, bw), lambda i, j: (i, j))
    return pl.pallas_call(
        body, name="ssdconv_fwd", grid=(S // ts, CONVD // bw),
        in_specs=[pl.BlockSpec((ts, bw), lambda i, j: (i, cb + j)),
                  pl.BlockSpec((8, bw), _prev8_map(ts, cb)),
                  pl.BlockSpec((8, bw), lambda i, j: (0, j)),
                  pl.BlockSpec((1, bw), lambda i, j: (0, j))],
        out_specs=[blk, blk], out_shape=[o, o],
        compiler_params=_cp("parallel", "parallel"),
    )(proj, proj, w8, b)


def _gelu_tanh(x):
    c = 0.7978845608028654
    t = jnp.tanh(c * (x + 0.044715 * x * x * x))
    return 0.5 * x * (1.0 + t), t


def ffnact_fwd(up, w8, b, ts):
    S = up.shape[0]

    def body(g_ref, gp_ref, v_ref, vp_ref, wg_ref, wv_ref, bg_ref, bv_ref, o_ref, gc_ref, vc_ref):
        first = pl.program_id(0) == 0
        sub = lax.broadcasted_iota(jnp.int32, (8, CW), 0)
        for c0 in range(0, FF, CW):
            cs = slice(c0, c0 + CW)
            tg, tv = _taps(wg_ref, cs, FFN_K), _taps(wv_ref, cs, FFN_K)
            bg = jnp.broadcast_to(bg_ref[:, cs], (8, CW))
            bv = jnp.broadcast_to(bv_ref[:, cs], (8, CW))

            def grp(r0, carry, cs=cs, tg=tg, tv=tv, bg=bg, bv=bv):
                pg, pv = carry
                rows = pl.ds(r0, RG)
                gx = g_ref[rows, cs].astype(f32)
                vx = v_ref[rows, cs].astype(f32)
                g = jnp.concatenate(_conv_group(pg, gx[0:8], gx[8:16], tg, bg, FFN_K, sub), axis=0)
                v = jnp.concatenate(_conv_group(pv, vx[0:8], vx[8:16], tv, bv, FFN_K, sub), axis=0)
                gc_ref[rows, cs] = g.astype(bf16)
                vc_ref[rows, cs] = v.astype(bf16)
                o_ref[rows, cs] = (_gelu_tanh(g)[0] * v).astype(bf16)
                return gx[8:16], vx[8:16]

            _groups(ts, grp, (jnp.where(first, 0.0, gp_ref[:, cs].astype(f32)),
                              jnp.where(first, 0.0, vp_ref[:, cs].astype(f32))))

    o = jax.ShapeDtypeStruct((S, FF), bf16)
    blk = pl.BlockSpec((ts, FF), lambda i: (i, 0))
    prev = lambda cb: pl.BlockSpec((8, FF), lambda i: (jnp.maximum(i * (ts // 8) - 1, 0), cb))
    return pl.pallas_call(
        body, name="ffnact_fwd", grid=(S // ts,),
        in_specs=[blk, prev(0), pl.BlockSpec((ts, FF), lambda i: (i, 1)), prev(1),
                  pl.BlockSpec((8, FF), lambda i: (0, 0)), pl.BlockSpec((8, FF), lambda i: (0, 1)),
                  pl.BlockSpec((1, FF), lambda i: (0, 0)), pl.BlockSpec((1, FF), lambda i: (0, 1))],
        out_specs=[blk, blk, blk], out_shape=[o, o, o],
        compiler_params=_cp("parallel"),
    )(up, up, up, up, w8, w8, b, b)


def ffnact_bwd(dact, gc, vc, ts):
    S = dact.shape[0]

    def body(d_ref, g_ref, v_ref, dg_ref, dv_ref):
        c = 0.7978845608028654
        for c0 in range(0, FF, CW):
            cs = slice(c0, c0 + CW)

            def grp(r0, _, cs=cs):
                rows = pl.ds(r0, RG)
                d = d_ref[rows, cs].astype(f32)
                g = g_ref[rows, cs].astype(f32)
                ge, t = _gelu_tanh(g)
                dgelu = 0.5 * (1.0 + t) + 0.5 * g * (1.0 - t * t) * c * (1.0 + 3.0 * 0.044715 * g * g)
                dg_ref[rows, cs] = (d * v_ref[rows, cs].astype(f32) * dgelu).astype(bf16)
                dv_ref[rows, cs] = (d * ge).astype(bf16)

            _groups(ts, grp)

    o = jax.ShapeDtypeStruct((S, FF), bf16)
    blk = pl.BlockSpec((ts, FF), lambda i: (i, 0))
    return pl.pallas_call(
        body, name="ffnact_bwd", grid=(S // ts,),
        in_specs=[blk, blk, blk], out_specs=[blk, blk], out_shape=[o, o],
        compiler_params=_cp("parallel"),
    )(dact, gc, vc)


def dwconv_bwd(dy, x, xcb, w8, wcb, K, bw, ts, name, act_c=None, into=None, ocb=0, out_cols=None):
    S, C = dy.shape
    nr = S // ts
    out_cols = out_cols or C
    n_act = 0 if act_c is None else 2

    def body(*refs):
        dy_ref, dyn_ref = refs[0:2]
        c_ref, cn_ref = (refs[2:4] if n_act else (None, None))
        x_ref, xp_ref, w_ref = refs[2 + n_act:5 + n_act]
        dx_ref, dw_ref, sd_ref = refs[-3:]
        i = pl.program_id(1)
        first, last = i == 0, i == nr - 1
        sub = lax.broadcasted_iota(jnp.int32, (8, CW), 0)

        def grad_y(d, c):
            if c is None:
                return d.astype(f32)
            cv = c.astype(f32)
            s = _sigmoid_fast(cv)
            return d.astype(f32) * s * (1.0 + cv * (1.0 - s))

        @pl.when(first)
        def _():
            dw_ref[...] = jnp.zeros_like(dw_ref)

        for c0 in range(0, bw, CW):
            cs = slice(c0, c0 + CW)
            taps = _taps(w_ref, cs, K)
            zero = jnp.zeros((8, CW), f32)

            def fwd(r0, carry, cs=cs):
                p, accs, accb = carry
                rows = pl.ds(r0, RG)
                g = grad_y(dy_ref[rows, cs], c_ref[rows, cs] if n_act else None)
                sd_ref[rows, cs] = g
                xv = x_ref[rows, cs].astype(f32)
                a, b = xv[0:8], xv[8:16]
                ga, gb = g[0:8], g[8:16]
                new = []
                for k in range(K):
                    s = K - 1 - k
                    xa, xb = (a, b) if s == 0 else (_down(p, a, s, sub), _down(a, b, s, sub))
                    new.append(accs[k] + ga * xa + gb * xb)
                return b, tuple(new), accb + ga + gb

            _, accs, accb = _groups(ts, fwd, (jnp.where(first, 0.0, xp_ref[:, cs].astype(f32)), (zero,) * K, zero))
            for k in range(K):
                dw_ref[k:k + 1, cs] += jnp.sum(accs[k], axis=0, keepdims=True)
            dw_ref[7:8, cs] += jnp.sum(accb, axis=0, keepdims=True)

            def bwd(r0, after, cs=cs, taps=taps):
                rows = pl.ds(r0, RG)
                g = sd_ref[rows, cs]
                a, b = g[0:8], g[8:16]
                da, db = zero, zero
                for k in range(K):
                    s = K - 1 - k
                    ua, ub = (a, b) if s == 0 else (_up(a, b, s, sub), _up(b, after, s, sub))
                    da = da + taps[k] * ua
                    db = db + taps[k] * ub
                dx_ref[rows, cs] = jnp.concatenate([da, db], axis=0).astype(bf16)
                return a

            halo = grad_y(dyn_ref[:, cs], cn_ref[:, cs] if n_act else None)
            _groups(ts, bwd, jnp.where(last, 0.0, halo), reverse=True)

    nxt = lambda j, i: (jnp.minimum((i + 1) * (ts // 8), S // 8 - 1), j)
    tile = pl.BlockSpec((ts, bw), lambda j, i: (i, j))
    acts = [] if act_c is None else [act_c, act_c]
    extra = [] if into is None else [into]
    n_in = 5 + n_act
    return pl.pallas_call(
        body, name=name, grid=(C // bw, nr),
        in_specs=[tile, pl.BlockSpec((8, bw), nxt)] + ([tile, pl.BlockSpec((8, bw), nxt)] if n_act else []) + [
            pl.BlockSpec((ts, bw), lambda j, i: (i, xcb + j)),
            pl.BlockSpec((8, bw), lambda j, i: (jnp.maximum(i * (ts // 8) - 1, 0), xcb + j)),
            pl.BlockSpec((8, bw), lambda j, i: (0, wcb + j))] + [pl.BlockSpec(memory_space=pl.ANY)] * len(extra),
        out_specs=[pl.BlockSpec((ts, bw), lambda j, i: (i, ocb + j)), pl.BlockSpec((8, bw), lambda j, i: (0, j))],
        out_shape=[jax.ShapeDtypeStruct((S, out_cols), bf16), jax.ShapeDtypeStruct((8, C), f32)],
        scratch_shapes=[pltpu.VMEM((ts, bw), f32)],
        input_output_aliases={n_in: 0} if extra else {},
        compiler_params=_cp("parallel", "arbitrary"),
    )(dy, dy, *acts, x, x, w8, *extra)


def gnorm_fwd(y, proj, w, ts):
    S = y.shape[0]

    def body(y_ref, z_ref, w_ref, o_ref):
        for k in range(NG):
            sl = slice(k * GW, (k + 1) * GW)
            wv = w_ref[:, sl]

            def grp(r0, _, sl=sl, wv=wv):
                rows = pl.ds(r0, NRG)
                z = z_ref[rows, sl].astype(f32)
                g = y_ref[rows, sl].astype(f32) * z * _sigmoid_fast(z)
                o_ref[rows, sl] = (g * _rms(g) * wv).astype(bf16)

            _groups(ts, grp, rg=NRG)

    row = pl.BlockSpec((ts, DI), lambda i: (i, 0))
    return pl.pallas_call(
        body, name="gnorm_fwd", grid=(S // ts,),
        in_specs=[row, row, pl.BlockSpec((1, DI), lambda i: (0, 0))],
        out_specs=row, out_shape=jax.ShapeDtypeStruct((S, DI), bf16),
        compiler_params=_cp("parallel"),
    )(y, proj, w)


def gnorm_bwd(dyn, y, proj, w, dproj, ts):
    S = y.shape[0]
    nt = S // ts

    def body(d_ref, y_ref, z_ref, w_ref, _, dy_ref, dz_ref, gw_ref, acc_ref):
        i = pl.program_id(0)

        @pl.when(i == 0)
        def _():
            acc_ref[...] = jnp.zeros_like(acc_ref)
            gw_ref[...] = jnp.zeros_like(gw_ref)

        for k in range(NG):
            sl = slice(k * GW, (k + 1) * GW)
            wv = w_ref[:, sl]

            def grp(r0, _, sl=sl, wv=wv):
                rows = pl.ds(r0, NRG)
                z = z_ref[rows, sl].astype(f32)
                yv = y_ref[rows, sl].astype(f32)
                s = _sigmoid_fast(z)
                sz = z * s
                g = yv * sz
                r = _rms(g)
                d = d_ref[rows, sl].astype(f32)
                acc_ref[:, sl] += _fold(d * g * r)
                dg = _rms_bwd(g, r, d * wv)
                dy_ref[rows, sl] = (dg * sz).astype(bf16)
                dz_ref[rows, sl] = (dg * yv * s * (1.0 + z * (1.0 - s))).astype(bf16)

            _groups(ts, grp, rg=NRG)

        @pl.when(i == nt - 1)
        def _():
            _flush(acc_ref, gw_ref, 0)

    row = pl.BlockSpec((ts, DI), lambda i: (i, 0))
    return pl.pallas_call(
        body, name="gnorm_bwd", grid=(nt,),
        in_specs=[row, row, row, pl.BlockSpec((1, DI), lambda i: (0, 0)), pl.BlockSpec(memory_space=pl.ANY)],
        out_specs=[row, row, pl.BlockSpec((8, DI), lambda i: (0, 0))],
        out_shape=[jax.ShapeDtypeStruct((S, DI), bf16), jax.ShapeDtypeStruct(dproj.shape, bf16),
                   jax.ShapeDtypeStruct((8, DI), f32)],
        scratch_shapes=[pltpu.VMEM((8, DI), f32)],
        input_output_aliases={4: 1},
        compiler_params=_cp("arbitrary"),
    )(dyn, y, proj, w, dproj)


def merge_fwd(proj, ys, ya, ts):
    S = ys.shape[0]

    def body(gs_ref, ga_ref, ys_ref, ya_ref, o_ref):
        for c0 in range(0, D, CW):
            cs = slice(c0, c0 + CW)

            def grp(r0, _, cs=cs):
                rows = pl.ds(r0, NRG)
                o_ref[rows, cs] = (_sigmoid_fast(gs_ref[rows, cs].astype(f32)) * ys_ref[rows, cs].astype(f32)
                                   + _sigmoid_fast(ga_ref[rows, cs].astype(f32)) * ya_ref[rows, cs].astype(f32)
                                   ).astype(bf16)

            _groups(ts, grp, rg=NRG)

    row = pl.BlockSpec((ts, D), lambda i: (i, 0))
    return pl.pallas_call(
        body, name="merge_fwd", grid=(S // ts,),
        in_specs=[pl.BlockSpec((ts, D), lambda i: (i, C_GS // D)), pl.BlockSpec((ts, D), lambda i: (i, C_GA // D)), row, row],
        out_specs=row, out_shape=jax.ShapeDtypeStruct((S, D), bf16),
        compiler_params=_cp("parallel"),
    )(proj, proj, ys, ya)


def merge_bwd(dm, proj, ys, ya, ts):
    S = ys.shape[0]

    def body(d_ref, gs_ref, ga_ref, ys_ref, ya_ref, dys_ref, dya_ref, dg_ref):
        for c0 in range(0, D, CW):
            cs = slice(c0, c0 + CW)

            def grp(r0, _, c0=c0, cs=cs):
                rows = pl.ds(r0, NRG)
                d = d_ref[rows, cs].astype(f32)
                ss = _sigmoid_fast(gs_ref[rows, cs].astype(f32))
                sa = _sigmoid_fast(ga_ref[rows, cs].astype(f32))
                dys_ref[rows, cs] = (d * ss).astype(bf16)
                dya_ref[rows, cs] = (d * sa).astype(bf16)
                dg_ref[rows, cs] = (d * ys_ref[rows, cs].astype(f32) * ss * (1.0 - ss)).astype(bf16)
                dg_ref[rows, D + c0:D + c0 + CW] = (d * ya_ref[rows, cs].astype(f32) * sa * (1.0 - sa)).astype(bf16)

            _groups(ts, grp, rg=NRG)

    row = pl.BlockSpec((ts, D), lambda i: (i, 0))
    o = jax.ShapeDtypeStruct((S, D), bf16)
    return pl.pallas_call(
        body, name="merge_bwd", grid=(S // ts,),
        in_specs=[row, pl.BlockSpec((ts, D), lambda i: (i, C_GS // D)), pl.BlockSpec((ts, D), lambda i: (i, C_GA // D)), row, row],
        out_specs=[row, row, pl.BlockSpec((ts, 2 * D), lambda i: (i, C_GS // (2 * D)))],
        out_shape=[o, o, jax.ShapeDtypeStruct((S, PM), bf16)],
        compiler_params=_cp("parallel"),
    )(dm, proj, proj, ys, ya)


def _ssd_consts():
    h = lax.broadcasted_iota(jnp.int32, (LANES, DI), 0)
    c = lax.broadcasted_iota(jnp.int32, (LANES, DI), 1)
    expand = (c // HD == h).astype(bf16)
    r = lax.broadcasted_iota(jnp.int32, (CH, CH), 0)
    cc = lax.broadcasted_iota(jnp.int32, (CH, CH), 1)
    tril = (cc <= r).astype(bf16)
    triu = (cc >= r).astype(bf16)
    return expand, expand.T, tril, triu


def _ssd_common(xbc_ref, dtr_ref, bias_ref, alog_ref, tril_ref, expand_ref=None, saved=None):
    dtr = dtr_ref[...] + bias_ref[...]
    dt = jnp.maximum(dtr, 0.0) + jnp.log1p(jnp.exp(-jnp.abs(dtr)))
    a = -jnp.exp(alog_ref[...])
    acs = _dot3_left(tril_ref[...], dt * a)
    if saved is None:
        acsx = _dot3_right(acs, expand_ref[...])
        dtx = _dot3_right(dt, expand_ref[...])
    else:
        acsx, dtx = saved[0][...], saved[1][...]
    x = xbc_ref[:, 0:DI].astype(f32)
    xdt = x * dtx
    e = jnp.exp(acsx)
    dsx = jnp.exp(acsx[CH - 1:CH, :] - acsx)
    return dtr, dt, a, acs, acsx, dtx, x, xdt, e, dsx


def _ssd_lmat(acs, acs_t, hh, causal):
    seg = acs[:, hh:hh + 1] - acs_t[hh:hh + 1, :]
    return jnp.where(causal, jnp.exp(jnp.minimum(seg, 0.0)), 0.0)


def ssd_fwd(xbc, dtr, bias, alog, dx_row, comm=None):
    S = xbc.shape[0]
    nc = S // CH
    expand, _, tril, _ = _ssd_consts()
    cm = _Comm(comm)

    def body(*refs):
        ins, (y_ref, hp_ref, ax_ref, dtx_ref), (h_ref, yd_ref), copies = cm.split(refs, 7, 4)
        xbc_ref, dtr_ref, bias_ref, alog_ref, dxr_ref, expand_ref, tril_ref = ins
        c = pl.program_id(0)
        cm.start(copies, c == 0)

        @pl.when(c == 0)
        def _():
            h_ref[...] = jnp.zeros_like(h_ref)

        _, _, _, acs, acsx, dtx, x, xdt, e, dsx = _ssd_common(xbc_ref, dtr_ref, bias_ref, alog_ref, tril_ref,
                                                              expand_ref=expand_ref)
        ax_ref[...] = acsx
        dtx_ref[...] = dtx
        acs_t = acs.T
        xb = xdt.astype(bf16)
        xd = (xdt * dsx).astype(bf16)
        causal = tril_ref[...] > 0
        for g in range(NG):
            gs = slice(g * GW, (g + 1) * GW)
            bg = xbc_ref[:, DI + g * NS:DI + (g + 1) * NS]
            cg = xbc_ref[:, DI + NG * NS + g * NS:DI + NG * NS + (g + 1) * NS]
            cb = _dot(cg, bg, "nt")
            hp = h_ref[g]
            hpb = hp.astype(bf16)
            hp_ref[0, g] = hpb
            yd_ref[:, gs] = _dot(cg, hpb) * e[:, gs]
            h_ref[g] = hp * e[CH - 1:CH, gs] + _dot(bg, xd[:, gs], "tn")
            for j in range(NH // NG):
                hh = g * (NH // NG) + j
                hs = slice(hh * HD, (hh + 1) * HD)
                m = (cb * _ssd_lmat(acs, acs_t, hh, causal)).astype(bf16)
                yd_ref[:, hs] += _dot(m, xb[:, hs])
        y_ref[...] = (yd_ref[...] + dxr_ref[...] * x).astype(bf16)
        cm.wait(copies, c == nc - 1)

    par = lambda shape: pl.BlockSpec(shape, lambda c: (0,) * len(shape))
    res = pl.pallas_call(
        body, name="ssd_fwd", grid=(nc,),
        in_specs=[pl.BlockSpec((CH, CONVD), lambda c: (c, 0)), pl.BlockSpec((CH, LANES), lambda c: (c, 0)),
                  par((1, LANES)), par((1, LANES)), par((1, DI)), par((LANES, DI)), par((CH, CH))] + cm.in_specs(),
        out_specs=[pl.BlockSpec((CH, DI), lambda c: (c, 0)), pl.BlockSpec((1, NG, NS, GW), lambda c: (c, 0, 0, 0)),
                   pl.BlockSpec((CH, DI), lambda c: (c, 0)), pl.BlockSpec((CH, DI), lambda c: (c, 0))] + cm.out_specs(),
        out_shape=[jax.ShapeDtypeStruct((S, DI), bf16), jax.ShapeDtypeStruct((nc, NG, NS, GW), bf16),
                   jax.ShapeDtypeStruct((S, DI), f32), jax.ShapeDtypeStruct((S, DI), f32)] + cm.out_shape(),
        scratch_shapes=[pltpu.VMEM((NG, NS, GW), f32), pltpu.VMEM((CH, DI), f32)] + cm.scratch(),
        compiler_params=_cp("arbitrary", side_effects=bool(cm.n)),
    )(xbc, dtr, bias, alog, dx_row, expand, tril, *cm.bufs)
    return res[0], res[1], (res[2], res[3]), res[4:]


def ssd_bwd(xbc, dtr, dy, hprev, saved, bias, alog, dx_row, comm=None):
    S = xbc.shape[0]
    nc = S // CH
    _, expand_t, tril, triu = _ssd_consts()
    cm = _Comm(comm)

    def body(*refs):
        ins, outs, scr, copies = cm.split(refs, 12, 3)
        xbc_ref, dtr_ref, dy_ref, hp_ref, ax_ref, dtx_ref, bias_ref, alog_ref, dxr_ref, expt_ref, tril_ref, triu_ref = ins
        dxbc_ref, ddtr_ref, acc_ref = outs
        dh_ref, dxs_ref, t_ref, accb_ref, acca_ref, accd_ref = scr
        c = pl.program_id(0)
        cm.start(copies, c == 0)

        @pl.when(c == 0)
        def _():
            dh_ref[...] = jnp.zeros_like(dh_ref)
            accb_ref[...] = jnp.zeros_like(accb_ref)
            acca_ref[...] = jnp.zeros_like(acca_ref)
            accd_ref[...] = jnp.zeros_like(accd_ref)

        dtr, dt, a, acs, _, dtx, x, xdt, e, dsx = _ssd_common(xbc_ref, dtr_ref, bias_ref, alog_ref, tril_ref,
                                                              saved=(ax_ref, dtx_ref))
        acs_t = acs.T
        xb = xdt.astype(bf16)
        xdf = xdt * dsx
        xd = xdf.astype(bf16)
        dyv = dy_ref[...].astype(f32)
        dyb = dy_ref[...]
        dye = (dyv * e).astype(bf16)
        causal = tril_ref[...] > 0
        lane = lax.broadcasted_iota(jnp.int32, (CH, LANES), 1)
        subl = lax.broadcasted_iota(jnp.int32, (LANES, CH), 0)
        ccol = jnp.zeros((CH, LANES), f32)
        rrow = jnp.zeros((LANES, CH), f32)
        last_row = lax.broadcasted_iota(jnp.int32, (CH, 1), 0) == CH - 1
        for g in range(NG):
            gs = slice(g * GW, (g + 1) * GW)
            bsl = slice(DI + g * NS, DI + (g + 1) * NS)
            csl = slice(DI + NG * NS + g * NS, DI + NG * NS + (g + 1) * NS)
            bg = xbc_ref[:, bsl]
            cg = xbc_ref[:, csl]
            cb = _dot(cg, bg, "nt")
            hpb = hp_ref[0, g]
            dhn = dh_ref[g]
            dhnb = dhn.astype(bf16)
            yoff = _dot(cg, hpb) * e[:, gs]
            dxd = _dot(bg, dhnb)
            t2 = dxd * xdf[:, gs]
            t3 = jnp.sum(dhn * hpb.astype(f32), axis=0, keepdims=True) * e[CH - 1:CH, gs]
            t_ref[:, gs] = dyv[:, gs] * yoff - t2 + jnp.where(last_row, jnp.sum(t2, axis=0, keepdims=True) + t3, 0.0)
            dxs_ref[:, gs] = dxd * dsx[:, gs]
            dcg = _dot(dye[:, gs], hpb, "nt")
            dbg = _dot(xd[:, gs], dhnb, "nt")
            dh_ref[g] = dhn * e[CH - 1:CH, gs] + _dot(cg, dye[:, gs], "tn")
            dcb = jnp.zeros((CH, CH), f32)
            for j in range(NH // NG):
                hh = g * (NH // NG) + j
                hs = slice(hh * HD, (hh + 1) * HD)
                lm = _ssd_lmat(acs, acs_t, hh, causal)
                m = cb * lm
                dm = _dot(dyb[:, hs], xb[:, hs], "nt")
                gm = dm * m
                ccol = ccol + jnp.sum(gm, axis=1, keepdims=True) * (lane == hh).astype(f32)
                rrow = rrow + jnp.sum(gm, axis=0, keepdims=True) * (subl == hh).astype(f32)
                dcb = dcb + dm * lm
                dxs_ref[:, hs] += _dot(m.astype(bf16), dyb[:, hs], "tn")
            dcbb = dcb.astype(bf16)
            dxbc_ref[:, csl] = (dcg + _dot(dcbb, bg)).astype(bf16)
            dxbc_ref[:, bsl] = (dbg + _dot(dcbb, cg, "tn")).astype(bf16)
        dxf = dxs_ref[...]
        dxbc_ref[:, 0:DI] = (dxf * dtx + dxr_ref[...] * dyv).astype(bf16)
        expt = expt_ref[...]
        dacs = ccol - rrow.T + _dot2_right(t_ref[...], expt)
        dadt = _dot3_left(triu_ref[...], dacs)
        ddt = _dot2_right(dxf * x, expt) + dadt * a
        ddtr = ddt * _sigmoid(dtr)
        ddtr_ref[...] = ddtr
        accb_ref[...] += ddtr
        acca_ref[...] += dadt * dt
        accd_ref[...] += _dot2_right(dyv * x, expt)

        @pl.when(c == nc - 1)
        def _():
            acc_ref[...] = jnp.zeros_like(acc_ref)
            acc_ref[0:1, :] = jnp.sum(accb_ref[...], axis=0, keepdims=True)
            acc_ref[1:2, :] = jnp.sum(acca_ref[...], axis=0, keepdims=True) * a
            acc_ref[2:3, :] = jnp.sum(accd_ref[...], axis=0, keepdims=True)

        cm.wait(copies, c == nc - 1)

    par = lambda shape: pl.BlockSpec(shape, lambda c: (0,) * len(shape))
    rev = lambda c: (nc - 1 - c, 0)
    res = pl.pallas_call(
        body, name="ssd_bwd", grid=(nc,),
        in_specs=[pl.BlockSpec((CH, CONVD), rev), pl.BlockSpec((CH, LANES), rev), pl.BlockSpec((CH, DI), rev),
                  pl.BlockSpec((1, NG, NS, GW), lambda c: (nc - 1 - c, 0, 0, 0)),
                  pl.BlockSpec((CH, DI), rev), pl.BlockSpec((CH, DI), rev),
                  par((1, LANES)), par((1, LANES)), par((1, DI)), par((DI, LANES)),
                  par((CH, CH)), par((CH, CH))] + cm.in_specs(),
        out_specs=[pl.BlockSpec((CH, CONVD), rev), pl.BlockSpec((CH, LANES), rev), par((8, LANES))] + cm.out_specs(),
        out_shape=[jax.ShapeDtypeStruct((S, CONVD), bf16), jax.ShapeDtypeStruct((S, LANES), f32),
                   jax.ShapeDtypeStruct((8, LANES), f32)] + cm.out_shape(),
        scratch_shapes=[pltpu.VMEM((NG, NS, GW), f32), pltpu.VMEM((CH, DI), f32), pltpu.VMEM((CH, DI), f32),
                        pltpu.VMEM((CH, LANES), f32), pltpu.VMEM((CH, LANES), f32),
                        pltpu.VMEM((CH, LANES), f32)] + cm.scratch(),
        compiler_params=_cp("arbitrary", side_effects=bool(cm.n)),
    )(xbc, dtr, dy, hprev, *saved, bias, alog, dx_row, expand_t, tril, triu, *cm.bufs)
    return res[0], res[1], res[2], res[3:]


def rope_tables(pos_row, ts):
    S = pos_row.shape[1]
    half = AD // 2
    inv = ROPE_THETA ** (-jnp.arange(half, dtype=f32) * 2.0 / AD)
    inv_col = jnp.tile(inv, 2)[:, None]

    def body(p_ref, inv_ref, cos_ref, sin_ref):
        ang = inv_ref[...] * p_ref[...].astype(f32)
        row = lax.broadcasted_iota(jnp.int32, ang.shape, 0)
        cos_ref[...] = jnp.cos(ang)
        sin_ref[...] = jnp.where(row < half, -1.0, 1.0) * jnp.sin(ang)

    o = jax.ShapeDtypeStruct((AD, S), f32)
    return pl.pallas_call(
        body, name="rope_tables", grid=(S // ts,),
        in_specs=[pl.BlockSpec((1, ts), lambda i: (0, i)), pl.BlockSpec((AD, 1), lambda i: (0, 0))],
        out_specs=[pl.BlockSpec((AD, ts), lambda i: (0, i))] * 2, out_shape=[o, o],
        compiler_params=_cp("parallel"),
    )(pos_row, inv_col)


def _partner(t):
    half = AD // 2
    return jnp.concatenate([t[h * AD + o:h * AD + o + half] for h in range(t.shape[0] // AD) for o in (half, 0)], axis=0)


def _rope(t, cos, sin):
    reps = t.shape[0] // AD
    return t * jnp.tile(cos, (reps, 1)) + _partner(t) * jnp.tile(sin, (reps, 1))


def _rope_t(d, cos, sin):
    reps = d.shape[0] // AD
    return d * jnp.tile(cos, (reps, 1)) - _partner(d) * jnp.tile(sin, (reps, 1))


def _lanes_of_group(t, g):
    return jnp.concatenate([t[(g * REP + r) * AD:(g * REP + r + 1) * AD] for r in range(REP)], axis=1)


def _attn_probs(qg, kp, kc, sink_ref, g, not_first):
    n = qg.shape[1]
    s = lax.broadcasted_iota(jnp.int32, (WIN, n), 0)
    t = lax.broadcasted_iota(jnp.int32, (WIN, n), 1) % WIN
    neg = -1e30
    sink = jnp.concatenate([jnp.broadcast_to(sink_ref[0:1, g * REP + r:g * REP + r + 1], (1, WIN)) for r in range(REP)],
                           axis=1)
    sp = jnp.where(jnp.logical_and(s > t, not_first), _dot(kp, qg, "tn"), neg)
    sc = jnp.where(s <= t, _dot(kc, qg, "tn"), neg)
    m = jnp.maximum(jnp.maximum(jnp.max(sp, axis=0, keepdims=True), jnp.max(sc, axis=0, keepdims=True)), sink)
    pp = jnp.exp(sp - m)
    pc = jnp.exp(sc - m)
    ps = jnp.exp(sink - m)
    inv = 1.0 / (jnp.sum(pp, axis=0, keepdims=True) + jnp.sum(pc, axis=0, keepdims=True) + ps)
    return pp * inv, pc * inv, ps * inv


def attn_fwd(qt, kvt, cos, sin, sinks):
    S = qt.shape[1]
    nb = S // WIN
    cur = lambda i: (0, i)
    prev = lambda i: (0, jnp.maximum(i - 1, 0))

    def body(q_ref, kv_ref, kvp_ref, cos_ref, sin_ref, cosp_ref, sinp_ref, sink_ref, o_ref):
        i = pl.program_id(0)
        q = (_rope(q_ref[...].astype(f32), cos_ref[...], sin_ref[...]) * (AD ** -0.5)).astype(bf16)
        kc = _rope(kv_ref[0:KVW, :].astype(f32), cos_ref[...], sin_ref[...]).astype(bf16)
        kp = _rope(kvp_ref[0:KVW, :].astype(f32), cosp_ref[...], sinp_ref[...]).astype(bf16)
        for g in range(KVH):
            ks = slice(g * AD, (g + 1) * AD)
            vs = slice(KVW + g * AD, KVW + (g + 1) * AD)
            pp, pc, _ = _attn_probs(_lanes_of_group(q, g), kp[ks], kc[ks], sink_ref, g, i > 0)
            o = _dot(kvp_ref[vs, :], pp.astype(bf16)) + _dot(kv_ref[vs, :], pc.astype(bf16))
            for r in range(REP):
                h = g * REP + r
                o_ref[h * AD:(h + 1) * AD, :] = o[:, r * WIN:(r + 1) * WIN].astype(bf16)

    tab = pl.BlockSpec((AD, WIN), cur)
    tabp = pl.BlockSpec((AD, WIN), prev)
    return pl.pallas_call(
        body, name="attn_fwd", grid=(nb,),
        in_specs=[pl.BlockSpec((D, WIN), cur), pl.BlockSpec((2 * KVW, WIN), cur), pl.BlockSpec((2 * KVW, WIN), prev),
                  tab, tab, tabp, tabp, pl.BlockSpec((1, LANES), lambda i: (0, 0))],
        out_specs=pl.BlockSpec((D, WIN), cur),
        out_shape=jax.ShapeDtypeStruct((D, S), bf16),
        compiler_params=_cp("parallel"),
    )(qt, kvt, kvt, cos, sin, cos, sin, sinks)


def attn_bwd(qt, kvt, cos, sin, sinks, daot, comm=None):
    S = qt.shape[1]
    nb = S // WIN
    cur = lambda i: (0, jnp.minimum(i, nb - 1))
    prev = lambda i: (0, jnp.maximum(i - 1, 0))
    cm = _Comm(comm)

    def body(*refs):
        ins, (dq_ref, dkv_ref, ds_ref), scr, copies = cm.split(refs, 9, 3)
        q_ref, kv_ref, kvp_ref, cos_ref, sin_ref, cosp_ref, sinp_ref, sink_ref, do_ref = ins
        ck_ref, cv_ref, dqs_ref, dkp_ref, dvp_ref, dkc_ref, dvc_ref, accs_ref = scr
        i = pl.program_id(0)
        cm.start(copies, i == 0)

        @pl.when(i == 0)
        def _():
            ck_ref[...] = jnp.zeros_like(ck_ref)
            cv_ref[...] = jnp.zeros_like(cv_ref)
            accs_ref[...] = jnp.zeros_like(accs_ref)

        @pl.when(i == nb)
        def _():
            dkp_ref[...] = jnp.zeros_like(dkp_ref)
            dvp_ref[...] = jnp.zeros_like(dvp_ref)

        @pl.when(i < nb)
        def _():
            q = (_rope(q_ref[...].astype(f32), cos_ref[...], sin_ref[...]) * (AD ** -0.5)).astype(bf16)
            kc = _rope(kv_ref[0:KVW, :].astype(f32), cos_ref[...], sin_ref[...]).astype(bf16)
            kp = _rope(kvp_ref[0:KVW, :].astype(f32), cosp_ref[...], sinp_ref[...]).astype(bf16)
            do = do_ref[...]
            for g in range(KVH):
                ks = slice(g * AD, (g + 1) * AD)
                vs = slice(KVW + g * AD, KVW + (g + 1) * AD)
                qg = _lanes_of_group(q, g)
                dog = _lanes_of_group(do, g)
                pp, pc, ps = _attn_probs(qg, kp[ks], kc[ks], sink_ref, g, i > 0)
                dpp = _dot(kvp_ref[vs, :], dog, "tn")
                dpc = _dot(kv_ref[vs, :], dog, "tn")
                delta = jnp.sum(pp * dpp + pc * dpc, axis=0, keepdims=True)
                dsp = (pp * (dpp - delta)).astype(bf16)
                dsc = (pc * (dpc - delta)).astype(bf16)
                accs_ref[g:g + 1, :] -= ps * delta
                dqg = (_dot(kp[ks], dsp) + _dot(kc[ks], dsc)) * (AD ** -0.5)
                for r in range(REP):
                    h = g * REP + r
                    dqs_ref[h * AD:(h + 1) * AD, :] = dqg[:, r * WIN:(r + 1) * WIN]
                dkp_ref[ks, :] = _dot(qg, dsp, "nt")
                dkc_ref[ks, :] = _dot(qg, dsc, "nt")
                dvp_ref[ks, :] = _dot(dog, pp.astype(bf16), "nt")
                dvc_ref[ks, :] = _dot(dog, pc.astype(bf16), "nt")
            dq_ref[...] = _rope_t(dqs_ref[...], cos_ref[...], sin_ref[...]).astype(bf16)

        dkv_ref[0:KVW, :] = _rope_t(ck_ref[...] + dkp_ref[...], cosp_ref[...], sinp_ref[...]).astype(bf16)
        dkv_ref[KVW:2 * KVW, :] = (cv_ref[...] + dvp_ref[...]).astype(bf16)

        @pl.when(i < nb)
        def _():
            ck_ref[...] = dkc_ref[...]
            cv_ref[...] = dvc_ref[...]

        @pl.when(i == nb)
        def _():
            lane = lax.broadcasted_iota(jnp.int32, (1, LANES), 1)
            row = jnp.zeros((1, LANES), f32)
            for h in range(AH):
                part = accs_ref[h // REP:h // REP + 1, (h % REP) * WIN:(h % REP + 1) * WIN]
                row = row + jnp.where(lane == h, jnp.sum(part, axis=1, keepdims=True), 0.0)
            ds_ref[...] = jnp.zeros_like(ds_ref)
            ds_ref[0:1, :] = row

        cm.wait(copies, i == nb)

    tab = pl.BlockSpec((AD, WIN), cur)
    tabp = pl.BlockSpec((AD, WIN), prev)
    kvs = lambda: pltpu.VMEM((KVW, WIN), f32)
    res = pl.pallas_call(
        body, name="attn_bwd", grid=(nb + 1,),
        in_specs=[pl.BlockSpec((D, WIN), cur), pl.BlockSpec((2 * KVW, WIN), cur), pl.BlockSpec((2 * KVW, WIN), prev),
                  tab, tab, tabp, tabp, pl.BlockSpec((1, LANES), lambda i: (0, 0)),
                  pl.BlockSpec((D, WIN), cur)] + cm.in_specs(),
        out_specs=[pl.BlockSpec((D, WIN), cur), pl.BlockSpec((2 * KVW, WIN), prev),
                   pl.BlockSpec((8, LANES), lambda i: (0, 0))] + cm.out_specs(),
        out_shape=[jax.ShapeDtypeStruct((D, S), bf16), jax.ShapeDtypeStruct((2 * KVW, S), bf16),
                   jax.ShapeDtypeStruct((8, LANES), f32)] + cm.out_shape(),
        scratch_shapes=[kvs(), kvs(), pltpu.VMEM((D, WIN), f32), kvs(), kvs(), kvs(), kvs(),
                        pltpu.VMEM((8, REP * WIN), f32)] + cm.scratch(),
        compiler_params=_cp("arbitrary", side_effects=bool(cm.n)),
    )(qt, kvt, kvt, cos, sin, cos, sin, sinks, daot, *cm.bufs)
    return res[0], res[1], res[2], res[3:]


ADAM_C1 = 1.0 / (1.0 - ADAM_B1 ** ADAM_STEP)
ADAM_C2 = 1.0 / (1.0 - ADAM_B2 ** ADAM_STEP)


def _adam_update(g, w, m, v):
    nm = ADAM_B1 * m + (1.0 - ADAM_B1) * g
    nv = ADAM_B2 * v + (1.0 - ADAM_B2) * (g * g)
    return -ADAM_LR * ((nm * ADAM_C1) / (jnp.sqrt(nv * ADAM_C2) + ADAM_EPS) + ADAM_WD * w), nm, nv


def adamw(parts, w, m, v, tr, name):
    n, R, C = parts.shape

    def body(p_ref, w_ref, m_ref, v_ref, g_ref, d_ref, nm_ref, nv_ref):
        def grp(g0, _):
            r0 = pl.multiple_of(g0 * RG, RG)
            rows = pl.ds(r0, RG)
            g = p_ref[0, rows, :].astype(f32)
            for k in range(1, n):
                g = g + p_ref[k, rows, :].astype(f32)
            d, nm, nv = _adam_update(g, w_ref[rows, :], m_ref[rows, :], v_ref[rows, :])
            g_ref[rows, :] = g
            d_ref[rows, :] = d
            nm_ref[rows, :] = nm
            nv_ref[rows, :] = nv
            return 0

        lax.fori_loop(0, tr // RG, grp, 0)

    row = pl.BlockSpec((tr, C), lambda i: (i, 0))
    o = jax.ShapeDtypeStruct((R, C), f32)
    return pl.pallas_call(
        body, name=name, grid=(R // tr,),
        in_specs=[pl.BlockSpec((n, tr, C), lambda i: (0, i, 0)), row, row, row],
        out_specs=[row, row, row, row], out_shape=[o, o, o, o],
        compiler_params=_cp("parallel"),
    )(parts, w, m, v)


SMALL_ROW = (("norm_mix_post_w", D), ("norm_ffn_pre_w", D), ("norm_ffn_post_w", D), ("ssd_norm_w", DI),
             ("ssd_conv_b", CONVD), ("ffn_conv_b", 2 * FF), ("ssd_dt_bias", NH), ("ssd_a_log", NH), ("ssd_d", NH),
             ("attn_sinks", AH), ("loss", 1))
CONV_BLOCK = 1152
SSD_CONV_COLS = CONVD // N_DEV
FFN_CONV_COLS = 2 * FF // N_DEV


def _row_offsets():
    off, o = {}, 0
    for name, n in SMALL_ROW:
        off[name] = (o, n)
        o += -(-n // LANES) * LANES
    return off, o


def adamw_small(recv_row, recv_pre, recv_conv, params):
    off, _ = _row_offsets()
    names = list(params)
    n = len(names)

    def total(ref, rows, lo, width):
        g = ref[0, rows, lo:lo + width]
        for d in range(1, N_DEV):
            g = g + ref[d, rows, lo:lo + width]
        return g

    def grad_of(name, row_ref, pre_ref, conv_ref):
        if name == "norm_mix_pre_w":
            return total(pre_ref, slice(0, 1), 0, D)
        if name == "ssd_conv_w":
            return total(conv_ref, slice(0, SSD_K), 0, SSD_CONV_COLS)
        if name == "ffn_conv_w":
            return total(conv_ref, slice(0, FFN_K), 3 * LANES, FFN_CONV_COLS)
        o, width = off[name]
        return total(row_ref, slice(0, 1), o, width)

    def body(row_ref, pre_ref, conv_ref, *refs):
        ins, outs = refs[:3 * n], refs[3 * n:]
        for k, name in enumerate(names):
            w_ref, m_ref, v_ref = ins[3 * k:3 * k + 3]
            g_ref, d_ref, nm_ref, nv_ref = outs[4 * k:4 * k + 4]
            g = grad_of(name, row_ref, pre_ref, conv_ref)
            d, nm, nv = _adam_update(g, w_ref[...], m_ref[...], v_ref[...])
            g_ref[...] = g
            d_ref[...] = d
            nm_ref[...] = nm
            nv_ref[...] = nv
        outs[4 * n][...] = total(row_ref, slice(0, 1), off["loss"][0], LANES)

    flat = [t for name in names for t in params[name]]
    out_shape = [jax.ShapeDtypeStruct(params[name][0].shape, f32) for name in names for _ in range(4)]
    res = pl.pallas_call(
        body, name="adamw_small",
        out_shape=out_shape + [jax.ShapeDtypeStruct((1, LANES), f32)],
        compiler_params=pltpu.CompilerParams(vmem_limit_bytes=VMEM_LIMIT),
    )(recv_row, recv_pre, recv_conv, *flat)
    return {name: res[4 * k:4 * k + 4] for k, name in enumerate(names)}, res[4 * n]


def _pad_rows8(w):
    return jnp.pad(w, ((0, 8 - w.shape[0]), (0, 0)))


def _pad_lanes(v):
    return jnp.pad(v.reshape(1, -1), ((0, 0), (0, LANES - v.size)))


WEIGHTS = ('norm_mix_pre_w', 'w_in', 'ssd_conv_w', 'ssd_conv_b', 'ssd_dt_bias', 'ssd_a_log', 'ssd_d', 'ssd_norm_w',
           'ssd_w_out', 'attn_sinks', 'attn_w_out', 'w_mix_out', 'norm_mix_post_w', 'norm_ffn_pre_w', 'ffn_w_up',
           'ffn_conv_w', 'ffn_conv_b', 'ffn_w_down', 'norm_ffn_post_w')
W_IN_ROWS = IN_DIM // N_DEV
W_IN_PAD = 1104
W_IN_SPLIT = 832
TS = 512


def kernel(x, positions, norm_mix_pre_w, w_in, ssd_conv_w, ssd_conv_b, ssd_dt_bias, ssd_a_log, ssd_d, ssd_norm_w, ssd_w_out, attn_sinks, attn_w_out, w_mix_out, norm_mix_post_w, norm_ffn_pre_w, ffn_w_up, ffn_conv_w, ffn_conv_b, ffn_w_down, norm_ffn_post_w, loss_target, m_norm_mix_pre_w, m_w_in, m_ssd_conv_w, m_ssd_conv_b, m_ssd_dt_bias, m_ssd_a_log, m_ssd_d, m_ssd_norm_w, m_ssd_w_out, m_attn_sinks, m_attn_w_out, m_w_mix_out, m_norm_mix_post_w, m_norm_ffn_pre_w, m_ffn_w_up, m_ffn_conv_w, m_ffn_conv_b, m_ffn_w_down, m_norm_ffn_post_w, v_norm_mix_pre_w, v_w_in, v_ssd_conv_w, v_ssd_conv_b, v_ssd_dt_bias, v_ssd_a_log, v_ssd_d, v_ssd_norm_w, v_ssd_w_out, v_attn_sinks, v_attn_w_out, v_w_mix_out, v_norm_mix_post_w, v_norm_ffn_pre_w, v_ffn_w_up, v_ffn_conv_w, v_ffn_conv_b, v_ffn_w_down, v_norm_ffn_post_w):
    a = locals()
    r2 = lambda t: t.reshape(t.shape[-2], t.shape[-1])
    w = {n: r2(a[n]) for n in WEIGHTS}
    m = {n: r2(a["m_" + n]) for n in WEIGHTS}
    v = {n: r2(a["v_" + n]) for n in WEIGHTS}
    xs, target = x[0], loss_target[0]
    S = xs.shape[0]
    ts = TS

    w_in_blk = jnp.pad(w["w_in"].T.astype(bf16), ((0, W_IN_PAD - W_IN_ROWS), (0, 0)))
    conv_blk = jnp.concatenate([_pad_rows8(w["ssd_conv_w"]), _pad_rows8(w["ffn_conv_w"]),
                                jnp.zeros((8, CONV_BLOCK - SSD_CONV_COLS - FFN_CONV_COLS), f32)], axis=1)
    u, (g_in, g_conv) = prenorm_fwd(xs, w["norm_mix_pre_w"], ts, [w_in_blk, conv_blk])
    wt = g_in[:, :W_IN_ROWS].reshape(IN_DIM, D)
    w_main_t = jnp.concatenate([wt[IN_OFF[0]:IN_OFF[1]], wt[IN_OFF[6]:IN_OFF[8]], wt[IN_OFF[1]:IN_OFF[2]]], axis=0)
    w_q_t = wt[IN_OFF[3]:IN_OFF[4]]
    w_kv_t = wt[IN_OFF[4]:IN_OFF[6]]
    w_dt_t = jnp.pad(wt[IN_OFF[2]:IN_OFF[3]], ((0, LANES - NH), (0, 0)))
    conv_w8 = g_conv[:, :, 0:SSD_CONV_COLS].transpose(1, 0, 2).reshape(8, CONVD)
    fconv_w8 = g_conv[:, :, SSD_CONV_COLS:SSD_CONV_COLS + FFN_CONV_COLS].transpose(1, 0, 2).reshape(8, 2 * FF)
    bias = _pad_lanes(w["ssd_dt_bias"])
    alog = _pad_lanes(w["ssd_a_log"])
    dx_row = jnp.repeat(w["ssd_d"].reshape(-1), HD).reshape(1, DI)
    sinks = _pad_lanes(w["attn_sinks"])

    later = [w["ssd_w_out"].astype(bf16), w["attn_w_out"].astype(bf16), w["w_mix_out"].astype(bf16)]
    proj, (g_so, g_ao, g_mix) = mm(u, w_main_t, "nt", bf16, "mm_proj", comm=(later, (False,) * 3))
    w_ssd_out, w_attn_out, w_mix = g_so.reshape(DI, D), g_ao.reshape(D, D), g_mix.reshape(D, D)
    qt = mm(w_q_t, u, "nt", bf16, "mm_q")
    kvt = mm(w_kv_t, u, "nt", bf16, "mm_kv")
    dtr = mm(u, w_dt_t, "nt", f32, "mm_dt")
    xbc, conv_c = ssdconv_fwd(proj, conv_w8, w["ssd_conv_b"], ts)
    y, hprev, ssd_saved, (g_up, g_down) = ssd_fwd(xbc, dtr, bias, alog, dx_row, comm=(
        [w["ffn_w_up"].T.astype(bf16), w["ffn_w_down"].astype(bf16)], (False, False)))
    w_up_t = g_up.reshape(2 * FF, D)
    w_down = g_down.reshape(FF, D)
    yn = gnorm_fwd(y, proj, w["ssd_norm_w"], ts)
    ys = mm(yn, w_ssd_out, "nn", bf16, "mm_ssd_out")
    cos, sin = rope_tables(positions, ts)
    aot = attn_fwd(qt, kvt, cos, sin, sinks)
    ya = mm(aot, w_attn_out, "tn", bf16, "mm_attn_out")
    merged = merge_fwd(proj, ys, ya, ts)
    mo = mm(merged, w_mix, "nn", bf16, "mm_mix")
    x1, h = post_fwd(xs, mo, w["norm_mix_post_w"], w["norm_ffn_pre_w"], ts)
    up = mm(h, w_up_t, "nt", bf16, "mm_up")
    act, gate_c, val_c = ffnact_fwd(up, fconv_w8, w["ffn_conv_b"], ts)
    ff = mm(act, w_down, "nn", bf16, "mm_down")
    loss_blk, dout, dff, g_post2 = loss_head(x1, ff, target, w["norm_ffn_post_w"], ts)

    dact = mm(dff, w_down, "nt", bf16, "mm_dact")
    gw_down = mm(act, dff, "tn", bf16, "mm_g_down")
    dgate, dval = ffnact_bwd(dact, gate_c, val_c, ts)
    dup_pre, g_fconv_a = dwconv_bwd(dgate, up, 0, fconv_w8, 0, FFN_K, FF, ts, "ffnconv_bwd_gate", out_cols=2 * FF)
    dup_pre, g_fconv_b = dwconv_bwd(dval, up, 1, fconv_w8, 1, FFN_K, FF, ts, "ffnconv_bwd_val", into=dup_pre, ocb=1,
                                    out_cols=2 * FF)
    g_fconv = jnp.concatenate([g_fconv_a, g_fconv_b], axis=1)
    dh, (r_down,) = mm(dup_pre, w_up_t, "nn", bf16, "mm_dh", comm=([gw_down.reshape(N_DEV, FF // N_DEV, D)], (True,)))
    gw_up_t = mm(dup_pre, h, "tn", bf16, "mm_g_up")
    dx1, dmo, g_norms = post_bwd(dout, dh, x1, mo, w["norm_mix_post_w"], w["norm_ffn_pre_w"], ts)
    dmerged = mm(dmo, w_mix, "nt", bf16, "mm_dmerged")
    gw_mix = mm(merged, dmo, "tn", bf16, "mm_g_mix")
    dys, dya, dproj = merge_bwd(dmerged, proj, ys, ya, ts)
    daot = mm(w_attn_out, dya, "nt", bf16, "mm_dao")
    gw_attn_out = mm(aot, dya, "nn", bf16, "mm_g_attn_out")
    dqt, dkvt, g_sinks, (r_up,) = attn_bwd(qt, kvt, cos, sin, sinks, daot,
                                           comm=([gw_up_t.reshape(N_DEV, 2 * FF // N_DEV, D)], (True,)))
    du_b = mm(dkvt, w_kv_t, "tn", bf16, "mm_du_kv")
    du_d = mm(dqt, w_q_t, "tn", bf16, "mm_du_q")
    dyn = mm(dys, w_ssd_out, "nt", bf16, "mm_dyn")
    gw_ssd_out = mm(yn, dys, "tn", bf16, "mm_g_ssd_out")
    dy, dproj, g_gnorm = gnorm_bwd(dyn, y, proj, w["ssd_norm_w"], dproj, ts)
    sends = [gw_ssd_out.reshape(N_DEV, DI // N_DEV, D), gw_attn_out.reshape(N_DEV, D // N_DEV, D),
             gw_mix.reshape(N_DEV, D // N_DEV, D)]
    dxbc, ddtr, g_ssd, (r_so, r_ao, r_mix) = ssd_bwd(xbc, dtr, dy, hprev, ssd_saved, bias, alog, dx_row,
                                                     comm=(sends, (True,) * 3))
    dproj, g_conv_w = dwconv_bwd(dxbc, proj, C_XBC // 1024, conv_w8, 0, SSD_K, 1024, ts, "ssdconv_bwd", act_c=conv_c,
                                 into=dproj, ocb=C_XBC // 1024, out_cols=PM)
    ddtr_b = ddtr.astype(bf16)
    du_c = mm(ddtr_b, w_dt_t, "nn", bf16, "mm_du_dt")
    g_main_t = mm(dproj, u, "tn", bf16, "mm_g_in")
    g_q_t = mm(dqt, u, "nn", bf16, "mm_g_q")
    g_kv_t = mm(dkvt, u, "nn", bf16, "mm_g_kv")
    g_dt_t = mm(ddtr_b, u, "tn", bf16, "mm_g_dt")
    g_wt = jnp.concatenate([g_main_t[C_Z:C_GS], g_main_t[C_XBC:PM], g_dt_t[:NH], g_q_t, g_kv_t, g_main_t[C_GS:C_XBC]],
                           axis=0)
    send_in = jnp.pad(g_wt.reshape(N_DEV, W_IN_ROWS, D), ((0, 0), (0, W_IN_PAD - W_IN_ROWS), (0, 0)))
    pieces = {"norm_mix_post_w": g_norms[1:2], "norm_ffn_pre_w": g_norms[0:1], "norm_ffn_post_w": g_post2[0:1],
              "ssd_norm_w": g_gnorm[0:1], "ssd_conv_b": g_conv_w[7:8], "ffn_conv_b": g_fconv[7:8],
              "ssd_dt_bias": g_ssd[0:1], "ssd_a_log": g_ssd[1:2], "ssd_d": g_ssd[2:3], "attn_sinks": g_sinks[0:1],
              "loss": loss_blk[0:1]}
    row = jnp.concatenate([jnp.pad(pieces[n][:, :min(k, pieces[n].shape[1])],
                                   ((0, 0), (0, -(-k // LANES) * LANES - min(k, pieces[n].shape[1]))))
                           for n, k in SMALL_ROW], axis=1)
    send_row = jnp.pad(row, ((0, 7), (0, 0)))
    send_conv = jnp.concatenate(
        [g_conv_w.reshape(8, N_DEV, SSD_CONV_COLS).transpose(1, 0, 2),
         g_fconv.reshape(8, N_DEV, FFN_CONV_COLS).transpose(1, 0, 2),
         jnp.zeros((N_DEV, 8, CONV_BLOCK - SSD_CONV_COLS - FFN_CONV_COLS), f32)], axis=2)
    du_a, (r_in_a, recv_row, recv_conv) = mm(dproj, w_main_t, "nn", bf16, "mm_du", comm=(
        [send_in[:, :W_IN_SPLIT], send_row, send_conv], (True, False, True)))
    grad_x, g_pre, (r_in_b,) = prenorm_bwd(xs, w["norm_mix_pre_w"], (du_a, du_b, du_c, du_d), dx1, ts,
                                           comm=([send_in[:, W_IN_SPLIT:]], (True,)))
    (recv_pre,) = exchange([g_pre], (False,), "gather_last")

    r_in = jnp.concatenate([r_in_a, r_in_b], axis=1)
    tpad = lambda t: jnp.pad(t.T, ((0, W_IN_PAD - W_IN_ROWS), (0, 0)))
    o_in = [t[:W_IN_ROWS].T for t in adamw(r_in, tpad(w["w_in"]), tpad(m["w_in"]), tpad(v["w_in"]), 368, "adamw_w_in")]
    o_up = [t.T for t in adamw(r_up, w["ffn_w_up"].T, m["ffn_w_up"].T, v["ffn_w_up"].T, 352, "adamw_w_up")]
    big = {"w_in": o_in, "ffn_w_up": o_up,
           "ssd_w_out": adamw(r_so, w["ssd_w_out"], m["ssd_w_out"], v["ssd_w_out"], 256, "adamw_ssd_out"),
           "attn_w_out": adamw(r_ao, w["attn_w_out"], m["attn_w_out"], v["attn_w_out"], 128, "adamw_attn_out"),
           "w_mix_out": adamw(r_mix, w["w_mix_out"], m["w_mix_out"], v["w_mix_out"], 128, "adamw_mix"),
           "ffn_w_down": adamw(r_down, w["ffn_w_down"], m["ffn_w_down"], v["ffn_w_down"], 352, "adamw_down")}
    small_names = [n for n in WEIGHTS if n not in big]
    small, loss_row = adamw_small(recv_row, recv_pre, recv_conv, {n: (w[n], m[n], v[n]) for n in small_names})

    outs = [loss_row[0, 0], grad_x[None]]
    for k in range(4):
        for n in WEIGHTS:
            outs.append((big[n][k] if n in big else small[n][k]).reshape(a[n].shape))
    return tuple(outs)
```

```python
import jax
import jax.numpy as jnp
import numpy as np
from jax import lax
from jax.experimental import pallas as pl
from jax.experimental.pallas import tpu as pltpu

f32 = jnp.float32
bf16 = jnp.bfloat16

N_DEV = 8
D = 1024
DI = 2048
NH = 32
HD = 64
NG = 4
GW = DI // NG
NS = 128
CH = 128
CONVD = DI + 2 * NG * NS
SSD_K = 4
AH = 16
AD = 64
KVH = 4
REP = AH // KVH
KVW = KVH * AD
WIN = 128
FF = 2816
FFN_K = 3
EPS = 1e-6
ROPE_THETA = 10000.0
LANES = 128
RG = 16
CW = 256

C_Z, C_GS, C_GA, C_XBC, PM = 0, 2048, 3072, 4096, 7168
IN_SIZES = (DI, CONVD, NH, D, KVW, KVW, D, D)
IN_OFF = tuple(int(v) for v in np.cumsum((0,) + IN_SIZES))
IN_DIM = IN_OFF[-1]

ADAM_LR, ADAM_B1, ADAM_B2, ADAM_EPS, ADAM_WD, ADAM_STEP = 0.001, 0.9, 0.999, 1e-08, 0.01, 10

VMEM_LIMIT = 56 * 1024 * 1024


def _cp(*sem, side_effects=False):
    return pltpu.CompilerParams(dimension_semantics=sem, vmem_limit_bytes=VMEM_LIMIT, has_side_effects=side_effects)


def _dot(a, b, mode="nn"):
    dims = {"nn": (((1,), (0,)), ((), ())), "nt": (((1,), (1,)), ((), ())), "tn": (((0,), (0,)), ((), ()))}[mode]
    return lax.dot_general(a, b, dims, preferred_element_type=f32)


def _split3(v):
    hi = v.astype(bf16)
    r = v - hi.astype(f32)
    mid = r.astype(bf16)
    lo = (r - mid.astype(f32)).astype(bf16)
    return hi, mid, lo


def _dot3_left(m01, v):
    hi, mid, lo = _split3(v)
    return _dot(m01, hi) + _dot(m01, mid) + _dot(m01, lo)


def _dot3_right(v, m01):
    hi, mid, lo = _split3(v)
    return _dot(hi, m01) + _dot(mid, m01) + _dot(lo, m01)


def _dot2_right(v, m01):
    hi = v.astype(bf16)
    lo = (v - hi.astype(f32)).astype(bf16)
    return _dot(hi, m01) + _dot(lo, m01)


def _sigmoid(x):
    return 1.0 / (1.0 + jnp.exp(-x))


def _sigmoid_fast(x):
    return pl.reciprocal(1.0 + jnp.exp(-x), approx=True)


def _peer(k, x, y, c):
    return ((1 - x) if k & 4 else x, (1 - y) if k & 2 else y, (1 - c) if k & 1 else c)


def _xchg_copies(buf_refs, out_refs, send_sems, recv_sems, local_sems, personalised):
    x, y, c = lax.axis_index("x"), lax.axis_index("y"), lax.axis_index("c")
    me = 4 * x + 2 * y + c
    local, remote = [], []
    for b, (buf, out, pers) in enumerate(zip(buf_refs, out_refs, personalised)):
        local.append(pltpu.make_async_copy(buf.at[me] if pers else buf, out.at[me], local_sems.at[b]))
        for k in range(1, N_DEV):
            px, py, pc = _peer(k, x, y, c)
            s = b * (N_DEV - 1) + k - 1
            remote.append(pltpu.make_async_remote_copy(
                src_ref=buf.at[4 * px + 2 * py + pc] if pers else buf, dst_ref=out.at[me],
                send_sem=send_sems.at[s], recv_sem=recv_sems.at[s],
                device_id=(px, py, pc), device_id_type=pl.DeviceIdType.MESH))
    return local, remote


class _Comm:
    def __init__(self, comm):
        self.bufs, self.pers = comm if comm else ((), ())
        self.n = len(self.bufs)

    def in_specs(self):
        return [pl.BlockSpec(memory_space=pl.ANY)] * self.n

    out_specs = in_specs

    def out_shape(self):
        return [jax.ShapeDtypeStruct((N_DEV,) + tuple(b.shape[1:] if p else b.shape), b.dtype)
                for b, p in zip(self.bufs, self.pers)]

    def scratch(self):
        n = self.n
        return [pltpu.SemaphoreType.DMA((n * (N_DEV - 1),)), pltpu.SemaphoreType.DMA((n * (N_DEV - 1),)),
                pltpu.SemaphoreType.DMA((n,))] if n else []

    def split(self, refs, n_in, n_out):
        n = self.n
        ins, outs = refs[:n_in], refs[n_in + n:n_in + n + n_out]
        rest = refs[n_in + n + n_out + n:]
        if not n:
            return ins, outs, rest, None
        copies = _xchg_copies(refs[n_in:n_in + n], refs[n_in + n + n_out:n_in + n + n_out + n], *rest[-3:], self.pers)
        return ins, outs, rest[:-3], copies

    def start(self, copies, first):
        if copies:
            @pl.when(first)
            def _():
                for cp in copies[0] + copies[1]:
                    cp.start()

    def wait(self, copies, last):
        if copies:
            @pl.when(last)
            def _():
                for cp in copies[1]:
                    cp.wait_recv()
                for cp in copies[1]:
                    cp.wait_send()
                for cp in copies[0]:
                    cp.wait()


def exchange(bufs, personalised, name):
    cm = _Comm((bufs, personalised))

    def body(*refs):
        _, _, _, copies = cm.split(refs, 0, 0)
        cm.start(copies, True)
        cm.wait(copies, True)

    return pl.pallas_call(
        body, name=name, in_specs=cm.in_specs(), out_specs=cm.out_specs(), out_shape=cm.out_shape(),
        scratch_shapes=cm.scratch(), compiler_params=pltpu.CompilerParams(has_side_effects=True),
    )(*bufs)


class _TwoLevelGather:
    def __init__(self, bufs):
        self.bufs = list(bufs)
        self.n = len(self.bufs)

    def in_specs(self):
        return [pl.BlockSpec(memory_space=pl.ANY)] * self.n

    out_specs = in_specs

    def out_shape(self):
        return [jax.ShapeDtypeStruct((N_DEV,) + tuple(b.shape), b.dtype) for b in self.bufs]

    def scratch(self):
        per = N_DEV - 1
        return [pltpu.SemaphoreType.DMA((self.n * per,)), pltpu.SemaphoreType.DMA((self.n * per,)),
                pltpu.SemaphoreType.DMA((self.n,))]

    def bind(self, ins, outs, send_sems, recv_sems, local_sems):
        n, per = self.n, N_DEV - 1
        x, y, c = lax.axis_index("x"), lax.axis_index("y"), lax.axis_index("c")
        me, sibling = (x, y, c), (x, y, 1 - c)
        chips = [(1 - x, y), (x, 1 - y), (1 - x, 1 - y)]

        def copy(b, k, block, to, src=None):
            dst = outs[b].at[4 * block[0] + 2 * block[1] + block[2]]
            return pltpu.make_async_remote_copy(
                src_ref=dst if src is None else src, dst_ref=dst,
                send_sem=send_sems.at[b * per + k], recv_sem=recv_sems.at[b * per + k],
                device_id=to, device_id_type=pl.DeviceIdType.MESH)

        mine = [pltpu.make_async_copy(ins[b], outs[b].at[4 * x + 2 * y + c], local_sems.at[b]) for b in range(n)]
        first = []
        for b in range(n):
            first.append(copy(b, 0, me, sibling, src=ins[b]))
            first += [copy(b, 1 + j, me, (*chip, c), src=ins[b]) for j, chip in enumerate(chips)]

        def start():
            for cp in mine + first:
                cp.start()

        def finish():
            passed = []
            for j, chip in enumerate(chips):
                for b in range(n):
                    copy(b, 1 + j, (*chip, c), me).wait_recv()
                    passed.append(copy(b, 4 + j, (*chip, c), sibling))
                    passed[-1].start()
            for b in range(n):
                copy(b, 0, sibling, me).wait_recv()
                for j, chip in enumerate(chips):
                    copy(b, 4 + j, (*chip, 1 - c), me).wait_recv()
            for cp in first + passed:
                cp.wait_send()
            for cp in mine:
                cp.wait()

        return start, finish


MM_TILES = (3584, 2176, 2048, 1792, 1408, 1024, 512, 256, 128)
MM_VMEM_BUDGET = 40 * 1024 * 1024


def _mm_tiles(M, N, K, out_bytes):
    cm = [t for t in MM_TILES if M % t == 0]
    cn = [t for t in MM_TILES if N % t == 0]
    ck = [t for t in MM_TILES if K % t == 0]
    best = None
    for bm in cm[:2]:
        for bn in cn:
            for bk in ck:
                need = 4 * (bm * bk + bk * bn) + bm * bn * (4 + 2 * out_bytes)
                if need <= MM_VMEM_BUDGET:
                    score = (bm * bn * bk, bk)
                    if best is None or score > best[0]:
                        best = (score, (bm, bn, bk))
    return best[1]


def mm(a, b, mode, out_dtype, name, comm=None):
    if mode == "nn":
        (M, K), (_, N) = a.shape, b.shape
    elif mode == "nt":
        (M, K), (N, _) = a.shape, b.shape
    else:
        (K, M), (_, N) = a.shape, b.shape
    bm, bn, bk = _mm_tiles(M, N, K, jnp.dtype(out_dtype).itemsize)
    gm, gn, nk = M // bm, N // bn, K // bk
    cm = _Comm(comm)

    def body(*refs):
        (a_ref, b_ref), (o_ref,), scr, copies = cm.split(refs, 2, 1)
        i, j, k = pl.program_id(0), pl.program_id(1), pl.program_id(2)
        cm.start(copies, jnp.logical_and(jnp.logical_and(i == 0, j == 0), k == 0))
        p = _dot(a_ref[...], b_ref[...], mode)
        if nk == 1:
            o_ref[...] = p.astype(o_ref.dtype)
        else:
            acc_ref = scr[0]

            @pl.when(k == 0)
            def _():
                acc_ref[...] = p

            @pl.when(k > 0)
            def _():
                acc_ref[...] += p

            @pl.when(k == nk - 1)
            def _():
                o_ref[...] = acc_ref[...].astype(o_ref.dtype)

        cm.wait(copies, jnp.logical_and(jnp.logical_and(i == gm - 1, j == gn - 1), k == nk - 1))

    if mode == "nn":
        a_spec = pl.BlockSpec((bm, bk), lambda i, j, k: (i, k))
        b_spec = pl.BlockSpec((bk, bn), lambda i, j, k: (k, j))
    elif mode == "nt":
        a_spec = pl.BlockSpec((bm, bk), lambda i, j, k: (i, k))
        b_spec = pl.BlockSpec((bn, bk), lambda i, j, k: (j, k))
    else:
        a_spec = pl.BlockSpec((bk, bm), lambda i, j, k: (k, i))
        b_spec = pl.BlockSpec((bk, bn), lambda i, j, k: (k, j))
    sem = ("arbitrary",) * 3 if cm.n else ("parallel", "parallel", "arbitrary")
    res = pl.pallas_call(
        body, name=name, grid=(gm, gn, nk),
        in_specs=[a_spec, b_spec] + cm.in_specs(),
        out_specs=[pl.BlockSpec((bm, bn), lambda i, j, k: (i, j))] + cm.out_specs(),
        out_shape=[jax.ShapeDtypeStruct((M, N), out_dtype)] + cm.out_shape(),
        scratch_shapes=([pltpu.VMEM((bm, bn), f32)] if nk > 1 else []) + cm.scratch(),
        compiler_params=_cp(*sem, side_effects=bool(cm.n)),
    )(a, b, *cm.bufs)
    return (res[0], res[1:]) if cm.n else res[0]


def _groups(ts, fn, carry=None, reverse=False, unroll=8, rg=RG):
    n = ts // rg
    if n == 1:
        return fn(0, carry)
    unroll = min(unroll, n)
    span = rg * unroll

    def body(g, c):
        r0 = pl.multiple_of((n // unroll - 1 - g if reverse else g) * span, span)
        for u in (range(unroll - 1, -1, -1) if reverse else range(unroll)):
            c = fn(pl.multiple_of(r0 + u * rg, rg), c)
        return c

    return lax.fori_loop(0, n // unroll, body, carry)


def _rms(x):
    return lax.rsqrt(jnp.mean(x * x, axis=-1, keepdims=True) + EPS)


def _rms_bwd(x, r, dn):
    n = x * r
    return r * (dn - n * jnp.mean(dn * n, axis=-1, keepdims=True))


NRG = 256


def _fold(x):
    return jnp.sum(x.reshape(x.shape[0] // 8, 8, x.shape[1]), axis=0)


def _flush(acc_ref, out_ref, row):
    out_ref[row:row + 1, :] = jnp.sum(acc_ref[...], axis=0, keepdims=True)


def prenorm_fwd(x, w, pos_row, ts, gather):
    S = x.shape[0]
    nt = S // ts
    tg = _TwoLevelGather(gather)
    n = tg.n
    half = AD // 2
    inv = ROPE_THETA ** (-jnp.arange(half, dtype=f32) * 2.0 / AD)
    inv_col = jnp.tile(inv, 2)[:, None]

    def body(x_ref, w_ref, p_ref, inv_ref, *refs):
        u_ref, cos_ref, sin_ref = refs[n:n + 3]
        start, finish = tg.bind(refs[:n], refs[n + 3:2 * n + 3], *refs[2 * n + 3:])
        i = pl.program_id(0)
        pl.when(i == 0)(start)
        wv = w_ref[...]

        def grp(r0, _):
            xv = x_ref[pl.ds(r0, NRG), :]
            u_ref[pl.ds(r0, NRG), :] = (xv * _rms(xv) * wv).astype(bf16)

        _groups(ts, grp, rg=NRG)
        ang = inv_ref[...] * p_ref[...].astype(f32)
        row = lax.broadcasted_iota(jnp.int32, ang.shape, 0)
        cos_ref[...] = jnp.cos(ang)
        sin_ref[...] = jnp.where(row < half, -1.0, 1.0) * jnp.sin(ang)
        pl.when(i == nt - 1)(finish)

    tab = pl.BlockSpec((AD, ts), lambda i: (0, i))
    res = pl.pallas_call(
        body, name="prenorm_fwd", grid=(nt,),
        in_specs=[pl.BlockSpec((ts, D), lambda i: (i, 0)), pl.BlockSpec((1, D), lambda i: (0, 0)),
                  pl.BlockSpec((1, ts), lambda i: (0, i)), pl.BlockSpec((AD, 1), lambda i: (0, 0))] + tg.in_specs(),
        out_specs=[pl.BlockSpec((ts, D), lambda i: (i, 0)), tab, tab] + tg.out_specs(),
        out_shape=[jax.ShapeDtypeStruct((S, D), bf16), jax.ShapeDtypeStruct((AD, S), f32),
                   jax.ShapeDtypeStruct((AD, S), f32)] + tg.out_shape(),
        scratch_shapes=tg.scratch(),
        compiler_params=_cp("arbitrary", side_effects=True),
    )(x, w, pos_row, inv_col, *tg.bufs)
    return res[0], res[1], res[2], res[3:]


def prenorm_bwd(x, w, dus, dx1, ts, comm=None):
    S = x.shape[0]
    nt = S // ts
    nd = len(dus)
    cm = _Comm(comm)

    def body(*refs):
        ins, (gx_ref, gw_ref), (acc_ref,), copies = cm.split(refs, nd + 3, 2)
        x_ref, w_ref = ins[:2]
        du_refs, dx1_ref = ins[2:2 + nd], ins[2 + nd]
        i = pl.program_id(0)
        cm.start(copies, i == 0)
        wv = w_ref[...]

        @pl.when(i == 0)
        def _():
            acc_ref[...] = jnp.zeros_like(acc_ref)
            gw_ref[...] = jnp.zeros_like(gw_ref)

        def grp(r0, _):
            rows = pl.ds(r0, NRG)
            xv = x_ref[rows, :]
            r = _rms(xv)
            du = du_refs[0][rows, :].astype(f32)
            for d_ref in du_refs[1:]:
                du = du + d_ref[rows, :].astype(f32)
            gx_ref[rows, :] = dx1_ref[rows, :] + _rms_bwd(xv, r, du * wv)
            acc_ref[...] += _fold(du * xv * r)

        _groups(ts, grp, rg=NRG)

        @pl.when(i == nt - 1)
        def _():
            _flush(acc_ref, gw_ref, 0)

        cm.wait(copies, i == nt - 1)

    row = pl.BlockSpec((ts, D), lambda i: (i, 0))
    res = pl.pallas_call(
        body, name="prenorm_bwd", grid=(nt,),
        in_specs=[row, pl.BlockSpec((1, D), lambda i: (0, 0))] + [row] * (nd + 1) + cm.in_specs(),
        out_specs=[row, pl.BlockSpec((8, D), lambda i: (0, 0))] + cm.out_specs(),
        out_shape=[jax.ShapeDtypeStruct((S, D), f32), jax.ShapeDtypeStruct((8, D), f32)] + cm.out_shape(),
        scratch_shapes=[pltpu.VMEM((8, D), f32)] + cm.scratch(),
        compiler_params=_cp("arbitrary", side_effects=bool(cm.n)),
    )(x, w, *dus, dx1, *cm.bufs)
    return res[0], res[1], res[2:]


def post_fwd(x, mo, w_post, w_pre2, ts):
    S = x.shape[0]

    def body(x_ref, mo_ref, wp_ref, w2_ref, x1_ref, h_ref):
        wp, w2 = wp_ref[...], w2_ref[...]

        def grp(r0, _):
            rows = pl.ds(r0, NRG)
            mv = mo_ref[rows, :].astype(f32)
            x1 = x_ref[rows, :] + mv * _rms(mv) * wp
            x1_ref[rows, :] = x1
            h_ref[rows, :] = (x1 * _rms(x1) * w2).astype(bf16)

        _groups(ts, grp, rg=NRG)

    row = pl.BlockSpec((ts, D), lambda i: (i, 0))
    par = pl.BlockSpec((1, D), lambda i: (0, 0))
    return pl.pallas_call(
        body, name="post_fwd", grid=(S // ts,),
        in_specs=[row, row, par, par], out_specs=[row, row],
        out_shape=[jax.ShapeDtypeStruct((S, D), f32), jax.ShapeDtypeStruct((S, D), bf16)],
        compiler_params=_cp("parallel"),
    )(x, mo, w_post, w_pre2)


def post_bwd(dout, dh, x1, mo, w_post, w_pre2, ts):
    S = x1.shape[0]
    nt = S // ts

    def body(dout_ref, dh_ref, x1_ref, mo_ref, wp_ref, w2_ref, dx1_ref, dmo_ref, gw_ref, acc2_ref, accp_ref):
        i = pl.program_id(0)
        wp, w2 = wp_ref[...], w2_ref[...]

        @pl.when(i == 0)
        def _():
            acc2_ref[...] = jnp.zeros_like(acc2_ref)
            accp_ref[...] = jnp.zeros_like(accp_ref)
            gw_ref[...] = jnp.zeros_like(gw_ref)

        def grp(r0, _):
            rows = pl.ds(r0, NRG)
            x1 = x1_ref[rows, :]
            r1 = _rms(x1)
            dh = dh_ref[rows, :].astype(f32)
            dx1 = dout_ref[rows, :] + _rms_bwd(x1, r1, dh * w2)
            dx1_ref[rows, :] = dx1
            acc2_ref[...] += _fold(dh * x1 * r1)
            mv = mo_ref[rows, :].astype(f32)
            rm = _rms(mv)
            dmo_ref[rows, :] = _rms_bwd(mv, rm, dx1 * wp).astype(bf16)
            accp_ref[...] += _fold(dx1 * mv * rm)

        _groups(ts, grp, rg=NRG)

        @pl.when(i == nt - 1)
        def _():
            _flush(acc2_ref, gw_ref, 0)
            _flush(accp_ref, gw_ref, 1)

    row = pl.BlockSpec((ts, D), lambda i: (i, 0))
    par = pl.BlockSpec((1, D), lambda i: (0, 0))
    return pl.pallas_call(
        body, name="post_bwd", grid=(nt,),
        in_specs=[row, row, row, row, par, par],
        out_specs=[row, row, pl.BlockSpec((8, D), lambda i: (0, 0))],
        out_shape=[jax.ShapeDtypeStruct((S, D), f32), jax.ShapeDtypeStruct((S, D), bf16),
                   jax.ShapeDtypeStruct((8, D), f32)],
        scratch_shapes=[pltpu.VMEM((8, D), f32), pltpu.VMEM((8, D), f32)],
        compiler_params=_cp("arbitrary"),
    )(dout, dh, x1, mo, w_post, w_pre2)


def loss_head(x1, ff, target, w, ts):
    S = x1.shape[0]
    nt = S // ts

    def body(x1_ref, ff_ref, t_ref, w_ref, loss_ref, dout_ref, dff_ref, gw_ref, accw_ref, accl_ref):
        i = pl.program_id(0)
        wv = w_ref[...]

        @pl.when(i == 0)
        def _():
            accw_ref[...] = jnp.zeros_like(accw_ref)
            accl_ref[...] = jnp.zeros_like(accl_ref)
            gw_ref[...] = jnp.zeros_like(gw_ref)

        def grp(r0, _):
            rows = pl.ds(r0, NRG)
            fv = ff_ref[rows, :].astype(f32)
            r = _rms(fv)
            n = fv * r
            e = x1_ref[rows, :] + n * wv - t_ref[rows, :]
            dout = e * (1.0 / D)
            dout_ref[rows, :] = dout
            dff_ref[rows, :] = _rms_bwd(fv, r, dout * wv).astype(bf16)
            accw_ref[...] += _fold(dout * n)
            accl_ref[...] += _fold(e * e)

        _groups(ts, grp, rg=NRG)

        @pl.when(i == nt - 1)
        def _():
            _flush(accw_ref, gw_ref, 0)
            tot = jnp.sum(jnp.sum(accl_ref[...], axis=1, keepdims=True), axis=0, keepdims=True) * (0.5 / D)
            loss_ref[...] = jnp.broadcast_to(tot, loss_ref.shape)

    row = pl.BlockSpec((ts, D), lambda i: (i, 0))
    return pl.pallas_call(
        body, name="loss_head", grid=(nt,),
        in_specs=[row, row, row, pl.BlockSpec((1, D), lambda i: (0, 0))],
        out_specs=[pl.BlockSpec((8, LANES), lambda i: (0, 0)), row, row, pl.BlockSpec((8, D), lambda i: (0, 0))],
        out_shape=[jax.ShapeDtypeStruct((8, LANES), f32), jax.ShapeDtypeStruct((S, D), f32),
                   jax.ShapeDtypeStruct((S, D), bf16), jax.ShapeDtypeStruct((8, D), f32)],
        scratch_shapes=[pltpu.VMEM((8, D), f32), pltpu.VMEM((8, D), f32)],
        compiler_params=_cp("arbitrary"),
    )(x1, ff, target, w)


def _taps(w_ref, cs, K):
    return [jnp.broadcast_to(w_ref[k:k + 1, cs], (8, CW)) for k in range(K)]


def _down(before, cur, s, sub):
    return jnp.where(sub < s, pltpu.roll(before, s, 0), pltpu.roll(cur, s, 0))


def _up(cur, after, s, sub):
    return jnp.where(sub < 8 - s, pltpu.roll(cur, 8 - s, 0), pltpu.roll(after, 8 - s, 0))


def _conv_group(p, a, b, taps, bias, K, sub):
    ya, yb = bias, bias
    for k in range(K):
        s = K - 1 - k
        xa, xb = (a, b) if s == 0 else (_down(p, a, s, sub), _down(a, b, s, sub))
        ya = ya + taps[k] * xa
        yb = yb + taps[k] * xb
    return ya, yb


def _prev8_map(ts, cb):
    return lambda i, j: (jnp.maximum(i * (ts // 8) - 1, 0), cb + j)


def ssdconv_fwd(proj, w8, b, ts):
    S = proj.shape[0]
    bw = 1024
    cb = C_XBC // bw

    def body(cur_ref, prev_ref, w_ref, b_ref, o_ref, c_ref):
        first = pl.program_id(0) == 0
        sub = lax.broadcasted_iota(jnp.int32, (8, CW), 0)
        for c0 in range(0, bw, CW):
            cs = slice(c0, c0 + CW)
            taps = _taps(w_ref, cs, SSD_K)
            bias = jnp.broadcast_to(b_ref[:, cs], (8, CW))

            def grp(r0, p, cs=cs, taps=taps, bias=bias):
                rows = pl.ds(r0, RG)
                xv = cur_ref[rows, cs].astype(f32)
                ya, yb = _conv_group(p, xv[0:8], xv[8:16], taps, bias, SSD_K, sub)
                y = jnp.concatenate([ya, yb], axis=0)
                c_ref[rows, cs] = y.astype(bf16)
                o_ref[rows, cs] = (y * _sigmoid_fast(y)).astype(bf16)
                return xv[8:16]

            _groups(ts, grp, jnp.where(first, 0.0, prev_ref[:, cs].astype(f32)))

    o = jax.ShapeDtypeStruct((S, CONVD), bf16)
    blk = pl.BlockSpec((ts, bw), lambda i, j: (i, j))
    return pl.pallas_call(
        body, name="ssdconv_fwd", grid=(S // ts, CONVD // bw),
        in_specs=[pl.BlockSpec((ts, bw), lambda i, j: (i, cb + j)),
                  pl.BlockSpec((8, bw), _prev8_map(ts, cb)),
                  pl.BlockSpec((8, bw), lambda i, j: (0, j)),
                  pl.BlockSpec((1, bw), lambda i, j: (0, j))],
        out_specs=[blk, blk], out_shape=[o, o],
        compiler_params=_cp("parallel", "parallel"),
    )(proj, proj, w8, b)


def _gelu_tanh(x):
    c = 0.7978845608028654
    t = jnp.tanh(c * (x + 0.044715 * x * x * x))
    return 0.5 * x * (1.0 + t), t


def ffnact_fwd(up, w8, b, ts):
    S = up.shape[0]

    def body(g_ref, gp_ref, v_ref, vp_ref, wg_ref, wv_ref, bg_ref, bv_ref, o_ref, gc_ref, vc_ref):
        first = pl.program_id(0) == 0
        sub = lax.broadcasted_iota(jnp.int32, (8, CW), 0)
        for c0 in range(0, FF, CW):
            cs = slice(c0, c0 + CW)
            tg, tv = _taps(wg_ref, cs, FFN_K), _taps(wv_ref, cs, FFN_K)
            bg = jnp.broadcast_to(bg_ref[:, cs], (8, CW))
            bv = jnp.broadcast_to(bv_ref[:, cs], (8, CW))

            def grp(r0, carry, cs=cs, tg=tg, tv=tv, bg=bg, bv=bv):
                pg, pv = carry
                rows = pl.ds(r0, RG)
                gx = g_ref[rows, cs].astype(f32)
                vx = v_ref[rows, cs].astype(f32)
                g = jnp.concatenate(_conv_group(pg, gx[0:8], gx[8:16], tg, bg, FFN_K, sub), axis=0)
                v = jnp.concatenate(_conv_group(pv, vx[0:8], vx[8:16], tv, bv, FFN_K, sub), axis=0)
                gc_ref[rows, cs] = g.astype(bf16)
                vc_ref[rows, cs] = v.astype(bf16)
                o_ref[rows, cs] = (_gelu_tanh(g)[0] * v).astype(bf16)
                return gx[8:16], vx[8:16]

            _groups(ts, grp, (jnp.where(first, 0.0, gp_ref[:, cs].astype(f32)),
                              jnp.where(first, 0.0, vp_ref[:, cs].astype(f32))))

    o = jax.ShapeDtypeStruct((S, FF), bf16)
    blk = pl.BlockSpec((ts, FF), lambda i: (i, 0))
    prev = lambda cb: pl.BlockSpec((8, FF), lambda i: (jnp.maximum(i * (ts // 8) - 1, 0), cb))
    return pl.pallas_call(
        body, name="ffnact_fwd", grid=(S // ts,),
        in_specs=[blk, prev(0), pl.BlockSpec((ts, FF), lambda i: (i, 1)), prev(1),
                  pl.BlockSpec((8, FF), lambda i: (0, 0)), pl.BlockSpec((8, FF), lambda i: (0, 1)),
                  pl.BlockSpec((1, FF), lambda i: (0, 0)), pl.BlockSpec((1, FF), lambda i: (0, 1))],
        out_specs=[blk, blk, blk], out_shape=[o, o, o],
        compiler_params=_cp("parallel"),
    )(up, up, up, up, w8, w8, b, b)


def ffnact_bwd(dact, gc, vc, ts):
    S = dact.shape[0]

    def body(d_ref, g_ref, v_ref, dg_ref, dv_ref):
        c = 0.7978845608028654
        for c0 in range(0, FF, CW):
            cs = slice(c0, c0 + CW)

            def grp(r0, _, cs=cs):
                rows = pl.ds(r0, RG)
                d = d_ref[rows, cs].astype(f32)
                g = g_ref[rows, cs].astype(f32)
                ge, t = _gelu_tanh(g)
                dgelu = 0.5 * (1.0 + t) + 0.5 * g * (1.0 - t * t) * c * (1.0 + 3.0 * 0.044715 * g * g)
                dg_ref[rows, cs] = (d * v_ref[rows, cs].astype(f32) * dgelu).astype(bf16)
                dv_ref[rows, cs] = (d * ge).astype(bf16)

            _groups(ts, grp)

    o = jax.ShapeDtypeStruct((S, FF), bf16)
    blk = pl.BlockSpec((ts, FF), lambda i: (i, 0))
    return pl.pallas_call(
        body, name="ffnact_bwd", grid=(S // ts,),
        in_specs=[blk, blk, blk], out_specs=[blk, blk], out_shape=[o, o],
        compiler_params=_cp("parallel"),
    )(dact, gc, vc)


def dwconv_bwd(dy, x, xcb, w8, wcb, K, bw, ts, name, act_c=None, into=None, ocb=0, out_cols=None):
    S, C = dy.shape
    nr = S // ts
    out_cols = out_cols or C
    n_act = 0 if act_c is None else 2

    def body(*refs):
        dy_ref, dyn_ref = refs[0:2]
        c_ref, cn_ref = (refs[2:4] if n_act else (None, None))
        x_ref, xp_ref, w_ref = refs[2 + n_act:5 + n_act]
        dx_ref, dw_ref, sd_ref = refs[-3:]
        i = pl.program_id(1)
        first, last = i == 0, i == nr - 1
        sub = lax.broadcasted_iota(jnp.int32, (8, CW), 0)

        def grad_y(d, c):
            if c is None:
                return d.astype(f32)
            cv = c.astype(f32)
            s = _sigmoid_fast(cv)
            return d.astype(f32) * s * (1.0 + cv * (1.0 - s))

        @pl.when(first)
        def _():
            dw_ref[...] = jnp.zeros_like(dw_ref)

        for c0 in range(0, bw, CW):
            cs = slice(c0, c0 + CW)
            taps = _taps(w_ref, cs, K)
            zero = jnp.zeros((8, CW), f32)

            def fwd(r0, carry, cs=cs):
                p, accs, accb = carry
                rows = pl.ds(r0, RG)
                g = grad_y(dy_ref[rows, cs], c_ref[rows, cs] if n_act else None)
                sd_ref[rows, cs] = g
                xv = x_ref[rows, cs].astype(f32)
                a, b = xv[0:8], xv[8:16]
                ga, gb = g[0:8], g[8:16]
                new = []
                for k in range(K):
                    s = K - 1 - k
                    xa, xb = (a, b) if s == 0 else (_down(p, a, s, sub), _down(a, b, s, sub))
                    new.append(accs[k] + ga * xa + gb * xb)
                return b, tuple(new), accb + ga + gb

            _, accs, accb = _groups(ts, fwd, (jnp.where(first, 0.0, xp_ref[:, cs].astype(f32)), (zero,) * K, zero))
            for k in range(K):
                dw_ref[k:k + 1, cs] += jnp.sum(accs[k], axis=0, keepdims=True)
            dw_ref[7:8, cs] += jnp.sum(accb, axis=0, keepdims=True)

            def bwd(r0, after, cs=cs, taps=taps):
                rows = pl.ds(r0, RG)
                g = sd_ref[rows, cs]
                a, b = g[0:8], g[8:16]
                da, db = zero, zero
                for k in range(K):
                    s = K - 1 - k
                    ua, ub = (a, b) if s == 0 else (_up(a, b, s, sub), _up(b, after, s, sub))
                    da = da + taps[k] * ua
                    db = db + taps[k] * ub
                dx_ref[rows, cs] = jnp.concatenate([da, db], axis=0).astype(bf16)
                return a

            halo = grad_y(dyn_ref[:, cs], cn_ref[:, cs] if n_act else None)
            _groups(ts, bwd, jnp.where(last, 0.0, halo), reverse=True)

    nxt = lambda j, i: (jnp.minimum((i + 1) * (ts // 8), S // 8 - 1), j)
    tile = pl.BlockSpec((ts, bw), lambda j, i: (i, j))
    acts = [] if act_c is None else [act_c, act_c]
    extra = [] if into is None else [into]
    n_in = 5 + n_act
    return pl.pallas_call(
        body, name=name, grid=(C // bw, nr),
        in_specs=[tile, pl.BlockSpec((8, bw), nxt)] + ([tile, pl.BlockSpec((8, bw), nxt)] if n_act else []) + [
            pl.BlockSpec((ts, bw), lambda j, i: (i, xcb + j)),
            pl.BlockSpec((8, bw), lambda j, i: (jnp.maximum(i * (ts // 8) - 1, 0), xcb + j)),
            pl.BlockSpec((8, bw), lambda j, i: (0, wcb + j))] + [pl.BlockSpec(memory_space=pl.ANY)] * len(extra),
        out_specs=[pl.BlockSpec((ts, bw), lambda j, i: (i, ocb + j)), pl.BlockSpec((8, bw), lambda j, i: (0, j))],
        out_shape=[jax.ShapeDtypeStruct((S, out_cols), bf16), jax.ShapeDtypeStruct((8, C), f32)],
        scratch_shapes=[pltpu.VMEM((ts, bw), f32)],
        input_output_aliases={n_in: 0} if extra else {},
        compiler_params=_cp("parallel", "arbitrary"),
    )(dy, dy, *acts, x, x, w8, *extra)


def gnorm_fwd(y, proj, w, ts):
    S = y.shape[0]

    def body(y_ref, z_ref, w_ref, o_ref):
        for k in range(NG):
            sl = slice(k * GW, (k + 1) * GW)
            wv = w_ref[:, sl]

            def grp(r0, _, sl=sl, wv=wv):
                rows = pl.ds(r0, NRG)
                z = z_ref[rows, sl].astype(f32)
                g = y_ref[rows, sl].astype(f32) * z * _sigmoid_fast(z)
                o_ref[rows, sl] = (g * _rms(g) * wv).astype(bf16)

            _groups(ts, grp, rg=NRG)

    row = pl.BlockSpec((ts, DI), lambda i: (i, 0))
    return pl.pallas_call(
        body, name="gnorm_fwd", grid=(S // ts,),
        in_specs=[row, row, pl.BlockSpec((1, DI), lambda i: (0, 0))],
        out_specs=row, out_shape=jax.ShapeDtypeStruct((S, DI), bf16),
        compiler_params=_cp("parallel"),
    )(y, proj, w)


def gnorm_bwd(dyn, y, proj, w, dproj, ts):
    S = y.shape[0]
    nt = S // ts

    def body(d_ref, y_ref, z_ref, w_ref, _, dy_ref, dz_ref, gw_ref, acc_ref):
        i = pl.program_id(0)

        @pl.when(i == 0)
        def _():
            acc_ref[...] = jnp.zeros_like(acc_ref)
            gw_ref[...] = jnp.zeros_like(gw_ref)

        for k in range(NG):
            sl = slice(k * GW, (k + 1) * GW)
            wv = w_ref[:, sl]

            def grp(r0, _, sl=sl, wv=wv):
                rows = pl.ds(r0, NRG)
                z = z_ref[rows, sl].astype(f32)
                yv = y_ref[rows, sl].astype(f32)
                s = _sigmoid_fast(z)
                sz = z * s
                g = yv * sz
                r = _rms(g)
                d = d_ref[rows, sl].astype(f32)
                acc_ref[:, sl] += _fold(d * g * r)
                dg = _rms_bwd(g, r, d * wv)
                dy_ref[rows, sl] = (dg * sz).astype(bf16)
                dz_ref[rows, sl] = (dg * yv * s * (1.0 + z * (1.0 - s))).astype(bf16)

            _groups(ts, grp, rg=NRG)

        @pl.when(i == nt - 1)
        def _():
            _flush(acc_ref, gw_ref, 0)

    row = pl.BlockSpec((ts, DI), lambda i: (i, 0))
    return pl.pallas_call(
        body, name="gnorm_bwd", grid=(nt,),
        in_specs=[row, row, row, pl.BlockSpec((1, DI), lambda i: (0, 0)), pl.BlockSpec(memory_space=pl.ANY)],
        out_specs=[row, row, pl.BlockSpec((8, DI), lambda i: (0, 0))],
        out_shape=[jax.ShapeDtypeStruct((S, DI), bf16), jax.ShapeDtypeStruct(dproj.shape, bf16),
                   jax.ShapeDtypeStruct((8, DI), f32)],
        scratch_shapes=[pltpu.VMEM((8, DI), f32)],
        input_output_aliases={4: 1},
        compiler_params=_cp("arbitrary"),
    )(dyn, y, proj, w, dproj)


def merge_fwd(proj, ys, ya, ts):
    S = ys.shape[0]

    def body(gs_ref, ga_ref, ys_ref, ya_ref, o_ref):
        for c0 in range(0, D, CW):
            cs = slice(c0, c0 + CW)

            def grp(r0, _, cs=cs):
                rows = pl.ds(r0, NRG)
                o_ref[rows, cs] = (_sigmoid_fast(gs_ref[rows, cs].astype(f32)) * ys_ref[rows, cs].astype(f32)
                                   + _sigmoid_fast(ga_ref[rows, cs].astype(f32)) * ya_ref[rows, cs].astype(f32)
                                   ).astype(bf16)

            _groups(ts, grp, rg=NRG)

    row = pl.BlockSpec((ts, D), lambda i: (i, 0))
    return pl.pallas_call(
        body, name="merge_fwd", grid=(S // ts,),
        in_specs=[pl.BlockSpec((ts, D), lambda i: (i, C_GS // D)), pl.BlockSpec((ts, D), lambda i: (i, C_GA // D)), row, row],
        out_specs=row, out_shape=jax.ShapeDtypeStruct((S, D), bf16),
        compiler_params=_cp("parallel"),
    )(proj, proj, ys, ya)


def merge_bwd(dm, proj, ys, ya, ts):
    S = ys.shape[0]

    def body(d_ref, gs_ref, ga_ref, ys_ref, ya_ref, dys_ref, dya_ref, dg_ref):
        for c0 in range(0, D, CW):
            cs = slice(c0, c0 + CW)

            def grp(r0, _, c0=c0, cs=cs):
                rows = pl.ds(r0, NRG)
                d = d_ref[rows, cs].astype(f32)
                ss = _sigmoid_fast(gs_ref[rows, cs].astype(f32))
                sa = _sigmoid_fast(ga_ref[rows, cs].astype(f32))
                dys_ref[rows, cs] = (d * ss).astype(bf16)
                dya_ref[rows, cs] = (d * sa).astype(bf16)
                dg_ref[rows, cs] = (d * ys_ref[rows, cs].astype(f32) * ss * (1.0 - ss)).astype(bf16)
                dg_ref[rows, D + c0:D + c0 + CW] = (d * ya_ref[rows, cs].astype(f32) * sa * (1.0 - sa)).astype(bf16)

            _groups(ts, grp, rg=NRG)

    row = pl.BlockSpec((ts, D), lambda i: (i, 0))
    o = jax.ShapeDtypeStruct((S, D), bf16)
    return pl.pallas_call(
        body, name="merge_bwd", grid=(S // ts,),
        in_specs=[row, pl.BlockSpec((ts, D), lambda i: (i, C_GS // D)), pl.BlockSpec((ts, D), lambda i: (i, C_GA // D)), row, row],
        out_specs=[row, row, pl.BlockSpec((ts, 2 * D), lambda i: (i, C_GS // (2 * D)))],
        out_shape=[o, o, jax.ShapeDtypeStruct((S, PM), bf16)],
        compiler_params=_cp("parallel"),
    )(dm, proj, proj, ys, ya)


def _ssd_consts():
    h = lax.broadcasted_iota(jnp.int32, (LANES, DI), 0)
    c = lax.broadcasted_iota(jnp.int32, (LANES, DI), 1)
    expand = (c // HD == h).astype(bf16)
    r = lax.broadcasted_iota(jnp.int32, (CH, CH), 0)
    cc = lax.broadcasted_iota(jnp.int32, (CH, CH), 1)
    tril = (cc <= r).astype(bf16)
    triu = (cc >= r).astype(bf16)
    return expand, expand.T, tril, triu


def _ssd_common(xbc_ref, dtr_ref, bias_ref, alog_ref, tril_ref, expand_ref=None, saved=None):
    dtr = dtr_ref[...] + bias_ref[...]
    dt = jnp.maximum(dtr, 0.0) + jnp.log1p(jnp.exp(-jnp.abs(dtr)))
    a = -jnp.exp(alog_ref[...])
    acs = _dot3_left(tril_ref[...], dt * a)
    if saved is None:
        acsx = _dot3_right(acs, expand_ref[...])
        dtx = _dot3_right(dt, expand_ref[...])
    else:
        acsx, dtx = saved[0][...], saved[1][...]
    x = xbc_ref[:, 0:DI].astype(f32)
    xdt = x * dtx
    e = jnp.exp(acsx)
    dsx = jnp.exp(acsx[CH - 1:CH, :] - acsx)
    return dtr, dt, a, acs, acsx, dtx, x, xdt, e, dsx


def _ssd_lmat(acs, acs_t, hh, causal):
    seg = acs[:, hh:hh + 1] - acs_t[hh:hh + 1, :]
    return jnp.where(causal, jnp.exp(jnp.minimum(seg, 0.0)), 0.0)


def ssd_fwd(xbc, dtr, bias, alog, dx_row, comm=None):
    S = xbc.shape[0]
    nc = S // CH
    expand, _, tril, _ = _ssd_consts()
    cm = _Comm(comm)

    def body(*refs):
        ins, (y_ref, hp_ref, ax_ref, dtx_ref), (h_ref, yd_ref), copies = cm.split(refs, 7, 4)
        xbc_ref, dtr_ref, bias_ref, alog_ref, dxr_ref, expand_ref, tril_ref = ins
        c = pl.program_id(0)
        cm.start(copies, c == 0)

        @pl.when(c == 0)
        def _():
            h_ref[...] = jnp.zeros_like(h_ref)

        _, _, _, acs, acsx, dtx, x, xdt, e, dsx = _ssd_common(xbc_ref, dtr_ref, bias_ref, alog_ref, tril_ref,
                                                              expand_ref=expand_ref)
        ax_ref[...] = acsx
        dtx_ref[...] = dtx
        acs_t = acs.T
        xb = xdt.astype(bf16)
        xd = (xdt * dsx).astype(bf16)
        causal = tril_ref[...] > 0
        for g in range(NG):
            gs = slice(g * GW, (g + 1) * GW)
            bg = xbc_ref[:, DI + g * NS:DI + (g + 1) * NS]
            cg = xbc_ref[:, DI + NG * NS + g * NS:DI + NG * NS + (g + 1) * NS]
            cb = _dot(cg, bg, "nt")
            hp = h_ref[g]
            hpb = hp.astype(bf16)
            hp_ref[0, g] = hpb
            yd_ref[:, gs] = _dot(cg, hpb) * e[:, gs]
            h_ref[g] = hp * e[CH - 1:CH, gs] + _dot(bg, xd[:, gs], "tn")
            for j in range(NH // NG):
                hh = g * (NH // NG) + j
                hs = slice(hh * HD, (hh + 1) * HD)
                m = (cb * _ssd_lmat(acs, acs_t, hh, causal)).astype(bf16)
                yd_ref[:, hs] += _dot(m, xb[:, hs])
        y_ref[...] = (yd_ref[...] + dxr_ref[...] * x).astype(bf16)
        cm.wait(copies, c == nc - 1)

    par = lambda shape: pl.BlockSpec(shape, lambda c: (0,) * len(shape))
    res = pl.pallas_call(
        body, name="ssd_fwd", grid=(nc,),
        in_specs=[pl.BlockSpec((CH, CONVD), lambda c: (c, 0)), pl.BlockSpec((CH, LANES), lambda c: (c, 0)),
                  par((1, LANES)), par((1, LANES)), par((1, DI)), par((LANES, DI)), par((CH, CH))] + cm.in_specs(),
        out_specs=[pl.BlockSpec((CH, DI), lambda c: (c, 0)), pl.BlockSpec((1, NG, NS, GW), lambda c: (c, 0, 0, 0)),
                   pl.BlockSpec((CH, DI), lambda c: (c, 0)), pl.BlockSpec((CH, DI), lambda c: (c, 0))] + cm.out_specs(),
        out_shape=[jax.ShapeDtypeStruct((S, DI), bf16), jax.ShapeDtypeStruct((nc, NG, NS, GW), bf16),
                   jax.ShapeDtypeStruct((S, DI), f32), jax.ShapeDtypeStruct((S, DI), f32)] + cm.out_shape(),
        scratch_shapes=[pltpu.VMEM((NG, NS, GW), f32), pltpu.VMEM((CH, DI), f32)] + cm.scratch(),
        compiler_params=_cp("arbitrary", side_effects=bool(cm.n)),
    )(xbc, dtr, bias, alog, dx_row, expand, tril, *cm.bufs)
    return res[0], res[1], (res[2], res[3]), res[4:]


def ssd_bwd(xbc, dtr, dy, hprev, saved, bias, alog, dx_row, comm=None):
    S = xbc.shape[0]
    nc = S // CH
    _, expand_t, tril, triu = _ssd_consts()
    cm = _Comm(comm)

    def body(*refs):
        ins, outs, scr, copies = cm.split(refs, 12, 3)
        xbc_ref, dtr_ref, dy_ref, hp_ref, ax_ref, dtx_ref, bias_ref, alog_ref, dxr_ref, expt_ref, tril_ref, triu_ref = ins
        dxbc_ref, ddtr_ref, acc_ref = outs
        dh_ref, dxs_ref, t_ref, accb_ref, acca_ref, accd_ref = scr
        c = pl.program_id(0)
        cm.start(copies, c == 0)

        @pl.when(c == 0)
        def _():
            dh_ref[...] = jnp.zeros_like(dh_ref)
            accb_ref[...] = jnp.zeros_like(accb_ref)
            acca_ref[...] = jnp.zeros_like(acca_ref)
            accd_ref[...] = jnp.zeros_like(accd_ref)

        dtr, dt, a, acs, _, dtx, x, xdt, e, dsx = _ssd_common(xbc_ref, dtr_ref, bias_ref, alog_ref, tril_ref,
                                                              saved=(ax_ref, dtx_ref))
        acs_t = acs.T
        xb = xdt.astype(bf16)
        xdf = xdt * dsx
        xd = xdf.astype(bf16)
        dyv = dy_ref[...].astype(f32)
        dyb = dy_ref[...]
        dye = (dyv * e).astype(bf16)
        causal = tril_ref[...] > 0
        lane = lax.broadcasted_iota(jnp.int32, (CH, LANES), 1)
        subl = lax.broadcasted_iota(jnp.int32, (LANES, CH), 0)
        ccol = jnp.zeros((CH, LANES), f32)
        rrow = jnp.zeros((LANES, CH), f32)
        last_row = lax.broadcasted_iota(jnp.int32, (CH, 1), 0) == CH - 1
        for g in range(NG):
            gs = slice(g * GW, (g + 1) * GW)
            bsl = slice(DI + g * NS, DI + (g + 1) * NS)
            csl = slice(DI + NG * NS + g * NS, DI + NG * NS + (g + 1) * NS)
            bg = xbc_ref[:, bsl]
            cg = xbc_ref[:, csl]
            cb = _dot(cg, bg, "nt")
            hpb = hp_ref[0, g]
            dhn = dh_ref[g]
            dhnb = dhn.astype(bf16)
            yoff = _dot(cg, hpb) * e[:, gs]
            dxd = _dot(bg, dhnb)
            t2 = dxd * xdf[:, gs]
            t3 = jnp.sum(dhn * hpb.astype(f32), axis=0, keepdims=True) * e[CH - 1:CH, gs]
            t_ref[:, gs] = dyv[:, gs] * yoff - t2 + jnp.where(last_row, jnp.sum(t2, axis=0, keepdims=True) + t3, 0.0)
            dxs_ref[:, gs] = dxd * dsx[:, gs]
            dcg = _dot(dye[:, gs], hpb, "nt")
            dbg = _dot(xd[:, gs], dhnb, "nt")
            dh_ref[g] = dhn * e[CH - 1:CH, gs] + _dot(cg, dye[:, gs], "tn")
            dcb = jnp.zeros((CH, CH), f32)
            for j in range(NH // NG):
                hh = g * (NH // NG) + j
                hs = slice(hh * HD, (hh + 1) * HD)
                lm = _ssd_lmat(acs, acs_t, hh, causal)
                m = cb * lm
                dm = _dot(dyb[:, hs], xb[:, hs], "nt")
                gm = dm * m
                ccol = ccol + jnp.sum(gm, axis=1, keepdims=True) * (lane == hh).astype(f32)
                rrow = rrow + jnp.sum(gm, axis=0, keepdims=True) * (subl == hh).astype(f32)
                dcb = dcb + dm * lm
                dxs_ref[:, hs] += _dot(m.astype(bf16), dyb[:, hs], "tn")
            dcbb = dcb.astype(bf16)
            dxbc_ref[:, csl] = (dcg + _dot(dcbb, bg)).astype(bf16)
            dxbc_ref[:, bsl] = (dbg + _dot(dcbb, cg, "tn")).astype(bf16)
        dxf = dxs_ref[...]
        dxbc_ref[:, 0:DI] = (dxf * dtx + dxr_ref[...] * dyv).astype(bf16)
        expt = expt_ref[...]
        dacs = ccol - rrow.T + _dot2_right(t_ref[...], expt)
        dadt = _dot3_left(triu_ref[...], dacs)
        ddt = _dot2_right(dxf * x, expt) + dadt * a
        ddtr = ddt * _sigmoid(dtr)
        ddtr_ref[...] = ddtr
        accb_ref[...] += ddtr
        acca_ref[...] += dadt * dt
        accd_ref[...] += _dot2_right(dyv * x, expt)

        @pl.when(c == nc - 1)
        def _():
            acc_ref[...] = jnp.zeros_like(acc_ref)
            acc_ref[0:1, :] = jnp.sum(accb_ref[...], axis=0, keepdims=True)
            acc_ref[1:2, :] = jnp.sum(acca_ref[...], axis=0, keepdims=True) * a
            acc_ref[2:3, :] = jnp.sum(accd_ref[...], axis=0, keepdims=True)

        cm.wait(copies, c == nc - 1)

    par = lambda shape: pl.BlockSpec(shape, lambda c: (0,) * len(shape))
    rev = lambda c: (nc - 1 - c, 0)
    res = pl.pallas_call(
        body, name="ssd_bwd", grid=(nc,),
        in_specs=[pl.BlockSpec((CH, CONVD), rev), pl.BlockSpec((CH, LANES), rev), pl.BlockSpec((CH, DI), rev),
                  pl.BlockSpec((1, NG, NS, GW), lambda c: (nc - 1 - c, 0, 0, 0)),
                  pl.BlockSpec((CH, DI), rev), pl.BlockSpec((CH, DI), rev),
                  par((1, LANES)), par((1, LANES)), par((1, DI)), par((DI, LANES)),
                  par((CH, CH)), par((CH, CH))] + cm.in_specs(),
        out_specs=[pl.BlockSpec((CH, CONVD), rev), pl.BlockSpec((CH, LANES), rev), par((8, LANES))] + cm.out_specs(),
        out_shape=[jax.ShapeDtypeStruct((S, CONVD), bf16), jax.ShapeDtypeStruct((S, LANES), f32),
                   jax.ShapeDtypeStruct((8, LANES), f32)] + cm.out_shape(),
        scratch_shapes=[pltpu.VMEM((NG, NS, GW), f32), pltpu.VMEM((CH, DI), f32), pltpu.VMEM((CH, DI), f32),
                        pltpu.VMEM((CH, LANES), f32), pltpu.VMEM((CH, LANES), f32),
                        pltpu.VMEM((CH, LANES), f32)] + cm.scratch(),
        compiler_params=_cp("arbitrary", side_effects=bool(cm.n)),
    )(xbc, dtr, dy, hprev, *saved, bias, alog, dx_row, expand_t, tril, triu, *cm.bufs)
    return res[0], res[1], res[2], res[3:]


def _partner(t):
    half = AD // 2
    return jnp.concatenate([t[h * AD + o:h * AD + o + half] for h in range(t.shape[0] // AD) for o in (half, 0)], axis=0)


def _rope(t, cos, sin):
    reps = t.shape[0] // AD
    return t * jnp.tile(cos, (reps, 1)) + _partner(t) * jnp.tile(sin, (reps, 1))


def _rope_t(d, cos, sin):
    reps = d.shape[0] // AD
    return d * jnp.tile(cos, (reps, 1)) - _partner(d) * jnp.tile(sin, (reps, 1))


def _lanes_of_group(t, g):
    return jnp.concatenate([t[(g * REP + r) * AD:(g * REP + r + 1) * AD] for r in range(REP)], axis=1)


def _attn_probs(qg, kp, kc, sink_ref, g, not_first):
    n = qg.shape[1]
    s = lax.broadcasted_iota(jnp.int32, (WIN, n), 0)
    t = lax.broadcasted_iota(jnp.int32, (WIN, n), 1) % WIN
    neg = -1e30
    sink = jnp.concatenate([jnp.broadcast_to(sink_ref[0:1, g * REP + r:g * REP + r + 1], (1, WIN)) for r in range(REP)],
                           axis=1)
    sp = jnp.where(jnp.logical_and(s > t, not_first), _dot(kp, qg, "tn"), neg)
    sc = jnp.where(s <= t, _dot(kc, qg, "tn"), neg)
    m = jnp.maximum(jnp.maximum(jnp.max(sp, axis=0, keepdims=True), jnp.max(sc, axis=0, keepdims=True)), sink)
    pp = jnp.exp(sp - m)
    pc = jnp.exp(sc - m)
    ps = jnp.exp(sink - m)
    inv = 1.0 / (jnp.sum(pp, axis=0, keepdims=True) + jnp.sum(pc, axis=0, keepdims=True) + ps)
    return pp * inv, pc * inv, ps * inv


def attn_fwd(qt, kvt, cos, sin, sinks):
    S = qt.shape[1]
    nb = S // WIN
    cur = lambda i: (0, i)
    prev = lambda i: (0, jnp.maximum(i - 1, 0))

    def body(q_ref, kv_ref, kvp_ref, cos_ref, sin_ref, cosp_ref, sinp_ref, sink_ref, o_ref):
        i = pl.program_id(0)
        q = (_rope(q_ref[...].astype(f32), cos_ref[...], sin_ref[...]) * (AD ** -0.5)).astype(bf16)
        kc = _rope(kv_ref[0:KVW, :].astype(f32), cos_ref[...], sin_ref[...]).astype(bf16)
        kp = _rope(kvp_ref[0:KVW, :].astype(f32), cosp_ref[...], sinp_ref[...]).astype(bf16)
        for g in range(KVH):
            ks = slice(g * AD, (g + 1) * AD)
            vs = slice(KVW + g * AD, KVW + (g + 1) * AD)
            pp, pc, _ = _attn_probs(_lanes_of_group(q, g), kp[ks], kc[ks], sink_ref, g, i > 0)
            o = _dot(kvp_ref[vs, :], pp.astype(bf16)) + _dot(kv_ref[vs, :], pc.astype(bf16))
            for r in range(REP):
                h = g * REP + r
                o_ref[h * AD:(h + 1) * AD, :] = o[:, r * WIN:(r + 1) * WIN].astype(bf16)

    tab = pl.BlockSpec((AD, WIN), cur)
    tabp = pl.BlockSpec((AD, WIN), prev)
    return pl.pallas_call(
        body, name="attn_fwd", grid=(nb,),
        in_specs=[pl.BlockSpec((D, WIN), cur), pl.BlockSpec((2 * KVW, WIN), cur), pl.BlockSpec((2 * KVW, WIN), prev),
                  tab, tab, tabp, tabp, pl.BlockSpec((1, LANES), lambda i: (0, 0))],
        out_specs=pl.BlockSpec((D, WIN), cur),
        out_shape=jax.ShapeDtypeStruct((D, S), bf16),
        compiler_params=_cp("parallel"),
    )(qt, kvt, kvt, cos, sin, cos, sin, sinks)


def attn_bwd(qt, kvt, cos, sin, sinks, daot, comm=None):
    S = qt.shape[1]
    nb = S // WIN
    cur = lambda i: (0, jnp.minimum(i, nb - 1))
    prev = lambda i: (0, jnp.maximum(i - 1, 0))
    cm = _Comm(comm)

    def body(*refs):
        ins, (dq_ref, dkv_ref, ds_ref), scr, copies = cm.split(refs, 9, 3)
        q_ref, kv_ref, kvp_ref, cos_ref, sin_ref, cosp_ref, sinp_ref, sink_ref, do_ref = ins
        ck_ref, cv_ref, dqs_ref, dkp_ref, dvp_ref, dkc_ref, dvc_ref, accs_ref = scr
        i = pl.program_id(0)
        cm.start(copies, i == 0)

        @pl.when(i == 0)
        def _():
            ck_ref[...] = jnp.zeros_like(ck_ref)
            cv_ref[...] = jnp.zeros_like(cv_ref)
            accs_ref[...] = jnp.zeros_like(accs_ref)

        @pl.when(i == nb)
        def _():
            dkp_ref[...] = jnp.zeros_like(dkp_ref)
            dvp_ref[...] = jnp.zeros_like(dvp_ref)

        @pl.when(i < nb)
        def _():
            q = (_rope(q_ref[...].astype(f32), cos_ref[...], sin_ref[...]) * (AD ** -0.5)).astype(bf16)
            kc = _rope(kv_ref[0:KVW, :].astype(f32), cos_ref[...], sin_ref[...]).astype(bf16)
            kp = _rope(kvp_ref[0:KVW, :].astype(f32), cosp_ref[...], sinp_ref[...]).astype(bf16)
            do = do_ref[...]
            for g in range(KVH):
                ks = slice(g * AD, (g + 1) * AD)
                vs = slice(KVW + g * AD, KVW + (g + 1) * AD)
                qg = _lanes_of_group(q, g)
                dog = _lanes_of_group(do, g)
                pp, pc, ps = _attn_probs(qg, kp[ks], kc[ks], sink_ref, g, i > 0)
                dpp = _dot(kvp_ref[vs, :], dog, "tn")
                dpc = _dot(kv_ref[vs, :], dog, "tn")
                delta = jnp.sum(pp * dpp + pc * dpc, axis=0, keepdims=True)
                dsp = (pp * (dpp - delta)).astype(bf16)
                dsc = (pc * (dpc - delta)).astype(bf16)
                accs_ref[g:g + 1, :] -= ps * delta
                dqg = (_dot(kp[ks], dsp) + _dot(kc[ks], dsc)) * (AD ** -0.5)
                for r in range(REP):
                    h = g * REP + r
                    dqs_ref[h * AD:(h + 1) * AD, :] = dqg[:, r * WIN:(r + 1) * WIN]
                dkp_ref[ks, :] = _dot(qg, dsp, "nt")
                dkc_ref[ks, :] = _dot(qg, dsc, "nt")
                dvp_ref[ks, :] = _dot(dog, pp.astype(bf16), "nt")
                dvc_ref[ks, :] = _dot(dog, pc.astype(bf16), "nt")
            dq_ref[...] = _rope_t(dqs_ref[...], cos_ref[...], sin_ref[...]).astype(bf16)

        dkv_ref[0:KVW, :] = _rope_t(ck_ref[...] + dkp_ref[...], cosp_ref[...], sinp_ref[...]).astype(bf16)
        dkv_ref[KVW:2 * KVW, :] = (cv_ref[...] + dvp_ref[...]).astype(bf16)

        @pl.when(i < nb)
        def _():
            ck_ref[...] = dkc_ref[...]
            cv_ref[...] = dvc_ref[...]

        @pl.when(i == nb)
        def _():
            lane = lax.broadcasted_iota(jnp.int32, (1, LANES), 1)
            row = jnp.zeros((1, LANES), f32)
            for h in range(AH):
                part = accs_ref[h // REP:h // REP + 1, (h % REP) * WIN:(h % REP + 1) * WIN]
                row = row + jnp.where(lane == h, jnp.sum(part, axis=1, keepdims=True), 0.0)
            ds_ref[...] = jnp.zeros_like(ds_ref)
            ds_ref[0:1, :] = row

        cm.wait(copies, i == nb)

    tab = pl.BlockSpec((AD, WIN), cur)
    tabp = pl.BlockSpec((AD, WIN), prev)
    kvs = lambda: pltpu.VMEM((KVW, WIN), f32)
    res = pl.pallas_call(
        body, name="attn_bwd", grid=(nb + 1,),
        in_specs=[pl.BlockSpec((D, WIN), cur), pl.BlockSpec((2 * KVW, WIN), cur), pl.BlockSpec((2 * KVW, WIN), prev),
                  tab, tab, tabp, tabp, pl.BlockSpec((1, LANES), lambda i: (0, 0)),
                  pl.BlockSpec((D, WIN), cur)] + cm.in_specs(),
        out_specs=[pl.BlockSpec((D, WIN), cur), pl.BlockSpec((2 * KVW, WIN), prev),
                   pl.BlockSpec((8, LANES), lambda i: (0, 0))] + cm.out_specs(),
        out_shape=[jax.ShapeDtypeStruct((D, S), bf16), jax.ShapeDtypeStruct((2 * KVW, S), bf16),
                   jax.ShapeDtypeStruct((8, LANES), f32)] + cm.out_shape(),
        scratch_shapes=[kvs(), kvs(), pltpu.VMEM((D, WIN), f32), kvs(), kvs(), kvs(), kvs(),
                        pltpu.VMEM((8, REP * WIN), f32)] + cm.scratch(),
        compiler_params=_cp("arbitrary", side_effects=bool(cm.n)),
    )(qt, kvt, kvt, cos, sin, cos, sin, sinks, daot, *cm.bufs)
    return res[0], res[1], res[2], res[3:]


ADAM_C1 = 1.0 / (1.0 - ADAM_B1 ** ADAM_STEP)
ADAM_C2 = 1.0 / (1.0 - ADAM_B2 ** ADAM_STEP)


def _adam_update(g, w, m, v):
    nm = ADAM_B1 * m + (1.0 - ADAM_B1) * g
    nv = ADAM_B2 * v + (1.0 - ADAM_B2) * (g * g)
    return -ADAM_LR * ((nm * ADAM_C1) / (jnp.sqrt(nv * ADAM_C2) + ADAM_EPS) + ADAM_WD * w), nm, nv


def adamw(parts, w, m, v, tr, name):
    n, R, C = parts.shape

    def body(p_ref, w_ref, m_ref, v_ref, g_ref, d_ref, nm_ref, nv_ref):
        def grp(g0, _):
            r0 = pl.multiple_of(g0 * RG, RG)
            rows = pl.ds(r0, RG)
            g = p_ref[0, rows, :].astype(f32)
            for k in range(1, n):
                g = g + p_ref[k, rows, :].astype(f32)
            d, nm, nv = _adam_update(g, w_ref[rows, :], m_ref[rows, :], v_ref[rows, :])
            g_ref[rows, :] = g
            d_ref[rows, :] = d
            nm_ref[rows, :] = nm
            nv_ref[rows, :] = nv
            return 0

        lax.fori_loop(0, tr // RG, grp, 0)

    row = pl.BlockSpec((tr, C), lambda i: (i, 0))
    o = jax.ShapeDtypeStruct((R, C), f32)
    return pl.pallas_call(
        body, name=name, grid=(R // tr,),
        in_specs=[pl.BlockSpec((n, tr, C), lambda i: (0, i, 0)), row, row, row],
        out_specs=[row, row, row, row], out_shape=[o, o, o, o],
        compiler_params=_cp("parallel"),
    )(parts, w, m, v)


SMALL_ROW = (("norm_mix_post_w", D), ("norm_ffn_pre_w", D), ("norm_ffn_post_w", D), ("ssd_norm_w", DI),
             ("ssd_conv_b", CONVD), ("ffn_conv_b", 2 * FF), ("ssd_dt_bias", NH), ("ssd_a_log", NH), ("ssd_d", NH),
             ("attn_sinks", AH), ("loss", 1))
CONV_BLOCK = 1152
SSD_CONV_COLS = CONVD // N_DEV
FFN_CONV_COLS = 2 * FF // N_DEV


def _row_offsets():
    off, o = {}, 0
    for name, n in SMALL_ROW:
        off[name] = (o, n)
        o += -(-n // LANES) * LANES
    return off, o


def adamw_small(recv_row, recv_pre, recv_conv, params):
    off, _ = _row_offsets()
    names = list(params)
    n = len(names)

    def total(ref, rows, lo, width):
        g = ref[0, rows, lo:lo + width]
        for d in range(1, N_DEV):
            g = g + ref[d, rows, lo:lo + width]
        return g

    def grad_of(name, row_ref, pre_ref, conv_ref):
        if name == "norm_mix_pre_w":
            return total(pre_ref, slice(0, 1), 0, D)
        if name == "ssd_conv_w":
            return total(conv_ref, slice(0, SSD_K), 0, SSD_CONV_COLS)
        if name == "ffn_conv_w":
            return total(conv_ref, slice(0, FFN_K), 3 * LANES, FFN_CONV_COLS)
        o, width = off[name]
        return total(row_ref, slice(0, 1), o, width)

    def body(row_ref, pre_ref, conv_ref, *refs):
        ins, outs = refs[:3 * n], refs[3 * n:]
        for k, name in enumerate(names):
            w_ref, m_ref, v_ref = ins[3 * k:3 * k + 3]
            g_ref, d_ref, nm_ref, nv_ref = outs[4 * k:4 * k + 4]
            g = grad_of(name, row_ref, pre_ref, conv_ref)
            d, nm, nv = _adam_update(g, w_ref[...], m_ref[...], v_ref[...])
            g_ref[...] = g
            d_ref[...] = d
            nm_ref[...] = nm
            nv_ref[...] = nv
        outs[4 * n][...] = total(row_ref, slice(0, 1), off["loss"][0], LANES)

    flat = [t for name in names for t in params[name]]
    out_shape = [jax.ShapeDtypeStruct(params[name][0].shape, f32) for name in names for _ in range(4)]
    res = pl.pallas_call(
        body, name="adamw_small",
        out_shape=out_shape + [jax.ShapeDtypeStruct((1, LANES), f32)],
        compiler_params=pltpu.CompilerParams(vmem_limit_bytes=VMEM_LIMIT),
    )(recv_row, recv_pre, recv_conv, *flat)
    return {name: res[4 * k:4 * k + 4] for k, name in enumerate(names)}, res[4 * n]


def _pad_rows8(w):
    return jnp.pad(w, ((0, 8 - w.shape[0]), (0, 0)))


def _pad_lanes(v):
    return jnp.pad(v.reshape(1, -1), ((0, 0), (0, LANES - v.size)))


WEIGHTS = ('norm_mix_pre_w', 'w_in', 'ssd_conv_w', 'ssd_conv_b', 'ssd_dt_bias', 'ssd_a_log', 'ssd_d', 'ssd_norm_w',
           'ssd_w_out', 'attn_sinks', 'attn_w_out', 'w_mix_out', 'norm_mix_post_w', 'norm_ffn_pre_w', 'ffn_w_up',
           'ffn_conv_w', 'ffn_conv_b', 'ffn_w_down', 'norm_ffn_post_w')
W_IN_ROWS = IN_DIM // N_DEV
W_IN_PAD = 1104
W_IN_SPLIT = 832
TS = 512


def kernel(x, positions, norm_mix_pre_w, w_in, ssd_conv_w, ssd_conv_b, ssd_dt_bias, ssd_a_log, ssd_d, ssd_norm_w, ssd_w_out, attn_sinks, attn_w_out, w_mix_out, norm_mix_post_w, norm_ffn_pre_w, ffn_w_up, ffn_conv_w, ffn_conv_b, ffn_w_down, norm_ffn_post_w, loss_target, m_norm_mix_pre_w, m_w_in, m_ssd_conv_w, m_ssd_conv_b, m_ssd_dt_bias, m_ssd_a_log, m_ssd_d, m_ssd_norm_w, m_ssd_w_out, m_attn_sinks, m_attn_w_out, m_w_mix_out, m_norm_mix_post_w, m_norm_ffn_pre_w, m_ffn_w_up, m_ffn_conv_w, m_ffn_conv_b, m_ffn_w_down, m_norm_ffn_post_w, v_norm_mix_pre_w, v_w_in, v_ssd_conv_w, v_ssd_conv_b, v_ssd_dt_bias, v_ssd_a_log, v_ssd_d, v_ssd_norm_w, v_ssd_w_out, v_attn_sinks, v_attn_w_out, v_w_mix_out, v_norm_mix_post_w, v_norm_ffn_pre_w, v_ffn_w_up, v_ffn_conv_w, v_ffn_conv_b, v_ffn_w_down, v_norm_ffn_post_w):
    a = locals()
    r2 = lambda t: t.reshape(t.shape[-2], t.shape[-1])
    w = {n: r2(a[n]) for n in WEIGHTS}
    m = {n: r2(a["m_" + n]) for n in WEIGHTS}
    v = {n: r2(a["v_" + n]) for n in WEIGHTS}
    xs, target = x[0], loss_target[0]
    S = xs.shape[0]
    ts = TS

    w_in_blk = jnp.pad(w["w_in"].T.astype(bf16), ((0, W_IN_PAD - W_IN_ROWS), (0, 0)))
    conv_blk = jnp.concatenate([_pad_rows8(w["ssd_conv_w"]), _pad_rows8(w["ffn_conv_w"]),
                                jnp.zeros((8, CONV_BLOCK - SSD_CONV_COLS - FFN_CONV_COLS), f32)], axis=1)
    u, cos, sin, (g_in, g_conv) = prenorm_fwd(xs, w["norm_mix_pre_w"], positions, ts, [w_in_blk, conv_blk])
    wt = g_in[:, :W_IN_ROWS].reshape(IN_DIM, D)
    w_main_t = jnp.concatenate([wt[IN_OFF[0]:IN_OFF[1]], wt[IN_OFF[6]:IN_OFF[8]], wt[IN_OFF[1]:IN_OFF[2]]], axis=0)
    w_q_t = wt[IN_OFF[3]:IN_OFF[4]]
    w_kv_t = wt[IN_OFF[4]:IN_OFF[6]]
    w_dt_t = jnp.pad(wt[IN_OFF[2]:IN_OFF[3]], ((0, LANES - NH), (0, 0)))
    conv_w8 = g_conv[:, :, 0:SSD_CONV_COLS].transpose(1, 0, 2).reshape(8, CONVD)
    fconv_w8 = g_conv[:, :, SSD_CONV_COLS:SSD_CONV_COLS + FFN_CONV_COLS].transpose(1, 0, 2).reshape(8, 2 * FF)
    bias = _pad_lanes(w["ssd_dt_bias"])
    alog = _pad_lanes(w["ssd_a_log"])
    dx_row = jnp.repeat(w["ssd_d"].reshape(-1), HD).reshape(1, DI)
    sinks = _pad_lanes(w["attn_sinks"])

    later = [w["ssd_w_out"].astype(bf16), w["attn_w_out"].astype(bf16), w["w_mix_out"].astype(bf16)]
    proj, (g_so, g_ao, g_mix) = mm(u, w_main_t, "nt", bf16, "mm_proj", comm=(later, (False,) * 3))
    w_ssd_out, w_attn_out, w_mix = g_so.reshape(DI, D), g_ao.reshape(D, D), g_mix.reshape(D, D)
    qt = mm(w_q_t, u, "nt", bf16, "mm_q")
    kvt = mm(w_kv_t, u, "nt", bf16, "mm_kv")
    dtr = mm(u, w_dt_t, "nt", f32, "mm_dt")
    xbc, conv_c = ssdconv_fwd(proj, conv_w8, w["ssd_conv_b"], ts)
    y, hprev, ssd_saved, (g_up, g_down) = ssd_fwd(xbc, dtr, bias, alog, dx_row, comm=(
        [w["ffn_w_up"].T.astype(bf16), w["ffn_w_down"].astype(bf16)], (False, False)))
    w_up_t = g_up.reshape(2 * FF, D)
    w_down = g_down.reshape(FF, D)
    yn = gnorm_fwd(y, proj, w["ssd_norm_w"], ts)
    ys = mm(yn, w_ssd_out, "nn", bf16, "mm_ssd_out")
    aot = attn_fwd(qt, kvt, cos, sin, sinks)
    ya = mm(aot, w_attn_out, "tn", bf16, "mm_attn_out")
    merged = merge_fwd(proj, ys, ya, ts)
    mo = mm(merged, w_mix, "nn", bf16, "mm_mix")
    x1, h = post_fwd(xs, mo, w["norm_mix_post_w"], w["norm_ffn_pre_w"], ts)
    up = mm(h, w_up_t, "nt", bf16, "mm_up")
    act, gate_c, val_c = ffnact_fwd(up, fconv_w8, w["ffn_conv_b"], ts)
    ff = mm(act, w_down, "nn", bf16, "mm_down")
    loss_blk, dout, dff, g_post2 = loss_head(x1, ff, target, w["norm_ffn_post_w"], ts)

    dact = mm(dff, w_down, "nt", bf16, "mm_dact")
    gw_down = mm(act, dff, "tn", bf16, "mm_g_down")
    dgate, dval = ffnact_bwd(dact, gate_c, val_c, ts)
    dup_pre, g_fconv_a = dwconv_bwd(dgate, up, 0, fconv_w8, 0, FFN_K, FF, ts, "ffnconv_bwd_gate", out_cols=2 * FF)
    dup_pre, g_fconv_b = dwconv_bwd(dval, up, 1, fconv_w8, 1, FFN_K, FF, ts, "ffnconv_bwd_val", into=dup_pre, ocb=1,
                                    out_cols=2 * FF)
    g_fconv = jnp.concatenate([g_fconv_a, g_fconv_b], axis=1)
    dh, (r_down,) = mm(dup_pre, w_up_t, "nn", bf16, "mm_dh", comm=([gw_down.reshape(N_DEV, FF // N_DEV, D)], (True,)))
    gw_up_t = mm(dup_pre, h, "tn", bf16, "mm_g_up")
    dx1, dmo, g_norms = post_bwd(dout, dh, x1, mo, w["norm_mix_post_w"], w["norm_ffn_pre_w"], ts)
    dmerged = mm(dmo, w_mix, "nt", bf16, "mm_dmerged")
    gw_mix = mm(merged, dmo, "tn", bf16, "mm_g_mix")
    dys, dya, dproj = merge_bwd(dmerged, proj, ys, ya, ts)
    daot = mm(w_attn_out, dya, "nt", bf16, "mm_dao")
    gw_attn_out = mm(aot, dya, "nn", bf16, "mm_g_attn_out")
    dqt, dkvt, g_sinks, (r_up,) = attn_bwd(qt, kvt, cos, sin, sinks, daot,
                                           comm=([gw_up_t.reshape(N_DEV, 2 * FF // N_DEV, D)], (True,)))
    du_b = mm(dkvt, w_kv_t, "tn", bf16, "mm_du_kv")
    du_d = mm(dqt, w_q_t, "tn", bf16, "mm_du_q")
    dyn = mm(dys, w_ssd_out, "nt", bf16, "mm_dyn")
    gw_ssd_out = mm(yn, dys, "tn", bf16, "mm_g_ssd_out")
    dy, dproj, g_gnorm = gnorm_bwd(dyn, y, proj, w["ssd_norm_w"], dproj, ts)
    sends = [gw_ssd_out.reshape(N_DEV, DI // N_DEV, D), gw_attn_out.reshape(N_DEV, D // N_DEV, D),
             gw_mix.reshape(N_DEV, D // N_DEV, D)]
    dxbc, ddtr, g_ssd, (r_so, r_ao, r_mix) = ssd_bwd(xbc, dtr, dy, hprev, ssd_saved, bias, alog, dx_row,
                                                     comm=(sends, (True,) * 3))
    dproj, g_conv_w = dwconv_bwd(dxbc, proj, C_XBC // 1024, conv_w8, 0, SSD_K, 1024, ts, "ssdconv_bwd", act_c=conv_c,
                                 into=dproj, ocb=C_XBC // 1024, out_cols=PM)
    ddtr_b = ddtr.astype(bf16)
    du_c = mm(ddtr_b, w_dt_t, "nn", bf16, "mm_du_dt")
    g_main_t = mm(dproj, u, "tn", bf16, "mm_g_in")
    g_q_t = mm(dqt, u, "nn", bf16, "mm_g_q")
    g_kv_t = mm(dkvt, u, "nn", bf16, "mm_g_kv")
    g_dt_t = mm(ddtr_b, u, "tn", bf16, "mm_g_dt")
    g_wt = jnp.concatenate([g_main_t[C_Z:C_GS], g_main_t[C_XBC:PM], g_dt_t[:NH], g_q_t, g_kv_t, g_main_t[C_GS:C_XBC]],
                           axis=0)
    send_in = jnp.pad(g_wt.reshape(N_DEV, W_IN_ROWS, D), ((0, 0), (0, W_IN_PAD - W_IN_ROWS), (0, 0)))
    pieces = {"norm_mix_post_w": g_norms[1:2], "norm_ffn_pre_w": g_norms[0:1], "norm_ffn_post_w": g_post2[0:1],
              "ssd_norm_w": g_gnorm[0:1], "ssd_conv_b": g_conv_w[7:8], "ffn_conv_b": g_fconv[7:8],
              "ssd_dt_bias": g_ssd[0:1], "ssd_a_log": g_ssd[1:2], "ssd_d": g_ssd[2:3], "attn_sinks": g_sinks[0:1],
              "loss": loss_blk[0:1]}
    row = jnp.concatenate([jnp.pad(pieces[n][:, :min(k, pieces[n].shape[1])],
                                   ((0, 0), (0, -(-k // LANES) * LANES - min(k, pieces[n].shape[1]))))
                           for n, k in SMALL_ROW], axis=1)
    send_row = jnp.pad(row, ((0, 7), (0, 0)))
    send_conv = jnp.concatenate(
        [g_conv_w.reshape(8, N_DEV, SSD_CONV_COLS).transpose(1, 0, 2),
         g_fconv.reshape(8, N_DEV, FFN_CONV_COLS).transpose(1, 0, 2),
         jnp.zeros((N_DEV, 8, CONV_BLOCK - SSD_CONV_COLS - FFN_CONV_COLS), f32)], axis=2)
    du_a, (r_in_a, recv_row, recv_conv) = mm(dproj, w_main_t, "nn", bf16, "mm_du", comm=(
        [send_in[:, :W_IN_SPLIT], send_row, send_conv], (True, False, True)))
    grad_x, g_pre, (r_in_b,) = prenorm_bwd(xs, w["norm_mix_pre_w"], (du_a, du_b, du_c, du_d), dx1, ts,
                                           comm=([send_in[:, W_IN_SPLIT:]], (True,)))
    (recv_pre,) = exchange([g_pre], (False,), "gather_last")

    r_in = jnp.concatenate([r_in_a, r_in_b], axis=1)
    tpad = lambda t: jnp.pad(t.T, ((0, W_IN_PAD - W_IN_ROWS), (0, 0)))
    o_in = [t[:W_IN_ROWS].T for t in adamw(r_in, tpad(w["w_in"]), tpad(m["w_in"]), tpad(v["w_in"]), 368, "adamw_w_in")]
    o_up = [t.T for t in adamw(r_up, w["ffn_w_up"].T, m["ffn_w_up"].T, v["ffn_w_up"].T, 352, "adamw_w_up")]
    big = {"w_in": o_in, "ffn_w_up": o_up,
           "ssd_w_out": adamw(r_so, w["ssd_w_out"], m["ssd_w_out"], v["ssd_w_out"], 256, "adamw_ssd_out"),
           "attn_w_out": adamw(r_ao, w["attn_w_out"], m["attn_w_out"], v["attn_w_out"], 128, "adamw_attn_out"),
           "w_mix_out": adamw(r_mix, w["w_mix_out"], m["w_mix_out"], v["w_mix_out"], 128, "adamw_mix"),
           "ffn_w_down": adamw(r_down, w["ffn_w_down"], m["ffn_w_down"], v["ffn_w_down"], 352, "adamw_down")}
    small_names = [n for n in WEIGHTS if n not in big]
    small, loss_row = adamw_small(recv_row, recv_pre, recv_conv, {n: (w[n], m[n], v[n]) for n in small_names})

    outs = [loss_row[0, 0], grad_x[None]]
    for k in range(4):
        for n in WEIGHTS:
            outs.append((big[n][k] if n in big else small[n][k]).reshape(a[n].shape))
    return tuple(outs)
```

```python
import jax
import jax.numpy as jnp
import numpy as np
from jax import lax
from jax.experimental import pallas as pl
from jax.experimental.pallas import tpu as pltpu

f32 = jnp.float32
bf16 = jnp.bfloat16

N_DEV = 8
D = 1024
DI = 2048
NH = 32
HD = 64
NG = 4
GW = DI // NG
NS = 128
CH = 128
CONVD = DI + 2 * NG * NS
SSD_K = 4
AH = 16
AD = 64
KVH = 4
REP = AH // KVH
KVW = KVH * AD
WIN = 128
FF = 2816
FFN_K = 3
EPS = 1e-6
ROPE_THETA = 10000.0
LANES = 128
RG = 16
CW = 256

C_Z, C_GS, C_GA, C_XBC, PM = 0, 2048, 3072, 4096, 7168
IN_SIZES = (DI, CONVD, NH, D, KVW, KVW, D, D)
IN_OFF = tuple(int(v) for v in np.cumsum((0,) + IN_SIZES))
IN_DIM = IN_OFF[-1]

ADAM_LR, ADAM_B1, ADAM_B2, ADAM_EPS, ADAM_WD, ADAM_STEP = 0.001, 0.9, 0.999, 1e-08, 0.01, 10

VMEM_LIMIT = 56 * 1024 * 1024


def _cp(*sem, side_effects=False):
    return pltpu.CompilerParams(dimension_semantics=sem, vmem_limit_bytes=VMEM_LIMIT, has_side_effects=side_effects)


def _dot(a, b, mode="nn"):
    dims = {"nn": (((1,), (0,)), ((), ())), "nt": (((1,), (1,)), ((), ())), "tn": (((0,), (0,)), ((), ()))}[mode]
    return lax.dot_general(a, b, dims, preferred_element_type=f32)


def _split3(v):
    hi = v.astype(bf16)
    r = v - hi.astype(f32)
    mid = r.astype(bf16)
    lo = (r - mid.astype(f32)).astype(bf16)
    return hi, mid, lo


def _dot3_left(m01, v):
    hi, mid, lo = _split3(v)
    return _dot(m01, hi) + _dot(m01, mid) + _dot(m01, lo)


def _dot3_right(v, m01):
    hi, mid, lo = _split3(v)
    return _dot(hi, m01) + _dot(mid, m01) + _dot(lo, m01)


def _dot2_right(v, m01):
    hi = v.astype(bf16)
    lo = (v - hi.astype(f32)).astype(bf16)
    return _dot(hi, m01) + _dot(lo, m01)


def _sigmoid(x):
    return 1.0 / (1.0 + jnp.exp(-x))


def _sigmoid_fast(x):
    return pl.reciprocal(1.0 + jnp.exp(-x), approx=True)


def _peer(k, x, y, c):
    return ((1 - x) if k & 4 else x, (1 - y) if k & 2 else y, (1 - c) if k & 1 else c)


def _xchg_copies(buf_refs, out_refs, send_sems, recv_sems, local_sems, personalised):
    x, y, c = lax.axis_index("x"), lax.axis_index("y"), lax.axis_index("c")
    me = 4 * x + 2 * y + c
    local, remote = [], []
    for b, (buf, out, pers) in enumerate(zip(buf_refs, out_refs, personalised)):
        local.append(pltpu.make_async_copy(buf.at[me] if pers else buf, out.at[me], local_sems.at[b]))
        for k in range(1, N_DEV):
            px, py, pc = _peer(k, x, y, c)
            s = b * (N_DEV - 1) + k - 1
            remote.append(pltpu.make_async_remote_copy(
                src_ref=buf.at[4 * px + 2 * py + pc] if pers else buf, dst_ref=out.at[me],
                send_sem=send_sems.at[s], recv_sem=recv_sems.at[s],
                device_id=(px, py, pc), device_id_type=pl.DeviceIdType.MESH))
    return local, remote


class _Comm:
    def __init__(self, comm):
        self.bufs, self.pers = comm if comm else ((), ())
        self.n = len(self.bufs)

    def in_specs(self):
        return [pl.BlockSpec(memory_space=pl.ANY)] * self.n

    out_specs = in_specs

    def out_shape(self):
        return [jax.ShapeDtypeStruct((N_DEV,) + tuple(b.shape[1:] if p else b.shape), b.dtype)
                for b, p in zip(self.bufs, self.pers)]

    def scratch(self):
        n = self.n
        return [pltpu.SemaphoreType.DMA((n * (N_DEV - 1),)), pltpu.SemaphoreType.DMA((n * (N_DEV - 1),)),
                pltpu.SemaphoreType.DMA((n,))] if n else []

    def split(self, refs, n_in, n_out):
        n = self.n
        ins, outs = refs[:n_in], refs[n_in + n:n_in + n + n_out]
        rest = refs[n_in + n + n_out + n:]
        if not n:
            return ins, outs, rest, None
        copies = _xchg_copies(refs[n_in:n_in + n], refs[n_in + n + n_out:n_in + n + n_out + n], *rest[-3:], self.pers)
        return ins, outs, rest[:-3], copies

    def start(self, copies, first):
        if copies:
            @pl.when(first)
            def _():
                for cp in copies[0] + copies[1]:
                    cp.start()

    def wait(self, copies, last):
        if copies:
            @pl.when(last)
            def _():
                for cp in copies[1]:
                    cp.wait_recv()
                for cp in copies[1]:
                    cp.wait_send()
                for cp in copies[0]:
                    cp.wait()


def exchange(bufs, personalised, name):
    cm = _Comm((bufs, personalised))

    def body(*refs):
        _, _, _, copies = cm.split(refs, 0, 0)
        cm.start(copies, True)
        cm.wait(copies, True)

    return pl.pallas_call(
        body, name=name, in_specs=cm.in_specs(), out_specs=cm.out_specs(), out_shape=cm.out_shape(),
        scratch_shapes=cm.scratch(), compiler_params=pltpu.CompilerParams(has_side_effects=True),
    )(*bufs)


class _TwoLevelGather:
    def __init__(self, bufs):
        self.bufs = list(bufs)
        self.n = len(self.bufs)

    def in_specs(self):
        return [pl.BlockSpec(memory_space=pl.ANY)] * self.n

    out_specs = in_specs

    def out_shape(self):
        return [jax.ShapeDtypeStruct((N_DEV,) + tuple(b.shape), b.dtype) for b in self.bufs]

    def scratch(self):
        per = N_DEV - 1
        return [pltpu.SemaphoreType.DMA((self.n * per,)), pltpu.SemaphoreType.DMA((self.n * per,)),
                pltpu.SemaphoreType.DMA((self.n,))]

    def bind(self, ins, outs, send_sems, recv_sems, local_sems):
        n, per = self.n, N_DEV - 1
        x, y, c = lax.axis_index("x"), lax.axis_index("y"), lax.axis_index("c")
        me, sibling = (x, y, c), (x, y, 1 - c)
        chips = [(1 - x, y), (x, 1 - y), (1 - x, 1 - y)]

        def copy(b, k, block, to, src=None):
            dst = outs[b].at[4 * block[0] + 2 * block[1] + block[2]]
            return pltpu.make_async_remote_copy(
                src_ref=dst if src is None else src, dst_ref=dst,
                send_sem=send_sems.at[b * per + k], recv_sem=recv_sems.at[b * per + k],
                device_id=to, device_id_type=pl.DeviceIdType.MESH)

        mine = [pltpu.make_async_copy(ins[b], outs[b].at[4 * x + 2 * y + c], local_sems.at[b]) for b in range(n)]
        first = []
        for b in range(n):
            first.append(copy(b, 0, me, sibling, src=ins[b]))
            first += [copy(b, 1 + j, me, (*chip, c), src=ins[b]) for j, chip in enumerate(chips)]

        def start():
            for cp in mine + first:
                cp.start()

        def finish():
            passed = []
            for j, chip in enumerate(chips):
                for b in range(n):
                    copy(b, 1 + j, (*chip, c), me).wait_recv()
                    passed.append(copy(b, 4 + j, (*chip, c), sibling))
                    passed[-1].start()
            for b in range(n):
                copy(b, 0, sibling, me).wait_recv()
                for j, chip in enumerate(chips):
                    copy(b, 4 + j, (*chip, 1 - c), me).wait_recv()
            for cp in first + passed:
                cp.wait_send()
            for cp in mine:
                cp.wait()

        return start, finish


MM_TILES = (3584, 2176, 2048, 1792, 1408, 1024, 512, 256, 128)
MM_VMEM_BUDGET = 40 * 1024 * 1024


def _mm_tiles(M, N, K, out_bytes):
    cm = [t for t in MM_TILES if M % t == 0]
    cn = [t for t in MM_TILES if N % t == 0]
    ck = [t for t in MM_TILES if K % t == 0]
    best = None
    for bm in cm[:2]:
        for bn in cn:
            for bk in ck:
                need = 4 * (bm * bk + bk * bn) + bm * bn * (4 + 2 * out_bytes)
                if need <= MM_VMEM_BUDGET:
                    score = (bm * bn * bk, bk)
                    if best is None or score > best[0]:
                        best = (score, (bm, bn, bk))
    return best[1]


def mm(a, b, mode, out_dtype, name, comm=None):
    if mode == "nn":
        (M, K), (_, N) = a.shape, b.shape
    elif mode == "nt":
        (M, K), (N, _) = a.shape, b.shape
    else:
        (K, M), (_, N) = a.shape, b.shape
    bm, bn, bk = _mm_tiles(M, N, K, jnp.dtype(out_dtype).itemsize)
    gm, gn, nk = M // bm, N // bn, K // bk
    cm = _Comm(comm)

    def body(*refs):
        (a_ref, b_ref), (o_ref,), scr, copies = cm.split(refs, 2, 1)
        i, j, k = pl.program_id(0), pl.program_id(1), pl.program_id(2)
        cm.start(copies, jnp.logical_and(jnp.logical_and(i == 0, j == 0), k == 0))
        p = _dot(a_ref[...], b_ref[...], mode)
        if nk == 1:
            o_ref[...] = p.astype(o_ref.dtype)
        else:
            acc_ref = scr[0]

            @pl.when(k == 0)
            def _():
                acc_ref[...] = p

            @pl.when(k > 0)
            def _():
                acc_ref[...] += p

            @pl.when(k == nk - 1)
            def _():
                o_ref[...] = acc_ref[...].astype(o_ref.dtype)

        cm.wait(copies, jnp.logical_and(jnp.logical_and(i == gm - 1, j == gn - 1), k == nk - 1))

    if mode == "nn":
        a_spec = pl.BlockSpec((bm, bk), lambda i, j, k: (i, k))
        b_spec = pl.BlockSpec((bk, bn), lambda i, j, k: (k, j))
    elif mode == "nt":
        a_spec = pl.BlockSpec((bm, bk), lambda i, j, k: (i, k))
        b_spec = pl.BlockSpec((bn, bk), lambda i, j, k: (j, k))
    else:
        a_spec = pl.BlockSpec((bk, bm), lambda i, j, k: (k, i))
        b_spec = pl.BlockSpec((bk, bn), lambda i, j, k: (k, j))
    sem = ("arbitrary",) * 3 if cm.n else ("parallel", "parallel", "arbitrary")
    res = pl.pallas_call(
        body, name=name, grid=(gm, gn, nk),
        in_specs=[a_spec, b_spec] + cm.in_specs(),
        out_specs=[pl.BlockSpec((bm, bn), lambda i, j, k: (i, j))] + cm.out_specs(),
        out_shape=[jax.ShapeDtypeStruct((M, N), out_dtype)] + cm.out_shape(),
        scratch_shapes=([pltpu.VMEM((bm, bn), f32)] if nk > 1 else []) + cm.scratch(),
        compiler_params=_cp(*sem, side_effects=bool(cm.n)),
    )(a, b, *cm.bufs)
    return (res[0], res[1:]) if cm.n else res[0]


def _groups(ts, fn, carry=None, reverse=False, unroll=8, rg=RG):
    n = ts // rg
    if n == 1:
        return fn(0, carry)
    unroll = min(unroll, n)
    span = rg * unroll

    def body(g, c):
        r0 = pl.multiple_of((n // unroll - 1 - g if reverse else g) * span, span)
        for u in (range(unroll - 1, -1, -1) if reverse else range(unroll)):
            c = fn(pl.multiple_of(r0 + u * rg, rg), c)
        return c

    return lax.fori_loop(0, n // unroll, body, carry)


def _rms(x):
    return lax.rsqrt(jnp.mean(x * x, axis=-1, keepdims=True) + EPS)


def _rms_bwd(x, r, dn):
    n = x * r
    return r * (dn - n * jnp.mean(dn * n, axis=-1, keepdims=True))


NRG = 256


def _fold(x):
    return jnp.sum(x.reshape(x.shape[0] // 8, 8, x.shape[1]), axis=0)


def _flush(acc_ref, out_ref, row):
    out_ref[row:row + 1, :] = jnp.sum(acc_ref[...], axis=0, keepdims=True)


def prenorm_fwd(x, w, pos_row, ts, gather):
    S = x.shape[0]
    nt = S // ts
    tg = _TwoLevelGather(gather)
    n = tg.n
    half = AD // 2
    inv = ROPE_THETA ** (-jnp.arange(half, dtype=f32) * 2.0 / AD)
    inv_col = jnp.tile(inv, 2)[:, None]

    def body(x_ref, w_ref, p_ref, inv_ref, *refs):
        u_ref, cos_ref, sin_ref = refs[n:n + 3]
        start, finish = tg.bind(refs[:n], refs[n + 3:2 * n + 3], *refs[2 * n + 3:])
        i = pl.program_id(0)
        pl.when(i == 0)(start)
        wv = w_ref[...]

        def grp(r0, _):
            xv = x_ref[pl.ds(r0, NRG), :]
            u_ref[pl.ds(r0, NRG), :] = (xv * _rms(xv) * wv).astype(bf16)

        _groups(ts, grp, rg=NRG)
        ang = inv_ref[...] * p_ref[...].astype(f32)
        row = lax.broadcasted_iota(jnp.int32, ang.shape, 0)
        cos_ref[...] = jnp.cos(ang)
        sin_ref[...] = jnp.where(row < half, -1.0, 1.0) * jnp.sin(ang)
        pl.when(i == nt - 1)(finish)

    tab = pl.BlockSpec((AD, ts), lambda i: (0, i))
    res = pl.pallas_call(
        body, name="prenorm_fwd", grid=(nt,),
        in_specs=[pl.BlockSpec((ts, D), lambda i: (i, 0)), pl.BlockSpec((1, D), lambda i: (0, 0)),
                  pl.BlockSpec((1, ts), lambda i: (0, i)), pl.BlockSpec((AD, 1), lambda i: (0, 0))] + tg.in_specs(),
        out_specs=[pl.BlockSpec((ts, D), lambda i: (i, 0)), tab, tab] + tg.out_specs(),
        out_shape=[jax.ShapeDtypeStruct((S, D), bf16), jax.ShapeDtypeStruct((AD, S), f32),
                   jax.ShapeDtypeStruct((AD, S), f32)] + tg.out_shape(),
        scratch_shapes=tg.scratch(),
        compiler_params=_cp("arbitrary", side_effects=True),
    )(x, w, pos_row, inv_col, *tg.bufs)
    return res[0], res[1], res[2], res[3:]


def prenorm_bwd(x, w, dus, dx1, ts, comm=None):
    S = x.shape[0]
    nt = S // ts
    nd = len(dus)
    cm = _Comm(comm)

    def body(*refs):
        ins, (gx_ref, gw_ref), (acc_ref,), copies = cm.split(refs, nd + 3, 2)
        x_ref, w_ref = ins[:2]
        du_refs, dx1_ref = ins[2:2 + nd], ins[2 + nd]
        i = pl.program_id(0)
        cm.start(copies, i == 0)
        wv = w_ref[...]

        @pl.when(i == 0)
        def _():
            acc_ref[...] = jnp.zeros_like(acc_ref)
            gw_ref[...] = jnp.zeros_like(gw_ref)

        def grp(r0, _):
            rows = pl.ds(r0, NRG)
            xv = x_ref[rows, :]
            r = _rms(xv)
            du = du_refs[0][rows, :].astype(f32)
            for d_ref in du_refs[1:]:
                du = du + d_ref[rows, :].astype(f32)
            gx_ref[rows, :] = dx1_ref[rows, :] + _rms_bwd(xv, r, du * wv)
            acc_ref[...] += _fold(du * xv * r)

        _groups(ts, grp, rg=NRG)

        @pl.when(i == nt - 1)
        def _():
            _flush(acc_ref, gw_ref, 0)

        cm.wait(copies, i == nt - 1)

    row = pl.BlockSpec((ts, D), lambda i: (i, 0))
    res = pl.pallas_call(
        body, name="prenorm_bwd", grid=(nt,),
        in_specs=[row, pl.BlockSpec((1, D), lambda i: (0, 0))] + [row] * (nd + 1) + cm.in_specs(),
        out_specs=[row, pl.BlockSpec((8, D), lambda i: (0, 0))] + cm.out_specs(),
        out_shape=[jax.ShapeDtypeStruct((S, D), f32), jax.ShapeDtypeStruct((8, D), f32)] + cm.out_shape(),
        scratch_shapes=[pltpu.VMEM((8, D), f32)] + cm.scratch(),
        compiler_params=_cp("arbitrary", side_effects=bool(cm.n)),
    )(x, w, *dus, dx1, *cm.bufs)
    return res[0], res[1], res[2:]


def post_fwd(x, mo, w_post, w_pre2, ts):
    S = x.shape[0]

    def body(x_ref, mo_ref, wp_ref, w2_ref, x1_ref, h_ref):
        wp, w2 = wp_ref[...], w2_ref[...]

        def grp(r0, _):
            rows = pl.ds(r0, NRG)
            mv = mo_ref[rows, :].astype(f32)
            x1 = x_ref[rows, :] + mv * _rms(mv) * wp
            x1_ref[rows, :] = x1
            h_ref[rows, :] = (x1 * _rms(x1) * w2).astype(bf16)

        _groups(ts, grp, rg=NRG)

    row = pl.BlockSpec((ts, D), lambda i: (i, 0))
    par = pl.BlockSpec((1, D), lambda i: (0, 0))
    return pl.pallas_call(
        body, name="post_fwd", grid=(S // ts,),
        in_specs=[row, row, par, par], out_specs=[row, row],
        out_shape=[jax.ShapeDtypeStruct((S, D), f32), jax.ShapeDtypeStruct((S, D), bf16)],
        compiler_params=_cp("parallel"),
    )(x, mo, w_post, w_pre2)


def post_bwd(dout, dh, x1, mo, w_post, w_pre2, ts):
    S = x1.shape[0]
    nt = S // ts

    def body(dout_ref, dh_ref, x1_ref, mo_ref, wp_ref, w2_ref, dx1_ref, dmo_ref, gw_ref, acc2_ref, accp_ref):
        i = pl.program_id(0)
        wp, w2 = wp_ref[...], w2_ref[...]

        @pl.when(i == 0)
        def _():
            acc2_ref[...] = jnp.zeros_like(acc2_ref)
            accp_ref[...] = jnp.zeros_like(accp_ref)
            gw_ref[...] = jnp.zeros_like(gw_ref)

        def grp(r0, _):
            rows = pl.ds(r0, NRG)
            x1 = x1_ref[rows, :]
            r1 = _rms(x1)
            dh = dh_ref[rows, :].astype(f32)
            dx1 = dout_ref[rows, :] + _rms_bwd(x1, r1, dh * w2)
            dx1_ref[rows, :] = dx1
            acc2_ref[...] += _fold(dh * x1 * r1)
            mv = mo_ref[rows, :].astype(f32)
            rm = _rms(mv)
            dmo_ref[rows, :] = _rms_bwd(mv, rm, dx1 * wp).astype(bf16)
            accp_ref[...] += _fold(dx1 * mv * rm)

        _groups(ts, grp, rg=NRG)

        @pl.when(i == nt - 1)
        def _():
            _flush(acc2_ref, gw_ref, 0)
            _flush(accp_ref, gw_ref, 1)

    row = pl.BlockSpec((ts, D), lambda i: (i, 0))
    par = pl.BlockSpec((1, D), lambda i: (0, 0))
    return pl.pallas_call(
        body, name="post_bwd", grid=(nt,),
        in_specs=[row, row, row, row, par, par],
        out_specs=[row, row, pl.BlockSpec((8, D), lambda i: (0, 0))],
        out_shape=[jax.ShapeDtypeStruct((S, D), f32), jax.ShapeDtypeStruct((S, D), bf16),
                   jax.ShapeDtypeStruct((8, D), f32)],
        scratch_shapes=[pltpu.VMEM((8, D), f32), pltpu.VMEM((8, D), f32)],
        compiler_params=_cp("arbitrary"),
    )(dout, dh, x1, mo, w_post, w_pre2)


def loss_head(x1, ff, target, w, ts):
    S = x1.shape[0]
    nt = S // ts

    def body(x1_ref, ff_ref, t_ref, w_ref, loss_ref, dout_ref, dff_ref, gw_ref, accw_ref, accl_ref):
        i = pl.program_id(0)
        wv = w_ref[...]

        @pl.when(i == 0)
        def _():
            accw_ref[...] = jnp.zeros_like(accw_ref)
            accl_ref[...] = jnp.zeros_like(accl_ref)
            gw_ref[...] = jnp.zeros_like(gw_ref)

        def grp(r0, _):
            rows = pl.ds(r0, NRG)
            fv = ff_ref[rows, :].astype(f32)
            r = _rms(fv)
            n = fv * r
            e = x1_ref[rows, :] + n * wv - t_ref[rows, :]
            dout = e * (1.0 / D)
            dout_ref[rows, :] = dout
            dff_ref[rows, :] = _rms_bwd(fv, r, dout * wv).astype(bf16)
            accw_ref[...] += _fold(dout * n)
            accl_ref[...] += _fold(e * e)

        _groups(ts, grp, rg=NRG)

        @pl.when(i == nt - 1)
        def _():
            _flush(accw_ref, gw_ref, 0)
            tot = jnp.sum(jnp.sum(accl_ref[...], axis=1, keepdims=True), axis=0, keepdims=True) * (0.5 / D)
            loss_ref[...] = jnp.broadcast_to(tot, loss_ref.shape)

    row = pl.BlockSpec((ts, D), lambda i: (i, 0))
    return pl.pallas_call(
        body, name="loss_head", grid=(nt,),
        in_specs=[row, row, row, pl.BlockSpec((1, D), lambda i: (0, 0))],
        out_specs=[pl.BlockSpec((8, LANES), lambda i: (0, 0)), row, row, pl.BlockSpec((8, D), lambda i: (0, 0))],
        out_shape=[jax.ShapeDtypeStruct((8, LANES), f32), jax.ShapeDtypeStruct((S, D), f32),
                   jax.ShapeDtypeStruct((S, D), bf16), jax.ShapeDtypeStruct((8, D), f32)],
        scratch_shapes=[pltpu.VMEM((8, D), f32), pltpu.VMEM((8, D), f32)],
        compiler_params=_cp("arbitrary"),
    )(x1, ff, target, w)


def _taps(w_ref, cs, K):
    return [jnp.broadcast_to(w_ref[k:k + 1, cs], (8, CW)) for k in range(K)]


def _down(before, cur, s, sub):
    return jnp.where(sub < s, pltpu.roll(before, s, 0), pltpu.roll(cur, s, 0))


def _up(cur, after, s, sub):
    return jnp.where(sub < 8 - s, pltpu.roll(cur, 8 - s, 0), pltpu.roll(after, 8 - s, 0))


def _conv_group(p, a, b, taps, bias, K, sub):
    ya, yb = bias, bias
    for k in range(K):
        s = K - 1 - k
        xa, xb = (a, b) if s == 0 else (_down(p, a, s, sub), _down(a, b, s, sub))
        ya = ya + taps[k] * xa
        yb = yb + taps[k] * xb
    return ya, yb


def _prev8_map(ts, cb):
    return lambda i, j: (jnp.maximum(i * (ts // 8) - 1, 0), cb + j)


def ssdconv_fwd(proj, w8, b, ts):
    S = proj.shape[0]
    bw = 1024
    cb = C_XBC // bw

    def body(cur_ref, prev_ref, w_ref, b_ref, o_ref, c_ref):
        first = pl.program_id(0) == 0
        sub = lax.broadcasted_iota(jnp.int32, (8, CW), 0)
        for c0 in range(0, bw, CW):
            cs = slice(c0, c0 + CW)
            taps = _taps(w_ref, cs, SSD_K)
            bias = jnp.broadcast_to(b_ref[:, cs], (8, CW))

            def grp(r0, p, cs=cs, taps=taps, bias=bias):
                rows = pl.ds(r0, RG)
                xv = cur_ref[rows, cs].astype(f32)
                ya, yb = _conv_group(p, xv[0:8], xv[8:16], taps, bias, SSD_K, sub)
                y = jnp.concatenate([ya, yb], axis=0)
                c_ref[rows, cs] = y.astype(bf16)
                o_ref[rows, cs] = (y * _sigmoid_fast(y)).astype(bf16)
                return xv[8:16]

            _groups(ts, grp, jnp.where(first, 0.0, prev_ref[:, cs].astype(f32)))

    o = jax.ShapeDtypeStruct((S, CONVD), bf16)
    blk = pl.BlockSpec((ts, bw), lambda i, j: (i, j))
    return pl.pallas_call(
        body, name="ssdconv_fwd", grid=(S // ts, CONVD // bw),
        in_specs=[pl.BlockSpec((ts, bw), lambda i, j: (i, cb + j)),
                  pl.BlockSpec((8, bw), _prev8_map(ts, cb)),
                  pl.BlockSpec((8, bw), lambda i, j: (0, j)),
                  pl.BlockSpec((1, bw), lambda i, j: (0, j))],
        out_specs=[blk, blk], out_shape=[o, o],
        compiler_params=_cp("parallel", "parallel"),
    )(proj, proj, w8, b)


def _gelu_tanh(x):
    c = 0.7978845608028654
    t = jnp.tanh(c * (x + 0.044715 * x * x * x))
    return 0.5 * x * (1.0 + t), t


def ffnact_fwd(up, w8, b, ts):
    S = up.shape[0]

    def body(g_ref, gp_ref, v_ref, vp_ref, wg_ref, wv_ref, bg_ref, bv_ref, o_ref, gc_ref, vc_ref):
        first = pl.program_id(0) == 0
        sub = lax.broadcasted_iota(jnp.int32, (8, CW), 0)
        for c0 in range(0, FF, CW):
            cs = slice(c0, c0 + CW)
            tg, tv = _taps(wg_ref, cs, FFN_K), _taps(wv_ref, cs, FFN_K)
            bg = jnp.broadcast_to(bg_ref[:, cs], (8, CW))
            bv = jnp.broadcast_to(bv_ref[:, cs], (8, CW))

            def grp(r0, carry, cs=cs, tg=tg, tv=tv, bg=bg, bv=bv):
                pg, pv = carry
                rows = pl.ds(r0, RG)
                gx = g_ref[rows, cs].astype(f32)
                vx = v_ref[rows, cs].astype(f32)
                g = jnp.concatenate(_conv_group(pg, gx[0:8], gx[8:16], tg, bg, FFN_K, sub), axis=0)
                v = jnp.concatenate(_conv_group(pv, vx[0:8], vx[8:16], tv, bv, FFN_K, sub), axis=0)
                gc_ref[rows, cs] = g.astype(bf16)
                vc_ref[rows, cs] = v.astype(bf16)
                o_ref[rows, cs] = (_gelu_tanh(g)[0] * v).astype(bf16)
                return gx[8:16], vx[8:16]

            _groups(ts, grp, (jnp.where(first, 0.0, gp_ref[:, cs].astype(f32)),
                              jnp.where(first, 0.0, vp_ref[:, cs].astype(f32))))

    o = jax.ShapeDtypeStruct((S, FF), bf16)
    blk = pl.BlockSpec((ts, FF), lambda i: (i, 0))
    prev = lambda cb: pl.BlockSpec((8, FF), lambda i: (jnp.maximum(i * (ts // 8) - 1, 0), cb))
    return pl.pallas_call(
        body, name="ffnact_fwd", grid=(S // ts,),
        in_specs=[blk, prev(0), pl.BlockSpec((ts, FF), lambda i: (i, 1)), prev(1),
                  pl.BlockSpec((8, FF), lambda i: (0, 0)), pl.BlockSpec((8, FF), lambda i: (0, 1)),
                  pl.BlockSpec((1, FF), lambda i: (0, 0)), pl.BlockSpec((1, FF), lambda i: (0, 1))],
        out_specs=[blk, blk, blk], out_shape=[o, o, o],
        compiler_params=_cp("parallel"),
    )(up, up, up, up, w8, w8, b, b)


def ffnact_bwd(dact, gc, vc, ts):
    S = dact.shape[0]

    def body(d_ref, g_ref, v_ref, dg_ref, dv_ref):
        c = 0.7978845608028654
        for c0 in range(0, FF, CW):
            cs = slice(c0, c0 + CW)

            def grp(r0, _, cs=cs):
                rows = pl.ds(r0, RG)
                d = d_ref[rows, cs].astype(f32)
                g = g_ref[rows, cs].astype(f32)
                ge, t = _gelu_tanh(g)
                dgelu = 0.5 * (1.0 + t) + 0.5 * g * (1.0 - t * t) * c * (1.0 + 3.0 * 0.044715 * g * g)
                dg_ref[rows, cs] = (d * v_ref[rows, cs].astype(f32) * dgelu).astype(bf16)
                dv_ref[rows, cs] = (d * ge).astype(bf16)

            _groups(ts, grp)

    o = jax.ShapeDtypeStruct((S, FF), bf16)
    blk = pl.BlockSpec((ts, FF), lambda i: (i, 0))
    return pl.pallas_call(
        body, name="ffnact_bwd", grid=(S // ts,),
        in_specs=[blk, blk, blk], out_specs=[blk, blk], out_shape=[o, o],
        compiler_params=_cp("parallel"),
    )(dact, gc, vc)


def dwconv_bwd(dy, x, xcb, w8, wcb, K, bw, ts, name, act_c=None, into=None, ocb=0, out_cols=None):
    S, C = dy.shape
    nr = S // ts
    out_cols = out_cols or C
    n_act = 0 if act_c is None else 2

    def body(*refs):
        dy_ref, dyn_ref = refs[0:2]
        c_ref, cn_ref = (refs[2:4] if n_act else (None, None))
        x_ref, xp_ref, w_ref = refs[2 + n_act:5 + n_act]
        dx_ref, dw_ref, sd_ref = refs[-3:]
        i = pl.program_id(1)
        first, last = i == 0, i == nr - 1
        sub = lax.broadcasted_iota(jnp.int32, (8, CW), 0)

        def grad_y(d, c):
            if c is None:
                return d.astype(f32)
            cv = c.astype(f32)
            s = _sigmoid_fast(cv)
            return d.astype(f32) * s * (1.0 + cv * (1.0 - s))

        @pl.when(first)
        def _():
            dw_ref[...] = jnp.zeros_like(dw_ref)

        for c0 in range(0, bw, CW):
            cs = slice(c0, c0 + CW)
            taps = _taps(w_ref, cs, K)
            zero = jnp.zeros((8, CW), f32)

            def fwd(r0, carry, cs=cs):
                p, accs, accb = carry
                rows = pl.ds(r0, RG)
                g = grad_y(dy_ref[rows, cs], c_ref[rows, cs] if n_act else None)
                sd_ref[rows, cs] = g
                xv = x_ref[rows, cs].astype(f32)
                a, b = xv[0:8], xv[8:16]
                ga, gb = g[0:8], g[8:16]
                new = []
                for k in range(K):
                    s = K - 1 - k
                    xa, xb = (a, b) if s == 0 else (_down(p, a, s, sub), _down(a, b, s, sub))
                    new.append(accs[k] + ga * xa + gb * xb)
                return b, tuple(new), accb + ga + gb

            _, accs, accb = _groups(ts, fwd, (jnp.where(first, 0.0, xp_ref[:, cs].astype(f32)), (zero,) * K, zero))
            for k in range(K):
                dw_ref[k:k + 1, cs] += jnp.sum(accs[k], axis=0, keepdims=True)
            dw_ref[7:8, cs] += jnp.sum(accb, axis=0, keepdims=True)

            def bwd(r0, after, cs=cs, taps=taps):
                rows = pl.ds(r0, RG)
                g = sd_ref[rows, cs]
                a, b = g[0:8], g[8:16]
                da, db = zero, zero
                for k in range(K):
                    s = K - 1 - k
                    ua, ub = (a, b) if s == 0 else (_up(a, b, s, sub), _up(b, after, s, sub))
                    da = da + taps[k] * ua
                    db = db + taps[k] * ub
                dx_ref[rows, cs] = jnp.concatenate([da, db], axis=0).astype(bf16)
                return a

            halo = grad_y(dyn_ref[:, cs], cn_ref[:, cs] if n_act else None)
            _groups(ts, bwd, jnp.where(last, 0.0, halo), reverse=True)

    nxt = lambda j, i: (jnp.minimum((i + 1) * (ts // 8), S // 8 - 1), j)
    tile = pl.BlockSpec((ts, bw), lambda j, i: (i, j))
    acts = [] if act_c is None else [act_c, act_c]
    extra = [] if into is None else [into]
    n_in = 5 + n_act
    return pl.pallas_call(
        body, name=name, grid=(C // bw, nr),
        in_specs=[tile, pl.BlockSpec((8, bw), nxt)] + ([tile, pl.BlockSpec((8, bw), nxt)] if n_act else []) + [
            pl.BlockSpec((ts, bw), lambda j, i: (i, xcb + j)),
            pl.BlockSpec((8, bw), lambda j, i: (jnp.maximum(i * (ts // 8) - 1, 0), xcb + j)),
            pl.BlockSpec((8, bw), lambda j, i: (0, wcb + j))] + [pl.BlockSpec(memory_space=pl.ANY)] * len(extra),
        out_specs=[pl.BlockSpec((ts, bw), lambda j, i: (i, ocb + j)), pl.BlockSpec((8, bw), lambda j, i: (0, j))],
        out_shape=[jax.ShapeDtypeStruct((S, out_cols), bf16), jax.ShapeDtypeStruct((8, C), f32)],
        scratch_shapes=[pltpu.VMEM((ts, bw), f32)],
        input_output_aliases={n_in: 0} if extra else {},
        compiler_params=_cp("parallel", "arbitrary"),
    )(dy, dy, *acts, x, x, w8, *extra)


def gnorm_fwd(y, proj, w, ts):
    S = y.shape[0]

    def body(y_ref, z_ref, w_ref, o_ref):
        for k in range(NG):
            sl = slice(k * GW, (k + 1) * GW)
            wv = w_ref[:, sl]

            def grp(r0, _, sl=sl, wv=wv):
                rows = pl.ds(r0, NRG)
                z = z_ref[rows, sl].astype(f32)
                g = y_ref[rows, sl].astype(f32) * z * _sigmoid_fast(z)
                o_ref[rows, sl] = (g * _rms(g) * wv).astype(bf16)

            _groups(ts, grp, rg=NRG)

    row = pl.BlockSpec((ts, DI), lambda i: (i, 0))
    return pl.pallas_call(
        body, name="gnorm_fwd", grid=(S // ts,),
        in_specs=[row, row, pl.BlockSpec((1, DI), lambda i: (0, 0))],
        out_specs=row, out_shape=jax.ShapeDtypeStruct((S, DI), bf16),
        compiler_params=_cp("parallel"),
    )(y, proj, w)


def gnorm_bwd(dyn, y, proj, w, dproj, ts):
    S = y.shape[0]
    nt = S // ts

    def body(d_ref, y_ref, z_ref, w_ref, _, dy_ref, dz_ref, gw_ref, acc_ref):
        i = pl.program_id(0)

        @pl.when(i == 0)
        def _():
            acc_ref[...] = jnp.zeros_like(acc_ref)
            gw_ref[...] = jnp.zeros_like(gw_ref)

        for k in range(NG):
            sl = slice(k * GW, (k + 1) * GW)
            wv = w_ref[:, sl]

            def grp(r0, _, sl=sl, wv=wv):
                rows = pl.ds(r0, NRG)
                z = z_ref[rows, sl].astype(f32)
                yv = y_ref[rows, sl].astype(f32)
                s = _sigmoid_fast(z)
                sz = z * s
                g = yv * sz
                r = _rms(g)
                d = d_ref[rows, sl].astype(f32)
                acc_ref[:, sl] += _fold(d * g * r)
                dg = _rms_bwd(g, r, d * wv)
                dy_ref[rows, sl] = (dg * sz).astype(bf16)
                dz_ref[rows, sl] = (dg * yv * s * (1.0 + z * (1.0 - s))).astype(bf16)

            _groups(ts, grp, rg=NRG)

        @pl.when(i == nt - 1)
        def _():
            _flush(acc_ref, gw_ref, 0)

    row = pl.BlockSpec((ts, DI), lambda i: (i, 0))
    return pl.pallas_call(
        body, name="gnorm_bwd", grid=(nt,),
        in_specs=[row, row, row, pl.BlockSpec((1, DI), lambda i: (0, 0)), pl.BlockSpec(memory_space=pl.ANY)],
        out_specs=[row, row, pl.BlockSpec((8, DI), lambda i: (0, 0))],
        out_shape=[jax.ShapeDtypeStruct((S, DI), bf16), jax.ShapeDtypeStruct(dproj.shape, bf16),
                   jax.ShapeDtypeStruct((8, DI), f32)],
        scratch_shapes=[pltpu.VMEM((8, DI), f32)],
        input_output_aliases={4: 1},
        compiler_params=_cp("arbitrary"),
    )(dyn, y, proj, w, dproj)


def merge_fwd(proj, ys, ya, ts):
    S = ys.shape[0]

    def body(gs_ref, ga_ref, ys_ref, ya_ref, o_ref):
        for c0 in range(0, D, CW):
            cs = slice(c0, c0 + CW)

            def grp(r0, _, cs=cs):
                rows = pl.ds(r0, NRG)
                o_ref[rows, cs] = (_sigmoid_fast(gs_ref[rows, cs].astype(f32)) * ys_ref[rows, cs].astype(f32)
                                   + _sigmoid_fast(ga_ref[rows, cs].astype(f32)) * ya_ref[rows, cs].astype(f32)
                                   ).astype(bf16)

            _groups(ts, grp, rg=NRG)

    row = pl.BlockSpec((ts, D), lambda i: (i, 0))
    return pl.pallas_call(
        body, name="merge_fwd", grid=(S // ts,),
        in_specs=[pl.BlockSpec((ts, D), lambda i: (i, C_GS // D)), pl.BlockSpec((ts, D), lambda i: (i, C_GA // D)), row, row],
        out_specs=row, out_shape=jax.ShapeDtypeStruct((S, D), bf16),
        compiler_params=_cp("parallel"),
    )(proj, proj, ys, ya)


def merge_bwd(dm, proj, ys, ya, ts):
    S = ys.shape[0]

    def body(d_ref, gs_ref, ga_ref, ys_ref, ya_ref, dys_ref, dya_ref, dg_ref):
        for c0 in range(0, D, CW):
            cs = slice(c0, c0 + CW)

            def grp(r0, _, c0=c0, cs=cs):
                rows = pl.ds(r0, NRG)
                d = d_ref[rows, cs].astype(f32)
                ss = _sigmoid_fast(gs_ref[rows, cs].astype(f32))
                sa = _sigmoid_fast(ga_ref[rows, cs].astype(f32))
                dys_ref[rows, cs] = (d * ss).astype(bf16)
                dya_ref[rows, cs] = (d * sa).astype(bf16)
                dg_ref[rows, cs] = (d * ys_ref[rows, cs].astype(f32) * ss * (1.0 - ss)).astype(bf16)
                dg_ref[rows, D + c0:D + c0 + CW] = (d * ya_ref[rows, cs].astype(f32) * sa * (1.0 - sa)).astype(bf16)

            _groups(ts, grp, rg=NRG)

    row = pl.BlockSpec((ts, D), lambda i: (i, 0))
    o = jax.ShapeDtypeStruct((S, D), bf16)
    return pl.pallas_call(
        body, name="merge_bwd", grid=(S // ts,),
        in_specs=[row, pl.BlockSpec((ts, D), lambda i: (i, C_GS // D)), pl.BlockSpec((ts, D), lambda i: (i, C_GA // D)), row, row],
        out_specs=[row, row, pl.BlockSpec((ts, 2 * D), lambda i: (i, C_GS // (2 * D)))],
        out_shape=[o, o, jax.ShapeDtypeStruct((S, PM), bf16)],
        compiler_params=_cp("parallel"),
    )(dm, proj, proj, ys, ya)


def _ssd_consts():
    h = lax.broadcasted_iota(jnp.int32, (LANES, DI), 0)
    c = lax.broadcasted_iota(jnp.int32, (LANES, DI), 1)
    expand = (c // HD == h).astype(bf16)
    r = lax.broadcasted_iota(jnp.int32, (CH, CH), 0)
    cc = lax.broadcasted_iota(jnp.int32, (CH, CH), 1)
    tril = (cc <= r).astype(bf16)
    triu = (cc >= r).astype(bf16)
    return expand, expand.T, tril, triu


def _ssd_common(xbc_ref, dtr_ref, bias_ref, alog_ref, tril_ref, expand_ref=None, saved=None):
    dtr = dtr_ref[...] + bias_ref[...]
    dt = jnp.maximum(dtr, 0.0) + jnp.log1p(jnp.exp(-jnp.abs(dtr)))
    a = -jnp.exp(alog_ref[...])
    acs = _dot3_left(tril_ref[...], dt * a)
    if saved is None:
        acsx = _dot3_right(acs, expand_ref[...])
        dtx = _dot3_right(dt, expand_ref[...])
    else:
        acsx, dtx = saved[0][...], saved[1][...]
    x = xbc_ref[:, 0:DI].astype(f32)
    xdt = x * dtx
    e = jnp.exp(acsx)
    dsx = jnp.exp(acsx[CH - 1:CH, :] - acsx)
    return dtr, dt, a, acs, acsx, dtx, x, xdt, e, dsx


def _ssd_lmat(acs, acs_t, hh, causal):
    seg = acs[:, hh:hh + 1] - acs_t[hh:hh + 1, :]
    return jnp.where(causal, jnp.exp(jnp.minimum(seg, 0.0)), 0.0)


def ssd_fwd(xbc, dtr, bias, alog, dx_row, comm=None):
    S = xbc.shape[0]
    nc = S // CH
    expand, _, tril, _ = _ssd_consts()
    cm = _Comm(comm)

    def body(*refs):
        ins, (y_ref, hp_ref, ax_ref, dtx_ref), (h_ref, yd_ref), copies = cm.split(refs, 7, 4)
        xbc_ref, dtr_ref, bias_ref, alog_ref, dxr_ref, expand_ref, tril_ref = ins
        c = pl.program_id(0)
        cm.start(copies, c == 0)

        @pl.when(c == 0)
        def _():
            h_ref[...] = jnp.zeros_like(h_ref)

        _, _, _, acs, acsx, dtx, x, xdt, e, dsx = _ssd_common(xbc_ref, dtr_ref, bias_ref, alog_ref, tril_ref,
                                                              expand_ref=expand_ref)
        ax_ref[...] = acsx
        dtx_ref[...] = dtx
        acs_t = acs.T
        xb = xdt.astype(bf16)
        xd = (xdt * dsx).astype(bf16)
        causal = tril_ref[...] > 0
        for g in range(NG):
            gs = slice(g * GW, (g + 1) * GW)
            bg = xbc_ref[:, DI + g * NS:DI + (g + 1) * NS]
            cg = xbc_ref[:, DI + NG * NS + g * NS:DI + NG * NS + (g + 1) * NS]
            cb = _dot(cg, bg, "nt")
            hp = h_ref[g]
            hpb = hp.astype(bf16)
            hp_ref[0, g] = hpb
            yd_ref[:, gs] = _dot(cg, hpb) * e[:, gs]
            h_ref[g] = hp * e[CH - 1:CH, gs] + _dot(bg, xd[:, gs], "tn")
            for j in range(NH // NG):
                hh = g * (NH // NG) + j
                hs = slice(hh * HD, (hh + 1) * HD)
                m = (cb * _ssd_lmat(acs, acs_t, hh, causal)).astype(bf16)
                yd_ref[:, hs] += _dot(m, xb[:, hs])
        y_ref[...] = (yd_ref[...] + dxr_ref[...] * x).astype(bf16)
        cm.wait(copies, c == nc - 1)

    par = lambda shape: pl.BlockSpec(shape, lambda c: (0,) * len(shape))
    res = pl.pallas_call(
        body, name="ssd_fwd", grid=(nc,),
        in_specs=[pl.BlockSpec((CH, CONVD), lambda c: (c, 0)), pl.BlockSpec((CH, LANES), lambda c: (c, 0)),
                  par((1, LANES)), par((1, LANES)), par((1, DI)), par((LANES, DI)), par((CH, CH))] + cm.in_specs(),
        out_specs=[pl.BlockSpec((CH, DI), lambda c: (c, 0)), pl.BlockSpec((1, NG, NS, GW), lambda c: (c, 0, 0, 0)),
                   pl.BlockSpec((CH, DI), lambda c: (c, 0)), pl.BlockSpec((CH, DI), lambda c: (c, 0))] + cm.out_specs(),
        out_shape=[jax.ShapeDtypeStruct((S, DI), bf16), jax.ShapeDtypeStruct((nc, NG, NS, GW), bf16),
                   jax.ShapeDtypeStruct((S, DI), f32), jax.ShapeDtypeStruct((S, DI), f32)] + cm.out_shape(),
        scratch_shapes=[pltpu.VMEM((NG, NS, GW), f32), pltpu.VMEM((CH, DI), f32)] + cm.scratch(),
        compiler_params=_cp("arbitrary", side_effects=bool(cm.n)),
    )(xbc, dtr, bias, alog, dx_row, expand, tril, *cm.bufs)
    return res[0], res[1], (res[2], res[3]), res[4:]


def ssd_bwd(xbc, dtr, dy, hprev, saved, bias, alog, dx_row, comm=None):
    S = xbc.shape[0]
    nc = S // CH
    _, expand_t, tril, triu = _ssd_consts()
    cm = _Comm(comm)

    def body(*refs):
        ins, outs, scr, copies = cm.split(refs, 12, 3)
        xbc_ref, dtr_ref, dy_ref, hp_ref, ax_ref, dtx_ref, bias_ref, alog_ref, dxr_ref, expt_ref, tril_ref, triu_ref = ins
        dxbc_ref, ddtr_ref, acc_ref = outs
        dh_ref, dxs_ref, t_ref, accb_ref, acca_ref, accd_ref = scr
        c = pl.program_id(0)
        cm.start(copies, c == 0)

        @pl.when(c == 0)
        def _():
            dh_ref[...] = jnp.zeros_like(dh_ref)
            accb_ref[...] = jnp.zeros_like(accb_ref)
            acca_ref[...] = jnp.zeros_like(acca_ref)
            accd_ref[...] = jnp.zeros_like(accd_ref)

        dtr, dt, a, acs, _, dtx, x, xdt, e, dsx = _ssd_common(xbc_ref, dtr_ref, bias_ref, alog_ref, tril_ref,
                                                              saved=(ax_ref, dtx_ref))
        acs_t = acs.T
        xb = xdt.astype(bf16)
        xdf = xdt * dsx
        xd = xdf.astype(bf16)
        dyv = dy_ref[...].astype(f32)
        dyb = dy_ref[...]
        dye = (dyv * e).astype(bf16)
        causal = tril_ref[...] > 0
        lane = lax.broadcasted_iota(jnp.int32, (CH, LANES), 1)
        subl = lax.broadcasted_iota(jnp.int32, (LANES, CH), 0)
        ccol = jnp.zeros((CH, LANES), f32)
        rrow = jnp.zeros((LANES, CH), f32)
        last_row = lax.broadcasted_iota(jnp.int32, (CH, 1), 0) == CH - 1
        for g in range(NG):
            gs = slice(g * GW, (g + 1) * GW)
            bsl = slice(DI + g * NS, DI + (g + 1) * NS)
            csl = slice(DI + NG * NS + g * NS, DI + NG * NS + (g + 1) * NS)
            bg = xbc_ref[:, bsl]
            cg = xbc_ref[:, csl]
            cb = _dot(cg, bg, "nt")
            hpb = hp_ref[0, g]
            dhn = dh_ref[g]
            dhnb = dhn.astype(bf16)
            yoff = _dot(cg, hpb) * e[:, gs]
            dxd = _dot(bg, dhnb)
            t2 = dxd * xdf[:, gs]
            t3 = jnp.sum(dhn * hpb.astype(f32), axis=0, keepdims=True) * e[CH - 1:CH, gs]
            t_ref[:, gs] = dyv[:, gs] * yoff - t2 + jnp.where(last_row, jnp.sum(t2, axis=0, keepdims=True) + t3, 0.0)
            dxs_ref[:, gs] = dxd * dsx[:, gs]
            dcg = _dot(dye[:, gs], hpb, "nt")
            dbg = _dot(xd[:, gs], dhnb, "nt")
            dh_ref[g] = dhn * e[CH - 1:CH, gs] + _dot(cg, dye[:, gs], "tn")
            dcb = jnp.zeros((CH, CH), f32)
            for j in range(NH // NG):
                hh = g * (NH // NG) + j
                hs = slice(hh * HD, (hh + 1) * HD)
                lm = _ssd_lmat(acs, acs_t, hh, causal)
                m = cb * lm
                dm = _dot(dyb[:, hs], xb[:, hs], "nt")
                gm = dm * m
                ccol = ccol + jnp.sum(gm, axis=1, keepdims=True) * (lane == hh).astype(f32)
                rrow = rrow + jnp.sum(gm, axis=0, keepdims=True) * (subl == hh).astype(f32)
                dcb = dcb + dm * lm
                dxs_ref[:, hs] += _dot(m.astype(bf16), dyb[:, hs], "tn")
            dcbb = dcb.astype(bf16)
            dxbc_ref[:, csl] = (dcg + _dot(dcbb, bg)).astype(bf16)
            dxbc_ref[:, bsl] = (dbg + _dot(dcbb, cg, "tn")).astype(bf16)
        dxf = dxs_ref[...]
        dxbc_ref[:, 0:DI] = (dxf * dtx + dxr_ref[...] * dyv).astype(bf16)
        expt = expt_ref[...]
        dacs = ccol - rrow.T + _dot2_right(t_ref[...], expt)
        dadt = _dot3_left(triu_ref[...], dacs)
        ddt = _dot2_right(dxf * x, expt) + dadt * a
        ddtr = ddt * _sigmoid(dtr)
        ddtr_ref[...] = ddtr
        accb_ref[...] += ddtr
        acca_ref[...] += dadt * dt
        accd_ref[...] += _dot2_right(dyv * x, expt)

        @pl.when(c == nc - 1)
        def _():
            acc_ref[...] = jnp.zeros_like(acc_ref)
            acc_ref[0:1, :] = jnp.sum(accb_ref[...], axis=0, keepdims=True)
            acc_ref[1:2, :] = jnp.sum(acca_ref[...], axis=0, keepdims=True) * a
            acc_ref[2:3, :] = jnp.sum(accd_ref[...], axis=0, keepdims=True)

        cm.wait(copies, c == nc - 1)

    par = lambda shape: pl.BlockSpec(shape, lambda c: (0,) * len(shape))
    rev = lambda c: (nc - 1 - c, 0)
    res = pl.pallas_call(
        body, name="ssd_bwd", grid=(nc,),
        in_specs=[pl.BlockSpec((CH, CONVD), rev), pl.BlockSpec((CH, LANES), rev), pl.BlockSpec((CH, DI), rev),
                  pl.BlockSpec((1, NG, NS, GW), lambda c: (nc - 1 - c, 0, 0, 0)),
                  pl.BlockSpec((CH, DI), rev), pl.BlockSpec((CH, DI), rev),
                  par((1, LANES)), par((1, LANES)), par((1, DI)), par((DI, LANES)),
                  par((CH, CH)), par((CH, CH))] + cm.in_specs(),
        out_specs=[pl.BlockSpec((CH, CONVD), rev), pl.BlockSpec((CH, LANES), rev), par((8, LANES))] + cm.out_specs(),
        out_shape=[jax.ShapeDtypeStruct((S, CONVD), bf16), jax.ShapeDtypeStruct((S, LANES), f32),
                   jax.ShapeDtypeStruct((8, LANES), f32)] + cm.out_shape(),
        scratch_shapes=[pltpu.VMEM((NG, NS, GW), f32), pltpu.VMEM((CH, DI), f32), pltpu.VMEM((CH, DI), f32),
                        pltpu.VMEM((CH, LANES), f32), pltpu.VMEM((CH, LANES), f32),
                        pltpu.VMEM((CH, LANES), f32)] + cm.scratch(),
        compiler_params=_cp("arbitrary", side_effects=bool(cm.n)),
    )(xbc, dtr, dy, hprev, *saved, bias, alog, dx_row, expand_t, tril, triu, *cm.bufs)
    return res[0], res[1], res[2], res[3:]


def _partner(t):
    half = AD // 2
    return jnp.concatenate([t[h * AD + o:h * AD + o + half] for h in range(t.shape[0] // AD) for o in (half, 0)], axis=0)


def _rope(t, cos, sin):
    reps = t.shape[0] // AD
    return t * jnp.tile(cos, (reps, 1)) + _partner(t) * jnp.tile(sin, (reps, 1))


def _rope_t(d, cos, sin):
    reps = d.shape[0] // AD
    return d * jnp.tile(cos, (reps, 1)) - _partner(d) * jnp.tile(sin, (reps, 1))


def _lanes_of_group(t, g):
    return jnp.concatenate([t[(g * REP + r) * AD:(g * REP + r + 1) * AD] for r in range(REP)], axis=1)


def _attn_probs(qg, kp, kc, sink_ref, g, not_first):
    n = qg.shape[1]
    s = lax.broadcasted_iota(jnp.int32, (WIN, n), 0)
    t = lax.broadcasted_iota(jnp.int32, (WIN, n), 1) % WIN
    neg = -1e30
    sink = jnp.concatenate([jnp.broadcast_to(sink_ref[0:1, g * REP + r:g * REP + r + 1], (1, WIN)) for r in range(REP)],
                           axis=1)
    sp = jnp.where(jnp.logical_and(s > t, not_first), _dot(kp, qg, "tn"), neg)
    sc = jnp.where(s <= t, _dot(kc, qg, "tn"), neg)
    m = jnp.maximum(jnp.maximum(jnp.max(sp, axis=0, keepdims=True), jnp.max(sc, axis=0, keepdims=True)), sink)
    pp = jnp.exp(sp - m)
    pc = jnp.exp(sc - m)
    ps = jnp.exp(sink - m)
    inv = 1.0 / (jnp.sum(pp, axis=0, keepdims=True) + jnp.sum(pc, axis=0, keepdims=True) + ps)
    return pp * inv, pc * inv, ps * inv


def attn_fwd(qt, kvt, cos, sin, sinks):
    S = qt.shape[1]
    nb = S // WIN
    cur = lambda i: (0, i)
    prev = lambda i: (0, jnp.maximum(i - 1, 0))

    def body(q_ref, kv_ref, kvp_ref, cos_ref, sin_ref, cosp_ref, sinp_ref, sink_ref, o_ref):
        i = pl.program_id(0)
        q = (_rope(q_ref[...].astype(f32), cos_ref[...], sin_ref[...]) * (AD ** -0.5)).astype(bf16)
        kc = _rope(kv_ref[0:KVW, :].astype(f32), cos_ref[...], sin_ref[...]).astype(bf16)
        kp = _rope(kvp_ref[0:KVW, :].astype(f32), cosp_ref[...], sinp_ref[...]).astype(bf16)
        for g in range(KVH):
            ks = slice(g * AD, (g + 1) * AD)
            vs = slice(KVW + g * AD, KVW + (g + 1) * AD)
            pp, pc, _ = _attn_probs(_lanes_of_group(q, g), kp[ks], kc[ks], sink_ref, g, i > 0)
            o = _dot(kvp_ref[vs, :], pp.astype(bf16)) + _dot(kv_ref[vs, :], pc.astype(bf16))
            for r in range(REP):
                h = g * REP + r
                o_ref[h * AD:(h + 1) * AD, :] = o[:, r * WIN:(r + 1) * WIN].astype(bf16)

    tab = pl.BlockSpec((AD, WIN), cur)
    tabp = pl.BlockSpec((AD, WIN), prev)
    return pl.pallas_call(
        body, name="attn_fwd", grid=(nb,),
        in_specs=[pl.BlockSpec((D, WIN), cur), pl.BlockSpec((2 * KVW, WIN), cur), pl.BlockSpec((2 * KVW, WIN), prev),
                  tab, tab, tabp, tabp, pl.BlockSpec((1, LANES), lambda i: (0, 0))],
        out_specs=pl.BlockSpec((D, WIN), cur),
        out_shape=jax.ShapeDtypeStruct((D, S), bf16),
        compiler_params=_cp("parallel"),
    )(qt, kvt, kvt, cos, sin, cos, sin, sinks)


def attn_bwd(qt, kvt, cos, sin, sinks, daot, comm=None):
    S = qt.shape[1]
    nb = S // WIN
    cur = lambda i: (0, jnp.minimum(i, nb - 1))
    prev = lambda i: (0, jnp.maximum(i - 1, 0))
    cm = _Comm(comm)

    def body(*refs):
        ins, (dq_ref, dkv_ref, ds_ref), scr, copies = cm.split(refs, 9, 3)
        q_ref, kv_ref, kvp_ref, cos_ref, sin_ref, cosp_ref, sinp_ref, sink_ref, do_ref = ins
        ck_ref, cv_ref, dqs_ref, dkp_ref, dvp_ref, dkc_ref, dvc_ref, accs_ref = scr
        i = pl.program_id(0)
        cm.start(copies, i == 0)

        @pl.when(i == 0)
        def _():
            ck_ref[...] = jnp.zeros_like(ck_ref)
            cv_ref[...] = jnp.zeros_like(cv_ref)
            accs_ref[...] = jnp.zeros_like(accs_ref)

        @pl.when(i == nb)
        def _():
            dkp_ref[...] = jnp.zeros_like(dkp_ref)
            dvp_ref[...] = jnp.zeros_like(dvp_ref)

        @pl.when(i < nb)
        def _():
            q = (_rope(q_ref[...].astype(f32), cos_ref[...], sin_ref[...]) * (AD ** -0.5)).astype(bf16)
            kc = _rope(kv_ref[0:KVW, :].astype(f32), cos_ref[...], sin_ref[...]).astype(bf16)
            kp = _rope(kvp_ref[0:KVW, :].astype(f32), cosp_ref[...], sinp_ref[...]).astype(bf16)
            do = do_ref[...]
            for g in range(KVH):
                ks = slice(g * AD, (g + 1) * AD)
                vs = slice(KVW + g * AD, KVW + (g + 1) * AD)
                qg = _lanes_of_group(q, g)
                dog = _lanes_of_group(do, g)
                pp, pc, ps = _attn_probs(qg, kp[ks], kc[ks], sink_ref, g, i > 0)
                dpp = _dot(kvp_ref[vs, :], dog, "tn")
                dpc = _dot(kv_ref[vs, :], dog, "tn")
                delta = jnp.sum(pp * dpp + pc * dpc, axis=0, keepdims=True)
                dsp = (pp * (dpp - delta)).astype(bf16)
                dsc = (pc * (dpc - delta)).astype(bf16)
                accs_ref[g:g + 1, :] -= ps * delta
                dqg = (_dot(kp[ks], dsp) + _dot(kc[ks], dsc)) * (AD ** -0.5)
                for r in range(REP):
                    h = g * REP + r
                    dqs_ref[h * AD:(h + 1) * AD, :] = dqg[:, r * WIN:(r + 1) * WIN]
                dkp_ref[ks, :] = _dot(qg, dsp, "nt")
                dkc_ref[ks, :] = _dot(qg, dsc, "nt")
                dvp_ref[ks, :] = _dot(dog, pp.astype(bf16), "nt")
                dvc_ref[ks, :] = _dot(dog, pc.astype(bf16), "nt")
            dq_ref[...] = _rope_t(dqs_ref[...], cos_ref[...], sin_ref[...]).astype(bf16)

        dkv_ref[0:KVW, :] = _rope_t(ck_ref[...] + dkp_ref[...], cosp_ref[...], sinp_ref[...]).astype(bf16)
        dkv_ref[KVW:2 * KVW, :] = (cv_ref[...] + dvp_ref[...]).astype(bf16)

        @pl.when(i < nb)
        def _():
            ck_ref[...] = dkc_ref[...]
            cv_ref[...] = dvc_ref[...]

        @pl.when(i == nb)
        def _():
            lane = lax.broadcasted_iota(jnp.int32, (1, LANES), 1)
            row = jnp.zeros((1, LANES), f32)
            for h in range(AH):
                part = accs_ref[h // REP:h // REP + 1, (h % REP) * WIN:(h % REP + 1) * WIN]
                row = row + jnp.where(lane == h, jnp.sum(part, axis=1, keepdims=True), 0.0)
            ds_ref[...] = jnp.zeros_like(ds_ref)
            ds_ref[0:1, :] = row

        cm.wait(copies, i == nb)

    tab = pl.BlockSpec((AD, WIN), cur)
    tabp = pl.BlockSpec((AD, WIN), prev)
    kvs = lambda: pltpu.VMEM((KVW, WIN), f32)
    res = pl.pallas_call(
        body, name="attn_bwd", grid=(nb + 1,),
        in_specs=[pl.BlockSpec((D, WIN), cur), pl.BlockSpec((2 * KVW, WIN), cur), pl.BlockSpec((2 * KVW, WIN), prev),
                  tab, tab, tabp, tabp, pl.BlockSpec((1, LANES), lambda i: (0, 0)),
                  pl.BlockSpec((D, WIN), cur)] + cm.in_specs(),
        out_specs=[pl.BlockSpec((D, WIN), cur), pl.BlockSpec((2 * KVW, WIN), prev),
                   pl.BlockSpec((8, LANES), lambda i: (0, 0))] + cm.out_specs(),
        out_shape=[jax.ShapeDtypeStruct((D, S), bf16), jax.ShapeDtypeStruct((2 * KVW, S), bf16),
                   jax.ShapeDtypeStruct((8, LANES), f32)] + cm.out_shape(),
        scratch_shapes=[kvs(), kvs(), pltpu.VMEM((D, WIN), f32), kvs(), kvs(), kvs(), kvs(),
                        pltpu.VMEM((8, REP * WIN), f32)] + cm.scratch(),
        compiler_params=_cp("arbitrary", side_effects=bool(cm.n)),
    )(qt, kvt, kvt, cos, sin, cos, sin, sinks, daot, *cm.bufs)
    return res[0], res[1], res[2], res[3:]


ADAM_C1 = 1.0 / (1.0 - ADAM_B1 ** ADAM_STEP)
ADAM_C2 = 1.0 / (1.0 - ADAM_B2 ** ADAM_STEP)


def _adam_update(g, w, m, v):
    nm = ADAM_B1 * m + (1.0 - ADAM_B1) * g
    nv = ADAM_B2 * v + (1.0 - ADAM_B2) * (g * g)
    return -ADAM_LR * ((nm * ADAM_C1) / (jnp.sqrt(nv * ADAM_C2) + ADAM_EPS) + ADAM_WD * w), nm, nv


def adamw(parts, w, m, v, tr, name):
    n, R, C = parts.shape

    def body(p_ref, w_ref, m_ref, v_ref, g_ref, d_ref, nm_ref, nv_ref):
        def grp(g0, _):
            r0 = pl.multiple_of(g0 * RG, RG)
            rows = pl.ds(r0, RG)
            g = p_ref[0, rows, :].astype(f32)
            for k in range(1, n):
                g = g + p_ref[k, rows, :].astype(f32)
            d, nm, nv = _adam_update(g, w_ref[rows, :], m_ref[rows, :], v_ref[rows, :])
            g_ref[rows, :] = g
            d_ref[rows, :] = d
            nm_ref[rows, :] = nm
            nv_ref[rows, :] = nv
            return 0

        lax.fori_loop(0, tr // RG, grp, 0)

    row = pl.BlockSpec((tr, C), lambda i: (i, 0))
    o = jax.ShapeDtypeStruct((R, C), f32)
    return pl.pallas_call(
        body, name=name, grid=(R // tr,),
        in_specs=[pl.BlockSpec((n, tr, C), lambda i: (0, i, 0)), row, row, row],
        out_specs=[row, row, row, row], out_shape=[o, o, o, o],
        compiler_params=_cp("parallel"),
    )(parts, w, m, v)


SMALL_ROW = (("norm_mix_post_w", D), ("norm_ffn_pre_w", D), ("norm_ffn_post_w", D), ("ssd_norm_w", DI),
             ("ssd_conv_b", CONVD), ("ffn_conv_b", 2 * FF), ("ssd_dt_bias", NH), ("ssd_a_log", NH), ("ssd_d", NH),
             ("attn_sinks", AH), ("loss", 1))
CONV_BLOCK = 1152
SSD_CONV_COLS = CONVD // N_DEV
FFN_CONV_COLS = 2 * FF // N_DEV


def _row_offsets():
    off, o = {}, 0
    for name, n in SMALL_ROW:
        off[name] = (o, n)
        o += -(-n // LANES) * LANES
    return off, o


def adamw_small(recv_row, recv_pre, recv_conv, params):
    off, _ = _row_offsets()
    names = list(params)
    n = len(names)

    def total(ref, rows, lo, width):
        g = ref[0, rows, lo:lo + width]
        for d in range(1, N_DEV):
            g = g + ref[d, rows, lo:lo + width]
        return g

    def grad_of(name, row_ref, pre_ref, conv_ref):
        if name == "norm_mix_pre_w":
            return total(pre_ref, slice(0, 1), 0, D)
        if name == "ssd_conv_w":
            return total(conv_ref, slice(0, SSD_K), 0, SSD_CONV_COLS)
        if name == "ffn_conv_w":
            return total(conv_ref, slice(0, FFN_K), 3 * LANES, FFN_CONV_COLS)
        o, width = off[name]
        return total(row_ref, slice(0, 1), o, width)

    def body(row_ref, pre_ref, conv_ref, *refs):
        ins, outs = refs[:3 * n], refs[3 * n:]
        for k, name in enumerate(names):
            w_ref, m_ref, v_ref = ins[3 * k:3 * k + 3]
            g_ref, d_ref, nm_ref, nv_ref = outs[4 * k:4 * k + 4]
            g = grad_of(name, row_ref, pre_ref, conv_ref)
            d, nm, nv = _adam_update(g, w_ref[...], m_ref[...], v_ref[...])
            g_ref[...] = g
            d_ref[...] = d
            nm_ref[...] = nm
            nv_ref[...] = nv
        outs[4 * n][...] = total(row_ref, slice(0, 1), off["loss"][0], LANES)

    flat = [t for name in names for t in params[name]]
    out_shape = [jax.ShapeDtypeStruct(params[name][0].shape, f32) for name in names for _ in range(4)]
    res = pl.pallas_call(
        body, name="adamw_small",
        out_shape=out_shape + [jax.ShapeDtypeStruct((1, LANES), f32)],
        compiler_params=pltpu.CompilerParams(vmem_limit_bytes=VMEM_LIMIT),
    )(recv_row, recv_pre, recv_conv, *flat)
    return {name: res[4 * k:4 * k + 4] for k, name in enumerate(names)}, res[4 * n]


def _cat_rows(parts):
    words = [lax.bitcast_convert_type(p, jnp.uint16) for p in parts]
    return lax.bitcast_convert_type(jnp.concatenate(words, axis=0), bf16)


def _pad_rows8(w):
    return jnp.pad(w, ((0, 8 - w.shape[0]), (0, 0)))


def _pad_lanes(v):
    return jnp.pad(v.reshape(1, -1), ((0, 0), (0, LANES - v.size)))


WEIGHTS = ('norm_mix_pre_w', 'w_in', 'ssd_conv_w', 'ssd_conv_b', 'ssd_dt_bias', 'ssd_a_log', 'ssd_d', 'ssd_norm_w',
           'ssd_w_out', 'attn_sinks', 'attn_w_out', 'w_mix_out', 'norm_mix_post_w', 'norm_ffn_pre_w', 'ffn_w_up',
           'ffn_conv_w', 'ffn_conv_b', 'ffn_w_down', 'norm_ffn_post_w')
W_IN_ROWS = IN_DIM // N_DEV
W_IN_PAD = 1104
W_IN_SPLIT = (672, 768, 832)
TS = 512


def kernel(x, positions, norm_mix_pre_w, w_in, ssd_conv_w, ssd_conv_b, ssd_dt_bias, ssd_a_log, ssd_d, ssd_norm_w, ssd_w_out, attn_sinks, attn_w_out, w_mix_out, norm_mix_post_w, norm_ffn_pre_w, ffn_w_up, ffn_conv_w, ffn_conv_b, ffn_w_down, norm_ffn_post_w, loss_target, m_norm_mix_pre_w, m_w_in, m_ssd_conv_w, m_ssd_conv_b, m_ssd_dt_bias, m_ssd_a_log, m_ssd_d, m_ssd_norm_w, m_ssd_w_out, m_attn_sinks, m_attn_w_out, m_w_mix_out, m_norm_mix_post_w, m_norm_ffn_pre_w, m_ffn_w_up, m_ffn_conv_w, m_ffn_conv_b, m_ffn_w_down, m_norm_ffn_post_w, v_norm_mix_pre_w, v_w_in, v_ssd_conv_w, v_ssd_conv_b, v_ssd_dt_bias, v_ssd_a_log, v_ssd_d, v_ssd_norm_w, v_ssd_w_out, v_attn_sinks, v_attn_w_out, v_w_mix_out, v_norm_mix_post_w, v_norm_ffn_pre_w, v_ffn_w_up, v_ffn_conv_w, v_ffn_conv_b, v_ffn_w_down, v_norm_ffn_post_w):
    a = locals()
    r2 = lambda t: t.reshape(t.shape[-2], t.shape[-1])
    w = {n: r2(a[n]) for n in WEIGHTS}
    m = {n: r2(a["m_" + n]) for n in WEIGHTS}
    v = {n: r2(a["v_" + n]) for n in WEIGHTS}
    xs, target = x[0], loss_target[0]
    S = xs.shape[0]
    ts = TS

    w_in_blk = jnp.pad(w["w_in"].T.astype(bf16), ((0, W_IN_PAD - W_IN_ROWS), (0, 0)))
    conv_blk = jnp.concatenate([_pad_rows8(w["ssd_conv_w"]), _pad_rows8(w["ffn_conv_w"]),
                                jnp.zeros((8, CONV_BLOCK - SSD_CONV_COLS - FFN_CONV_COLS), f32)], axis=1)
    u, cos, sin, (g_in, g_conv) = prenorm_fwd(xs, w["norm_mix_pre_w"], positions, ts, [w_in_blk, conv_blk])
    wt = g_in[:, :W_IN_ROWS].reshape(IN_DIM, D)
    w_main_t = _cat_rows([wt[IN_OFF[0]:IN_OFF[1]], wt[IN_OFF[6]:IN_OFF[8]], wt[IN_OFF[1]:IN_OFF[2]]])
    w_q_t = wt[IN_OFF[3]:IN_OFF[4]]
    w_kv_t = wt[IN_OFF[4]:IN_OFF[6]]
    w_dt_t = jnp.pad(wt[IN_OFF[2]:IN_OFF[3]], ((0, LANES - NH), (0, 0)))
    conv_w8 = g_conv[:, :, 0:SSD_CONV_COLS].transpose(1, 0, 2).reshape(8, CONVD)
    fconv_w8 = g_conv[:, :, SSD_CONV_COLS:SSD_CONV_COLS + FFN_CONV_COLS].transpose(1, 0, 2).reshape(8, 2 * FF)
    bias = _pad_lanes(w["ssd_dt_bias"])
    alog = _pad_lanes(w["ssd_a_log"])
    dx_row = jnp.repeat(w["ssd_d"].reshape(-1), HD).reshape(1, DI)
    sinks = _pad_lanes(w["attn_sinks"])

    later = [w["ssd_w_out"].astype(bf16), w["attn_w_out"].astype(bf16), w["w_mix_out"].astype(bf16)]
    proj, (g_so, g_ao, g_mix) = mm(u, w_main_t, "nt", bf16, "mm_proj", comm=(later, (False,) * 3))
    w_ssd_out, w_attn_out, w_mix = g_so.reshape(DI, D), g_ao.reshape(D, D), g_mix.reshape(D, D)
    qt = mm(w_q_t, u, "nt", bf16, "mm_q")
    kvt = mm(w_kv_t, u, "nt", bf16, "mm_kv")
    dtr = mm(u, w_dt_t, "nt", f32, "mm_dt")
    xbc, conv_c = ssdconv_fwd(proj, conv_w8, w["ssd_conv_b"], ts)
    y, hprev, ssd_saved, (g_up, g_down) = ssd_fwd(xbc, dtr, bias, alog, dx_row, comm=(
        [w["ffn_w_up"].T.astype(bf16), w["ffn_w_down"].astype(bf16)], (False, False)))
    w_up_t = g_up.reshape(2 * FF, D)
    w_down = g_down.reshape(FF, D)
    yn = gnorm_fwd(y, proj, w["ssd_norm_w"], ts)
    ys = mm(yn, w_ssd_out, "nn", bf16, "mm_ssd_out")
    aot = attn_fwd(qt, kvt, cos, sin, sinks)
    ya = mm(aot, w_attn_out, "tn", bf16, "mm_attn_out")
    merged = merge_fwd(proj, ys, ya, ts)
    mo = mm(merged, w_mix, "nn", bf16, "mm_mix")
    x1, h = post_fwd(xs, mo, w["norm_mix_post_w"], w["norm_ffn_pre_w"], ts)
    up = mm(h, w_up_t, "nt", bf16, "mm_up")
    act, gate_c, val_c = ffnact_fwd(up, fconv_w8, w["ffn_conv_b"], ts)
    ff = mm(act, w_down, "nn", bf16, "mm_down")
    loss_blk, dout, dff, g_post2 = loss_head(x1, ff, target, w["norm_ffn_post_w"], ts)

    dact = mm(dff, w_down, "nt", bf16, "mm_dact")
    gw_down = mm(act, dff, "tn", bf16, "mm_g_down")
    dgate, dval = ffnact_bwd(dact, gate_c, val_c, ts)
    dup_pre, g_fconv_a = dwconv_bwd(dgate, up, 0, fconv_w8, 0, FFN_K, FF, ts, "ffnconv_bwd_gate", out_cols=2 * FF)
    dup_pre, g_fconv_b = dwconv_bwd(dval, up, 1, fconv_w8, 1, FFN_K, FF, ts, "ffnconv_bwd_val", into=dup_pre, ocb=1,
                                    out_cols=2 * FF)
    g_fconv = jnp.concatenate([g_fconv_a, g_fconv_b], axis=1)
    dh, (r_down,) = mm(dup_pre, w_up_t, "nn", bf16, "mm_dh", comm=([gw_down.reshape(N_DEV, FF // N_DEV, D)], (True,)))
    gw_up_t = mm(dup_pre, h, "tn", bf16, "mm_g_up")
    dx1, dmo, g_norms = post_bwd(dout, dh, x1, mo, w["norm_mix_post_w"], w["norm_ffn_pre_w"], ts)
    dmerged = mm(dmo, w_mix, "nt", bf16, "mm_dmerged")
    gw_mix = mm(merged, dmo, "tn", bf16, "mm_g_mix")
    dys, dya, dproj = merge_bwd(dmerged, proj, ys, ya, ts)
    daot = mm(w_attn_out, dya, "nt", bf16, "mm_dao")
    gw_attn_out = mm(aot, dya, "nn", bf16, "mm_g_attn_out")
    dqt, dkvt, g_sinks, (r_up,) = attn_bwd(qt, kvt, cos, sin, sinks, daot,
                                           comm=([gw_up_t.reshape(N_DEV, 2 * FF // N_DEV, D)], (True,)))
    dyn = mm(dys, w_ssd_out, "nt", bf16, "mm_dyn")
    gw_ssd_out = mm(yn, dys, "tn", bf16, "mm_g_ssd_out")
    dy, dproj, g_gnorm = gnorm_bwd(dyn, y, proj, w["ssd_norm_w"], dproj, ts)
    sends = [gw_ssd_out.reshape(N_DEV, DI // N_DEV, D), gw_attn_out.reshape(N_DEV, D // N_DEV, D),
             gw_mix.reshape(N_DEV, D // N_DEV, D)]
    dxbc, ddtr, g_ssd, (r_so, r_ao, r_mix) = ssd_bwd(xbc, dtr, dy, hprev, ssd_saved, bias, alog, dx_row,
                                                     comm=(sends, (True,) * 3))
    dproj, g_conv_w = dwconv_bwd(dxbc, proj, C_XBC // 1024, conv_w8, 0, SSD_K, 1024, ts, "ssdconv_bwd", act_c=conv_c,
                                 into=dproj, ocb=C_XBC // 1024, out_cols=PM)
    ddtr_b = ddtr.astype(bf16)
    du_c = mm(ddtr_b, w_dt_t, "nn", bf16, "mm_du_dt")
    g_main_t = mm(dproj, u, "tn", bf16, "mm_g_in")
    g_q_t = mm(dqt, u, "nn", bf16, "mm_g_q")
    g_kv_t = mm(dkvt, u, "nn", bf16, "mm_g_kv")
    g_dt_t = mm(ddtr_b, u, "tn", bf16, "mm_g_dt")
    g_wt = _cat_rows([g_main_t[C_Z:C_GS], g_main_t[C_XBC:PM], g_dt_t[:NH], g_q_t, g_kv_t, g_main_t[C_GS:C_XBC]])
    send_in = jnp.pad(g_wt.reshape(N_DEV, W_IN_ROWS, D), ((0, 0), (0, W_IN_PAD - W_IN_ROWS), (0, 0)))
    pieces = {"norm_mix_post_w": g_norms[1:2], "norm_ffn_pre_w": g_norms[0:1], "norm_ffn_post_w": g_post2[0:1],
              "ssd_norm_w": g_gnorm[0:1], "ssd_conv_b": g_conv_w[7:8], "ffn_conv_b": g_fconv[7:8],
              "ssd_dt_bias": g_ssd[0:1], "ssd_a_log": g_ssd[1:2], "ssd_d": g_ssd[2:3], "attn_sinks": g_sinks[0:1],
              "loss": loss_blk[0:1]}
    row = jnp.concatenate([jnp.pad(pieces[n][:, :min(k, pieces[n].shape[1])],
                                   ((0, 0), (0, -(-k // LANES) * LANES - min(k, pieces[n].shape[1]))))
                           for n, k in SMALL_ROW], axis=1)
    send_row = jnp.pad(row, ((0, 7), (0, 0)))
    send_conv = jnp.concatenate(
        [g_conv_w.reshape(8, N_DEV, SSD_CONV_COLS).transpose(1, 0, 2),
         g_fconv.reshape(8, N_DEV, FFN_CONV_COLS).transpose(1, 0, 2),
         jnp.zeros((N_DEV, 8, CONV_BLOCK - SSD_CONV_COLS - FFN_CONV_COLS), f32)], axis=2)
    r0, r1, r2 = W_IN_SPLIT
    du_a, (r_in_a, recv_row, recv_conv) = mm(dproj, w_main_t, "nn", bf16, "mm_du", comm=(
        [send_in[:, :r0], send_row, send_conv], (True, False, True)))
    du_d, (r_in_b,) = mm(dqt, w_q_t, "tn", bf16, "mm_du_q", comm=([send_in[:, r0:r1]], (True,)))
    du_b, (r_in_c,) = mm(dkvt, w_kv_t, "tn", bf16, "mm_du_kv", comm=([send_in[:, r1:r2]], (True,)))
    grad_x, g_pre, (r_in_d,) = prenorm_bwd(xs, w["norm_mix_pre_w"], (du_a, du_b, du_c, du_d), dx1, ts,
                                           comm=([send_in[:, r2:]], (True,)))
    (recv_pre,) = exchange([g_pre], (False,), "gather_last")

    r_in = jnp.concatenate([r_in_a, r_in_b, r_in_c, r_in_d], axis=1)
    tpad = lambda t: jnp.pad(t.T, ((0, W_IN_PAD - W_IN_ROWS), (0, 0)))
    o_in = [t[:W_IN_ROWS].T for t in adamw(r_in, tpad(w["w_in"]), tpad(m["w_in"]), tpad(v["w_in"]), 368, "adamw_w_in")]
    o_up = [t.T for t in adamw(r_up, w["ffn_w_up"].T, m["ffn_w_up"].T, v["ffn_w_up"].T, 352, "adamw_w_up")]
    big = {"w_in": o_in, "ffn_w_up": o_up,
           "ssd_w_out": adamw(r_so, w["ssd_w_out"], m["ssd_w_out"], v["ssd_w_out"], 256, "adamw_ssd_out"),
           "attn_w_out": adamw(r_ao, w["attn_w_out"], m["attn_w_out"], v["attn_w_out"], 128, "adamw_attn_out"),
           "w_mix_out": adamw(r_mix, w["w_mix_out"], m["w_mix_out"], v["w_mix_out"], 128, "adamw_mix"),
           "ffn_w_down": adamw(r_down, w["ffn_w_down"], m["ffn_w_down"], v["ffn_w_down"], 352, "adamw_down")}
    small_names = [n for n in WEIGHTS if n not in big]
    small, loss_row = adamw_small(recv_row, recv_pre, recv_conv, {n: (w[n], m[n], v[n]) for n in small_names})

    outs = [loss_row[0, 0], grad_x[None]]
    for k in range(4):
        for n in WEIGHTS:
            outs.append((big[n][k] if n in big else small[n][k]).reshape(a[n].shape))
    return tuple(outs)
```

```python
import jax
import jax.numpy as jnp
import numpy as np
from jax import lax
from jax.experimental import pallas as pl
from jax.experimental.pallas import tpu as pltpu

f32 = jnp.float32
bf16 = jnp.bfloat16

N_DEV = 8
D = 1024
DI = 2048
NH = 32
HD = 64
NG = 4
GW = DI // NG
NS = 128
CH = 128
CONVD = DI + 2 * NG * NS
SSD_K = 4
AH = 16
AD = 64
KVH = 4
REP = AH // KVH
KVW = KVH * AD
WIN = 128
FF = 2816
FFN_K = 3
EPS = 1e-6
ROPE_THETA = 10000.0
LANES = 128
RG = 16
CW = 256

C_Z, C_GS, C_GA, C_XBC, PM = 0, 2048, 3072, 4096, 7168
IN_SIZES = (DI, CONVD, NH, D, KVW, KVW, D, D)
IN_OFF = tuple(int(v) for v in np.cumsum((0,) + IN_SIZES))
IN_DIM = IN_OFF[-1]

ADAM_LR, ADAM_B1, ADAM_B2, ADAM_EPS, ADAM_WD, ADAM_STEP = 0.001, 0.9, 0.999, 1e-08, 0.01, 10

VMEM_LIMIT = 56 * 1024 * 1024


def _cp(*sem, side_effects=False):
    return pltpu.CompilerParams(dimension_semantics=sem, vmem_limit_bytes=VMEM_LIMIT, has_side_effects=side_effects)


def _dot(a, b, mode="nn"):
    dims = {"nn": (((1,), (0,)), ((), ())), "nt": (((1,), (1,)), ((), ())), "tn": (((0,), (0,)), ((), ()))}[mode]
    return lax.dot_general(a, b, dims, preferred_element_type=f32)


def _split3(v):
    hi = v.astype(bf16)
    r = v - hi.astype(f32)
    mid = r.astype(bf16)
    lo = (r - mid.astype(f32)).astype(bf16)
    return hi, mid, lo


def _dot3_left(m01, v):
    hi, mid, lo = _split3(v)
    return _dot(m01, hi) + _dot(m01, mid) + _dot(m01, lo)


def _dot3_right(v, m01):
    hi, mid, lo = _split3(v)
    return _dot(hi, m01) + _dot(mid, m01) + _dot(lo, m01)


def _dot2_right(v, m01):
    hi = v.astype(bf16)
    lo = (v - hi.astype(f32)).astype(bf16)
    return _dot(hi, m01) + _dot(lo, m01)


def _sigmoid(x):
    return 1.0 / (1.0 + jnp.exp(-x))


def _sigmoid_fast(x):
    return pl.reciprocal(1.0 + jnp.exp(-x), approx=True)


def _peer(k, x, y, c):
    return ((1 - x) if k & 4 else x, (1 - y) if k & 2 else y, (1 - c) if k & 1 else c)


def _xchg_copies(buf_refs, out_refs, send_sems, recv_sems, local_sems, personalised):
    x, y, c = lax.axis_index("x"), lax.axis_index("y"), lax.axis_index("c")
    me = 4 * x + 2 * y + c
    local, remote = [], []
    for b, (buf, out, pers) in enumerate(zip(buf_refs, out_refs, personalised)):
        local.append(pltpu.make_async_copy(buf.at[me] if pers else buf, out.at[me], local_sems.at[b]))
        for k in range(1, N_DEV):
            px, py, pc = _peer(k, x, y, c)
            s = b * (N_DEV - 1) + k - 1
            remote.append(pltpu.make_async_remote_copy(
                src_ref=buf.at[4 * px + 2 * py + pc] if pers else buf, dst_ref=out.at[me],
                send_sem=send_sems.at[s], recv_sem=recv_sems.at[s],
                device_id=(px, py, pc), device_id_type=pl.DeviceIdType.MESH))
    return local, remote


class _Comm:
    def __init__(self, comm):
        self.bufs, self.pers = comm if comm else ((), ())
        self.n = len(self.bufs)

    def in_specs(self):
        return [pl.BlockSpec(memory_space=pl.ANY)] * self.n

    out_specs = in_specs

    def out_shape(self):
        return [jax.ShapeDtypeStruct((N_DEV,) + tuple(b.shape[1:] if p else b.shape), b.dtype)
                for b, p in zip(self.bufs, self.pers)]

    def scratch(self):
        n = self.n
        return [pltpu.SemaphoreType.DMA((n * (N_DEV - 1),)), pltpu.SemaphoreType.DMA((n * (N_DEV - 1),)),
                pltpu.SemaphoreType.DMA((n,))] if n else []

    def split(self, refs, n_in, n_out):
        n = self.n
        ins, outs = refs[:n_in], refs[n_in + n:n_in + n + n_out]
        rest = refs[n_in + n + n_out + n:]
        if not n:
            return ins, outs, rest, None
        copies = _xchg_copies(refs[n_in:n_in + n], refs[n_in + n + n_out:n_in + n + n_out + n], *rest[-3:], self.pers)
        return ins, outs, rest[:-3], copies

    def start(self, copies, first):
        if copies:
            @pl.when(first)
            def _():
                for cp in copies[0] + copies[1]:
                    cp.start()

    def wait(self, copies, last):
        if copies:
            @pl.when(last)
            def _():
                for cp in copies[1]:
                    cp.wait_recv()
                for cp in copies[1]:
                    cp.wait_send()
                for cp in copies[0]:
                    cp.wait()


def exchange(bufs, personalised, name):
    cm = _Comm((bufs, personalised))

    def body(*refs):
        _, _, _, copies = cm.split(refs, 0, 0)
        cm.start(copies, True)
        cm.wait(copies, True)

    return pl.pallas_call(
        body, name=name, in_specs=cm.in_specs(), out_specs=cm.out_specs(), out_shape=cm.out_shape(),
        scratch_shapes=cm.scratch(), compiler_params=pltpu.CompilerParams(has_side_effects=True),
    )(*bufs)


class _TwoLevelGather:
    def __init__(self, bufs):
        self.bufs = list(bufs)
        self.n = len(self.bufs)

    def in_specs(self):
        return [pl.BlockSpec(memory_space=pl.ANY)] * self.n

    out_specs = in_specs

    def out_shape(self):
        return [jax.ShapeDtypeStruct((N_DEV,) + tuple(b.shape), b.dtype) for b in self.bufs]

    def scratch(self):
        per = N_DEV - 1
        return [pltpu.SemaphoreType.DMA((self.n * per,)), pltpu.SemaphoreType.DMA((self.n * per,)),
                pltpu.SemaphoreType.DMA((self.n,))]

    def bind(self, ins, outs, send_sems, recv_sems, local_sems):
        n, per = self.n, N_DEV - 1
        x, y, c = lax.axis_index("x"), lax.axis_index("y"), lax.axis_index("c")
        me, sibling = (x, y, c), (x, y, 1 - c)
        chips = [(1 - x, y), (x, 1 - y), (1 - x, 1 - y)]

        def copy(b, k, block, to, src=None):
            dst = outs[b].at[4 * block[0] + 2 * block[1] + block[2]]
            return pltpu.make_async_remote_copy(
                src_ref=dst if src is None else src, dst_ref=dst,
                send_sem=send_sems.at[b * per + k], recv_sem=recv_sems.at[b * per + k],
                device_id=to, device_id_type=pl.DeviceIdType.MESH)

        mine = [pltpu.make_async_copy(ins[b], outs[b].at[4 * x + 2 * y + c], local_sems.at[b]) for b in range(n)]
        first = []
        for b in range(n):
            first.append(copy(b, 0, me, sibling, src=ins[b]))
            first += [copy(b, 1 + j, me, (*chip, c), src=ins[b]) for j, chip in enumerate(chips)]

        def start():
            for cp in mine + first:
                cp.start()

        def finish():
            passed = []
            for j, chip in enumerate(chips):
                for b in range(n):
                    copy(b, 1 + j, (*chip, c), me).wait_recv()
                    passed.append(copy(b, 4 + j, (*chip, c), sibling))
                    passed[-1].start()
            for b in range(n):
                copy(b, 0, sibling, me).wait_recv()
                for j, chip in enumerate(chips):
                    copy(b, 4 + j, (*chip, 1 - c), me).wait_recv()
            for cp in first + passed:
                cp.wait_send()
            for cp in mine:
                cp.wait()

        return start, finish


MM_TILES = (3584, 2176, 2048, 1792, 1408, 1024, 512, 256, 128)
MM_VMEM_BUDGET = 40 * 1024 * 1024


def _mm_tiles(M, N, K, out_bytes):
    cm = [t for t in MM_TILES if M % t == 0]
    cn = [t for t in MM_TILES if N % t == 0]
    ck = [t for t in MM_TILES if K % t == 0]
    best = None
    for bm in cm[:2]:
        for bn in cn:
            for bk in ck:
                need = 4 * (bm * bk + bk * bn) + bm * bn * (4 + 2 * out_bytes)
                if need <= MM_VMEM_BUDGET:
                    score = (bm * bn * bk, bk)
                    if best is None or score > best[0]:
                        best = (score, (bm, bn, bk))
    return best[1]


def mm(a, b, mode, out_dtype, name, comm=None):
    if mode == "nn":
        (M, K), (_, N) = a.shape, b.shape
    elif mode == "nt":
        (M, K), (N, _) = a.shape, b.shape
    else:
        (K, M), (_, N) = a.shape, b.shape
    bm, bn, bk = _mm_tiles(M, N, K, jnp.dtype(out_dtype).itemsize)
    gm, gn, nk = M // bm, N // bn, K // bk
    cm = _Comm(comm)

    def body(*refs):
        (a_ref, b_ref), (o_ref,), scr, copies = cm.split(refs, 2, 1)
        i, j, k = pl.program_id(0), pl.program_id(1), pl.program_id(2)
        cm.start(copies, jnp.logical_and(jnp.logical_and(i == 0, j == 0), k == 0))
        p = _dot(a_ref[...], b_ref[...], mode)
        if nk == 1:
            o_ref[...] = p.astype(o_ref.dtype)
        else:
            acc_ref = scr[0]

            @pl.when(k == 0)
            def _():
                acc_ref[...] = p

            @pl.when(k > 0)
            def _():
                acc_ref[...] += p

            @pl.when(k == nk - 1)
            def _():
                o_ref[...] = acc_ref[...].astype(o_ref.dtype)

        cm.wait(copies, jnp.logical_and(jnp.logical_and(i == gm - 1, j == gn - 1), k == nk - 1))

    if mode == "nn":
        a_spec = pl.BlockSpec((bm, bk), lambda i, j, k: (i, k))
        b_spec = pl.BlockSpec((bk, bn), lambda i, j, k: (k, j))
    elif mode == "nt":
        a_spec = pl.BlockSpec((bm, bk), lambda i, j, k: (i, k))
        b_spec = pl.BlockSpec((bn, bk), lambda i, j, k: (j, k))
    else:
        a_spec = pl.BlockSpec((bk, bm), lambda i, j, k: (k, i))
        b_spec = pl.BlockSpec((bk, bn), lambda i, j, k: (k, j))
    sem = ("arbitrary",) * 3 if cm.n else ("parallel", "parallel", "arbitrary")
    res = pl.pallas_call(
        body, name=name, grid=(gm, gn, nk),
        in_specs=[a_spec, b_spec] + cm.in_specs(),
        out_specs=[pl.BlockSpec((bm, bn), lambda i, j, k: (i, j))] + cm.out_specs(),
        out_shape=[jax.ShapeDtypeStruct((M, N), out_dtype)] + cm.out_shape(),
        scratch_shapes=([pltpu.VMEM((bm, bn), f32)] if nk > 1 else []) + cm.scratch(),
        compiler_params=_cp(*sem, side_effects=bool(cm.n)),
    )(a, b, *cm.bufs)
    return (res[0], res[1:]) if cm.n else res[0]


def _groups(ts, fn, carry=None, reverse=False, unroll=8, rg=RG):
    n = ts // rg
    if n == 1:
        return fn(0, carry)
    unroll = min(unroll, n)
    span = rg * unroll

    def body(g, c):
        r0 = pl.multiple_of((n // unroll - 1 - g if reverse else g) * span, span)
        for u in (range(unroll - 1, -1, -1) if reverse else range(unroll)):
            c = fn(pl.multiple_of(r0 + u * rg, rg), c)
        return c

    return lax.fori_loop(0, n // unroll, body, carry)


def _rms(x):
    return lax.rsqrt(jnp.mean(x * x, axis=-1, keepdims=True) + EPS)


def _rms_bwd(x, r, dn):
    n = x * r
    return r * (dn - n * jnp.mean(dn * n, axis=-1, keepdims=True))


NRG = 256


def _fold(x):
    return jnp.sum(x.reshape(x.shape[0] // 8, 8, x.shape[1]), axis=0)


def _flush(acc_ref, out_ref, row):
    out_ref[row:row + 1, :] = jnp.sum(acc_ref[...], axis=0, keepdims=True)


def prenorm_fwd(x, w, pos_row, ts, gather):
    S = x.shape[0]
    nt = S // ts
    tg = _TwoLevelGather(gather)
    n = tg.n
    half = AD // 2
    inv = ROPE_THETA ** (-jnp.arange(half, dtype=f32) * 2.0 / AD)
    inv_col = jnp.tile(inv, 2)[:, None]

    def body(x_ref, w_ref, p_ref, inv_ref, *refs):
        u_ref, cos_ref, sin_ref = refs[n:n + 3]
        start, finish = tg.bind(refs[:n], refs[n + 3:2 * n + 3], *refs[2 * n + 3:])
        i = pl.program_id(0)
        pl.when(i == 0)(start)
        wv = w_ref[...]

        def grp(r0, _):
            xv = x_ref[pl.ds(r0, NRG), :]
            u_ref[pl.ds(r0, NRG), :] = (xv * _rms(xv) * wv).astype(bf16)

        _groups(ts, grp, rg=NRG)
        ang = inv_ref[...] * p_ref[...].astype(f32)
        row = lax.broadcasted_iota(jnp.int32, ang.shape, 0)
        cos_ref[...] = jnp.cos(ang)
        sin_ref[...] = jnp.where(row < half, -1.0, 1.0) * jnp.sin(ang)
        pl.when(i == nt - 1)(finish)

    tab = pl.BlockSpec((AD, ts), lambda i: (0, i))
    res = pl.pallas_call(
        body, name="prenorm_fwd", grid=(nt,),
        in_specs=[pl.BlockSpec((ts, D), lambda i: (i, 0)), pl.BlockSpec((1, D), lambda i: (0, 0)),
                  pl.BlockSpec((1, ts), lambda i: (0, i)), pl.BlockSpec((AD, 1), lambda i: (0, 0))] + tg.in_specs(),
        out_specs=[pl.BlockSpec((ts, D), lambda i: (i, 0)), tab, tab] + tg.out_specs(),
        out_shape=[jax.ShapeDtypeStruct((S, D), bf16), jax.ShapeDtypeStruct((AD, S), f32),
                   jax.ShapeDtypeStruct((AD, S), f32)] + tg.out_shape(),
        scratch_shapes=tg.scratch(),
        compiler_params=_cp("arbitrary", side_effects=True),
    )(x, w, pos_row, inv_col, *tg.bufs)
    return res[0], res[1], res[2], res[3:]


def prenorm_bwd(x, w, dus, dx1, ts, comm=None):
    S = x.shape[0]
    nt = S // ts
    nd = len(dus)
    cm = _Comm(comm)

    def body(*refs):
        ins, (gx_ref, gw_ref), (acc_ref,), copies = cm.split(refs, nd + 3, 2)
        x_ref, w_ref = ins[:2]
        du_refs, dx1_ref = ins[2:2 + nd], ins[2 + nd]
        i = pl.program_id(0)
        cm.start(copies, i == 0)
        wv = w_ref[...]

        @pl.when(i == 0)
        def _():
            acc_ref[...] = jnp.zeros_like(acc_ref)
            gw_ref[...] = jnp.zeros_like(gw_ref)

        def grp(r0, _):
            rows = pl.ds(r0, NRG)
            xv = x_ref[rows, :]
            r = _rms(xv)
            du = du_refs[0][rows, :].astype(f32)
            for d_ref in du_refs[1:]:
                du = du + d_ref[rows, :].astype(f32)
            gx_ref[rows, :] = dx1_ref[rows, :] + _rms_bwd(xv, r, du * wv)
            acc_ref[...] += _fold(du * xv * r)

        _groups(ts, grp, rg=NRG)

        @pl.when(i == nt - 1)
        def _():
            _flush(acc_ref, gw_ref, 0)

        cm.wait(copies, i == nt - 1)

    row = pl.BlockSpec((ts, D), lambda i: (i, 0))
    res = pl.pallas_call(
        body, name="prenorm_bwd", grid=(nt,),
        in_specs=[row, pl.BlockSpec((1, D), lambda i: (0, 0))] + [row] * (nd + 1) + cm.in_specs(),
        out_specs=[row, pl.BlockSpec((8, D), lambda i: (0, 0))] + cm.out_specs(),
        out_shape=[jax.ShapeDtypeStruct((S, D), f32), jax.ShapeDtypeStruct((8, D), f32)] + cm.out_shape(),
        scratch_shapes=[pltpu.VMEM((8, D), f32)] + cm.scratch(),
        compiler_params=_cp("arbitrary", side_effects=bool(cm.n)),
    )(x, w, *dus, dx1, *cm.bufs)
    return res[0], res[1], res[2:]


def post_fwd(x, mo, w_post, w_pre2, ts):
    S = x.shape[0]

    def body(x_ref, mo_ref, wp_ref, w2_ref, x1_ref, h_ref):
        wp, w2 = wp_ref[...], w2_ref[...]

        def grp(r0, _):
            rows = pl.ds(r0, NRG)
            mv = mo_ref[rows, :].astype(f32)
            x1 = x_ref[rows, :] + mv * _rms(mv) * wp
            x1_ref[rows, :] = x1
            h_ref[rows, :] = (x1 * _rms(x1) * w2).astype(bf16)

        _groups(ts, grp, rg=NRG)

    row = pl.BlockSpec((ts, D), lambda i: (i, 0))
    par = pl.BlockSpec((1, D), lambda i: (0, 0))
    return pl.pallas_call(
        body, name="post_fwd", grid=(S // ts,),
        in_specs=[row, row, par, par], out_specs=[row, row],
        out_shape=[jax.ShapeDtypeStruct((S, D), f32), jax.ShapeDtypeStruct((S, D), bf16)],
        compiler_params=_cp("parallel"),
    )(x, mo, w_post, w_pre2)


def post_bwd(dout, dh, x1, mo, w_post, w_pre2, ts):
    S = x1.shape[0]
    nt = S // ts

    def body(dout_ref, dh_ref, x1_ref, mo_ref, wp_ref, w2_ref, dx1_ref, dmo_ref, gw_ref, acc2_ref, accp_ref):
        i = pl.program_id(0)
        wp, w2 = wp_ref[...], w2_ref[...]

        @pl.when(i == 0)
        def _():
            acc2_ref[...] = jnp.zeros_like(acc2_ref)
            accp_ref[...] = jnp.zeros_like(accp_ref)
            gw_ref[...] = jnp.zeros_like(gw_ref)

        def grp(r0, _):
            rows = pl.ds(r0, NRG)
            x1 = x1_ref[rows, :]
            r1 = _rms(x1)
            dh = dh_ref[rows, :].astype(f32)
            dx1 = dout_ref[rows, :] + _rms_bwd(x1, r1, dh * w2)
            dx1_ref[rows, :] = dx1
            acc2_ref[...] += _fold(dh * x1 * r1)
            mv = mo_ref[rows, :].astype(f32)
            rm = _rms(mv)
            dmo_ref[rows, :] = _rms_bwd(mv, rm, dx1 * wp).astype(bf16)
            accp_ref[...] += _fold(dx1 * mv * rm)

        _groups(ts, grp, rg=NRG)

        @pl.when(i == nt - 1)
        def _():
            _flush(acc2_ref, gw_ref, 0)
            _flush(accp_ref, gw_ref, 1)

    row = pl.BlockSpec((ts, D), lambda i: (i, 0))
    par = pl.BlockSpec((1, D), lambda i: (0, 0))
    return pl.pallas_call(
        body, name="post_bwd", grid=(nt,),
        in_specs=[row, row, row, row, par, par],
        out_specs=[row, row, pl.BlockSpec((8, D), lambda i: (0, 0))],
        out_shape=[jax.ShapeDtypeStruct((S, D), f32), jax.ShapeDtypeStruct((S, D), bf16),
                   jax.ShapeDtypeStruct((8, D), f32)],
        scratch_shapes=[pltpu.VMEM((8, D), f32), pltpu.VMEM((8, D), f32)],
        compiler_params=_cp("arbitrary"),
    )(dout, dh, x1, mo, w_post, w_pre2)


def loss_head(x1, ff, target, w, ts):
    S = x1.shape[0]
    nt = S // ts

    def body(x1_ref, ff_ref, t_ref, w_ref, loss_ref, dout_ref, dff_ref, gw_ref, accw_ref, accl_ref):
        i = pl.program_id(0)
        wv = w_ref[...]

        @pl.when(i == 0)
        def _():
            accw_ref[...] = jnp.zeros_like(accw_ref)
            accl_ref[...] = jnp.zeros_like(accl_ref)
            gw_ref[...] = jnp.zeros_like(gw_ref)

        def grp(r0, _):
            rows = pl.ds(r0, NRG)
            fv = ff_ref[rows, :].astype(f32)
            r = _rms(fv)
            n = fv * r
            e = x1_ref[rows, :] + n * wv - t_ref[rows, :]
            dout = e * (1.0 / D)
            dout_ref[rows, :] = dout
            dff_ref[rows, :] = _rms_bwd(fv, r, dout * wv).astype(bf16)
            accw_ref[...] += _fold(dout * n)
            accl_ref[...] += _fold(e * e)

        _groups(ts, grp, rg=NRG)

        @pl.when(i == nt - 1)
        def _():
            _flush(accw_ref, gw_ref, 0)
            tot = jnp.sum(jnp.sum(accl_ref[...], axis=1, keepdims=True), axis=0, keepdims=True) * (0.5 / D)
            loss_ref[...] = jnp.broadcast_to(tot, loss_ref.shape)

    row = pl.BlockSpec((ts, D), lambda i: (i, 0))
    return pl.pallas_call(
        body, name="loss_head", grid=(nt,),
        in_specs=[row, row, row, pl.BlockSpec((1, D), lambda i: (0, 0))],
        out_specs=[pl.BlockSpec((8, LANES), lambda i: (0, 0)), row, row, pl.BlockSpec((8, D), lambda i: (0, 0))],
        out_shape=[jax.ShapeDtypeStruct((8, LANES), f32), jax.ShapeDtypeStruct((S, D), f32),
                   jax.ShapeDtypeStruct((S, D), bf16), jax.ShapeDtypeStruct((8, D), f32)],
        scratch_shapes=[pltpu.VMEM((8, D), f32), pltpu.VMEM((8, D), f32)],
        compiler_params=_cp("arbitrary"),
    )(x1, ff, target, w)


def _taps(w_ref, cs, K):
    return [jnp.broadcast_to(w_ref[k:k + 1, cs], (8, CW)) for k in range(K)]


def _down(before, cur, s, sub):
    return jnp.where(sub < s, pltpu.roll(before, s, 0), pltpu.roll(cur, s, 0))


def _up(cur, after, s, sub):
    return jnp.where(sub < 8 - s, pltpu.roll(cur, 8 - s, 0), pltpu.roll(after, 8 - s, 0))


def _conv_group(p, a, b, taps, bias, K, sub):
    ya, yb = bias, bias
    for k in range(K):
        s = K - 1 - k
        xa, xb = (a, b) if s == 0 else (_down(p, a, s, sub), _down(a, b, s, sub))
        ya = ya + taps[k] * xa
        yb = yb + taps[k] * xb
    return ya, yb


def _prev8_map(ts, cb):
    return lambda i, j: (jnp.maximum(i * (ts // 8) - 1, 0), cb + j)


def ssdconv_fwd(proj, w8, b, ts):
    S = proj.shape[0]
    bw = 1024
    cb = C_XBC // bw

    def body(cur_ref, prev_ref, w_ref, b_ref, o_ref, c_ref):
        first = pl.program_id(0) == 0
        sub = lax.broadcasted_iota(jnp.int32, (8, CW), 0)
        for c0 in range(0, bw, CW):
            cs = slice(c0, c0 + CW)
            taps = _taps(w_ref, cs, SSD_K)
            bias = jnp.broadcast_to(b_ref[:, cs], (8, CW))

            def grp(r0, p, cs=cs, taps=taps, bias=bias):
                rows = pl.ds(r0, RG)
                xv = cur_ref[rows, cs].astype(f32)
                ya, yb = _conv_group(p, xv[0:8], xv[8:16], taps, bias, SSD_K, sub)
                y = jnp.concatenate([ya, yb], axis=0)
                c_ref[rows, cs] = y.astype(bf16)
                o_ref[rows, cs] = (y * _sigmoid_fast(y)).astype(bf16)
                return xv[8:16]

            _groups(ts, grp, jnp.where(first, 0.0, prev_ref[:, cs].astype(f32)))

    o = jax.ShapeDtypeStruct((S, CONVD), bf16)
    blk = pl.BlockSpec((ts, bw), lambda i, j: (i, j))
    return pl.pallas_call(
        body, name="ssdconv_fwd", grid=(S // ts, CONVD // bw),
        in_specs=[pl.BlockSpec((ts, bw), lambda i, j: (i, cb + j)),
                  pl.BlockSpec((8, bw), _prev8_map(ts, cb)),
                  pl.BlockSpec((8, bw), lambda i, j: (0, j)),
                  pl.BlockSpec((1, bw), lambda i, j: (0, j))],
        out_specs=[blk, blk], out_shape=[o, o],
        compiler_params=_cp("parallel", "parallel"),
    )(proj, proj, w8, b)


def _gelu_tanh(x):
    c = 0.7978845608028654
    t = jnp.tanh(c * (x + 0.044715 * x * x * x))
    return 0.5 * x * (1.0 + t), t


def ffnact_fwd(up, w8, b, ts):
    S = up.shape[0]

    def body(g_ref, gp_ref, v_ref, vp_ref, wg_ref, wv_ref, bg_ref, bv_ref, o_ref, gc_ref, vc_ref):
        first = pl.program_id(0) == 0
        sub = lax.broadcasted_iota(jnp.int32, (8, CW), 0)
        for c0 in range(0, FF, CW):
            cs = slice(c0, c0 + CW)
            tg, tv = _taps(wg_ref, cs, FFN_K), _taps(wv_ref, cs, FFN_K)
            bg = jnp.broadcast_to(bg_ref[:, cs], (8, CW))
            bv = jnp.broadcast_to(bv_ref[:, cs], (8, CW))

            def grp(r0, carry, cs=cs, tg=tg, tv=tv, bg=bg, bv=bv):
                pg, pv = carry
                rows = pl.ds(r0, RG)
                gx = g_ref[rows, cs].astype(f32)
                vx = v_ref[rows, cs].astype(f32)
                g = jnp.concatenate(_conv_group(pg, gx[0:8], gx[8:16], tg, bg, FFN_K, sub), axis=0)
                v = jnp.concatenate(_conv_group(pv, vx[0:8], vx[8:16], tv, bv, FFN_K, sub), axis=0)
                gc_ref[rows, cs] = g.astype(bf16)
                vc_ref[rows, cs] = v.astype(bf16)
                o_ref[rows, cs] = (_gelu_tanh(g)[0] * v).astype(bf16)
                return gx[8:16], vx[8:16]

            _groups(ts, grp, (jnp.where(first, 0.0, gp_ref[:, cs].astype(f32)),
                              jnp.where(first, 0.0, vp_ref[:, cs].astype(f32))))

    o = jax.ShapeDtypeStruct((S, FF), bf16)
    blk = pl.BlockSpec((ts, FF), lambda i: (i, 0))
    prev = lambda cb: pl.BlockSpec((8, FF), lambda i: (jnp.maximum(i * (ts // 8) - 1, 0), cb))
    return pl.pallas_call(
        body, name="ffnact_fwd", grid=(S // ts,),
        in_specs=[blk, prev(0), pl.BlockSpec((ts, FF), lambda i: (i, 1)), prev(1),
                  pl.BlockSpec((8, FF), lambda i: (0, 0)), pl.BlockSpec((8, FF), lambda i: (0, 1)),
                  pl.BlockSpec((1, FF), lambda i: (0, 0)), pl.BlockSpec((1, FF), lambda i: (0, 1))],
        out_specs=[blk, blk, blk], out_shape=[o, o, o],
        compiler_params=_cp("parallel"),
    )(up, up, up, up, w8, w8, b, b)


def ffnact_bwd(dact, gc, vc, ts):
    S = dact.shape[0]

    def body(d_ref, g_ref, v_ref, dg_ref, dv_ref):
        c = 0.7978845608028654
        for c0 in range(0, FF, CW):
            cs = slice(c0, c0 + CW)

            def grp(r0, _, cs=cs):
                rows = pl.ds(r0, RG)
                d = d_ref[rows, cs].astype(f32)
                g = g_ref[rows, cs].astype(f32)
                ge, t = _gelu_tanh(g)
                dgelu = 0.5 * (1.0 + t) + 0.5 * g * (1.0 - t * t) * c * (1.0 + 3.0 * 0.044715 * g * g)
                dg_ref[rows, cs] = (d * v_ref[rows, cs].astype(f32) * dgelu).astype(bf16)
                dv_ref[rows, cs] = (d * ge).astype(bf16)

            _groups(ts, grp)

    o = jax.ShapeDtypeStruct((S, FF), bf16)
    blk = pl.BlockSpec((ts, FF), lambda i: (i, 0))
    return pl.pallas_call(
        body, name="ffnact_bwd", grid=(S // ts,),
        in_specs=[blk, blk, blk], out_specs=[blk, blk], out_shape=[o, o],
        compiler_params=_cp("parallel"),
    )(dact, gc, vc)


def dwconv_bwd(dy, x, xcb, w8, wcb, K, bw, ts, name, act_c=None, into=None, ocb=0, out_cols=None):
    S, C = dy.shape
    nr = S // ts
    out_cols = out_cols or C
    n_act = 0 if act_c is None else 2

    def body(*refs):
        dy_ref, dyn_ref = refs[0:2]
        c_ref, cn_ref = (refs[2:4] if n_act else (None, None))
        x_ref, xp_ref, w_ref = refs[2 + n_act:5 + n_act]
        dx_ref, dw_ref, sd_ref = refs[-3:]
        i = pl.program_id(1)
        first, last = i == 0, i == nr - 1
        sub = lax.broadcasted_iota(jnp.int32, (8, CW), 0)

        def grad_y(d, c):
            if c is None:
                return d.astype(f32)
            cv = c.astype(f32)
            s = _sigmoid_fast(cv)
            return d.astype(f32) * s * (1.0 + cv * (1.0 - s))

        @pl.when(first)
        def _():
            dw_ref[...] = jnp.zeros_like(dw_ref)

        for c0 in range(0, bw, CW):
            cs = slice(c0, c0 + CW)
            taps = _taps(w_ref, cs, K)
            zero = jnp.zeros((8, CW), f32)

            def fwd(r0, carry, cs=cs):
                p, accs, accb = carry
                rows = pl.ds(r0, RG)
                g = grad_y(dy_ref[rows, cs], c_ref[rows, cs] if n_act else None)
                sd_ref[rows, cs] = g
                xv = x_ref[rows, cs].astype(f32)
                a, b = xv[0:8], xv[8:16]
                ga, gb = g[0:8], g[8:16]
                new = []
                for k in range(K):
                    s = K - 1 - k
                    xa, xb = (a, b) if s == 0 else (_down(p, a, s, sub), _down(a, b, s, sub))
                    new.append(accs[k] + ga * xa + gb * xb)
                return b, tuple(new), accb + ga + gb

            _, accs, accb = _groups(ts, fwd, (jnp.where(first, 0.0, xp_ref[:, cs].astype(f32)), (zero,) * K, zero))
            for k in range(K):
                dw_ref[k:k + 1, cs] += jnp.sum(accs[k], axis=0, keepdims=True)
            dw_ref[7:8, cs] += jnp.sum(accb, axis=0, keepdims=True)

            def bwd(r0, after, cs=cs, taps=taps):
                rows = pl.ds(r0, RG)
                g = sd_ref[rows, cs]
                a, b = g[0:8], g[8:16]
                da, db = zero, zero
                for k in range(K):
                    s = K - 1 - k
                    ua, ub = (a, b) if s == 0 else (_up(a, b, s, sub), _up(b, after, s, sub))
                    da = da + taps[k] * ua
                    db = db + taps[k] * ub
                dx_ref[rows, cs] = jnp.concatenate([da, db], axis=0).astype(bf16)
                return a

            halo = grad_y(dyn_ref[:, cs], cn_ref[:, cs] if n_act else None)
            _groups(ts, bwd, jnp.where(last, 0.0, halo), reverse=True)

    nxt = lambda j, i: (jnp.minimum((i + 1) * (ts // 8), S // 8 - 1), j)
    tile = pl.BlockSpec((ts, bw), lambda j, i: (i, j))
    acts = [] if act_c is None else [act_c, act_c]
    extra = [] if into is None else [into]
    n_in = 5 + n_act
    return pl.pallas_call(
        body, name=name, grid=(C // bw, nr),
        in_specs=[tile, pl.BlockSpec((8, bw), nxt)] + ([tile, pl.BlockSpec((8, bw), nxt)] if n_act else []) + [
            pl.BlockSpec((ts, bw), lambda j, i: (i, xcb + j)),
            pl.BlockSpec((8, bw), lambda j, i: (jnp.maximum(i * (ts // 8) - 1, 0), xcb + j)),
            pl.BlockSpec((8, bw), lambda j, i: (0, wcb + j))] + [pl.BlockSpec(memory_space=pl.ANY)] * len(extra),
        out_specs=[pl.BlockSpec((ts, bw), lambda j, i: (i, ocb + j)), pl.BlockSpec((8, bw), lambda j, i: (0, j))],
        out_shape=[jax.ShapeDtypeStruct((S, out_cols), bf16), jax.ShapeDtypeStruct((8, C), f32)],
        scratch_shapes=[pltpu.VMEM((ts, bw), f32)],
        input_output_aliases={n_in: 0} if extra else {},
        compiler_params=_cp("parallel", "arbitrary"),
    )(dy, dy, *acts, x, x, w8, *extra)


def gnorm_fwd(y, proj, w, ts):
    S = y.shape[0]

    def body(y_ref, z_ref, w_ref, o_ref):
        for k in range(NG):
            sl = slice(k * GW, (k + 1) * GW)
            wv = w_ref[:, sl]

            def grp(r0, _, sl=sl, wv=wv):
                rows = pl.ds(r0, NRG)
                z = z_ref[rows, sl].astype(f32)
                g = y_ref[rows, sl].astype(f32) * z * _sigmoid_fast(z)
                o_ref[rows, sl] = (g * _rms(g) * wv).astype(bf16)

            _groups(ts, grp, rg=NRG)

    row = pl.BlockSpec((ts, DI), lambda i: (i, 0))
    return pl.pallas_call(
        body, name="gnorm_fwd", grid=(S // ts,),
        in_specs=[row, row, pl.BlockSpec((1, DI), lambda i: (0, 0))],
        out_specs=row, out_shape=jax.ShapeDtypeStruct((S, DI), bf16),
        compiler_params=_cp("parallel"),
    )(y, proj, w)


def gnorm_bwd(dyn, y, proj, w, dproj, ts):
    S = y.shape[0]
    nt = S // ts

    def body(d_ref, y_ref, z_ref, w_ref, _, dy_ref, dz_ref, gw_ref, acc_ref):
        i = pl.program_id(0)

        @pl.when(i == 0)
        def _():
            acc_ref[...] = jnp.zeros_like(acc_ref)
            gw_ref[...] = jnp.zeros_like(gw_ref)

        for k in range(NG):
            sl = slice(k * GW, (k + 1) * GW)
            wv = w_ref[:, sl]

            def grp(r0, _, sl=sl, wv=wv):
                rows = pl.ds(r0, NRG)
                z = z_ref[rows, sl].astype(f32)
                yv = y_ref[rows, sl].astype(f32)
                s = _sigmoid_fast(z)
                sz = z * s
                g = yv * sz
                r = _rms(g)
                d = d_ref[rows, sl].astype(f32)
                acc_ref[:, sl] += _fold(d * g * r)
                dg = _rms_bwd(g, r, d * wv)
                dy_ref[rows, sl] = (dg * sz).astype(bf16)
                dz_ref[rows, sl] = (dg * yv * s * (1.0 + z * (1.0 - s))).astype(bf16)

            _groups(ts, grp, rg=NRG)

        @pl.when(i == nt - 1)
        def _():
            _flush(acc_ref, gw_ref, 0)

    row = pl.BlockSpec((ts, DI), lambda i: (i, 0))
    return pl.pallas_call(
        body, name="gnorm_bwd", grid=(nt,),
        in_specs=[row, row, row, pl.BlockSpec((1, DI), lambda i: (0, 0)), pl.BlockSpec(memory_space=pl.ANY)],
        out_specs=[row, row, pl.BlockSpec((8, DI), lambda i: (0, 0))],
        out_shape=[jax.ShapeDtypeStruct((S, DI), bf16), jax.ShapeDtypeStruct(dproj.shape, bf16),
                   jax.ShapeDtypeStruct((8, DI), f32)],
        scratch_shapes=[pltpu.VMEM((8, DI), f32)],
        input_output_aliases={4: 1},
        compiler_params=_cp("arbitrary"),
    )(dyn, y, proj, w, dproj)


def merge_fwd(proj, ys, ya, ts):
    S = ys.shape[0]

    def body(gs_ref, ga_ref, ys_ref, ya_ref, o_ref):
        for c0 in range(0, D, CW):
            cs = slice(c0, c0 + CW)

            def grp(r0, _, cs=cs):
                rows = pl.ds(r0, NRG)
                o_ref[rows, cs] = (_sigmoid_fast(gs_ref[rows, cs].astype(f32)) * ys_ref[rows, cs].astype(f32)
                                   + _sigmoid_fast(ga_ref[rows, cs].astype(f32)) * ya_ref[rows, cs].astype(f32)
                                   ).astype(bf16)

            _groups(ts, grp, rg=NRG)

    row = pl.BlockSpec((ts, D), lambda i: (i, 0))
    return pl.pallas_call(
        body, name="merge_fwd", grid=(S // ts,),
        in_specs=[pl.BlockSpec((ts, D), lambda i: (i, C_GS // D)), pl.BlockSpec((ts, D), lambda i: (i, C_GA // D)), row, row],
        out_specs=row, out_shape=jax.ShapeDtypeStruct((S, D), bf16),
        compiler_params=_cp("parallel"),
    )(proj, proj, ys, ya)


def merge_bwd(dm, proj, ys, ya, ts):
    S = ys.shape[0]

    def body(d_ref, gs_ref, ga_ref, ys_ref, ya_ref, dys_ref, dya_ref, dg_ref):
        for c0 in range(0, D, CW):
            cs = slice(c0, c0 + CW)

            def grp(r0, _, c0=c0, cs=cs):
                rows = pl.ds(r0, NRG)
                d = d_ref[rows, cs].astype(f32)
                ss = _sigmoid_fast(gs_ref[rows, cs].astype(f32))
                sa = _sigmoid_fast(ga_ref[rows, cs].astype(f32))
                dys_ref[rows, cs] = (d * ss).astype(bf16)
                dya_ref[rows, cs] = (d * sa).astype(bf16)
                dg_ref[rows, cs] = (d * ys_ref[rows, cs].astype(f32) * ss * (1.0 - ss)).astype(bf16)
                dg_ref[rows, D + c0:D + c0 + CW] = (d * ya_ref[rows, cs].astype(f32) * sa * (1.0 - sa)).astype(bf16)

            _groups(ts, grp, rg=NRG)

    row = pl.BlockSpec((ts, D), lambda i: (i, 0))
    o = jax.ShapeDtypeStruct((S, D), bf16)
    return pl.pallas_call(
        body, name="merge_bwd", grid=(S // ts,),
        in_specs=[row, pl.BlockSpec((ts, D), lambda i: (i, C_GS // D)), pl.BlockSpec((ts, D), lambda i: (i, C_GA // D)), row, row],
        out_specs=[row, row, pl.BlockSpec((ts, 2 * D), lambda i: (i, C_GS // (2 * D)))],
        out_shape=[o, o, jax.ShapeDtypeStruct((S, PM), bf16)],
        compiler_params=_cp("parallel"),
    )(dm, proj, proj, ys, ya)


def _ssd_consts():
    h = lax.broadcasted_iota(jnp.int32, (LANES, DI), 0)
    c = lax.broadcasted_iota(jnp.int32, (LANES, DI), 1)
    expand = (c // HD == h).astype(bf16)
    r = lax.broadcasted_iota(jnp.int32, (CH, CH), 0)
    cc = lax.broadcasted_iota(jnp.int32, (CH, CH), 1)
    tril = (cc <= r).astype(bf16)
    triu = (cc >= r).astype(bf16)
    return expand, expand.T, tril, triu


def _ssd_common(xbc_ref, dtr_ref, bias_ref, alog_ref, tril_ref, expand_ref=None, saved=None):
    dtr = dtr_ref[...] + bias_ref[...]
    dt = jnp.maximum(dtr, 0.0) + jnp.log1p(jnp.exp(-jnp.abs(dtr)))
    a = -jnp.exp(alog_ref[...])
    acs = _dot3_left(tril_ref[...], dt * a)
    if saved is None:
        acsx = _dot3_right(acs, expand_ref[...])
        dtx = _dot3_right(dt, expand_ref[...])
    else:
        acsx, dtx = saved[0][...], saved[1][...]
    x = xbc_ref[:, 0:DI].astype(f32)
    xdt = x * dtx
    e = jnp.exp(acsx)
    dsx = jnp.exp(acsx[CH - 1:CH, :] - acsx)
    return dtr, dt, a, acs, acsx, dtx, x, xdt, e, dsx


def _head_halves():
    first = lax.broadcasted_iota(jnp.int32, (CH, LANES), 1) < HD
    return first, jnp.logical_not(first)


def _ssd_lmat(acs, acs_t, hh, causal):
    seg = acs[:, hh:hh + 1] - acs_t[hh:hh + 1, :]
    return jnp.where(causal, jnp.exp(jnp.minimum(seg, 0.0)), 0.0)


def ssd_fwd(xbc, dtr, bias, alog, dx_row, comm=None):
    S = xbc.shape[0]
    nc = S // CH
    expand, _, tril, _ = _ssd_consts()
    cm = _Comm(comm)

    def body(*refs):
        ins, (y_ref, hp_ref, ax_ref, dtx_ref), (h_ref, yd_ref), copies = cm.split(refs, 7, 4)
        xbc_ref, dtr_ref, bias_ref, alog_ref, dxr_ref, expand_ref, tril_ref = ins
        c = pl.program_id(0)
        cm.start(copies, c == 0)

        @pl.when(c == 0)
        def _():
            h_ref[...] = jnp.zeros_like(h_ref)

        _, _, _, acs, acsx, dtx, x, xdt, e, dsx = _ssd_common(xbc_ref, dtr_ref, bias_ref, alog_ref, tril_ref,
                                                              expand_ref=expand_ref)
        ax_ref[...] = acsx
        dtx_ref[...] = dtx
        acs_t = acs.T
        xb = xdt.astype(bf16)
        xd = (xdt * dsx).astype(bf16)
        causal = tril_ref[...] > 0
        halves = _head_halves()
        for g in range(NG):
            gs = slice(g * GW, (g + 1) * GW)
            bg = xbc_ref[:, DI + g * NS:DI + (g + 1) * NS]
            cg = xbc_ref[:, DI + NG * NS + g * NS:DI + NG * NS + (g + 1) * NS]
            cb = _dot(cg, bg, "nt")
            hp = h_ref[g]
            hpb = hp.astype(bf16)
            hp_ref[0, g] = hpb
            yd_ref[:, gs] = _dot(cg, hpb) * e[:, gs]
            h_ref[g] = hp * e[CH - 1:CH, gs] + _dot(bg, xd[:, gs], "tn")
            for k in range(NH // NG // 2):
                hh = g * (NH // NG) + 2 * k
                ps = slice(hh * HD, (hh + 2) * HD)
                xp = xb[:, ps]
                acc = None
                for o in range(2):
                    m = (cb * _ssd_lmat(acs, acs_t, hh + o, causal)).astype(bf16)
                    part = _dot(m, jnp.where(halves[o], xp, jnp.zeros_like(xp)))
                    acc = part if acc is None else acc + part
                yd_ref[:, ps] += acc
        y_ref[...] = (yd_ref[...] + dxr_ref[...] * x).astype(bf16)
        cm.wait(copies, c == nc - 1)

    par = lambda shape: pl.BlockSpec(shape, lambda c: (0,) * len(shape))
    res = pl.pallas_call(
        body, name="ssd_fwd", grid=(nc,),
        in_specs=[pl.BlockSpec((CH, CONVD), lambda c: (c, 0)), pl.BlockSpec((CH, LANES), lambda c: (c, 0)),
                  par((1, LANES)), par((1, LANES)), par((1, DI)), par((LANES, DI)), par((CH, CH))] + cm.in_specs(),
        out_specs=[pl.BlockSpec((CH, DI), lambda c: (c, 0)), pl.BlockSpec((1, NG, NS, GW), lambda c: (c, 0, 0, 0)),
                   pl.BlockSpec((CH, DI), lambda c: (c, 0)), pl.BlockSpec((CH, DI), lambda c: (c, 0))] + cm.out_specs(),
        out_shape=[jax.ShapeDtypeStruct((S, DI), bf16), jax.ShapeDtypeStruct((nc, NG, NS, GW), bf16),
                   jax.ShapeDtypeStruct((S, DI), f32), jax.ShapeDtypeStruct((S, DI), f32)] + cm.out_shape(),
        scratch_shapes=[pltpu.VMEM((NG, NS, GW), f32), pltpu.VMEM((CH, DI), f32)] + cm.scratch(),
        compiler_params=_cp("arbitrary", side_effects=bool(cm.n)),
    )(xbc, dtr, bias, alog, dx_row, expand, tril, *cm.bufs)
    return res[0], res[1], (res[2], res[3]), res[4:]


def ssd_bwd(xbc, dtr, dy, hprev, saved, bias, alog, dx_row, comm=None):
    S = xbc.shape[0]
    nc = S // CH
    _, expand_t, tril, triu = _ssd_consts()
    cm = _Comm(comm)

    def body(*refs):
        ins, outs, scr, copies = cm.split(refs, 12, 3)
        xbc_ref, dtr_ref, dy_ref, hp_ref, ax_ref, dtx_ref, bias_ref, alog_ref, dxr_ref, expt_ref, tril_ref, triu_ref = ins
        dxbc_ref, ddtr_ref, acc_ref = outs
        dh_ref, dxs_ref, t_ref, accb_ref, acca_ref, accd_ref, cc_ref, rr_ref = scr
        c = pl.program_id(0)
        cm.start(copies, c == 0)

        @pl.when(c == 0)
        def _():
            cc_ref[...] = jnp.zeros_like(cc_ref)
            rr_ref[...] = jnp.zeros_like(rr_ref)

        @pl.when(c == 0)
        def _():
            dh_ref[...] = jnp.zeros_like(dh_ref)
            accb_ref[...] = jnp.zeros_like(accb_ref)
            acca_ref[...] = jnp.zeros_like(acca_ref)
            accd_ref[...] = jnp.zeros_like(accd_ref)

        dtr, dt, a, acs, _, dtx, x, xdt, e, dsx = _ssd_common(xbc_ref, dtr_ref, bias_ref, alog_ref, tril_ref,
                                                              saved=(ax_ref, dtx_ref))
        acs_t = acs.T
        xb = xdt.astype(bf16)
        xdf = xdt * dsx
        xd = xdf.astype(bf16)
        dyv = dy_ref[...].astype(f32)
        dyb = dy_ref[...]
        dye = (dyv * e).astype(bf16)
        causal = tril_ref[...] > 0
        halves = _head_halves()
        last_row = lax.broadcasted_iota(jnp.int32, (CH, 1), 0) == CH - 1
        for g in range(NG):
            gs = slice(g * GW, (g + 1) * GW)
            bsl = slice(DI + g * NS, DI + (g + 1) * NS)
            csl = slice(DI + NG * NS + g * NS, DI + NG * NS + (g + 1) * NS)
            bg = xbc_ref[:, bsl]
            cg = xbc_ref[:, csl]
            cb = _dot(cg, bg, "nt")
            hpb = hp_ref[0, g]
            dhn = dh_ref[g]
            dhnb = dhn.astype(bf16)
            yoff = _dot(cg, hpb) * e[:, gs]
            dxd = _dot(bg, dhnb)
            t2 = dxd * xdf[:, gs]
            t3 = jnp.sum(dhn * hpb.astype(f32), axis=0, keepdims=True) * e[CH - 1:CH, gs]
            t_ref[:, gs] = dyv[:, gs] * yoff - t2 + jnp.where(last_row, jnp.sum(t2, axis=0, keepdims=True) + t3, 0.0)
            dxs_ref[:, gs] = dxd * dsx[:, gs]
            dcg = _dot(dye[:, gs], hpb, "nt")
            dbg = _dot(xd[:, gs], dhnb, "nt")
            dh_ref[g] = dhn * e[CH - 1:CH, gs] + _dot(cg, dye[:, gs], "tn")
            dcb = jnp.zeros((CH, CH), f32)
            for k in range(NH // NG // 2):
                hh0 = g * (NH // NG) + 2 * k
                ps = slice(hh0 * HD, (hh0 + 2) * HD)
                xp = xb[:, ps]
                dyp = dyb[:, ps]
                acc = None
                for o in range(2):
                    hh = hh0 + o
                    dyh = jnp.where(halves[o], dyp, jnp.zeros_like(dyp))
                    lm = _ssd_lmat(acs, acs_t, hh, causal)
                    m = cb * lm
                    dm = _dot(dyh, xp, "nt")
                    gm = dm * m
                    cc_ref[:, hh:hh + 1] = jnp.sum(gm, axis=1, keepdims=True)
                    rr_ref[hh:hh + 1, :] = jnp.sum(gm, axis=0, keepdims=True)
                    dcb = dcb + dm * lm
                    part = _dot(m.astype(bf16), dyh, "tn")
                    acc = part if acc is None else acc + part
                dxs_ref[:, ps] += acc
            dcbb = dcb.astype(bf16)
            dxbc_ref[:, csl] = (dcg + _dot(dcbb, bg)).astype(bf16)
            dxbc_ref[:, bsl] = (dbg + _dot(dcbb, cg, "tn")).astype(bf16)
        dxf = dxs_ref[...]
        dxbc_ref[:, 0:DI] = (dxf * dtx + dxr_ref[...] * dyv).astype(bf16)
        expt = expt_ref[...]
        dacs = cc_ref[...] - rr_ref[...].T + _dot2_right(t_ref[...], expt)
        dadt = _dot3_left(triu_ref[...], dacs)
        ddt = _dot2_right(dxf * x, expt) + dadt * a
        ddtr = ddt * _sigmoid(dtr)
        ddtr_ref[...] = ddtr
        accb_ref[...] += ddtr
        acca_ref[...] += dadt * dt
        accd_ref[...] += _dot2_right(dyv * x, expt)

        @pl.when(c == nc - 1)
        def _():
            acc_ref[...] = jnp.zeros_like(acc_ref)
            acc_ref[0:1, :] = jnp.sum(accb_ref[...], axis=0, keepdims=True)
            acc_ref[1:2, :] = jnp.sum(acca_ref[...], axis=0, keepdims=True) * a
            acc_ref[2:3, :] = jnp.sum(accd_ref[...], axis=0, keepdims=True)

        cm.wait(copies, c == nc - 1)

    par = lambda shape: pl.BlockSpec(shape, lambda c: (0,) * len(shape))
    rev = lambda c: (nc - 1 - c, 0)
    res = pl.pallas_call(
        body, name="ssd_bwd", grid=(nc,),
        in_specs=[pl.BlockSpec((CH, CONVD), rev), pl.BlockSpec((CH, LANES), rev), pl.BlockSpec((CH, DI), rev),
                  pl.BlockSpec((1, NG, NS, GW), lambda c: (nc - 1 - c, 0, 0, 0)),
                  pl.BlockSpec((CH, DI), rev), pl.BlockSpec((CH, DI), rev),
                  par((1, LANES)), par((1, LANES)), par((1, DI)), par((DI, LANES)),
                  par((CH, CH)), par((CH, CH))] + cm.in_specs(),
        out_specs=[pl.BlockSpec((CH, CONVD), rev), pl.BlockSpec((CH, LANES), rev), par((8, LANES))] + cm.out_specs(),
        out_shape=[jax.ShapeDtypeStruct((S, CONVD), bf16), jax.ShapeDtypeStruct((S, LANES), f32),
                   jax.ShapeDtypeStruct((8, LANES), f32)] + cm.out_shape(),
        scratch_shapes=[pltpu.VMEM((NG, NS, GW), f32), pltpu.VMEM((CH, DI), f32), pltpu.VMEM((CH, DI), f32),
                        pltpu.VMEM((CH, LANES), f32), pltpu.VMEM((CH, LANES), f32), pltpu.VMEM((CH, LANES), f32),
                        pltpu.VMEM((CH, LANES), f32), pltpu.VMEM((LANES, CH), f32)] + cm.scratch(),
        compiler_params=_cp("arbitrary", side_effects=bool(cm.n)),
    )(xbc, dtr, dy, hprev, *saved, bias, alog, dx_row, expand_t, tril, triu, *cm.bufs)
    return res[0], res[1], res[2], res[3:]


def _partner(t):
    half = AD // 2
    return jnp.concatenate([t[h * AD + o:h * AD + o + half] for h in range(t.shape[0] // AD) for o in (half, 0)], axis=0)


def _rope(t, cos, sin):
    reps = t.shape[0] // AD
    return t * jnp.tile(cos, (reps, 1)) + _partner(t) * jnp.tile(sin, (reps, 1))


def _rope_t(d, cos, sin):
    reps = d.shape[0] // AD
    return d * jnp.tile(cos, (reps, 1)) - _partner(d) * jnp.tile(sin, (reps, 1))


def _lanes_of_group(t, g):
    return jnp.concatenate([t[(g * REP + r) * AD:(g * REP + r + 1) * AD] for r in range(REP)], axis=1)


def _attn_probs(qg, k2, sink_ref, g, not_first):
    n = qg.shape[1]
    s = lax.broadcasted_iota(jnp.int32, (2 * WIN, n), 0)
    t = lax.broadcasted_iota(jnp.int32, (2 * WIN, n), 1) % WIN
    valid = jnp.logical_or(jnp.logical_and(jnp.logical_and(s < WIN, s > t), not_first),
                           jnp.logical_and(s >= WIN, s - WIN <= t))
    sink = jnp.concatenate([jnp.broadcast_to(sink_ref[0:1, g * REP + r:g * REP + r + 1], (1, WIN)) for r in range(REP)],
                           axis=1)
    sc = jnp.where(valid, _dot(k2, qg, "tn"), -1e30)
    m = jnp.maximum(jnp.max(sc, axis=0, keepdims=True), sink)
    p = jnp.exp(sc - m)
    ps = jnp.exp(sink - m)
    inv = 1.0 / (jnp.sum(p, axis=0, keepdims=True) + ps)
    return p * inv, ps * inv


def attn_fwd(qt, kvt, cos, sin, sinks):
    S = qt.shape[1]
    nb = S // WIN
    cur = lambda i: (0, i)
    prev = lambda i: (0, jnp.maximum(i - 1, 0))

    def body(q_ref, kv_ref, kvp_ref, cos_ref, sin_ref, cosp_ref, sinp_ref, sink_ref, o_ref):
        i = pl.program_id(0)
        q = (_rope(q_ref[...].astype(f32), cos_ref[...], sin_ref[...]) * (AD ** -0.5)).astype(bf16)
        kc = _rope(kv_ref[0:KVW, :].astype(f32), cos_ref[...], sin_ref[...]).astype(bf16)
        kp = _rope(kvp_ref[0:KVW, :].astype(f32), cosp_ref[...], sinp_ref[...]).astype(bf16)
        for g in range(KVH):
            ks = slice(g * AD, (g + 1) * AD)
            vs = slice(KVW + g * AD, KVW + (g + 1) * AD)
            k2 = jnp.concatenate([kp[ks], kc[ks]], axis=1)
            v2 = jnp.concatenate([kvp_ref[vs, :], kv_ref[vs, :]], axis=1)
            p, _ = _attn_probs(_lanes_of_group(q, g), k2, sink_ref, g, i > 0)
            o = _dot(v2, p.astype(bf16))
            for r in range(REP):
                h = g * REP + r
                o_ref[h * AD:(h + 1) * AD, :] = o[:, r * WIN:(r + 1) * WIN].astype(bf16)

    tab = pl.BlockSpec((AD, WIN), cur)
    tabp = pl.BlockSpec((AD, WIN), prev)
    return pl.pallas_call(
        body, name="attn_fwd", grid=(nb,),
        in_specs=[pl.BlockSpec((D, WIN), cur), pl.BlockSpec((2 * KVW, WIN), cur), pl.BlockSpec((2 * KVW, WIN), prev),
                  tab, tab, tabp, tabp, pl.BlockSpec((1, LANES), lambda i: (0, 0))],
        out_specs=pl.BlockSpec((D, WIN), cur),
        out_shape=jax.ShapeDtypeStruct((D, S), bf16),
        compiler_params=_cp("parallel"),
    )(qt, kvt, kvt, cos, sin, cos, sin, sinks)


def attn_bwd(qt, kvt, cos, sin, sinks, daot, comm=None):
    S = qt.shape[1]
    nb = S // WIN
    cur = lambda i: (0, jnp.minimum(i, nb - 1))
    prev = lambda i: (0, jnp.maximum(i - 1, 0))
    cm = _Comm(comm)

    def body(*refs):
        ins, (dq_ref, dkv_ref, ds_ref), scr, copies = cm.split(refs, 9, 3)
        q_ref, kv_ref, kvp_ref, cos_ref, sin_ref, cosp_ref, sinp_ref, sink_ref, do_ref = ins
        ck_ref, cv_ref, dqs_ref, dkp_ref, dvp_ref, dkc_ref, dvc_ref, accs_ref = scr
        i = pl.program_id(0)
        cm.start(copies, i == 0)

        @pl.when(i == 0)
        def _():
            ck_ref[...] = jnp.zeros_like(ck_ref)
            cv_ref[...] = jnp.zeros_like(cv_ref)
            accs_ref[...] = jnp.zeros_like(accs_ref)

        @pl.when(i == nb)
        def _():
            dkp_ref[...] = jnp.zeros_like(dkp_ref)
            dvp_ref[...] = jnp.zeros_like(dvp_ref)

        @pl.when(i < nb)
        def _():
            q = (_rope(q_ref[...].astype(f32), cos_ref[...], sin_ref[...]) * (AD ** -0.5)).astype(bf16)
            kc = _rope(kv_ref[0:KVW, :].astype(f32), cos_ref[...], sin_ref[...]).astype(bf16)
            kp = _rope(kvp_ref[0:KVW, :].astype(f32), cosp_ref[...], sinp_ref[...]).astype(bf16)
            do = do_ref[...]
            for g in range(KVH):
                ks = slice(g * AD, (g + 1) * AD)
                vs = slice(KVW + g * AD, KVW + (g + 1) * AD)
                qg = _lanes_of_group(q, g)
                dog = _lanes_of_group(do, g)
                k2 = jnp.concatenate([kp[ks], kc[ks]], axis=1)
                v2 = jnp.concatenate([kvp_ref[vs, :], kv_ref[vs, :]], axis=1)
                p, ps = _attn_probs(qg, k2, sink_ref, g, i > 0)
                dp = _dot(v2, dog, "tn")
                delta = jnp.sum(p * dp, axis=0, keepdims=True)
                ds = (p * (dp - delta)).astype(bf16)
                accs_ref[g:g + 1, :] -= ps * delta
                dqg = _dot(k2, ds) * (AD ** -0.5)
                for r in range(REP):
                    h = g * REP + r
                    dqs_ref[h * AD:(h + 1) * AD, :] = dqg[:, r * WIN:(r + 1) * WIN]
                dk2 = _dot(qg, ds, "nt")
                dv2 = _dot(dog, p.astype(bf16), "nt")
                dkp_ref[ks, :] = dk2[:, 0:WIN]
                dkc_ref[ks, :] = dk2[:, WIN:2 * WIN]
                dvp_ref[ks, :] = dv2[:, 0:WIN]
                dvc_ref[ks, :] = dv2[:, WIN:2 * WIN]
            dq_ref[...] = _rope_t(dqs_ref[...], cos_ref[...], sin_ref[...]).astype(bf16)

        dkv_ref[0:KVW, :] = _rope_t(ck_ref[...] + dkp_ref[...], cosp_ref[...], sinp_ref[...]).astype(bf16)
        dkv_ref[KVW:2 * KVW, :] = (cv_ref[...] + dvp_ref[...]).astype(bf16)

        @pl.when(i < nb)
        def _():
            ck_ref[...] = dkc_ref[...]
            cv_ref[...] = dvc_ref[...]

        @pl.when(i == nb)
        def _():
            lane = lax.broadcasted_iota(jnp.int32, (1, LANES), 1)
            row = jnp.zeros((1, LANES), f32)
            for h in range(AH):
                part = accs_ref[h // REP:h // REP + 1, (h % REP) * WIN:(h % REP + 1) * WIN]
                row = row + jnp.where(lane == h, jnp.sum(part, axis=1, keepdims=True), 0.0)
            ds_ref[...] = jnp.zeros_like(ds_ref)
            ds_ref[0:1, :] = row

        cm.wait(copies, i == nb)

    tab = pl.BlockSpec((AD, WIN), cur)
    tabp = pl.BlockSpec((AD, WIN), prev)
    kvs = lambda: pltpu.VMEM((KVW, WIN), f32)
    res = pl.pallas_call(
        body, name="attn_bwd", grid=(nb + 1,),
        in_specs=[pl.BlockSpec((D, WIN), cur), pl.BlockSpec((2 * KVW, WIN), cur), pl.BlockSpec((2 * KVW, WIN), prev),
                  tab, tab, tabp, tabp, pl.BlockSpec((1, LANES), lambda i: (0, 0)),
                  pl.BlockSpec((D, WIN), cur)] + cm.in_specs(),
        out_specs=[pl.BlockSpec((D, WIN), cur), pl.BlockSpec((2 * KVW, WIN), prev),
                   pl.BlockSpec((8, LANES), lambda i: (0, 0))] + cm.out_specs(),
        out_shape=[jax.ShapeDtypeStruct((D, S), bf16), jax.ShapeDtypeStruct((2 * KVW, S), bf16),
                   jax.ShapeDtypeStruct((8, LANES), f32)] + cm.out_shape(),
        scratch_shapes=[kvs(), kvs(), pltpu.VMEM((D, WIN), f32), kvs(), kvs(), kvs(), kvs(),
                        pltpu.VMEM((8, REP * WIN), f32)] + cm.scratch(),
        compiler_params=_cp("arbitrary", side_effects=bool(cm.n)),
    )(qt, kvt, kvt, cos, sin, cos, sin, sinks, daot, *cm.bufs)
    return res[0], res[1], res[2], res[3:]


ADAM_C1 = 1.0 / (1.0 - ADAM_B1 ** ADAM_STEP)
ADAM_C2 = 1.0 / (1.0 - ADAM_B2 ** ADAM_STEP)


def _adam_update(g, w, m, v):
    nm = ADAM_B1 * m + (1.0 - ADAM_B1) * g
    nv = ADAM_B2 * v + (1.0 - ADAM_B2) * (g * g)
    return -ADAM_LR * ((nm * ADAM_C1) / (jnp.sqrt(nv * ADAM_C2) + ADAM_EPS) + ADAM_WD * w), nm, nv


def adamw(parts, w, m, v, tr, name):
    n, R, C = parts.shape

    def body(p_ref, w_ref, m_ref, v_ref, g_ref, d_ref, nm_ref, nv_ref):
        def grp(g0, _):
            r0 = pl.multiple_of(g0 * RG, RG)
            rows = pl.ds(r0, RG)
            g = p_ref[0, rows, :].astype(f32)
            for k in range(1, n):
                g = g + p_ref[k, rows, :].astype(f32)
            d, nm, nv = _adam_update(g, w_ref[rows, :], m_ref[rows, :], v_ref[rows, :])
            g_ref[rows, :] = g
            d_ref[rows, :] = d
            nm_ref[rows, :] = nm
            nv_ref[rows, :] = nv
            return 0

        lax.fori_loop(0, tr // RG, grp, 0)

    row = pl.BlockSpec((tr, C), lambda i: (i, 0))
    o = jax.ShapeDtypeStruct((R, C), f32)
    return pl.pallas_call(
        body, name=name, grid=(R // tr,),
        in_specs=[pl.BlockSpec((n, tr, C), lambda i: (0, i, 0)), row, row, row],
        out_specs=[row, row, row, row], out_shape=[o, o, o, o],
        compiler_params=_cp("parallel"),
    )(parts, w, m, v)


SMALL_ROW = (("norm_mix_post_w", D), ("norm_ffn_pre_w", D), ("norm_ffn_post_w", D), ("ssd_norm_w", DI),
             ("ssd_conv_b", CONVD), ("ffn_conv_b", 2 * FF), ("ssd_dt_bias", NH), ("ssd_a_log", NH), ("ssd_d", NH),
             ("attn_sinks", AH), ("loss", 1))
CONV_BLOCK = 1152
SSD_CONV_COLS = CONVD // N_DEV
FFN_CONV_COLS = 2 * FF // N_DEV


def _row_offsets():
    off, o = {}, 0
    for name, n in SMALL_ROW:
        off[name] = (o, n)
        o += -(-n // LANES) * LANES
    return off, o


def adamw_small(recv_row, recv_pre, recv_conv, params):
    off, _ = _row_offsets()
    names = list(params)
    n = len(names)

    def total(ref, rows, lo, width):
        g = ref[0, rows, lo:lo + width]
        for d in range(1, N_DEV):
            g = g + ref[d, rows, lo:lo + width]
        return g

    def grad_of(name, row_ref, pre_ref, conv_ref):
        if name == "norm_mix_pre_w":
            return total(pre_ref, slice(0, 1), 0, D)
        if name == "ssd_conv_w":
            return total(conv_ref, slice(0, SSD_K), 0, SSD_CONV_COLS)
        if name == "ffn_conv_w":
            return total(conv_ref, slice(0, FFN_K), 3 * LANES, FFN_CONV_COLS)
        o, width = off[name]
        return total(row_ref, slice(0, 1), o, width)

    def body(row_ref, pre_ref, conv_ref, *refs):
        ins, outs = refs[:3 * n], refs[3 * n:]
        for k, name in enumerate(names):
            w_ref, m_ref, v_ref = ins[3 * k:3 * k + 3]
            g_ref, d_ref, nm_ref, nv_ref = outs[4 * k:4 * k + 4]
            g = grad_of(name, row_ref, pre_ref, conv_ref)
            d, nm, nv = _adam_update(g, w_ref[...], m_ref[...], v_ref[...])
            g_ref[...] = g
            d_ref[...] = d
            nm_ref[...] = nm
            nv_ref[...] = nv
        outs[4 * n][...] = total(row_ref, slice(0, 1), off["loss"][0], LANES)

    flat = [t for name in names for t in params[name]]
    out_shape = [jax.ShapeDtypeStruct(params[name][0].shape, f32) for name in names for _ in range(4)]
    res = pl.pallas_call(
        body, name="adamw_small",
        out_shape=out_shape + [jax.ShapeDtypeStruct((1, LANES), f32)],
        compiler_params=pltpu.CompilerParams(vmem_limit_bytes=VMEM_LIMIT),
    )(recv_row, recv_pre, recv_conv, *flat)
    return {name: res[4 * k:4 * k + 4] for k, name in enumerate(names)}, res[4 * n]


def _cat_rows(parts):
    words = [lax.bitcast_convert_type(p, jnp.uint16) for p in parts]
    return lax.bitcast_convert_type(jnp.concatenate(words, axis=0), bf16)


def _pad_rows8(w):
    return jnp.pad(w, ((0, 8 - w.shape[0]), (0, 0)))


def _pad_lanes(v):
    return jnp.pad(v.reshape(1, -1), ((0, 0), (0, LANES - v.size)))


WEIGHTS = ('norm_mix_pre_w', 'w_in', 'ssd_conv_w', 'ssd_conv_b', 'ssd_dt_bias', 'ssd_a_log', 'ssd_d', 'ssd_norm_w',
           'ssd_w_out', 'attn_sinks', 'attn_w_out', 'w_mix_out', 'norm_mix_post_w', 'norm_ffn_pre_w', 'ffn_w_up',
           'ffn_conv_w', 'ffn_conv_b', 'ffn_w_down', 'norm_ffn_post_w')
W_IN_ROWS = IN_DIM // N_DEV
W_IN_PAD = 1104
W_IN_SPLIT = (672, 768, 832)
TS = 512


def kernel(x, positions, norm_mix_pre_w, w_in, ssd_conv_w, ssd_conv_b, ssd_dt_bias, ssd_a_log, ssd_d, ssd_norm_w, ssd_w_out, attn_sinks, attn_w_out, w_mix_out, norm_mix_post_w, norm_ffn_pre_w, ffn_w_up, ffn_conv_w, ffn_conv_b, ffn_w_down, norm_ffn_post_w, loss_target, m_norm_mix_pre_w, m_w_in, m_ssd_conv_w, m_ssd_conv_b, m_ssd_dt_bias, m_ssd_a_log, m_ssd_d, m_ssd_norm_w, m_ssd_w_out, m_attn_sinks, m_attn_w_out, m_w_mix_out, m_norm_mix_post_w, m_norm_ffn_pre_w, m_ffn_w_up, m_ffn_conv_w, m_ffn_conv_b, m_ffn_w_down, m_norm_ffn_post_w, v_norm_mix_pre_w, v_w_in, v_ssd_conv_w, v_ssd_conv_b, v_ssd_dt_bias, v_ssd_a_log, v_ssd_d, v_ssd_norm_w, v_ssd_w_out, v_attn_sinks, v_attn_w_out, v_w_mix_out, v_norm_mix_post_w, v_norm_ffn_pre_w, v_ffn_w_up, v_ffn_conv_w, v_ffn_conv_b, v_ffn_w_down, v_norm_ffn_post_w):
    a = locals()
    r2 = lambda t: t.reshape(t.shape[-2], t.shape[-1])
    w = {n: r2(a[n]) for n in WEIGHTS}
    m = {n: r2(a["m_" + n]) for n in WEIGHTS}
    v = {n: r2(a["v_" + n]) for n in WEIGHTS}
    xs, target = x[0], loss_target[0]
    S = xs.shape[0]
    ts = TS

    w_in_blk = jnp.pad(w["w_in"].T.astype(bf16), ((0, W_IN_PAD - W_IN_ROWS), (0, 0)))
    conv_blk = jnp.concatenate([_pad_rows8(w["ssd_conv_w"]), _pad_rows8(w["ffn_conv_w"]),
                                jnp.zeros((8, CONV_BLOCK - SSD_CONV_COLS - FFN_CONV_COLS), f32)], axis=1)
    u, cos, sin, (g_in, g_conv) = prenorm_fwd(xs, w["norm_mix_pre_w"], positions, ts, [w_in_blk, conv_blk])
    wt = g_in[:, :W_IN_ROWS].reshape(IN_DIM, D)
    w_main_t = _cat_rows([wt[IN_OFF[0]:IN_OFF[1]], wt[IN_OFF[6]:IN_OFF[8]], wt[IN_OFF[1]:IN_OFF[2]]])
    w_q_t = wt[IN_OFF[3]:IN_OFF[4]]
    w_kv_t = wt[IN_OFF[4]:IN_OFF[6]]
    w_dt_t = jnp.pad(wt[IN_OFF[2]:IN_OFF[3]], ((0, LANES - NH), (0, 0)))
    conv_w8 = g_conv[:, :, 0:SSD_CONV_COLS].transpose(1, 0, 2).reshape(8, CONVD)
    fconv_w8 = g_conv[:, :, SSD_CONV_COLS:SSD_CONV_COLS + FFN_CONV_COLS].transpose(1, 0, 2).reshape(8, 2 * FF)
    bias = _pad_lanes(w["ssd_dt_bias"])
    alog = _pad_lanes(w["ssd_a_log"])
    dx_row = jnp.repeat(w["ssd_d"].reshape(-1), HD).reshape(1, DI)
    sinks = _pad_lanes(w["attn_sinks"])

    later = [w["ssd_w_out"].astype(bf16), w["attn_w_out"].astype(bf16), w["w_mix_out"].astype(bf16)]
    proj, (g_so, g_ao, g_mix) = mm(u, w_main_t, "nt", bf16, "mm_proj", comm=(later, (False,) * 3))
    w_ssd_out, w_attn_out, w_mix = g_so.reshape(DI, D), g_ao.reshape(D, D), g_mix.reshape(D, D)
    qt = mm(w_q_t, u, "nt", bf16, "mm_q")
    kvt = mm(w_kv_t, u, "nt", bf16, "mm_kv")
    dtr = mm(u, w_dt_t, "nt", f32, "mm_dt")
    xbc, conv_c = ssdconv_fwd(proj, conv_w8, w["ssd_conv_b"], ts)
    y, hprev, ssd_saved, (g_up, g_down) = ssd_fwd(xbc, dtr, bias, alog, dx_row, comm=(
        [w["ffn_w_up"].T.astype(bf16), w["ffn_w_down"].astype(bf16)], (False, False)))
    w_up_t = g_up.reshape(2 * FF, D)
    w_down = g_down.reshape(FF, D)
    yn = gnorm_fwd(y, proj, w["ssd_norm_w"], ts)
    ys = mm(yn, w_ssd_out, "nn", bf16, "mm_ssd_out")
    aot = attn_fwd(qt, kvt, cos, sin, sinks)
    ya = mm(aot, w_attn_out, "tn", bf16, "mm_attn_out")
    merged = merge_fwd(proj, ys, ya, ts)
    mo = mm(merged, w_mix, "nn", bf16, "mm_mix")
    x1, h = post_fwd(xs, mo, w["norm_mix_post_w"], w["norm_ffn_pre_w"], ts)
    up = mm(h, w_up_t, "nt", bf16, "mm_up")
    act, gate_c, val_c = ffnact_fwd(up, fconv_w8, w["ffn_conv_b"], ts)
    ff = mm(act, w_down, "nn", bf16, "mm_down")
    loss_blk, dout, dff, g_post2 = loss_head(x1, ff, target, w["norm_ffn_post_w"], ts)

    dact = mm(dff, w_down, "nt", bf16, "mm_dact")
    gw_down = mm(act, dff, "tn", bf16, "mm_g_down")
    dgate, dval = ffnact_bwd(dact, gate_c, val_c, ts)
    dup_pre, g_fconv_a = dwconv_bwd(dgate, up, 0, fconv_w8, 0, FFN_K, FF, ts, "ffnconv_bwd_gate", out_cols=2 * FF)
    dup_pre, g_fconv_b = dwconv_bwd(dval, up, 1, fconv_w8, 1, FFN_K, FF, ts, "ffnconv_bwd_val", into=dup_pre, ocb=1,
                                    out_cols=2 * FF)
    g_fconv = jnp.concatenate([g_fconv_a, g_fconv_b], axis=1)
    dh, (r_down,) = mm(dup_pre, w_up_t, "nn", bf16, "mm_dh", comm=([gw_down.reshape(N_DEV, FF // N_DEV, D)], (True,)))
    gw_up_t = mm(dup_pre, h, "tn", bf16, "mm_g_up")
    dx1, dmo, g_norms = post_bwd(dout, dh, x1, mo, w["norm_mix_post_w"], w["norm_ffn_pre_w"], ts)
    dmerged = mm(dmo, w_mix, "nt", bf16, "mm_dmerged")
    gw_mix = mm(merged, dmo, "tn", bf16, "mm_g_mix")
    dys, dya, dproj = merge_bwd(dmerged, proj, ys, ya, ts)
    daot = mm(w_attn_out, dya, "nt", bf16, "mm_dao")
    gw_attn_out = mm(aot, dya, "nn", bf16, "mm_g_attn_out")
    dqt, dkvt, g_sinks, (r_up,) = attn_bwd(qt, kvt, cos, sin, sinks, daot,
                                           comm=([gw_up_t.reshape(N_DEV, 2 * FF // N_DEV, D)], (True,)))
    dyn = mm(dys, w_ssd_out, "nt", bf16, "mm_dyn")
    gw_ssd_out = mm(yn, dys, "tn", bf16, "mm_g_ssd_out")
    dy, dproj, g_gnorm = gnorm_bwd(dyn, y, proj, w["ssd_norm_w"], dproj, ts)
    sends = [gw_ssd_out.reshape(N_DEV, DI // N_DEV, D), gw_attn_out.reshape(N_DEV, D // N_DEV, D),
             gw_mix.reshape(N_DEV, D // N_DEV, D)]
    dxbc, ddtr, g_ssd, (r_so, r_ao, r_mix) = ssd_bwd(xbc, dtr, dy, hprev, ssd_saved, bias, alog, dx_row,
                                                     comm=(sends, (True,) * 3))
    dproj, g_conv_w = dwconv_bwd(dxbc, proj, C_XBC // 1024, conv_w8, 0, SSD_K, 1024, ts, "ssdconv_bwd", act_c=conv_c,
                                 into=dproj, ocb=C_XBC // 1024, out_cols=PM)
    ddtr_b = ddtr.astype(bf16)
    du_c = mm(ddtr_b, w_dt_t, "nn", bf16, "mm_du_dt")
    g_main_t = mm(dproj, u, "tn", bf16, "mm_g_in")
    g_q_t = mm(dqt, u, "nn", bf16, "mm_g_q")
    g_kv_t = mm(dkvt, u, "nn", bf16, "mm_g_kv")
    g_dt_t = mm(ddtr_b, u, "tn", bf16, "mm_g_dt")
    g_wt = _cat_rows([g_main_t[C_Z:C_GS], g_main_t[C_XBC:PM], g_dt_t[:NH], g_q_t, g_kv_t, g_main_t[C_GS:C_XBC]])
    send_in = jnp.pad(g_wt.reshape(N_DEV, W_IN_ROWS, D), ((0, 0), (0, W_IN_PAD - W_IN_ROWS), (0, 0)))
    pieces = {"norm_mix_post_w": g_norms[1:2], "norm_ffn_pre_w": g_norms[0:1], "norm_ffn_post_w": g_post2[0:1],
              "ssd_norm_w": g_gnorm[0:1], "ssd_conv_b": g_conv_w[7:8], "ffn_conv_b": g_fconv[7:8],
              "ssd_dt_bias": g_ssd[0:1], "ssd_a_log": g_ssd[1:2], "ssd_d": g_ssd[2:3], "attn_sinks": g_sinks[0:1],
              "loss": loss_blk[0:1]}
    row = jnp.concatenate([jnp.pad(pieces[n][:, :min(k, pieces[n].shape[1])],
                                   ((0, 0), (0, -(-k // LANES) * LANES - min(k, pieces[n].shape[1]))))
                           for n, k in SMALL_ROW], axis=1)
    send_row = jnp.pad(row, ((0, 7), (0, 0)))
    send_conv = jnp.concatenate(
        [g_conv_w.reshape(8, N_DEV, SSD_CONV_COLS).transpose(1, 0, 2),
         g_fconv.reshape(8, N_DEV, FFN_CONV_COLS).transpose(1, 0, 2),
         jnp.zeros((N_DEV, 8, CONV_BLOCK - SSD_CONV_COLS - FFN_CONV_COLS), f32)], axis=2)
    r0, r1, r2 = W_IN_SPLIT
    du_a, (r_in_a, recv_row, recv_conv) = mm(dproj, w_main_t, "nn", bf16, "mm_du", comm=(
        [send_in[:, :r0], send_row, send_conv], (True, False, True)))
    du_d, (r_in_b,) = mm(dqt, w_q_t, "tn", bf16, "mm_du_q", comm=([send_in[:, r0:r1]], (True,)))
    du_b, (r_in_c,) = mm(dkvt, w_kv_t, "tn", bf16, "mm_du_kv", comm=([send_in[:, r1:r2]], (True,)))
    grad_x, g_pre, (r_in_d,) = prenorm_bwd(xs, w["norm_mix_pre_w"], (du_a, du_b, du_c, du_d), dx1, ts,
                                           comm=([send_in[:, r2:]], (True,)))
    (recv_pre,) = exchange([g_pre], (False,), "gather_last")

    r_in = jnp.concatenate([r_in_a, r_in_b, r_in_c, r_in_d], axis=1)
    tpad = lambda t: jnp.pad(t.T, ((0, W_IN_PAD - W_IN_ROWS), (0, 0)))
    o_in = [t[:W_IN_ROWS].T for t in adamw(r_in, tpad(w["w_in"]), tpad(m["w_in"]), tpad(v["w_in"]), 368, "adamw_w_in")]
    o_up = [t.T for t in adamw(r_up, w["ffn_w_up"].T, m["ffn_w_up"].T, v["ffn_w_up"].T, 352, "adamw_w_up")]
    big = {"w_in": o_in, "ffn_w_up": o_up,
           "ssd_w_out": adamw(r_so, w["ssd_w_out"], m["ssd_w_out"], v["ssd_w_out"], 256, "adamw_ssd_out"),
           "attn_w_out": adamw(r_ao, w["attn_w_out"], m["attn_w_out"], v["attn_w_out"], 128, "adamw_attn_out"),
           "w_mix_out": adamw(r_mix, w["w_mix_out"], m["w_mix_out"], v["w_mix_out"], 128, "adamw_mix"),
           "ffn_w_down": adamw(r_down, w["ffn_w_down"], m["ffn_w_down"], v["ffn_w_down"], 352, "adamw_down")}
    small_names = [n for n in WEIGHTS if n not in big]
    small, loss_row = adamw_small(recv_row, recv_pre, recv_conv, {n: (w[n], m[n], v[n]) for n in small_names})

    outs = [loss_row[0, 0], grad_x[None]]
    for k in range(4):
        for n in WEIGHTS:
            outs.append((big[n][k] if n in big else small[n][k]).reshape(a[n].shape))
    return tuple(outs)
```

```python
import jax
import jax.numpy as jnp
import numpy as np
from jax import lax
from jax.experimental import pallas as pl
from jax.experimental.pallas import tpu as pltpu

f32 = jnp.float32
bf16 = jnp.bfloat16

N_DEV = 8
D = 1024
DI = 2048
NH = 32
HD = 64
NG = 4
GW = DI // NG
NS = 128
CH = 128
CONVD = DI + 2 * NG * NS
SSD_K = 4
AH = 16
AD = 64
KVH = 4
REP = AH // KVH
KVW = KVH * AD
WIN = 128
FF = 2816
FFN_K = 3
EPS = 1e-6
ROPE_THETA = 10000.0
LANES = 128
RG = 16
CW = 256

C_Z, C_GS, C_GA, C_XBC, PM = 0, 2048, 3072, 4096, 7168
IN_SIZES = (DI, CONVD, NH, D, KVW, KVW, D, D)
IN_OFF = tuple(int(v) for v in np.cumsum((0,) + IN_SIZES))
IN_DIM = IN_OFF[-1]

ADAM_LR, ADAM_B1, ADAM_B2, ADAM_EPS, ADAM_WD, ADAM_STEP = 0.001, 0.9, 0.999, 1e-08, 0.01, 10

VMEM_LIMIT = 56 * 1024 * 1024


def _cp(*sem, side_effects=False):
    return pltpu.CompilerParams(dimension_semantics=sem, vmem_limit_bytes=VMEM_LIMIT, has_side_effects=side_effects)


def _dot(a, b, mode="nn"):
    dims = {"nn": (((1,), (0,)), ((), ())), "nt": (((1,), (1,)), ((), ())), "tn": (((0,), (0,)), ((), ()))}[mode]
    return lax.dot_general(a, b, dims, preferred_element_type=f32)


def _split3(v):
    hi = v.astype(bf16)
    r = v - hi.astype(f32)
    mid = r.astype(bf16)
    lo = (r - mid.astype(f32)).astype(bf16)
    return hi, mid, lo


def _dot3_left(m01, v):
    hi, mid, lo = _split3(v)
    return _dot(m01, hi) + _dot(m01, mid) + _dot(m01, lo)


def _dot3_right(v, m01):
    hi, mid, lo = _split3(v)
    return _dot(hi, m01) + _dot(mid, m01) + _dot(lo, m01)


def _dot2_right(v, m01):
    hi = v.astype(bf16)
    lo = (v - hi.astype(f32)).astype(bf16)
    return _dot(hi, m01) + _dot(lo, m01)


def _sigmoid(x):
    return 1.0 / (1.0 + jnp.exp(-x))


def _sigmoid_fast(x):
    return pl.reciprocal(1.0 + jnp.exp(-x), approx=True)


def _peer(k, x, y, c):
    return ((1 - x) if k & 4 else x, (1 - y) if k & 2 else y, (1 - c) if k & 1 else c)


def _xchg_copies(buf_refs, out_refs, send_sems, recv_sems, local_sems, personalised):
    x, y, c = lax.axis_index("x"), lax.axis_index("y"), lax.axis_index("c")
    me = 4 * x + 2 * y + c
    local, remote = [], []
    for b, (buf, out, pers) in enumerate(zip(buf_refs, out_refs, personalised)):
        local.append(pltpu.make_async_copy(buf.at[me] if pers else buf, out.at[me], local_sems.at[b]))
        for k in range(1, N_DEV):
            px, py, pc = _peer(k, x, y, c)
            s = b * (N_DEV - 1) + k - 1
            remote.append(pltpu.make_async_remote_copy(
                src_ref=buf.at[4 * px + 2 * py + pc] if pers else buf, dst_ref=out.at[me],
                send_sem=send_sems.at[s], recv_sem=recv_sems.at[s],
                device_id=(px, py, pc), device_id_type=pl.DeviceIdType.MESH))
    return local, remote


class _Comm:
    def __init__(self, comm):
        self.bufs, self.pers = comm if comm else ((), ())
        self.n = len(self.bufs)

    def in_specs(self):
        return [pl.BlockSpec(memory_space=pl.ANY)] * self.n

    out_specs = in_specs

    def out_shape(self):
        return [jax.ShapeDtypeStruct((N_DEV,) + tuple(b.shape[1:] if p else b.shape), b.dtype)
                for b, p in zip(self.bufs, self.pers)]

    def scratch(self):
        n = self.n
        return [pltpu.SemaphoreType.DMA((n * (N_DEV - 1),)), pltpu.SemaphoreType.DMA((n * (N_DEV - 1),)),
                pltpu.SemaphoreType.DMA((n,))] if n else []

    def split(self, refs, n_in, n_out):
        n = self.n
        ins, outs = refs[:n_in], refs[n_in + n:n_in + n + n_out]
        rest = refs[n_in + n + n_out + n:]
        if not n:
            return ins, outs, rest, None
        copies = _xchg_copies(refs[n_in:n_in + n], refs[n_in + n + n_out:n_in + n + n_out + n], *rest[-3:], self.pers)
        return ins, outs, rest[:-3], copies

    def start(self, copies, first):
        if copies:
            @pl.when(first)
            def _():
                for cp in copies[0] + copies[1]:
                    cp.start()

    def wait(self, copies, last):
        if copies:
            @pl.when(last)
            def _():
                for cp in copies[1]:
                    cp.wait_recv()
                for cp in copies[1]:
                    cp.wait_send()
                for cp in copies[0]:
                    cp.wait()


def exchange(bufs, personalised, name):
    cm = _Comm((bufs, personalised))

    def body(*refs):
        _, _, _, copies = cm.split(refs, 0, 0)
        cm.start(copies, True)
        cm.wait(copies, True)

    return pl.pallas_call(
        body, name=name, in_specs=cm.in_specs(), out_specs=cm.out_specs(), out_shape=cm.out_shape(),
        scratch_shapes=cm.scratch(), compiler_params=pltpu.CompilerParams(has_side_effects=True),
    )(*bufs)


class _TwoLevelGather:
    def __init__(self, bufs):
        self.bufs = list(bufs)
        self.n = len(self.bufs)

    def in_specs(self):
        return [pl.BlockSpec(memory_space=pl.ANY)] * self.n

    out_specs = in_specs

    def out_shape(self):
        return [jax.ShapeDtypeStruct((N_DEV,) + tuple(b.shape), b.dtype) for b in self.bufs]

    def scratch(self):
        per = N_DEV - 1
        return [pltpu.SemaphoreType.DMA((self.n * per,)), pltpu.SemaphoreType.DMA((self.n * per,)),
                pltpu.SemaphoreType.DMA((self.n,))]

    def bind(self, ins, outs, send_sems, recv_sems, local_sems):
        n, per = self.n, N_DEV - 1
        x, y, c = lax.axis_index("x"), lax.axis_index("y"), lax.axis_index("c")
        me, sibling = (x, y, c), (x, y, 1 - c)
        chips = [(1 - x, y), (x, 1 - y), (1 - x, 1 - y)]

        def copy(b, k, block, to, src=None):
            dst = outs[b].at[4 * block[0] + 2 * block[1] + block[2]]
            return pltpu.make_async_remote_copy(
                src_ref=dst if src is None else src, dst_ref=dst,
                send_sem=send_sems.at[b * per + k], recv_sem=recv_sems.at[b * per + k],
                device_id=to, device_id_type=pl.DeviceIdType.MESH)

        mine = [pltpu.make_async_copy(ins[b], outs[b].at[4 * x + 2 * y + c], local_sems.at[b]) for b in range(n)]
        first = []
        for b in range(n):
            first.append(copy(b, 0, me, sibling, src=ins[b]))
            first += [copy(b, 1 + j, me, (*chip, c), src=ins[b]) for j, chip in enumerate(chips)]

        def start():
            for cp in mine + first:
                cp.start()

        def finish():
            passed = []
            for j, chip in enumerate(chips):
                for b in range(n):
                    copy(b, 1 + j, (*chip, c), me).wait_recv()
                    passed.append(copy(b, 4 + j, (*chip, c), sibling))
                    passed[-1].start()
            for b in range(n):
                copy(b, 0, sibling, me).wait_recv()
                for j, chip in enumerate(chips):
                    copy(b, 4 + j, (*chip, 1 - c), me).wait_recv()
            for cp in first + passed:
                cp.wait_send()
            for cp in mine:
                cp.wait()

        return start, finish


MM_TILES = (3584, 2176, 2048, 1792, 1408, 1024, 512, 256, 128)
MM_VMEM_BUDGET = 40 * 1024 * 1024


def _mm_tiles(M, N, K, out_bytes):
    cm = [t for t in MM_TILES if M % t == 0]
    cn = [t for t in MM_TILES if N % t == 0]
    ck = [t for t in MM_TILES if K % t == 0]
    best = None
    for bm in cm[:2]:
        for bn in cn:
            for bk in ck:
                need = 4 * (bm * bk + bk * bn) + bm * bn * (4 + 2 * out_bytes)
                if need <= MM_VMEM_BUDGET:
                    score = (bm * bn * bk, bk)
                    if best is None or score > best[0]:
                        best = (score, (bm, bn, bk))
    return best[1]


def mm(a, b, mode, out_dtype, name, comm=None):
    if mode == "nn":
        (M, K), (_, N) = a.shape, b.shape
    elif mode == "nt":
        (M, K), (N, _) = a.shape, b.shape
    else:
        (K, M), (_, N) = a.shape, b.shape
    bm, bn, bk = _mm_tiles(M, N, K, jnp.dtype(out_dtype).itemsize)
    gm, gn, nk = M // bm, N // bn, K // bk
    cm = _Comm(comm)

    def body(*refs):
        (a_ref, b_ref), (o_ref,), scr, copies = cm.split(refs, 2, 1)
        i, j, k = pl.program_id(0), pl.program_id(1), pl.program_id(2)
        cm.start(copies, jnp.logical_and(jnp.logical_and(i == 0, j == 0), k == 0))
        p = _dot(a_ref[...], b_ref[...], mode)
        if nk == 1:
            o_ref[...] = p.astype(o_ref.dtype)
        else:
            acc_ref = scr[0]

            @pl.when(k == 0)
            def _():
                acc_ref[...] = p

            @pl.when(k > 0)
            def _():
                acc_ref[...] += p

            @pl.when(k == nk - 1)
            def _():
                o_ref[...] = acc_ref[...].astype(o_ref.dtype)

        cm.wait(copies, jnp.logical_and(jnp.logical_and(i == gm - 1, j == gn - 1), k == nk - 1))

    if mode == "nn":
        a_spec = pl.BlockSpec((bm, bk), lambda i, j, k: (i, k))
        b_spec = pl.BlockSpec((bk, bn), lambda i, j, k: (k, j))
    elif mode == "nt":
        a_spec = pl.BlockSpec((bm, bk), lambda i, j, k: (i, k))
        b_spec = pl.BlockSpec((bn, bk), lambda i, j, k: (j, k))
    else:
        a_spec = pl.BlockSpec((bk, bm), lambda i, j, k: (k, i))
        b_spec = pl.BlockSpec((bk, bn), lambda i, j, k: (k, j))
    sem = ("arbitrary",) * 3 if cm.n else ("parallel", "parallel", "arbitrary")
    res = pl.pallas_call(
        body, name=name, grid=(gm, gn, nk),
        in_specs=[a_spec, b_spec] + cm.in_specs(),
        out_specs=[pl.BlockSpec((bm, bn), lambda i, j, k: (i, j))] + cm.out_specs(),
        out_shape=[jax.ShapeDtypeStruct((M, N), out_dtype)] + cm.out_shape(),
        scratch_shapes=([pltpu.VMEM((bm, bn), f32)] if nk > 1 else []) + cm.scratch(),
        compiler_params=_cp(*sem, side_effects=bool(cm.n)),
    )(a, b, *cm.bufs)
    return (res[0], res[1:]) if cm.n else res[0]


def _groups(ts, fn, carry=None, reverse=False, unroll=8, rg=RG):
    n = ts // rg
    if n == 1:
        return fn(0, carry)
    unroll = min(unroll, n)
    span = rg * unroll

    def body(g, c):
        r0 = pl.multiple_of((n // unroll - 1 - g if reverse else g) * span, span)
        for u in (range(unroll - 1, -1, -1) if reverse else range(unroll)):
            c = fn(pl.multiple_of(r0 + u * rg, rg), c)
        return c

    return lax.fori_loop(0, n // unroll, body, carry)


def _rms(x):
    return lax.rsqrt(jnp.mean(x * x, axis=-1, keepdims=True) + EPS)


def _rms_bwd(x, r, dn):
    n = x * r
    return r * (dn - n * jnp.mean(dn * n, axis=-1, keepdims=True))


NRG = 256


def _fold(x):
    return jnp.sum(x.reshape(x.shape[0] // 8, 8, x.shape[1]), axis=0)


def _flush(acc_ref, out_ref, row):
    out_ref[row:row + 1, :] = jnp.sum(acc_ref[...], axis=0, keepdims=True)


def prenorm_fwd(x, w, pos_row, ts, gather):
    S = x.shape[0]
    nt = S // ts
    tg = _TwoLevelGather(gather)
    n = tg.n
    half = AD // 2
    inv = ROPE_THETA ** (-jnp.arange(half, dtype=f32) * 2.0 / AD)
    inv_col = jnp.tile(inv, 2)[:, None]

    def body(x_ref, w_ref, p_ref, inv_ref, *refs):
        u_ref, cos_ref, sin_ref = refs[n:n + 3]
        start, finish = tg.bind(refs[:n], refs[n + 3:2 * n + 3], *refs[2 * n + 3:])
        i = pl.program_id(0)
        pl.when(i == 0)(start)
        wv = w_ref[...]

        def grp(r0, _):
            xv = x_ref[pl.ds(r0, NRG), :]
            u_ref[pl.ds(r0, NRG), :] = (xv * _rms(xv) * wv).astype(bf16)

        _groups(ts, grp, rg=NRG)
        ang = inv_ref[...] * p_ref[...].astype(f32)
        row = lax.broadcasted_iota(jnp.int32, ang.shape, 0)
        cos_ref[...] = jnp.cos(ang)
        sin_ref[...] = jnp.where(row < half, -1.0, 1.0) * jnp.sin(ang)
        pl.when(i == nt - 1)(finish)

    tab = pl.BlockSpec((AD, ts), lambda i: (0, i))
    res = pl.pallas_call(
        body, name="prenorm_fwd", grid=(nt,),
        in_specs=[pl.BlockSpec((ts, D), lambda i: (i, 0)), pl.BlockSpec((1, D), lambda i: (0, 0)),
                  pl.BlockSpec((1, ts), lambda i: (0, i)), pl.BlockSpec((AD, 1), lambda i: (0, 0))] + tg.in_specs(),
        out_specs=[pl.BlockSpec((ts, D), lambda i: (i, 0)), tab, tab] + tg.out_specs(),
        out_shape=[jax.ShapeDtypeStruct((S, D), bf16), jax.ShapeDtypeStruct((AD, S), f32),
                   jax.ShapeDtypeStruct((AD, S), f32)] + tg.out_shape(),
        scratch_shapes=tg.scratch(),
        compiler_params=_cp("arbitrary", side_effects=True),
    )(x, w, pos_row, inv_col, *tg.bufs)
    return res[0], res[1], res[2], res[3:]


def prenorm_bwd(x, w, dus, dx1, ts, comm=None):
    S = x.shape[0]
    nt = S // ts
    nd = len(dus)
    cm = _Comm(comm)

    def body(*refs):
        ins, (gx_ref, gw_ref), (acc_ref,), copies = cm.split(refs, nd + 3, 2)
        x_ref, w_ref = ins[:2]
        du_refs, dx1_ref = ins[2:2 + nd], ins[2 + nd]
        i = pl.program_id(0)
        cm.start(copies, i == 0)
        wv = w_ref[...]

        @pl.when(i == 0)
        def _():
            acc_ref[...] = jnp.zeros_like(acc_ref)
            gw_ref[...] = jnp.zeros_like(gw_ref)

        def grp(r0, _):
            rows = pl.ds(r0, NRG)
            xv = x_ref[rows, :]
            r = _rms(xv)
            du = du_refs[0][rows, :].astype(f32)
            for d_ref in du_refs[1:]:
                du = du + d_ref[rows, :].astype(f32)
            gx_ref[rows, :] = dx1_ref[rows, :] + _rms_bwd(xv, r, du * wv)
            acc_ref[...] += _fold(du * xv * r)

        _groups(ts, grp, rg=NRG)

        @pl.when(i == nt - 1)
        def _():
            _flush(acc_ref, gw_ref, 0)

        cm.wait(copies, i == nt - 1)

    row = pl.BlockSpec((ts, D), lambda i: (i, 0))
    res = pl.pallas_call(
        body, name="prenorm_bwd", grid=(nt,),
        in_specs=[row, pl.BlockSpec((1, D), lambda i: (0, 0))] + [row] * (nd + 1) + cm.in_specs(),
        out_specs=[row, pl.BlockSpec((8, D), lambda i: (0, 0))] + cm.out_specs(),
        out_shape=[jax.ShapeDtypeStruct((S, D), f32), jax.ShapeDtypeStruct((8, D), f32)] + cm.out_shape(),
        scratch_shapes=[pltpu.VMEM((8, D), f32)] + cm.scratch(),
        compiler_params=_cp("arbitrary", side_effects=bool(cm.n)),
    )(x, w, *dus, dx1, *cm.bufs)
    return res[0], res[1], res[2:]


def post_fwd(x, mo, w_post, w_pre2, ts):
    S = x.shape[0]

    def body(x_ref, mo_ref, wp_ref, w2_ref, x1_ref, h_ref):
        wp, w2 = wp_ref[...], w2_ref[...]

        def grp(r0, _):
            rows = pl.ds(r0, NRG)
            mv = mo_ref[rows, :].astype(f32)
            x1 = x_ref[rows, :] + mv * _rms(mv) * wp
            x1_ref[rows, :] = x1
            h_ref[rows, :] = (x1 * _rms(x1) * w2).astype(bf16)

        _groups(ts, grp, rg=NRG)

    row = pl.BlockSpec((ts, D), lambda i: (i, 0))
    par = pl.BlockSpec((1, D), lambda i: (0, 0))
    return pl.pallas_call(
        body, name="post_fwd", grid=(S // ts,),
        in_specs=[row, row, par, par], out_specs=[row, row],
        out_shape=[jax.ShapeDtypeStruct((S, D), f32), jax.ShapeDtypeStruct((S, D), bf16)],
        compiler_params=_cp("parallel"),
    )(x, mo, w_post, w_pre2)


def post_bwd(dout, dh, x1, mo, w_post, w_pre2, ts):
    S = x1.shape[0]
    nt = S // ts

    def body(dout_ref, dh_ref, x1_ref, mo_ref, wp_ref, w2_ref, dx1_ref, dmo_ref, gw_ref, acc2_ref, accp_ref):
        i = pl.program_id(0)
        wp, w2 = wp_ref[...], w2_ref[...]

        @pl.when(i == 0)
        def _():
            acc2_ref[...] = jnp.zeros_like(acc2_ref)
            accp_ref[...] = jnp.zeros_like(accp_ref)
            gw_ref[...] = jnp.zeros_like(gw_ref)

        def grp(r0, _):
            rows = pl.ds(r0, NRG)
            x1 = x1_ref[rows, :]
            r1 = _rms(x1)
            dh = dh_ref[rows, :].astype(f32)
            dx1 = dout_ref[rows, :] + _rms_bwd(x1, r1, dh * w2)
            dx1_ref[rows, :] = dx1
            acc2_ref[...] += _fold(dh * x1 * r1)
            mv = mo_ref[rows, :].astype(f32)
            rm = _rms(mv)
            dmo_ref[rows, :] = _rms_bwd(mv, rm, dx1 * wp).astype(bf16)
            accp_ref[...] += _fold(dx1 * mv * rm)

        _groups(ts, grp, rg=NRG)

        @pl.when(i == nt - 1)
        def _():
            _flush(acc2_ref, gw_ref, 0)
            _flush(accp_ref, gw_ref, 1)

    row = pl.BlockSpec((ts, D), lambda i: (i, 0))
    par = pl.BlockSpec((1, D), lambda i: (0, 0))
    return pl.pallas_call(
        body, name="post_bwd", grid=(nt,),
        in_specs=[row, row, row, row, par, par],
        out_specs=[row, row, pl.BlockSpec((8, D), lambda i: (0, 0))],
        out_shape=[jax.ShapeDtypeStruct((S, D), f32), jax.ShapeDtypeStruct((S, D), bf16),
                   jax.ShapeDtypeStruct((8, D), f32)],
        scratch_shapes=[pltpu.VMEM((8, D), f32), pltpu.VMEM((8, D), f32)],
        compiler_params=_cp("arbitrary"),
    )(dout, dh, x1, mo, w_post, w_pre2)


def loss_head(x1, ff, target, w, ts):
    S = x1.shape[0]
    nt = S // ts

    def body(x1_ref, ff_ref, t_ref, w_ref, loss_ref, dout_ref, dff_ref, gw_ref, accw_ref, accl_ref):
        i = pl.program_id(0)
        wv = w_ref[...]

        @pl.when(i == 0)
        def _():
            accw_ref[...] = jnp.zeros_like(accw_ref)
            accl_ref[...] = jnp.zeros_like(accl_ref)
            gw_ref[...] = jnp.zeros_like(gw_ref)

        def grp(r0, _):
            rows = pl.ds(r0, NRG)
            fv = ff_ref[rows, :].astype(f32)
            r = _rms(fv)
            n = fv * r
            e = x1_ref[rows, :] + n * wv - t_ref[rows, :]
            dout = e * (1.0 / D)
            dout_ref[rows, :] = dout
            dff_ref[rows, :] = _rms_bwd(fv, r, dout * wv).astype(bf16)
            accw_ref[...] += _fold(dout * n)
            accl_ref[...] += _fold(e * e)

        _groups(ts, grp, rg=NRG)

        @pl.when(i == nt - 1)
        def _():
            _flush(accw_ref, gw_ref, 0)
            tot = jnp.sum(jnp.sum(accl_ref[...], axis=1, keepdims=True), axis=0, keepdims=True) * (0.5 / D)
            loss_ref[...] = jnp.broadcast_to(tot, loss_ref.shape)

    row = pl.BlockSpec((ts, D), lambda i: (i, 0))
    return pl.pallas_call(
        body, name="loss_head", grid=(nt,),
        in_specs=[row, row, row, pl.BlockSpec((1, D), lambda i: (0, 0))],
        out_specs=[pl.BlockSpec((8, LANES), lambda i: (0, 0)), row, row, pl.BlockSpec((8, D), lambda i: (0, 0))],
        out_shape=[jax.ShapeDtypeStruct((8, LANES), f32), jax.ShapeDtypeStruct((S, D), f32),
                   jax.ShapeDtypeStruct((S, D), bf16), jax.ShapeDtypeStruct((8, D), f32)],
        scratch_shapes=[pltpu.VMEM((8, D), f32), pltpu.VMEM((8, D), f32)],
        compiler_params=_cp("arbitrary"),
    )(x1, ff, target, w)


def _taps(w_ref, cs, K):
    return [jnp.broadcast_to(w_ref[k:k + 1, cs], (8, CW)) for k in range(K)]


def _down(before, cur, s, sub):
    return jnp.where(sub < s, pltpu.roll(before, s, 0), pltpu.roll(cur, s, 0))


def _up(cur, after, s, sub):
    return jnp.where(sub < 8 - s, pltpu.roll(cur, 8 - s, 0), pltpu.roll(after, 8 - s, 0))


def _conv_group(p, a, b, taps, bias, K, sub):
    ya, yb = bias, bias
    for k in range(K):
        s = K - 1 - k
        xa, xb = (a, b) if s == 0 else (_down(p, a, s, sub), _down(a, b, s, sub))
        ya = ya + taps[k] * xa
        yb = yb + taps[k] * xb
    return ya, yb


def _prev8_map(ts, cb):
    return lambda i, j: (jnp.maximum(i * (ts // 8) - 1, 0), cb + j)


def ssdconv_fwd(proj, w8, b, ts):
    S = proj.shape[0]
    bw = 1024
    cb = C_XBC // bw

    def body(cur_ref, prev_ref, w_ref, b_ref, o_ref, c_ref):
        first = pl.program_id(0) == 0
        sub = lax.broadcasted_iota(jnp.int32, (8, CW), 0)
        for c0 in range(0, bw, CW):
            cs = slice(c0, c0 + CW)
            taps = _taps(w_ref, cs, SSD_K)
            bias = jnp.broadcast_to(b_ref[:, cs], (8, CW))

            def grp(r0, p, cs=cs, taps=taps, bias=bias):
                rows = pl.ds(r0, RG)
                xv = cur_ref[rows, cs].astype(f32)
                ya, yb = _conv_group(p, xv[0:8], xv[8:16], taps, bias, SSD_K, sub)
                y = jnp.concatenate([ya, yb], axis=0)
                c_ref[rows, cs] = y.astype(bf16)
                o_ref[rows, cs] = (y * _sigmoid_fast(y)).astype(bf16)
                return xv[8:16]

            _groups(ts, grp, jnp.where(first, 0.0, prev_ref[:, cs].astype(f32)))

    o = jax.ShapeDtypeStruct((S, CONVD), bf16)
    blk = pl.BlockSpec((ts, bw), lambda i, j: (i, j))
    return pl.pallas_call(
        body, name="ssdconv_fwd", grid=(S // ts, CONVD // bw),
        in_specs=[pl.BlockSpec((ts, bw), lambda i, j: (i, cb + j)),
                  pl.BlockSpec((8, bw), _prev8_map(ts, cb)),
                  pl.BlockSpec((8, bw), lambda i, j: (0, j)),
                  pl.BlockSpec((1, bw), lambda i, j: (0, j))],
        out_specs=[blk, blk], out_shape=[o, o],
        compiler_params=_cp("parallel", "parallel"),
    )(proj, proj, w8, b)


def _gelu_tanh(x):
    c = 0.7978845608028654
    t = jnp.tanh(c * (x + 0.044715 * x * x * x))
    return 0.5 * x * (1.0 + t), t


def ffnact_fwd(up, w8, b, ts):
    S = up.shape[0]

    def body(g_ref, gp_ref, v_ref, vp_ref, wg_ref, wv_ref, bg_ref, bv_ref, o_ref, gc_ref, vc_ref):
        first = pl.program_id(0) == 0
        sub = lax.broadcasted_iota(jnp.int32, (8, CW), 0)
        for c0 in range(0, FF, CW):
            cs = slice(c0, c0 + CW)
            tg, tv = _taps(wg_ref, cs, FFN_K), _taps(wv_ref, cs, FFN_K)
            bg = jnp.broadcast_to(bg_ref[:, cs], (8, CW))
            bv = jnp.broadcast_to(bv_ref[:, cs], (8, CW))

            def grp(r0, carry, cs=cs, tg=tg, tv=tv, bg=bg, bv=bv):
                pg, pv = carry
                rows = pl.ds(r0, RG)
                gx = g_ref[rows, cs].astype(f32)
                vx = v_ref[rows, cs].astype(f32)
                g = jnp.concatenate(_conv_group(pg, gx[0:8], gx[8:16], tg, bg, FFN_K, sub), axis=0)
                v = jnp.concatenate(_conv_group(pv, vx[0:8], vx[8:16], tv, bv, FFN_K, sub), axis=0)
                gc_ref[rows, cs] = g.astype(bf16)
                vc_ref[rows, cs] = v.astype(bf16)
                o_ref[rows, cs] = (_gelu_tanh(g)[0] * v).astype(bf16)
                return gx[8:16], vx[8:16]

            _groups(ts, grp, (jnp.where(first, 0.0, gp_ref[:, cs].astype(f32)),
                              jnp.where(first, 0.0, vp_ref[:, cs].astype(f32))))

    o = jax.ShapeDtypeStruct((S, FF), bf16)
    blk = pl.BlockSpec((ts, FF), lambda i: (i, 0))
    prev = lambda cb: pl.BlockSpec((8, FF), lambda i: (jnp.maximum(i * (ts // 8) - 1, 0), cb))
    return pl.pallas_call(
        body, name="ffnact_fwd", grid=(S // ts,),
        in_specs=[blk, prev(0), pl.BlockSpec((ts, FF), lambda i: (i, 1)), prev(1),
                  pl.BlockSpec((8, FF), lambda i: (0, 0)), pl.BlockSpec((8, FF), lambda i: (0, 1)),
                  pl.BlockSpec((1, FF), lambda i: (0, 0)), pl.BlockSpec((1, FF), lambda i: (0, 1))],
        out_specs=[blk, blk, blk], out_shape=[o, o, o],
        compiler_params=_cp("parallel"),
    )(up, up, up, up, w8, w8, b, b)


def ffnact_bwd(dact, gc, vc, ts):
    S = dact.shape[0]

    def body(d_ref, g_ref, v_ref, dg_ref, dv_ref):
        c = 0.7978845608028654
        for c0 in range(0, FF, CW):
            cs = slice(c0, c0 + CW)

            def grp(r0, _, cs=cs):
                rows = pl.ds(r0, RG)
                d = d_ref[rows, cs].astype(f32)
                g = g_ref[rows, cs].astype(f32)
                ge, t = _gelu_tanh(g)
                dgelu = 0.5 * (1.0 + t) + 0.5 * g * (1.0 - t * t) * c * (1.0 + 3.0 * 0.044715 * g * g)
                dg_ref[rows, cs] = (d * v_ref[rows, cs].astype(f32) * dgelu).astype(bf16)
                dv_ref[rows, cs] = (d * ge).astype(bf16)

            _groups(ts, grp)

    o = jax.ShapeDtypeStruct((S, FF), bf16)
    blk = pl.BlockSpec((ts, FF), lambda i: (i, 0))
    return pl.pallas_call(
        body, name="ffnact_bwd", grid=(S // ts,),
        in_specs=[blk, blk, blk], out_specs=[blk, blk], out_shape=[o, o],
        compiler_params=_cp("parallel"),
    )(dact, gc, vc)


def dwconv_bwd(dy, x, xcb, w8, wcb, K, bw, ts, name, act_c=None, into=None, ocb=0, out_cols=None):
    S, C = dy.shape
    nr = S // ts
    out_cols = out_cols or C
    n_act = 0 if act_c is None else 2

    def body(*refs):
        dy_ref, dyn_ref = refs[0:2]
        c_ref, cn_ref = (refs[2:4] if n_act else (None, None))
        x_ref, xp_ref, w_ref = refs[2 + n_act:5 + n_act]
        dx_ref, dw_ref, sd_ref = refs[-3:]
        i = pl.program_id(1)
        first, last = i == 0, i == nr - 1
        sub = lax.broadcasted_iota(jnp.int32, (8, CW), 0)

        def grad_y(d, c):
            if c is None:
                return d.astype(f32)
            cv = c.astype(f32)
            s = _sigmoid_fast(cv)
            return d.astype(f32) * s * (1.0 + cv * (1.0 - s))

        @pl.when(first)
        def _():
            dw_ref[...] = jnp.zeros_like(dw_ref)

        for c0 in range(0, bw, CW):
            cs = slice(c0, c0 + CW)
            taps = _taps(w_ref, cs, K)
            zero = jnp.zeros((8, CW), f32)

            def fwd(r0, carry, cs=cs):
                p, accs, accb = carry
                rows = pl.ds(r0, RG)
                g = grad_y(dy_ref[rows, cs], c_ref[rows, cs] if n_act else None)
                sd_ref[rows, cs] = g
                xv = x_ref[rows, cs].astype(f32)
                a, b = xv[0:8], xv[8:16]
                ga, gb = g[0:8], g[8:16]
                new = []
                for k in range(K):
                    s = K - 1 - k
                    xa, xb = (a, b) if s == 0 else (_down(p, a, s, sub), _down(a, b, s, sub))
                    new.append(accs[k] + ga * xa + gb * xb)
                return b, tuple(new), accb + ga + gb

            _, accs, accb = _groups(ts, fwd, (jnp.where(first, 0.0, xp_ref[:, cs].astype(f32)), (zero,) * K, zero))
            for k in range(K):
                dw_ref[k:k + 1, cs] += jnp.sum(accs[k], axis=0, keepdims=True)
            dw_ref[7:8, cs] += jnp.sum(accb, axis=0, keepdims=True)

            def bwd(r0, after, cs=cs, taps=taps):
                rows = pl.ds(r0, RG)
                g = sd_ref[rows, cs]
                a, b = g[0:8], g[8:16]
                da, db = zero, zero
                for k in range(K):
                    s = K - 1 - k
                    ua, ub = (a, b) if s == 0 else (_up(a, b, s, sub), _up(b, after, s, sub))
                    da = da + taps[k] * ua
                    db = db + taps[k] * ub
                dx_ref[rows, cs] = jnp.concatenate([da, db], axis=0).astype(bf16)
                return a

            halo = grad_y(dyn_ref[:, cs], cn_ref[:, cs] if n_act else None)
            _groups(ts, bwd, jnp.where(last, 0.0, halo), reverse=True)

    nxt = lambda j, i: (jnp.minimum((i + 1) * (ts // 8), S // 8 - 1), j)
    tile = pl.BlockSpec((ts, bw), lambda j, i: (i, j))
    acts = [] if act_c is None else [act_c, act_c]
    extra = [] if into is None else [into]
    n_in = 5 + n_act
    return pl.pallas_call(
        body, name=name, grid=(C // bw, nr),
        in_specs=[tile, pl.BlockSpec((8, bw), nxt)] + ([tile, pl.BlockSpec((8, bw), nxt)] if n_act else []) + [
            pl.BlockSpec((ts, bw), lambda j, i: (i, xcb + j)),
            pl.BlockSpec((8, bw), lambda j, i: (jnp.maximum(i * (ts // 8) - 1, 0), xcb + j)),
            pl.BlockSpec((8, bw), lambda j, i: (0, wcb + j))] + [pl.BlockSpec(memory_space=pl.ANY)] * len(extra),
        out_specs=[pl.BlockSpec((ts, bw), lambda j, i: (i, ocb + j)), pl.BlockSpec((8, bw), lambda j, i: (0, j))],
        out_shape=[jax.ShapeDtypeStruct((S, out_cols), bf16), jax.ShapeDtypeStruct((8, C), f32)],
        scratch_shapes=[pltpu.VMEM((ts, bw), f32)],
        input_output_aliases={n_in: 0} if extra else {},
        compiler_params=_cp("parallel", "arbitrary"),
    )(dy, dy, *acts, x, x, w8, *extra)


def gnorm_fwd(y, proj, w, ts):
    S = y.shape[0]

    def body(y_ref, z_ref, w_ref, o_ref):
        for k in range(NG):
            sl = slice(k * GW, (k + 1) * GW)
            wv = w_ref[:, sl]

            def grp(r0, _, sl=sl, wv=wv):
                rows = pl.ds(r0, NRG)
                z = z_ref[rows, sl].astype(f32)
                g = y_ref[rows, sl].astype(f32) * z * _sigmoid_fast(z)
                o_ref[rows, sl] = (g * _rms(g) * wv).astype(bf16)

            _groups(ts, grp, rg=NRG)

    row = pl.BlockSpec((ts, DI), lambda i: (i, 0))
    return pl.pallas_call(
        body, name="gnorm_fwd", grid=(S // ts,),
        in_specs=[row, row, pl.BlockSpec((1, DI), lambda i: (0, 0))],
        out_specs=row, out_shape=jax.ShapeDtypeStruct((S, DI), bf16),
        compiler_params=_cp("parallel"),
    )(y, proj, w)


def gnorm_bwd(dyn, y, proj, w, dproj, ts):
    S = y.shape[0]
    nt = S // ts

    def body(d_ref, y_ref, z_ref, w_ref, _, dy_ref, dz_ref, gw_ref, acc_ref):
        i = pl.program_id(0)

        @pl.when(i == 0)
        def _():
            acc_ref[...] = jnp.zeros_like(acc_ref)
            gw_ref[...] = jnp.zeros_like(gw_ref)

        for k in range(NG):
            sl = slice(k * GW, (k + 1) * GW)
            wv = w_ref[:, sl]

            def grp(r0, _, sl=sl, wv=wv):
                rows = pl.ds(r0, NRG)
                z = z_ref[rows, sl].astype(f32)
                yv = y_ref[rows, sl].astype(f32)
                s = _sigmoid_fast(z)
                sz = z * s
                g = yv * sz
                r = _rms(g)
                d = d_ref[rows, sl].astype(f32)
                acc_ref[:, sl] += _fold(d * g * r)
                dg = _rms_bwd(g, r, d * wv)
                dy_ref[rows, sl] = (dg * sz).astype(bf16)
                dz_ref[rows, sl] = (dg * yv * s * (1.0 + z * (1.0 - s))).astype(bf16)

            _groups(ts, grp, rg=NRG)

        @pl.when(i == nt - 1)
        def _():
            _flush(acc_ref, gw_ref, 0)

    row = pl.BlockSpec((ts, DI), lambda i: (i, 0))
    return pl.pallas_call(
        body, name="gnorm_bwd", grid=(nt,),
        in_specs=[row, row, row, pl.BlockSpec((1, DI), lambda i: (0, 0)), pl.BlockSpec(memory_space=pl.ANY)],
        out_specs=[row, row, pl.BlockSpec((8, DI), lambda i: (0, 0))],
        out_shape=[jax.ShapeDtypeStruct((S, DI), bf16), jax.ShapeDtypeStruct(dproj.shape, bf16),
                   jax.ShapeDtypeStruct((8, DI), f32)],
        scratch_shapes=[pltpu.VMEM((8, DI), f32)],
        input_output_aliases={4: 1},
        compiler_params=_cp("arbitrary"),
    )(dyn, y, proj, w, dproj)


def merge_fwd(proj, ys, ya, ts):
    S = ys.shape[0]

    def body(gs_ref, ga_ref, ys_ref, ya_ref, o_ref):
        for c0 in range(0, D, CW):
            cs = slice(c0, c0 + CW)

            def grp(r0, _, cs=cs):
                rows = pl.ds(r0, NRG)
                o_ref[rows, cs] = (_sigmoid_fast(gs_ref[rows, cs].astype(f32)) * ys_ref[rows, cs].astype(f32)
                                   + _sigmoid_fast(ga_ref[rows, cs].astype(f32)) * ya_ref[rows, cs].astype(f32)
                                   ).astype(bf16)

            _groups(ts, grp, rg=NRG)

    row = pl.BlockSpec((ts, D), lambda i: (i, 0))
    return pl.pallas_call(
        body, name="merge_fwd", grid=(S // ts,),
        in_specs=[pl.BlockSpec((ts, D), lambda i: (i, C_GS // D)), pl.BlockSpec((ts, D), lambda i: (i, C_GA // D)), row, row],
        out_specs=row, out_shape=jax.ShapeDtypeStruct((S, D), bf16),
        compiler_params=_cp("parallel"),
    )(proj, proj, ys, ya)


def merge_bwd(dm, proj, ys, ya, ts):
    S = ys.shape[0]

    def body(d_ref, gs_ref, ga_ref, ys_ref, ya_ref, dys_ref, dya_ref, dg_ref):
        for c0 in range(0, D, CW):
            cs = slice(c0, c0 + CW)

            def grp(r0, _, c0=c0, cs=cs):
                rows = pl.ds(r0, NRG)
                d = d_ref[rows, cs].astype(f32)
                ss = _sigmoid_fast(gs_ref[rows, cs].astype(f32))
                sa = _sigmoid_fast(ga_ref[rows, cs].astype(f32))
                dys_ref[rows, cs] = (d * ss).astype(bf16)
                dya_ref[rows, cs] = (d * sa).astype(bf16)
                dg_ref[rows, cs] = (d * ys_ref[rows, cs].astype(f32) * ss * (1.0 - ss)).astype(bf16)
                dg_ref[rows, D + c0:D + c0 + CW] = (d * ya_ref[rows, cs].astype(f32) * sa * (1.0 - sa)).astype(bf16)

            _groups(ts, grp, rg=NRG)

    row = pl.BlockSpec((ts, D), lambda i: (i, 0))
    o = jax.ShapeDtypeStruct((S, D), bf16)
    return pl.pallas_call(
        body, name="merge_bwd", grid=(S // ts,),
        in_specs=[row, pl.BlockSpec((ts, D), lambda i: (i, C_GS // D)), pl.BlockSpec((ts, D), lambda i: (i, C_GA // D)), row, row],
        out_specs=[row, row, pl.BlockSpec((ts, 2 * D), lambda i: (i, C_GS // (2 * D)))],
        out_shape=[o, o, jax.ShapeDtypeStruct((S, PM), bf16)],
        compiler_params=_cp("parallel"),
    )(dm, proj, proj, ys, ya)


def _ssd_consts():
    h = lax.broadcasted_iota(jnp.int32, (LANES, DI), 0)
    c = lax.broadcasted_iota(jnp.int32, (LANES, DI), 1)
    expand = (c // HD == h).astype(bf16)
    r = lax.broadcasted_iota(jnp.int32, (CH, CH), 0)
    cc = lax.broadcasted_iota(jnp.int32, (CH, CH), 1)
    tril = (cc <= r).astype(bf16)
    triu = (cc >= r).astype(bf16)
    return expand, expand.T, tril, triu


def _ssd_common(xbc_ref, dtr_ref, bias_ref, alog_ref, tril_ref, expand_ref=None, saved=None):
    dtr = dtr_ref[...] + bias_ref[...]
    dt = jnp.maximum(dtr, 0.0) + jnp.log1p(jnp.exp(-jnp.abs(dtr)))
    a = -jnp.exp(alog_ref[...])
    acs = _dot3_left(tril_ref[...], dt * a)
    if saved is None:
        acsx = _dot3_right(acs, expand_ref[...])
        dtx = _dot3_right(dt, expand_ref[...])
    else:
        acsx, dtx = saved[0][...], saved[1][...]
    x = xbc_ref[:, 0:DI].astype(f32)
    xdt = x * dtx
    e = jnp.exp(acsx)
    dsx = jnp.exp(acsx[CH - 1:CH, :] - acsx)
    return dtr, dt, a, acs, acsx, dtx, x, xdt, e, dsx


def _head_halves():
    first = lax.broadcasted_iota(jnp.int32, (CH, LANES), 1) < HD
    return first, jnp.logical_not(first)


def _ssd_lmat(acs, acs_t, hh, causal):
    seg = acs[:, hh:hh + 1] - acs_t[hh:hh + 1, :]
    return jnp.where(causal, jnp.exp(jnp.minimum(seg, 0.0)), 0.0)


def ssd_fwd(xbc, dtr, bias, alog, dx_row, comm=None):
    S = xbc.shape[0]
    nc = S // CH
    expand, _, tril, _ = _ssd_consts()
    cm = _Comm(comm)

    def body(*refs):
        ins, (y_ref, hp_ref, ax_ref, dtx_ref), (h_ref, yd_ref), copies = cm.split(refs, 7, 4)
        xbc_ref, dtr_ref, bias_ref, alog_ref, dxr_ref, expand_ref, tril_ref = ins
        c = pl.program_id(0)
        cm.start(copies, c == 0)

        @pl.when(c == 0)
        def _():
            h_ref[...] = jnp.zeros_like(h_ref)

        _, _, _, acs, acsx, dtx, x, xdt, e, dsx = _ssd_common(xbc_ref, dtr_ref, bias_ref, alog_ref, tril_ref,
                                                              expand_ref=expand_ref)
        ax_ref[...] = acsx
        dtx_ref[...] = dtx
        acs_t = acs.T
        xb = xdt.astype(bf16)
        xd = (xdt * dsx).astype(bf16)
        causal = tril_ref[...] > 0
        halves = _head_halves()
        for g in range(NG):
            gs = slice(g * GW, (g + 1) * GW)
            bg = xbc_ref[:, DI + g * NS:DI + (g + 1) * NS]
            cg = xbc_ref[:, DI + NG * NS + g * NS:DI + NG * NS + (g + 1) * NS]
            cb = _dot(cg, bg, "nt")
            hp = h_ref[g]
            hpb = hp.astype(bf16)
            hp_ref[0, g] = hpb
            yd_ref[:, gs] = _dot(cg, hpb) * e[:, gs]
            h_ref[g] = hp * e[CH - 1:CH, gs] + _dot(bg, xd[:, gs], "tn")
            for k in range(NH // NG // 2):
                hh = g * (NH // NG) + 2 * k
                ps = slice(hh * HD, (hh + 2) * HD)
                xp = xb[:, ps]
                acc = None
                for o in range(2):
                    m = (cb * _ssd_lmat(acs, acs_t, hh + o, causal)).astype(bf16)
                    part = _dot(m, jnp.where(halves[o], xp, jnp.zeros_like(xp)))
                    acc = part if acc is None else acc + part
                yd_ref[:, ps] += acc
        y_ref[...] = (yd_ref[...] + dxr_ref[...] * x).astype(bf16)
        cm.wait(copies, c == nc - 1)

    par = lambda shape: pl.BlockSpec(shape, lambda c: (0,) * len(shape))
    res = pl.pallas_call(
        body, name="ssd_fwd", grid=(nc,),
        in_specs=[pl.BlockSpec((CH, CONVD), lambda c: (c, 0)), pl.BlockSpec((CH, LANES), lambda c: (c, 0)),
                  par((1, LANES)), par((1, LANES)), par((1, DI)), par((LANES, DI)), par((CH, CH))] + cm.in_specs(),
        out_specs=[pl.BlockSpec((CH, DI), lambda c: (c, 0)), pl.BlockSpec((1, NG, NS, GW), lambda c: (c, 0, 0, 0)),
                   pl.BlockSpec((CH, DI), lambda c: (c, 0)), pl.BlockSpec((CH, DI), lambda c: (c, 0))] + cm.out_specs(),
        out_shape=[jax.ShapeDtypeStruct((S, DI), bf16), jax.ShapeDtypeStruct((nc, NG, NS, GW), bf16),
                   jax.ShapeDtypeStruct((S, DI), f32), jax.ShapeDtypeStruct((S, DI), f32)] + cm.out_shape(),
        scratch_shapes=[pltpu.VMEM((NG, NS, GW), f32), pltpu.VMEM((CH, DI), f32)] + cm.scratch(),
        compiler_params=_cp("arbitrary", side_effects=bool(cm.n)),
    )(xbc, dtr, bias, alog, dx_row, expand, tril, *cm.bufs)
    return res[0], res[1], (res[2], res[3]), res[4:]


def ssd_bwd(xbc, dtr, dy, hprev, saved, bias, alog, dx_row, comm=None):
    S = xbc.shape[0]
    nc = S // CH
    _, expand_t, tril, triu = _ssd_consts()
    cm = _Comm(comm)

    def body(*refs):
        ins, outs, scr, copies = cm.split(refs, 12, 3)
        xbc_ref, dtr_ref, dy_ref, hp_ref, ax_ref, dtx_ref, bias_ref, alog_ref, dxr_ref, expt_ref, tril_ref, triu_ref = ins
        dxbc_ref, ddtr_ref, acc_ref = outs
        dh_ref, dxs_ref, t_ref, accb_ref, acca_ref, accd_ref, cc_ref, rr_ref = scr
        c = pl.program_id(0)
        cm.start(copies, c == 0)

        @pl.when(c == 0)
        def _():
            cc_ref[...] = jnp.zeros_like(cc_ref)
            rr_ref[...] = jnp.zeros_like(rr_ref)

        @pl.when(c == 0)
        def _():
            dh_ref[...] = jnp.zeros_like(dh_ref)
            accb_ref[...] = jnp.zeros_like(accb_ref)
            acca_ref[...] = jnp.zeros_like(acca_ref)
            accd_ref[...] = jnp.zeros_like(accd_ref)

        dtr, dt, a, acs, _, dtx, x, xdt, e, dsx = _ssd_common(xbc_ref, dtr_ref, bias_ref, alog_ref, tril_ref,
                                                              saved=(ax_ref, dtx_ref))
        acs_t = acs.T
        xb = xdt.astype(bf16)
        xdf = xdt * dsx
        xd = xdf.astype(bf16)
        dyv = dy_ref[...].astype(f32)
        dyb = dy_ref[...]
        dye = (dyv * e).astype(bf16)
        causal = tril_ref[...] > 0
        halves = _head_halves()
        last_row = lax.broadcasted_iota(jnp.int32, (CH, 1), 0) == CH - 1
        for g in range(NG):
            gs = slice(g * GW, (g + 1) * GW)
            bsl = slice(DI + g * NS, DI + (g + 1) * NS)
            csl = slice(DI + NG * NS + g * NS, DI + NG * NS + (g + 1) * NS)
            bg = xbc_ref[:, bsl]
            cg = xbc_ref[:, csl]
            cb = _dot(cg, bg, "nt")
            hpb = hp_ref[0, g]
            dhn = dh_ref[g]
            dhnb = dhn.astype(bf16)
            yoff = _dot(cg, hpb) * e[:, gs]
            dxd = _dot(bg, dhnb)
            t2 = dxd * xdf[:, gs]
            t3 = jnp.sum(dhn * hpb.astype(f32), axis=0, keepdims=True) * e[CH - 1:CH, gs]
            t_ref[:, gs] = dyv[:, gs] * yoff - t2 + jnp.where(last_row, jnp.sum(t2, axis=0, keepdims=True) + t3, 0.0)
            dxs_ref[:, gs] = dxd * dsx[:, gs]
            dcg = _dot(dye[:, gs], hpb, "nt")
            dbg = _dot(xd[:, gs], dhnb, "nt")
            dh_ref[g] = dhn * e[CH - 1:CH, gs] + _dot(cg, dye[:, gs], "tn")
            dcb = jnp.zeros((CH, CH), f32)
            for k in range(NH // NG // 2):
                hh0 = g * (NH // NG) + 2 * k
                ps = slice(hh0 * HD, (hh0 + 2) * HD)
                xp = xb[:, ps]
                dyp = dyb[:, ps]
                acc = None
                for o in range(2):
                    hh = hh0 + o
                    dyh = jnp.where(halves[o], dyp, jnp.zeros_like(dyp))
                    lm = _ssd_lmat(acs, acs_t, hh, causal)
                    m = cb * lm
                    dm = _dot(dyh, xp, "nt")
                    gm = dm * m
                    cc_ref[:, hh:hh + 1] = jnp.sum(gm, axis=1, keepdims=True)
                    rr_ref[hh:hh + 1, :] = jnp.sum(gm, axis=0, keepdims=True)
                    dcb = dcb + dm * lm
                    part = _dot(m.astype(bf16), dyh, "tn")
                    acc = part if acc is None else acc + part
                dxs_ref[:, ps] += acc
            dcbb = dcb.astype(bf16)
            dxbc_ref[:, csl] = (dcg + _dot(dcbb, bg)).astype(bf16)
            dxbc_ref[:, bsl] = (dbg + _dot(dcbb, cg, "tn")).astype(bf16)
        dxf = dxs_ref[...]
        dxbc_ref[:, 0:DI] = (dxf * dtx + dxr_ref[...] * dyv).astype(bf16)
        expt = expt_ref[...]
        dacs = cc_ref[...] - rr_ref[...].T + _dot2_right(t_ref[...], expt)
        dadt = _dot3_left(triu_ref[...], dacs)
        ddt = _dot2_right(dxf * x, expt) + dadt * a
        ddtr = ddt * _sigmoid(dtr)
        ddtr_ref[...] = ddtr
        accb_ref[...] += ddtr
        acca_ref[...] += dadt * dt
        accd_ref[...] += _dot2_right(dyv * x, expt)

        @pl.when(c == nc - 1)
        def _():
            acc_ref[...] = jnp.zeros_like(acc_ref)
            acc_ref[0:1, :] = jnp.sum(accb_ref[...], axis=0, keepdims=True)
            acc_ref[1:2, :] = jnp.sum(acca_ref[...], axis=0, keepdims=True) * a
            acc_ref[2:3, :] = jnp.sum(accd_ref[...], axis=0, keepdims=True)

        cm.wait(copies, c == nc - 1)

    par = lambda shape: pl.BlockSpec(shape, lambda c: (0,) * len(shape))
    rev = lambda c: (nc - 1 - c, 0)
    res = pl.pallas_call(
        body, name="ssd_bwd", grid=(nc,),
        in_specs=[pl.BlockSpec((CH, CONVD), rev), pl.BlockSpec((CH, LANES), rev), pl.BlockSpec((CH, DI), rev),
                  pl.BlockSpec((1, NG, NS, GW), lambda c: (nc - 1 - c, 0, 0, 0)),
                  pl.BlockSpec((CH, DI), rev), pl.BlockSpec((CH, DI), rev),
                  par((1, LANES)), par((1, LANES)), par((1, DI)), par((DI, LANES)),
                  par((CH, CH)), par((CH, CH))] + cm.in_specs(),
        out_specs=[pl.BlockSpec((CH, CONVD), rev), pl.BlockSpec((CH, LANES), rev), par((8, LANES))] + cm.out_specs(),
        out_shape=[jax.ShapeDtypeStruct((S, CONVD), bf16), jax.ShapeDtypeStruct((S, LANES), f32),
                   jax.ShapeDtypeStruct((8, LANES), f32)] + cm.out_shape(),
        scratch_shapes=[pltpu.VMEM((NG, NS, GW), f32), pltpu.VMEM((CH, DI), f32), pltpu.VMEM((CH, DI), f32),
                        pltpu.VMEM((CH, LANES), f32), pltpu.VMEM((CH, LANES), f32), pltpu.VMEM((CH, LANES), f32),
                        pltpu.VMEM((CH, LANES), f32), pltpu.VMEM((LANES, CH), f32)] + cm.scratch(),
        compiler_params=_cp("arbitrary", side_effects=bool(cm.n)),
    )(xbc, dtr, dy, hprev, *saved, bias, alog, dx_row, expand_t, tril, triu, *cm.bufs)
    return res[0], res[1], res[2], res[3:]


def _partner(t):
    half = AD // 2
    return jnp.concatenate([t[h * AD + o:h * AD + o + half] for h in range(t.shape[0] // AD) for o in (half, 0)], axis=0)


def _rope(t, cos, sin):
    reps = t.shape[0] // AD
    return t * jnp.tile(cos, (reps, 1)) + _partner(t) * jnp.tile(sin, (reps, 1))


def _rope_t(d, cos, sin):
    reps = d.shape[0] // AD
    return d * jnp.tile(cos, (reps, 1)) - _partner(d) * jnp.tile(sin, (reps, 1))


def _lanes_of_group(t, g):
    return jnp.concatenate([t[(g * REP + r) * AD:(g * REP + r + 1) * AD] for r in range(REP)], axis=1)


def _attn_probs(qg, k2, sink_ref, g, not_first):
    n = qg.shape[1]
    s = lax.broadcasted_iota(jnp.int32, (2 * WIN, n), 0)
    t = lax.broadcasted_iota(jnp.int32, (2 * WIN, n), 1) % WIN
    valid = jnp.logical_or(jnp.logical_and(jnp.logical_and(s < WIN, s > t), not_first),
                           jnp.logical_and(s >= WIN, s - WIN <= t))
    sink = jnp.concatenate([jnp.broadcast_to(sink_ref[0:1, g * REP + r:g * REP + r + 1], (1, WIN)) for r in range(REP)],
                           axis=1)
    sc = jnp.where(valid, _dot(k2, qg, "tn"), -1e30)
    m = jnp.maximum(jnp.max(sc, axis=0, keepdims=True), sink)
    p = jnp.exp(sc - m)
    ps = jnp.exp(sink - m)
    inv = 1.0 / (jnp.sum(p, axis=0, keepdims=True) + ps)
    return p * inv, ps * inv


def attn_fwd(qt, kvt, cos, sin, sinks):
    S = qt.shape[1]
    nb = S // WIN
    cur = lambda i: (0, i)
    prev = lambda i: (0, jnp.maximum(i - 1, 0))

    def body(q_ref, kv_ref, kvp_ref, cos_ref, sin_ref, cosp_ref, sinp_ref, sink_ref, o_ref):
        i = pl.program_id(0)
        q = (_rope(q_ref[...].astype(f32), cos_ref[...], sin_ref[...]) * (AD ** -0.5)).astype(bf16)
        kc = _rope(kv_ref[0:KVW, :].astype(f32), cos_ref[...], sin_ref[...]).astype(bf16)
        kp = _rope(kvp_ref[0:KVW, :].astype(f32), cosp_ref[...], sinp_ref[...]).astype(bf16)
        for g in range(KVH):
            ks = slice(g * AD, (g + 1) * AD)
            vs = slice(KVW + g * AD, KVW + (g + 1) * AD)
            k2 = jnp.concatenate([kp[ks], kc[ks]], axis=1)
            v2 = jnp.concatenate([kvp_ref[vs, :], kv_ref[vs, :]], axis=1)
            p, _ = _attn_probs(_lanes_of_group(q, g), k2, sink_ref, g, i > 0)
            o = _dot(v2, p.astype(bf16))
            for r in range(REP):
                h = g * REP + r
                o_ref[h * AD:(h + 1) * AD, :] = o[:, r * WIN:(r + 1) * WIN].astype(bf16)

    tab = pl.BlockSpec((AD, WIN), cur)
    tabp = pl.BlockSpec((AD, WIN), prev)
    return pl.pallas_call(
        body, name="attn_fwd", grid=(nb,),
        in_specs=[pl.BlockSpec((D, WIN), cur), pl.BlockSpec((2 * KVW, WIN), cur), pl.BlockSpec((2 * KVW, WIN), prev),
                  tab, tab, tabp, tabp, pl.BlockSpec((1, LANES), lambda i: (0, 0))],
        out_specs=pl.BlockSpec((D, WIN), cur),
        out_shape=jax.ShapeDtypeStruct((D, S), bf16),
        compiler_params=_cp("parallel"),
    )(qt, kvt, kvt, cos, sin, cos, sin, sinks)


def attn_bwd(qt, kvt, cos, sin, sinks, daot, comm=None):
    S = qt.shape[1]
    nb = S // WIN
    cur = lambda i: (0, jnp.minimum(i, nb - 1))
    prev = lambda i: (0, jnp.maximum(i - 1, 0))
    cm = _Comm(comm)

    def body(*refs):
        ins, (dq_ref, dkv_ref, ds_ref), scr, copies = cm.split(refs, 9, 3)
        q_ref, kv_ref, kvp_ref, cos_ref, sin_ref, cosp_ref, sinp_ref, sink_ref, do_ref = ins
        ck_ref, cv_ref, dqs_ref, dkp_ref, dvp_ref, dkc_ref, dvc_ref, accs_ref = scr
        i = pl.program_id(0)
        cm.start(copies, i == 0)

        @pl.when(i == 0)
        def _():
            ck_ref[...] = jnp.zeros_like(ck_ref)
            cv_ref[...] = jnp.zeros_like(cv_ref)
            accs_ref[...] = jnp.zeros_like(accs_ref)

        @pl.when(i == nb)
        def _():
            dkp_ref[...] = jnp.zeros_like(dkp_ref)
            dvp_ref[...] = jnp.zeros_like(dvp_ref)

        @pl.when(i < nb)
        def _():
            q = (_rope(q_ref[...].astype(f32), cos_ref[...], sin_ref[...]) * (AD ** -0.5)).astype(bf16)
            kc = _rope(kv_ref[0:KVW, :].astype(f32), cos_ref[...], sin_ref[...]).astype(bf16)
            kp = _rope(kvp_ref[0:KVW, :].astype(f32), cosp_ref[...], sinp_ref[...]).astype(bf16)
            do = do_ref[...]
            for g in range(KVH):
                ks = slice(g * AD, (g + 1) * AD)
                vs = slice(KVW + g * AD, KVW + (g + 1) * AD)
                qg = _lanes_of_group(q, g)
                dog = _lanes_of_group(do, g)
                k2 = jnp.concatenate([kp[ks], kc[ks]], axis=1)
                v2 = jnp.concatenate([kvp_ref[vs, :], kv_ref[vs, :]], axis=1)
                p, ps = _attn_probs(qg, k2, sink_ref, g, i > 0)
                dp = _dot(v2, dog, "tn")
                delta = jnp.sum(p * dp, axis=0, keepdims=True)
                ds = (p * (dp - delta)).astype(bf16)
                accs_ref[g:g + 1, :] -= ps * delta
                dqg = _dot(k2, ds) * (AD ** -0.5)
                for r in range(REP):
                    h = g * REP + r
                    dqs_ref[h * AD:(h + 1) * AD, :] = dqg[:, r * WIN:(r + 1) * WIN]
                dk2 = _dot(qg, ds, "nt")
                dv2 = _dot(dog, p.astype(bf16), "nt")
                dkp_ref[ks, :] = dk2[:, 0:WIN]
                dkc_ref[ks, :] = dk2[:, WIN:2 * WIN]
                dvp_ref[ks, :] = dv2[:, 0:WIN]
                dvc_ref[ks, :] = dv2[:, WIN:2 * WIN]
            dq_ref[...] = _rope_t(dqs_ref[...], cos_ref[...], sin_ref[...]).astype(bf16)

        dkv_ref[0:KVW, :] = _rope_t(ck_ref[...] + dkp_ref[...], cosp_ref[...], sinp_ref[...]).astype(bf16)
        dkv_ref[KVW:2 * KVW, :] = (cv_ref[...] + dvp_ref[...]).astype(bf16)

        @pl.when(i < nb)
        def _():
            ck_ref[...] = dkc_ref[...]
            cv_ref[...] = dvc_ref[...]

        @pl.when(i == nb)
        def _():
            lane = lax.broadcasted_iota(jnp.int32, (1, LANES), 1)
            row = jnp.zeros((1, LANES), f32)
            for h in range(AH):
                part = accs_ref[h // REP:h // REP + 1, (h % REP) * WIN:(h % REP + 1) * WIN]
                row = row + jnp.where(lane == h, jnp.sum(part, axis=1, keepdims=True), 0.0)
            ds_ref[...] = jnp.zeros_like(ds_ref)
            ds_ref[0:1, :] = row

        cm.wait(copies, i == nb)

    tab = pl.BlockSpec((AD, WIN), cur)
    tabp = pl.BlockSpec((AD, WIN), prev)
    kvs = lambda: pltpu.VMEM((KVW, WIN), f32)
    res = pl.pallas_call(
        body, name="attn_bwd", grid=(nb + 1,),
        in_specs=[pl.BlockSpec((D, WIN), cur), pl.BlockSpec((2 * KVW, WIN), cur), pl.BlockSpec((2 * KVW, WIN), prev),
                  tab, tab, tabp, tabp, pl.BlockSpec((1, LANES), lambda i: (0, 0)),
                  pl.BlockSpec((D, WIN), cur)] + cm.in_specs(),
        out_specs=[pl.BlockSpec((D, WIN), cur), pl.BlockSpec((2 * KVW, WIN), prev),
                   pl.BlockSpec((8, LANES), lambda i: (0, 0))] + cm.out_specs(),
        out_shape=[jax.ShapeDtypeStruct((D, S), bf16), jax.ShapeDtypeStruct((2 * KVW, S), bf16),
                   jax.ShapeDtypeStruct((8, LANES), f32)] + cm.out_shape(),
        scratch_shapes=[kvs(), kvs(), pltpu.VMEM((D, WIN), f32), kvs(), kvs(), kvs(), kvs(),
                        pltpu.VMEM((8, REP * WIN), f32)] + cm.scratch(),
        compiler_params=_cp("arbitrary", side_effects=bool(cm.n)),
    )(qt, kvt, kvt, cos, sin, cos, sin, sinks, daot, *cm.bufs)
    return res[0], res[1], res[2], res[3:]


ADAM_C1 = 1.0 / (1.0 - ADAM_B1 ** ADAM_STEP)
ADAM_C2 = 1.0 / (1.0 - ADAM_B2 ** ADAM_STEP)


def _adam_update(g, w, m, v):
    nm = ADAM_B1 * m + (1.0 - ADAM_B1) * g
    nv = ADAM_B2 * v + (1.0 - ADAM_B2) * (g * g)
    return -ADAM_LR * ((nm * ADAM_C1) / (jnp.sqrt(nv * ADAM_C2) + ADAM_EPS) + ADAM_WD * w), nm, nv


def adamw(parts, w, m, v, tr, name):
    n, R, C = parts.shape

    def body(p_ref, w_ref, m_ref, v_ref, g_ref, d_ref, nm_ref, nv_ref):
        def grp(g0, _):
            r0 = pl.multiple_of(g0 * RG, RG)
            rows = pl.ds(r0, RG)
            g = p_ref[0, rows, :].astype(f32)
            for k in range(1, n):
                g = g + p_ref[k, rows, :].astype(f32)
            d, nm, nv = _adam_update(g, w_ref[rows, :], m_ref[rows, :], v_ref[rows, :])
            g_ref[rows, :] = g
            d_ref[rows, :] = d
            nm_ref[rows, :] = nm
            nv_ref[rows, :] = nv
            return 0

        lax.fori_loop(0, tr // RG, grp, 0)

    row = pl.BlockSpec((tr, C), lambda i: (i, 0))
    o = jax.ShapeDtypeStruct((R, C), f32)
    return pl.pallas_call(
        body, name=name, grid=(R // tr,),
        in_specs=[pl.BlockSpec((n, tr, C), lambda i: (0, i, 0)), row, row, row],
        out_specs=[row, row, row, row], out_shape=[o, o, o, o],
        compiler_params=_cp("parallel"),
    )(parts, w, m, v)


SMALL_ROW = (("norm_mix_post_w", D), ("norm_ffn_pre_w", D), ("norm_ffn_post_w", D), ("ssd_norm_w", DI),
             ("ssd_conv_b", CONVD), ("ffn_conv_b", 2 * FF), ("ssd_dt_bias", NH), ("ssd_a_log", NH), ("ssd_d", NH),
             ("attn_sinks", AH), ("loss", 1))
CONV_BLOCK = 1152
SSD_CONV_COLS = CONVD // N_DEV
FFN_CONV_COLS = 2 * FF // N_DEV


def _row_offsets():
    off, o = {}, 0
    for name, n in SMALL_ROW:
        off[name] = (o, n)
        o += -(-n // LANES) * LANES
    return off, o


def adamw_small(recv_row, recv_pre, recv_conv, params):
    off, _ = _row_offsets()
    names = list(params)
    n = len(names)

    def total(ref, rows, lo, width):
        g = ref[0, rows, lo:lo + width]
        for d in range(1, N_DEV):
            g = g + ref[d, rows, lo:lo + width]
        return g

    def grad_of(name, row_ref, pre_ref, conv_ref):
        if name == "norm_mix_pre_w":
            return total(pre_ref, slice(0, 1), 0, D)
        if name == "ssd_conv_w":
            return total(conv_ref, slice(0, SSD_K), 0, SSD_CONV_COLS)
        if name == "ffn_conv_w":
            return total(conv_ref, slice(0, FFN_K), 3 * LANES, FFN_CONV_COLS)
        o, width = off[name]
        return total(row_ref, slice(0, 1), o, width)

    def body(row_ref, pre_ref, conv_ref, *refs):
        ins, outs = refs[:3 * n], refs[3 * n:]
        for k, name in enumerate(names):
            w_ref, m_ref, v_ref = ins[3 * k:3 * k + 3]
            g_ref, d_ref, nm_ref, nv_ref = outs[4 * k:4 * k + 4]
            g = grad_of(name, row_ref, pre_ref, conv_ref)
            d, nm, nv = _adam_update(g, w_ref[...], m_ref[...], v_ref[...])
            g_ref[...] = g
            d_ref[...] = d
            nm_ref[...] = nm
            nv_ref[...] = nv
        outs[4 * n][...] = total(row_ref, slice(0, 1), off["loss"][0], LANES)

    flat = [t for name in names for t in params[name]]
    out_shape = [jax.ShapeDtypeStruct(params[name][0].shape, f32) for name in names for _ in range(4)]
    res = pl.pallas_call(
        body, name="adamw_small",
        out_shape=out_shape + [jax.ShapeDtypeStruct((1, LANES), f32)],
        compiler_params=pltpu.CompilerParams(vmem_limit_bytes=VMEM_LIMIT),
    )(recv_row, recv_pre, recv_conv, *flat)
    return {name: res[4 * k:4 * k + 4] for k, name in enumerate(names)}, res[4 * n]


def _cat_rows(parts):
    words = [lax.bitcast_convert_type(p, jnp.uint16) for p in parts]
    return lax.bitcast_convert_type(jnp.concatenate(words, axis=0), bf16)


def _pad_rows8(w):
    return jnp.pad(w, ((0, 8 - w.shape[0]), (0, 0)))


def _pad_lanes(v):
    return jnp.pad(v.reshape(1, -1), ((0, 0), (0, LANES - v.size)))


WEIGHTS = ('norm_mix_pre_w', 'w_in', 'ssd_conv_w', 'ssd_conv_b', 'ssd_dt_bias', 'ssd_a_log', 'ssd_d', 'ssd_norm_w',
           'ssd_w_out', 'attn_sinks', 'attn_w_out', 'w_mix_out', 'norm_mix_post_w', 'norm_ffn_pre_w', 'ffn_w_up',
           'ffn_conv_w', 'ffn_conv_b', 'ffn_w_down', 'norm_ffn_post_w')
W_IN_ROWS = IN_DIM // N_DEV
W_IN_PAD = 1104
W_IN_SPLIT = (672, 768, 832)
TS = 512
TS_NORM = 1024


def kernel(x, positions, norm_mix_pre_w, w_in, ssd_conv_w, ssd_conv_b, ssd_dt_bias, ssd_a_log, ssd_d, ssd_norm_w, ssd_w_out, attn_sinks, attn_w_out, w_mix_out, norm_mix_post_w, norm_ffn_pre_w, ffn_w_up, ffn_conv_w, ffn_conv_b, ffn_w_down, norm_ffn_post_w, loss_target, m_norm_mix_pre_w, m_w_in, m_ssd_conv_w, m_ssd_conv_b, m_ssd_dt_bias, m_ssd_a_log, m_ssd_d, m_ssd_norm_w, m_ssd_w_out, m_attn_sinks, m_attn_w_out, m_w_mix_out, m_norm_mix_post_w, m_norm_ffn_pre_w, m_ffn_w_up, m_ffn_conv_w, m_ffn_conv_b, m_ffn_w_down, m_norm_ffn_post_w, v_norm_mix_pre_w, v_w_in, v_ssd_conv_w, v_ssd_conv_b, v_ssd_dt_bias, v_ssd_a_log, v_ssd_d, v_ssd_norm_w, v_ssd_w_out, v_attn_sinks, v_attn_w_out, v_w_mix_out, v_norm_mix_post_w, v_norm_ffn_pre_w, v_ffn_w_up, v_ffn_conv_w, v_ffn_conv_b, v_ffn_w_down, v_norm_ffn_post_w):
    a = locals()
    r2 = lambda t: t.reshape(t.shape[-2], t.shape[-1])
    w = {n: r2(a[n]) for n in WEIGHTS}
    m = {n: r2(a["m_" + n]) for n in WEIGHTS}
    v = {n: r2(a["v_" + n]) for n in WEIGHTS}
    xs, target = x[0], loss_target[0]
    S = xs.shape[0]
    ts, tsn = TS, min(TS_NORM, S)

    w_in_blk = jnp.pad(w["w_in"].T.astype(bf16), ((0, W_IN_PAD - W_IN_ROWS), (0, 0)))
    conv_blk = jnp.concatenate([_pad_rows8(w["ssd_conv_w"]), _pad_rows8(w["ffn_conv_w"]),
                                jnp.zeros((8, CONV_BLOCK - SSD_CONV_COLS - FFN_CONV_COLS), f32)], axis=1)
    u, cos, sin, (g_in, g_conv) = prenorm_fwd(xs, w["norm_mix_pre_w"], positions, ts, [w_in_blk, conv_blk])
    wt = g_in[:, :W_IN_ROWS].reshape(IN_DIM, D)
    w_main_t = _cat_rows([wt[IN_OFF[0]:IN_OFF[1]], wt[IN_OFF[6]:IN_OFF[8]], wt[IN_OFF[1]:IN_OFF[2]]])
    w_q_t = wt[IN_OFF[3]:IN_OFF[4]]
    w_kv_t = wt[IN_OFF[4]:IN_OFF[6]]
    w_dt_t = jnp.pad(wt[IN_OFF[2]:IN_OFF[3]], ((0, LANES - NH), (0, 0)))
    conv_w8 = g_conv[:, :, 0:SSD_CONV_COLS].transpose(1, 0, 2).reshape(8, CONVD)
    fconv_w8 = g_conv[:, :, SSD_CONV_COLS:SSD_CONV_COLS + FFN_CONV_COLS].transpose(1, 0, 2).reshape(8, 2 * FF)
    bias = _pad_lanes(w["ssd_dt_bias"])
    alog = _pad_lanes(w["ssd_a_log"])
    dx_row = jnp.repeat(w["ssd_d"].reshape(-1), HD).reshape(1, DI)
    sinks = _pad_lanes(w["attn_sinks"])

    later = [w["ssd_w_out"].astype(bf16), w["attn_w_out"].astype(bf16), w["w_mix_out"].astype(bf16)]
    proj, (g_so, g_ao, g_mix) = mm(u, w_main_t, "nt", bf16, "mm_proj", comm=(later, (False,) * 3))
    w_ssd_out, w_attn_out, w_mix = g_so.reshape(DI, D), g_ao.reshape(D, D), g_mix.reshape(D, D)
    qt = mm(w_q_t, u, "nt", bf16, "mm_q")
    kvt = mm(w_kv_t, u, "nt", bf16, "mm_kv")
    dtr = mm(u, w_dt_t, "nt", f32, "mm_dt")
    xbc, conv_c = ssdconv_fwd(proj, conv_w8, w["ssd_conv_b"], ts)
    y, hprev, ssd_saved, (g_up, g_down) = ssd_fwd(xbc, dtr, bias, alog, dx_row, comm=(
        [w["ffn_w_up"].T.astype(bf16), w["ffn_w_down"].astype(bf16)], (False, False)))
    w_up_t = g_up.reshape(2 * FF, D)
    w_down = g_down.reshape(FF, D)
    yn = gnorm_fwd(y, proj, w["ssd_norm_w"], tsn)
    ys = mm(yn, w_ssd_out, "nn", bf16, "mm_ssd_out")
    aot = attn_fwd(qt, kvt, cos, sin, sinks)
    ya = mm(aot, w_attn_out, "tn", bf16, "mm_attn_out")
    merged = merge_fwd(proj, ys, ya, tsn)
    mo = mm(merged, w_mix, "nn", bf16, "mm_mix")
    x1, h = post_fwd(xs, mo, w["norm_mix_post_w"], w["norm_ffn_pre_w"], tsn)
    up = mm(h, w_up_t, "nt", bf16, "mm_up")
    act, gate_c, val_c = ffnact_fwd(up, fconv_w8, w["ffn_conv_b"], ts)
    ff = mm(act, w_down, "nn", bf16, "mm_down")
    loss_blk, dout, dff, g_post2 = loss_head(x1, ff, target, w["norm_ffn_post_w"], tsn)

    dact = mm(dff, w_down, "nt", bf16, "mm_dact")
    gw_down = mm(act, dff, "tn", bf16, "mm_g_down")
    dgate, dval = ffnact_bwd(dact, gate_c, val_c, ts)
    dup_pre, g_fconv_a = dwconv_bwd(dgate, up, 0, fconv_w8, 0, FFN_K, FF, ts, "ffnconv_bwd_gate", out_cols=2 * FF)
    dup_pre, g_fconv_b = dwconv_bwd(dval, up, 1, fconv_w8, 1, FFN_K, FF, ts, "ffnconv_bwd_val", into=dup_pre, ocb=1,
                                    out_cols=2 * FF)
    g_fconv = jnp.concatenate([g_fconv_a, g_fconv_b], axis=1)
    dh, (r_down,) = mm(dup_pre, w_up_t, "nn", bf16, "mm_dh", comm=([gw_down.reshape(N_DEV, FF // N_DEV, D)], (True,)))
    gw_up_t = mm(dup_pre, h, "tn", bf16, "mm_g_up")
    dx1, dmo, g_norms = post_bwd(dout, dh, x1, mo, w["norm_mix_post_w"], w["norm_ffn_pre_w"], tsn)
    dmerged = mm(dmo, w_mix, "nt", bf16, "mm_dmerged")
    gw_mix = mm(merged, dmo, "tn", bf16, "mm_g_mix")
    dys, dya, dproj = merge_bwd(dmerged, proj, ys, ya, tsn)
    daot = mm(w_attn_out, dya, "nt", bf16, "mm_dao")
    gw_attn_out = mm(aot, dya, "nn", bf16, "mm_g_attn_out")
    dqt, dkvt, g_sinks, (r_up,) = attn_bwd(qt, kvt, cos, sin, sinks, daot,
                                           comm=([gw_up_t.reshape(N_DEV, 2 * FF // N_DEV, D)], (True,)))
    dyn = mm(dys, w_ssd_out, "nt", bf16, "mm_dyn")
    gw_ssd_out = mm(yn, dys, "tn", bf16, "mm_g_ssd_out")
    dy, dproj, g_gnorm = gnorm_bwd(dyn, y, proj, w["ssd_norm_w"], dproj, tsn)
    sends = [gw_ssd_out.reshape(N_DEV, DI // N_DEV, D), gw_attn_out.reshape(N_DEV, D // N_DEV, D),
             gw_mix.reshape(N_DEV, D // N_DEV, D)]
    dxbc, ddtr, g_ssd, (r_so, r_ao, r_mix) = ssd_bwd(xbc, dtr, dy, hprev, ssd_saved, bias, alog, dx_row,
                                                     comm=(sends, (True,) * 3))
    dproj, g_conv_w = dwconv_bwd(dxbc, proj, C_XBC // 1024, conv_w8, 0, SSD_K, 1024, ts, "ssdconv_bwd", act_c=conv_c,
                                 into=dproj, ocb=C_XBC // 1024, out_cols=PM)
    ddtr_b = ddtr.astype(bf16)
    du_c = mm(ddtr_b, w_dt_t, "nn", bf16, "mm_du_dt")
    g_main_t = mm(dproj, u, "tn", bf16, "mm_g_in")
    g_q_t = mm(dqt, u, "nn", bf16, "mm_g_q")
    g_kv_t = mm(dkvt, u, "nn", bf16, "mm_g_kv")
    g_dt_t = mm(ddtr_b, u, "tn", bf16, "mm_g_dt")
    g_wt = _cat_rows([g_main_t[C_Z:C_GS], g_main_t[C_XBC:PM], g_dt_t[:NH], g_q_t, g_kv_t, g_main_t[C_GS:C_XBC]])
    send_in = jnp.pad(g_wt.reshape(N_DEV, W_IN_ROWS, D), ((0, 0), (0, W_IN_PAD - W_IN_ROWS), (0, 0)))
    pieces = {"norm_mix_post_w": g_norms[1:2], "norm_ffn_pre_w": g_norms[0:1], "norm_ffn_post_w": g_post2[0:1],
              "ssd_norm_w": g_gnorm[0:1], "ssd_conv_b": g_conv_w[7:8], "ffn_conv_b": g_fconv[7:8],
              "ssd_dt_bias": g_ssd[0:1], "ssd_a_log": g_ssd[1:2], "ssd_d": g_ssd[2:3], "attn_sinks": g_sinks[0:1],
              "loss": loss_blk[0:1]}
    row = jnp.concatenate([jnp.pad(pieces[n][:, :min(k, pieces[n].shape[1])],
                                   ((0, 0), (0, -(-k // LANES) * LANES - min(k, pieces[n].shape[1]))))
                           for n, k in SMALL_ROW], axis=1)
    send_row = jnp.pad(row, ((0, 7), (0, 0)))
    send_conv = jnp.concatenate(
        [g_conv_w.reshape(8, N_DEV, SSD_CONV_COLS).transpose(1, 0, 2),
         g_fconv.reshape(8, N_DEV, FFN_CONV_COLS).transpose(1, 0, 2),
         jnp.zeros((N_DEV, 8, CONV_BLOCK - SSD_CONV_COLS - FFN_CONV_COLS), f32)], axis=2)
    r0, r1, r2 = W_IN_SPLIT
    du_a, (r_in_a, recv_row, recv_conv) = mm(dproj, w_main_t, "nn", bf16, "mm_du", comm=(
        [send_in[:, :r0], send_row, send_conv], (True, False, True)))
    du_d, (r_in_b,) = mm(dqt, w_q_t, "tn", bf16, "mm_du_q", comm=([send_in[:, r0:r1]], (True,)))
    du_b, (r_in_c,) = mm(dkvt, w_kv_t, "tn", bf16, "mm_du_kv", comm=([send_in[:, r1:r2]], (True,)))
    grad_x, g_pre, (r_in_d,) = prenorm_bwd(xs, w["norm_mix_pre_w"], (du_a, du_b, du_c, du_d), dx1, tsn,
                                           comm=([send_in[:, r2:]], (True,)))
    (recv_pre,) = exchange([g_pre], (False,), "gather_last")

    r_in = jnp.concatenate([r_in_a, r_in_b, r_in_c, r_in_d], axis=1)
    tpad = lambda t: jnp.pad(t.T, ((0, W_IN_PAD - W_IN_ROWS), (0, 0)))
    o_in = [t[:W_IN_ROWS].T for t in adamw(r_in, tpad(w["w_in"]), tpad(m["w_in"]), tpad(v["w_in"]), 368, "adamw_w_in")]
    o_up = [t.T for t in adamw(r_up, w["ffn_w_up"].T, m["ffn_w_up"].T, v["ffn_w_up"].T, 352, "adamw_w_up")]
    big = {"w_in": o_in, "ffn_w_up": o_up,
           "ssd_w_out": adamw(r_so, w["ssd_w_out"], m["ssd_w_out"], v["ssd_w_out"], 256, "adamw_ssd_out"),
           "attn_w_out": adamw(r_ao, w["attn_w_out"], m["attn_w_out"], v["attn_w_out"], 128, "adamw_attn_out"),
           "w_mix_out": adamw(r_mix, w["w_mix_out"], m["w_mix_out"], v["w_mix_out"], 128, "adamw_mix"),
           "ffn_w_down": adamw(r_down, w["ffn_w_down"], m["ffn_w_down"], v["ffn_w_down"], 352, "adamw_down")}
    small_names = [n for n in WEIGHTS if n not in big]
    small, loss_row = adamw_small(recv_row, recv_pre, recv_conv, {n: (w[n], m[n], v[n]) for n in small_names})

    outs = [loss_row[0, 0], grad_x[None]]
    for k in range(4):
        for n in WEIGHTS:
            outs.append((big[n][k] if n in big else small[n][k]).reshape(a[n].shape))
    return tuple(outs)
```

```python
import jax
import jax.numpy as jnp
import numpy as np
from jax import lax
from jax.experimental import pallas as pl
from jax.experimental.pallas import tpu as pltpu

f32 = jnp.float32
bf16 = jnp.bfloat16

N_DEV = 8
D = 1024
DI = 2048
NH = 32
HD = 64
NG = 4
GW = DI // NG
NS = 128
CH = 128
CONVD = DI + 2 * NG * NS
SSD_K = 4
AH = 16
AD = 64
KVH = 4
REP = AH // KVH
KVW = KVH * AD
WIN = 128
FF = 2816
FFN_K = 3
EPS = 1e-6
ROPE_THETA = 10000.0
LANES = 128
RG = 16
CW = 256

C_Z, C_GS, C_GA, C_XBC, PM = 0, 2048, 3072, 4096, 7168
IN_SIZES = (DI, CONVD, NH, D, KVW, KVW, D, D)
IN_OFF = tuple(int(v) for v in np.cumsum((0,) + IN_SIZES))
IN_DIM = IN_OFF[-1]

ADAM_LR, ADAM_B1, ADAM_B2, ADAM_EPS, ADAM_WD, ADAM_STEP = 0.001, 0.9, 0.999, 1e-08, 0.01, 10

VMEM_LIMIT = 56 * 1024 * 1024


def _cp(*sem, side_effects=False):
    return pltpu.CompilerParams(dimension_semantics=sem, vmem_limit_bytes=VMEM_LIMIT, has_side_effects=side_effects)


def _dot(a, b, mode="nn"):
    dims = {"nn": (((1,), (0,)), ((), ())), "nt": (((1,), (1,)), ((), ())), "tn": (((0,), (0,)), ((), ()))}[mode]
    return lax.dot_general(a, b, dims, preferred_element_type=f32)


def _split3(v):
    hi = v.astype(bf16)
    r = v - hi.astype(f32)
    mid = r.astype(bf16)
    lo = (r - mid.astype(f32)).astype(bf16)
    return hi, mid, lo


def _dot3_left(m01, v):
    hi, mid, lo = _split3(v)
    return _dot(m01, hi) + _dot(m01, mid) + _dot(m01, lo)


def _dot3_right(v, m01):
    hi, mid, lo = _split3(v)
    return _dot(hi, m01) + _dot(mid, m01) + _dot(lo, m01)


def _dot2_right(v, m01):
    hi = v.astype(bf16)
    lo = (v - hi.astype(f32)).astype(bf16)
    return _dot(hi, m01) + _dot(lo, m01)


def _sigmoid(x):
    return 1.0 / (1.0 + jnp.exp(-x))


def _sigmoid_fast(x):
    return pl.reciprocal(1.0 + jnp.exp(-x), approx=True)


def _peer(k, x, y, c):
    return ((1 - x) if k & 4 else x, (1 - y) if k & 2 else y, (1 - c) if k & 1 else c)


def _xchg_copies(buf_refs, out_refs, send_sems, recv_sems, local_sems, personalised):
    x, y, c = lax.axis_index("x"), lax.axis_index("y"), lax.axis_index("c")
    me = 4 * x + 2 * y + c
    local, remote = [], []
    for b, (buf, out, pers) in enumerate(zip(buf_refs, out_refs, personalised)):
        local.append(pltpu.make_async_copy(buf.at[me] if pers else buf, out.at[me], local_sems.at[b]))
        for k in range(1, N_DEV):
            px, py, pc = _peer(k, x, y, c)
            s = b * (N_DEV - 1) + k - 1
            remote.append(pltpu.make_async_remote_copy(
                src_ref=buf.at[4 * px + 2 * py + pc] if pers else buf, dst_ref=out.at[me],
                send_sem=send_sems.at[s], recv_sem=recv_sems.at[s],
                device_id=(px, py, pc), device_id_type=pl.DeviceIdType.MESH))
    return local, remote


class _Comm:
    def __init__(self, comm):
        self.bufs, self.pers = comm if comm else ((), ())
        self.n = len(self.bufs)

    def in_specs(self):
        return [pl.BlockSpec(memory_space=pl.ANY)] * self.n

    out_specs = in_specs

    def out_shape(self):
        return [jax.ShapeDtypeStruct((N_DEV,) + tuple(b.shape[1:] if p else b.shape), b.dtype)
                for b, p in zip(self.bufs, self.pers)]

    def scratch(self):
        n = self.n
        return [pltpu.SemaphoreType.DMA((n * (N_DEV - 1),)), pltpu.SemaphoreType.DMA((n * (N_DEV - 1),)),
                pltpu.SemaphoreType.DMA((n,))] if n else []

    def split(self, refs, n_in, n_out):
        n = self.n
        ins, outs = refs[:n_in], refs[n_in + n:n_in + n + n_out]
        rest = refs[n_in + n + n_out + n:]
        if not n:
            return ins, outs, rest, None
        copies = _xchg_copies(refs[n_in:n_in + n], refs[n_in + n + n_out:n_in + n + n_out + n], *rest[-3:], self.pers)
        return ins, outs, rest[:-3], copies

    def start(self, copies, first):
        if copies:
            @pl.when(first)
            def _():
                for cp in copies[0] + copies[1]:
                    cp.start()

    def wait(self, copies, last):
        if copies:
            @pl.when(last)
            def _():
                for cp in copies[1]:
                    cp.wait_recv()
                for cp in copies[1]:
                    cp.wait_send()
                for cp in copies[0]:
                    cp.wait()


def exchange(bufs, personalised, name):
    cm = _Comm((bufs, personalised))

    def body(*refs):
        _, _, _, copies = cm.split(refs, 0, 0)
        cm.start(copies, True)
        cm.wait(copies, True)

    return pl.pallas_call(
        body, name=name, in_specs=cm.in_specs(), out_specs=cm.out_specs(), out_shape=cm.out_shape(),
        scratch_shapes=cm.scratch(), compiler_params=pltpu.CompilerParams(has_side_effects=True),
    )(*bufs)


class _TwoLevelGather:
    def __init__(self, bufs):
        self.bufs = list(bufs)
        self.n = len(self.bufs)

    def in_specs(self):
        return [pl.BlockSpec(memory_space=pl.ANY)] * self.n

    out_specs = in_specs

    def out_shape(self):
        return [jax.ShapeDtypeStruct((N_DEV,) + tuple(b.shape), b.dtype) for b in self.bufs]

    def scratch(self):
        per = N_DEV - 1
        return [pltpu.SemaphoreType.DMA((self.n * per,)), pltpu.SemaphoreType.DMA((self.n * per,)),
                pltpu.SemaphoreType.DMA((self.n,))]

    def bind(self, ins, outs, send_sems, recv_sems, local_sems):
        n, per = self.n, N_DEV - 1
        x, y, c = lax.axis_index("x"), lax.axis_index("y"), lax.axis_index("c")
        me, sibling = (x, y, c), (x, y, 1 - c)
        chips = [(1 - x, y), (x, 1 - y), (1 - x, 1 - y)]

        def copy(b, k, block, to, src=None):
            dst = outs[b].at[4 * block[0] + 2 * block[1] + block[2]]
            return pltpu.make_async_remote_copy(
                src_ref=dst if src is None else src, dst_ref=dst,
                send_sem=send_sems.at[b * per + k], recv_sem=recv_sems.at[b * per + k],
                device_id=to, device_id_type=pl.DeviceIdType.MESH)

        mine = [pltpu.make_async_copy(ins[b], outs[b].at[4 * x + 2 * y + c], local_sems.at[b]) for b in range(n)]
        first = []
        for b in range(n):
            first.append(copy(b, 0, me, sibling, src=ins[b]))
            first += [copy(b, 1 + j, me, (*chip, c), src=ins[b]) for j, chip in enumerate(chips)]

        def start():
            for cp in mine + first:
                cp.start()

        def finish():
            passed = []
            for j, chip in enumerate(chips):
                for b in range(n):
                    copy(b, 1 + j, (*chip, c), me).wait_recv()
                    passed.append(copy(b, 4 + j, (*chip, c), sibling))
                    passed[-1].start()
            for b in range(n):
                copy(b, 0, sibling, me).wait_recv()
                for j, chip in enumerate(chips):
                    copy(b, 4 + j, (*chip, 1 - c), me).wait_recv()
            for cp in first + passed:
                cp.wait_send()
            for cp in mine:
                cp.wait()

        return start, finish


MM_TILES = (3584, 2176, 2048, 1792, 1408, 1024, 512, 256, 128)
MM_VMEM_BUDGET = 40 * 1024 * 1024


def _mm_tiles(M, N, K, out_bytes):
    cm = [t for t in MM_TILES if M % t == 0]
    cn = [t for t in MM_TILES if N % t == 0]
    ck = [t for t in MM_TILES if K % t == 0]
    best = None
    for bm in cm[:2]:
        for bn in cn:
            for bk in ck:
                need = 4 * (bm * bk + bk * bn) + bm * bn * (4 + 2 * out_bytes)
                if need <= MM_VMEM_BUDGET:
                    score = (bm * bn * bk, bk)
                    if best is None or score > best[0]:
                        best = (score, (bm, bn, bk))
    return best[1]


def mm(a, b, mode, out_dtype, name, comm=None):
    if mode == "nn":
        (M, K), (_, N) = a.shape, b.shape
    elif mode == "nt":
        (M, K), (N, _) = a.shape, b.shape
    else:
        (K, M), (_, N) = a.shape, b.shape
    bm, bn, bk = _mm_tiles(M, N, K, jnp.dtype(out_dtype).itemsize)
    gm, gn, nk = M // bm, N // bn, K // bk
    cm = _Comm(comm)

    def body(*refs):
        (a_ref, b_ref), (o_ref,), scr, copies = cm.split(refs, 2, 1)
        i, j, k = pl.program_id(0), pl.program_id(1), pl.program_id(2)
        cm.start(copies, jnp.logical_and(jnp.logical_and(i == 0, j == 0), k == 0))
        p = _dot(a_ref[...], b_ref[...], mode)
        if nk == 1:
            o_ref[...] = p.astype(o_ref.dtype)
        else:
            acc_ref = scr[0]

            @pl.when(k == 0)
            def _():
                acc_ref[...] = p

            @pl.when(k > 0)
            def _():
                acc_ref[...] += p

            @pl.when(k == nk - 1)
            def _():
                o_ref[...] = acc_ref[...].astype(o_ref.dtype)

        cm.wait(copies, jnp.logical_and(jnp.logical_and(i == gm - 1, j == gn - 1), k == nk - 1))

    if mode == "nn":
        a_spec = pl.BlockSpec((bm, bk), lambda i, j, k: (i, k))
        b_spec = pl.BlockSpec((bk, bn), lambda i, j, k: (k, j))
    elif mode == "nt":
        a_spec = pl.BlockSpec((bm, bk), lambda i, j, k: (i, k))
        b_spec = pl.BlockSpec((bn, bk), lambda i, j, k: (j, k))
    else:
        a_spec = pl.BlockSpec((bk, bm), lambda i, j, k: (k, i))
        b_spec = pl.BlockSpec((bk, bn), lambda i, j, k: (k, j))
    sem = ("arbitrary",) * 3 if cm.n else ("parallel", "parallel", "arbitrary")
    res = pl.pallas_call(
        body, name=name, grid=(gm, gn, nk),
        in_specs=[a_spec, b_spec] + cm.in_specs(),
        out_specs=[pl.BlockSpec((bm, bn), lambda i, j, k: (i, j))] + cm.out_specs(),
        out_shape=[jax.ShapeDtypeStruct((M, N), out_dtype)] + cm.out_shape(),
        scratch_shapes=([pltpu.VMEM((bm, bn), f32)] if nk > 1 else []) + cm.scratch(),
        compiler_params=_cp(*sem, side_effects=bool(cm.n)),
    )(a, b, *cm.bufs)
    return (res[0], res[1:]) if cm.n else res[0]


def _groups(ts, fn, carry=None, reverse=False, unroll=8, rg=RG):
    n = ts // rg
    if n == 1:
        return fn(0, carry)
    unroll = min(unroll, n)
    span = rg * unroll

    def body(g, c):
        r0 = pl.multiple_of((n // unroll - 1 - g if reverse else g) * span, span)
        for u in (range(unroll - 1, -1, -1) if reverse else range(unroll)):
            c = fn(pl.multiple_of(r0 + u * rg, rg), c)
        return c

    return lax.fori_loop(0, n // unroll, body, carry)


def _rms(x):
    return lax.rsqrt(jnp.mean(x * x, axis=-1, keepdims=True) + EPS)


def _rms_bwd(x, r, dn):
    n = x * r
    return r * (dn - n * jnp.mean(dn * n, axis=-1, keepdims=True))


NRG = 256


def _fold(x):
    return jnp.sum(x.reshape(x.shape[0] // 8, 8, x.shape[1]), axis=0)


def _flush(acc_ref, out_ref, row):
    out_ref[row:row + 1, :] = jnp.sum(acc_ref[...], axis=0, keepdims=True)


def prenorm_fwd(x, w, pos_row, ts, gather):
    S = x.shape[0]
    nt = S // ts
    tg = _TwoLevelGather(gather)
    n = tg.n
    half = AD // 2
    inv = ROPE_THETA ** (-jnp.arange(half, dtype=f32) * 2.0 / AD)
    inv_col = jnp.tile(inv, 2)[:, None]

    def body(x_ref, w_ref, p_ref, inv_ref, *refs):
        u_ref, cos_ref, sin_ref = refs[n:n + 3]
        start, finish = tg.bind(refs[:n], refs[n + 3:2 * n + 3], *refs[2 * n + 3:])
        i = pl.program_id(0)
        pl.when(i == 0)(start)
        wv = w_ref[...]

        def grp(r0, _):
            xv = x_ref[pl.ds(r0, NRG), :]
            u_ref[pl.ds(r0, NRG), :] = (xv * _rms(xv) * wv).astype(bf16)

        _groups(ts, grp, rg=NRG)
        ang = inv_ref[...] * p_ref[...].astype(f32)
        row = lax.broadcasted_iota(jnp.int32, ang.shape, 0)
        cos_ref[...] = jnp.cos(ang)
        sin_ref[...] = jnp.where(row < half, -1.0, 1.0) * jnp.sin(ang)
        pl.when(i == nt - 1)(finish)

    tab = pl.BlockSpec((AD, ts), lambda i: (0, i))
    res = pl.pallas_call(
        body, name="prenorm_fwd", grid=(nt,),
        in_specs=[pl.BlockSpec((ts, D), lambda i: (i, 0)), pl.BlockSpec((1, D), lambda i: (0, 0)),
                  pl.BlockSpec((1, ts), lambda i: (0, i)), pl.BlockSpec((AD, 1), lambda i: (0, 0))] + tg.in_specs(),
        out_specs=[pl.BlockSpec((ts, D), lambda i: (i, 0)), tab, tab] + tg.out_specs(),
        out_shape=[jax.ShapeDtypeStruct((S, D), bf16), jax.ShapeDtypeStruct((AD, S), f32),
                   jax.ShapeDtypeStruct((AD, S), f32)] + tg.out_shape(),
        scratch_shapes=tg.scratch(),
        compiler_params=_cp("arbitrary", side_effects=True),
    )(x, w, pos_row, inv_col, *tg.bufs)
    return res[0], res[1], res[2], res[3:]


def prenorm_bwd(x, w, dus, dx1, ts, comm=None):
    S = x.shape[0]
    nt = S // ts
    nd = len(dus)
    cm = _Comm(comm)

    def body(*refs):
        ins, (gx_ref, gw_ref), (acc_ref,), copies = cm.split(refs, nd + 3, 2)
        x_ref, w_ref = ins[:2]
        du_refs, dx1_ref = ins[2:2 + nd], ins[2 + nd]
        i = pl.program_id(0)
        cm.start(copies, i == 0)
        wv = w_ref[...]

        @pl.when(i == 0)
        def _():
            acc_ref[...] = jnp.zeros_like(acc_ref)
            gw_ref[...] = jnp.zeros_like(gw_ref)

        def grp(r0, _):
            rows = pl.ds(r0, NRG)
            xv = x_ref[rows, :]
            r = _rms(xv)
            du = du_refs[0][rows, :].astype(f32)
            for d_ref in du_refs[1:]:
                du = du + d_ref[rows, :].astype(f32)
            gx_ref[rows, :] = dx1_ref[rows, :] + _rms_bwd(xv, r, du * wv)
            acc_ref[...] += _fold(du * xv * r)

        _groups(ts, grp, rg=NRG)

        @pl.when(i == nt - 1)
        def _():
            _flush(acc_ref, gw_ref, 0)

        cm.wait(copies, i == nt - 1)

    row = pl.BlockSpec((ts, D), lambda i: (i, 0))
    res = pl.pallas_call(
        body, name="prenorm_bwd", grid=(nt,),
        in_specs=[row, pl.BlockSpec((1, D), lambda i: (0, 0))] + [row] * (nd + 1) + cm.in_specs(),
        out_specs=[row, pl.BlockSpec((8, D), lambda i: (0, 0))] + cm.out_specs(),
        out_shape=[jax.ShapeDtypeStruct((S, D), f32), jax.ShapeDtypeStruct((8, D), f32)] + cm.out_shape(),
        scratch_shapes=[pltpu.VMEM((8, D), f32)] + cm.scratch(),
        compiler_params=_cp("arbitrary", side_effects=bool(cm.n)),
    )(x, w, *dus, dx1, *cm.bufs)
    return res[0], res[1], res[2:]


def post_fwd(x, mo, w_post, w_pre2, ts):
    S = x.shape[0]

    def body(x_ref, mo_ref, wp_ref, w2_ref, x1_ref, h_ref):
        wp, w2 = wp_ref[...], w2_ref[...]

        def grp(r0, _):
            rows = pl.ds(r0, NRG)
            mv = mo_ref[rows, :].astype(f32)
            x1 = x_ref[rows, :] + mv * _rms(mv) * wp
            x1_ref[rows, :] = x1
            h_ref[rows, :] = (x1 * _rms(x1) * w2).astype(bf16)

        _groups(ts, grp, rg=NRG)

    row = pl.BlockSpec((ts, D), lambda i: (i, 0))
    par = pl.BlockSpec((1, D), lambda i: (0, 0))
    return pl.pallas_call(
        body, name="post_fwd", grid=(S // ts,),
        in_specs=[row, row, par, par], out_specs=[row, row],
        out_shape=[jax.ShapeDtypeStruct((S, D), f32), jax.ShapeDtypeStruct((S, D), bf16)],
        compiler_params=_cp("parallel"),
    )(x, mo, w_post, w_pre2)


def post_bwd(dout, dh, x1, mo, w_post, w_pre2, ts):
    S = x1.shape[0]
    nt = S // ts

    def body(dout_ref, dh_ref, x1_ref, mo_ref, wp_ref, w2_ref, dx1_ref, dmo_ref, gw_ref, acc2_ref, accp_ref):
        i = pl.program_id(0)
        wp, w2 = wp_ref[...], w2_ref[...]

        @pl.when(i == 0)
        def _():
            acc2_ref[...] = jnp.zeros_like(acc2_ref)
            accp_ref[...] = jnp.zeros_like(accp_ref)
            gw_ref[...] = jnp.zeros_like(gw_ref)

        def grp(r0, _):
            rows = pl.ds(r0, NRG)
            x1 = x1_ref[rows, :]
            r1 = _rms(x1)
            dh = dh_ref[rows, :].astype(f32)
            dx1 = dout_ref[rows, :] + _rms_bwd(x1, r1, dh * w2)
            dx1_ref[rows, :] = dx1
            acc2_ref[...] += _fold(dh * x1 * r1)
            mv = mo_ref[rows, :].astype(f32)
            rm = _rms(mv)
            dmo_ref[rows, :] = _rms_bwd(mv, rm, dx1 * wp).astype(bf16)
            accp_ref[...] += _fold(dx1 * mv * rm)

        _groups(ts, grp, rg=NRG)

        @pl.when(i == nt - 1)
        def _():
            _flush(acc2_ref, gw_ref, 0)
            _flush(accp_ref, gw_ref, 1)

    row = pl.BlockSpec((ts, D), lambda i: (i, 0))
    par = pl.BlockSpec((1, D), lambda i: (0, 0))
    return pl.pallas_call(
        body, name="post_bwd", grid=(nt,),
        in_specs=[row, row, row, row, par, par],
        out_specs=[row, row, pl.BlockSpec((8, D), lambda i: (0, 0))],
        out_shape=[jax.ShapeDtypeStruct((S, D), f32), jax.ShapeDtypeStruct((S, D), bf16),
                   jax.ShapeDtypeStruct((8, D), f32)],
        scratch_shapes=[pltpu.VMEM((8, D), f32), pltpu.VMEM((8, D), f32)],
        compiler_params=_cp("arbitrary"),
    )(dout, dh, x1, mo, w_post, w_pre2)


def loss_head(x1, ff, target, w, ts):
    S = x1.shape[0]
    nt = S // ts

    def body(x1_ref, ff_ref, t_ref, w_ref, loss_ref, dout_ref, dff_ref, gw_ref, accw_ref, accl_ref):
        i = pl.program_id(0)
        wv = w_ref[...]

        @pl.when(i == 0)
        def _():
            accw_ref[...] = jnp.zeros_like(accw_ref)
            accl_ref[...] = jnp.zeros_like(accl_ref)
            gw_ref[...] = jnp.zeros_like(gw_ref)

        def grp(r0, _):
            rows = pl.ds(r0, NRG)
            fv = ff_ref[rows, :].astype(f32)
            r = _rms(fv)
            n = fv * r
            e = x1_ref[rows, :] + n * wv - t_ref[rows, :]
            dout = e * (1.0 / D)
            dout_ref[rows, :] = dout
            dff_ref[rows, :] = _rms_bwd(fv, r, dout * wv).astype(bf16)
            accw_ref[...] += _fold(dout * n)
            accl_ref[...] += _fold(e * e)

        _groups(ts, grp, rg=NRG)

        @pl.when(i == nt - 1)
        def _():
            _flush(accw_ref, gw_ref, 0)
            tot = jnp.sum(jnp.sum(accl_ref[...], axis=1, keepdims=True), axis=0, keepdims=True) * (0.5 / D)
            loss_ref[...] = jnp.broadcast_to(tot, loss_ref.shape)

    row = pl.BlockSpec((ts, D), lambda i: (i, 0))
    return pl.pallas_call(
        body, name="loss_head", grid=(nt,),
        in_specs=[row, row, row, pl.BlockSpec((1, D), lambda i: (0, 0))],
        out_specs=[pl.BlockSpec((8, LANES), lambda i: (0, 0)), row, row, pl.BlockSpec((8, D), lambda i: (0, 0))],
        out_shape=[jax.ShapeDtypeStruct((8, LANES), f32), jax.ShapeDtypeStruct((S, D), f32),
                   jax.ShapeDtypeStruct((S, D), bf16), jax.ShapeDtypeStruct((8, D), f32)],
        scratch_shapes=[pltpu.VMEM((8, D), f32), pltpu.VMEM((8, D), f32)],
        compiler_params=_cp("arbitrary"),
    )(x1, ff, target, w)


def _taps(w_ref, cs, K):
    return [jnp.broadcast_to(w_ref[k:k + 1, cs], (8, CW)) for k in range(K)]


def _rolls_down(v, K):
    return tuple(pltpu.roll(v, s, 0) for s in range(1, K))


def _rolls_up(v, K):
    return tuple(pltpu.roll(v, 8 - s, 0) for s in range(1, K))


def _shifted_down(prolls, a, b, K, sub):
    arolls, brolls = _rolls_down(a, K), _rolls_down(b, K)
    out = [(a, b)]
    for s in range(1, K):
        m = sub < s
        out.append((jnp.where(m, prolls[s - 1], arolls[s - 1]), jnp.where(m, arolls[s - 1], brolls[s - 1])))
    return out, brolls


def _shifted_up(a, b, nrolls, K, sub):
    arolls, brolls = _rolls_up(a, K), _rolls_up(b, K)
    out = [(a, b)]
    for s in range(1, K):
        m = sub < 8 - s
        out.append((jnp.where(m, arolls[s - 1], brolls[s - 1]), jnp.where(m, brolls[s - 1], nrolls[s - 1])))
    return out, arolls


def _conv_group(prolls, a, b, taps, bias, K, sub):
    shifted, brolls = _shifted_down(prolls, a, b, K, sub)
    ya, yb = bias, bias
    for k in range(K):
        xa, xb = shifted[K - 1 - k]
        ya = ya + taps[k] * xa
        yb = yb + taps[k] * xb
    return ya, yb, brolls


def _prev8_map(ts, cb):
    return lambda i, j: (jnp.maximum(i * (ts // 8) - 1, 0), cb + j)


def ssdconv_fwd(proj, w8, b, ts):
    S = proj.shape[0]
    bw = 1024
    cb = C_XBC // bw

    def body(cur_ref, prev_ref, w_ref, b_ref, o_ref, c_ref):
        first = pl.program_id(0) == 0
        sub = lax.broadcasted_iota(jnp.int32, (8, CW), 0)
        for c0 in range(0, bw, CW):
            cs = slice(c0, c0 + CW)
            taps = _taps(w_ref, cs, SSD_K)
            bias = jnp.broadcast_to(b_ref[:, cs], (8, CW))

            def grp(r0, prolls, cs=cs, taps=taps, bias=bias):
                rows = pl.ds(r0, RG)
                xv = cur_ref[rows, cs].astype(f32)
                ya, yb, brolls = _conv_group(prolls, xv[0:8], xv[8:16], taps, bias, SSD_K, sub)
                y = jnp.concatenate([ya, yb], axis=0)
                c_ref[rows, cs] = y.astype(bf16)
                o_ref[rows, cs] = (y * _sigmoid_fast(y)).astype(bf16)
                return brolls

            _groups(ts, grp, _rolls_down(jnp.where(first, 0.0, prev_ref[:, cs].astype(f32)), SSD_K))

    o = jax.ShapeDtypeStruct((S, CONVD), bf16)
    blk = pl.BlockSpec((ts, bw), lambda i, j: (i, j))
    return pl.pallas_call(
        body, name="ssdconv_fwd", grid=(S // ts, CONVD // bw),
        in_specs=[pl.BlockSpec((ts, bw), lambda i, j: (i, cb + j)),
                  pl.BlockSpec((8, bw), _prev8_map(ts, cb)),
                  pl.BlockSpec((8, bw), lambda i, j: (0, j)),
                  pl.BlockSpec((1, bw), lambda i, j: (0, j))],
        out_specs=[blk, blk], out_shape=[o, o],
        compiler_params=_cp("parallel", "parallel"),
    )(proj, proj, w8, b)


def _gelu_tanh(x):
    c = 0.7978845608028654
    t = jnp.tanh(c * (x + 0.044715 * x * x * x))
    return 0.5 * x * (1.0 + t), t


def ffnact_fwd(up, w8, b, ts):
    S = up.shape[0]

    def body(g_ref, gp_ref, v_ref, vp_ref, wg_ref, wv_ref, bg_ref, bv_ref, o_ref, gc_ref, vc_ref):
        first = pl.program_id(0) == 0
        sub = lax.broadcasted_iota(jnp.int32, (8, CW), 0)
        for c0 in range(0, FF, CW):
            cs = slice(c0, c0 + CW)
            tg, tv = _taps(wg_ref, cs, FFN_K), _taps(wv_ref, cs, FFN_K)
            bg = jnp.broadcast_to(bg_ref[:, cs], (8, CW))
            bv = jnp.broadcast_to(bv_ref[:, cs], (8, CW))

            def grp(r0, carry, cs=cs, tg=tg, tv=tv, bg=bg, bv=bv):
                pg, pv = carry
                rows = pl.ds(r0, RG)
                gx = g_ref[rows, cs].astype(f32)
                vx = v_ref[rows, cs].astype(f32)
                ga, gb, pg = _conv_group(pg, gx[0:8], gx[8:16], tg, bg, FFN_K, sub)
                va, vb, pv = _conv_group(pv, vx[0:8], vx[8:16], tv, bv, FFN_K, sub)
                g = jnp.concatenate([ga, gb], axis=0)
                v = jnp.concatenate([va, vb], axis=0)
                gc_ref[rows, cs] = g.astype(bf16)
                vc_ref[rows, cs] = v.astype(bf16)
                o_ref[rows, cs] = (_gelu_tanh(g)[0] * v).astype(bf16)
                return pg, pv

            _groups(ts, grp, (_rolls_down(jnp.where(first, 0.0, gp_ref[:, cs].astype(f32)), FFN_K),
                              _rolls_down(jnp.where(first, 0.0, vp_ref[:, cs].astype(f32)), FFN_K)))

    o = jax.ShapeDtypeStruct((S, FF), bf16)
    blk = pl.BlockSpec((ts, FF), lambda i: (i, 0))
    prev = lambda cb: pl.BlockSpec((8, FF), lambda i: (jnp.maximum(i * (ts // 8) - 1, 0), cb))
    return pl.pallas_call(
        body, name="ffnact_fwd", grid=(S // ts,),
        in_specs=[blk, prev(0), pl.BlockSpec((ts, FF), lambda i: (i, 1)), prev(1),
                  pl.BlockSpec((8, FF), lambda i: (0, 0)), pl.BlockSpec((8, FF), lambda i: (0, 1)),
                  pl.BlockSpec((1, FF), lambda i: (0, 0)), pl.BlockSpec((1, FF), lambda i: (0, 1))],
        out_specs=[blk, blk, blk], out_shape=[o, o, o],
        compiler_params=_cp("parallel"),
    )(up, up, up, up, w8, w8, b, b)


def ffnact_bwd(dact, gc, vc, ts):
    S = dact.shape[0]

    def body(d_ref, g_ref, v_ref, dg_ref, dv_ref):
        c = 0.7978845608028654
        for c0 in range(0, FF, CW):
            cs = slice(c0, c0 + CW)

            def grp(r0, _, cs=cs):
                rows = pl.ds(r0, RG)
                d = d_ref[rows, cs].astype(f32)
                g = g_ref[rows, cs].astype(f32)
                ge, t = _gelu_tanh(g)
                dgelu = 0.5 * (1.0 + t) + 0.5 * g * (1.0 - t * t) * c * (1.0 + 3.0 * 0.044715 * g * g)
                dg_ref[rows, cs] = (d * v_ref[rows, cs].astype(f32) * dgelu).astype(bf16)
                dv_ref[rows, cs] = (d * ge).astype(bf16)

            _groups(ts, grp)

    o = jax.ShapeDtypeStruct((S, FF), bf16)
    blk = pl.BlockSpec((ts, FF), lambda i: (i, 0))
    return pl.pallas_call(
        body, name="ffnact_bwd", grid=(S // ts,),
        in_specs=[blk, blk, blk], out_specs=[blk, blk], out_shape=[o, o],
        compiler_params=_cp("parallel"),
    )(dact, gc, vc)


def dwconv_bwd(dy, x, xcb, w8, wcb, K, bw, ts, name, act_c=None, into=None, ocb=0, out_cols=None):
    S, C = dy.shape
    nr = S // ts
    out_cols = out_cols or C
    n_act = 0 if act_c is None else 2

    def body(*refs):
        dy_ref, dyn_ref = refs[0:2]
        c_ref, cn_ref = (refs[2:4] if n_act else (None, None))
        x_ref, xp_ref, w_ref = refs[2 + n_act:5 + n_act]
        dx_ref, dw_ref, sd_ref = refs[-3:]
        i = pl.program_id(1)
        first, last = i == 0, i == nr - 1
        sub = lax.broadcasted_iota(jnp.int32, (8, CW), 0)

        def grad_y(d, c):
            if c is None:
                return d.astype(f32)
            cv = c.astype(f32)
            s = _sigmoid_fast(cv)
            return d.astype(f32) * s * (1.0 + cv * (1.0 - s))

        @pl.when(first)
        def _():
            dw_ref[...] = jnp.zeros_like(dw_ref)

        for c0 in range(0, bw, CW):
            cs = slice(c0, c0 + CW)
            taps = _taps(w_ref, cs, K)
            zero = jnp.zeros((8, CW), f32)

            def fwd(r0, carry, cs=cs):
                prolls, accs, accb = carry
                rows = pl.ds(r0, RG)
                g = grad_y(dy_ref[rows, cs], c_ref[rows, cs] if n_act else None)
                sd_ref[rows, cs] = g
                xv = x_ref[rows, cs].astype(f32)
                ga, gb = g[0:8], g[8:16]
                shifted, brolls = _shifted_down(prolls, xv[0:8], xv[8:16], K, sub)
                new = []
                for k in range(K):
                    xa, xb = shifted[K - 1 - k]
                    new.append(accs[k] + ga * xa + gb * xb)
                return brolls, tuple(new), accb + ga + gb

            before = jnp.where(first, 0.0, xp_ref[:, cs].astype(f32))
            _, accs, accb = _groups(ts, fwd, (_rolls_down(before, K), (zero,) * K, zero))
            for k in range(K):
                dw_ref[k:k + 1, cs] += jnp.sum(accs[k], axis=0, keepdims=True)
            dw_ref[7:8, cs] += jnp.sum(accb, axis=0, keepdims=True)

            def bwd(r0, nrolls, cs=cs, taps=taps):
                rows = pl.ds(r0, RG)
                g = sd_ref[rows, cs]
                shifted, arolls = _shifted_up(g[0:8], g[8:16], nrolls, K, sub)
                da, db = zero, zero
                for k in range(K):
                    ua, ub = shifted[K - 1 - k]
                    da = da + taps[k] * ua
                    db = db + taps[k] * ub
                dx_ref[rows, cs] = jnp.concatenate([da, db], axis=0).astype(bf16)
                return arolls

            halo = grad_y(dyn_ref[:, cs], cn_ref[:, cs] if n_act else None)
            _groups(ts, bwd, _rolls_up(jnp.where(last, 0.0, halo), K), reverse=True)

    nxt = lambda j, i: (jnp.minimum((i + 1) * (ts // 8), S // 8 - 1), j)
    tile = pl.BlockSpec((ts, bw), lambda j, i: (i, j))
    acts = [] if act_c is None else [act_c, act_c]
    extra = [] if into is None else [into]
    n_in = 5 + n_act
    return pl.pallas_call(
        body, name=name, grid=(C // bw, nr),
        in_specs=[tile, pl.BlockSpec((8, bw), nxt)] + ([tile, pl.BlockSpec((8, bw), nxt)] if n_act else []) + [
            pl.BlockSpec((ts, bw), lambda j, i: (i, xcb + j)),
            pl.BlockSpec((8, bw), lambda j, i: (jnp.maximum(i * (ts // 8) - 1, 0), xcb + j)),
            pl.BlockSpec((8, bw), lambda j, i: (0, wcb + j))] + [pl.BlockSpec(memory_space=pl.ANY)] * len(extra),
        out_specs=[pl.BlockSpec((ts, bw), lambda j, i: (i, ocb + j)), pl.BlockSpec((8, bw), lambda j, i: (0, j))],
        out_shape=[jax.ShapeDtypeStruct((S, out_cols), bf16), jax.ShapeDtypeStruct((8, C), f32)],
        scratch_shapes=[pltpu.VMEM((ts, bw), f32)],
        input_output_aliases={n_in: 0} if extra else {},
        compiler_params=_cp("parallel", "arbitrary"),
    )(dy, dy, *acts, x, x, w8, *extra)


def gnorm_fwd(y, proj, w, ts):
    S = y.shape[0]

    def body(y_ref, z_ref, w_ref, o_ref):
        for k in range(NG):
            sl = slice(k * GW, (k + 1) * GW)
            wv = w_ref[:, sl]

            def grp(r0, _, sl=sl, wv=wv):
                rows = pl.ds(r0, NRG)
                z = z_ref[rows, sl].astype(f32)
                g = y_ref[rows, sl].astype(f32) * z * _sigmoid_fast(z)
                o_ref[rows, sl] = (g * _rms(g) * wv).astype(bf16)

            _groups(ts, grp, rg=NRG)

    row = pl.BlockSpec((ts, DI), lambda i: (i, 0))
    return pl.pallas_call(
        body, name="gnorm_fwd", grid=(S // ts,),
        in_specs=[row, row, pl.BlockSpec((1, DI), lambda i: (0, 0))],
        out_specs=row, out_shape=jax.ShapeDtypeStruct((S, DI), bf16),
        compiler_params=_cp("parallel"),
    )(y, proj, w)


def gnorm_bwd(dyn, y, proj, w, dproj, ts):
    S = y.shape[0]
    nt = S // ts

    def body(d_ref, y_ref, z_ref, w_ref, _, dy_ref, dz_ref, gw_ref, acc_ref):
        i = pl.program_id(0)

        @pl.when(i == 0)
        def _():
            acc_ref[...] = jnp.zeros_like(acc_ref)
            gw_ref[...] = jnp.zeros_like(gw_ref)

        for k in range(NG):
            sl = slice(k * GW, (k + 1) * GW)
            wv = w_ref[:, sl]

            def grp(r0, _, sl=sl, wv=wv):
                rows = pl.ds(r0, NRG)
                z = z_ref[rows, sl].astype(f32)
                yv = y_ref[rows, sl].astype(f32)
                s = _sigmoid_fast(z)
                sz = z * s
                g = yv * sz
                r = _rms(g)
                d = d_ref[rows, sl].astype(f32)
                acc_ref[:, sl] += _fold(d * g * r)
                dg = _rms_bwd(g, r, d * wv)
                dy_ref[rows, sl] = (dg * sz).astype(bf16)
                dz_ref[rows, sl] = (dg * yv * s * (1.0 + z * (1.0 - s))).astype(bf16)

            _groups(ts, grp, rg=NRG)

        @pl.when(i == nt - 1)
        def _():
            _flush(acc_ref, gw_ref, 0)

    row = pl.BlockSpec((ts, DI), lambda i: (i, 0))
    return pl.pallas_call(
        body, name="gnorm_bwd", grid=(nt,),
        in_specs=[row, row, row, pl.BlockSpec((1, DI), lambda i: (0, 0)), pl.BlockSpec(memory_space=pl.ANY)],
        out_specs=[row, row, pl.BlockSpec((8, DI), lambda i: (0, 0))],
        out_shape=[jax.ShapeDtypeStruct((S, DI), bf16), jax.ShapeDtypeStruct(dproj.shape, bf16),
                   jax.ShapeDtypeStruct((8, DI), f32)],
        scratch_shapes=[pltpu.VMEM((8, DI), f32)],
        input_output_aliases={4: 1},
        compiler_params=_cp("arbitrary"),
    )(dyn, y, proj, w, dproj)


def merge_fwd(proj, ys, ya, ts):
    S = ys.shape[0]

    def body(gs_ref, ga_ref, ys_ref, ya_ref, o_ref):
        for c0 in range(0, D, CW):
            cs = slice(c0, c0 + CW)

            def grp(r0, _, cs=cs):
                rows = pl.ds(r0, NRG)
                o_ref[rows, cs] = (_sigmoid_fast(gs_ref[rows, cs].astype(f32)) * ys_ref[rows, cs].astype(f32)
                                   + _sigmoid_fast(ga_ref[rows, cs].astype(f32)) * ya_ref[rows, cs].astype(f32)
                                   ).astype(bf16)

            _groups(ts, grp, rg=NRG)

    row = pl.BlockSpec((ts, D), lambda i: (i, 0))
    return pl.pallas_call(
        body, name="merge_fwd", grid=(S // ts,),
        in_specs=[pl.BlockSpec((ts, D), lambda i: (i, C_GS // D)), pl.BlockSpec((ts, D), lambda i: (i, C_GA // D)), row, row],
        out_specs=row, out_shape=jax.ShapeDtypeStruct((S, D), bf16),
        compiler_params=_cp("parallel"),
    )(proj, proj, ys, ya)


def merge_bwd(dm, proj, ys, ya, ts):
    S = ys.shape[0]

    def body(d_ref, gs_ref, ga_ref, ys_ref, ya_ref, dys_ref, dya_ref, dg_ref):
        for c0 in range(0, D, CW):
            cs = slice(c0, c0 + CW)

            def grp(r0, _, c0=c0, cs=cs):
                rows = pl.ds(r0, NRG)
                d = d_ref[rows, cs].astype(f32)
                ss = _sigmoid_fast(gs_ref[rows, cs].astype(f32))
                sa = _sigmoid_fast(ga_ref[rows, cs].astype(f32))
                dys_ref[rows, cs] = (d * ss).astype(bf16)
                dya_ref[rows, cs] = (d * sa).astype(bf16)
                dg_ref[rows, cs] = (d * ys_ref[rows, cs].astype(f32) * ss * (1.0 - ss)).astype(bf16)
                dg_ref[rows, D + c0:D + c0 + CW] = (d * ya_ref[rows, cs].astype(f32) * sa * (1.0 - sa)).astype(bf16)

            _groups(ts, grp, rg=NRG)

    row = pl.BlockSpec((ts, D), lambda i: (i, 0))
    o = jax.ShapeDtypeStruct((S, D), bf16)
    return pl.pallas_call(
        body, name="merge_bwd", grid=(S // ts,),
        in_specs=[row, pl.BlockSpec((ts, D), lambda i: (i, C_GS // D)), pl.BlockSpec((ts, D), lambda i: (i, C_GA // D)), row, row],
        out_specs=[row, row, pl.BlockSpec((ts, 2 * D), lambda i: (i, C_GS // (2 * D)))],
        out_shape=[o, o, jax.ShapeDtypeStruct((S, PM), bf16)],
        compiler_params=_cp("parallel"),
    )(dm, proj, proj, ys, ya)


def _ssd_consts():
    h = lax.broadcasted_iota(jnp.int32, (LANES, DI), 0)
    c = lax.broadcasted_iota(jnp.int32, (LANES, DI), 1)
    expand = (c // HD == h).astype(bf16)
    r = lax.broadcasted_iota(jnp.int32, (CH, CH), 0)
    cc = lax.broadcasted_iota(jnp.int32, (CH, CH), 1)
    tril = (cc <= r).astype(bf16)
    triu = (cc >= r).astype(bf16)
    return expand, expand.T, tril, triu


def _ssd_common(xbc_ref, dtr_ref, bias_ref, alog_ref, tril_ref, expand_ref=None, saved=None):
    dtr = dtr_ref[...] + bias_ref[...]
    dt = jnp.maximum(dtr, 0.0) + jnp.log1p(jnp.exp(-jnp.abs(dtr)))
    a = -jnp.exp(alog_ref[...])
    acs = _dot3_left(tril_ref[...], dt * a)
    if saved is None:
        acsx = _dot3_right(acs, expand_ref[...])
        dtx = _dot3_right(dt, expand_ref[...])
    else:
        acsx, dtx = saved[0][...], saved[1][...]
    x = xbc_ref[:, 0:DI].astype(f32)
    xdt = x * dtx
    e = jnp.exp(acsx)
    dsx = jnp.exp(acsx[CH - 1:CH, :] - acsx)
    return dtr, dt, a, acs, acsx, dtx, x, xdt, e, dsx


def _head_halves():
    first = lax.broadcasted_iota(jnp.int32, (CH, LANES), 1) < HD
    return first, jnp.logical_not(first)


def _ssd_lmat(acs, acs_t, hh, causal):
    seg = acs[:, hh:hh + 1] - acs_t[hh:hh + 1, :]
    return jnp.where(causal, jnp.exp(jnp.minimum(seg, 0.0)), 0.0)


def ssd_fwd(xbc, dtr, bias, alog, dx_row, comm=None):
    S = xbc.shape[0]
    nc = S // CH
    expand, _, tril, _ = _ssd_consts()
    cm = _Comm(comm)

    def body(*refs):
        ins, (y_ref, hp_ref, ax_ref, dtx_ref), (h_ref, yd_ref), copies = cm.split(refs, 7, 4)
        xbc_ref, dtr_ref, bias_ref, alog_ref, dxr_ref, expand_ref, tril_ref = ins
        c = pl.program_id(0)
        cm.start(copies, c == 0)

        @pl.when(c == 0)
        def _():
            h_ref[...] = jnp.zeros_like(h_ref)

        _, _, _, acs, acsx, dtx, x, xdt, e, dsx = _ssd_common(xbc_ref, dtr_ref, bias_ref, alog_ref, tril_ref,
                                                              expand_ref=expand_ref)
        ax_ref[...] = acsx
        dtx_ref[...] = dtx
        acs_t = acs.T
        xb = xdt.astype(bf16)
        xd = (xdt * dsx).astype(bf16)
        causal = tril_ref[...] > 0
        halves = _head_halves()
        for g in range(NG):
            gs = slice(g * GW, (g + 1) * GW)
            bg = xbc_ref[:, DI + g * NS:DI + (g + 1) * NS]
            cg = xbc_ref[:, DI + NG * NS + g * NS:DI + NG * NS + (g + 1) * NS]
            cb = _dot(cg, bg, "nt")
            hp = h_ref[g]
            hpb = hp.astype(bf16)
            hp_ref[0, g] = hpb
            yd_ref[:, gs] = _dot(cg, hpb) * e[:, gs]
            h_ref[g] = hp * e[CH - 1:CH, gs] + _dot(bg, xd[:, gs], "tn")
            for k in range(NH // NG // 2):
                hh = g * (NH // NG) + 2 * k
                ps = slice(hh * HD, (hh + 2) * HD)
                xp = xb[:, ps]
                acc = None
                for o in range(2):
                    m = (cb * _ssd_lmat(acs, acs_t, hh + o, causal)).astype(bf16)
                    part = _dot(m, jnp.where(halves[o], xp, jnp.zeros_like(xp)))
                    acc = part if acc is None else acc + part
                yd_ref[:, ps] += acc
        y_ref[...] = (yd_ref[...] + dxr_ref[...] * x).astype(bf16)
        cm.wait(copies, c == nc - 1)

    par = lambda shape: pl.BlockSpec(shape, lambda c: (0,) * len(shape))
    res = pl.pallas_call(
        body, name="ssd_fwd", grid=(nc,),
        in_specs=[pl.BlockSpec((CH, CONVD), lambda c: (c, 0)), pl.BlockSpec((CH, LANES), lambda c: (c, 0)),
                  par((1, LANES)), par((1, LANES)), par((1, DI)), par((LANES, DI)), par((CH, CH))] + cm.in_specs(),
        out_specs=[pl.BlockSpec((CH, DI), lambda c: (c, 0)), pl.BlockSpec((1, NG, NS, GW), lambda c: (c, 0, 0, 0)),
                   pl.BlockSpec((CH, DI), lambda c: (c, 0)), pl.BlockSpec((CH, DI), lambda c: (c, 0))] + cm.out_specs(),
        out_shape=[jax.ShapeDtypeStruct((S, DI), bf16), jax.ShapeDtypeStruct((nc, NG, NS, GW), bf16),
                   jax.ShapeDtypeStruct((S, DI), f32), jax.ShapeDtypeStruct((S, DI), f32)] + cm.out_shape(),
        scratch_shapes=[pltpu.VMEM((NG, NS, GW), f32), pltpu.VMEM((CH, DI), f32)] + cm.scratch(),
        compiler_params=_cp("arbitrary", side_effects=bool(cm.n)),
    )(xbc, dtr, bias, alog, dx_row, expand, tril, *cm.bufs)
    return res[0], res[1], (res[2], res[3]), res[4:]


def ssd_bwd(xbc, dtr, dy, hprev, saved, bias, alog, dx_row, comm=None):
    S = xbc.shape[0]
    nc = S // CH
    _, expand_t, tril, triu = _ssd_consts()
    cm = _Comm(comm)

    def body(*refs):
        ins, outs, scr, copies = cm.split(refs, 12, 3)
        xbc_ref, dtr_ref, dy_ref, hp_ref, ax_ref, dtx_ref, bias_ref, alog_ref, dxr_ref, expt_ref, tril_ref, triu_ref = ins
        dxbc_ref, ddtr_ref, acc_ref = outs
        dh_ref, dxs_ref, t_ref, accb_ref, acca_ref, accd_ref, cc_ref, rr_ref = scr
        c = pl.program_id(0)
        cm.start(copies, c == 0)

        @pl.when(c == 0)
        def _():
            cc_ref[...] = jnp.zeros_like(cc_ref)
            rr_ref[...] = jnp.zeros_like(rr_ref)

        @pl.when(c == 0)
        def _():
            dh_ref[...] = jnp.zeros_like(dh_ref)
            accb_ref[...] = jnp.zeros_like(accb_ref)
            acca_ref[...] = jnp.zeros_like(acca_ref)
            accd_ref[...] = jnp.zeros_like(accd_ref)

        dtr, dt, a, acs, _, dtx, x, xdt, e, dsx = _ssd_common(xbc_ref, dtr_ref, bias_ref, alog_ref, tril_ref,
                                                              saved=(ax_ref, dtx_ref))
        acs_t = acs.T
        xb = xdt.astype(bf16)
        xdf = xdt * dsx
        xd = xdf.astype(bf16)
        dyv = dy_ref[...].astype(f32)
        dyb = dy_ref[...]
        dye = (dyv * e).astype(bf16)
        causal = tril_ref[...] > 0
        halves = _head_halves()
        last_row = lax.broadcasted_iota(jnp.int32, (CH, 1), 0) == CH - 1
        for g in range(NG):
            gs = slice(g * GW, (g + 1) * GW)
            bsl = slice(DI + g * NS, DI + (g + 1) * NS)
            csl = slice(DI + NG * NS + g * NS, DI + NG * NS + (g + 1) * NS)
            bg = xbc_ref[:, bsl]
            cg = xbc_ref[:, csl]
            cb = _dot(cg, bg, "nt")
            hpb = hp_ref[0, g]
            dhn = dh_ref[g]
            dhnb = dhn.astype(bf16)
            yoff = _dot(cg, hpb) * e[:, gs]
            dxd = _dot(bg, dhnb)
            t2 = dxd * xdf[:, gs]
            t3 = jnp.sum(dhn * hpb.astype(f32), axis=0, keepdims=True) * e[CH - 1:CH, gs]
            t_ref[:, gs] = dyv[:, gs] * yoff - t2 + jnp.where(last_row, jnp.sum(t2, axis=0, keepdims=True) + t3, 0.0)
            dxs_ref[:, gs] = dxd * dsx[:, gs]
            dcg = _dot(dye[:, gs], hpb, "nt")
            dbg = _dot(xd[:, gs], dhnb, "nt")
            dh_ref[g] = dhn * e[CH - 1:CH, gs] + _dot(cg, dye[:, gs], "tn")
            dcb = jnp.zeros((CH, CH), f32)
            for k in range(NH // NG // 2):
                hh0 = g * (NH // NG) + 2 * k
                ps = slice(hh0 * HD, (hh0 + 2) * HD)
                xp = xb[:, ps]
                dyp = dyb[:, ps]
                acc = None
                for o in range(2):
                    hh = hh0 + o
                    dyh = jnp.where(halves[o], dyp, jnp.zeros_like(dyp))
                    lm = _ssd_lmat(acs, acs_t, hh, causal)
                    m = cb * lm
                    dm = _dot(dyh, xp, "nt")
                    gm = dm * m
                    cc_ref[:, hh:hh + 1] = jnp.sum(gm, axis=1, keepdims=True)
                    rr_ref[hh:hh + 1, :] = jnp.sum(gm, axis=0, keepdims=True)
                    dcb = dcb + dm * lm
                    part = _dot(m.astype(bf16), dyh, "tn")
                    acc = part if acc is None else acc + part
                dxs_ref[:, ps] += acc
            dcbb = dcb.astype(bf16)
            dxbc_ref[:, csl] = (dcg + _dot(dcbb, bg)).astype(bf16)
            dxbc_ref[:, bsl] = (dbg + _dot(dcbb, cg, "tn")).astype(bf16)
        dxf = dxs_ref[...]
        dxbc_ref[:, 0:DI] = (dxf * dtx + dxr_ref[...] * dyv).astype(bf16)
        expt = expt_ref[...]
        dacs = cc_ref[...] - rr_ref[...].T + _dot2_right(t_ref[...], expt)
        dadt = _dot3_left(triu_ref[...], dacs)
        ddt = _dot2_right(dxf * x, expt) + dadt * a
        ddtr = ddt * _sigmoid(dtr)
        ddtr_ref[...] = ddtr
        accb_ref[...] += ddtr
        acca_ref[...] += dadt * dt
        accd_ref[...] += _dot2_right(dyv * x, expt)

        @pl.when(c == nc - 1)
        def _():
            acc_ref[...] = jnp.zeros_like(acc_ref)
            acc_ref[0:1, :] = jnp.sum(accb_ref[...], axis=0, keepdims=True)
            acc_ref[1:2, :] = jnp.sum(acca_ref[...], axis=0, keepdims=True) * a
            acc_ref[2:3, :] = jnp.sum(accd_ref[...], axis=0, keepdims=True)

        cm.wait(copies, c == nc - 1)

    par = lambda shape: pl.BlockSpec(shape, lambda c: (0,) * len(shape))
    rev = lambda c: (nc - 1 - c, 0)
    res = pl.pallas_call(
        body, name="ssd_bwd", grid=(nc,),
        in_specs=[pl.BlockSpec((CH, CONVD), rev), pl.BlockSpec((CH, LANES), rev), pl.BlockSpec((CH, DI), rev),
                  pl.BlockSpec((1, NG, NS, GW), lambda c: (nc - 1 - c, 0, 0, 0)),
                  pl.BlockSpec((CH, DI), rev), pl.BlockSpec((CH, DI), rev),
                  par((1, LANES)), par((1, LANES)), par((1, DI)), par((DI, LANES)),
                  par((CH, CH)), par((CH, CH))] + cm.in_specs(),
        out_specs=[pl.BlockSpec((CH, CONVD), rev), pl.BlockSpec((CH, LANES), rev), par((8, LANES))] + cm.out_specs(),
        out_shape=[jax.ShapeDtypeStruct((S, CONVD), bf16), jax.ShapeDtypeStruct((S, LANES), f32),
                   jax.ShapeDtypeStruct((8, LANES), f32)] + cm.out_shape(),
        scratch_shapes=[pltpu.VMEM((NG, NS, GW), f32), pltpu.VMEM((CH, DI), f32), pltpu.VMEM((CH, DI), f32),
                        pltpu.VMEM((CH, LANES), f32), pltpu.VMEM((CH, LANES), f32), pltpu.VMEM((CH, LANES), f32),
                        pltpu.VMEM((CH, LANES), f32), pltpu.VMEM((LANES, CH), f32)] + cm.scratch(),
        compiler_params=_cp("arbitrary", side_effects=bool(cm.n)),
    )(xbc, dtr, dy, hprev, *saved, bias, alog, dx_row, expand_t, tril, triu, *cm.bufs)
    return res[0], res[1], res[2], res[3:]


def _partner(t):
    half = AD // 2
    return jnp.concatenate([t[h * AD + o:h * AD + o + half] for h in range(t.shape[0] // AD) for o in (half, 0)], axis=0)


def _rope(t, cos, sin):
    reps = t.shape[0] // AD
    return t * jnp.tile(cos, (reps, 1)) + _partner(t) * jnp.tile(sin, (reps, 1))


def _rope_t(d, cos, sin):
    reps = d.shape[0] // AD
    return d * jnp.tile(cos, (reps, 1)) - _partner(d) * jnp.tile(sin, (reps, 1))


def _lanes_of_group(t, g):
    return jnp.concatenate([t[(g * REP + r) * AD:(g * REP + r + 1) * AD] for r in range(REP)], axis=1)


def _attn_probs(qg, k2, sink_ref, g, not_first):
    n = qg.shape[1]
    s = lax.broadcasted_iota(jnp.int32, (2 * WIN, n), 0)
    t = lax.broadcasted_iota(jnp.int32, (2 * WIN, n), 1) % WIN
    valid = jnp.logical_or(jnp.logical_and(jnp.logical_and(s < WIN, s > t), not_first),
                           jnp.logical_and(s >= WIN, s - WIN <= t))
    sink = jnp.concatenate([jnp.broadcast_to(sink_ref[0:1, g * REP + r:g * REP + r + 1], (1, WIN)) for r in range(REP)],
                           axis=1)
    sc = jnp.where(valid, _dot(k2, qg, "tn"), -1e30)
    m = jnp.maximum(jnp.max(sc, axis=0, keepdims=True), sink)
    p = jnp.exp(sc - m)
    ps = jnp.exp(sink - m)
    inv = 1.0 / (jnp.sum(p, axis=0, keepdims=True) + ps)
    return p * inv, ps * inv


def attn_fwd(qt, kvt, cos, sin, sinks):
    S = qt.shape[1]
    nb = S // WIN
    cur = lambda i: (0, i)
    prev = lambda i: (0, jnp.maximum(2 * i - 1, 0))

    def body(q_ref, kv_ref, kvp_ref, cos_ref, sin_ref, cosp_ref, sinp_ref, sink_ref, o_ref):
        i = pl.program_id(0)
        q = (_rope(q_ref[...].astype(f32), cos_ref[...], sin_ref[...]) * (AD ** -0.5)).astype(bf16)
        kc = _rope(kv_ref[0:KVW, :].astype(f32), cos_ref[...], sin_ref[...]).astype(bf16)
        kp = _rope(kvp_ref[0:KVW, :].astype(f32), cosp_ref[...], sinp_ref[...]).astype(bf16)
        k3 = jnp.concatenate([kp, kc], axis=1)
        for g in range(KVH):
            ks = slice(g * AD, (g + 1) * AD)
            vs = slice(KVW + g * AD, KVW + (g + 1) * AD)
            v3 = jnp.concatenate([kvp_ref[vs, :], kv_ref[vs, :]], axis=1)
            for b in range(2):
                win = slice(b * WIN, (b + 2) * WIN)
                qg = _lanes_of_group(q[:, b * WIN:(b + 1) * WIN], g)
                p, _ = _attn_probs(qg, k3[ks, win], sink_ref, g, jnp.logical_or(i > 0, b > 0))
                o = _dot(v3[:, win], p.astype(bf16))
                for r in range(REP):
                    h = g * REP + r
                    o_ref[h * AD:(h + 1) * AD, b * WIN:(b + 1) * WIN] = o[:, r * WIN:(r + 1) * WIN].astype(bf16)

    tab = pl.BlockSpec((AD, 2 * WIN), cur)
    tabp = pl.BlockSpec((AD, WIN), prev)
    return pl.pallas_call(
        body, name="attn_fwd", grid=(nb // 2,),
        in_specs=[pl.BlockSpec((D, 2 * WIN), cur), pl.BlockSpec((2 * KVW, 2 * WIN), cur),
                  pl.BlockSpec((2 * KVW, WIN), prev), tab, tab, tabp, tabp, pl.BlockSpec((1, LANES), lambda i: (0, 0))],
        out_specs=pl.BlockSpec((D, 2 * WIN), cur),
        out_shape=jax.ShapeDtypeStruct((D, S), bf16),
        compiler_params=_cp("parallel"),
    )(qt, kvt, kvt, cos, sin, cos, sin, sinks)


def attn_bwd(qt, kvt, cos, sin, sinks, daot, comm=None):
    S = qt.shape[1]
    nb = S // WIN
    cur = lambda i: (0, jnp.minimum(i, nb - 1))
    prev = lambda i: (0, jnp.maximum(i - 1, 0))
    cm = _Comm(comm)

    def body(*refs):
        ins, (dq_ref, dkv_ref, ds_ref), scr, copies = cm.split(refs, 9, 3)
        q_ref, kv_ref, kvp_ref, cos_ref, sin_ref, cosp_ref, sinp_ref, sink_ref, do_ref = ins
        ck_ref, cv_ref, dqs_ref, dkp_ref, dvp_ref, dkc_ref, dvc_ref, accs_ref = scr
        i = pl.program_id(0)
        cm.start(copies, i == 0)

        @pl.when(i == 0)
        def _():
            ck_ref[...] = jnp.zeros_like(ck_ref)
            cv_ref[...] = jnp.zeros_like(cv_ref)
            accs_ref[...] = jnp.zeros_like(accs_ref)

        @pl.when(i == nb)
        def _():
            dkp_ref[...] = jnp.zeros_like(dkp_ref)
            dvp_ref[...] = jnp.zeros_like(dvp_ref)

        @pl.when(i < nb)
        def _():
            q = (_rope(q_ref[...].astype(f32), cos_ref[...], sin_ref[...]) * (AD ** -0.5)).astype(bf16)
            kc = _rope(kv_ref[0:KVW, :].astype(f32), cos_ref[...], sin_ref[...]).astype(bf16)
            kp = _rope(kvp_ref[0:KVW, :].astype(f32), cosp_ref[...], sinp_ref[...]).astype(bf16)
            do = do_ref[...]
            for g in range(KVH):
                ks = slice(g * AD, (g + 1) * AD)
                vs = slice(KVW + g * AD, KVW + (g + 1) * AD)
                qg = _lanes_of_group(q, g)
                dog = _lanes_of_group(do, g)
                k2 = jnp.concatenate([kp[ks], kc[ks]], axis=1)
                v2 = jnp.concatenate([kvp_ref[vs, :], kv_ref[vs, :]], axis=1)
                p, ps = _attn_probs(qg, k2, sink_ref, g, i > 0)
                dp = _dot(v2, dog, "tn")
                delta = jnp.sum(p * dp, axis=0, keepdims=True)
                ds = (p * (dp - delta)).astype(bf16)
                accs_ref[g:g + 1, :] -= ps * delta
                dqg = _dot(k2, ds) * (AD ** -0.5)
                for r in range(REP):
                    h = g * REP + r
                    dqs_ref[h * AD:(h + 1) * AD, :] = dqg[:, r * WIN:(r + 1) * WIN]
                dk2 = _dot(qg, ds, "nt")
                dv2 = _dot(dog, p.astype(bf16), "nt")
                dkp_ref[ks, :] = dk2[:, 0:WIN]
                dkc_ref[ks, :] = dk2[:, WIN:2 * WIN]
                dvp_ref[ks, :] = dv2[:, 0:WIN]
                dvc_ref[ks, :] = dv2[:, WIN:2 * WIN]
            dq_ref[...] = _rope_t(dqs_ref[...], cos_ref[...], sin_ref[...]).astype(bf16)

        dkv_ref[0:KVW, :] = _rope_t(ck_ref[...] + dkp_ref[...], cosp_ref[...], sinp_ref[...]).astype(bf16)
        dkv_ref[KVW:2 * KVW, :] = (cv_ref[...] + dvp_ref[...]).astype(bf16)

        @pl.when(i < nb)
        def _():
            ck_ref[...] = dkc_ref[...]
            cv_ref[...] = dvc_ref[...]

        @pl.when(i == nb)
        def _():
            lane = lax.broadcasted_iota(jnp.int32, (1, LANES), 1)
            row = jnp.zeros((1, LANES), f32)
            for h in range(AH):
                part = accs_ref[h // REP:h // REP + 1, (h % REP) * WIN:(h % REP + 1) * WIN]
                row = row + jnp.where(lane == h, jnp.sum(part, axis=1, keepdims=True), 0.0)
            ds_ref[...] = jnp.zeros_like(ds_ref)
            ds_ref[0:1, :] = row

        cm.wait(copies, i == nb)

    tab = pl.BlockSpec((AD, WIN), cur)
    tabp = pl.BlockSpec((AD, WIN), prev)
    kvs = lambda: pltpu.VMEM((KVW, WIN), f32)
    res = pl.pallas_call(
        body, name="attn_bwd", grid=(nb + 1,),
        in_specs=[pl.BlockSpec((D, WIN), cur), pl.BlockSpec((2 * KVW, WIN), cur), pl.BlockSpec((2 * KVW, WIN), prev),
                  tab, tab, tabp, tabp, pl.BlockSpec((1, LANES), lambda i: (0, 0)),
                  pl.BlockSpec((D, WIN), cur)] + cm.in_specs(),
        out_specs=[pl.BlockSpec((D, WIN), cur), pl.BlockSpec((2 * KVW, WIN), prev),
                   pl.BlockSpec((8, LANES), lambda i: (0, 0))] + cm.out_specs(),
        out_shape=[jax.ShapeDtypeStruct((D, S), bf16), jax.ShapeDtypeStruct((2 * KVW, S), bf16),
                   jax.ShapeDtypeStruct((8, LANES), f32)] + cm.out_shape(),
        scratch_shapes=[kvs(), kvs(), pltpu.VMEM((D, WIN), f32), kvs(), kvs(), kvs(), kvs(),
                        pltpu.VMEM((8, REP * WIN), f32)] + cm.scratch(),
        compiler_params=_cp("arbitrary", side_effects=bool(cm.n)),
    )(qt, kvt, kvt, cos, sin, cos, sin, sinks, daot, *cm.bufs)
    return res[0], res[1], res[2], res[3:]


ADAM_C1 = 1.0 / (1.0 - ADAM_B1 ** ADAM_STEP)
ADAM_C2 = 1.0 / (1.0 - ADAM_B2 ** ADAM_STEP)


def _adam_update(g, w, m, v):
    nm = ADAM_B1 * m + (1.0 - ADAM_B1) * g
    nv = ADAM_B2 * v + (1.0 - ADAM_B2) * (g * g)
    return -ADAM_LR * ((nm * ADAM_C1) / (jnp.sqrt(nv * ADAM_C2) + ADAM_EPS) + ADAM_WD * w), nm, nv


def adamw(parts, w, m, v, tr, name):
    n, R, C = parts.shape

    def body(p_ref, w_ref, m_ref, v_ref, g_ref, d_ref, nm_ref, nv_ref):
        def grp(g0, _):
            r0 = pl.multiple_of(g0 * RG, RG)
            rows = pl.ds(r0, RG)
            g = p_ref[0, rows, :].astype(f32)
            for k in range(1, n):
                g = g + p_ref[k, rows, :].astype(f32)
            d, nm, nv = _adam_update(g, w_ref[rows, :], m_ref[rows, :], v_ref[rows, :])
            g_ref[rows, :] = g
            d_ref[rows, :] = d
            nm_ref[rows, :] = nm
            nv_ref[rows, :] = nv
            return 0

        lax.fori_loop(0, tr // RG, grp, 0)

    row = pl.BlockSpec((tr, C), lambda i: (i, 0))
    o = jax.ShapeDtypeStruct((R, C), f32)
    return pl.pallas_call(
        body, name=name, grid=(R // tr,),
        in_specs=[pl.BlockSpec((n, tr, C), lambda i: (0, i, 0)), row, row, row],
        out_specs=[row, row, row, row], out_shape=[o, o, o, o],
        compiler_params=_cp("parallel"),
    )(parts, w, m, v)


SMALL_ROW = (("norm_mix_post_w", D), ("norm_ffn_pre_w", D), ("norm_ffn_post_w", D), ("ssd_norm_w", DI),
             ("ssd_conv_b", CONVD), ("ffn_conv_b", 2 * FF), ("ssd_dt_bias", NH), ("ssd_a_log", NH), ("ssd_d", NH),
             ("attn_sinks", AH), ("loss", 1))
CONV_BLOCK = 1152
SSD_CONV_COLS = CONVD // N_DEV
FFN_CONV_COLS = 2 * FF // N_DEV


def _row_offsets():
    off, o = {}, 0
    for name, n in SMALL_ROW:
        off[name] = (o, n)
        o += -(-n // LANES) * LANES
    return off, o


def adamw_small(recv_row, recv_pre, recv_conv, params):
    off, _ = _row_offsets()
    names = list(params)
    n = len(names)

    def total(ref, rows, lo, width):
        g = ref[0, rows, lo:lo + width]
        for d in range(1, N_DEV):
            g = g + ref[d, rows, lo:lo + width]
        return g

    def grad_of(name, row_ref, pre_ref, conv_ref):
        if name == "norm_mix_pre_w":
            return total(pre_ref, slice(0, 1), 0, D)
        if name == "ssd_conv_w":
            return total(conv_ref, slice(0, SSD_K), 0, SSD_CONV_COLS)
        if name == "ffn_conv_w":
            return total(conv_ref, slice(0, FFN_K), 3 * LANES, FFN_CONV_COLS)
        o, width = off[name]
        return total(row_ref, slice(0, 1), o, width)

    def body(row_ref, pre_ref, conv_ref, *refs):
        ins, outs = refs[:3 * n], refs[3 * n:]
        for k, name in enumerate(names):
            w_ref, m_ref, v_ref = ins[3 * k:3 * k + 3]
            g_ref, d_ref, nm_ref, nv_ref = outs[4 * k:4 * k + 4]
            g = grad_of(name, row_ref, pre_ref, conv_ref)
            d, nm, nv = _adam_update(g, w_ref[...], m_ref[...], v_ref[...])
            g_ref[...] = g
            d_ref[...] = d
            nm_ref[...] = nm
            nv_ref[...] = nv
        outs[4 * n][...] = total(row_ref, slice(0, 1), off["loss"][0], LANES)

    flat = [t for name in names for t in params[name]]
    out_shape = [jax.ShapeDtypeStruct(params[name][0].shape, f32) for name in names for _ in range(4)]
    res = pl.pallas_call(
        body, name="adamw_small",
        out_shape=out_shape + [jax.ShapeDtypeStruct((1, LANES), f32)],
        compiler_params=pltpu.CompilerParams(vmem_limit_bytes=VMEM_LIMIT),
    )(recv_row, recv_pre, recv_conv, *flat)
    return {name: res[4 * k:4 * k + 4] for k, name in enumerate(names)}, res[4 * n]


def _cat_rows(parts):
    words = [lax.bitcast_convert_type(p, jnp.uint16) for p in parts]
    return lax.bitcast_convert_type(jnp.concatenate(words, axis=0), bf16)


def _pad_rows8(w):
    return jnp.pad(w, ((0, 8 - w.shape[0]), (0, 0)))


def _pad_lanes(v):
    return jnp.pad(v.reshape(1, -1), ((0, 0), (0, LANES - v.size)))


WEIGHTS = ('norm_mix_pre_w', 'w_in', 'ssd_conv_w', 'ssd_conv_b', 'ssd_dt_bias', 'ssd_a_log', 'ssd_d', 'ssd_norm_w',
           'ssd_w_out', 'attn_sinks', 'attn_w_out', 'w_mix_out', 'norm_mix_post_w', 'norm_ffn_pre_w', 'ffn_w_up',
           'ffn_conv_w', 'ffn_conv_b', 'ffn_w_down', 'norm_ffn_post_w')
W_IN_ROWS = IN_DIM // N_DEV
W_IN_PAD = 1104
W_IN_SPLIT = (672, 768, 832)
TS = 512
TS_NORM = 1024


def kernel(x, positions, norm_mix_pre_w, w_in, ssd_conv_w, ssd_conv_b, ssd_dt_bias, ssd_a_log, ssd_d, ssd_norm_w, ssd_w_out, attn_sinks, attn_w_out, w_mix_out, norm_mix_post_w, norm_ffn_pre_w, ffn_w_up, ffn_conv_w, ffn_conv_b, ffn_w_down, norm_ffn_post_w, loss_target, m_norm_mix_pre_w, m_w_in, m_ssd_conv_w, m_ssd_conv_b, m_ssd_dt_bias, m_ssd_a_log, m_ssd_d, m_ssd_norm_w, m_ssd_w_out, m_attn_sinks, m_attn_w_out, m_w_mix_out, m_norm_mix_post_w, m_norm_ffn_pre_w, m_ffn_w_up, m_ffn_conv_w, m_ffn_conv_b, m_ffn_w_down, m_norm_ffn_post_w, v_norm_mix_pre_w, v_w_in, v_ssd_conv_w, v_ssd_conv_b, v_ssd_dt_bias, v_ssd_a_log, v_ssd_d, v_ssd_norm_w, v_ssd_w_out, v_attn_sinks, v_attn_w_out, v_w_mix_out, v_norm_mix_post_w, v_norm_ffn_pre_w, v_ffn_w_up, v_ffn_conv_w, v_ffn_conv_b, v_ffn_w_down, v_norm_ffn_post_w):
    a = locals()
    r2 = lambda t: t.reshape(t.shape[-2], t.shape[-1])
    w = {n: r2(a[n]) for n in WEIGHTS}
    m = {n: r2(a["m_" + n]) for n in WEIGHTS}
    v = {n: r2(a["v_" + n]) for n in WEIGHTS}
    xs, target = x[0], loss_target[0]
    S = xs.shape[0]
    ts, tsn = TS, min(TS_NORM, S)

    w_in_blk = jnp.pad(w["w_in"].T.astype(bf16), ((0, W_IN_PAD - W_IN_ROWS), (0, 0)))
    conv_blk = jnp.concatenate([_pad_rows8(w["ssd_conv_w"]), _pad_rows8(w["ffn_conv_w"]),
                                jnp.zeros((8, CONV_BLOCK - SSD_CONV_COLS - FFN_CONV_COLS), f32)], axis=1)
    u, cos, sin, (g_in, g_conv) = prenorm_fwd(xs, w["norm_mix_pre_w"], positions, ts, [w_in_blk, conv_blk])
    wt = g_in[:, :W_IN_ROWS].reshape(IN_DIM, D)
    w_main_t = _cat_rows([wt[IN_OFF[0]:IN_OFF[1]], wt[IN_OFF[6]:IN_OFF[8]], wt[IN_OFF[1]:IN_OFF[2]]])
    w_q_t = wt[IN_OFF[3]:IN_OFF[4]]
    w_kv_t = wt[IN_OFF[4]:IN_OFF[6]]
    w_dt_t = jnp.pad(wt[IN_OFF[2]:IN_OFF[3]], ((0, LANES - NH), (0, 0)))
    conv_w8 = g_conv[:, :, 0:SSD_CONV_COLS].transpose(1, 0, 2).reshape(8, CONVD)
    fconv_w8 = g_conv[:, :, SSD_CONV_COLS:SSD_CONV_COLS + FFN_CONV_COLS].transpose(1, 0, 2).reshape(8, 2 * FF)
    bias = _pad_lanes(w["ssd_dt_bias"])
    alog = _pad_lanes(w["ssd_a_log"])
    dx_row = jnp.repeat(w["ssd_d"].reshape(-1), HD).reshape(1, DI)
    sinks = _pad_lanes(w["attn_sinks"])

    later = [w["ssd_w_out"].astype(bf16), w["attn_w_out"].astype(bf16), w["w_mix_out"].astype(bf16)]
    proj, (g_so, g_ao, g_mix) = mm(u, w_main_t, "nt", bf16, "mm_proj", comm=(later, (False,) * 3))
    w_ssd_out, w_attn_out, w_mix = g_so.reshape(DI, D), g_ao.reshape(D, D), g_mix.reshape(D, D)
    qt = mm(w_q_t, u, "nt", bf16, "mm_q")
    kvt = mm(w_kv_t, u, "nt", bf16, "mm_kv")
    dtr = mm(u, w_dt_t, "nt", f32, "mm_dt")
    xbc, conv_c = ssdconv_fwd(proj, conv_w8, w["ssd_conv_b"], ts)
    y, hprev, ssd_saved, (g_up, g_down) = ssd_fwd(xbc, dtr, bias, alog, dx_row, comm=(
        [w["ffn_w_up"].T.astype(bf16), w["ffn_w_down"].astype(bf16)], (False, False)))
    w_up_t = g_up.reshape(2 * FF, D)
    w_down = g_down.reshape(FF, D)
    yn = gnorm_fwd(y, proj, w["ssd_norm_w"], tsn)
    ys = mm(yn, w_ssd_out, "nn", bf16, "mm_ssd_out")
    aot = attn_fwd(qt, kvt, cos, sin, sinks)
    ya = mm(aot, w_attn_out, "tn", bf16, "mm_attn_out")
    merged = merge_fwd(proj, ys, ya, tsn)
    mo = mm(merged, w_mix, "nn", bf16, "mm_mix")
    x1, h = post_fwd(xs, mo, w["norm_mix_post_w"], w["norm_ffn_pre_w"], tsn)
    up = mm(h, w_up_t, "nt", bf16, "mm_up")
    act, gate_c, val_c = ffnact_fwd(up, fconv_w8, w["ffn_conv_b"], ts)
    ff = mm(act, w_down, "nn", bf16, "mm_down")
    loss_blk, dout, dff, g_post2 = loss_head(x1, ff, target, w["norm_ffn_post_w"], tsn)

    dact = mm(dff, w_down, "nt", bf16, "mm_dact")
    gw_down = mm(act, dff, "tn", bf16, "mm_g_down")
    dgate, dval = ffnact_bwd(dact, gate_c, val_c, ts)
    dup_pre, g_fconv_a = dwconv_bwd(dgate, up, 0, fconv_w8, 0, FFN_K, FF, ts, "ffnconv_bwd_gate", out_cols=2 * FF)
    dup_pre, g_fconv_b = dwconv_bwd(dval, up, 1, fconv_w8, 1, FFN_K, FF, ts, "ffnconv_bwd_val", into=dup_pre, ocb=1,
                                    out_cols=2 * FF)
    g_fconv = jnp.concatenate([g_fconv_a, g_fconv_b], axis=1)
    dh, (r_down,) = mm(dup_pre, w_up_t, "nn", bf16, "mm_dh", comm=([gw_down.reshape(N_DEV, FF // N_DEV, D)], (True,)))
    gw_up_t = mm(dup_pre, h, "tn", bf16, "mm_g_up")
    dx1, dmo, g_norms = post_bwd(dout, dh, x1, mo, w["norm_mix_post_w"], w["norm_ffn_pre_w"], tsn)
    dmerged = mm(dmo, w_mix, "nt", bf16, "mm_dmerged")
    gw_mix = mm(merged, dmo, "tn", bf16, "mm_g_mix")
    dys, dya, dproj = merge_bwd(dmerged, proj, ys, ya, tsn)
    daot = mm(w_attn_out, dya, "nt", bf16, "mm_dao")
    gw_attn_out = mm(aot, dya, "nn", bf16, "mm_g_attn_out")
    dqt, dkvt, g_sinks, (r_up,) = attn_bwd(qt, kvt, cos, sin, sinks, daot,
                                           comm=([gw_up_t.reshape(N_DEV, 2 * FF // N_DEV, D)], (True,)))
    dyn = mm(dys, w_ssd_out, "nt", bf16, "mm_dyn")
    gw_ssd_out = mm(yn, dys, "tn", bf16, "mm_g_ssd_out")
    dy, dproj, g_gnorm = gnorm_bwd(dyn, y, proj, w["ssd_norm_w"], dproj, tsn)
    sends = [gw_ssd_out.reshape(N_DEV, DI // N_DEV, D), gw_attn_out.reshape(N_DEV, D // N_DEV, D),
             gw_mix.reshape(N_DEV, D // N_DEV, D)]
    dxbc, ddtr, g_ssd, (r_so, r_ao, r_mix) = ssd_bwd(xbc, dtr, dy, hprev, ssd_saved, bias, alog, dx_row,
                                                     comm=(sends, (True,) * 3))
    dproj, g_conv_w = dwconv_bwd(dxbc, proj, C_XBC // 1024, conv_w8, 0, SSD_K, 1024, ts, "ssdconv_bwd", act_c=conv_c,
                                 into=dproj, ocb=C_XBC // 1024, out_cols=PM)
    ddtr_b = ddtr.astype(bf16)
    du_c = mm(ddtr_b, w_dt_t, "nn", bf16, "mm_du_dt")
    g_main_t = mm(dproj, u, "tn", bf16, "mm_g_in")
    g_q_t = mm(dqt, u, "nn", bf16, "mm_g_q")
    g_kv_t = mm(dkvt, u, "nn", bf16, "mm_g_kv")
    g_dt_t = mm(ddtr_b, u, "tn", bf16, "mm_g_dt")
    g_wt = _cat_rows([g_main_t[C_Z:C_GS], g_main_t[C_XBC:PM], g_dt_t[:NH], g_q_t, g_kv_t, g_main_t[C_GS:C_XBC]])
    send_in = jnp.pad(g_wt.reshape(N_DEV, W_IN_ROWS, D), ((0, 0), (0, W_IN_PAD - W_IN_ROWS), (0, 0)))
    pieces = {"norm_mix_post_w": g_norms[1:2], "norm_ffn_pre_w": g_norms[0:1], "norm_ffn_post_w": g_post2[0:1],
              "ssd_norm_w": g_gnorm[0:1], "ssd_conv_b": g_conv_w[7:8], "ffn_conv_b": g_fconv[7:8],
              "ssd_dt_bias": g_ssd[0:1], "ssd_a_log": g_ssd[1:2], "ssd_d": g_ssd[2:3], "attn_sinks": g_sinks[0:1],
              "loss": loss_blk[0:1]}
    row = jnp.concatenate([jnp.pad(pieces[n][:, :min(k, pieces[n].shape[1])],
                                   ((0, 0), (0, -(-k // LANES) * LANES - min(k, pieces[n].shape[1]))))
                           for n, k in SMALL_ROW], axis=1)
    send_row = jnp.pad(row, ((0, 7), (0, 0)))
    send_conv = jnp.concatenate(
        [g_conv_w.reshape(8, N_DEV, SSD_CONV_COLS).transpose(1, 0, 2),
         g_fconv.reshape(8, N_DEV, FFN_CONV_COLS).transpose(1, 0, 2),
         jnp.zeros((N_DEV, 8, CONV_BLOCK - SSD_CONV_COLS - FFN_CONV_COLS), f32)], axis=2)
    r0, r1, r2 = W_IN_SPLIT
    du_a, (r_in_a, recv_row, recv_conv) = mm(dproj, w_main_t, "nn", bf16, "mm_du", comm=(
        [send_in[:, :r0], send_row, send_conv], (True, False, True)))
    du_d, (r_in_b,) = mm(dqt, w_q_t, "tn", bf16, "mm_du_q", comm=([send_in[:, r0:r1]], (True,)))
    du_b, (r_in_c,) = mm(dkvt, w_kv_t, "tn", bf16, "mm_du_kv", comm=([send_in[:, r1:r2]], (True,)))
    grad_x, g_pre, (r_in_d,) = prenorm_bwd(xs, w["norm_mix_pre_w"], (du_a, du_b, du_c, du_d), dx1, tsn,
                                           comm=([send_in[:, r2:]], (True,)))
    (recv_pre,) = exchange([g_pre], (False,), "gather_last")

    r_in = jnp.concatenate([r_in_a, r_in_b, r_in_c, r_in_d], axis=1)
    tpad = lambda t: jnp.pad(t.T, ((0, W_IN_PAD - W_IN_ROWS), (0, 0)))
    o_in = [t[:W_IN_ROWS].T for t in adamw(r_in, tpad(w["w_in"]), tpad(m["w_in"]), tpad(v["w_in"]), 368, "adamw_w_in")]
    o_up = [t.T for t in adamw(r_up, w["ffn_w_up"].T, m["ffn_w_up"].T, v["ffn_w_up"].T, 352, "adamw_w_up")]
    big = {"w_in": o_in, "ffn_w_up": o_up,
           "ssd_w_out": adamw(r_so, w["ssd_w_out"], m["ssd_w_out"], v["ssd_w_out"], 256, "adamw_ssd_out"),
           "attn_w_out": adamw(r_ao, w["attn_w_out"], m["attn_w_out"], v["attn_w_out"], 128, "adamw_attn_out"),
           "w_mix_out": adamw(r_mix, w["w_mix_out"], m["w_mix_out"], v["w_mix_out"], 128, "adamw_mix"),
           "ffn_w_down": adamw(r_down, w["ffn_w_down"], m["ffn_w_down"], v["ffn_w_down"], 352, "adamw_down")}
    small_names = [n for n in WEIGHTS if n not in big]
    small, loss_row = adamw_small(recv_row, recv_pre, recv_conv, {n: (w[n], m[n], v[n]) for n in small_names})

    outs = [loss_row[0, 0], grad_x[None]]
    for k in range(4):
        for n in WEIGHTS:
            outs.append((big[n][k] if n in big else small[n][k]).reshape(a[n].shape))
    return tuple(outs)
```

```python
import jax
import jax.numpy as jnp
import numpy as np
from jax import lax
from jax.experimental import pallas as pl
from jax.experimental.pallas import tpu as pltpu

f32 = jnp.float32
bf16 = jnp.bfloat16

N_DEV = 8
D = 1024
DI = 2048
NH = 32
HD = 64
NG = 4
GW = DI // NG
NS = 128
CH = 128
CONVD = DI + 2 * NG * NS
SSD_K = 4
AH = 16
AD = 64
KVH = 4
REP = AH // KVH
KVW = KVH * AD
WIN = 128
FF = 2816
FFN_K = 3
EPS = 1e-6
ROPE_THETA = 10000.0
LANES = 128
RG = 16
CW = 256

C_Z, C_GS, C_GA, C_XBC, PM = 0, 2048, 3072, 4096, 7168
IN_SIZES = (DI, CONVD, NH, D, KVW, KVW, D, D)
IN_OFF = tuple(int(v) for v in np.cumsum((0,) + IN_SIZES))
IN_DIM = IN_OFF[-1]

ADAM_LR, ADAM_B1, ADAM_B2, ADAM_EPS, ADAM_WD, ADAM_STEP = 0.001, 0.9, 0.999, 1e-08, 0.01, 10

VMEM_LIMIT = 56 * 1024 * 1024


def _cp(*sem, side_effects=False):
    return pltpu.CompilerParams(dimension_semantics=sem, vmem_limit_bytes=VMEM_LIMIT, has_side_effects=side_effects)


def _dot(a, b, mode="nn"):
    dims = {"nn": (((1,), (0,)), ((), ())), "nt": (((1,), (1,)), ((), ())), "tn": (((0,), (0,)), ((), ()))}[mode]
    return lax.dot_general(a, b, dims, preferred_element_type=f32)


def _split3(v):
    hi = v.astype(bf16)
    r = v - hi.astype(f32)
    mid = r.astype(bf16)
    lo = (r - mid.astype(f32)).astype(bf16)
    return hi, mid, lo


def _dot3_left(m01, v):
    hi, mid, lo = _split3(v)
    return _dot(m01, hi) + _dot(m01, mid) + _dot(m01, lo)


def _dot3_right(v, m01):
    hi, mid, lo = _split3(v)
    return _dot(hi, m01) + _dot(mid, m01) + _dot(lo, m01)


def _dot2_right(v, m01):
    hi = v.astype(bf16)
    lo = (v - hi.astype(f32)).astype(bf16)
    return _dot(hi, m01) + _dot(lo, m01)


def _sigmoid(x):
    return 1.0 / (1.0 + jnp.exp(-x))


def _sigmoid_fast(x):
    return pl.reciprocal(1.0 + jnp.exp(-x), approx=True)


def _peer(k, x, y, c):
    return ((1 - x) if k & 4 else x, (1 - y) if k & 2 else y, (1 - c) if k & 1 else c)


def _xchg_copies(buf_refs, out_refs, send_sems, recv_sems, local_sems, personalised):
    x, y, c = lax.axis_index("x"), lax.axis_index("y"), lax.axis_index("c")
    me = 4 * x + 2 * y + c
    local, remote = [], []
    for b, (buf, out, pers) in enumerate(zip(buf_refs, out_refs, personalised)):
        local.append(pltpu.make_async_copy(buf.at[me] if pers else buf, out.at[me], local_sems.at[b]))
        for k in range(1, N_DEV):
            px, py, pc = _peer(k, x, y, c)
            s = b * (N_DEV - 1) + k - 1
            remote.append(pltpu.make_async_remote_copy(
                src_ref=buf.at[4 * px + 2 * py + pc] if pers else buf, dst_ref=out.at[me],
                send_sem=send_sems.at[s], recv_sem=recv_sems.at[s],
                device_id=(px, py, pc), device_id_type=pl.DeviceIdType.MESH))
    return local, remote


class _Comm:
    def __init__(self, comm):
        self.bufs, self.pers = comm if comm else ((), ())
        self.n = len(self.bufs)

    def in_specs(self):
        return [pl.BlockSpec(memory_space=pl.ANY)] * self.n

    out_specs = in_specs

    def out_shape(self):
        return [jax.ShapeDtypeStruct((N_DEV,) + tuple(b.shape[1:] if p else b.shape), b.dtype)
                for b, p in zip(self.bufs, self.pers)]

    def scratch(self):
        n = self.n
        return [pltpu.SemaphoreType.DMA((n * (N_DEV - 1),)), pltpu.SemaphoreType.DMA((n * (N_DEV - 1),)),
                pltpu.SemaphoreType.DMA((n,))] if n else []

    def split(self, refs, n_in, n_out):
        n = self.n
        ins, outs = refs[:n_in], refs[n_in + n:n_in + n + n_out]
        rest = refs[n_in + n + n_out + n:]
        if not n:
            return ins, outs, rest, None
        copies = _xchg_copies(refs[n_in:n_in + n], refs[n_in + n + n_out:n_in + n + n_out + n], *rest[-3:], self.pers)
        return ins, outs, rest[:-3], copies

    def start(self, copies, first):
        if copies:
            @pl.when(first)
            def _():
                for cp in copies[0] + copies[1]:
                    cp.start()

    def wait(self, copies, last):
        if copies:
            @pl.when(last)
            def _():
                for cp in copies[1]:
                    cp.wait_recv()
                for cp in copies[1]:
                    cp.wait_send()
                for cp in copies[0]:
                    cp.wait()


def exchange(bufs, personalised, name):
    cm = _Comm((bufs, personalised))

    def body(*refs):
        _, _, _, copies = cm.split(refs, 0, 0)
        cm.start(copies, True)
        cm.wait(copies, True)

    return pl.pallas_call(
        body, name=name, in_specs=cm.in_specs(), out_specs=cm.out_specs(), out_shape=cm.out_shape(),
        scratch_shapes=cm.scratch(), compiler_params=pltpu.CompilerParams(has_side_effects=True),
    )(*bufs)


class _TwoLevelGather:
    def __init__(self, bufs):
        self.bufs = list(bufs)
        self.n = len(self.bufs)

    def in_specs(self):
        return [pl.BlockSpec(memory_space=pl.ANY)] * self.n

    out_specs = in_specs

    def out_shape(self):
        return [jax.ShapeDtypeStruct((N_DEV,) + tuple(b.shape), b.dtype) for b in self.bufs]

    def scratch(self):
        per = N_DEV - 1
        return [pltpu.SemaphoreType.DMA((self.n * per,)), pltpu.SemaphoreType.DMA((self.n * per,)),
                pltpu.SemaphoreType.DMA((self.n,))]

    def bind(self, ins, outs, send_sems, recv_sems, local_sems):
        n, per = self.n, N_DEV - 1
        x, y, c = lax.axis_index("x"), lax.axis_index("y"), lax.axis_index("c")
        me, sibling = (x, y, c), (x, y, 1 - c)
        chips = [(1 - x, y), (x, 1 - y), (1 - x, 1 - y)]

        def copy(b, k, block, to, src=None):
            dst = outs[b].at[4 * block[0] + 2 * block[1] + block[2]]
            return pltpu.make_async_remote_copy(
                src_ref=dst if src is None else src, dst_ref=dst,
                send_sem=send_sems.at[b * per + k], recv_sem=recv_sems.at[b * per + k],
                device_id=to, device_id_type=pl.DeviceIdType.MESH)

        mine = [pltpu.make_async_copy(ins[b], outs[b].at[4 * x + 2 * y + c], local_sems.at[b]) for b in range(n)]
        first = []
        for b in range(n):
            first.append(copy(b, 0, me, sibling, src=ins[b]))
            first += [copy(b, 1 + j, me, (*chip, c), src=ins[b]) for j, chip in enumerate(chips)]

        def start():
            for cp in mine + first:
                cp.start()

        def finish():
            passed = []
            for j, chip in enumerate(chips):
                for b in range(n):
                    copy(b, 1 + j, (*chip, c), me).wait_recv()
                    passed.append(copy(b, 4 + j, (*chip, c), sibling))
                    passed[-1].start()
            for b in range(n):
                copy(b, 0, sibling, me).wait_recv()
                for j, chip in enumerate(chips):
                    copy(b, 4 + j, (*chip, 1 - c), me).wait_recv()
            for cp in first + passed:
                cp.wait_send()
            for cp in mine:
                cp.wait()

        return start, finish


MM_TILES = (3584, 2176, 2048, 1792, 1408, 1024, 512, 256, 128)
MM_K_TILES = (3584, 2816) + MM_TILES[1:]
MM_VMEM_BUDGET = 40 * 1024 * 1024


def _mm_tiles(M, N, K, out_bytes):
    cm = [t for t in MM_TILES if M % t == 0]
    cn = [t for t in MM_TILES if N % t == 0]
    ck = [t for t in MM_K_TILES if K % t == 0]
    best = None
    for bm in cm[:2]:
        for bn in cn:
            for bk in ck:
                need = 4 * (bm * bk + bk * bn) + bm * bn * (4 + 2 * out_bytes)
                if need <= MM_VMEM_BUDGET:
                    score = (bm * bn * bk, bk)
                    if best is None or score > best[0]:
                        best = (score, (bm, bn, bk))
    return best[1]


def mm(a, b, mode, out_dtype, name, comm=None):
    if mode == "nn":
        (M, K), (_, N) = a.shape, b.shape
    elif mode == "nt":
        (M, K), (N, _) = a.shape, b.shape
    else:
        (K, M), (_, N) = a.shape, b.shape
    bm, bn, bk = _mm_tiles(M, N, K, jnp.dtype(out_dtype).itemsize)
    gm, gn, nk = M // bm, N // bn, K // bk
    cm = _Comm(comm)

    def body(*refs):
        (a_ref, b_ref), (o_ref,), scr, copies = cm.split(refs, 2, 1)
        i, j, k = pl.program_id(0), pl.program_id(1), pl.program_id(2)
        cm.start(copies, jnp.logical_and(jnp.logical_and(i == 0, j == 0), k == 0))
        p = _dot(a_ref[...], b_ref[...], mode)
        if nk == 1:
            o_ref[...] = p.astype(o_ref.dtype)
        else:
            acc_ref = scr[0]

            @pl.when(k == 0)
            def _():
                acc_ref[...] = p

            @pl.when(k > 0)
            def _():
                acc_ref[...] += p

            @pl.when(k == nk - 1)
            def _():
                o_ref[...] = acc_ref[...].astype(o_ref.dtype)

        cm.wait(copies, jnp.logical_and(jnp.logical_and(i == gm - 1, j == gn - 1), k == nk - 1))

    if mode == "nn":
        a_spec = pl.BlockSpec((bm, bk), lambda i, j, k: (i, k))
        b_spec = pl.BlockSpec((bk, bn), lambda i, j, k: (k, j))
    elif mode == "nt":
        a_spec = pl.BlockSpec((bm, bk), lambda i, j, k: (i, k))
        b_spec = pl.BlockSpec((bn, bk), lambda i, j, k: (j, k))
    else:
        a_spec = pl.BlockSpec((bk, bm), lambda i, j, k: (k, i))
        b_spec = pl.BlockSpec((bk, bn), lambda i, j, k: (k, j))
    sem = ("arbitrary",) * 3 if cm.n else ("parallel", "parallel", "arbitrary")
    res = pl.pallas_call(
        body, name=name, grid=(gm, gn, nk),
        in_specs=[a_spec, b_spec] + cm.in_specs(),
        out_specs=[pl.BlockSpec((bm, bn), lambda i, j, k: (i, j))] + cm.out_specs(),
        out_shape=[jax.ShapeDtypeStruct((M, N), out_dtype)] + cm.out_shape(),
        scratch_shapes=([pltpu.VMEM((bm, bn), f32)] if nk > 1 else []) + cm.scratch(),
        compiler_params=_cp(*sem, side_effects=bool(cm.n)),
    )(a, b, *cm.bufs)
    return (res[0], res[1:]) if cm.n else res[0]


def _groups(ts, fn, carry=None, reverse=False, unroll=8, rg=RG):
    n = ts // rg
    if n == 1:
        return fn(0, carry)
    unroll = min(unroll, n)
    span = rg * unroll

    def body(g, c):
        r0 = pl.multiple_of((n // unroll - 1 - g if reverse else g) * span, span)
        for u in (range(unroll - 1, -1, -1) if reverse else range(unroll)):
            c = fn(pl.multiple_of(r0 + u * rg, rg), c)
        return c

    return lax.fori_loop(0, n // unroll, body, carry)


def _rms(x):
    return lax.rsqrt(jnp.mean(x * x, axis=-1, keepdims=True) + EPS)


def _rms_bwd(x, r, dn):
    n = x * r
    return r * (dn - n * jnp.mean(dn * n, axis=-1, keepdims=True))


NRG = 256


def _fold(x):
    return jnp.sum(x.reshape(x.shape[0] // 8, 8, x.shape[1]), axis=0)


def _flush(acc_ref, out_ref, row):
    out_ref[row:row + 1, :] = jnp.sum(acc_ref[...], axis=0, keepdims=True)


def prenorm_fwd(x, w, pos_row, ts, gather):
    S = x.shape[0]
    nt = S // ts
    tg = _TwoLevelGather(gather)
    n = tg.n
    half = AD // 2
    inv = ROPE_THETA ** (-jnp.arange(half, dtype=f32) * 2.0 / AD)
    inv_col = jnp.tile(inv, 2)[:, None]

    def body(x_ref, w_ref, p_ref, inv_ref, *refs):
        u_ref, cos_ref, sin_ref = refs[n:n + 3]
        start, finish = tg.bind(refs[:n], refs[n + 3:2 * n + 3], *refs[2 * n + 3:])
        i = pl.program_id(0)
        pl.when(i == 0)(start)
        wv = w_ref[...]

        def grp(r0, _):
            xv = x_ref[pl.ds(r0, NRG), :]
            u_ref[pl.ds(r0, NRG), :] = (xv * _rms(xv) * wv).astype(bf16)

        _groups(ts, grp, rg=NRG)
        ang = inv_ref[...] * p_ref[...].astype(f32)
        row = lax.broadcasted_iota(jnp.int32, ang.shape, 0)
        cos_ref[...] = jnp.cos(ang)
        sin_ref[...] = jnp.where(row < half, -1.0, 1.0) * jnp.sin(ang)
        pl.when(i == nt - 1)(finish)

    tab = pl.BlockSpec((AD, ts), lambda i: (0, i))
    res = pl.pallas_call(
        body, name="prenorm_fwd", grid=(nt,),
        in_specs=[pl.BlockSpec((ts, D), lambda i: (i, 0)), pl.BlockSpec((1, D), lambda i: (0, 0)),
                  pl.BlockSpec((1, ts), lambda i: (0, i)), pl.BlockSpec((AD, 1), lambda i: (0, 0))] + tg.in_specs(),
        out_specs=[pl.BlockSpec((ts, D), lambda i: (i, 0)), tab, tab] + tg.out_specs(),
        out_shape=[jax.ShapeDtypeStruct((S, D), bf16), jax.ShapeDtypeStruct((AD, S), f32),
                   jax.ShapeDtypeStruct((AD, S), f32)] + tg.out_shape(),
        scratch_shapes=tg.scratch(),
        compiler_params=_cp("arbitrary", side_effects=True),
    )(x, w, pos_row, inv_col, *tg.bufs)
    return res[0], res[1], res[2], res[3:]


def prenorm_bwd(x, w, dus, dx1, ts, comm=None):
    S = x.shape[0]
    nt = S // ts
    nd = len(dus)
    cm = _Comm(comm)

    def body(*refs):
        ins, (gx_ref, gw_ref), (acc_ref,), copies = cm.split(refs, nd + 3, 2)
        x_ref, w_ref = ins[:2]
        du_refs, dx1_ref = ins[2:2 + nd], ins[2 + nd]
        i = pl.program_id(0)
        cm.start(copies, i == 0)
        wv = w_ref[...]

        @pl.when(i == 0)
        def _():
            acc_ref[...] = jnp.zeros_like(acc_ref)
            gw_ref[...] = jnp.zeros_like(gw_ref)

        def grp(r0, _):
            rows = pl.ds(r0, NRG)
            xv = x_ref[rows, :]
            r = _rms(xv)
            du = du_refs[0][rows, :].astype(f32)
            for d_ref in du_refs[1:]:
                du = du + d_ref[rows, :].astype(f32)
            gx_ref[rows, :] = dx1_ref[rows, :] + _rms_bwd(xv, r, du * wv)
            acc_ref[...] += _fold(du * xv * r)

        _groups(ts, grp, rg=NRG)

        @pl.when(i == nt - 1)
        def _():
            _flush(acc_ref, gw_ref, 0)

        cm.wait(copies, i == nt - 1)

    row = pl.BlockSpec((ts, D), lambda i: (i, 0))
    res = pl.pallas_call(
        body, name="prenorm_bwd", grid=(nt,),
        in_specs=[row, pl.BlockSpec((1, D), lambda i: (0, 0))] + [row] * (nd + 1) + cm.in_specs(),
        out_specs=[row, pl.BlockSpec((8, D), lambda i: (0, 0))] + cm.out_specs(),
        out_shape=[jax.ShapeDtypeStruct((S, D), f32), jax.ShapeDtypeStruct((8, D), f32)] + cm.out_shape(),
        scratch_shapes=[pltpu.VMEM((8, D), f32)] + cm.scratch(),
        compiler_params=_cp("arbitrary", side_effects=bool(cm.n)),
    )(x, w, *dus, dx1, *cm.bufs)
    return res[0], res[1], res[2:]


def post_fwd(x, mo, w_post, w_pre2, ts):
    S = x.shape[0]

    def body(x_ref, mo_ref, wp_ref, w2_ref, x1_ref, h_ref):
        wp, w2 = wp_ref[...], w2_ref[...]

        def grp(r0, _):
            rows = pl.ds(r0, NRG)
            mv = mo_ref[rows, :].astype(f32)
            x1 = x_ref[rows, :] + mv * _rms(mv) * wp
            x1_ref[rows, :] = x1
            h_ref[rows, :] = (x1 * _rms(x1) * w2).astype(bf16)

        _groups(ts, grp, rg=NRG)

    row = pl.BlockSpec((ts, D), lambda i: (i, 0))
    par = pl.BlockSpec((1, D), lambda i: (0, 0))
    return pl.pallas_call(
        body, name="post_fwd", grid=(S // ts,),
        in_specs=[row, row, par, par], out_specs=[row, row],
        out_shape=[jax.ShapeDtypeStruct((S, D), f32), jax.ShapeDtypeStruct((S, D), bf16)],
        compiler_params=_cp("parallel"),
    )(x, mo, w_post, w_pre2)


def post_bwd(dout, dh, x1, mo, w_post, w_pre2, ts):
    S = x1.shape[0]
    nt = S // ts

    def body(dout_ref, dh_ref, x1_ref, mo_ref, wp_ref, w2_ref, dx1_ref, dmo_ref, gw_ref, acc2_ref, accp_ref):
        i = pl.program_id(0)
        wp, w2 = wp_ref[...], w2_ref[...]

        @pl.when(i == 0)
        def _():
            acc2_ref[...] = jnp.zeros_like(acc2_ref)
            accp_ref[...] = jnp.zeros_like(accp_ref)
            gw_ref[...] = jnp.zeros_like(gw_ref)

        def grp(r0, _):
            rows = pl.ds(r0, NRG)
            x1 = x1_ref[rows, :]
            r1 = _rms(x1)
            dh = dh_ref[rows, :].astype(f32)
            dx1 = dout_ref[rows, :] + _rms_bwd(x1, r1, dh * w2)
            dx1_ref[rows, :] = dx1
            acc2_ref[...] += _fold(dh * x1 * r1)
            mv = mo_ref[rows, :].astype(f32)
            rm = _rms(mv)
            dmo_ref[rows, :] = _rms_bwd(mv, rm, dx1 * wp).astype(bf16)
            accp_ref[...] += _fold(dx1 * mv * rm)

        _groups(ts, grp, rg=NRG)

        @pl.when(i == nt - 1)
        def _():
            _flush(acc2_ref, gw_ref, 0)
            _flush(accp_ref, gw_ref, 1)

    row = pl.BlockSpec((ts, D), lambda i: (i, 0))
    par = pl.BlockSpec((1, D), lambda i: (0, 0))
    return pl.pallas_call(
        body, name="post_bwd", grid=(nt,),
        in_specs=[row, row, row, row, par, par],
        out_specs=[row, row, pl.BlockSpec((8, D), lambda i: (0, 0))],
        out_shape=[jax.ShapeDtypeStruct((S, D), f32), jax.ShapeDtypeStruct((S, D), bf16),
                   jax.ShapeDtypeStruct((8, D), f32)],
        scratch_shapes=[pltpu.VMEM((8, D), f32), pltpu.VMEM((8, D), f32)],
        compiler_params=_cp("arbitrary"),
    )(dout, dh, x1, mo, w_post, w_pre2)


def loss_head(x1, ff, target, w, ts):
    S = x1.shape[0]
    nt = S // ts

    def body(x1_ref, ff_ref, t_ref, w_ref, loss_ref, dout_ref, dff_ref, gw_ref, accw_ref, accl_ref):
        i = pl.program_id(0)
        wv = w_ref[...]

        @pl.when(i == 0)
        def _():
            accw_ref[...] = jnp.zeros_like(accw_ref)
            accl_ref[...] = jnp.zeros_like(accl_ref)
            gw_ref[...] = jnp.zeros_like(gw_ref)

        def grp(r0, _):
            rows = pl.ds(r0, NRG)
            fv = ff_ref[rows, :].astype(f32)
            r = _rms(fv)
            n = fv * r
            e = x1_ref[rows, :] + n * wv - t_ref[rows, :]
            dout = e * (1.0 / D)
            dout_ref[rows, :] = dout
            dff_ref[rows, :] = _rms_bwd(fv, r, dout * wv).astype(bf16)
            accw_ref[...] += _fold(dout * n)
            accl_ref[...] += _fold(e * e)

        _groups(ts, grp, rg=NRG)

        @pl.when(i == nt - 1)
        def _():
            _flush(accw_ref, gw_ref, 0)
            tot = jnp.sum(jnp.sum(accl_ref[...], axis=1, keepdims=True), axis=0, keepdims=True) * (0.5 / D)
            loss_ref[...] = jnp.broadcast_to(tot, loss_ref.shape)

    row = pl.BlockSpec((ts, D), lambda i: (i, 0))
    return pl.pallas_call(
        body, name="loss_head", grid=(nt,),
        in_specs=[row, row, row, pl.BlockSpec((1, D), lambda i: (0, 0))],
        out_specs=[pl.BlockSpec((8, LANES), lambda i: (0, 0)), row, row, pl.BlockSpec((8, D), lambda i: (0, 0))],
        out_shape=[jax.ShapeDtypeStruct((8, LANES), f32), jax.ShapeDtypeStruct((S, D), f32),
                   jax.ShapeDtypeStruct((S, D), bf16), jax.ShapeDtypeStruct((8, D), f32)],
        scratch_shapes=[pltpu.VMEM((8, D), f32), pltpu.VMEM((8, D), f32)],
        compiler_params=_cp("arbitrary"),
    )(x1, ff, target, w)


def _taps(w_ref, cs, K):
    return [jnp.broadcast_to(w_ref[k:k + 1, cs], (8, CW)) for k in range(K)]


def _rolls_down(v, K):
    return tuple(pltpu.roll(v, s, 0) for s in range(1, K))


def _rolls_up(v, K):
    return tuple(pltpu.roll(v, 8 - s, 0) for s in range(1, K))


def _shifted_down(prolls, a, b, K, sub):
    arolls, brolls = _rolls_down(a, K), _rolls_down(b, K)
    out = [(a, b)]
    for s in range(1, K):
        m = sub < s
        out.append((jnp.where(m, prolls[s - 1], arolls[s - 1]), jnp.where(m, arolls[s - 1], brolls[s - 1])))
    return out, brolls


def _shifted_up(a, b, nrolls, K, sub):
    arolls, brolls = _rolls_up(a, K), _rolls_up(b, K)
    out = [(a, b)]
    for s in range(1, K):
        m = sub < 8 - s
        out.append((jnp.where(m, arolls[s - 1], brolls[s - 1]), jnp.where(m, brolls[s - 1], nrolls[s - 1])))
    return out, arolls


def _conv_group(prolls, a, b, taps, bias, K, sub):
    shifted, brolls = _shifted_down(prolls, a, b, K, sub)
    ya, yb = bias, bias
    for k in range(K):
        xa, xb = shifted[K - 1 - k]
        ya = ya + taps[k] * xa
        yb = yb + taps[k] * xb
    return ya, yb, brolls


def _prev8_map(ts, cb):
    return lambda i, j: (jnp.maximum(i * (ts // 8) - 1, 0), cb + j)


def ssdconv_fwd(proj, w8, b, ts):
    S = proj.shape[0]
    bw = 1024
    cb = C_XBC // bw

    def body(cur_ref, prev_ref, w_ref, b_ref, o_ref, c_ref):
        first = pl.program_id(0) == 0
        sub = lax.broadcasted_iota(jnp.int32, (8, CW), 0)
        for c0 in range(0, bw, CW):
            cs = slice(c0, c0 + CW)
            taps = _taps(w_ref, cs, SSD_K)
            bias = jnp.broadcast_to(b_ref[:, cs], (8, CW))

            def grp(r0, prolls, cs=cs, taps=taps, bias=bias):
                rows = pl.ds(r0, RG)
                xv = cur_ref[rows, cs].astype(f32)
                ya, yb, brolls = _conv_group(prolls, xv[0:8], xv[8:16], taps, bias, SSD_K, sub)
                y = jnp.concatenate([ya, yb], axis=0)
                c_ref[rows, cs] = y.astype(bf16)
                o_ref[rows, cs] = (y * _sigmoid_fast(y)).astype(bf16)
                return brolls

            _groups(ts, grp, _rolls_down(jnp.where(first, 0.0, prev_ref[:, cs].astype(f32)), SSD_K))

    o = jax.ShapeDtypeStruct((S, CONVD), bf16)
    blk = pl.BlockSpec((ts, bw), lambda i, j: (i, j))
    return pl.pallas_call(
        body, name="ssdconv_fwd", grid=(S // ts, CONVD // bw),
        in_specs=[pl.BlockSpec((ts, bw), lambda i, j: (i, cb + j)),
                  pl.BlockSpec((8, bw), _prev8_map(ts, cb)),
                  pl.BlockSpec((8, bw), lambda i, j: (0, j)),
                  pl.BlockSpec((1, bw), lambda i, j: (0, j))],
        out_specs=[blk, blk], out_shape=[o, o],
        compiler_params=_cp("parallel", "parallel"),
    )(proj, proj, w8, b)


def _gelu_tanh(x):
    c = 0.7978845608028654
    t = jnp.tanh(c * (x + 0.044715 * x * x * x))
    return 0.5 * x * (1.0 + t), t


def ffnact_fwd(up, w8, b, ts):
    S = up.shape[0]

    def body(g_ref, gp_ref, v_ref, vp_ref, wg_ref, wv_ref, bg_ref, bv_ref, o_ref, gc_ref, vc_ref):
        first = pl.program_id(0) == 0
        sub = lax.broadcasted_iota(jnp.int32, (8, CW), 0)
        for c0 in range(0, FF, CW):
            cs = slice(c0, c0 + CW)
            tg, tv = _taps(wg_ref, cs, FFN_K), _taps(wv_ref, cs, FFN_K)
            bg = jnp.broadcast_to(bg_ref[:, cs], (8, CW))
            bv = jnp.broadcast_to(bv_ref[:, cs], (8, CW))

            def grp(r0, carry, cs=cs, tg=tg, tv=tv, bg=bg, bv=bv):
                pg, pv = carry
                rows = pl.ds(r0, RG)
                gx = g_ref[rows, cs].astype(f32)
                vx = v_ref[rows, cs].astype(f32)
                ga, gb, pg = _conv_group(pg, gx[0:8], gx[8:16], tg, bg, FFN_K, sub)
                va, vb, pv = _conv_group(pv, vx[0:8], vx[8:16], tv, bv, FFN_K, sub)
                g = jnp.concatenate([ga, gb], axis=0)
                v = jnp.concatenate([va, vb], axis=0)
                gc_ref[rows, cs] = g.astype(bf16)
                vc_ref[rows, cs] = v.astype(bf16)
                o_ref[rows, cs] = (_gelu_tanh(g)[0] * v).astype(bf16)
                return pg, pv

            _groups(ts, grp, (_rolls_down(jnp.where(first, 0.0, gp_ref[:, cs].astype(f32)), FFN_K),
                              _rolls_down(jnp.where(first, 0.0, vp_ref[:, cs].astype(f32)), FFN_K)))

    o = jax.ShapeDtypeStruct((S, FF), bf16)
    blk = pl.BlockSpec((ts, FF), lambda i: (i, 0))
    prev = lambda cb: pl.BlockSpec((8, FF), lambda i: (jnp.maximum(i * (ts // 8) - 1, 0), cb))
    return pl.pallas_call(
        body, name="ffnact_fwd", grid=(S // ts,),
        in_specs=[blk, prev(0), pl.BlockSpec((ts, FF), lambda i: (i, 1)), prev(1),
                  pl.BlockSpec((8, FF), lambda i: (0, 0)), pl.BlockSpec((8, FF), lambda i: (0, 1)),
                  pl.BlockSpec((1, FF), lambda i: (0, 0)), pl.BlockSpec((1, FF), lambda i: (0, 1))],
        out_specs=[blk, blk, blk], out_shape=[o, o, o],
        compiler_params=_cp("parallel"),
    )(up, up, up, up, w8, w8, b, b)


def ffnact_bwd(dact, gc, vc, ts):
    S = dact.shape[0]

    def body(d_ref, g_ref, v_ref, dg_ref, dv_ref):
        c = 0.7978845608028654
        for c0 in range(0, FF, CW):
            cs = slice(c0, c0 + CW)

            def grp(r0, _, cs=cs):
                rows = pl.ds(r0, RG)
                d = d_ref[rows, cs].astype(f32)
                g = g_ref[rows, cs].astype(f32)
                ge, t = _gelu_tanh(g)
                dgelu = 0.5 * (1.0 + t) + 0.5 * g * (1.0 - t * t) * c * (1.0 + 3.0 * 0.044715 * g * g)
                dg_ref[rows, cs] = (d * v_ref[rows, cs].astype(f32) * dgelu).astype(bf16)
                dv_ref[rows, cs] = (d * ge).astype(bf16)

            _groups(ts, grp)

    o = jax.ShapeDtypeStruct((S, FF), bf16)
    blk = pl.BlockSpec((ts, FF), lambda i: (i, 0))
    return pl.pallas_call(
        body, name="ffnact_bwd", grid=(S // ts,),
        in_specs=[blk, blk, blk], out_specs=[blk, blk], out_shape=[o, o],
        compiler_params=_cp("parallel"),
    )(dact, gc, vc)


def dwconv_bwd(dy, x, xcb, w8, wcb, K, bw, ts, name, act_c=None, into=None, ocb=0, out_cols=None):
    S, C = dy.shape
    nr = S // ts
    out_cols = out_cols or C
    n_act = 0 if act_c is None else 2

    def body(*refs):
        dy_ref, dyn_ref = refs[0:2]
        c_ref, cn_ref = (refs[2:4] if n_act else (None, None))
        x_ref, xp_ref, w_ref = refs[2 + n_act:5 + n_act]
        dx_ref, dw_ref, sd_ref = refs[-3:]
        i = pl.program_id(1)
        first, last = i == 0, i == nr - 1
        sub = lax.broadcasted_iota(jnp.int32, (8, CW), 0)

        def grad_y(d, c):
            if c is None:
                return d.astype(f32)
            cv = c.astype(f32)
            s = _sigmoid_fast(cv)
            return d.astype(f32) * s * (1.0 + cv * (1.0 - s))

        @pl.when(first)
        def _():
            dw_ref[...] = jnp.zeros_like(dw_ref)

        for c0 in range(0, bw, CW):
            cs = slice(c0, c0 + CW)
            taps = _taps(w_ref, cs, K)
            zero = jnp.zeros((8, CW), f32)

            def fwd(r0, carry, cs=cs):
                prolls, accs, accb = carry
                rows = pl.ds(r0, RG)
                g = grad_y(dy_ref[rows, cs], c_ref[rows, cs] if n_act else None)
                sd_ref[rows, cs] = g
                xv = x_ref[rows, cs].astype(f32)
                ga, gb = g[0:8], g[8:16]
                shifted, brolls = _shifted_down(prolls, xv[0:8], xv[8:16], K, sub)
                new = []
                for k in range(K):
                    xa, xb = shifted[K - 1 - k]
                    new.append(accs[k] + ga * xa + gb * xb)
                return brolls, tuple(new), accb + ga + gb

            before = jnp.where(first, 0.0, xp_ref[:, cs].astype(f32))
            _, accs, accb = _groups(ts, fwd, (_rolls_down(before, K), (zero,) * K, zero))
            for k in range(K):
                dw_ref[k:k + 1, cs] += jnp.sum(accs[k], axis=0, keepdims=True)
            dw_ref[7:8, cs] += jnp.sum(accb, axis=0, keepdims=True)

            def bwd(r0, nrolls, cs=cs, taps=taps):
                rows = pl.ds(r0, RG)
                g = sd_ref[rows, cs]
                shifted, arolls = _shifted_up(g[0:8], g[8:16], nrolls, K, sub)
                da, db = zero, zero
                for k in range(K):
                    ua, ub = shifted[K - 1 - k]
                    da = da + taps[k] * ua
                    db = db + taps[k] * ub
                dx_ref[rows, cs] = jnp.concatenate([da, db], axis=0).astype(bf16)
                return arolls

            halo = grad_y(dyn_ref[:, cs], cn_ref[:, cs] if n_act else None)
            _groups(ts, bwd, _rolls_up(jnp.where(last, 0.0, halo), K), reverse=True)

    nxt = lambda j, i: (jnp.minimum((i + 1) * (ts // 8), S // 8 - 1), j)
    tile = pl.BlockSpec((ts, bw), lambda j, i: (i, j))
    acts = [] if act_c is None else [act_c, act_c]
    extra = [] if into is None else [into]
    n_in = 5 + n_act
    return pl.pallas_call(
        body, name=name, grid=(C // bw, nr),
        in_specs=[tile, pl.BlockSpec((8, bw), nxt)] + ([tile, pl.BlockSpec((8, bw), nxt)] if n_act else []) + [
            pl.BlockSpec((ts, bw), lambda j, i: (i, xcb + j)),
            pl.BlockSpec((8, bw), lambda j, i: (jnp.maximum(i * (ts // 8) - 1, 0), xcb + j)),
            pl.BlockSpec((8, bw), lambda j, i: (0, wcb + j))] + [pl.BlockSpec(memory_space=pl.ANY)] * len(extra),
        out_specs=[pl.BlockSpec((ts, bw), lambda j, i: (i, ocb + j)), pl.BlockSpec((8, bw), lambda j, i: (0, j))],
        out_shape=[jax.ShapeDtypeStruct((S, out_cols), bf16), jax.ShapeDtypeStruct((8, C), f32)],
        scratch_shapes=[pltpu.VMEM((ts, bw), f32)],
        input_output_aliases={n_in: 0} if extra else {},
        compiler_params=_cp("parallel", "arbitrary"),
    )(dy, dy, *acts, x, x, w8, *extra)


def gnorm_fwd(y, proj, w, ts):
    S = y.shape[0]

    def body(y_ref, z_ref, w_ref, o_ref):
        for k in range(NG):
            sl = slice(k * GW, (k + 1) * GW)
            wv = w_ref[:, sl]

            def grp(r0, _, sl=sl, wv=wv):
                rows = pl.ds(r0, NRG)
                z = z_ref[rows, sl].astype(f32)
                g = y_ref[rows, sl].astype(f32) * z * _sigmoid_fast(z)
                o_ref[rows, sl] = (g * _rms(g) * wv).astype(bf16)

            _groups(ts, grp, rg=NRG)

    row = pl.BlockSpec((ts, DI), lambda i: (i, 0))
    return pl.pallas_call(
        body, name="gnorm_fwd", grid=(S // ts,),
        in_specs=[row, row, pl.BlockSpec((1, DI), lambda i: (0, 0))],
        out_specs=row, out_shape=jax.ShapeDtypeStruct((S, DI), bf16),
        compiler_params=_cp("parallel"),
    )(y, proj, w)


def gnorm_bwd(dyn, y, proj, w, dproj, ts):
    S = y.shape[0]
    nt = S // ts

    def body(d_ref, y_ref, z_ref, w_ref, _, dy_ref, dz_ref, gw_ref, acc_ref):
        i = pl.program_id(0)

        @pl.when(i == 0)
        def _():
            acc_ref[...] = jnp.zeros_like(acc_ref)
            gw_ref[...] = jnp.zeros_like(gw_ref)

        for k in range(NG):
            sl = slice(k * GW, (k + 1) * GW)
            wv = w_ref[:, sl]

            def grp(r0, _, sl=sl, wv=wv):
                rows = pl.ds(r0, NRG)
                z = z_ref[rows, sl].astype(f32)
                yv = y_ref[rows, sl].astype(f32)
                s = _sigmoid_fast(z)
                sz = z * s
                g = yv * sz
                r = _rms(g)
                d = d_ref[rows, sl].astype(f32)
                acc_ref[:, sl] += _fold(d * g * r)
                dg = _rms_bwd(g, r, d * wv)
                dy_ref[rows, sl] = (dg * sz).astype(bf16)
                dz_ref[rows, sl] = (dg * yv * s * (1.0 + z * (1.0 - s))).astype(bf16)

            _groups(ts, grp, rg=NRG)

        @pl.when(i == nt - 1)
        def _():
            _flush(acc_ref, gw_ref, 0)

    row = pl.BlockSpec((ts, DI), lambda i: (i, 0))
    return pl.pallas_call(
        body, name="gnorm_bwd", grid=(nt,),
        in_specs=[row, row, row, pl.BlockSpec((1, DI), lambda i: (0, 0)), pl.BlockSpec(memory_space=pl.ANY)],
        out_specs=[row, row, pl.BlockSpec((8, DI), lambda i: (0, 0))],
        out_shape=[jax.ShapeDtypeStruct((S, DI), bf16), jax.ShapeDtypeStruct(dproj.shape, bf16),
                   jax.ShapeDtypeStruct((8, DI), f32)],
        scratch_shapes=[pltpu.VMEM((8, DI), f32)],
        input_output_aliases={4: 1},
        compiler_params=_cp("arbitrary"),
    )(dyn, y, proj, w, dproj)


def merge_fwd(proj, ys, ya, ts):
    S = ys.shape[0]

    def body(gs_ref, ga_ref, ys_ref, ya_ref, o_ref):
        for c0 in range(0, D, CW):
            cs = slice(c0, c0 + CW)

            def grp(r0, _, cs=cs):
                rows = pl.ds(r0, NRG)
                o_ref[rows, cs] = (_sigmoid_fast(gs_ref[rows, cs].astype(f32)) * ys_ref[rows, cs].astype(f32)
                                   + _sigmoid_fast(ga_ref[rows, cs].astype(f32)) * ya_ref[rows, cs].astype(f32)
                                   ).astype(bf16)

            _groups(ts, grp, rg=NRG)

    row = pl.BlockSpec((ts, D), lambda i: (i, 0))
    return pl.pallas_call(
        body, name="merge_fwd", grid=(S // ts,),
        in_specs=[pl.BlockSpec((ts, D), lambda i: (i, C_GS // D)), pl.BlockSpec((ts, D), lambda i: (i, C_GA // D)), row, row],
        out_specs=row, out_shape=jax.ShapeDtypeStruct((S, D), bf16),
        compiler_params=_cp("parallel"),
    )(proj, proj, ys, ya)


def merge_bwd(dm, proj, ys, ya, ts):
    S = ys.shape[0]

    def body(d_ref, gs_ref, ga_ref, ys_ref, ya_ref, dys_ref, dya_ref, dg_ref):
        for c0 in range(0, D, CW):
            cs = slice(c0, c0 + CW)

            def grp(r0, _, c0=c0, cs=cs):
                rows = pl.ds(r0, NRG)
                d = d_ref[rows, cs].astype(f32)
                ss = _sigmoid_fast(gs_ref[rows, cs].astype(f32))
                sa = _sigmoid_fast(ga_ref[rows, cs].astype(f32))
                dys_ref[rows, cs] = (d * ss).astype(bf16)
                dya_ref[rows, cs] = (d * sa).astype(bf16)
                dg_ref[rows, cs] = (d * ys_ref[rows, cs].astype(f32) * ss * (1.0 - ss)).astype(bf16)
                dg_ref[rows, D + c0:D + c0 + CW] = (d * ya_ref[rows, cs].astype(f32) * sa * (1.0 - sa)).astype(bf16)

            _groups(ts, grp, rg=NRG)

    row = pl.BlockSpec((ts, D), lambda i: (i, 0))
    o = jax.ShapeDtypeStruct((S, D), bf16)
    return pl.pallas_call(
        body, name="merge_bwd", grid=(S // ts,),
        in_specs=[row, pl.BlockSpec((ts, D), lambda i: (i, C_GS // D)), pl.BlockSpec((ts, D), lambda i: (i, C_GA // D)), row, row],
        out_specs=[row, row, pl.BlockSpec((ts, 2 * D), lambda i: (i, C_GS // (2 * D)))],
        out_shape=[o, o, jax.ShapeDtypeStruct((S, PM), bf16)],
        compiler_params=_cp("parallel"),
    )(dm, proj, proj, ys, ya)


def _ssd_consts():
    h = lax.broadcasted_iota(jnp.int32, (LANES, DI), 0)
    c = lax.broadcasted_iota(jnp.int32, (LANES, DI), 1)
    expand = (c // HD == h).astype(bf16)
    r = lax.broadcasted_iota(jnp.int32, (CH, CH), 0)
    cc = lax.broadcasted_iota(jnp.int32, (CH, CH), 1)
    tril = (cc <= r).astype(bf16)
    triu = (cc >= r).astype(bf16)
    return expand, expand.T, tril, triu


def _ssd_common(xbc_ref, dtr_ref, bias_ref, alog_ref, tril_ref, expand_ref=None, saved=None):
    dtr = dtr_ref[...] + bias_ref[...]
    dt = jnp.maximum(dtr, 0.0) + jnp.log1p(jnp.exp(-jnp.abs(dtr)))
    a = -jnp.exp(alog_ref[...])
    acs = _dot3_left(tril_ref[...], dt * a)
    if saved is None:
        acsx = _dot3_right(acs, expand_ref[...])
        dtx = _dot3_right(dt, expand_ref[...])
    else:
        acsx, dtx = saved[0][...], saved[1][...]
    x = xbc_ref[:, 0:DI].astype(f32)
    xdt = x * dtx
    e = jnp.exp(acsx)
    dsx = jnp.exp(acsx[CH - 1:CH, :] - acsx)
    return dtr, dt, a, acs, acsx, dtx, x, xdt, e, dsx


def _head_halves():
    first = lax.broadcasted_iota(jnp.int32, (CH, LANES), 1) < HD
    return first, jnp.logical_not(first)


def _ssd_lmat(acs, acs_t, hh, causal):
    seg = acs[:, hh:hh + 1] - acs_t[hh:hh + 1, :]
    return jnp.where(causal, jnp.exp(jnp.minimum(seg, 0.0)), 0.0)


def ssd_fwd(xbc, dtr, bias, alog, dx_row, comm=None):
    S = xbc.shape[0]
    nc = S // CH
    expand, _, tril, _ = _ssd_consts()
    cm = _Comm(comm)

    def body(*refs):
        ins, (y_ref, hp_ref, ax_ref, dtx_ref), (h_ref, yd_ref), copies = cm.split(refs, 7, 4)
        xbc_ref, dtr_ref, bias_ref, alog_ref, dxr_ref, expand_ref, tril_ref = ins
        c = pl.program_id(0)
        cm.start(copies, c == 0)

        @pl.when(c == 0)
        def _():
            h_ref[...] = jnp.zeros_like(h_ref)

        _, _, _, acs, acsx, dtx, x, xdt, e, dsx = _ssd_common(xbc_ref, dtr_ref, bias_ref, alog_ref, tril_ref,
                                                              expand_ref=expand_ref)
        ax_ref[...] = acsx
        dtx_ref[...] = dtx
        acs_t = acs.T
        xb = xdt.astype(bf16)
        xd = (xdt * dsx).astype(bf16)
        causal = tril_ref[...] > 0
        halves = _head_halves()
        for g in range(NG):
            gs = slice(g * GW, (g + 1) * GW)
            bg = xbc_ref[:, DI + g * NS:DI + (g + 1) * NS]
            cg = xbc_ref[:, DI + NG * NS + g * NS:DI + NG * NS + (g + 1) * NS]
            cb = _dot(cg, bg, "nt")
            hp = h_ref[g]
            hpb = hp.astype(bf16)
            hp_ref[0, g] = hpb
            yd_ref[:, gs] = _dot(cg, hpb) * e[:, gs]
            h_ref[g] = hp * e[CH - 1:CH, gs] + _dot(bg, xd[:, gs], "tn")
            for k in range(NH // NG // 2):
                hh = g * (NH // NG) + 2 * k
                ps = slice(hh * HD, (hh + 2) * HD)
                xp = xb[:, ps]
                acc = None
                for o in range(2):
                    m = (cb * _ssd_lmat(acs, acs_t, hh + o, causal)).astype(bf16)
                    part = _dot(m, jnp.where(halves[o], xp, jnp.zeros_like(xp)))
                    acc = part if acc is None else acc + part
                yd_ref[:, ps] += acc
        y_ref[...] = (yd_ref[...] + dxr_ref[...] * x).astype(bf16)
        cm.wait(copies, c == nc - 1)

    par = lambda shape: pl.BlockSpec(shape, lambda c: (0,) * len(shape))
    res = pl.pallas_call(
        body, name="ssd_fwd", grid=(nc,),
        in_specs=[pl.BlockSpec((CH, CONVD), lambda c: (c, 0)), pl.BlockSpec((CH, LANES), lambda c: (c, 0)),
                  par((1, LANES)), par((1, LANES)), par((1, DI)), par((LANES, DI)), par((CH, CH))] + cm.in_specs(),
        out_specs=[pl.BlockSpec((CH, DI), lambda c: (c, 0)), pl.BlockSpec((1, NG, NS, GW), lambda c: (c, 0, 0, 0)),
                   pl.BlockSpec((CH, DI), lambda c: (c, 0)), pl.BlockSpec((CH, DI), lambda c: (c, 0))] + cm.out_specs(),
        out_shape=[jax.ShapeDtypeStruct((S, DI), bf16), jax.ShapeDtypeStruct((nc, NG, NS, GW), bf16),
                   jax.ShapeDtypeStruct((S, DI), f32), jax.ShapeDtypeStruct((S, DI), f32)] + cm.out_shape(),
        scratch_shapes=[pltpu.VMEM((NG, NS, GW), f32), pltpu.VMEM((CH, DI), f32)] + cm.scratch(),
        compiler_params=_cp("arbitrary", side_effects=bool(cm.n)),
    )(xbc, dtr, bias, alog, dx_row, expand, tril, *cm.bufs)
    return res[0], res[1], (res[2], res[3]), res[4:]


def ssd_bwd(xbc, dtr, dy, hprev, saved, bias, alog, dx_row, comm=None):
    S = xbc.shape[0]
    nc = S // CH
    _, expand_t, tril, triu = _ssd_consts()
    cm = _Comm(comm)

    def body(*refs):
        ins, outs, scr, copies = cm.split(refs, 12, 3)
        xbc_ref, dtr_ref, dy_ref, hp_ref, ax_ref, dtx_ref, bias_ref, alog_ref, dxr_ref, expt_ref, tril_ref, triu_ref = ins
        dxbc_ref, ddtr_ref, acc_ref = outs
        dh_ref, dxs_ref, t_ref, accb_ref, acca_ref, accd_ref, cc_ref, rr_ref = scr
        c = pl.program_id(0)
        cm.start(copies, c == 0)

        @pl.when(c == 0)
        def _():
            cc_ref[...] = jnp.zeros_like(cc_ref)
            rr_ref[...] = jnp.zeros_like(rr_ref)

        @pl.when(c == 0)
        def _():
            dh_ref[...] = jnp.zeros_like(dh_ref)
            accb_ref[...] = jnp.zeros_like(accb_ref)
            acca_ref[...] = jnp.zeros_like(acca_ref)
            accd_ref[...] = jnp.zeros_like(accd_ref)

        dtr, dt, a, acs, _, dtx, x, xdt, e, dsx = _ssd_common(xbc_ref, dtr_ref, bias_ref, alog_ref, tril_ref,
                                                              saved=(ax_ref, dtx_ref))
        acs_t = acs.T
        xb = xdt.astype(bf16)
        xdf = xdt * dsx
        xd = xdf.astype(bf16)
        dyv = dy_ref[...].astype(f32)
        dyb = dy_ref[...]
        dye = (dyv * e).astype(bf16)
        causal = tril_ref[...] > 0
        halves = _head_halves()
        last_row = lax.broadcasted_iota(jnp.int32, (CH, 1), 0) == CH - 1
        for g in range(NG):
            gs = slice(g * GW, (g + 1) * GW)
            bsl = slice(DI + g * NS, DI + (g + 1) * NS)
            csl = slice(DI + NG * NS + g * NS, DI + NG * NS + (g + 1) * NS)
            bg = xbc_ref[:, bsl]
            cg = xbc_ref[:, csl]
            cb = _dot(cg, bg, "nt")
            hpb = hp_ref[0, g]
            dhn = dh_ref[g]
            dhnb = dhn.astype(bf16)
            yoff = _dot(cg, hpb) * e[:, gs]
            dxd = _dot(bg, dhnb)
            t2 = dxd * xdf[:, gs]
            t3 = jnp.sum(dhn * hpb.astype(f32), axis=0, keepdims=True) * e[CH - 1:CH, gs]
            t_ref[:, gs] = dyv[:, gs] * yoff - t2 + jnp.where(last_row, jnp.sum(t2, axis=0, keepdims=True) + t3, 0.0)
            dxs_ref[:, gs] = dxd * dsx[:, gs]
            dcg = _dot(dye[:, gs], hpb, "nt")
            dbg = _dot(xd[:, gs], dhnb, "nt")
            dh_ref[g] = dhn * e[CH - 1:CH, gs] + _dot(cg, dye[:, gs], "tn")
            dcb = jnp.zeros((CH, CH), f32)
            for k in range(NH // NG // 2):
                hh0 = g * (NH // NG) + 2 * k
                ps = slice(hh0 * HD, (hh0 + 2) * HD)
                xp = xb[:, ps]
                dyp = dyb[:, ps]
                acc = None
                for o in range(2):
                    hh = hh0 + o
                    dyh = jnp.where(halves[o], dyp, jnp.zeros_like(dyp))
                    lm = _ssd_lmat(acs, acs_t, hh, causal)
                    m = cb * lm
                    dm = _dot(dyh, xp, "nt")
                    gm = dm * m
                    cc_ref[:, hh:hh + 1] = jnp.sum(gm, axis=1, keepdims=True)
                    rr_ref[hh:hh + 1, :] = jnp.sum(gm, axis=0, keepdims=True)
                    dcb = dcb + dm * lm
                    part = _dot(m.astype(bf16), dyh, "tn")
                    acc = part if acc is None else acc + part
                dxs_ref[:, ps] += acc
            dcbb = dcb.astype(bf16)
            dxbc_ref[:, csl] = (dcg + _dot(dcbb, bg)).astype(bf16)
            dxbc_ref[:, bsl] = (dbg + _dot(dcbb, cg, "tn")).astype(bf16)
        dxf = dxs_ref[...]
        dxbc_ref[:, 0:DI] = (dxf * dtx + dxr_ref[...] * dyv).astype(bf16)
        expt = expt_ref[...]
        dacs = cc_ref[...] - rr_ref[...].T + _dot2_right(t_ref[...], expt)
        dadt = _dot3_left(triu_ref[...], dacs)
        ddt = _dot2_right(dxf * x, expt) + dadt * a
        ddtr = ddt * _sigmoid(dtr)
        ddtr_ref[...] = ddtr
        accb_ref[...] += ddtr
        acca_ref[...] += dadt * dt
        accd_ref[...] += _dot2_right(dyv * x, expt)

        @pl.when(c == nc - 1)
        def _():
            acc_ref[...] = jnp.zeros_like(acc_ref)
            acc_ref[0:1, :] = jnp.sum(accb_ref[...], axis=0, keepdims=True)
            acc_ref[1:2, :] = jnp.sum(acca_ref[...], axis=0, keepdims=True) * a
            acc_ref[2:3, :] = jnp.sum(accd_ref[...], axis=0, keepdims=True)

        cm.wait(copies, c == nc - 1)

    par = lambda shape: pl.BlockSpec(shape, lambda c: (0,) * len(shape))
    rev = lambda c: (nc - 1 - c, 0)
    res = pl.pallas_call(
        body, name="ssd_bwd", grid=(nc,),
        in_specs=[pl.BlockSpec((CH, CONVD), rev), pl.BlockSpec((CH, LANES), rev), pl.BlockSpec((CH, DI), rev),
                  pl.BlockSpec((1, NG, NS, GW), lambda c: (nc - 1 - c, 0, 0, 0)),
                  pl.BlockSpec((CH, DI), rev), pl.BlockSpec((CH, DI), rev),
                  par((1, LANES)), par((1, LANES)), par((1, DI)), par((DI, LANES)),
                  par((CH, CH)), par((CH, CH))] + cm.in_specs(),
        out_specs=[pl.BlockSpec((CH, CONVD), rev), pl.BlockSpec((CH, LANES), rev), par((8, LANES))] + cm.out_specs(),
        out_shape=[jax.ShapeDtypeStruct((S, CONVD), bf16), jax.ShapeDtypeStruct((S, LANES), f32),
                   jax.ShapeDtypeStruct((8, LANES), f32)] + cm.out_shape(),
        scratch_shapes=[pltpu.VMEM((NG, NS, GW), f32), pltpu.VMEM((CH, DI), f32), pltpu.VMEM((CH, DI), f32),
                        pltpu.VMEM((CH, LANES), f32), pltpu.VMEM((CH, LANES), f32), pltpu.VMEM((CH, LANES), f32),
                        pltpu.VMEM((CH, LANES), f32), pltpu.VMEM((LANES, CH), f32)] + cm.scratch(),
        compiler_params=_cp("arbitrary", side_effects=bool(cm.n)),
    )(xbc, dtr, dy, hprev, *saved, bias, alog, dx_row, expand_t, tril, triu, *cm.bufs)
    return res[0], res[1], res[2], res[3:]


def _partner(t):
    half = AD // 2
    return jnp.concatenate([t[h * AD + o:h * AD + o + half] for h in range(t.shape[0] // AD) for o in (half, 0)], axis=0)


def _rope(t, cos, sin):
    reps = t.shape[0] // AD
    return t * jnp.tile(cos, (reps, 1)) + _partner(t) * jnp.tile(sin, (reps, 1))


def _rope_t(d, cos, sin):
    reps = d.shape[0] // AD
    return d * jnp.tile(cos, (reps, 1)) - _partner(d) * jnp.tile(sin, (reps, 1))


def _lanes_of_group(t, g):
    return jnp.concatenate([t[(g * REP + r) * AD:(g * REP + r + 1) * AD] for r in range(REP)], axis=1)


def _attn_probs(qg, k2, sink_ref, g, not_first):
    n = qg.shape[1]
    s = lax.broadcasted_iota(jnp.int32, (2 * WIN, n), 0)
    t = lax.broadcasted_iota(jnp.int32, (2 * WIN, n), 1) % WIN
    valid = jnp.logical_or(jnp.logical_and(jnp.logical_and(s < WIN, s > t), not_first),
                           jnp.logical_and(s >= WIN, s - WIN <= t))
    sink = jnp.concatenate([jnp.broadcast_to(sink_ref[0:1, g * REP + r:g * REP + r + 1], (1, WIN)) for r in range(REP)],
                           axis=1)
    sc = jnp.where(valid, _dot(k2, qg, "tn"), -1e30)
    m = jnp.maximum(jnp.max(sc, axis=0, keepdims=True), sink)
    p = jnp.exp(sc - m)
    ps = jnp.exp(sink - m)
    inv = 1.0 / (jnp.sum(p, axis=0, keepdims=True) + ps)
    return p * inv, ps * inv


def attn_fwd(qt, kvt, cos, sin, sinks):
    S = qt.shape[1]
    nb = S // WIN
    cur = lambda i: (0, i)
    prev = lambda i: (0, jnp.maximum(2 * i - 1, 0))

    def body(q_ref, kv_ref, kvp_ref, cos_ref, sin_ref, cosp_ref, sinp_ref, sink_ref, o_ref):
        i = pl.program_id(0)
        q = (_rope(q_ref[...].astype(f32), cos_ref[...], sin_ref[...]) * (AD ** -0.5)).astype(bf16)
        kc = _rope(kv_ref[0:KVW, :].astype(f32), cos_ref[...], sin_ref[...]).astype(bf16)
        kp = _rope(kvp_ref[0:KVW, :].astype(f32), cosp_ref[...], sinp_ref[...]).astype(bf16)
        k3 = jnp.concatenate([kp, kc], axis=1)
        for g in range(KVH):
            ks = slice(g * AD, (g + 1) * AD)
            vs = slice(KVW + g * AD, KVW + (g + 1) * AD)
            v3 = jnp.concatenate([kvp_ref[vs, :], kv_ref[vs, :]], axis=1)
            for b in range(2):
                win = slice(b * WIN, (b + 2) * WIN)
                qg = _lanes_of_group(q[:, b * WIN:(b + 1) * WIN], g)
                p, _ = _attn_probs(qg, k3[ks, win], sink_ref, g, jnp.logical_or(i > 0, b > 0))
                o = _dot(v3[:, win], p.astype(bf16))
                for r in range(REP):
                    h = g * REP + r
                    o_ref[h * AD:(h + 1) * AD, b * WIN:(b + 1) * WIN] = o[:, r * WIN:(r + 1) * WIN].astype(bf16)

    tab = pl.BlockSpec((AD, 2 * WIN), cur)
    tabp = pl.BlockSpec((AD, WIN), prev)
    return pl.pallas_call(
        body, name="attn_fwd", grid=(nb // 2,),
        in_specs=[pl.BlockSpec((D, 2 * WIN), cur), pl.BlockSpec((2 * KVW, 2 * WIN), cur),
                  pl.BlockSpec((2 * KVW, WIN), prev), tab, tab, tabp, tabp, pl.BlockSpec((1, LANES), lambda i: (0, 0))],
        out_specs=pl.BlockSpec((D, 2 * WIN), cur),
        out_shape=jax.ShapeDtypeStruct((D, S), bf16),
        compiler_params=_cp("parallel"),
    )(qt, kvt, kvt, cos, sin, cos, sin, sinks)


def attn_bwd(qt, kvt, cos, sin, sinks, daot, comm=None):
    S = qt.shape[1]
    nb = S // WIN
    cur = lambda i: (0, jnp.minimum(i, nb - 1))
    prev = lambda i: (0, jnp.maximum(i - 1, 0))
    cm = _Comm(comm)

    def body(*refs):
        ins, (dq_ref, dkv_ref, ds_ref), scr, copies = cm.split(refs, 9, 3)
        q_ref, kv_ref, kvp_ref, cos_ref, sin_ref, cosp_ref, sinp_ref, sink_ref, do_ref = ins
        ck_ref, cv_ref, dqs_ref, dkp_ref, dvp_ref, dkc_ref, dvc_ref, accs_ref = scr
        i = pl.program_id(0)
        cm.start(copies, i == 0)

        @pl.when(i == 0)
        def _():
            ck_ref[...] = jnp.zeros_like(ck_ref)
            cv_ref[...] = jnp.zeros_like(cv_ref)
            accs_ref[...] = jnp.zeros_like(accs_ref)

        @pl.when(i == nb)
        def _():
            dkp_ref[...] = jnp.zeros_like(dkp_ref)
            dvp_ref[...] = jnp.zeros_like(dvp_ref)

        @pl.when(i < nb)
        def _():
            q = (_rope(q_ref[...].astype(f32), cos_ref[...], sin_ref[...]) * (AD ** -0.5)).astype(bf16)
            kc = _rope(kv_ref[0:KVW, :].astype(f32), cos_ref[...], sin_ref[...]).astype(bf16)
            kp = _rope(kvp_ref[0:KVW, :].astype(f32), cosp_ref[...], sinp_ref[...]).astype(bf16)
            do = do_ref[...]
            for g in range(KVH):
                ks = slice(g * AD, (g + 1) * AD)
                vs = slice(KVW + g * AD, KVW + (g + 1) * AD)
                qg = _lanes_of_group(q, g)
                dog = _lanes_of_group(do, g)
                k2 = jnp.concatenate([kp[ks], kc[ks]], axis=1)
                v2 = jnp.concatenate([kvp_ref[vs, :], kv_ref[vs, :]], axis=1)
                p, ps = _attn_probs(qg, k2, sink_ref, g, i > 0)
                dp = _dot(v2, dog, "tn")
                delta = jnp.sum(p * dp, axis=0, keepdims=True)
                ds = (p * (dp - delta)).astype(bf16)
                accs_ref[g:g + 1, :] -= ps * delta
                dqg = _dot(k2, ds) * (AD ** -0.5)
                for r in range(REP):
                    h = g * REP + r
                    dqs_ref[h * AD:(h + 1) * AD, :] = dqg[:, r * WIN:(r + 1) * WIN]
                dk2 = _dot(qg, ds, "nt")
                dv2 = _dot(dog, p.astype(bf16), "nt")
                dkp_ref[ks, :] = dk2[:, 0:WIN]
                dkc_ref[ks, :] = dk2[:, WIN:2 * WIN]
                dvp_ref[ks, :] = dv2[:, 0:WIN]
                dvc_ref[ks, :] = dv2[:, WIN:2 * WIN]
            dq_ref[...] = _rope_t(dqs_ref[...], cos_ref[...], sin_ref[...]).astype(bf16)

        dkv_ref[0:KVW, :] = _rope_t(ck_ref[...] + dkp_ref[...], cosp_ref[...], sinp_ref[...]).astype(bf16)
        dkv_ref[KVW:2 * KVW, :] = (cv_ref[...] + dvp_ref[...]).astype(bf16)

        @pl.when(i < nb)
        def _():
            ck_ref[...] = dkc_ref[...]
            cv_ref[...] = dvc_ref[...]

        @pl.when(i == nb)
        def _():
            lane = lax.broadcasted_iota(jnp.int32, (1, LANES), 1)
            row = jnp.zeros((1, LANES), f32)
            for h in range(AH):
                part = accs_ref[h // REP:h // REP + 1, (h % REP) * WIN:(h % REP + 1) * WIN]
                row = row + jnp.where(lane == h, jnp.sum(part, axis=1, keepdims=True), 0.0)
            ds_ref[...] = jnp.zeros_like(ds_ref)
            ds_ref[0:1, :] = row

        cm.wait(copies, i == nb)

    tab = pl.BlockSpec((AD, WIN), cur)
    tabp = pl.BlockSpec((AD, WIN), prev)
    kvs = lambda: pltpu.VMEM((KVW, WIN), f32)
    res = pl.pallas_call(
        body, name="attn_bwd", grid=(nb + 1,),
        in_specs=[pl.BlockSpec((D, WIN), cur), pl.BlockSpec((2 * KVW, WIN), cur), pl.BlockSpec((2 * KVW, WIN), prev),
                  tab, tab, tabp, tabp, pl.BlockSpec((1, LANES), lambda i: (0, 0)),
                  pl.BlockSpec((D, WIN), cur)] + cm.in_specs(),
        out_specs=[pl.BlockSpec((D, WIN), cur), pl.BlockSpec((2 * KVW, WIN), prev),
                   pl.BlockSpec((8, LANES), lambda i: (0, 0))] + cm.out_specs(),
        out_shape=[jax.ShapeDtypeStruct((D, S), bf16), jax.ShapeDtypeStruct((2 * KVW, S), bf16),
                   jax.ShapeDtypeStruct((8, LANES), f32)] + cm.out_shape(),
        scratch_shapes=[kvs(), kvs(), pltpu.VMEM((D, WIN), f32), kvs(), kvs(), kvs(), kvs(),
                        pltpu.VMEM((8, REP * WIN), f32)] + cm.scratch(),
        compiler_params=_cp("arbitrary", side_effects=bool(cm.n)),
    )(qt, kvt, kvt, cos, sin, cos, sin, sinks, daot, *cm.bufs)
    return res[0], res[1], res[2], res[3:]


ADAM_C1 = 1.0 / (1.0 - ADAM_B1 ** ADAM_STEP)
ADAM_C2 = 1.0 / (1.0 - ADAM_B2 ** ADAM_STEP)


def _adam_update(g, w, m, v):
    nm = ADAM_B1 * m + (1.0 - ADAM_B1) * g
    nv = ADAM_B2 * v + (1.0 - ADAM_B2) * (g * g)
    return -ADAM_LR * ((nm * ADAM_C1) / (jnp.sqrt(nv * ADAM_C2) + ADAM_EPS) + ADAM_WD * w), nm, nv


def adamw(parts, w, m, v, tr, name):
    n, R, C = parts.shape

    def body(p_ref, w_ref, m_ref, v_ref, g_ref, d_ref, nm_ref, nv_ref):
        def grp(g0, _):
            r0 = pl.multiple_of(g0 * RG, RG)
            rows = pl.ds(r0, RG)
            g = p_ref[0, rows, :].astype(f32)
            for k in range(1, n):
                g = g + p_ref[k, rows, :].astype(f32)
            d, nm, nv = _adam_update(g, w_ref[rows, :], m_ref[rows, :], v_ref[rows, :])
            g_ref[rows, :] = g
            d_ref[rows, :] = d
            nm_ref[rows, :] = nm
            nv_ref[rows, :] = nv
            return 0

        lax.fori_loop(0, tr // RG, grp, 0)

    row = pl.BlockSpec((tr, C), lambda i: (i, 0))
    o = jax.ShapeDtypeStruct((R, C), f32)
    return pl.pallas_call(
        body, name=name, grid=(R // tr,),
        in_specs=[pl.BlockSpec((n, tr, C), lambda i: (0, i, 0)), row, row, row],
        out_specs=[row, row, row, row], out_shape=[o, o, o, o],
        compiler_params=_cp("parallel"),
    )(parts, w, m, v)


SMALL_ROW = (("norm_mix_post_w", D), ("norm_ffn_pre_w", D), ("norm_ffn_post_w", D), ("ssd_norm_w", DI),
             ("ssd_conv_b", CONVD), ("ffn_conv_b", 2 * FF), ("ssd_dt_bias", NH), ("ssd_a_log", NH), ("ssd_d", NH),
             ("attn_sinks", AH), ("loss", 1))
CONV_BLOCK = 1152
SSD_CONV_COLS = CONVD // N_DEV
FFN_CONV_COLS = 2 * FF // N_DEV


def _row_offsets():
    off, o = {}, 0
    for name, n in SMALL_ROW:
        off[name] = (o, n)
        o += -(-n // LANES) * LANES
    return off, o


def adamw_small(recv_row, recv_pre, recv_conv, params):
    off, _ = _row_offsets()
    names = list(params)
    n = len(names)

    def total(ref, rows, lo, width):
        g = ref[0, rows, lo:lo + width]
        for d in range(1, N_DEV):
            g = g + ref[d, rows, lo:lo + width]
        return g

    def grad_of(name, row_ref, pre_ref, conv_ref):
        if name == "norm_mix_pre_w":
            return total(pre_ref, slice(0, 1), 0, D)
        if name == "ssd_conv_w":
            return total(conv_ref, slice(0, SSD_K), 0, SSD_CONV_COLS)
        if name == "ffn_conv_w":
            return total(conv_ref, slice(0, FFN_K), 3 * LANES, FFN_CONV_COLS)
        o, width = off[name]
        return total(row_ref, slice(0, 1), o, width)

    def body(row_ref, pre_ref, conv_ref, *refs):
        ins, outs = refs[:3 * n], refs[3 * n:]
        for k, name in enumerate(names):
            w_ref, m_ref, v_ref = ins[3 * k:3 * k + 3]
            g_ref, d_ref, nm_ref, nv_ref = outs[4 * k:4 * k + 4]
            g = grad_of(name, row_ref, pre_ref, conv_ref)
            d, nm, nv = _adam_update(g, w_ref[...], m_ref[...], v_ref[...])
            g_ref[...] = g
            d_ref[...] = d
            nm_ref[...] = nm
            nv_ref[...] = nv
        outs[4 * n][...] = total(row_ref, slice(0, 1), off["loss"][0], LANES)

    flat = [t for name in names for t in params[name]]
    out_shape = [jax.ShapeDtypeStruct(params[name][0].shape, f32) for name in names for _ in range(4)]
    res = pl.pallas_call(
        body, name="adamw_small",
        out_shape=out_shape + [jax.ShapeDtypeStruct((1, LANES), f32)],
        compiler_params=pltpu.CompilerParams(vmem_limit_bytes=VMEM_LIMIT),
    )(recv_row, recv_pre, recv_conv, *flat)
    return {name: res[4 * k:4 * k + 4] for k, name in enumerate(names)}, res[4 * n]


def _cat_rows(parts):
    words = [lax.bitcast_convert_type(p, jnp.uint16) for p in parts]
    return lax.bitcast_convert_type(jnp.concatenate(words, axis=0), bf16)


def _pad_rows8(w):
    return jnp.pad(w, ((0, 8 - w.shape[0]), (0, 0)))


def _pad_lanes(v):
    return jnp.pad(v.reshape(1, -1), ((0, 0), (0, LANES - v.size)))


WEIGHTS = ('norm_mix_pre_w', 'w_in', 'ssd_conv_w', 'ssd_conv_b', 'ssd_dt_bias', 'ssd_a_log', 'ssd_d', 'ssd_norm_w',
           'ssd_w_out', 'attn_sinks', 'attn_w_out', 'w_mix_out', 'norm_mix_post_w', 'norm_ffn_pre_w', 'ffn_w_up',
           'ffn_conv_w', 'ffn_conv_b', 'ffn_w_down', 'norm_ffn_post_w')
W_IN_ROWS = IN_DIM // N_DEV
W_IN_PAD = 1104
W_IN_SPLIT = (672, 768, 832)
TS = 512
TS_NORM = 1024


def kernel(x, positions, norm_mix_pre_w, w_in, ssd_conv_w, ssd_conv_b, ssd_dt_bias, ssd_a_log, ssd_d, ssd_norm_w, ssd_w_out, attn_sinks, attn_w_out, w_mix_out, norm_mix_post_w, norm_ffn_pre_w, ffn_w_up, ffn_conv_w, ffn_conv_b, ffn_w_down, norm_ffn_post_w, loss_target, m_norm_mix_pre_w, m_w_in, m_ssd_conv_w, m_ssd_conv_b, m_ssd_dt_bias, m_ssd_a_log, m_ssd_d, m_ssd_norm_w, m_ssd_w_out, m_attn_sinks, m_attn_w_out, m_w_mix_out, m_norm_mix_post_w, m_norm_ffn_pre_w, m_ffn_w_up, m_ffn_conv_w, m_ffn_conv_b, m_ffn_w_down, m_norm_ffn_post_w, v_norm_mix_pre_w, v_w_in, v_ssd_conv_w, v_ssd_conv_b, v_ssd_dt_bias, v_ssd_a_log, v_ssd_d, v_ssd_norm_w, v_ssd_w_out, v_attn_sinks, v_attn_w_out, v_w_mix_out, v_norm_mix_post_w, v_norm_ffn_pre_w, v_ffn_w_up, v_ffn_conv_w, v_ffn_conv_b, v_ffn_w_down, v_norm_ffn_post_w):
    a = locals()
    r2 = lambda t: t.reshape(t.shape[-2], t.shape[-1])
    w = {n: r2(a[n]) for n in WEIGHTS}
    m = {n: r2(a["m_" + n]) for n in WEIGHTS}
    v = {n: r2(a["v_" + n]) for n in WEIGHTS}
    xs, target = x[0], loss_target[0]
    S = xs.shape[0]
    ts, tsn = TS, min(TS_NORM, S)

    w_in_blk = jnp.pad(w["w_in"].T.astype(bf16), ((0, W_IN_PAD - W_IN_ROWS), (0, 0)))
    conv_blk = jnp.concatenate([_pad_rows8(w["ssd_conv_w"]), _pad_rows8(w["ffn_conv_w"]),
                                jnp.zeros((8, CONV_BLOCK - SSD_CONV_COLS - FFN_CONV_COLS), f32)], axis=1)
    u, cos, sin, (g_in, g_conv) = prenorm_fwd(xs, w["norm_mix_pre_w"], positions, ts, [w_in_blk, conv_blk])
    wt = g_in[:, :W_IN_ROWS].reshape(IN_DIM, D)
    w_main_t = _cat_rows([wt[IN_OFF[0]:IN_OFF[1]], wt[IN_OFF[6]:IN_OFF[8]], wt[IN_OFF[1]:IN_OFF[2]]])
    w_q_t = wt[IN_OFF[3]:IN_OFF[4]]
    w_kv_t = wt[IN_OFF[4]:IN_OFF[6]]
    w_dt_t = jnp.pad(wt[IN_OFF[2]:IN_OFF[3]], ((0, LANES - NH), (0, 0)))
    conv_w8 = g_conv[:, :, 0:SSD_CONV_COLS].transpose(1, 0, 2).reshape(8, CONVD)
    fconv_w8 = g_conv[:, :, SSD_CONV_COLS:SSD_CONV_COLS + FFN_CONV_COLS].transpose(1, 0, 2).reshape(8, 2 * FF)
    bias = _pad_lanes(w["ssd_dt_bias"])
    alog = _pad_lanes(w["ssd_a_log"])
    dx_row = jnp.repeat(w["ssd_d"].reshape(-1), HD).reshape(1, DI)
    sinks = _pad_lanes(w["attn_sinks"])

    later = [w["ssd_w_out"].astype(bf16), w["attn_w_out"].astype(bf16), w["w_mix_out"].astype(bf16)]
    proj, (g_so, g_ao, g_mix) = mm(u, w_main_t, "nt", bf16, "mm_proj", comm=(later, (False,) * 3))
    w_ssd_out, w_attn_out, w_mix = g_so.reshape(DI, D), g_ao.reshape(D, D), g_mix.reshape(D, D)
    qt = mm(w_q_t, u, "nt", bf16, "mm_q")
    kvt = mm(w_kv_t, u, "nt", bf16, "mm_kv")
    dtr = mm(u, w_dt_t, "nt", f32, "mm_dt")
    xbc, conv_c = ssdconv_fwd(proj, conv_w8, w["ssd_conv_b"], tsn)
    y, hprev, ssd_saved, (g_up, g_down) = ssd_fwd(xbc, dtr, bias, alog, dx_row, comm=(
        [w["ffn_w_up"].T.astype(bf16), w["ffn_w_down"].astype(bf16)], (False, False)))
    w_up_t = g_up.reshape(2 * FF, D)
    w_down = g_down.reshape(FF, D)
    yn = gnorm_fwd(y, proj, w["ssd_norm_w"], tsn)
    ys = mm(yn, w_ssd_out, "nn", bf16, "mm_ssd_out")
    aot = attn_fwd(qt, kvt, cos, sin, sinks)
    ya = mm(aot, w_attn_out, "tn", bf16, "mm_attn_out")
    merged = merge_fwd(proj, ys, ya, tsn)
    mo = mm(merged, w_mix, "nn", bf16, "mm_mix")
    x1, h = post_fwd(xs, mo, w["norm_mix_post_w"], w["norm_ffn_pre_w"], tsn)
    up = mm(h, w_up_t, "nt", bf16, "mm_up")
    act, gate_c, val_c = ffnact_fwd(up, fconv_w8, w["ffn_conv_b"], ts)
    ff = mm(act, w_down, "nn", bf16, "mm_down")
    loss_blk, dout, dff, g_post2 = loss_head(x1, ff, target, w["norm_ffn_post_w"], tsn)

    dact = mm(dff, w_down, "nt", bf16, "mm_dact")
    gw_down = mm(act, dff, "tn", bf16, "mm_g_down")
    dgate, dval = ffnact_bwd(dact, gate_c, val_c, ts)
    dup_pre, g_fconv_a = dwconv_bwd(dgate, up, 0, fconv_w8, 0, FFN_K, FF, ts, "ffnconv_bwd_gate", out_cols=2 * FF)
    dup_pre, g_fconv_b = dwconv_bwd(dval, up, 1, fconv_w8, 1, FFN_K, FF, ts, "ffnconv_bwd_val", into=dup_pre, ocb=1,
                                    out_cols=2 * FF)
    g_fconv = jnp.concatenate([g_fconv_a, g_fconv_b], axis=1)
    dh, (r_down,) = mm(dup_pre, w_up_t, "nn", bf16, "mm_dh", comm=([gw_down.reshape(N_DEV, FF // N_DEV, D)], (True,)))
    gw_up_t = mm(dup_pre, h, "tn", bf16, "mm_g_up")
    dx1, dmo, g_norms = post_bwd(dout, dh, x1, mo, w["norm_mix_post_w"], w["norm_ffn_pre_w"], tsn)
    dmerged = mm(dmo, w_mix, "nt", bf16, "mm_dmerged")
    gw_mix = mm(merged, dmo, "tn", bf16, "mm_g_mix")
    dys, dya, dproj = merge_bwd(dmerged, proj, ys, ya, tsn)
    daot = mm(w_attn_out, dya, "nt", bf16, "mm_dao")
    gw_attn_out = mm(aot, dya, "nn", bf16, "mm_g_attn_out")
    dqt, dkvt, g_sinks, (r_up,) = attn_bwd(qt, kvt, cos, sin, sinks, daot,
                                           comm=([gw_up_t.reshape(N_DEV, 2 * FF // N_DEV, D)], (True,)))
    dyn = mm(dys, w_ssd_out, "nt", bf16, "mm_dyn")
    gw_ssd_out = mm(yn, dys, "tn", bf16, "mm_g_ssd_out")
    dy, dproj, g_gnorm = gnorm_bwd(dyn, y, proj, w["ssd_norm_w"], dproj, tsn)
    sends = [gw_ssd_out.reshape(N_DEV, DI // N_DEV, D), gw_attn_out.reshape(N_DEV, D // N_DEV, D),
             gw_mix.reshape(N_DEV, D // N_DEV, D)]
    dxbc, ddtr, g_ssd, (r_so, r_ao, r_mix) = ssd_bwd(xbc, dtr, dy, hprev, ssd_saved, bias, alog, dx_row,
                                                     comm=(sends, (True,) * 3))
    dproj, g_conv_w = dwconv_bwd(dxbc, proj, C_XBC // 1024, conv_w8, 0, SSD_K, 1024, tsn, "ssdconv_bwd", act_c=conv_c,
                                 into=dproj, ocb=C_XBC // 1024, out_cols=PM)
    ddtr_b = ddtr.astype(bf16)
    du_c = mm(ddtr_b, w_dt_t, "nn", bf16, "mm_du_dt")
    g_main_t = mm(dproj, u, "tn", bf16, "mm_g_in")
    g_q_t = mm(dqt, u, "nn", bf16, "mm_g_q")
    g_kv_t = mm(dkvt, u, "nn", bf16, "mm_g_kv")
    g_dt_t = mm(ddtr_b, u, "tn", bf16, "mm_g_dt")
    g_wt = _cat_rows([g_main_t[C_Z:C_GS], g_main_t[C_XBC:PM], g_dt_t[:NH], g_q_t, g_kv_t, g_main_t[C_GS:C_XBC]])
    send_in = jnp.pad(g_wt.reshape(N_DEV, W_IN_ROWS, D), ((0, 0), (0, W_IN_PAD - W_IN_ROWS), (0, 0)))
    pieces = {"norm_mix_post_w": g_norms[1:2], "norm_ffn_pre_w": g_norms[0:1], "norm_ffn_post_w": g_post2[0:1],
              "ssd_norm_w": g_gnorm[0:1], "ssd_conv_b": g_conv_w[7:8], "ffn_conv_b": g_fconv[7:8],
              "ssd_dt_bias": g_ssd[0:1], "ssd_a_log": g_ssd[1:2], "ssd_d": g_ssd[2:3], "attn_sinks": g_sinks[0:1],
              "loss": loss_blk[0:1]}
    row = jnp.concatenate([jnp.pad(pieces[n][:, :min(k, pieces[n].shape[1])],
                                   ((0, 0), (0, -(-k // LANES) * LANES - min(k, pieces[n].shape[1]))))
                           for n, k in SMALL_ROW], axis=1)
    send_row = jnp.pad(row, ((0, 7), (0, 0)))
    send_conv = jnp.concatenate(
        [g_conv_w.reshape(8, N_DEV, SSD_CONV_COLS).transpose(1, 0, 2),
         g_fconv.reshape(8, N_DEV, FFN_CONV_COLS).transpose(1, 0, 2),
         jnp.zeros((N_DEV, 8, CONV_BLOCK - SSD_CONV_COLS - FFN_CONV_COLS), f32)], axis=2)
    r0, r1, r2 = W_IN_SPLIT
    du_a, (r_in_a, recv_row, recv_conv) = mm(dproj, w_main_t, "nn", bf16, "mm_du", comm=(
        [send_in[:, :r0], send_row, send_conv], (True, False, True)))
    du_d, (r_in_b,) = mm(dqt, w_q_t, "tn", bf16, "mm_du_q", comm=([send_in[:, r0:r1]], (True,)))
    du_b, (r_in_c,) = mm(dkvt, w_kv_t, "tn", bf16, "mm_du_kv", comm=([send_in[:, r1:r2]], (True,)))
    grad_x, g_pre, (r_in_d,) = prenorm_bwd(xs, w["norm_mix_pre_w"], (du_a, du_b, du_c, du_d), dx1, tsn,
                                           comm=([send_in[:, r2:]], (True,)))
    (recv_pre,) = exchange([g_pre], (False,), "gather_last")

    r_in = jnp.concatenate([r_in_a, r_in_b, r_in_c, r_in_d], axis=1)
    tpad = lambda t: jnp.pad(t.T, ((0, W_IN_PAD - W_IN_ROWS), (0, 0)))
    o_in = [t[:W_IN_ROWS].T for t in adamw(r_in, tpad(w["w_in"]), tpad(m["w_in"]), tpad(v["w_in"]), 368, "adamw_w_in")]
    o_up = [t.T for t in adamw(r_up, w["ffn_w_up"].T, m["ffn_w_up"].T, v["ffn_w_up"].T, 352, "adamw_w_up")]
    big = {"w_in": o_in, "ffn_w_up": o_up,
           "ssd_w_out": adamw(r_so, w["ssd_w_out"], m["ssd_w_out"], v["ssd_w_out"], 256, "adamw_ssd_out"),
           "attn_w_out": adamw(r_ao, w["attn_w_out"], m["attn_w_out"], v["attn_w_out"], 128, "adamw_attn_out"),
           "w_mix_out": adamw(r_mix, w["w_mix_out"], m["w_mix_out"], v["w_mix_out"], 128, "adamw_mix"),
           "ffn_w_down": adamw(r_down, w["ffn_w_down"], m["ffn_w_down"], v["ffn_w_down"], 352, "adamw_down")}
    small_names = [n for n in WEIGHTS if n not in big]
    small, loss_row = adamw_small(recv_row, recv_pre, recv_conv, {n: (w[n], m[n], v[n]) for n in small_names})

    outs = [loss_row[0, 0], grad_x[None]]
    for k in range(4):
        for n in WEIGHTS:
            outs.append((big[n][k] if n in big else small[n][k]).reshape(a[n].shape))
    return tuple(outs)
```

```python
import jax
import jax.numpy as jnp
import numpy as np
from jax import lax
from jax.experimental import pallas as pl
from jax.experimental.pallas import tpu as pltpu

f32 = jnp.float32
bf16 = jnp.bfloat16

N_DEV = 8
D = 1024
DI = 2048
NH = 32
HD = 64
NG = 4
GW = DI // NG
NS = 128
CH = 128
CONVD = DI + 2 * NG * NS
SSD_K = 4
AH = 16
AD = 64
KVH = 4
REP = AH // KVH
KVW = KVH * AD
WIN = 128
FF = 2816
FFN_K = 3
EPS = 1e-6
ROPE_THETA = 10000.0
LANES = 128
RG = 16
CW = 256

C_Z, C_GS, C_GA, C_XBC, PM = 0, 2048, 3072, 4096, 7168
IN_SIZES = (DI, CONVD, NH, D, KVW, KVW, D, D)
IN_OFF = tuple(int(v) for v in np.cumsum((0,) + IN_SIZES))
IN_DIM = IN_OFF[-1]

ADAM_LR, ADAM_B1, ADAM_B2, ADAM_EPS, ADAM_WD, ADAM_STEP = 0.001, 0.9, 0.999, 1e-08, 0.01, 10

VMEM_LIMIT = 56 * 1024 * 1024


def _cp(*sem, side_effects=False):
    return pltpu.CompilerParams(dimension_semantics=sem, vmem_limit_bytes=VMEM_LIMIT, has_side_effects=side_effects)


def _dot(a, b, mode="nn"):
    dims = {"nn": (((1,), (0,)), ((), ())), "nt": (((1,), (1,)), ((), ())), "tn": (((0,), (0,)), ((), ()))}[mode]
    return lax.dot_general(a, b, dims, preferred_element_type=f32)


def _split3(v):
    hi = v.astype(bf16)
    r = v - hi.astype(f32)
    mid = r.astype(bf16)
    lo = (r - mid.astype(f32)).astype(bf16)
    return hi, mid, lo


def _dot3_left(m01, v):
    hi, mid, lo = _split3(v)
    return _dot(m01, hi) + _dot(m01, mid) + _dot(m01, lo)


def _dot3_right(v, m01):
    hi, mid, lo = _split3(v)
    return _dot(hi, m01) + _dot(mid, m01) + _dot(lo, m01)


def _dot2_right(v, m01):
    hi = v.astype(bf16)
    lo = (v - hi.astype(f32)).astype(bf16)
    return _dot(hi, m01) + _dot(lo, m01)


def _sigmoid(x):
    return 1.0 / (1.0 + jnp.exp(-x))


def _sigmoid_fast(x):
    return pl.reciprocal(1.0 + jnp.exp(-x), approx=True)


def _peer(k, x, y, c):
    return ((1 - x) if k & 4 else x, (1 - y) if k & 2 else y, (1 - c) if k & 1 else c)


def _xchg_copies(buf_refs, out_refs, send_sems, recv_sems, local_sems, personalised):
    x, y, c = lax.axis_index("x"), lax.axis_index("y"), lax.axis_index("c")
    me = 4 * x + 2 * y + c
    local, remote = [], []
    for b, (buf, out, pers) in enumerate(zip(buf_refs, out_refs, personalised)):
        local.append(pltpu.make_async_copy(buf.at[me] if pers else buf, out.at[me], local_sems.at[b]))
        for k in range(1, N_DEV):
            px, py, pc = _peer(k, x, y, c)
            s = b * (N_DEV - 1) + k - 1
            remote.append(pltpu.make_async_remote_copy(
                src_ref=buf.at[4 * px + 2 * py + pc] if pers else buf, dst_ref=out.at[me],
                send_sem=send_sems.at[s], recv_sem=recv_sems.at[s],
                device_id=(px, py, pc), device_id_type=pl.DeviceIdType.MESH))
    return local, remote


class _Comm:
    def __init__(self, comm):
        self.bufs, self.pers = comm if comm else ((), ())
        self.n = len(self.bufs)

    def in_specs(self):
        return [pl.BlockSpec(memory_space=pl.ANY)] * self.n

    out_specs = in_specs

    def out_shape(self):
        return [jax.ShapeDtypeStruct((N_DEV,) + tuple(b.shape[1:] if p else b.shape), b.dtype)
                for b, p in zip(self.bufs, self.pers)]

    def scratch(self):
        n = self.n
        return [pltpu.SemaphoreType.DMA((n * (N_DEV - 1),)), pltpu.SemaphoreType.DMA((n * (N_DEV - 1),)),
                pltpu.SemaphoreType.DMA((n,))] if n else []

    def split(self, refs, n_in, n_out):
        n = self.n
        ins, outs = refs[:n_in], refs[n_in + n:n_in + n + n_out]
        rest = refs[n_in + n + n_out + n:]
        if not n:
            return ins, outs, rest, None
        copies = _xchg_copies(refs[n_in:n_in + n], refs[n_in + n + n_out:n_in + n + n_out + n], *rest[-3:], self.pers)
        return ins, outs, rest[:-3], copies

    def start(self, copies, first):
        if copies:
            @pl.when(first)
            def _():
                for cp in copies[0] + copies[1]:
                    cp.start()

    def wait(self, copies, last):
        if copies:
            @pl.when(last)
            def _():
                for cp in copies[1]:
                    cp.wait_recv()
                for cp in copies[1]:
                    cp.wait_send()
                for cp in copies[0]:
                    cp.wait()


def exchange(bufs, personalised, name):
    cm = _Comm((bufs, personalised))

    def body(*refs):
        _, _, _, copies = cm.split(refs, 0, 0)
        cm.start(copies, True)
        cm.wait(copies, True)

    return pl.pallas_call(
        body, name=name, in_specs=cm.in_specs(), out_specs=cm.out_specs(), out_shape=cm.out_shape(),
        scratch_shapes=cm.scratch(), compiler_params=pltpu.CompilerParams(has_side_effects=True),
    )(*bufs)


class _TwoLevelGather:
    def __init__(self, bufs):
        self.bufs = list(bufs)
        self.n = len(self.bufs)

    def in_specs(self):
        return [pl.BlockSpec(memory_space=pl.ANY)] * self.n

    out_specs = in_specs

    def out_shape(self):
        return [jax.ShapeDtypeStruct((N_DEV,) + tuple(b.shape), b.dtype) for b in self.bufs]

    def scratch(self):
        per = N_DEV - 1
        return [pltpu.SemaphoreType.DMA((self.n * per,)), pltpu.SemaphoreType.DMA((self.n * per,)),
                pltpu.SemaphoreType.DMA((self.n,))]

    def bind(self, ins, outs, send_sems, recv_sems, local_sems):
        n, per = self.n, N_DEV - 1
        x, y, c = lax.axis_index("x"), lax.axis_index("y"), lax.axis_index("c")
        me, sibling = (x, y, c), (x, y, 1 - c)
        chips = [(1 - x, y), (x, 1 - y), (1 - x, 1 - y)]

        def copy(b, k, block, to, src=None):
            dst = outs[b].at[4 * block[0] + 2 * block[1] + block[2]]
            return pltpu.make_async_remote_copy(
                src_ref=dst if src is None else src, dst_ref=dst,
                send_sem=send_sems.at[b * per + k], recv_sem=recv_sems.at[b * per + k],
                device_id=to, device_id_type=pl.DeviceIdType.MESH)

        mine = [pltpu.make_async_copy(ins[b], outs[b].at[4 * x + 2 * y + c], local_sems.at[b]) for b in range(n)]
        first = []
        for b in range(n):
            first.append(copy(b, 0, me, sibling, src=ins[b]))
            first += [copy(b, 1 + j, me, (*chip, c), src=ins[b]) for j, chip in enumerate(chips)]

        def start():
            for cp in mine + first:
                cp.start()

        def finish():
            passed = []
            for j, chip in enumerate(chips):
                for b in range(n):
                    copy(b, 1 + j, (*chip, c), me).wait_recv()
                    passed.append(copy(b, 4 + j, (*chip, c), sibling))
                    passed[-1].start()
            for b in range(n):
                copy(b, 0, sibling, me).wait_recv()
                for j, chip in enumerate(chips):
                    copy(b, 4 + j, (*chip, 1 - c), me).wait_recv()
            for cp in first + passed:
                cp.wait_send()
            for cp in mine:
                cp.wait()

        return start, finish


MM_TILES = (3584, 2176, 2048, 1792, 1408, 1024, 512, 256, 128)
MM_K_TILES = (4096, 3584, 2816) + MM_TILES[1:]
MM_VMEM_BUDGET = 44 * 1024 * 1024


def _mm_tiles(M, N, K, out_bytes):
    cm = [t for t in MM_TILES if M % t == 0]
    cn = [t for t in MM_TILES if N % t == 0]
    ck = [t for t in MM_K_TILES if K % t == 0]
    best = None
    for bm in cm[:2]:
        for bn in cn:
            for bk in ck:
                need = 4 * (bm * bk + bk * bn) + bm * bn * (4 + 2 * out_bytes)
                if need <= MM_VMEM_BUDGET:
                    score = (bm * bn * bk, bk)
                    if best is None or score > best[0]:
                        best = (score, (bm, bn, bk))
    return best[1]


def mm(a, b, mode, out_dtype, name, comm=None):
    if mode == "nn":
        (M, K), (_, N) = a.shape, b.shape
    elif mode == "nt":
        (M, K), (N, _) = a.shape, b.shape
    else:
        (K, M), (_, N) = a.shape, b.shape
    bm, bn, bk = _mm_tiles(M, N, K, jnp.dtype(out_dtype).itemsize)
    gm, gn, nk = M // bm, N // bn, K // bk
    cm = _Comm(comm)

    def body(*refs):
        (a_ref, b_ref), (o_ref,), scr, copies = cm.split(refs, 2, 1)
        i, j, k = pl.program_id(0), pl.program_id(1), pl.program_id(2)
        cm.start(copies, jnp.logical_and(jnp.logical_and(i == 0, j == 0), k == 0))
        p = _dot(a_ref[...], b_ref[...], mode)
        if nk == 1:
            o_ref[...] = p.astype(o_ref.dtype)
        else:
            acc_ref = scr[0]

            @pl.when(k == 0)
            def _():
                acc_ref[...] = p

            @pl.when(k > 0)
            def _():
                acc_ref[...] += p

            @pl.when(k == nk - 1)
            def _():
                o_ref[...] = acc_ref[...].astype(o_ref.dtype)

        cm.wait(copies, jnp.logical_and(jnp.logical_and(i == gm - 1, j == gn - 1), k == nk - 1))

    if mode == "nn":
        a_spec = pl.BlockSpec((bm, bk), lambda i, j, k: (i, k))
        b_spec = pl.BlockSpec((bk, bn), lambda i, j, k: (k, j))
    elif mode == "nt":
        a_spec = pl.BlockSpec((bm, bk), lambda i, j, k: (i, k))
        b_spec = pl.BlockSpec((bn, bk), lambda i, j, k: (j, k))
    else:
        a_spec = pl.BlockSpec((bk, bm), lambda i, j, k: (k, i))
        b_spec = pl.BlockSpec((bk, bn), lambda i, j, k: (k, j))
    sem = ("arbitrary",) * 3 if cm.n else ("parallel", "parallel", "arbitrary")
    res = pl.pallas_call(
        body, name=name, grid=(gm, gn, nk),
        in_specs=[a_spec, b_spec] + cm.in_specs(),
        out_specs=[pl.BlockSpec((bm, bn), lambda i, j, k: (i, j))] + cm.out_specs(),
        out_shape=[jax.ShapeDtypeStruct((M, N), out_dtype)] + cm.out_shape(),
        scratch_shapes=([pltpu.VMEM((bm, bn), f32)] if nk > 1 else []) + cm.scratch(),
        compiler_params=_cp(*sem, side_effects=bool(cm.n)),
    )(a, b, *cm.bufs)
    return (res[0], res[1:]) if cm.n else res[0]


def _groups(ts, fn, carry=None, reverse=False, unroll=8, rg=RG):
    n = ts // rg
    if n == 1:
        return fn(0, carry)
    unroll = min(unroll, n)
    span = rg * unroll

    def body(g, c):
        r0 = pl.multiple_of((n // unroll - 1 - g if reverse else g) * span, span)
        for u in (range(unroll - 1, -1, -1) if reverse else range(unroll)):
            c = fn(pl.multiple_of(r0 + u * rg, rg), c)
        return c

    return lax.fori_loop(0, n // unroll, body, carry)


def _rms(x):
    return lax.rsqrt(jnp.mean(x * x, axis=-1, keepdims=True) + EPS)


def _rms_bwd(x, r, dn):
    n = x * r
    return r * (dn - n * jnp.mean(dn * n, axis=-1, keepdims=True))


NRG = 256


def _fold(x):
    return jnp.sum(x.reshape(x.shape[0] // 8, 8, x.shape[1]), axis=0)


def _flush(acc_ref, out_ref, row):
    out_ref[row:row + 1, :] = jnp.sum(acc_ref[...], axis=0, keepdims=True)


def prenorm_fwd(x, w, pos_row, ts, gather):
    S = x.shape[0]
    nt = S // ts
    tg = _TwoLevelGather(gather)
    n = tg.n
    half = AD // 2
    inv = ROPE_THETA ** (-jnp.arange(half, dtype=f32) * 2.0 / AD)
    inv_col = jnp.tile(inv, 2)[:, None]

    def body(x_ref, w_ref, p_ref, inv_ref, *refs):
        u_ref, cos_ref, sin_ref = refs[n:n + 3]
        start, finish = tg.bind(refs[:n], refs[n + 3:2 * n + 3], *refs[2 * n + 3:])
        i = pl.program_id(0)
        pl.when(i == 0)(start)
        wv = w_ref[...]

        def grp(r0, _):
            xv = x_ref[pl.ds(r0, NRG), :]
            u_ref[pl.ds(r0, NRG), :] = (xv * _rms(xv) * wv).astype(bf16)

        _groups(ts, grp, rg=NRG)
        ang = inv_ref[...] * p_ref[...].astype(f32)
        row = lax.broadcasted_iota(jnp.int32, ang.shape, 0)
        cos_ref[...] = jnp.cos(ang)
        sin_ref[...] = jnp.where(row < half, -1.0, 1.0) * jnp.sin(ang)
        pl.when(i == nt - 1)(finish)

    tab = pl.BlockSpec((AD, ts), lambda i: (0, i))
    res = pl.pallas_call(
        body, name="prenorm_fwd", grid=(nt,),
        in_specs=[pl.BlockSpec((ts, D), lambda i: (i, 0)), pl.BlockSpec((1, D), lambda i: (0, 0)),
                  pl.BlockSpec((1, ts), lambda i: (0, i)), pl.BlockSpec((AD, 1), lambda i: (0, 0))] + tg.in_specs(),
        out_specs=[pl.BlockSpec((ts, D), lambda i: (i, 0)), tab, tab] + tg.out_specs(),
        out_shape=[jax.ShapeDtypeStruct((S, D), bf16), jax.ShapeDtypeStruct((AD, S), f32),
                   jax.ShapeDtypeStruct((AD, S), f32)] + tg.out_shape(),
        scratch_shapes=tg.scratch(),
        compiler_params=_cp("arbitrary", side_effects=True),
    )(x, w, pos_row, inv_col, *tg.bufs)
    return res[0], res[1], res[2], res[3:]


def prenorm_bwd(x, w, dus, dx1, ts, comm=None):
    S = x.shape[0]
    nt = S // ts
    nd = len(dus)
    cm = _Comm(comm)

    def body(*refs):
        ins, (gx_ref, gw_ref), (acc_ref,), copies = cm.split(refs, nd + 3, 2)
        x_ref, w_ref = ins[:2]
        du_refs, dx1_ref = ins[2:2 + nd], ins[2 + nd]
        i = pl.program_id(0)
        cm.start(copies, i == 0)
        wv = w_ref[...]

        @pl.when(i == 0)
        def _():
            acc_ref[...] = jnp.zeros_like(acc_ref)
            gw_ref[...] = jnp.zeros_like(gw_ref)

        def grp(r0, _):
            rows = pl.ds(r0, NRG)
            xv = x_ref[rows, :]
            r = _rms(xv)
            du = du_refs[0][rows, :].astype(f32)
            for d_ref in du_refs[1:]:
                du = du + d_ref[rows, :].astype(f32)
            gx_ref[rows, :] = dx1_ref[rows, :] + _rms_bwd(xv, r, du * wv)
            acc_ref[...] += _fold(du * xv * r)

        _groups(ts, grp, rg=NRG)

        @pl.when(i == nt - 1)
        def _():
            _flush(acc_ref, gw_ref, 0)

        cm.wait(copies, i == nt - 1)

    row = pl.BlockSpec((ts, D), lambda i: (i, 0))
    res = pl.pallas_call(
        body, name="prenorm_bwd", grid=(nt,),
        in_specs=[row, pl.BlockSpec((1, D), lambda i: (0, 0))] + [row] * (nd + 1) + cm.in_specs(),
        out_specs=[row, pl.BlockSpec((8, D), lambda i: (0, 0))] + cm.out_specs(),
        out_shape=[jax.ShapeDtypeStruct((S, D), f32), jax.ShapeDtypeStruct((8, D), f32)] + cm.out_shape(),
        scratch_shapes=[pltpu.VMEM((8, D), f32)] + cm.scratch(),
        compiler_params=_cp("arbitrary", side_effects=bool(cm.n)),
    )(x, w, *dus, dx1, *cm.bufs)
    return res[0], res[1], res[2:]


def post_fwd(x, mo, w_post, w_pre2, ts):
    S = x.shape[0]

    def body(x_ref, mo_ref, wp_ref, w2_ref, x1_ref, h_ref):
        wp, w2 = wp_ref[...], w2_ref[...]

        def grp(r0, _):
            rows = pl.ds(r0, NRG)
            mv = mo_ref[rows, :].astype(f32)
            x1 = x_ref[rows, :] + mv * _rms(mv) * wp
            x1_ref[rows, :] = x1
            h_ref[rows, :] = (x1 * _rms(x1) * w2).astype(bf16)

        _groups(ts, grp, rg=NRG)

    row = pl.BlockSpec((ts, D), lambda i: (i, 0))
    par = pl.BlockSpec((1, D), lambda i: (0, 0))
    return pl.pallas_call(
        body, name="post_fwd", grid=(S // ts,),
        in_specs=[row, row, par, par], out_specs=[row, row],
        out_shape=[jax.ShapeDtypeStruct((S, D), f32), jax.ShapeDtypeStruct((S, D), bf16)],
        compiler_params=_cp("parallel"),
    )(x, mo, w_post, w_pre2)


def post_bwd(dout, dh, x1, mo, w_post, w_pre2, ts):
    S = x1.shape[0]
    nt = S // ts

    def body(dout_ref, dh_ref, x1_ref, mo_ref, wp_ref, w2_ref, dx1_ref, dmo_ref, gw_ref, acc2_ref, accp_ref):
        i = pl.program_id(0)
        wp, w2 = wp_ref[...], w2_ref[...]

        @pl.when(i == 0)
        def _():
            acc2_ref[...] = jnp.zeros_like(acc2_ref)
            accp_ref[...] = jnp.zeros_like(accp_ref)
            gw_ref[...] = jnp.zeros_like(gw_ref)

        def grp(r0, _):
            rows = pl.ds(r0, NRG)
            x1 = x1_ref[rows, :]
            r1 = _rms(x1)
            dh = dh_ref[rows, :].astype(f32)
            dx1 = dout_ref[rows, :] + _rms_bwd(x1, r1, dh * w2)
            dx1_ref[rows, :] = dx1
            acc2_ref[...] += _fold(dh * x1 * r1)
            mv = mo_ref[rows, :].astype(f32)
            rm = _rms(mv)
            dmo_ref[rows, :] = _rms_bwd(mv, rm, dx1 * wp).astype(bf16)
            accp_ref[...] += _fold(dx1 * mv * rm)

        _groups(ts, grp, rg=NRG)

        @pl.when(i == nt - 1)
        def _():
            _flush(acc2_ref, gw_ref, 0)
            _flush(accp_ref, gw_ref, 1)

    row = pl.BlockSpec((ts, D), lambda i: (i, 0))
    par = pl.BlockSpec((1, D), lambda i: (0, 0))
    return pl.pallas_call(
        body, name="post_bwd", grid=(nt,),
        in_specs=[row, row, row, row, par, par],
        out_specs=[row, row, pl.BlockSpec((8, D), lambda i: (0, 0))],
        out_shape=[jax.ShapeDtypeStruct((S, D), f32), jax.ShapeDtypeStruct((S, D), bf16),
                   jax.ShapeDtypeStruct((8, D), f32)],
        scratch_shapes=[pltpu.VMEM((8, D), f32), pltpu.VMEM((8, D), f32)],
        compiler_params=_cp("arbitrary"),
    )(dout, dh, x1, mo, w_post, w_pre2)


def loss_head(x1, ff, target, w, ts):
    S = x1.shape[0]
    nt = S // ts

    def body(x1_ref, ff_ref, t_ref, w_ref, loss_ref, dout_ref, dff_ref, gw_ref, accw_ref, accl_ref):
        i = pl.program_id(0)
        wv = w_ref[...]

        @pl.when(i == 0)
        def _():
            accw_ref[...] = jnp.zeros_like(accw_ref)
            accl_ref[...] = jnp.zeros_like(accl_ref)
            gw_ref[...] = jnp.zeros_like(gw_ref)

        def grp(r0, _):
            rows = pl.ds(r0, NRG)
            fv = ff_ref[rows, :].astype(f32)
            r = _rms(fv)
            n = fv * r
            e = x1_ref[rows, :] + n * wv - t_ref[rows, :]
            dout = e * (1.0 / D)
            dout_ref[rows, :] = dout
            dff_ref[rows, :] = _rms_bwd(fv, r, dout * wv).astype(bf16)
            accw_ref[...] += _fold(dout * n)
            accl_ref[...] += _fold(e * e)

        _groups(ts, grp, rg=NRG)

        @pl.when(i == nt - 1)
        def _():
            _flush(accw_ref, gw_ref, 0)
            tot = jnp.sum(jnp.sum(accl_ref[...], axis=1, keepdims=True), axis=0, keepdims=True) * (0.5 / D)
            loss_ref[...] = jnp.broadcast_to(tot, loss_ref.shape)

    row = pl.BlockSpec((ts, D), lambda i: (i, 0))
    return pl.pallas_call(
        body, name="loss_head", grid=(nt,),
        in_specs=[row, row, row, pl.BlockSpec((1, D), lambda i: (0, 0))],
        out_specs=[pl.BlockSpec((8, LANES), lambda i: (0, 0)), row, row, pl.BlockSpec((8, D), lambda i: (0, 0))],
        out_shape=[jax.ShapeDtypeStruct((8, LANES), f32), jax.ShapeDtypeStruct((S, D), f32),
                   jax.ShapeDtypeStruct((S, D), bf16), jax.ShapeDtypeStruct((8, D), f32)],
        scratch_shapes=[pltpu.VMEM((8, D), f32), pltpu.VMEM((8, D), f32)],
        compiler_params=_cp("arbitrary"),
    )(x1, ff, target, w)


def _taps(w_ref, cs, K):
    return [jnp.broadcast_to(w_ref[k:k + 1, cs], (8, CW)) for k in range(K)]


def _rolls_down(v, K):
    return tuple(pltpu.roll(v, s, 0) for s in range(1, K))


def _rolls_up(v, K):
    return tuple(pltpu.roll(v, 8 - s, 0) for s in range(1, K))


def _shifted_down(prolls, a, b, K, sub):
    arolls, brolls = _rolls_down(a, K), _rolls_down(b, K)
    out = [(a, b)]
    for s in range(1, K):
        m = sub < s
        out.append((jnp.where(m, prolls[s - 1], arolls[s - 1]), jnp.where(m, arolls[s - 1], brolls[s - 1])))
    return out, brolls


def _shifted_up(a, b, nrolls, K, sub):
    arolls, brolls = _rolls_up(a, K), _rolls_up(b, K)
    out = [(a, b)]
    for s in range(1, K):
        m = sub < 8 - s
        out.append((jnp.where(m, arolls[s - 1], brolls[s - 1]), jnp.where(m, brolls[s - 1], nrolls[s - 1])))
    return out, arolls


def _conv_group(prolls, a, b, taps, bias, K, sub):
    shifted, brolls = _shifted_down(prolls, a, b, K, sub)
    ya, yb = bias, bias
    for k in range(K):
        xa, xb = shifted[K - 1 - k]
        ya = ya + taps[k] * xa
        yb = yb + taps[k] * xb
    return ya, yb, brolls


def _prev8_map(ts, cb):
    return lambda i, j: (jnp.maximum(i * (ts // 8) - 1, 0), cb + j)


def ssdconv_fwd(proj, w8, b, ts):
    S = proj.shape[0]
    bw = 1024
    cb = C_XBC // bw

    def body(cur_ref, prev_ref, w_ref, b_ref, o_ref, c_ref):
        first = pl.program_id(0) == 0
        sub = lax.broadcasted_iota(jnp.int32, (8, CW), 0)
        for c0 in range(0, bw, CW):
            cs = slice(c0, c0 + CW)
            taps = _taps(w_ref, cs, SSD_K)
            bias = jnp.broadcast_to(b_ref[:, cs], (8, CW))

            def grp(r0, prolls, cs=cs, taps=taps, bias=bias):
                rows = pl.ds(r0, RG)
                xv = cur_ref[rows, cs].astype(f32)
                ya, yb, brolls = _conv_group(prolls, xv[0:8], xv[8:16], taps, bias, SSD_K, sub)
                y = jnp.concatenate([ya, yb], axis=0)
                c_ref[rows, cs] = y.astype(bf16)
                o_ref[rows, cs] = (y * _sigmoid_fast(y)).astype(bf16)
                return brolls

            _groups(ts, grp, _rolls_down(jnp.where(first, 0.0, prev_ref[:, cs].astype(f32)), SSD_K))

    o = jax.ShapeDtypeStruct((S, CONVD), bf16)
    blk = pl.BlockSpec((ts, bw), lambda i, j: (i, j))
    return pl.pallas_call(
        body, name="ssdconv_fwd", grid=(S // ts, CONVD // bw),
        in_specs=[pl.BlockSpec((ts, bw), lambda i, j: (i, cb + j)),
                  pl.BlockSpec((8, bw), _prev8_map(ts, cb)),
                  pl.BlockSpec((8, bw), lambda i, j: (0, j)),
                  pl.BlockSpec((1, bw), lambda i, j: (0, j))],
        out_specs=[blk, blk], out_shape=[o, o],
        compiler_params=_cp("parallel", "parallel"),
    )(proj, proj, w8, b)


def _gelu_tanh(x):
    c = 0.7978845608028654
    t = jnp.tanh(c * (x + 0.044715 * x * x * x))
    return 0.5 * x * (1.0 + t), t


def ffnact_fwd(up, w8, b, ts):
    S = up.shape[0]

    def body(g_ref, gp_ref, v_ref, vp_ref, wg_ref, wv_ref, bg_ref, bv_ref, o_ref, gc_ref, vc_ref):
        first = pl.program_id(0) == 0
        sub = lax.broadcasted_iota(jnp.int32, (8, CW), 0)
        for c0 in range(0, FF, CW):
            cs = slice(c0, c0 + CW)
            tg, tv = _taps(wg_ref, cs, FFN_K), _taps(wv_ref, cs, FFN_K)
            bg = jnp.broadcast_to(bg_ref[:, cs], (8, CW))
            bv = jnp.broadcast_to(bv_ref[:, cs], (8, CW))

            def grp(r0, carry, cs=cs, tg=tg, tv=tv, bg=bg, bv=bv):
                pg, pv = carry
                rows = pl.ds(r0, RG)
                gx = g_ref[rows, cs].astype(f32)
                vx = v_ref[rows, cs].astype(f32)
                ga, gb, pg = _conv_group(pg, gx[0:8], gx[8:16], tg, bg, FFN_K, sub)
                va, vb, pv = _conv_group(pv, vx[0:8], vx[8:16], tv, bv, FFN_K, sub)
                g = jnp.concatenate([ga, gb], axis=0)
                v = jnp.concatenate([va, vb], axis=0)
                gc_ref[rows, cs] = g.astype(bf16)
                vc_ref[rows, cs] = v.astype(bf16)
                o_ref[rows, cs] = (_gelu_tanh(g)[0] * v).astype(bf16)
                return pg, pv

            _groups(ts, grp, (_rolls_down(jnp.where(first, 0.0, gp_ref[:, cs].astype(f32)), FFN_K),
                              _rolls_down(jnp.where(first, 0.0, vp_ref[:, cs].astype(f32)), FFN_K)))

    o = jax.ShapeDtypeStruct((S, FF), bf16)
    blk = pl.BlockSpec((ts, FF), lambda i: (i, 0))
    prev = lambda cb: pl.BlockSpec((8, FF), lambda i: (jnp.maximum(i * (ts // 8) - 1, 0), cb))
    return pl.pallas_call(
        body, name="ffnact_fwd", grid=(S // ts,),
        in_specs=[blk, prev(0), pl.BlockSpec((ts, FF), lambda i: (i, 1)), prev(1),
                  pl.BlockSpec((8, FF), lambda i: (0, 0)), pl.BlockSpec((8, FF), lambda i: (0, 1)),
                  pl.BlockSpec((1, FF), lambda i: (0, 0)), pl.BlockSpec((1, FF), lambda i: (0, 1))],
        out_specs=[blk, blk, blk], out_shape=[o, o, o],
        compiler_params=_cp("parallel"),
    )(up, up, up, up, w8, w8, b, b)


def ffnact_bwd(dact, gc, vc, ts):
    S = dact.shape[0]

    def body(d_ref, g_ref, v_ref, dg_ref, dv_ref):
        c = 0.7978845608028654
        for c0 in range(0, FF, CW):
            cs = slice(c0, c0 + CW)

            def grp(r0, _, cs=cs):
                rows = pl.ds(r0, RG)
                d = d_ref[rows, cs].astype(f32)
                g = g_ref[rows, cs].astype(f32)
                ge, t = _gelu_tanh(g)
                dgelu = 0.5 * (1.0 + t) + 0.5 * g * (1.0 - t * t) * c * (1.0 + 3.0 * 0.044715 * g * g)
                dg_ref[rows, cs] = (d * v_ref[rows, cs].astype(f32) * dgelu).astype(bf16)
                dv_ref[rows, cs] = (d * ge).astype(bf16)

            _groups(ts, grp)

    o = jax.ShapeDtypeStruct((S, FF), bf16)
    blk = pl.BlockSpec((ts, FF), lambda i: (i, 0))
    return pl.pallas_call(
        body, name="ffnact_bwd", grid=(S // ts,),
        in_specs=[blk, blk, blk], out_specs=[blk, blk], out_shape=[o, o],
        compiler_params=_cp("parallel"),
    )(dact, gc, vc)


def dwconv_bwd(dy, x, xcb, w8, wcb, K, bw, ts, name, act_c=None, into=None, ocb=0, out_cols=None):
    S, C = dy.shape
    nr = S // ts
    out_cols = out_cols or C
    n_act = 0 if act_c is None else 2

    def body(*refs):
        dy_ref, dyn_ref = refs[0:2]
        c_ref, cn_ref = (refs[2:4] if n_act else (None, None))
        x_ref, xp_ref, w_ref = refs[2 + n_act:5 + n_act]
        dx_ref, dw_ref, sd_ref = refs[-3:]
        i = pl.program_id(1)
        first, last = i == 0, i == nr - 1
        sub = lax.broadcasted_iota(jnp.int32, (8, CW), 0)

        def grad_y(d, c):
            if c is None:
                return d.astype(f32)
            cv = c.astype(f32)
            s = _sigmoid_fast(cv)
            return d.astype(f32) * s * (1.0 + cv * (1.0 - s))

        @pl.when(first)
        def _():
            dw_ref[...] = jnp.zeros_like(dw_ref)

        for c0 in range(0, bw, CW):
            cs = slice(c0, c0 + CW)
            taps = _taps(w_ref, cs, K)
            zero = jnp.zeros((8, CW), f32)

            def fwd(r0, carry, cs=cs):
                prolls, accs, accb = carry
                rows = pl.ds(r0, RG)
                g = grad_y(dy_ref[rows, cs], c_ref[rows, cs] if n_act else None)
                sd_ref[rows, cs] = g
                xv = x_ref[rows, cs].astype(f32)
                ga, gb = g[0:8], g[8:16]
                shifted, brolls = _shifted_down(prolls, xv[0:8], xv[8:16], K, sub)
                new = []
                for k in range(K):
                    xa, xb = shifted[K - 1 - k]
                    new.append(accs[k] + ga * xa + gb * xb)
                return brolls, tuple(new), accb + ga + gb

            before = jnp.where(first, 0.0, xp_ref[:, cs].astype(f32))
            _, accs, accb = _groups(ts, fwd, (_rolls_down(before, K), (zero,) * K, zero))
            for k in range(K):
                dw_ref[k:k + 1, cs] += jnp.sum(accs[k], axis=0, keepdims=True)
            dw_ref[7:8, cs] += jnp.sum(accb, axis=0, keepdims=True)

            def bwd(r0, nrolls, cs=cs, taps=taps):
                rows = pl.ds(r0, RG)
                g = sd_ref[rows, cs]
                shifted, arolls = _shifted_up(g[0:8], g[8:16], nrolls, K, sub)
                da, db = zero, zero
                for k in range(K):
                    ua, ub = shifted[K - 1 - k]
                    da = da + taps[k] * ua
                    db = db + taps[k] * ub
                dx_ref[rows, cs] = jnp.concatenate([da, db], axis=0).astype(bf16)
                return arolls

            halo = grad_y(dyn_ref[:, cs], cn_ref[:, cs] if n_act else None)
            _groups(ts, bwd, _rolls_up(jnp.where(last, 0.0, halo), K), reverse=True)

    nxt = lambda j, i: (jnp.minimum((i + 1) * (ts // 8), S // 8 - 1), j)
    tile = pl.BlockSpec((ts, bw), lambda j, i: (i, j))
    acts = [] if act_c is None else [act_c, act_c]
    extra = [] if into is None else [into]
    n_in = 5 + n_act
    return pl.pallas_call(
        body, name=name, grid=(C // bw, nr),
        in_specs=[tile, pl.BlockSpec((8, bw), nxt)] + ([tile, pl.BlockSpec((8, bw), nxt)] if n_act else []) + [
            pl.BlockSpec((ts, bw), lambda j, i: (i, xcb + j)),
            pl.BlockSpec((8, bw), lambda j, i: (jnp.maximum(i * (ts // 8) - 1, 0), xcb + j)),
            pl.BlockSpec((8, bw), lambda j, i: (0, wcb + j))] + [pl.BlockSpec(memory_space=pl.ANY)] * len(extra),
        out_specs=[pl.BlockSpec((ts, bw), lambda j, i: (i, ocb + j)), pl.BlockSpec((8, bw), lambda j, i: (0, j))],
        out_shape=[jax.ShapeDtypeStruct((S, out_cols), bf16), jax.ShapeDtypeStruct((8, C), f32)],
        scratch_shapes=[pltpu.VMEM((ts, bw), f32)],
        input_output_aliases={n_in: 0} if extra else {},
        compiler_params=_cp("parallel", "arbitrary"),
    )(dy, dy, *acts, x, x, w8, *extra)


def gnorm_fwd(y, proj, w, ts):
    S = y.shape[0]

    def body(y_ref, z_ref, w_ref, o_ref):
        for k in range(NG):
            sl = slice(k * GW, (k + 1) * GW)
            wv = w_ref[:, sl]

            def grp(r0, _, sl=sl, wv=wv):
                rows = pl.ds(r0, NRG)
                z = z_ref[rows, sl].astype(f32)
                g = y_ref[rows, sl].astype(f32) * z * _sigmoid_fast(z)
                o_ref[rows, sl] = (g * _rms(g) * wv).astype(bf16)

            _groups(ts, grp, rg=NRG)

    row = pl.BlockSpec((ts, DI), lambda i: (i, 0))
    return pl.pallas_call(
        body, name="gnorm_fwd", grid=(S // ts,),
        in_specs=[row, row, pl.BlockSpec((1, DI), lambda i: (0, 0))],
        out_specs=row, out_shape=jax.ShapeDtypeStruct((S, DI), bf16),
        compiler_params=_cp("parallel"),
    )(y, proj, w)


def gnorm_bwd(dyn, y, proj, w, dproj, ts):
    S = y.shape[0]
    nt = S // ts

    def body(d_ref, y_ref, z_ref, w_ref, _, dy_ref, dz_ref, gw_ref, acc_ref):
        i = pl.program_id(0)

        @pl.when(i == 0)
        def _():
            acc_ref[...] = jnp.zeros_like(acc_ref)
            gw_ref[...] = jnp.zeros_like(gw_ref)

        for k in range(NG):
            sl = slice(k * GW, (k + 1) * GW)
            wv = w_ref[:, sl]

            def grp(r0, _, sl=sl, wv=wv):
                rows = pl.ds(r0, NRG)
                z = z_ref[rows, sl].astype(f32)
                yv = y_ref[rows, sl].astype(f32)
                s = _sigmoid_fast(z)
                sz = z * s
                g = yv * sz
                r = _rms(g)
                d = d_ref[rows, sl].astype(f32)
                acc_ref[:, sl] += _fold(d * g * r)
                dg = _rms_bwd(g, r, d * wv)
                dy_ref[rows, sl] = (dg * sz).astype(bf16)
                dz_ref[rows, sl] = (dg * yv * s * (1.0 + z * (1.0 - s))).astype(bf16)

            _groups(ts, grp, rg=NRG)

        @pl.when(i == nt - 1)
        def _():
            _flush(acc_ref, gw_ref, 0)

    row = pl.BlockSpec((ts, DI), lambda i: (i, 0))
    return pl.pallas_call(
        body, name="gnorm_bwd", grid=(nt,),
        in_specs=[row, row, row, pl.BlockSpec((1, DI), lambda i: (0, 0)), pl.BlockSpec(memory_space=pl.ANY)],
        out_specs=[row, row, pl.BlockSpec((8, DI), lambda i: (0, 0))],
        out_shape=[jax.ShapeDtypeStruct((S, DI), bf16), jax.ShapeDtypeStruct(dproj.shape, bf16),
                   jax.ShapeDtypeStruct((8, DI), f32)],
        scratch_shapes=[pltpu.VMEM((8, DI), f32)],
        input_output_aliases={4: 1},
        compiler_params=_cp("arbitrary"),
    )(dyn, y, proj, w, dproj)


def merge_fwd(proj, ys, ya, ts):
    S = ys.shape[0]

    def body(gs_ref, ga_ref, ys_ref, ya_ref, o_ref):
        for c0 in range(0, D, CW):
            cs = slice(c0, c0 + CW)

            def grp(r0, _, cs=cs):
                rows = pl.ds(r0, NRG)
                o_ref[rows, cs] = (_sigmoid_fast(gs_ref[rows, cs].astype(f32)) * ys_ref[rows, cs].astype(f32)
                                   + _sigmoid_fast(ga_ref[rows, cs].astype(f32)) * ya_ref[rows, cs].astype(f32)
                                   ).astype(bf16)

            _groups(ts, grp, rg=NRG)

    row = pl.BlockSpec((ts, D), lambda i: (i, 0))
    return pl.pallas_call(
        body, name="merge_fwd", grid=(S // ts,),
        in_specs=[pl.BlockSpec((ts, D), lambda i: (i, C_GS // D)), pl.BlockSpec((ts, D), lambda i: (i, C_GA // D)), row, row],
        out_specs=row, out_shape=jax.ShapeDtypeStruct((S, D), bf16),
        compiler_params=_cp("parallel"),
    )(proj, proj, ys, ya)


def merge_bwd(dm, proj, ys, ya, ts):
    S = ys.shape[0]

    def body(d_ref, gs_ref, ga_ref, ys_ref, ya_ref, dys_ref, dya_ref, dg_ref):
        for c0 in range(0, D, CW):
            cs = slice(c0, c0 + CW)

            def grp(r0, _, c0=c0, cs=cs):
                rows = pl.ds(r0, NRG)
                d = d_ref[rows, cs].astype(f32)
                ss = _sigmoid_fast(gs_ref[rows, cs].astype(f32))
                sa = _sigmoid_fast(ga_ref[rows, cs].astype(f32))
                dys_ref[rows, cs] = (d * ss).astype(bf16)
                dya_ref[rows, cs] = (d * sa).astype(bf16)
                dg_ref[rows, cs] = (d * ys_ref[rows, cs].astype(f32) * ss * (1.0 - ss)).astype(bf16)
                dg_ref[rows, D + c0:D + c0 + CW] = (d * ya_ref[rows, cs].astype(f32) * sa * (1.0 - sa)).astype(bf16)

            _groups(ts, grp, rg=NRG)

    row = pl.BlockSpec((ts, D), lambda i: (i, 0))
    o = jax.ShapeDtypeStruct((S, D), bf16)
    return pl.pallas_call(
        body, name="merge_bwd", grid=(S // ts,),
        in_specs=[row, pl.BlockSpec((ts, D), lambda i: (i, C_GS // D)), pl.BlockSpec((ts, D), lambda i: (i, C_GA // D)), row, row],
        out_specs=[row, row, pl.BlockSpec((ts, 2 * D), lambda i: (i, C_GS // (2 * D)))],
        out_shape=[o, o, jax.ShapeDtypeStruct((S, PM), bf16)],
        compiler_params=_cp("parallel"),
    )(dm, proj, proj, ys, ya)


def _ssd_consts():
    h = lax.broadcasted_iota(jnp.int32, (LANES, DI), 0)
    c = lax.broadcasted_iota(jnp.int32, (LANES, DI), 1)
    expand = (c // HD == h).astype(bf16)
    r = lax.broadcasted_iota(jnp.int32, (CH, CH), 0)
    cc = lax.broadcasted_iota(jnp.int32, (CH, CH), 1)
    tril = (cc <= r).astype(bf16)
    triu = (cc >= r).astype(bf16)
    return expand, expand.T, tril, triu


def _ssd_common(xbc_ref, dtr_ref, bias_ref, alog_ref, tril_ref, expand_ref=None, saved=None):
    dtr = dtr_ref[...] + bias_ref[...]
    dt = jnp.maximum(dtr, 0.0) + jnp.log1p(jnp.exp(-jnp.abs(dtr)))
    a = -jnp.exp(alog_ref[...])
    acs = _dot3_left(tril_ref[...], dt * a)
    if saved is None:
        acsx = _dot3_right(acs, expand_ref[...])
        dtx = _dot3_right(dt, expand_ref[...])
    else:
        acsx, dtx = saved[0][...], saved[1][...]
    x = xbc_ref[:, 0:DI].astype(f32)
    xdt = x * dtx
    e = jnp.exp(acsx)
    dsx = jnp.exp(acsx[CH - 1:CH, :] - acsx)
    return dtr, dt, a, acs, acsx, dtx, x, xdt, e, dsx


def _head_halves():
    first = lax.broadcasted_iota(jnp.int32, (CH, LANES), 1) < HD
    return first, jnp.logical_not(first)


def _ssd_lmat(acs, acs_t, hh, causal):
    seg = acs[:, hh:hh + 1] - acs_t[hh:hh + 1, :]
    return jnp.where(causal, jnp.exp(jnp.minimum(seg, 0.0)), 0.0)


def ssd_fwd(xbc, dtr, bias, alog, dx_row, comm=None):
    S = xbc.shape[0]
    nc = S // CH
    expand, _, tril, _ = _ssd_consts()
    cm = _Comm(comm)

    def body(*refs):
        ins, (y_ref, hp_ref, ax_ref, dtx_ref), (h_ref, yd_ref), copies = cm.split(refs, 7, 4)
        xbc_ref, dtr_ref, bias_ref, alog_ref, dxr_ref, expand_ref, tril_ref = ins
        c = pl.program_id(0)
        cm.start(copies, c == 0)

        @pl.when(c == 0)
        def _():
            h_ref[...] = jnp.zeros_like(h_ref)

        _, _, _, acs, acsx, dtx, x, xdt, e, dsx = _ssd_common(xbc_ref, dtr_ref, bias_ref, alog_ref, tril_ref,
                                                              expand_ref=expand_ref)
        ax_ref[...] = acsx
        dtx_ref[...] = dtx
        acs_t = acs.T
        xb = xdt.astype(bf16)
        xd = (xdt * dsx).astype(bf16)
        causal = tril_ref[...] > 0
        halves = _head_halves()
        for g in range(NG):
            gs = slice(g * GW, (g + 1) * GW)
            bg = xbc_ref[:, DI + g * NS:DI + (g + 1) * NS]
            cg = xbc_ref[:, DI + NG * NS + g * NS:DI + NG * NS + (g + 1) * NS]
            cb = _dot(cg, bg, "nt")
            hp = h_ref[g]
            hpb = hp.astype(bf16)
            hp_ref[0, g] = hpb
            yd_ref[:, gs] = _dot(cg, hpb) * e[:, gs]
            h_ref[g] = hp * e[CH - 1:CH, gs] + _dot(bg, xd[:, gs], "tn")
            for k in range(NH // NG // 2):
                hh = g * (NH // NG) + 2 * k
                ps = slice(hh * HD, (hh + 2) * HD)
                xp = xb[:, ps]
                acc = None
                for o in range(2):
                    m = (cb * _ssd_lmat(acs, acs_t, hh + o, causal)).astype(bf16)
                    part = _dot(m, jnp.where(halves[o], xp, jnp.zeros_like(xp)))
                    acc = part if acc is None else acc + part
                yd_ref[:, ps] += acc
        y_ref[...] = (yd_ref[...] + dxr_ref[...] * x).astype(bf16)
        cm.wait(copies, c == nc - 1)

    par = lambda shape: pl.BlockSpec(shape, lambda c: (0,) * len(shape))
    res = pl.pallas_call(
        body, name="ssd_fwd", grid=(nc,),
        in_specs=[pl.BlockSpec((CH, CONVD), lambda c: (c, 0)), pl.BlockSpec((CH, LANES), lambda c: (c, 0)),
                  par((1, LANES)), par((1, LANES)), par((1, DI)), par((LANES, DI)), par((CH, CH))] + cm.in_specs(),
        out_specs=[pl.BlockSpec((CH, DI), lambda c: (c, 0)), pl.BlockSpec((1, NG, NS, GW), lambda c: (c, 0, 0, 0)),
                   pl.BlockSpec((CH, DI), lambda c: (c, 0)), pl.BlockSpec((CH, DI), lambda c: (c, 0))] + cm.out_specs(),
        out_shape=[jax.ShapeDtypeStruct((S, DI), bf16), jax.ShapeDtypeStruct((nc, NG, NS, GW), bf16),
                   jax.ShapeDtypeStruct((S, DI), f32), jax.ShapeDtypeStruct((S, DI), f32)] + cm.out_shape(),
        scratch_shapes=[pltpu.VMEM((NG, NS, GW), f32), pltpu.VMEM((CH, DI), f32)] + cm.scratch(),
        compiler_params=_cp("arbitrary", side_effects=bool(cm.n)),
    )(xbc, dtr, bias, alog, dx_row, expand, tril, *cm.bufs)
    return res[0], res[1], (res[2], res[3]), res[4:]


def ssd_bwd(xbc, dtr, dy, hprev, saved, bias, alog, dx_row, comm=None):
    S = xbc.shape[0]
    nc = S // CH
    _, expand_t, tril, triu = _ssd_consts()
    cm = _Comm(comm)

    def body(*refs):
        ins, outs, scr, copies = cm.split(refs, 12, 3)
        xbc_ref, dtr_ref, dy_ref, hp_ref, ax_ref, dtx_ref, bias_ref, alog_ref, dxr_ref, expt_ref, tril_ref, triu_ref = ins
        dxbc_ref, ddtr_ref, acc_ref = outs
        dh_ref, dxs_ref, t_ref, accb_ref, acca_ref, accd_ref, cc_ref, rr_ref = scr
        c = pl.program_id(0)
        cm.start(copies, c == 0)

        @pl.when(c == 0)
        def _():
            cc_ref[...] = jnp.zeros_like(cc_ref)
            rr_ref[...] = jnp.zeros_like(rr_ref)

        @pl.when(c == 0)
        def _():
            dh_ref[...] = jnp.zeros_like(dh_ref)
            accb_ref[...] = jnp.zeros_like(accb_ref)
            acca_ref[...] = jnp.zeros_like(acca_ref)
            accd_ref[...] = jnp.zeros_like(accd_ref)

        dtr, dt, a, acs, _, dtx, x, xdt, e, dsx = _ssd_common(xbc_ref, dtr_ref, bias_ref, alog_ref, tril_ref,
                                                              saved=(ax_ref, dtx_ref))
        acs_t = acs.T
        xb = xdt.astype(bf16)
        xdf = xdt * dsx
        xd = xdf.astype(bf16)
        dyv = dy_ref[...].astype(f32)
        dyb = dy_ref[...]
        dye = (dyv * e).astype(bf16)
        causal = tril_ref[...] > 0
        halves = _head_halves()
        last_row = lax.broadcasted_iota(jnp.int32, (CH, 1), 0) == CH - 1
        for g in range(NG):
            gs = slice(g * GW, (g + 1) * GW)
            bsl = slice(DI + g * NS, DI + (g + 1) * NS)
            csl = slice(DI + NG * NS + g * NS, DI + NG * NS + (g + 1) * NS)
            bg = xbc_ref[:, bsl]
            cg = xbc_ref[:, csl]
            cb = _dot(cg, bg, "nt")
            hpb = hp_ref[0, g]
            dhn = dh_ref[g]
            dhnb = dhn.astype(bf16)
            yoff = _dot(cg, hpb) * e[:, gs]
            dxd = _dot(bg, dhnb)
            t2 = dxd * xdf[:, gs]
            t3 = jnp.sum(dhn * hpb.astype(f32), axis=0, keepdims=True) * e[CH - 1:CH, gs]
            t_ref[:, gs] = dyv[:, gs] * yoff - t2 + jnp.where(last_row, jnp.sum(t2, axis=0, keepdims=True) + t3, 0.0)
            dxs_ref[:, gs] = dxd * dsx[:, gs]
            dcg = _dot(dye[:, gs], hpb, "nt")
            dbg = _dot(xd[:, gs], dhnb, "nt")
            dh_ref[g] = dhn * e[CH - 1:CH, gs] + _dot(cg, dye[:, gs], "tn")
            dcb = jnp.zeros((CH, CH), f32)
            for k in range(NH // NG // 2):
                hh0 = g * (NH // NG) + 2 * k
                ps = slice(hh0 * HD, (hh0 + 2) * HD)
                xp = xb[:, ps]
                dyp = dyb[:, ps]
                acc = None
                for o in range(2):
                    hh = hh0 + o
                    dyh = jnp.where(halves[o], dyp, jnp.zeros_like(dyp))
                    lm = _ssd_lmat(acs, acs_t, hh, causal)
                    m = cb * lm
                    dm = _dot(dyh, xp, "nt")
                    gm = dm * m
                    cc_ref[:, hh:hh + 1] = jnp.sum(gm, axis=1, keepdims=True)
                    rr_ref[hh:hh + 1, :] = jnp.sum(gm, axis=0, keepdims=True)
                    dcb = dcb + dm * lm
                    part = _dot(m.astype(bf16), dyh, "tn")
                    acc = part if acc is None else acc + part
                dxs_ref[:, ps] += acc
            dcbb = dcb.astype(bf16)
            dxbc_ref[:, csl] = (dcg + _dot(dcbb, bg)).astype(bf16)
            dxbc_ref[:, bsl] = (dbg + _dot(dcbb, cg, "tn")).astype(bf16)
        dxf = dxs_ref[...]
        dxbc_ref[:, 0:DI] = (dxf * dtx + dxr_ref[...] * dyv).astype(bf16)
        expt = expt_ref[...]
        dacs = cc_ref[...] - rr_ref[...].T + _dot2_right(t_ref[...], expt)
        dadt = _dot3_left(triu_ref[...], dacs)
        ddt = _dot2_right(dxf * x, expt) + dadt * a
        ddtr = ddt * _sigmoid(dtr)
        ddtr_ref[...] = ddtr
        accb_ref[...] += ddtr
        acca_ref[...] += dadt * dt
        accd_ref[...] += _dot2_right(dyv * x, expt)

        @pl.when(c == nc - 1)
        def _():
            acc_ref[...] = jnp.zeros_like(acc_ref)
            acc_ref[0:1, :] = jnp.sum(accb_ref[...], axis=0, keepdims=True)
            acc_ref[1:2, :] = jnp.sum(acca_ref[...], axis=0, keepdims=True) * a
            acc_ref[2:3, :] = jnp.sum(accd_ref[...], axis=0, keepdims=True)

        cm.wait(copies, c == nc - 1)

    par = lambda shape: pl.BlockSpec(shape, lambda c: (0,) * len(shape))
    rev = lambda c: (nc - 1 - c, 0)
    res = pl.pallas_call(
        body, name="ssd_bwd", grid=(nc,),
        in_specs=[pl.BlockSpec((CH, CONVD), rev), pl.BlockSpec((CH, LANES), rev), pl.BlockSpec((CH, DI), rev),
                  pl.BlockSpec((1, NG, NS, GW), lambda c: (nc - 1 - c, 0, 0, 0)),
                  pl.BlockSpec((CH, DI), rev), pl.BlockSpec((CH, DI), rev),
                  par((1, LANES)), par((1, LANES)), par((1, DI)), par((DI, LANES)),
                  par((CH, CH)), par((CH, CH))] + cm.in_specs(),
        out_specs=[pl.BlockSpec((CH, CONVD), rev), pl.BlockSpec((CH, LANES), rev), par((8, LANES))] + cm.out_specs(),
        out_shape=[jax.ShapeDtypeStruct((S, CONVD), bf16), jax.ShapeDtypeStruct((S, LANES), f32),
                   jax.ShapeDtypeStruct((8, LANES), f32)] + cm.out_shape(),
        scratch_shapes=[pltpu.VMEM((NG, NS, GW), f32), pltpu.VMEM((CH, DI), f32), pltpu.VMEM((CH, DI), f32),
                        pltpu.VMEM((CH, LANES), f32), pltpu.VMEM((CH, LANES), f32), pltpu.VMEM((CH, LANES), f32),
                        pltpu.VMEM((CH, LANES), f32), pltpu.VMEM((LANES, CH), f32)] + cm.scratch(),
        compiler_params=_cp("arbitrary", side_effects=bool(cm.n)),
    )(xbc, dtr, dy, hprev, *saved, bias, alog, dx_row, expand_t, tril, triu, *cm.bufs)
    return res[0], res[1], res[2], res[3:]


def _partner(t):
    half = AD // 2
    return jnp.concatenate([t[h * AD + o:h * AD + o + half] for h in range(t.shape[0] // AD) for o in (half, 0)], axis=0)


def _rope(t, cos, sin):
    reps = t.shape[0] // AD
    return t * jnp.tile(cos, (reps, 1)) + _partner(t) * jnp.tile(sin, (reps, 1))


def _rope_t(d, cos, sin):
    reps = d.shape[0] // AD
    return d * jnp.tile(cos, (reps, 1)) - _partner(d) * jnp.tile(sin, (reps, 1))


def _lanes_of_group(t, g):
    return jnp.concatenate([t[(g * REP + r) * AD:(g * REP + r + 1) * AD] for r in range(REP)], axis=1)


def _attn_probs(qg, k2, sink_ref, g, not_first):
    n = qg.shape[1]
    s = lax.broadcasted_iota(jnp.int32, (2 * WIN, n), 0)
    t = lax.broadcasted_iota(jnp.int32, (2 * WIN, n), 1) % WIN
    valid = jnp.logical_or(jnp.logical_and(jnp.logical_and(s < WIN, s > t), not_first),
                           jnp.logical_and(s >= WIN, s - WIN <= t))
    sink = jnp.concatenate([jnp.broadcast_to(sink_ref[0:1, g * REP + r:g * REP + r + 1], (1, WIN)) for r in range(REP)],
                           axis=1)
    sc = jnp.where(valid, _dot(k2, qg, "tn"), -1e30)
    m = jnp.maximum(jnp.max(sc, axis=0, keepdims=True), sink)
    p = jnp.exp(sc - m)
    ps = jnp.exp(sink - m)
    inv = 1.0 / (jnp.sum(p, axis=0, keepdims=True) + ps)
    return p * inv, ps * inv


def attn_fwd(qt, kvt, cos, sin, sinks):
    S = qt.shape[1]
    nb = S // WIN
    cur = lambda i: (0, i)
    prev = lambda i: (0, jnp.maximum(2 * i - 1, 0))

    def body(q_ref, kv_ref, kvp_ref, cos_ref, sin_ref, cosp_ref, sinp_ref, sink_ref, o_ref):
        i = pl.program_id(0)
        q = (_rope(q_ref[...].astype(f32), cos_ref[...], sin_ref[...]) * (AD ** -0.5)).astype(bf16)
        kc = _rope(kv_ref[0:KVW, :].astype(f32), cos_ref[...], sin_ref[...]).astype(bf16)
        kp = _rope(kvp_ref[0:KVW, :].astype(f32), cosp_ref[...], sinp_ref[...]).astype(bf16)
        k3 = jnp.concatenate([kp, kc], axis=1)
        for g in range(KVH):
            ks = slice(g * AD, (g + 1) * AD)
            vs = slice(KVW + g * AD, KVW + (g + 1) * AD)
            v3 = jnp.concatenate([kvp_ref[vs, :], kv_ref[vs, :]], axis=1)
            for b in range(2):
                win = slice(b * WIN, (b + 2) * WIN)
                qg = _lanes_of_group(q[:, b * WIN:(b + 1) * WIN], g)
                p, _ = _attn_probs(qg, k3[ks, win], sink_ref, g, jnp.logical_or(i > 0, b > 0))
                o = _dot(v3[:, win], p.astype(bf16))
                for r in range(REP):
                    h = g * REP + r
                    o_ref[h * AD:(h + 1) * AD, b * WIN:(b + 1) * WIN] = o[:, r * WIN:(r + 1) * WIN].astype(bf16)

    tab = pl.BlockSpec((AD, 2 * WIN), cur)
    tabp = pl.BlockSpec((AD, WIN), prev)
    return pl.pallas_call(
        body, name="attn_fwd", grid=(nb // 2,),
        in_specs=[pl.BlockSpec((D, 2 * WIN), cur), pl.BlockSpec((2 * KVW, 2 * WIN), cur),
                  pl.BlockSpec((2 * KVW, WIN), prev), tab, tab, tabp, tabp, pl.BlockSpec((1, LANES), lambda i: (0, 0))],
        out_specs=pl.BlockSpec((D, 2 * WIN), cur),
        out_shape=jax.ShapeDtypeStruct((D, S), bf16),
        compiler_params=_cp("parallel"),
    )(qt, kvt, kvt, cos, sin, cos, sin, sinks)


def attn_bwd(qt, kvt, cos, sin, sinks, daot, comm=None):
    S = qt.shape[1]
    nb = S // WIN
    cur = lambda i: (0, jnp.minimum(i, nb - 1))
    prev = lambda i: (0, jnp.maximum(i - 1, 0))
    cm = _Comm(comm)

    def body(*refs):
        ins, (dq_ref, dkv_ref, ds_ref), scr, copies = cm.split(refs, 9, 3)
        q_ref, kv_ref, kvp_ref, cos_ref, sin_ref, cosp_ref, sinp_ref, sink_ref, do_ref = ins
        ck_ref, cv_ref, dqs_ref, dkp_ref, dvp_ref, dkc_ref, dvc_ref, accs_ref = scr
        i = pl.program_id(0)
        cm.start(copies, i == 0)

        @pl.when(i == 0)
        def _():
            ck_ref[...] = jnp.zeros_like(ck_ref)
            cv_ref[...] = jnp.zeros_like(cv_ref)
            accs_ref[...] = jnp.zeros_like(accs_ref)

        @pl.when(i == nb)
        def _():
            dkp_ref[...] = jnp.zeros_like(dkp_ref)
            dvp_ref[...] = jnp.zeros_like(dvp_ref)

        @pl.when(i < nb)
        def _():
            q = (_rope(q_ref[...].astype(f32), cos_ref[...], sin_ref[...]) * (AD ** -0.5)).astype(bf16)
            kc = _rope(kv_ref[0:KVW, :].astype(f32), cos_ref[...], sin_ref[...]).astype(bf16)
            kp = _rope(kvp_ref[0:KVW, :].astype(f32), cosp_ref[...], sinp_ref[...]).astype(bf16)
            do = do_ref[...]
            for g in range(KVH):
                ks = slice(g * AD, (g + 1) * AD)
                vs = slice(KVW + g * AD, KVW + (g + 1) * AD)
                qg = _lanes_of_group(q, g)
                dog = _lanes_of_group(do, g)
                k2 = jnp.concatenate([kp[ks], kc[ks]], axis=1)
                v2 = jnp.concatenate([kvp_ref[vs, :], kv_ref[vs, :]], axis=1)
                p, ps = _attn_probs(qg, k2, sink_ref, g, i > 0)
                dp = _dot(v2, dog, "tn")
                delta = jnp.sum(p * dp, axis=0, keepdims=True)
                ds = (p * (dp - delta)).astype(bf16)
                accs_ref[g:g + 1, :] -= ps * delta
                dqg = _dot(k2, ds) * (AD ** -0.5)
                for r in range(REP):
                    h = g * REP + r
                    dqs_ref[h * AD:(h + 1) * AD, :] = dqg[:, r * WIN:(r + 1) * WIN]
                dk2 = _dot(qg, ds, "nt")
                dv2 = _dot(dog, p.astype(bf16), "nt")
                dkp_ref[ks, :] = dk2[:, 0:WIN]
                dkc_ref[ks, :] = dk2[:, WIN:2 * WIN]
                dvp_ref[ks, :] = dv2[:, 0:WIN]
                dvc_ref[ks, :] = dv2[:, WIN:2 * WIN]
            dq_ref[...] = _rope_t(dqs_ref[...], cos_ref[...], sin_ref[...]).astype(bf16)

        dkv_ref[0:KVW, :] = _rope_t(ck_ref[...] + dkp_ref[...], cosp_ref[...], sinp_ref[...]).astype(bf16)
        dkv_ref[KVW:2 * KVW, :] = (cv_ref[...] + dvp_ref[...]).astype(bf16)

        @pl.when(i < nb)
        def _():
            ck_ref[...] = dkc_ref[...]
            cv_ref[...] = dvc_ref[...]

        @pl.when(i == nb)
        def _():
            lane = lax.broadcasted_iota(jnp.int32, (1, LANES), 1)
            row = jnp.zeros((1, LANES), f32)
            for h in range(AH):
                part = accs_ref[h // REP:h // REP + 1, (h % REP) * WIN:(h % REP + 1) * WIN]
                row = row + jnp.where(lane == h, jnp.sum(part, axis=1, keepdims=True), 0.0)
            ds_ref[...] = jnp.zeros_like(ds_ref)
            ds_ref[0:1, :] = row

        cm.wait(copies, i == nb)

    tab = pl.BlockSpec((AD, WIN), cur)
    tabp = pl.BlockSpec((AD, WIN), prev)
    kvs = lambda: pltpu.VMEM((KVW, WIN), f32)
    res = pl.pallas_call(
        body, name="attn_bwd", grid=(nb + 1,),
        in_specs=[pl.BlockSpec((D, WIN), cur), pl.BlockSpec((2 * KVW, WIN), cur), pl.BlockSpec((2 * KVW, WIN), prev),
                  tab, tab, tabp, tabp, pl.BlockSpec((1, LANES), lambda i: (0, 0)),
                  pl.BlockSpec((D, WIN), cur)] + cm.in_specs(),
        out_specs=[pl.BlockSpec((D, WIN), cur), pl.BlockSpec((2 * KVW, WIN), prev),
                   pl.BlockSpec((8, LANES), lambda i: (0, 0))] + cm.out_specs(),
        out_shape=[jax.ShapeDtypeStruct((D, S), bf16), jax.ShapeDtypeStruct((2 * KVW, S), bf16),
                   jax.ShapeDtypeStruct((8, LANES), f32)] + cm.out_shape(),
        scratch_shapes=[kvs(), kvs(), pltpu.VMEM((D, WIN), f32), kvs(), kvs(), kvs(), kvs(),
                        pltpu.VMEM((8, REP * WIN), f32)] + cm.scratch(),
        compiler_params=_cp("arbitrary", side_effects=bool(cm.n)),
    )(qt, kvt, kvt, cos, sin, cos, sin, sinks, daot, *cm.bufs)
    return res[0], res[1], res[2], res[3:]


ADAM_C1 = 1.0 / (1.0 - ADAM_B1 ** ADAM_STEP)
ADAM_C2 = 1.0 / (1.0 - ADAM_B2 ** ADAM_STEP)


def _adam_update(g, w, m, v):
    nm = ADAM_B1 * m + (1.0 - ADAM_B1) * g
    nv = ADAM_B2 * v + (1.0 - ADAM_B2) * (g * g)
    return -ADAM_LR * ((nm * ADAM_C1) / (jnp.sqrt(nv * ADAM_C2) + ADAM_EPS) + ADAM_WD * w), nm, nv


def adamw(parts, w, m, v, tr, name):
    n, R, C = parts.shape

    def body(p_ref, w_ref, m_ref, v_ref, g_ref, d_ref, nm_ref, nv_ref):
        def grp(g0, _):
            r0 = pl.multiple_of(g0 * RG, RG)
            rows = pl.ds(r0, RG)
            g = p_ref[0, rows, :].astype(f32)
            for k in range(1, n):
                g = g + p_ref[k, rows, :].astype(f32)
            d, nm, nv = _adam_update(g, w_ref[rows, :], m_ref[rows, :], v_ref[rows, :])
            g_ref[rows, :] = g
            d_ref[rows, :] = d
            nm_ref[rows, :] = nm
            nv_ref[rows, :] = nv
            return 0

        lax.fori_loop(0, tr // RG, grp, 0)

    row = pl.BlockSpec((tr, C), lambda i: (i, 0))
    o = jax.ShapeDtypeStruct((R, C), f32)
    return pl.pallas_call(
        body, name=name, grid=(R // tr,),
        in_specs=[pl.BlockSpec((n, tr, C), lambda i: (0, i, 0)), row, row, row],
        out_specs=[row, row, row, row], out_shape=[o, o, o, o],
        compiler_params=_cp("parallel"),
    )(parts, w, m, v)


SMALL_ROW = (("norm_mix_post_w", D), ("norm_ffn_pre_w", D), ("norm_ffn_post_w", D), ("ssd_norm_w", DI),
             ("ssd_conv_b", CONVD), ("ffn_conv_b", 2 * FF), ("ssd_dt_bias", NH), ("ssd_a_log", NH), ("ssd_d", NH),
             ("attn_sinks", AH), ("loss", 1))
CONV_BLOCK = 1152
SSD_CONV_COLS = CONVD // N_DEV
FFN_CONV_COLS = 2 * FF // N_DEV


def _row_offsets():
    off, o = {}, 0
    for name, n in SMALL_ROW:
        off[name] = (o, n)
        o += -(-n // LANES) * LANES
    return off, o


def adamw_small(recv_row, recv_pre, recv_conv, params):
    off, _ = _row_offsets()
    names = list(params)
    n = len(names)

    def total(ref, rows, lo, width):
        g = ref[0, rows, lo:lo + width]
        for d in range(1, N_DEV):
            g = g + ref[d, rows, lo:lo + width]
        return g

    def grad_of(name, row_ref, pre_ref, conv_ref):
        if name == "norm_mix_pre_w":
            return total(pre_ref, slice(0, 1), 0, D)
        if name == "ssd_conv_w":
            return total(conv_ref, slice(0, SSD_K), 0, SSD_CONV_COLS)
        if name == "ffn_conv_w":
            return total(conv_ref, slice(0, FFN_K), 3 * LANES, FFN_CONV_COLS)
        o, width = off[name]
        return total(row_ref, slice(0, 1), o, width)

    def body(row_ref, pre_ref, conv_ref, *refs):
        ins, outs = refs[:3 * n], refs[3 * n:]
        for k, name in enumerate(names):
            w_ref, m_ref, v_ref = ins[3 * k:3 * k + 3]
            g_ref, d_ref, nm_ref, nv_ref = outs[4 * k:4 * k + 4]
            g = grad_of(name, row_ref, pre_ref, conv_ref)
            d, nm, nv = _adam_update(g, w_ref[...], m_ref[...], v_ref[...])
            g_ref[...] = g
            d_ref[...] = d
            nm_ref[...] = nm
            nv_ref[...] = nv
        outs[4 * n][...] = total(row_ref, slice(0, 1), off["loss"][0], LANES)

    flat = [t for name in names for t in params[name]]
    out_shape = [jax.ShapeDtypeStruct(params[name][0].shape, f32) for name in names for _ in range(4)]
    res = pl.pallas_call(
        body, name="adamw_small",
        out_shape=out_shape + [jax.ShapeDtypeStruct((1, LANES), f32)],
        compiler_params=pltpu.CompilerParams(vmem_limit_bytes=VMEM_LIMIT),
    )(recv_row, recv_pre, recv_conv, *flat)
    return {name: res[4 * k:4 * k + 4] for k, name in enumerate(names)}, res[4 * n]


def _cat_rows(parts):
    words = [lax.bitcast_convert_type(p, jnp.uint16) for p in parts]
    return lax.bitcast_convert_type(jnp.concatenate(words, axis=0), bf16)


def _pad_rows8(w):
    return jnp.pad(w, ((0, 8 - w.shape[0]), (0, 0)))


def _pad_lanes(v):
    return jnp.pad(v.reshape(1, -1), ((0, 0), (0, LANES - v.size)))


WEIGHTS = ('norm_mix_pre_w', 'w_in', 'ssd_conv_w', 'ssd_conv_b', 'ssd_dt_bias', 'ssd_a_log', 'ssd_d', 'ssd_norm_w',
           'ssd_w_out', 'attn_sinks', 'attn_w_out', 'w_mix_out', 'norm_mix_post_w', 'norm_ffn_pre_w', 'ffn_w_up',
           'ffn_conv_w', 'ffn_conv_b', 'ffn_w_down', 'norm_ffn_post_w')
W_IN_ROWS = IN_DIM // N_DEV
W_IN_PAD = 1104
W_IN_SPLIT = (672, 768, 832)
TS = 512
TS_NORM = 1024


def kernel(x, positions, norm_mix_pre_w, w_in, ssd_conv_w, ssd_conv_b, ssd_dt_bias, ssd_a_log, ssd_d, ssd_norm_w, ssd_w_out, attn_sinks, attn_w_out, w_mix_out, norm_mix_post_w, norm_ffn_pre_w, ffn_w_up, ffn_conv_w, ffn_conv_b, ffn_w_down, norm_ffn_post_w, loss_target, m_norm_mix_pre_w, m_w_in, m_ssd_conv_w, m_ssd_conv_b, m_ssd_dt_bias, m_ssd_a_log, m_ssd_d, m_ssd_norm_w, m_ssd_w_out, m_attn_sinks, m_attn_w_out, m_w_mix_out, m_norm_mix_post_w, m_norm_ffn_pre_w, m_ffn_w_up, m_ffn_conv_w, m_ffn_conv_b, m_ffn_w_down, m_norm_ffn_post_w, v_norm_mix_pre_w, v_w_in, v_ssd_conv_w, v_ssd_conv_b, v_ssd_dt_bias, v_ssd_a_log, v_ssd_d, v_ssd_norm_w, v_ssd_w_out, v_attn_sinks, v_attn_w_out, v_w_mix_out, v_norm_mix_post_w, v_norm_ffn_pre_w, v_ffn_w_up, v_ffn_conv_w, v_ffn_conv_b, v_ffn_w_down, v_norm_ffn_post_w):
    a = locals()
    r2 = lambda t: t.reshape(t.shape[-2], t.shape[-1])
    w = {n: r2(a[n]) for n in WEIGHTS}
    m = {n: r2(a["m_" + n]) for n in WEIGHTS}
    v = {n: r2(a["v_" + n]) for n in WEIGHTS}
    xs, target = x[0], loss_target[0]
    S = xs.shape[0]
    ts, tsn = TS, min(TS_NORM, S)

    w_in_blk = jnp.pad(w["w_in"].T.astype(bf16), ((0, W_IN_PAD - W_IN_ROWS), (0, 0)))
    conv_blk = jnp.concatenate([_pad_rows8(w["ssd_conv_w"]), _pad_rows8(w["ffn_conv_w"]),
                                jnp.zeros((8, CONV_BLOCK - SSD_CONV_COLS - FFN_CONV_COLS), f32)], axis=1)
    u, cos, sin, (g_in, g_conv) = prenorm_fwd(xs, w["norm_mix_pre_w"], positions, ts, [w_in_blk, conv_blk])
    wt = g_in[:, :W_IN_ROWS].reshape(IN_DIM, D)
    w_main_t = _cat_rows([wt[IN_OFF[0]:IN_OFF[1]], wt[IN_OFF[6]:IN_OFF[8]], wt[IN_OFF[1]:IN_OFF[2]]])
    w_q_t = wt[IN_OFF[3]:IN_OFF[4]]
    w_kv_t = wt[IN_OFF[4]:IN_OFF[6]]
    w_dt_t = jnp.pad(wt[IN_OFF[2]:IN_OFF[3]], ((0, LANES - NH), (0, 0)))
    conv_w8 = g_conv[:, :, 0:SSD_CONV_COLS].transpose(1, 0, 2).reshape(8, CONVD)
    fconv_w8 = g_conv[:, :, SSD_CONV_COLS:SSD_CONV_COLS + FFN_CONV_COLS].transpose(1, 0, 2).reshape(8, 2 * FF)
    bias = _pad_lanes(w["ssd_dt_bias"])
    alog = _pad_lanes(w["ssd_a_log"])
    dx_row = jnp.repeat(w["ssd_d"].reshape(-1), HD).reshape(1, DI)
    sinks = _pad_lanes(w["attn_sinks"])

    later = [w["ssd_w_out"].astype(bf16), w["attn_w_out"].astype(bf16), w["w_mix_out"].astype(bf16)]
    proj, (g_so, g_ao, g_mix) = mm(u, w_main_t, "nt", bf16, "mm_proj", comm=(later, (False,) * 3))
    w_ssd_out, w_attn_out, w_mix = g_so.reshape(DI, D), g_ao.reshape(D, D), g_mix.reshape(D, D)
    qt = mm(w_q_t, u, "nt", bf16, "mm_q")
    kvt = mm(w_kv_t, u, "nt", bf16, "mm_kv")
    dtr = mm(u, w_dt_t, "nt", f32, "mm_dt")
    xbc, conv_c = ssdconv_fwd(proj, conv_w8, w["ssd_conv_b"], tsn)
    y, hprev, ssd_saved, (g_up, g_down) = ssd_fwd(xbc, dtr, bias, alog, dx_row, comm=(
        [w["ffn_w_up"].T.astype(bf16), w["ffn_w_down"].astype(bf16)], (False, False)))
    w_up_t = g_up.reshape(2 * FF, D)
    w_down = g_down.reshape(FF, D)
    yn = gnorm_fwd(y, proj, w["ssd_norm_w"], tsn)
    ys = mm(yn, w_ssd_out, "nn", bf16, "mm_ssd_out")
    aot = attn_fwd(qt, kvt, cos, sin, sinks)
    ya = mm(aot, w_attn_out, "tn", bf16, "mm_attn_out")
    merged = merge_fwd(proj, ys, ya, tsn)
    mo = mm(merged, w_mix, "nn", bf16, "mm_mix")
    x1, h = post_fwd(xs, mo, w["norm_mix_post_w"], w["norm_ffn_pre_w"], tsn)
    up = mm(h, w_up_t, "nt", bf16, "mm_up")
    act, gate_c, val_c = ffnact_fwd(up, fconv_w8, w["ffn_conv_b"], ts)
    ff = mm(act, w_down, "nn", bf16, "mm_down")
    loss_blk, dout, dff, g_post2 = loss_head(x1, ff, target, w["norm_ffn_post_w"], tsn)

    dact = mm(dff, w_down, "nt", bf16, "mm_dact")
    gw_down = mm(act, dff, "tn", bf16, "mm_g_down")
    dgate, dval = ffnact_bwd(dact, gate_c, val_c, ts)
    dup_pre, g_fconv_a = dwconv_bwd(dgate, up, 0, fconv_w8, 0, FFN_K, FF, ts, "ffnconv_bwd_gate", out_cols=2 * FF)
    dup_pre, g_fconv_b = dwconv_bwd(dval, up, 1, fconv_w8, 1, FFN_K, FF, ts, "ffnconv_bwd_val", into=dup_pre, ocb=1,
                                    out_cols=2 * FF)
    g_fconv = jnp.concatenate([g_fconv_a, g_fconv_b], axis=1)
    dh, (r_down,) = mm(dup_pre, w_up_t, "nn", bf16, "mm_dh", comm=([gw_down.reshape(N_DEV, FF // N_DEV, D)], (True,)))
    gw_up_t = mm(dup_pre, h, "tn", bf16, "mm_g_up")
    dx1, dmo, g_norms = post_bwd(dout, dh, x1, mo, w["norm_mix_post_w"], w["norm_ffn_pre_w"], tsn)
    dmerged = mm(dmo, w_mix, "nt", bf16, "mm_dmerged")
    gw_mix = mm(merged, dmo, "tn", bf16, "mm_g_mix")
    dys, dya, dproj = merge_bwd(dmerged, proj, ys, ya, tsn)
    daot = mm(w_attn_out, dya, "nt", bf16, "mm_dao")
    gw_attn_out = mm(aot, dya, "nn", bf16, "mm_g_attn_out")
    dqt, dkvt, g_sinks, (r_up,) = attn_bwd(qt, kvt, cos, sin, sinks, daot,
                                           comm=([gw_up_t.reshape(N_DEV, 2 * FF // N_DEV, D)], (True,)))
    dyn = mm(dys, w_ssd_out, "nt", bf16, "mm_dyn")
    gw_ssd_out = mm(yn, dys, "tn", bf16, "mm_g_ssd_out")
    dy, dproj, g_gnorm = gnorm_bwd(dyn, y, proj, w["ssd_norm_w"], dproj, tsn)
    sends = [gw_ssd_out.reshape(N_DEV, DI // N_DEV, D), gw_attn_out.reshape(N_DEV, D // N_DEV, D),
             gw_mix.reshape(N_DEV, D // N_DEV, D)]
    dxbc, ddtr, g_ssd, (r_so, r_ao, r_mix) = ssd_bwd(xbc, dtr, dy, hprev, ssd_saved, bias, alog, dx_row,
                                                     comm=(sends, (True,) * 3))
    dproj, g_conv_w = dwconv_bwd(dxbc, proj, C_XBC // 1024, conv_w8, 0, SSD_K, 1024, tsn, "ssdconv_bwd", act_c=conv_c,
                                 into=dproj, ocb=C_XBC // 1024, out_cols=PM)
    ddtr_b = ddtr.astype(bf16)
    du_c = mm(ddtr_b, w_dt_t, "nn", bf16, "mm_du_dt")
    g_main_t = mm(dproj, u, "tn", bf16, "mm_g_in")
    g_q_t = mm(dqt, u, "nn", bf16, "mm_g_q")
    g_kv_t = mm(dkvt, u, "nn", bf16, "mm_g_kv")
    g_dt_t = mm(ddtr_b, u, "tn", bf16, "mm_g_dt")
    g_wt = _cat_rows([g_main_t[C_Z:C_GS], g_main_t[C_XBC:PM], g_dt_t[:NH], g_q_t, g_kv_t, g_main_t[C_GS:C_XBC]])
    send_in = jnp.pad(g_wt.reshape(N_DEV, W_IN_ROWS, D), ((0, 0), (0, W_IN_PAD - W_IN_ROWS), (0, 0)))
    pieces = {"norm_mix_post_w": g_norms[1:2], "norm_ffn_pre_w": g_norms[0:1], "norm_ffn_post_w": g_post2[0:1],
              "ssd_norm_w": g_gnorm[0:1], "ssd_conv_b": g_conv_w[7:8], "ffn_conv_b": g_fconv[7:8],
              "ssd_dt_bias": g_ssd[0:1], "ssd_a_log": g_ssd[1:2], "ssd_d": g_ssd[2:3], "attn_sinks": g_sinks[0:1],
              "loss": loss_blk[0:1]}
    row = jnp.concatenate([jnp.pad(pieces[n][:, :min(k, pieces[n].shape[1])],
                                   ((0, 0), (0, -(-k // LANES) * LANES - min(k, pieces[n].shape[1]))))
                           for n, k in SMALL_ROW], axis=1)
    send_row = jnp.pad(row, ((0, 7), (0, 0)))
    send_conv = jnp.concatenate(
        [g_conv_w.reshape(8, N_DEV, SSD_CONV_COLS).transpose(1, 0, 2),
         g_fconv.reshape(8, N_DEV, FFN_CONV_COLS).transpose(1, 0, 2),
         jnp.zeros((N_DEV, 8, CONV_BLOCK - SSD_CONV_COLS - FFN_CONV_COLS), f32)], axis=2)
    r0, r1, r2 = W_IN_SPLIT
    du_a, (r_in_a, recv_row, recv_conv) = mm(dproj, w_main_t, "nn", bf16, "mm_du", comm=(
        [send_in[:, :r0], send_row, send_conv], (True, False, True)))
    du_d, (r_in_b,) = mm(dqt, w_q_t, "tn", bf16, "mm_du_q", comm=([send_in[:, r0:r1]], (True,)))
    du_b, (r_in_c,) = mm(dkvt, w_kv_t, "tn", bf16, "mm_du_kv", comm=([send_in[:, r1:r2]], (True,)))
    grad_x, g_pre, (r_in_d,) = prenorm_bwd(xs, w["norm_mix_pre_w"], (du_a, du_b, du_c, du_d), dx1, tsn,
                                           comm=([send_in[:, r2:]], (True,)))
    (recv_pre,) = exchange([g_pre], (False,), "gather_last")

    r_in = jnp.concatenate([r_in_a, r_in_b, r_in_c, r_in_d], axis=1)
    tpad = lambda t: jnp.pad(t.T, ((0, W_IN_PAD - W_IN_ROWS), (0, 0)))
    o_in = [t[:W_IN_ROWS].T for t in adamw(r_in, tpad(w["w_in"]), tpad(m["w_in"]), tpad(v["w_in"]), 368, "adamw_w_in")]
    o_up = [t.T for t in adamw(r_up, w["ffn_w_up"].T, m["ffn_w_up"].T, v["ffn_w_up"].T, 352, "adamw_w_up")]
    big = {"w_in": o_in, "ffn_w_up": o_up,
           "ssd_w_out": adamw(r_so, w["ssd_w_out"], m["ssd_w_out"], v["ssd_w_out"], 256, "adamw_ssd_out"),
           "attn_w_out": adamw(r_ao, w["attn_w_out"], m["attn_w_out"], v["attn_w_out"], 128, "adamw_attn_out"),
           "w_mix_out": adamw(r_mix, w["w_mix_out"], m["w_mix_out"], v["w_mix_out"], 128, "adamw_mix"),
           "ffn_w_down": adamw(r_down, w["ffn_w_down"], m["ffn_w_down"], v["ffn_w_down"], 352, "adamw_down")}
    small_names = [n for n in WEIGHTS if n not in big]
    small, loss_row = adamw_small(recv_row, recv_pre, recv_conv, {n: (w[n], m[n], v[n]) for n in small_names})

    outs = [loss_row[0, 0], grad_x[None]]
    for k in range(4):
        for n in WEIGHTS:
            outs.append((big[n][k] if n in big else small[n][k]).reshape(a[n].shape))
    return tuple(outs)
```

```python
import jax
import jax.numpy as jnp
import numpy as np
from jax import lax
from jax.experimental import pallas as pl
from jax.experimental.pallas import tpu as pltpu

f32 = jnp.float32
bf16 = jnp.bfloat16

N_DEV = 8
D = 1024
DI = 2048
NH = 32
HD = 64
NG = 4
GW = DI // NG
NS = 128
CH = 128
CONVD = DI + 2 * NG * NS
SSD_K = 4
AH = 16
AD = 64
KVH = 4
REP = AH // KVH
KVW = KVH * AD
WIN = 128
FF = 2816
FFN_K = 3
EPS = 1e-6
ROPE_THETA = 10000.0
LANES = 128
RG = 16
CW = 256

C_Z, C_GS, C_GA, C_XBC, PM = 0, 2048, 3072, 4096, 7168
IN_SIZES = (DI, CONVD, NH, D, KVW, KVW, D, D)
IN_OFF = tuple(int(v) for v in np.cumsum((0,) + IN_SIZES))
IN_DIM = IN_OFF[-1]

ADAM_LR, ADAM_B1, ADAM_B2, ADAM_EPS, ADAM_WD, ADAM_STEP = 0.001, 0.9, 0.999, 1e-08, 0.01, 10

VMEM_LIMIT = 56 * 1024 * 1024


def _cp(*sem, side_effects=False):
    return pltpu.CompilerParams(dimension_semantics=sem, vmem_limit_bytes=VMEM_LIMIT, has_side_effects=side_effects)


def _dot(a, b, mode="nn"):
    dims = {"nn": (((1,), (0,)), ((), ())), "nt": (((1,), (1,)), ((), ())), "tn": (((0,), (0,)), ((), ()))}[mode]
    return lax.dot_general(a, b, dims, preferred_element_type=f32)


def _split3(v):
    hi = v.astype(bf16)
    r = v - hi.astype(f32)
    mid = r.astype(bf16)
    lo = (r - mid.astype(f32)).astype(bf16)
    return hi, mid, lo


def _dot3_left(m01, v):
    hi, mid, lo = _split3(v)
    return _dot(m01, hi) + _dot(m01, mid) + _dot(m01, lo)


def _dot3_right(v, m01):
    hi, mid, lo = _split3(v)
    return _dot(hi, m01) + _dot(mid, m01) + _dot(lo, m01)


def _dot2_right(v, m01):
    hi = v.astype(bf16)
    lo = (v - hi.astype(f32)).astype(bf16)
    return _dot(hi, m01) + _dot(lo, m01)


def _sigmoid(x):
    return 1.0 / (1.0 + jnp.exp(-x))


def _sigmoid_fast(x):
    return pl.reciprocal(1.0 + jnp.exp(-x), approx=True)


def _peer(k, x, y, c):
    return ((1 - x) if k & 4 else x, (1 - y) if k & 2 else y, (1 - c) if k & 1 else c)


def _xchg_copies(buf_refs, out_refs, send_sems, recv_sems, local_sems, personalised):
    x, y, c = lax.axis_index("x"), lax.axis_index("y"), lax.axis_index("c")
    me = 4 * x + 2 * y + c
    local, remote = [], []
    for b, (buf, out, pers) in enumerate(zip(buf_refs, out_refs, personalised)):
        local.append(pltpu.make_async_copy(buf.at[me] if pers else buf, out.at[me], local_sems.at[b]))
        for k in range(1, N_DEV):
            px, py, pc = _peer(k, x, y, c)
            s = b * (N_DEV - 1) + k - 1
            remote.append(pltpu.make_async_remote_copy(
                src_ref=buf.at[4 * px + 2 * py + pc] if pers else buf, dst_ref=out.at[me],
                send_sem=send_sems.at[s], recv_sem=recv_sems.at[s],
                device_id=(px, py, pc), device_id_type=pl.DeviceIdType.MESH))
    return local, remote


class _Comm:
    def __init__(self, comm):
        self.bufs, self.pers = comm if comm else ((), ())
        self.n = len(self.bufs)

    def in_specs(self):
        return [pl.BlockSpec(memory_space=pl.ANY)] * self.n

    out_specs = in_specs

    def out_shape(self):
        return [jax.ShapeDtypeStruct((N_DEV,) + tuple(b.shape[1:] if p else b.shape), b.dtype)
                for b, p in zip(self.bufs, self.pers)]

    def scratch(self):
        n = self.n
        return [pltpu.SemaphoreType.DMA((n * (N_DEV - 1),)), pltpu.SemaphoreType.DMA((n * (N_DEV - 1),)),
                pltpu.SemaphoreType.DMA((n,))] if n else []

    def split(self, refs, n_in, n_out):
        n = self.n
        ins, outs = refs[:n_in], refs[n_in + n:n_in + n + n_out]
        rest = refs[n_in + n + n_out + n:]
        if not n:
            return ins, outs, rest, None
        copies = _xchg_copies(refs[n_in:n_in + n], refs[n_in + n + n_out:n_in + n + n_out + n], *rest[-3:], self.pers)
        return ins, outs, rest[:-3], copies

    def start(self, copies, first):
        if copies:
            @pl.when(first)
            def _():
                for cp in copies[0] + copies[1]:
                    cp.start()

    def wait(self, copies, last):
        if copies:
            @pl.when(last)
            def _():
                for cp in copies[1]:
                    cp.wait_recv()
                for cp in copies[1]:
                    cp.wait_send()
                for cp in copies[0]:
                    cp.wait()


def exchange(bufs, personalised, name):
    cm = _Comm((bufs, personalised))

    def body(*refs):
        _, _, _, copies = cm.split(refs, 0, 0)
        cm.start(copies, True)
        cm.wait(copies, True)

    return pl.pallas_call(
        body, name=name, in_specs=cm.in_specs(), out_specs=cm.out_specs(), out_shape=cm.out_shape(),
        scratch_shapes=cm.scratch(), compiler_params=pltpu.CompilerParams(has_side_effects=True),
    )(*bufs)


class _TwoLevelGather:
    def __init__(self, bufs):
        self.bufs = list(bufs)
        self.n = len(self.bufs)

    def in_specs(self):
        return [pl.BlockSpec(memory_space=pl.ANY)] * self.n

    out_specs = in_specs

    def out_shape(self):
        return [jax.ShapeDtypeStruct((N_DEV,) + tuple(b.shape), b.dtype) for b in self.bufs]

    def scratch(self):
        per = N_DEV - 1
        return [pltpu.SemaphoreType.DMA((self.n * per,)), pltpu.SemaphoreType.DMA((self.n * per,)),
                pltpu.SemaphoreType.DMA((self.n,))]

    def bind(self, ins, outs, send_sems, recv_sems, local_sems):
        n, per = self.n, N_DEV - 1
        x, y, c = lax.axis_index("x"), lax.axis_index("y"), lax.axis_index("c")
        me, sibling = (x, y, c), (x, y, 1 - c)
        chips = [(1 - x, y), (x, 1 - y), (1 - x, 1 - y)]

        def copy(b, k, block, to, src=None):
            dst = outs[b].at[4 * block[0] + 2 * block[1] + block[2]]
            return pltpu.make_async_remote_copy(
                src_ref=dst if src is None else src, dst_ref=dst,
                send_sem=send_sems.at[b * per + k], recv_sem=recv_sems.at[b * per + k],
                device_id=to, device_id_type=pl.DeviceIdType.MESH)

        mine = [pltpu.make_async_copy(ins[b], outs[b].at[4 * x + 2 * y + c], local_sems.at[b]) for b in range(n)]
        first = []
        for b in range(n):
            first.append(copy(b, 0, me, sibling, src=ins[b]))
            first += [copy(b, 1 + j, me, (*chip, c), src=ins[b]) for j, chip in enumerate(chips)]

        def start():
            for cp in mine + first:
                cp.start()

        def finish():
            passed = []
            for j, chip in enumerate(chips):
                for b in range(n):
                    copy(b, 1 + j, (*chip, c), me).wait_recv()
                    passed.append(copy(b, 4 + j, (*chip, c), sibling))
                    passed[-1].start()
            for b in range(n):
                copy(b, 0, sibling, me).wait_recv()
                for j, chip in enumerate(chips):
                    copy(b, 4 + j, (*chip, 1 - c), me).wait_recv()
            for cp in first + passed:
                cp.wait_send()
            for cp in mine:
                cp.wait()

        return start, finish


MM_TILES = (3584, 2176, 2048, 1792, 1408, 1024, 512, 256, 128)
MM_K_TILES = (4096, 3584, 2816) + MM_TILES[1:]
MM_VMEM_BUDGET = 44 * 1024 * 1024


def _mm_tiles(M, N, K, out_bytes):
    cm = [t for t in MM_TILES if M % t == 0]
    cn = [t for t in MM_TILES if N % t == 0]
    ck = [t for t in MM_K_TILES if K % t == 0]
    best = None
    for bm in cm[:2]:
        for bn in cn:
            for bk in ck:
                need = 4 * (bm * bk + bk * bn) + bm * bn * (4 + 2 * out_bytes)
                if need <= MM_VMEM_BUDGET:
                    score = (bm * bn * bk, bk)
                    if best is None or score > best[0]:
                        best = (score, (bm, bn, bk))
    return best[1]


def mm(a, b, mode, out_dtype, name, comm=None):
    if mode == "nn":
        (M, K), (_, N) = a.shape, b.shape
    elif mode == "nt":
        (M, K), (N, _) = a.shape, b.shape
    else:
        (K, M), (_, N) = a.shape, b.shape
    bm, bn, bk = _mm_tiles(M, N, K, jnp.dtype(out_dtype).itemsize)
    gm, gn, nk = M // bm, N // bn, K // bk
    cm = _Comm(comm)

    def body(*refs):
        (a_ref, b_ref), (o_ref,), scr, copies = cm.split(refs, 2, 1)
        i, j, k = pl.program_id(0), pl.program_id(1), pl.program_id(2)
        cm.start(copies, jnp.logical_and(jnp.logical_and(i == 0, j == 0), k == 0))
        p = _dot(a_ref[...], b_ref[...], mode)
        if nk == 1:
            o_ref[...] = p.astype(o_ref.dtype)
        else:
            acc_ref = scr[0]

            @pl.when(k == 0)
            def _():
                acc_ref[...] = p

            @pl.when(k > 0)
            def _():
                acc_ref[...] += p

            @pl.when(k == nk - 1)
            def _():
                o_ref[...] = acc_ref[...].astype(o_ref.dtype)

        cm.wait(copies, jnp.logical_and(jnp.logical_and(i == gm - 1, j == gn - 1), k == nk - 1))

    if mode == "nn":
        a_spec = pl.BlockSpec((bm, bk), lambda i, j, k: (i, k))
        b_spec = pl.BlockSpec((bk, bn), lambda i, j, k: (k, j))
    elif mode == "nt":
        a_spec = pl.BlockSpec((bm, bk), lambda i, j, k: (i, k))
        b_spec = pl.BlockSpec((bn, bk), lambda i, j, k: (j, k))
    else:
        a_spec = pl.BlockSpec((bk, bm), lambda i, j, k: (k, i))
        b_spec = pl.BlockSpec((bk, bn), lambda i, j, k: (k, j))
    sem = ("arbitrary",) * 3 if cm.n else ("parallel", "parallel", "arbitrary")
    res = pl.pallas_call(
        body, name=name, grid=(gm, gn, nk),
        in_specs=[a_spec, b_spec] + cm.in_specs(),
        out_specs=[pl.BlockSpec((bm, bn), lambda i, j, k: (i, j))] + cm.out_specs(),
        out_shape=[jax.ShapeDtypeStruct((M, N), out_dtype)] + cm.out_shape(),
        scratch_shapes=([pltpu.VMEM((bm, bn), f32)] if nk > 1 else []) + cm.scratch(),
        compiler_params=_cp(*sem, side_effects=bool(cm.n)),
    )(a, b, *cm.bufs)
    return (res[0], res[1:]) if cm.n else res[0]


def _groups(ts, fn, carry=None, reverse=False, unroll=8, rg=RG):
    n = ts // rg
    if n == 1:
        return fn(0, carry)
    unroll = min(unroll, n)
    span = rg * unroll

    def body(g, c):
        r0 = pl.multiple_of((n // unroll - 1 - g if reverse else g) * span, span)
        for u in (range(unroll - 1, -1, -1) if reverse else range(unroll)):
            c = fn(pl.multiple_of(r0 + u * rg, rg), c)
        return c

    return lax.fori_loop(0, n // unroll, body, carry)


def _rms(x):
    return lax.rsqrt(jnp.mean(x * x, axis=-1, keepdims=True) + EPS)


def _rms_bwd(x, r, dn):
    n = x * r
    return r * (dn - n * jnp.mean(dn * n, axis=-1, keepdims=True))


NRG = 256


def _fold(x):
    return jnp.sum(x.reshape(x.shape[0] // 8, 8, x.shape[1]), axis=0)


def _flush(acc_ref, out_ref, row):
    out_ref[row:row + 1, :] = jnp.sum(acc_ref[...], axis=0, keepdims=True)


def prenorm_fwd(x, w, pos_row, ts, gather):
    S = x.shape[0]
    nt = S // ts
    tg = _TwoLevelGather(gather)
    n = tg.n
    half = AD // 2
    inv = ROPE_THETA ** (-jnp.arange(half, dtype=f32) * 2.0 / AD)
    inv_col = jnp.tile(inv, 2)[:, None]

    def body(x_ref, w_ref, p_ref, inv_ref, *refs):
        u_ref, cos_ref, sin_ref = refs[n:n + 3]
        start, finish = tg.bind(refs[:n], refs[n + 3:2 * n + 3], *refs[2 * n + 3:])
        i = pl.program_id(0)
        pl.when(i == 0)(start)
        wv = w_ref[...]

        def grp(r0, _):
            xv = x_ref[pl.ds(r0, NRG), :]
            u_ref[pl.ds(r0, NRG), :] = (xv * _rms(xv) * wv).astype(bf16)

        _groups(ts, grp, rg=NRG)
        ang = inv_ref[...] * p_ref[...].astype(f32)
        row = lax.broadcasted_iota(jnp.int32, ang.shape, 0)
        cos_ref[...] = jnp.cos(ang)
        sin_ref[...] = jnp.where(row < half, -1.0, 1.0) * jnp.sin(ang)
        pl.when(i == nt - 1)(finish)

    tab = pl.BlockSpec((AD, ts), lambda i: (0, i))
    res = pl.pallas_call(
        body, name="prenorm_fwd", grid=(nt,),
        in_specs=[pl.BlockSpec((ts, D), lambda i: (i, 0)), pl.BlockSpec((1, D), lambda i: (0, 0)),
                  pl.BlockSpec((1, ts), lambda i: (0, i)), pl.BlockSpec((AD, 1), lambda i: (0, 0))] + tg.in_specs(),
        out_specs=[pl.BlockSpec((ts, D), lambda i: (i, 0)), tab, tab] + tg.out_specs(),
        out_shape=[jax.ShapeDtypeStruct((S, D), bf16), jax.ShapeDtypeStruct((AD, S), f32),
                   jax.ShapeDtypeStruct((AD, S), f32)] + tg.out_shape(),
        scratch_shapes=tg.scratch(),
        compiler_params=_cp("arbitrary", side_effects=True),
    )(x, w, pos_row, inv_col, *tg.bufs)
    return res[0], res[1], res[2], res[3:]


def prenorm_bwd(x, w, dus, dx1, ts, comm=None):
    S = x.shape[0]
    nt = S // ts
    nd = len(dus)
    cm = _Comm(comm)

    def body(*refs):
        ins, (gx_ref, gw_ref), (acc_ref,), copies = cm.split(refs, nd + 3, 2)
        x_ref, w_ref = ins[:2]
        du_refs, dx1_ref = ins[2:2 + nd], ins[2 + nd]
        i = pl.program_id(0)
        cm.start(copies, i == 0)
        wv = w_ref[...]

        @pl.when(i == 0)
        def _():
            acc_ref[...] = jnp.zeros_like(acc_ref)
            gw_ref[...] = jnp.zeros_like(gw_ref)

        def grp(r0, _):
            rows = pl.ds(r0, NRG)
            xv = x_ref[rows, :]
            r = _rms(xv)
            du = du_refs[0][rows, :].astype(f32)
            for d_ref in du_refs[1:]:
                du = du + d_ref[rows, :].astype(f32)
            gx_ref[rows, :] = dx1_ref[rows, :] + _rms_bwd(xv, r, du * wv)
            acc_ref[...] += _fold(du * xv * r)

        _groups(ts, grp, rg=NRG)

        @pl.when(i == nt - 1)
        def _():
            _flush(acc_ref, gw_ref, 0)

        cm.wait(copies, i == nt - 1)

    row = pl.BlockSpec((ts, D), lambda i: (i, 0))
    res = pl.pallas_call(
        body, name="prenorm_bwd", grid=(nt,),
        in_specs=[row, pl.BlockSpec((1, D), lambda i: (0, 0))] + [row] * (nd + 1) + cm.in_specs(),
        out_specs=[row, pl.BlockSpec((8, D), lambda i: (0, 0))] + cm.out_specs(),
        out_shape=[jax.ShapeDtypeStruct((S, D), f32), jax.ShapeDtypeStruct((8, D), f32)] + cm.out_shape(),
        scratch_shapes=[pltpu.VMEM((8, D), f32)] + cm.scratch(),
        compiler_params=_cp("arbitrary", side_effects=bool(cm.n)),
    )(x, w, *dus, dx1, *cm.bufs)
    return res[0], res[1], res[2:]


def post_fwd(x, mo, w_post, w_pre2, ts):
    S = x.shape[0]

    def body(x_ref, mo_ref, wp_ref, w2_ref, x1_ref, h_ref):
        wp, w2 = wp_ref[...], w2_ref[...]

        def grp(r0, _):
            rows = pl.ds(r0, NRG)
            mv = mo_ref[rows, :].astype(f32)
            x1 = x_ref[rows, :] + mv * _rms(mv) * wp
            x1_ref[rows, :] = x1
            h_ref[rows, :] = (x1 * _rms(x1) * w2).astype(bf16)

        _groups(ts, grp, rg=NRG)

    row = pl.BlockSpec((ts, D), lambda i: (i, 0))
    par = pl.BlockSpec((1, D), lambda i: (0, 0))
    return pl.pallas_call(
        body, name="post_fwd", grid=(S // ts,),
        in_specs=[row, row, par, par], out_specs=[row, row],
        out_shape=[jax.ShapeDtypeStruct((S, D), f32), jax.ShapeDtypeStruct((S, D), bf16)],
        compiler_params=_cp("parallel"),
    )(x, mo, w_post, w_pre2)


def post_bwd(dout, dh, x1, mo, w_post, w_pre2, ts):
    S = x1.shape[0]
    nt = S // ts

    def body(dout_ref, dh_ref, x1_ref, mo_ref, wp_ref, w2_ref, dx1_ref, dmo_ref, gw_ref, acc2_ref, accp_ref):
        i = pl.program_id(0)
        wp, w2 = wp_ref[...], w2_ref[...]

        @pl.when(i == 0)
        def _():
            acc2_ref[...] = jnp.zeros_like(acc2_ref)
            accp_ref[...] = jnp.zeros_like(accp_ref)
            gw_ref[...] = jnp.zeros_like(gw_ref)

        def grp(r0, _):
            rows = pl.ds(r0, NRG)
            x1 = x1_ref[rows, :]
            r1 = _rms(x1)
            dh = dh_ref[rows, :].astype(f32)
            dx1 = dout_ref[rows, :] + _rms_bwd(x1, r1, dh * w2)
            dx1_ref[rows, :] = dx1
            acc2_ref[...] += _fold(dh * x1 * r1)
            mv = mo_ref[rows, :].astype(f32)
            rm = _rms(mv)
            dmo_ref[rows, :] = _rms_bwd(mv, rm, dx1 * wp).astype(bf16)
            accp_ref[...] += _fold(dx1 * mv * rm)

        _groups(ts, grp, rg=NRG)

        @pl.when(i == nt - 1)
        def _():
            _flush(acc2_ref, gw_ref, 0)
            _flush(accp_ref, gw_ref, 1)

    row = pl.BlockSpec((ts, D), lambda i: (i, 0))
    par = pl.BlockSpec((1, D), lambda i: (0, 0))
    return pl.pallas_call(
        body, name="post_bwd", grid=(nt,),
        in_specs=[row, row, row, row, par, par],
        out_specs=[row, row, pl.BlockSpec((8, D), lambda i: (0, 0))],
        out_shape=[jax.ShapeDtypeStruct((S, D), f32), jax.ShapeDtypeStruct((S, D), bf16),
                   jax.ShapeDtypeStruct((8, D), f32)],
        scratch_shapes=[pltpu.VMEM((8, D), f32), pltpu.VMEM((8, D), f32)],
        compiler_params=_cp("arbitrary"),
    )(dout, dh, x1, mo, w_post, w_pre2)


def loss_head(x1, ff, target, w, ts):
    S = x1.shape[0]
    nt = S // ts

    def body(x1_ref, ff_ref, t_ref, w_ref, loss_ref, dout_ref, dff_ref, gw_ref, accw_ref, accl_ref):
        i = pl.program_id(0)
        wv = w_ref[...]

        @pl.when(i == 0)
        def _():
            accw_ref[...] = jnp.zeros_like(accw_ref)
            accl_ref[...] = jnp.zeros_like(accl_ref)
            gw_ref[...] = jnp.zeros_like(gw_ref)

        def grp(r0, _):
            rows = pl.ds(r0, NRG)
            fv = ff_ref[rows, :].astype(f32)
            r = _rms(fv)
            n = fv * r
            e = x1_ref[rows, :] + n * wv - t_ref[rows, :]
            dout = e * (1.0 / D)
            dout_ref[rows, :] = dout
            dff_ref[rows, :] = _rms_bwd(fv, r, dout * wv).astype(bf16)
            accw_ref[...] += _fold(dout * n)
            accl_ref[...] += _fold(e * e)

        _groups(ts, grp, rg=NRG)

        @pl.when(i == nt - 1)
        def _():
            _flush(accw_ref, gw_ref, 0)
            tot = jnp.sum(jnp.sum(accl_ref[...], axis=1, keepdims=True), axis=0, keepdims=True) * (0.5 / D)
            loss_ref[...] = jnp.broadcast_to(tot, loss_ref.shape)

    row = pl.BlockSpec((ts, D), lambda i: (i, 0))
    return pl.pallas_call(
        body, name="loss_head", grid=(nt,),
        in_specs=[row, row, row, pl.BlockSpec((1, D), lambda i: (0, 0))],
        out_specs=[pl.BlockSpec((8, LANES), lambda i: (0, 0)), row, row, pl.BlockSpec((8, D), lambda i: (0, 0))],
        out_shape=[jax.ShapeDtypeStruct((8, LANES), f32), jax.ShapeDtypeStruct((S, D), f32),
                   jax.ShapeDtypeStruct((S, D), bf16), jax.ShapeDtypeStruct((8, D), f32)],
        scratch_shapes=[pltpu.VMEM((8, D), f32), pltpu.VMEM((8, D), f32)],
        compiler_params=_cp("arbitrary"),
    )(x1, ff, target, w)


def _taps(w_ref, cs, K):
    return [jnp.broadcast_to(w_ref[k:k + 1, cs], (8, CW)) for k in range(K)]


def _rolls_down(v, K):
    return tuple(pltpu.roll(v, s, 0) for s in range(1, K))


def _rolls_up(v, K):
    return tuple(pltpu.roll(v, 8 - s, 0) for s in range(1, K))


def _shifted_down(prolls, a, b, K, sub):
    arolls, brolls = _rolls_down(a, K), _rolls_down(b, K)
    out = [(a, b)]
    for s in range(1, K):
        m = sub < s
        out.append((jnp.where(m, prolls[s - 1], arolls[s - 1]), jnp.where(m, arolls[s - 1], brolls[s - 1])))
    return out, brolls


def _shifted_up(a, b, nrolls, K, sub):
    arolls, brolls = _rolls_up(a, K), _rolls_up(b, K)
    out = [(a, b)]
    for s in range(1, K):
        m = sub < 8 - s
        out.append((jnp.where(m, arolls[s - 1], brolls[s - 1]), jnp.where(m, brolls[s - 1], nrolls[s - 1])))
    return out, arolls


def _conv_group(prolls, a, b, taps, bias, K, sub):
    shifted, brolls = _shifted_down(prolls, a, b, K, sub)
    ya, yb = bias, bias
    for k in range(K):
        xa, xb = shifted[K - 1 - k]
        ya = ya + taps[k] * xa
        yb = yb + taps[k] * xb
    return ya, yb, brolls


def _prev8_map(ts, cb):
    return lambda i, j: (jnp.maximum(i * (ts // 8) - 1, 0), cb + j)


def ssdconv_fwd(proj, w8, b, ts):
    S = proj.shape[0]
    bw = 1024
    cb = C_XBC // bw

    def body(cur_ref, prev_ref, w_ref, b_ref, o_ref, c_ref):
        first = pl.program_id(0) == 0
        sub = lax.broadcasted_iota(jnp.int32, (8, CW), 0)
        for c0 in range(0, bw, CW):
            cs = slice(c0, c0 + CW)
            taps = _taps(w_ref, cs, SSD_K)
            bias = jnp.broadcast_to(b_ref[:, cs], (8, CW))

            def grp(r0, prolls, cs=cs, taps=taps, bias=bias):
                rows = pl.ds(r0, RG)
                xv = cur_ref[rows, cs].astype(f32)
                ya, yb, brolls = _conv_group(prolls, xv[0:8], xv[8:16], taps, bias, SSD_K, sub)
                y = jnp.concatenate([ya, yb], axis=0)
                c_ref[rows, cs] = y.astype(bf16)
                o_ref[rows, cs] = (y * _sigmoid_fast(y)).astype(bf16)
                return brolls

            _groups(ts, grp, _rolls_down(jnp.where(first, 0.0, prev_ref[:, cs].astype(f32)), SSD_K))

    o = jax.ShapeDtypeStruct((S, CONVD), bf16)
    blk = pl.BlockSpec((ts, bw), lambda i, j: (i, j))
    return pl.pallas_call(
        body, name="ssdconv_fwd", grid=(S // ts, CONVD // bw),
        in_specs=[pl.BlockSpec((ts, bw), lambda i, j: (i, cb + j)),
                  pl.BlockSpec((8, bw), _prev8_map(ts, cb)),
                  pl.BlockSpec((8, bw), lambda i, j: (0, j)),
                  pl.BlockSpec((1, bw), lambda i, j: (0, j))],
        out_specs=[blk, blk], out_shape=[o, o],
        compiler_params=_cp("parallel", "parallel"),
    )(proj, proj, w8, b)


def _gelu_tanh(x):
    c = 0.7978845608028654
    t = jnp.tanh(c * (x + 0.044715 * x * x * x))
    return 0.5 * x * (1.0 + t), t


def ffnact_fwd(up, w8, b, ts):
    S = up.shape[0]

    def body(g_ref, gp_ref, v_ref, vp_ref, wg_ref, wv_ref, bg_ref, bv_ref, o_ref, gc_ref, vc_ref):
        first = pl.program_id(0) == 0
        sub = lax.broadcasted_iota(jnp.int32, (8, CW), 0)
        for c0 in range(0, FF, CW):
            cs = slice(c0, c0 + CW)
            tg, tv = _taps(wg_ref, cs, FFN_K), _taps(wv_ref, cs, FFN_K)
            bg = jnp.broadcast_to(bg_ref[:, cs], (8, CW))
            bv = jnp.broadcast_to(bv_ref[:, cs], (8, CW))

            def grp(r0, carry, cs=cs, tg=tg, tv=tv, bg=bg, bv=bv):
                pg, pv = carry
                rows = pl.ds(r0, RG)
                gx = g_ref[rows, cs].astype(f32)
                vx = v_ref[rows, cs].astype(f32)
                ga, gb, pg = _conv_group(pg, gx[0:8], gx[8:16], tg, bg, FFN_K, sub)
                va, vb, pv = _conv_group(pv, vx[0:8], vx[8:16], tv, bv, FFN_K, sub)
                g = jnp.concatenate([ga, gb], axis=0)
                v = jnp.concatenate([va, vb], axis=0)
                gc_ref[rows, cs] = g.astype(bf16)
                vc_ref[rows, cs] = v.astype(bf16)
                o_ref[rows, cs] = (_gelu_tanh(g)[0] * v).astype(bf16)
                return pg, pv

            _groups(ts, grp, (_rolls_down(jnp.where(first, 0.0, gp_ref[:, cs].astype(f32)), FFN_K),
                              _rolls_down(jnp.where(first, 0.0, vp_ref[:, cs].astype(f32)), FFN_K)))

    o = jax.ShapeDtypeStruct((S, FF), bf16)
    blk = pl.BlockSpec((ts, FF), lambda i: (i, 0))
    prev = lambda cb: pl.BlockSpec((8, FF), lambda i: (jnp.maximum(i * (ts // 8) - 1, 0), cb))
    return pl.pallas_call(
        body, name="ffnact_fwd", grid=(S // ts,),
        in_specs=[blk, prev(0), pl.BlockSpec((ts, FF), lambda i: (i, 1)), prev(1),
                  pl.BlockSpec((8, FF), lambda i: (0, 0)), pl.BlockSpec((8, FF), lambda i: (0, 1)),
                  pl.BlockSpec((1, FF), lambda i: (0, 0)), pl.BlockSpec((1, FF), lambda i: (0, 1))],
        out_specs=[blk, blk, blk], out_shape=[o, o, o],
        compiler_params=_cp("parallel"),
    )(up, up, up, up, w8, w8, b, b)


def ffnact_bwd(dact, gc, vc, ts):
    S = dact.shape[0]

    def body(d_ref, g_ref, v_ref, dg_ref, dv_ref):
        c = 0.7978845608028654
        for c0 in range(0, FF, CW):
            cs = slice(c0, c0 + CW)

            def grp(r0, _, cs=cs):
                rows = pl.ds(r0, RG)
                d = d_ref[rows, cs].astype(f32)
                g = g_ref[rows, cs].astype(f32)
                ge, t = _gelu_tanh(g)
                dgelu = 0.5 * (1.0 + t) + 0.5 * g * (1.0 - t * t) * c * (1.0 + 3.0 * 0.044715 * g * g)
                dg_ref[rows, cs] = (d * v_ref[rows, cs].astype(f32) * dgelu).astype(bf16)
                dv_ref[rows, cs] = (d * ge).astype(bf16)

            _groups(ts, grp)

    o = jax.ShapeDtypeStruct((S, FF), bf16)
    blk = pl.BlockSpec((ts, FF), lambda i: (i, 0))
    return pl.pallas_call(
        body, name="ffnact_bwd", grid=(S // ts,),
        in_specs=[blk, blk, blk], out_specs=[blk, blk], out_shape=[o, o],
        compiler_params=_cp("parallel"),
    )(dact, gc, vc)


def dwconv_bwd(dy, x, xcb, w8, wcb, K, bw, ts, name, act_c=None, into=None, ocb=0, out_cols=None):
    S, C = dy.shape
    nr = S // ts
    out_cols = out_cols or C
    n_act = 0 if act_c is None else 2

    def body(*refs):
        dy_ref, dyn_ref = refs[0:2]
        c_ref, cn_ref = (refs[2:4] if n_act else (None, None))
        x_ref, xp_ref, w_ref = refs[2 + n_act:5 + n_act]
        dx_ref, dw_ref, sd_ref = refs[-3:]
        i = pl.program_id(1)
        first, last = i == 0, i == nr - 1
        sub = lax.broadcasted_iota(jnp.int32, (8, CW), 0)

        def grad_y(d, c):
            if c is None:
                return d.astype(f32)
            cv = c.astype(f32)
            s = _sigmoid_fast(cv)
            return d.astype(f32) * s * (1.0 + cv * (1.0 - s))

        @pl.when(first)
        def _():
            dw_ref[...] = jnp.zeros_like(dw_ref)

        for c0 in range(0, bw, CW):
            cs = slice(c0, c0 + CW)
            taps = _taps(w_ref, cs, K)
            zero = jnp.zeros((8, CW), f32)

            def fwd(r0, carry, cs=cs):
                prolls, accs, accb = carry
                rows = pl.ds(r0, RG)
                g = grad_y(dy_ref[rows, cs], c_ref[rows, cs] if n_act else None)
                sd_ref[rows, cs] = g
                xv = x_ref[rows, cs].astype(f32)
                ga, gb = g[0:8], g[8:16]
                shifted, brolls = _shifted_down(prolls, xv[0:8], xv[8:16], K, sub)
                new = []
                for k in range(K):
                    xa, xb = shifted[K - 1 - k]
                    new.append(accs[k] + ga * xa + gb * xb)
                return brolls, tuple(new), accb + ga + gb

            before = jnp.where(first, 0.0, xp_ref[:, cs].astype(f32))
            _, accs, accb = _groups(ts, fwd, (_rolls_down(before, K), (zero,) * K, zero))
            for k in range(K):
                dw_ref[k:k + 1, cs] += jnp.sum(accs[k], axis=0, keepdims=True)
            dw_ref[7:8, cs] += jnp.sum(accb, axis=0, keepdims=True)

            def bwd(r0, nrolls, cs=cs, taps=taps):
                rows = pl.ds(r0, RG)
                g = sd_ref[rows, cs]
                shifted, arolls = _shifted_up(g[0:8], g[8:16], nrolls, K, sub)
                da, db = zero, zero
                for k in range(K):
                    ua, ub = shifted[K - 1 - k]
                    da = da + taps[k] * ua
                    db = db + taps[k] * ub
                dx_ref[rows, cs] = jnp.concatenate([da, db], axis=0).astype(bf16)
                return arolls

            halo = grad_y(dyn_ref[:, cs], cn_ref[:, cs] if n_act else None)
            _groups(ts, bwd, _rolls_up(jnp.where(last, 0.0, halo), K), reverse=True)

    nxt = lambda j, i: (jnp.minimum((i + 1) * (ts // 8), S // 8 - 1), j)
    tile = pl.BlockSpec((ts, bw), lambda j, i: (i, j))
    acts = [] if act_c is None else [act_c, act_c]
    extra = [] if into is None else [into]
    n_in = 5 + n_act
    return pl.pallas_call(
        body, name=name, grid=(C // bw, nr),
        in_specs=[tile, pl.BlockSpec((8, bw), nxt)] + ([tile, pl.BlockSpec((8, bw), nxt)] if n_act else []) + [
            pl.BlockSpec((ts, bw), lambda j, i: (i, xcb + j)),
            pl.BlockSpec((8, bw), lambda j, i: (jnp.maximum(i * (ts // 8) - 1, 0), xcb + j)),
            pl.BlockSpec((8, bw), lambda j, i: (0, wcb + j))] + [pl.BlockSpec(memory_space=pl.ANY)] * len(extra),
        out_specs=[pl.BlockSpec((ts, bw), lambda j, i: (i, ocb + j)), pl.BlockSpec((8, bw), lambda j, i: (0, j))],
        out_shape=[jax.ShapeDtypeStruct((S, out_cols), bf16), jax.ShapeDtypeStruct((8, C), f32)],
        scratch_shapes=[pltpu.VMEM((ts, bw), f32)],
        input_output_aliases={n_in: 0} if extra else {},
        compiler_params=_cp("parallel", "arbitrary"),
    )(dy, dy, *acts, x, x, w8, *extra)


def gnorm_fwd(y, proj, w, ts):
    S = y.shape[0]

    def body(y_ref, z_ref, w_ref, o_ref):
        for k in range(NG):
            sl = slice(k * GW, (k + 1) * GW)
            wv = w_ref[:, sl]

            def grp(r0, _, sl=sl, wv=wv):
                rows = pl.ds(r0, NRG)
                z = z_ref[rows, sl].astype(f32)
                g = y_ref[rows, sl].astype(f32) * z * _sigmoid_fast(z)
                o_ref[rows, sl] = (g * _rms(g) * wv).astype(bf16)

            _groups(ts, grp, rg=NRG)

    row = pl.BlockSpec((ts, DI), lambda i: (i, 0))
    return pl.pallas_call(
        body, name="gnorm_fwd", grid=(S // ts,),
        in_specs=[row, row, pl.BlockSpec((1, DI), lambda i: (0, 0))],
        out_specs=row, out_shape=jax.ShapeDtypeStruct((S, DI), bf16),
        compiler_params=_cp("parallel"),
    )(y, proj, w)


def gnorm_bwd(dyn, y, proj, w, dproj, ts):
    S = y.shape[0]
    nt = S // ts

    def body(d_ref, y_ref, z_ref, w_ref, _, dy_ref, dz_ref, gw_ref, acc_ref):
        i = pl.program_id(0)

        @pl.when(i == 0)
        def _():
            acc_ref[...] = jnp.zeros_like(acc_ref)
            gw_ref[...] = jnp.zeros_like(gw_ref)

        for k in range(NG):
            sl = slice(k * GW, (k + 1) * GW)
            wv = w_ref[:, sl]

            def grp(r0, _, sl=sl, wv=wv):
                rows = pl.ds(r0, NRG)
                z = z_ref[rows, sl].astype(f32)
                yv = y_ref[rows, sl].astype(f32)
                s = _sigmoid_fast(z)
                sz = z * s
                g = yv * sz
                r = _rms(g)
                d = d_ref[rows, sl].astype(f32)
                acc_ref[:, sl] += _fold(d * g * r)
                dg = _rms_bwd(g, r, d * wv)
                dy_ref[rows, sl] = (dg * sz).astype(bf16)
                dz_ref[rows, sl] = (dg * yv * s * (1.0 + z * (1.0 - s))).astype(bf16)

            _groups(ts, grp, rg=NRG)

        @pl.when(i == nt - 1)
        def _():
            _flush(acc_ref, gw_ref, 0)

    row = pl.BlockSpec((ts, DI), lambda i: (i, 0))
    return pl.pallas_call(
        body, name="gnorm_bwd", grid=(nt,),
        in_specs=[row, row, row, pl.BlockSpec((1, DI), lambda i: (0, 0)), pl.BlockSpec(memory_space=pl.ANY)],
        out_specs=[row, row, pl.BlockSpec((8, DI), lambda i: (0, 0))],
        out_shape=[jax.ShapeDtypeStruct((S, DI), bf16), jax.ShapeDtypeStruct(dproj.shape, bf16),
                   jax.ShapeDtypeStruct((8, DI), f32)],
        scratch_shapes=[pltpu.VMEM((8, DI), f32)],
        input_output_aliases={4: 1},
        compiler_params=_cp("arbitrary"),
    )(dyn, y, proj, w, dproj)


def merge_fwd(proj, ys, ya, ts):
    S = ys.shape[0]

    def body(gs_ref, ga_ref, ys_ref, ya_ref, o_ref):
        for c0 in range(0, D, CW):
            cs = slice(c0, c0 + CW)

            def grp(r0, _, cs=cs):
                rows = pl.ds(r0, NRG)
                o_ref[rows, cs] = (_sigmoid_fast(gs_ref[rows, cs].astype(f32)) * ys_ref[rows, cs].astype(f32)
                                   + _sigmoid_fast(ga_ref[rows, cs].astype(f32)) * ya_ref[rows, cs].astype(f32)
                                   ).astype(bf16)

            _groups(ts, grp, rg=NRG)

    row = pl.BlockSpec((ts, D), lambda i: (i, 0))
    return pl.pallas_call(
        body, name="merge_fwd", grid=(S // ts,),
        in_specs=[pl.BlockSpec((ts, D), lambda i: (i, C_GS // D)), pl.BlockSpec((ts, D), lambda i: (i, C_GA // D)), row, row],
        out_specs=row, out_shape=jax.ShapeDtypeStruct((S, D), bf16),
        compiler_params=_cp("parallel"),
    )(proj, proj, ys, ya)


def merge_bwd(dm, proj, ys, ya, ts):
    S = ys.shape[0]

    def body(d_ref, gs_ref, ga_ref, ys_ref, ya_ref, dys_ref, dya_ref, dg_ref):
        for c0 in range(0, D, CW):
            cs = slice(c0, c0 + CW)

            def grp(r0, _, c0=c0, cs=cs):
                rows = pl.ds(r0, NRG)
                d = d_ref[rows, cs].astype(f32)
                ss = _sigmoid_fast(gs_ref[rows, cs].astype(f32))
                sa = _sigmoid_fast(ga_ref[rows, cs].astype(f32))
                dys_ref[rows, cs] = (d * ss).astype(bf16)
                dya_ref[rows, cs] = (d * sa).astype(bf16)
                dg_ref[rows, cs] = (d * ys_ref[rows, cs].astype(f32) * ss * (1.0 - ss)).astype(bf16)
                dg_ref[rows, D + c0:D + c0 + CW] = (d * ya_ref[rows, cs].astype(f32) * sa * (1.0 - sa)).astype(bf16)

            _groups(ts, grp, rg=NRG)

    row = pl.BlockSpec((ts, D), lambda i: (i, 0))
    o = jax.ShapeDtypeStruct((S, D), bf16)
    return pl.pallas_call(
        body, name="merge_bwd", grid=(S // ts,),
        in_specs=[row, pl.BlockSpec((ts, D), lambda i: (i, C_GS // D)), pl.BlockSpec((ts, D), lambda i: (i, C_GA // D)), row, row],
        out_specs=[row, row, pl.BlockSpec((ts, 2 * D), lambda i: (i, C_GS // (2 * D)))],
        out_shape=[o, o, jax.ShapeDtypeStruct((S, PM), bf16)],
        compiler_params=_cp("parallel"),
    )(dm, proj, proj, ys, ya)


def _ssd_consts():
    h = lax.broadcasted_iota(jnp.int32, (LANES, DI), 0)
    c = lax.broadcasted_iota(jnp.int32, (LANES, DI), 1)
    expand = (c // HD == h).astype(bf16)
    r = lax.broadcasted_iota(jnp.int32, (CH, CH), 0)
    cc = lax.broadcasted_iota(jnp.int32, (CH, CH), 1)
    tril = (cc <= r).astype(bf16)
    triu = (cc >= r).astype(bf16)
    return expand, expand.T, tril, triu


def _ssd_common(xbc_ref, dtr_ref, bias_ref, alog_ref, tril_ref, expand_ref=None, saved=None):
    dtr = dtr_ref[...] + bias_ref[...]
    dt = jnp.maximum(dtr, 0.0) + jnp.log1p(jnp.exp(-jnp.abs(dtr)))
    a = -jnp.exp(alog_ref[...])
    acs = _dot3_left(tril_ref[...], dt * a)
    if saved is None:
        acsx = _dot3_right(acs, expand_ref[...])
        dtx = _dot3_right(dt, expand_ref[...])
    else:
        acsx, dtx = saved[0][...], saved[1][...]
    x = xbc_ref[:, 0:DI].astype(f32)
    xdt = x * dtx
    e = jnp.exp(acsx)
    dsx = jnp.exp(acsx[CH - 1:CH, :] - acsx)
    return dtr, dt, a, acs, acsx, dtx, x, xdt, e, dsx


def _head_halves():
    first = lax.broadcasted_iota(jnp.int32, (CH, LANES), 1) < HD
    return first, jnp.logical_not(first)


def _ssd_lmat(acs, acs_t, hh, causal):
    seg = acs[:, hh:hh + 1] - acs_t[hh:hh + 1, :]
    return jnp.where(causal, jnp.exp(jnp.minimum(seg, 0.0)), 0.0)


def ssd_fwd(xbc, dtr, bias, alog, dx_row, comm=None):
    S = xbc.shape[0]
    nc = S // CH
    expand, _, tril, _ = _ssd_consts()
    cm = _Comm(comm)

    def body(*refs):
        ins, (y_ref, hp_ref, ax_ref, dtx_ref), (h_ref, yd_ref), copies = cm.split(refs, 7, 4)
        xbc_ref, dtr_ref, bias_ref, alog_ref, dxr_ref, expand_ref, tril_ref = ins
        c = pl.program_id(0)
        cm.start(copies, c == 0)

        @pl.when(c == 0)
        def _():
            h_ref[...] = jnp.zeros_like(h_ref)

        _, _, _, acs, acsx, dtx, x, xdt, e, dsx = _ssd_common(xbc_ref, dtr_ref, bias_ref, alog_ref, tril_ref,
                                                              expand_ref=expand_ref)
        ax_ref[...] = acsx
        dtx_ref[...] = dtx
        acs_t = acs.T
        xb = xdt.astype(bf16)
        xd = (xdt * dsx).astype(bf16)
        causal = tril_ref[...] > 0
        halves = _head_halves()
        for g in range(NG):
            gs = slice(g * GW, (g + 1) * GW)
            bg = xbc_ref[:, DI + g * NS:DI + (g + 1) * NS]
            cg = xbc_ref[:, DI + NG * NS + g * NS:DI + NG * NS + (g + 1) * NS]
            cb = _dot(cg, bg, "nt")
            hp = h_ref[g]
            hpb = hp.astype(bf16)
            hp_ref[0, g] = hpb
            yd_ref[:, gs] = _dot(cg, hpb) * e[:, gs]
            h_ref[g] = hp * e[CH - 1:CH, gs] + _dot(bg, xd[:, gs], "tn")
            for k in range(NH // NG // 2):
                hh = g * (NH // NG) + 2 * k
                ps = slice(hh * HD, (hh + 2) * HD)
                xp = xb[:, ps]
                acc = None
                for o in range(2):
                    m = (cb * _ssd_lmat(acs, acs_t, hh + o, causal)).astype(bf16)
                    part = _dot(m, jnp.where(halves[o], xp, jnp.zeros_like(xp)))
                    acc = part if acc is None else acc + part
                yd_ref[:, ps] += acc
        y_ref[...] = (yd_ref[...] + dxr_ref[...] * x).astype(bf16)
        cm.wait(copies, c == nc - 1)

    par = lambda shape: pl.BlockSpec(shape, lambda c: (0,) * len(shape))
    res = pl.pallas_call(
        body, name="ssd_fwd", grid=(nc,),
        in_specs=[pl.BlockSpec((CH, CONVD), lambda c: (c, 0)), pl.BlockSpec((CH, LANES), lambda c: (c, 0)),
                  par((1, LANES)), par((1, LANES)), par((1, DI)), par((LANES, DI)), par((CH, CH))] + cm.in_specs(),
        out_specs=[pl.BlockSpec((CH, DI), lambda c: (c, 0)), pl.BlockSpec((1, NG, NS, GW), lambda c: (c, 0, 0, 0)),
                   pl.BlockSpec((CH, DI), lambda c: (c, 0)), pl.BlockSpec((CH, DI), lambda c: (c, 0))] + cm.out_specs(),
        out_shape=[jax.ShapeDtypeStruct((S, DI), bf16), jax.ShapeDtypeStruct((nc, NG, NS, GW), bf16),
                   jax.ShapeDtypeStruct((S, DI), f32), jax.ShapeDtypeStruct((S, DI), f32)] + cm.out_shape(),
        scratch_shapes=[pltpu.VMEM((NG, NS, GW), f32), pltpu.VMEM((CH, DI), f32)] + cm.scratch(),
        compiler_params=_cp("arbitrary", side_effects=bool(cm.n)),
    )(xbc, dtr, bias, alog, dx_row, expand, tril, *cm.bufs)
    return res[0], res[1], (res[2], res[3]), res[4:]


def ssd_bwd(xbc, dtr, dy, hprev, saved, bias, alog, dx_row, comm=None):
    S = xbc.shape[0]
    nc = S // CH
    _, expand_t, tril, triu = _ssd_consts()
    cm = _Comm(comm)

    def body(*refs):
        ins, outs, scr, copies = cm.split(refs, 12, 3)
        xbc_ref, dtr_ref, dy_ref, hp_ref, ax_ref, dtx_ref, bias_ref, alog_ref, dxr_ref, expt_ref, tril_ref, triu_ref = ins
        dxbc_ref, ddtr_ref, acc_ref = outs
        dh_ref, dxs_ref, t_ref, accb_ref, acca_ref, accd_ref, cc_ref, rr_ref = scr
        c = pl.program_id(0)
        cm.start(copies, c == 0)

        @pl.when(c == 0)
        def _():
            cc_ref[...] = jnp.zeros_like(cc_ref)
            rr_ref[...] = jnp.zeros_like(rr_ref)

        @pl.when(c == 0)
        def _():
            dh_ref[...] = jnp.zeros_like(dh_ref)
            accb_ref[...] = jnp.zeros_like(accb_ref)
            acca_ref[...] = jnp.zeros_like(acca_ref)
            accd_ref[...] = jnp.zeros_like(accd_ref)

        dtr, dt, a, acs, _, dtx, x, xdt, e, dsx = _ssd_common(xbc_ref, dtr_ref, bias_ref, alog_ref, tril_ref,
                                                              saved=(ax_ref, dtx_ref))
        acs_t = acs.T
        xb = xdt.astype(bf16)
        xdf = xdt * dsx
        xd = xdf.astype(bf16)
        dyv = dy_ref[...].astype(f32)
        dyb = dy_ref[...]
        dye = (dyv * e).astype(bf16)
        causal = tril_ref[...] > 0
        halves = _head_halves()
        last_row = lax.broadcasted_iota(jnp.int32, (CH, 1), 0) == CH - 1
        for g in range(NG):
            gs = slice(g * GW, (g + 1) * GW)
            bsl = slice(DI + g * NS, DI + (g + 1) * NS)
            csl = slice(DI + NG * NS + g * NS, DI + NG * NS + (g + 1) * NS)
            bg = xbc_ref[:, bsl]
            cg = xbc_ref[:, csl]
            cb = _dot(cg, bg, "nt")
            hpb = hp_ref[0, g]
            dhn = dh_ref[g]
            dhnb = dhn.astype(bf16)
            yoff = _dot(cg, hpb) * e[:, gs]
            dxd = _dot(bg, dhnb)
            t2 = dxd * xdf[:, gs]
            t3 = jnp.sum(dhn * hpb.astype(f32), axis=0, keepdims=True) * e[CH - 1:CH, gs]
            t_ref[:, gs] = dyv[:, gs] * yoff - t2 + jnp.where(last_row, jnp.sum(t2, axis=0, keepdims=True) + t3, 0.0)
            dxs_ref[:, gs] = dxd * dsx[:, gs]
            dcg = _dot(dye[:, gs], hpb, "nt")
            dbg = _dot(xd[:, gs], dhnb, "nt")
            dh_ref[g] = dhn * e[CH - 1:CH, gs] + _dot(cg, dye[:, gs], "tn")
            dcb = jnp.zeros((CH, CH), f32)
            for k in range(NH // NG // 2):
                hh0 = g * (NH // NG) + 2 * k
                ps = slice(hh0 * HD, (hh0 + 2) * HD)
                xp = xb[:, ps]
                dyp = dyb[:, ps]
                acc = None
                for o in range(2):
                    hh = hh0 + o
                    dyh = jnp.where(halves[o], dyp, jnp.zeros_like(dyp))
                    lm = _ssd_lmat(acs, acs_t, hh, causal)
                    m = cb * lm
                    dm = _dot(dyh, xp, "nt")
                    gm = dm * m
                    cc_ref[:, hh:hh + 1] = jnp.sum(gm, axis=1, keepdims=True)
                    rr_ref[hh:hh + 1, :] = jnp.sum(gm, axis=0, keepdims=True)
                    dcb = dcb + dm * lm
                    part = _dot(m.astype(bf16), dyh, "tn")
                    acc = part if acc is None else acc + part
                dxs_ref[:, ps] += acc
            dcbb = dcb.astype(bf16)
            dxbc_ref[:, csl] = (dcg + _dot(dcbb, bg)).astype(bf16)
            dxbc_ref[:, bsl] = (dbg + _dot(dcbb, cg, "tn")).astype(bf16)
        dxf = dxs_ref[...]
        dxbc_ref[:, 0:DI] = (dxf * dtx + dxr_ref[...] * dyv).astype(bf16)
        expt = expt_ref[...]
        dacs = cc_ref[...] - rr_ref[...].T + _dot2_right(t_ref[...], expt)
        dadt = _dot3_left(triu_ref[...], dacs)
        ddt = _dot2_right(dxf * x, expt) + dadt * a
        ddtr = ddt * _sigmoid(dtr)
        ddtr_ref[...] = ddtr
        accb_ref[...] += ddtr
        acca_ref[...] += dadt * dt
        accd_ref[...] += _dot2_right(dyv * x, expt)

        @pl.when(c == nc - 1)
        def _():
            acc_ref[...] = jnp.zeros_like(acc_ref)
            acc_ref[0:1, :] = jnp.sum(accb_ref[...], axis=0, keepdims=True)
            acc_ref[1:2, :] = jnp.sum(acca_ref[...], axis=0, keepdims=True) * a
            acc_ref[2:3, :] = jnp.sum(accd_ref[...], axis=0, keepdims=True)

        cm.wait(copies, c == nc - 1)

    par = lambda shape: pl.BlockSpec(shape, lambda c: (0,) * len(shape))
    rev = lambda c: (nc - 1 - c, 0)
    res = pl.pallas_call(
        body, name="ssd_bwd", grid=(nc,),
        in_specs=[pl.BlockSpec((CH, CONVD), rev), pl.BlockSpec((CH, LANES), rev), pl.BlockSpec((CH, DI), rev),
                  pl.BlockSpec((1, NG, NS, GW), lambda c: (nc - 1 - c, 0, 0, 0)),
                  pl.BlockSpec((CH, DI), rev), pl.BlockSpec((CH, DI), rev),
                  par((1, LANES)), par((1, LANES)), par((1, DI)), par((DI, LANES)),
                  par((CH, CH)), par((CH, CH))] + cm.in_specs(),
        out_specs=[pl.BlockSpec((CH, CONVD), rev), pl.BlockSpec((CH, LANES), rev), par((8, LANES))] + cm.out_specs(),
        out_shape=[jax.ShapeDtypeStruct((S, CONVD), bf16), jax.ShapeDtypeStruct((S, LANES), f32),
                   jax.ShapeDtypeStruct((8, LANES), f32)] + cm.out_shape(),
        scratch_shapes=[pltpu.VMEM((NG, NS, GW), f32), pltpu.VMEM((CH, DI), f32), pltpu.VMEM((CH, DI), f32),
                        pltpu.VMEM((CH, LANES), f32), pltpu.VMEM((CH, LANES), f32), pltpu.VMEM((CH, LANES), f32),
                        pltpu.VMEM((CH, LANES), f32), pltpu.VMEM((LANES, CH), f32)] + cm.scratch(),
        compiler_params=_cp("arbitrary", side_effects=bool(cm.n)),
    )(xbc, dtr, dy, hprev, *saved, bias, alog, dx_row, expand_t, tril, triu, *cm.bufs)
    return res[0], res[1], res[2], res[3:]


def _partner(t):
    half = AD // 2
    return jnp.concatenate([t[h * AD + o:h * AD + o + half] for h in range(t.shape[0] // AD) for o in (half, 0)], axis=0)


def _rope(t, cos, sin):
    reps = t.shape[0] // AD
    return t * jnp.tile(cos, (reps, 1)) + _partner(t) * jnp.tile(sin, (reps, 1))


def _rope_t(d, cos, sin):
    reps = d.shape[0] // AD
    return d * jnp.tile(cos, (reps, 1)) - _partner(d) * jnp.tile(sin, (reps, 1))


def _lanes_of_group(t, g):
    return jnp.concatenate([t[(g * REP + r) * AD:(g * REP + r + 1) * AD] for r in range(REP)], axis=1)


def _attn_probs(qg, k2, sink_ref, g, not_first):
    n = qg.shape[1]
    s = lax.broadcasted_iota(jnp.int32, (2 * WIN, n), 0)
    t = lax.broadcasted_iota(jnp.int32, (2 * WIN, n), 1) % WIN
    valid = jnp.logical_or(jnp.logical_and(jnp.logical_and(s < WIN, s > t), not_first),
                           jnp.logical_and(s >= WIN, s - WIN <= t))
    sink = jnp.concatenate([jnp.broadcast_to(sink_ref[0:1, g * REP + r:g * REP + r + 1], (1, WIN)) for r in range(REP)],
                           axis=1)
    sc = jnp.where(valid, _dot(k2, qg, "tn"), -1e30)
    m = jnp.maximum(jnp.max(sc, axis=0, keepdims=True), sink)
    p = jnp.exp(sc - m)
    ps = jnp.exp(sink - m)
    inv = 1.0 / (jnp.sum(p, axis=0, keepdims=True) + ps)
    return p * inv, ps * inv


def attn_fwd(qt, kvt, cos, sin, sinks):
    S = qt.shape[1]
    nb = S // WIN
    cur = lambda i: (0, i)
    prev = lambda i: (0, jnp.maximum(2 * i - 1, 0))

    def body(q_ref, kv_ref, kvp_ref, cos_ref, sin_ref, cosp_ref, sinp_ref, sink_ref, o_ref):
        i = pl.program_id(0)
        q = (_rope(q_ref[...].astype(f32), cos_ref[...], sin_ref[...]) * (AD ** -0.5)).astype(bf16)
        kc = _rope(kv_ref[0:KVW, :].astype(f32), cos_ref[...], sin_ref[...]).astype(bf16)
        kp = _rope(kvp_ref[0:KVW, :].astype(f32), cosp_ref[...], sinp_ref[...]).astype(bf16)
        k3 = jnp.concatenate([kp, kc], axis=1)
        for g in range(KVH):
            ks = slice(g * AD, (g + 1) * AD)
            vs = slice(KVW + g * AD, KVW + (g + 1) * AD)
            v3 = jnp.concatenate([kvp_ref[vs, :], kv_ref[vs, :]], axis=1)
            for b in range(2):
                win = slice(b * WIN, (b + 2) * WIN)
                qg = _lanes_of_group(q[:, b * WIN:(b + 1) * WIN], g)
                p, _ = _attn_probs(qg, k3[ks, win], sink_ref, g, jnp.logical_or(i > 0, b > 0))
                o = _dot(v3[:, win], p.astype(bf16))
                for r in range(REP):
                    h = g * REP + r
                    o_ref[h * AD:(h + 1) * AD, b * WIN:(b + 1) * WIN] = o[:, r * WIN:(r + 1) * WIN].astype(bf16)

    tab = pl.BlockSpec((AD, 2 * WIN), cur)
    tabp = pl.BlockSpec((AD, WIN), prev)
    return pl.pallas_call(
        body, name="attn_fwd", grid=(nb // 2,),
        in_specs=[pl.BlockSpec((D, 2 * WIN), cur), pl.BlockSpec((2 * KVW, 2 * WIN), cur),
                  pl.BlockSpec((2 * KVW, WIN), prev), tab, tab, tabp, tabp, pl.BlockSpec((1, LANES), lambda i: (0, 0))],
        out_specs=pl.BlockSpec((D, 2 * WIN), cur),
        out_shape=jax.ShapeDtypeStruct((D, S), bf16),
        compiler_params=_cp("parallel"),
    )(qt, kvt, kvt, cos, sin, cos, sin, sinks)


def attn_bwd(qt, kvt, cos, sin, sinks, daot, comm=None):
    S = qt.shape[1]
    nb = S // WIN
    cur = lambda i: (0, jnp.minimum(i, nb - 1))
    prev = lambda i: (0, jnp.maximum(i - 1, 0))
    cm = _Comm(comm)

    def body(*refs):
        ins, (dq_ref, dkv_ref, ds_ref), scr, copies = cm.split(refs, 9, 3)
        q_ref, kv_ref, kvp_ref, cos_ref, sin_ref, cosp_ref, sinp_ref, sink_ref, do_ref = ins
        ck_ref, cv_ref, dqs_ref, dkp_ref, dvp_ref, dkc_ref, dvc_ref, accs_ref = scr
        i = pl.program_id(0)
        cm.start(copies, i == 0)

        @pl.when(i == 0)
        def _():
            ck_ref[...] = jnp.zeros_like(ck_ref)
            cv_ref[...] = jnp.zeros_like(cv_ref)
            accs_ref[...] = jnp.zeros_like(accs_ref)

        @pl.when(i == nb)
        def _():
            dkp_ref[...] = jnp.zeros_like(dkp_ref)
            dvp_ref[...] = jnp.zeros_like(dvp_ref)

        @pl.when(i < nb)
        def _():
            q = (_rope(q_ref[...].astype(f32), cos_ref[...], sin_ref[...]) * (AD ** -0.5)).astype(bf16)
            kc = _rope(kv_ref[0:KVW, :].astype(f32), cos_ref[...], sin_ref[...]).astype(bf16)
            kp = _rope(kvp_ref[0:KVW, :].astype(f32), cosp_ref[...], sinp_ref[...]).astype(bf16)
            do = do_ref[...]
            for g in range(KVH):
                ks = slice(g * AD, (g + 1) * AD)
                vs = slice(KVW + g * AD, KVW + (g + 1) * AD)
                qg = _lanes_of_group(q, g)
                dog = _lanes_of_group(do, g)
                k2 = jnp.concatenate([kp[ks], kc[ks]], axis=1)
                v2 = jnp.concatenate([kvp_ref[vs, :], kv_ref[vs, :]], axis=1)
                p, ps = _attn_probs(qg, k2, sink_ref, g, i > 0)
                dp = _dot(v2, dog, "tn")
                delta = jnp.sum(p * dp, axis=0, keepdims=True)
                ds = (p * (dp - delta)).astype(bf16)
                accs_ref[g:g + 1, :] -= ps * delta
                dqg = _dot(k2, ds) * (AD ** -0.5)
                for r in range(REP):
                    h = g * REP + r
                    dqs_ref[h * AD:(h + 1) * AD, :] = dqg[:, r * WIN:(r + 1) * WIN]
                dk2 = _dot(qg, ds, "nt")
                dv2 = _dot(dog, p.astype(bf16), "nt")
                dkp_ref[ks, :] = dk2[:, 0:WIN]
                dkc_ref[ks, :] = dk2[:, WIN:2 * WIN]
                dvp_ref[ks, :] = dv2[:, 0:WIN]
                dvc_ref[ks, :] = dv2[:, WIN:2 * WIN]
            dq_ref[...] = _rope_t(dqs_ref[...], cos_ref[...], sin_ref[...]).astype(bf16)

        dkv_ref[0:KVW, :] = _rope_t(ck_ref[...] + dkp_ref[...], cosp_ref[...], sinp_ref[...]).astype(bf16)
        dkv_ref[KVW:2 * KVW, :] = (cv_ref[...] + dvp_ref[...]).astype(bf16)

        @pl.when(i < nb)
        def _():
            ck_ref[...] = dkc_ref[...]
            cv_ref[...] = dvc_ref[...]

        @pl.when(i == nb)
        def _():
            lane = lax.broadcasted_iota(jnp.int32, (1, LANES), 1)
            row = jnp.zeros((1, LANES), f32)
            for h in range(AH):
                part = accs_ref[h // REP:h // REP + 1, (h % REP) * WIN:(h % REP + 1) * WIN]
                row = row + jnp.where(lane == h, jnp.sum(part, axis=1, keepdims=True), 0.0)
            ds_ref[...] = jnp.zeros_like(ds_ref)
            ds_ref[0:1, :] = row

        cm.wait(copies, i == nb)

    tab = pl.BlockSpec((AD, WIN), cur)
    tabp = pl.BlockSpec((AD, WIN), prev)
    kvs = lambda: pltpu.VMEM((KVW, WIN), f32)
    res = pl.pallas_call(
        body, name="attn_bwd", grid=(nb + 1,),
        in_specs=[pl.BlockSpec((D, WIN), cur), pl.BlockSpec((2 * KVW, WIN), cur), pl.BlockSpec((2 * KVW, WIN), prev),
                  tab, tab, tabp, tabp, pl.BlockSpec((1, LANES), lambda i: (0, 0)),
                  pl.BlockSpec((D, WIN), cur)] + cm.in_specs(),
        out_specs=[pl.BlockSpec((D, WIN), cur), pl.BlockSpec((2 * KVW, WIN), prev),
                   pl.BlockSpec((8, LANES), lambda i: (0, 0))] + cm.out_specs(),
        out_shape=[jax.ShapeDtypeStruct((D, S), bf16), jax.ShapeDtypeStruct((2 * KVW, S), bf16),
                   jax.ShapeDtypeStruct((8, LANES), f32)] + cm.out_shape(),
        scratch_shapes=[kvs(), kvs(), pltpu.VMEM((D, WIN), f32), kvs(), kvs(), kvs(), kvs(),
                        pltpu.VMEM((8, REP * WIN), f32)] + cm.scratch(),
        compiler_params=_cp("arbitrary", side_effects=bool(cm.n)),
    )(qt, kvt, kvt, cos, sin, cos, sin, sinks, daot, *cm.bufs)
    return res[0], res[1], res[2], res[3:]


ADAM_C1 = 1.0 / (1.0 - ADAM_B1 ** ADAM_STEP)
ADAM_C2 = 1.0 / (1.0 - ADAM_B2 ** ADAM_STEP)


def _adam_update(g, w, m, v):
    nm = ADAM_B1 * m + (1.0 - ADAM_B1) * g
    nv = ADAM_B2 * v + (1.0 - ADAM_B2) * (g * g)
    return -ADAM_LR * ((nm * ADAM_C1) / (jnp.sqrt(nv * ADAM_C2) + ADAM_EPS) + ADAM_WD * w), nm, nv


def adamw(parts, w, m, v, tr, name):
    n, R, C = parts.shape

    def body(p_ref, w_ref, m_ref, v_ref, g_ref, d_ref, nm_ref, nv_ref):
        def grp(g0, _):
            r0 = pl.multiple_of(g0 * RG, RG)
            rows = pl.ds(r0, RG)
            g = p_ref[0, rows, :].astype(f32)
            for k in range(1, n):
                g = g + p_ref[k, rows, :].astype(f32)
            d, nm, nv = _adam_update(g, w_ref[rows, :], m_ref[rows, :], v_ref[rows, :])
            g_ref[rows, :] = g
            d_ref[rows, :] = d
            nm_ref[rows, :] = nm
            nv_ref[rows, :] = nv
            return 0

        lax.fori_loop(0, tr // RG, grp, 0)

    row = pl.BlockSpec((tr, C), lambda i: (i, 0))
    o = jax.ShapeDtypeStruct((R, C), f32)
    return pl.pallas_call(
        body, name=name, grid=(R // tr,),
        in_specs=[pl.BlockSpec((n, tr, C), lambda i: (0, i, 0)), row, row, row],
        out_specs=[row, row, row, row], out_shape=[o, o, o, o],
        compiler_params=_cp("parallel"),
    )(parts, w, m, v)


SMALL_ROW = (("norm_mix_post_w", D), ("norm_ffn_pre_w", D), ("norm_ffn_post_w", D), ("ssd_norm_w", DI),
             ("ssd_conv_b", CONVD), ("ffn_conv_b", 2 * FF), ("ssd_dt_bias", NH), ("ssd_a_log", NH), ("ssd_d", NH),
             ("attn_sinks", AH), ("loss", 1))
CONV_BLOCK = 1152
SSD_CONV_COLS = CONVD // N_DEV
FFN_CONV_COLS = 2 * FF // N_DEV


def _row_offsets():
    off, o = {}, 0
    for name, n in SMALL_ROW:
        off[name] = (o, n)
        o += -(-n // LANES) * LANES
    return off, o


def adamw_small(recv_row, recv_pre, recv_conv, params):
    off, _ = _row_offsets()
    names = list(params)
    n = len(names)

    def total(ref, rows, lo, width):
        g = ref[0, rows, lo:lo + width]
        for d in range(1, N_DEV):
            g = g + ref[d, rows, lo:lo + width]
        return g

    def grad_of(name, row_ref, pre_ref, conv_ref):
        if name == "norm_mix_pre_w":
            return total(pre_ref, slice(0, 1), 0, D)
        if name == "ssd_conv_w":
            return total(conv_ref, slice(0, SSD_K), 0, SSD_CONV_COLS)
        if name == "ffn_conv_w":
            return total(conv_ref, slice(0, FFN_K), 3 * LANES, FFN_CONV_COLS)
        o, width = off[name]
        return total(row_ref, slice(0, 1), o, width)

    def body(row_ref, pre_ref, conv_ref, *refs):
        ins, outs = refs[:3 * n], refs[3 * n:]
        for k, name in enumerate(names):
            w_ref, m_ref, v_ref = ins[3 * k:3 * k + 3]
            g_ref, d_ref, nm_ref, nv_ref = outs[4 * k:4 * k + 4]
            g = grad_of(name, row_ref, pre_ref, conv_ref)
            d, nm, nv = _adam_update(g, w_ref[...], m_ref[...], v_ref[...])
            g_ref[...] = g
            d_ref[...] = d
            nm_ref[...] = nm
            nv_ref[...] = nv
        outs[4 * n][...] = total(row_ref, slice(0, 1), off["loss"][0], LANES)

    flat = [t for name in names for t in params[name]]
    out_shape = [jax.ShapeDtypeStruct(params[name][0].shape, f32) for name in names for _ in range(4)]
    res = pl.pallas_call(
        body, name="adamw_small",
        out_shape=out_shape + [jax.ShapeDtypeStruct((1, LANES), f32)],
        compiler_params=pltpu.CompilerParams(vmem_limit_bytes=VMEM_LIMIT),
    )(recv_row, recv_pre, recv_conv, *flat)
    return {name: res[4 * k:4 * k + 4] for k, name in enumerate(names)}, res[4 * n]


def _cat_rows(parts):
    words = [lax.bitcast_convert_type(p, jnp.uint16) for p in parts]
    return lax.bitcast_convert_type(jnp.concatenate(words, axis=0), bf16)


def _pad_rows8(w):
    return jnp.pad(w, ((0, 8 - w.shape[0]), (0, 0)))


def _pad_lanes(v):
    return jnp.pad(v.reshape(1, -1), ((0, 0), (0, LANES - v.size)))


WEIGHTS = ('norm_mix_pre_w', 'w_in', 'ssd_conv_w', 'ssd_conv_b', 'ssd_dt_bias', 'ssd_a_log', 'ssd_d', 'ssd_norm_w',
           'ssd_w_out', 'attn_sinks', 'attn_w_out', 'w_mix_out', 'norm_mix_post_w', 'norm_ffn_pre_w', 'ffn_w_up',
           'ffn_conv_w', 'ffn_conv_b', 'ffn_w_down', 'norm_ffn_post_w')
W_IN_ROWS = IN_DIM // N_DEV
W_IN_PAD = 1104
W_IN_SPLIT = (672, 768, 832)
TS = 512
TS_NORM = 1024


def kernel(x, positions, norm_mix_pre_w, w_in, ssd_conv_w, ssd_conv_b, ssd_dt_bias, ssd_a_log, ssd_d, ssd_norm_w, ssd_w_out, attn_sinks, attn_w_out, w_mix_out, norm_mix_post_w, norm_ffn_pre_w, ffn_w_up, ffn_conv_w, ffn_conv_b, ffn_w_down, norm_ffn_post_w, loss_target, m_norm_mix_pre_w, m_w_in, m_ssd_conv_w, m_ssd_conv_b, m_ssd_dt_bias, m_ssd_a_log, m_ssd_d, m_ssd_norm_w, m_ssd_w_out, m_attn_sinks, m_attn_w_out, m_w_mix_out, m_norm_mix_post_w, m_norm_ffn_pre_w, m_ffn_w_up, m_ffn_conv_w, m_ffn_conv_b, m_ffn_w_down, m_norm_ffn_post_w, v_norm_mix_pre_w, v_w_in, v_ssd_conv_w, v_ssd_conv_b, v_ssd_dt_bias, v_ssd_a_log, v_ssd_d, v_ssd_norm_w, v_ssd_w_out, v_attn_sinks, v_attn_w_out, v_w_mix_out, v_norm_mix_post_w, v_norm_ffn_pre_w, v_ffn_w_up, v_ffn_conv_w, v_ffn_conv_b, v_ffn_w_down, v_norm_ffn_post_w):
    a = locals()
    r2 = lambda t: t.reshape(t.shape[-2], t.shape[-1])
    w = {n: r2(a[n]) for n in WEIGHTS}
    m = {n: r2(a["m_" + n]) for n in WEIGHTS}
    v = {n: r2(a["v_" + n]) for n in WEIGHTS}
    xs, target = x[0], loss_target[0]
    S = xs.shape[0]
    ts, tsn = TS, min(TS_NORM, S)

    w_in_blk = jnp.pad(w["w_in"].T.astype(bf16), ((0, W_IN_PAD - W_IN_ROWS), (0, 0)))
    conv_blk = jnp.concatenate([_pad_rows8(w["ssd_conv_w"]), _pad_rows8(w["ffn_conv_w"]),
                                jnp.zeros((8, CONV_BLOCK - SSD_CONV_COLS - FFN_CONV_COLS), f32)], axis=1)
    u, cos, sin, (g_in, g_conv) = prenorm_fwd(xs, w["norm_mix_pre_w"], positions, ts, [w_in_blk, conv_blk])
    wt = g_in[:, :W_IN_ROWS].reshape(IN_DIM, D)
    w_main_t = _cat_rows([wt[IN_OFF[0]:IN_OFF[1]], wt[IN_OFF[6]:IN_OFF[8]], wt[IN_OFF[1]:IN_OFF[2]]])
    w_q_t = wt[IN_OFF[3]:IN_OFF[4]]
    w_kv_t = wt[IN_OFF[4]:IN_OFF[6]]
    w_dt_t = jnp.pad(wt[IN_OFF[2]:IN_OFF[3]], ((0, LANES - NH), (0, 0)))
    conv_w8 = g_conv[:, :, 0:SSD_CONV_COLS].transpose(1, 0, 2).reshape(8, CONVD)
    fconv_w8 = g_conv[:, :, SSD_CONV_COLS:SSD_CONV_COLS + FFN_CONV_COLS].transpose(1, 0, 2).reshape(8, 2 * FF)
    bias = _pad_lanes(w["ssd_dt_bias"])
    alog = _pad_lanes(w["ssd_a_log"])
    dx_row = jnp.repeat(w["ssd_d"].reshape(-1), HD).reshape(1, DI)
    sinks = _pad_lanes(w["attn_sinks"])

    later = [w["ssd_w_out"].astype(bf16), w["attn_w_out"].astype(bf16), w["w_mix_out"].astype(bf16)]
    proj, (g_so, g_ao, g_mix) = mm(u, w_main_t, "nt", bf16, "mm_proj", comm=(later, (False,) * 3))
    w_ssd_out, w_attn_out, w_mix = g_so.reshape(DI, D), g_ao.reshape(D, D), g_mix.reshape(D, D)
    qt = mm(w_q_t, u, "nt", bf16, "mm_q")
    kvt = mm(w_kv_t, u, "nt", bf16, "mm_kv")
    dtr = mm(u, w_dt_t, "nt", f32, "mm_dt")
    xbc, conv_c = ssdconv_fwd(proj, conv_w8, w["ssd_conv_b"], tsn)
    y, hprev, ssd_saved, (g_up, g_down) = ssd_fwd(xbc, dtr, bias, alog, dx_row, comm=(
        [w["ffn_w_up"].T.astype(bf16), w["ffn_w_down"].astype(bf16)], (False, False)))
    w_up_t = g_up.reshape(2 * FF, D)
    w_down = g_down.reshape(FF, D)
    yn = gnorm_fwd(y, proj, w["ssd_norm_w"], tsn)
    ys = mm(yn, w_ssd_out, "nn", bf16, "mm_ssd_out")
    aot = attn_fwd(qt, kvt, cos, sin, sinks)
    ya = mm(aot, w_attn_out, "tn", bf16, "mm_attn_out")
    merged = merge_fwd(proj, ys, ya, tsn)
    mo = mm(merged, w_mix, "nn", bf16, "mm_mix")
    x1, h = post_fwd(xs, mo, w["norm_mix_post_w"], w["norm_ffn_pre_w"], tsn)
    up = mm(h, w_up_t, "nt", bf16, "mm_up")
    act, gate_c, val_c = ffnact_fwd(up, fconv_w8, w["ffn_conv_b"], ts)
    ff = mm(act, w_down, "nn", bf16, "mm_down")
    loss_blk, dout, dff, g_post2 = loss_head(x1, ff, target, w["norm_ffn_post_w"], tsn)

    dact = mm(dff, w_down, "nt", bf16, "mm_dact")
    gw_down = mm(act, dff, "tn", bf16, "mm_g_down")
    dgate, dval = ffnact_bwd(dact, gate_c, val_c, ts)
    dup_pre, g_fconv_a = dwconv_bwd(dgate, up, 0, fconv_w8, 0, FFN_K, FF, ts, "ffnconv_bwd_gate", out_cols=2 * FF)
    dup_pre, g_fconv_b = dwconv_bwd(dval, up, 1, fconv_w8, 1, FFN_K, FF, ts, "ffnconv_bwd_val", into=dup_pre, ocb=1,
                                    out_cols=2 * FF)
    g_fconv = jnp.concatenate([g_fconv_a, g_fconv_b], axis=1)
    dh, (r_down,) = mm(dup_pre, w_up_t, "nn", bf16, "mm_dh", comm=([gw_down.reshape(N_DEV, FF // N_DEV, D)], (True,)))
    gw_up_t = mm(dup_pre, h, "tn", bf16, "mm_g_up")
    dx1, dmo, g_norms = post_bwd(dout, dh, x1, mo, w["norm_mix_post_w"], w["norm_ffn_pre_w"], tsn)
    dmerged = mm(dmo, w_mix, "nt", bf16, "mm_dmerged")
    gw_mix = mm(merged, dmo, "tn", bf16, "mm_g_mix")
    dys, dya, dproj = merge_bwd(dmerged, proj, ys, ya, tsn)
    daot = mm(w_attn_out, dya, "nt", bf16, "mm_dao")
    gw_attn_out = mm(aot, dya, "nn", bf16, "mm_g_attn_out")
    dqt, dkvt, g_sinks, (r_up,) = attn_bwd(qt, kvt, cos, sin, sinks, daot,
                                           comm=([gw_up_t.reshape(N_DEV, 2 * FF // N_DEV, D)], (True,)))
    dyn = mm(dys, w_ssd_out, "nt", bf16, "mm_dyn")
    gw_ssd_out = mm(yn, dys, "tn", bf16, "mm_g_ssd_out")
    dy, dproj, g_gnorm = gnorm_bwd(dyn, y, proj, w["ssd_norm_w"], dproj, tsn)
    sends = [gw_ssd_out.reshape(N_DEV, DI // N_DEV, D), gw_attn_out.reshape(N_DEV, D // N_DEV, D),
             gw_mix.reshape(N_DEV, D // N_DEV, D)]
    dxbc, ddtr, g_ssd, (r_so, r_ao, r_mix) = ssd_bwd(xbc, dtr, dy, hprev, ssd_saved, bias, alog, dx_row,
                                                     comm=(sends, (True,) * 3))
    dproj, g_conv_w = dwconv_bwd(dxbc, proj, C_XBC // 1024, conv_w8, 0, SSD_K, 1024, tsn, "ssdconv_bwd", act_c=conv_c,
                                 into=dproj, ocb=C_XBC // 1024, out_cols=PM)
    ddtr_b = ddtr.astype(bf16)
    du_c = mm(ddtr_b, w_dt_t, "nn", bf16, "mm_du_dt")
    pieces = {"norm_mix_post_w": g_norms[1:2], "norm_ffn_pre_w": g_norms[0:1], "norm_ffn_post_w": g_post2[0:1],
              "ssd_norm_w": g_gnorm[0:1], "ssd_conv_b": g_conv_w[7:8], "ffn_conv_b": g_fconv[7:8],
              "ssd_dt_bias": g_ssd[0:1], "ssd_a_log": g_ssd[1:2], "ssd_d": g_ssd[2:3], "attn_sinks": g_sinks[0:1],
              "loss": loss_blk[0:1]}
    row = jnp.concatenate([jnp.pad(pieces[n][:, :min(k, pieces[n].shape[1])],
                                   ((0, 0), (0, -(-k // LANES) * LANES - min(k, pieces[n].shape[1]))))
                           for n, k in SMALL_ROW], axis=1)
    send_row = jnp.pad(row, ((0, 7), (0, 0)))
    send_conv = jnp.concatenate(
        [g_conv_w.reshape(8, N_DEV, SSD_CONV_COLS).transpose(1, 0, 2),
         g_fconv.reshape(8, N_DEV, FFN_CONV_COLS).transpose(1, 0, 2),
         jnp.zeros((N_DEV, 8, CONV_BLOCK - SSD_CONV_COLS - FFN_CONV_COLS), f32)], axis=2)
    g_main_t, (recv_row, recv_conv) = mm(dproj, u, "tn", bf16, "mm_g_in", comm=([send_row, send_conv], (False, True)))
    g_q_t = mm(dqt, u, "nn", bf16, "mm_g_q")
    g_kv_t = mm(dkvt, u, "nn", bf16, "mm_g_kv")
    g_dt_t = mm(ddtr_b, u, "tn", bf16, "mm_g_dt")
    g_wt = _cat_rows([g_main_t[C_Z:C_GS], g_main_t[C_XBC:PM], g_dt_t[:NH], g_q_t, g_kv_t, g_main_t[C_GS:C_XBC]])
    send_in = jnp.pad(g_wt.reshape(N_DEV, W_IN_ROWS, D), ((0, 0), (0, W_IN_PAD - W_IN_ROWS), (0, 0)))
    r0, r1, r2 = W_IN_SPLIT
    du_a, (r_in_a,) = mm(dproj, w_main_t, "nn", bf16, "mm_du", comm=([send_in[:, :r0]], (True,)))
    du_d, (r_in_b,) = mm(dqt, w_q_t, "tn", bf16, "mm_du_q", comm=([send_in[:, r0:r1]], (True,)))
    du_b, (r_in_c,) = mm(dkvt, w_kv_t, "tn", bf16, "mm_du_kv", comm=([send_in[:, r1:r2]], (True,)))
    grad_x, g_pre, (r_in_d,) = prenorm_bwd(xs, w["norm_mix_pre_w"], (du_a, du_b, du_c, du_d), dx1, tsn,
                                           comm=([send_in[:, r2:]], (True,)))
    (recv_pre,) = exchange([g_pre], (False,), "gather_last")

    r_in = jnp.concatenate([r_in_a, r_in_b, r_in_c, r_in_d], axis=1)
    tpad = lambda t: jnp.pad(t.T, ((0, W_IN_PAD - W_IN_ROWS), (0, 0)))
    o_in = [t[:W_IN_ROWS].T for t in adamw(r_in, tpad(w["w_in"]), tpad(m["w_in"]), tpad(v["w_in"]), 368, "adamw_w_in")]
    o_up = [t.T for t in adamw(r_up, w["ffn_w_up"].T, m["ffn_w_up"].T, v["ffn_w_up"].T, 352, "adamw_w_up")]
    big = {"w_in": o_in, "ffn_w_up": o_up,
           "ssd_w_out": adamw(r_so, w["ssd_w_out"], m["ssd_w_out"], v["ssd_w_out"], 256, "adamw_ssd_out"),
           "attn_w_out": adamw(r_ao, w["attn_w_out"], m["attn_w_out"], v["attn_w_out"], 128, "adamw_attn_out"),
           "w_mix_out": adamw(r_mix, w["w_mix_out"], m["w_mix_out"], v["w_mix_out"], 128, "adamw_mix"),
           "ffn_w_down": adamw(r_down, w["ffn_w_down"], m["ffn_w_down"], v["ffn_w_down"], 352, "adamw_down")}
    small_names = [n for n in WEIGHTS if n not in big]
    small, loss_row = adamw_small(recv_row, recv_pre, recv_conv, {n: (w[n], m[n], v[n]) for n in small_names})

    outs = [loss_row[0, 0], grad_x[None]]
    for k in range(4):
        for n in WEIGHTS:
            outs.append((big[n][k] if n in big else small[n][k]).reshape(a[n].shape))
    return tuple(outs)
```
